```python
import jax, jax.numpy as jnp
from jax import lax
import numpy as np

D_MODEL = 2048
BATCH = 8
SEQ = 4096
DEPTH = 1

CHUNK = 64
Q_BLOCK = 128
N_MEM = 256
EPS = 1e-6
MLA_HEADS = 16
Q_LORA_RANK = 512
KV_LORA_RANK = 512
QK_NOPE_DIM = 128
QK_ROPE_DIM = 64
V_HEAD_DIM = 128
ROPE_BASE = 10000.0
MLA_WIDTH = MLA_HEADS * V_HEAD_DIM
MLSTM_HEADS = 8
MLSTM_QK_DIM = 128
MLSTM_V_DIM = 256
MLSTM_CONV = 4
MLSTM_QK_WIDTH = MLSTM_HEADS * MLSTM_QK_DIM
MLSTM_V_WIDTH = MLSTM_HEADS * MLSTM_V_DIM
CROSS_HEADS = 4
CROSS_HEAD_DIM = 128
CROSS_WIDTH = CROSS_HEADS * CROSS_HEAD_DIM
D_FF = 5632
FFN_CONV = 3
IN_SPLITS = (Q_LORA_RANK, KV_LORA_RANK, QK_ROPE_DIM,
             MLSTM_QK_WIDTH, MLSTM_QK_WIDTH, MLSTM_V_WIDTH, MLSTM_HEADS, MLSTM_HEADS, MLSTM_V_WIDTH,
             D_MODEL, D_MODEL)
IN_WIDTH = (Q_LORA_RANK + KV_LORA_RANK + QK_ROPE_DIM + 2 * MLSTM_QK_WIDTH + MLSTM_V_WIDTH
            + 2 * MLSTM_HEADS + MLSTM_V_WIDTH + 2 * D_MODEL)

kernel_name = "hybrid_mla_mlstm_gated_streaming_layer"


def rms_norm(x, g):
    xf = x.astype(jnp.float32)
    y = xf * lax.rsqrt(jnp.mean(xf * xf, axis=-1, keepdims=True) + EPS)
    return (y * g.astype(jnp.float32)).astype(x.dtype)


def causal_dwconv(x, w):
    k_width, s = w.shape[0], x.shape[1]
    xp = jnp.pad(x, ((0, 0), (k_width - 1, 0), (0, 0)))
    return sum(xp[:, j:j + s] * w[j] for j in range(k_width))


def rope_tables(positions):
    inv_freq = ROPE_BASE ** (-jnp.arange(0, QK_ROPE_DIM, 2, dtype=jnp.float32) / QK_ROPE_DIM)
    ang = positions.astype(jnp.float32)[..., None] * inv_freq
    return jnp.cos(ang), jnp.sin(ang)


def apply_rope(x, cos, sin):
    half = x.shape[-1] // 2
    x1, x2 = x[..., :half], x[..., half:]
    return jnp.concatenate([x1 * cos - x2 * sin, x2 * cos + x1 * sin], axis=-1).astype(x.dtype)


def mla_attention(z_qa, z_kv, z_kpe, cos, sin, g_qa, w_qb, g_kva, w_kvb,
                  g_qn_nope, g_qn_pe, g_kn_nope, g_kn_pe):
    b, s, _ = z_qa.shape
    q = (rms_norm(z_qa, g_qa) @ w_qb).reshape(b, s, MLA_HEADS, QK_NOPE_DIM + QK_ROPE_DIM)
    kv = (rms_norm(z_kv, g_kva) @ w_kvb).reshape(b, s, MLA_HEADS, QK_NOPE_DIM + V_HEAD_DIM)
    q_nope = rms_norm(q[..., :QK_NOPE_DIM], g_qn_nope)
    q_pe = apply_rope(rms_norm(q[..., QK_NOPE_DIM:], g_qn_pe), cos[:, :, None], sin[:, :, None])
    k_nope = rms_norm(kv[..., :QK_NOPE_DIM], g_kn_nope)
    v = kv[..., QK_NOPE_DIM:]
    k_pe = apply_rope(rms_norm(z_kpe, g_kn_pe), cos, sin)
    scale = (QK_NOPE_DIM + QK_ROPE_DIM) ** -0.5
    outs = []
    for j in range(s // Q_BLOCK):
        q0, q1 = j * Q_BLOCK, (j + 1) * Q_BLOCK
        k_end = q1
        sc = (jnp.einsum('bqhd,bkhd->bhqk', q_nope[:, q0:q1], k_nope[:, :k_end])
              + jnp.einsum('bqhr,bkr->bhqk', q_pe[:, q0:q1], k_pe[:, :k_end])).astype(jnp.float32) * scale
        q_chunk = (q0 + jnp.arange(Q_BLOCK)) // CHUNK
        k_chunk = jnp.arange(k_end) // CHUNK
        sc = jnp.where(k_chunk[None, :] <= q_chunk[:, None], sc, -jnp.inf)
        p = jax.nn.softmax(sc, axis=-1).astype(v.dtype)
        outs.append(jnp.einsum('bhqk,bkhd->bqhd', p, v[:, :k_end]))
    return jnp.concatenate(outs, axis=1).reshape(b, s, MLA_WIDTH)


def mlstm(zq, zk, zv, zi, zf, zo, conv_qk, b_if, g_hnorm):
    b, s, _ = zq.shape
    h_, dk, dv, l_ = MLSTM_HEADS, MLSTM_QK_DIM, MLSTM_V_DIM, CHUNK
    nc = s // l_
    qk = jax.nn.silu(causal_dwconv(jnp.concatenate([zq, zk], axis=-1), conv_qk))

    def to_chunks(t, d):
        return t.astype(jnp.float32).reshape(b, nc, l_, h_, d).transpose(1, 0, 3, 2, 4)

    q = to_chunks(qk[..., :MLSTM_QK_WIDTH], dk) * (dk ** -0.5)
    k = to_chunks(qk[..., MLSTM_QK_WIDTH:], dk)
    v = to_chunks(zv, dv)
    gates = (jnp.concatenate([zi, zf], axis=-1) + b_if).astype(jnp.float32)

    def gate_chunks(t):
        return t.reshape(b, nc, l_, h_).transpose(1, 0, 3, 2)

    log_i = gate_chunks(gates[..., :h_])
    bcum = jnp.cumsum(gate_chunks(jax.nn.log_sigmoid(gates[..., h_:])), axis=-1)
    causal = jnp.tril(jnp.ones((l_, l_), dtype=bool))

    def step(carry, inp):
        c_mat, n_vec, m = carry
        qc, kc, vc, bc, ic = inp
        logw = jnp.where(causal, bc[..., :, None] - bc[..., None, :] + ic[..., None, :], -jnp.inf)
        inter = bc + m[..., None]
        m_t = jnp.maximum(inter, jnp.max(logw, axis=-1))
        w_intra = jnp.exp(logw - m_t[..., None])
        w_inter = jnp.exp(inter - m_t)
        sc = jnp.einsum('bhtd,bhsd->bhts', qc, kc) * w_intra
        num = w_inter[..., None] * jnp.einsum('bhtd,bhde->bhte', qc, c_mat) + jnp.einsum('bhts,bhse->bhte', sc, vc)
        den = w_inter * jnp.einsum('bhtd,bhd->bht', qc, n_vec) + jnp.sum(sc, axis=-1)
        h = num / jnp.maximum(jnp.abs(den), jnp.exp(-m_t))[..., None]
        b_last = bc[..., -1]
        logu = b_last[..., None] - bc + ic
        m_new = jnp.maximum(b_last + m, jnp.max(logu, axis=-1))
        decay = jnp.exp(b_last + m - m_new)
        u = jnp.exp(logu - m_new[..., None])
        c_mat = decay[..., None, None] * c_mat + jnp.einsum('bhs,bhsd,bhse->bhde', u, kc, vc)
        n_vec = decay[..., None] * n_vec + jnp.einsum('bhs,bhsd->bhd', u, kc)
        return (c_mat, n_vec, m_new), h

    init = (jnp.zeros((b, h_, dk, dv), jnp.float32), jnp.zeros((b, h_, dk), jnp.float32),
            jnp.zeros((b, h_), jnp.float32))
    _, h = lax.scan(step, init, (q, k, v, bcum, log_i))
    h = h.transpose(1, 0, 3, 2, 4).reshape(b, s, h_, dv)
    h = rms_norm(h, g_hnorm).reshape(b, s, MLSTM_V_WIDTH).astype(zq.dtype)
    return h * jax.nn.sigmoid(zo)


def cross_attend(u, mm, wq_c, wk_c, wv_c, g_cq, g_ck, wo_c):
    b, s, _ = u.shape
    n_mem = mm.shape[1]
    q = rms_norm((u @ wq_c).reshape(b, s, CROSS_HEADS, CROSS_HEAD_DIM), g_cq)
    k = rms_norm((mm @ wk_c).reshape(b, n_mem, CROSS_HEADS, CROSS_HEAD_DIM), g_ck)
    v = (mm @ wv_c).reshape(b, n_mem, CROSS_HEADS, CROSS_HEAD_DIM)
    sc = jnp.einsum('bqhd,bkhd->bhqk', q, k).astype(jnp.float32) * (CROSS_HEAD_DIM ** -0.5)
    p = jax.nn.softmax(sc, axis=-1).astype(v.dtype)
    o = jnp.einsum('bhqk,bkhd->bqhd', p, v).reshape(b, s, CROSS_WIDTH)
    return o @ wo_c


def conv_glu_ffn(u, w_up, conv_ffn, b_conv_ffn, w_down):
    h = causal_dwconv(u @ w_up, conv_ffn) + b_conv_ffn
    return (jax.nn.silu(h[..., :D_FF]) * h[..., D_FF:]) @ w_down


def hybrid_layer(x, mem, cos, sin, g_mix, w_in, g_qa, w_qb, g_kva, w_kvb, g_qn_nope, g_qn_pe,
                 g_kn_nope, g_kn_pe, conv_qk, b_if, g_hnorm, p_a, p_b, w_out, g_cross, g_mem,
                 wq_c, wk_c, wv_c, g_cq, g_ck, wo_c, g_ffn, w_up, conv_ffn, b_conv_ffn, w_down):
    split_points = [int(p) for p in np.cumsum(IN_SPLITS)[:-1]]
    z = rms_norm(x, g_mix) @ w_in
    z_qa, z_kv, z_kpe, zq, zk, zv, zi, zf, zo, gate_a, gate_b = jnp.split(z, split_points, axis=-1)
    y_a = mla_attention(z_qa, z_kv, z_kpe, cos, sin, g_qa, w_qb, g_kva, w_kvb,
                        g_qn_nope, g_qn_pe, g_kn_nope, g_kn_pe)
    y_b = mlstm(zq, zk, zv, zi, zf, zo, conv_qk, b_if, g_hnorm)
    merged = jax.nn.sigmoid(gate_a) * (y_a @ p_a) + jax.nn.sigmoid(gate_b) * (y_b @ p_b)
    x = x + merged @ w_out
    x = x + cross_attend(rms_norm(x, g_cross), rms_norm(mem, g_mem), wq_c, wk_c, wv_c, g_cq, g_ck, wo_c)
    x = x + conv_glu_ffn(rms_norm(x, g_ffn), w_up, conv_ffn, b_conv_ffn, w_down)
    return x


def _fwd_setup_inputs(seed: int = 0) -> dict:
    key = jax.random.key(seed)
    ks = iter(jax.random.split(key, 64))
    f32 = jnp.float32

    def w(shape, fan_in):
        return jax.random.normal(next(ks), (DEPTH,) + shape, f32) * (fan_in ** -0.5)

    def gain(shape):
        return 1.0 + 0.05 * jax.random.normal(next(ks), (DEPTH,) + shape, f32)

    x = jax.random.normal(next(ks), (BATCH, SEQ, D_MODEL), f32)
    mem = jax.random.normal(next(ks), (BATCH, N_MEM, D_MODEL), f32)
    offset = jax.random.randint(next(ks), (BATCH, 1), 0, 4096, dtype=jnp.int32)
    positions = (offset + jnp.arange(SEQ, dtype=jnp.int32)[None, :]).astype(jnp.int32)
    b_i = 0.1 * jax.random.normal(next(ks), (DEPTH, MLSTM_HEADS), f32)
    b_f = jnp.linspace(3.0, 6.0, MLSTM_HEADS, dtype=f32)[None, :] + 0.1 * jax.random.normal(next(ks), (DEPTH, MLSTM_HEADS), f32)
    return {
        "x": x,
        "mem": mem,
        "positions": positions,
        "g_mix": gain((D_MODEL,)),
        "w_in": w((D_MODEL, IN_WIDTH), D_MODEL),
        "g_qa": gain((Q_LORA_RANK,)),
        "w_qb": w((Q_LORA_RANK, MLA_HEADS * (QK_NOPE_DIM + QK_ROPE_DIM)), Q_LORA_RANK),
        "g_kva": gain((KV_LORA_RANK,)),
        "w_kvb": w((KV_LORA_RANK, MLA_HEADS * (QK_NOPE_DIM + V_HEAD_DIM)), KV_LORA_RANK),
        "g_qn_nope": gain((QK_NOPE_DIM,)),
        "g_qn_pe": gain((QK_ROPE_DIM,)),
        "g_kn_nope": gain((QK_NOPE_DIM,)),
        "g_kn_pe": gain((QK_ROPE_DIM,)),
        "conv_qk": w((MLSTM_CONV, 2 * MLSTM_QK_WIDTH), MLSTM_CONV),
        "b_if": jnp.concatenate([b_i, b_f], axis=-1),
        "g_hnorm": gain((MLSTM_HEADS, MLSTM_V_DIM)),
        "p_a": w((MLA_WIDTH, D_MODEL), MLA_WIDTH),
        "p_b": w((MLSTM_V_WIDTH, D_MODEL), MLSTM_V_WIDTH),
        "w_out": w((D_MODEL, D_MODEL), D_MODEL),
        "g_cross": gain((D_MODEL,)),
        "g_mem": gain((D_MODEL,)),
        "wq_c": w((D_MODEL, CROSS_WIDTH), D_MODEL),
        "wk_c": w((D_MODEL, CROSS_WIDTH), D_MODEL),
        "wv_c": w((D_MODEL, CROSS_WIDTH), D_MODEL),
        "g_cq": gain((CROSS_HEAD_DIM,)),
        "g_ck": gain((CROSS_HEAD_DIM,)),
        "wo_c": w((CROSS_WIDTH, D_MODEL), CROSS_WIDTH),
        "g_ffn": gain((D_MODEL,)),
        "w_up": w((D_MODEL, 2 * D_FF), D_MODEL),
        "conv_ffn": w((FFN_CONV, 2 * D_FF), FFN_CONV),
        "b_conv_ffn": 0.01 * jax.random.normal(next(ks), (DEPTH, 2 * D_FF), f32),
        "w_down": w((D_FF, D_MODEL), D_FF),
    }


def _fwd_reference(x, mem, positions, g_mix, w_in, g_qa, w_qb, g_kva, w_kvb, g_qn_nope, g_qn_pe,
              g_kn_nope, g_kn_pe, conv_qk, b_if, g_hnorm, p_a, p_b, w_out, g_cross, g_mem,
              wq_c, wk_c, wv_c, g_cq, g_ck, wo_c, g_ffn, w_up, conv_ffn, b_conv_ffn, w_down):
    cos, sin = rope_tables(positions)
    for l in range(DEPTH):
        x = hybrid_layer(x, mem, cos, sin, g_mix[l], w_in[l], g_qa[l], w_qb[l], g_kva[l], w_kvb[l],
                         g_qn_nope[l], g_qn_pe[l], g_kn_nope[l], g_kn_pe[l], conv_qk[l], b_if[l],
                         g_hnorm[l], p_a[l], p_b[l], w_out[l], g_cross[l], g_mem[l], wq_c[l], wk_c[l],
                         wv_c[l], g_cq[l], g_ck[l], wo_c[l], g_ffn[l], w_up[l], conv_ffn[l],
                         b_conv_ffn[l], w_down[l])
    return x


import jax as _jax
import jax.numpy as _jnp

TWIN_FORMAT = 'train_step'
FWD_PARAMS = ['x', 'mem', 'positions', 'g_mix', 'w_in', 'g_qa', 'w_qb', 'g_kva', 'w_kvb', 'g_qn_nope', 'g_qn_pe', 'g_kn_nope', 'g_kn_pe', 'conv_qk', 'b_if', 'g_hnorm', 'p_a', 'p_b', 'w_out', 'g_cross', 'g_mem', 'wq_c', 'wk_c', 'wv_c', 'g_cq', 'g_ck', 'wo_c', 'g_ffn', 'w_up', 'conv_ffn', 'b_conv_ffn', 'w_down']
TWIN_WEIGHTS = ['g_mix', 'w_in', 'g_qa', 'w_qb', 'g_kva', 'w_kvb', 'g_qn_nope', 'g_qn_pe', 'g_kn_nope', 'g_kn_pe', 'conv_qk', 'b_if', 'g_hnorm', 'p_a', 'p_b', 'w_out', 'g_cross', 'g_mem', 'wq_c', 'wk_c', 'wv_c', 'g_cq', 'g_ck', 'wo_c', 'g_ffn', 'w_up', 'conv_ffn', 'b_conv_ffn', 'w_down']
TWIN_DIFF_INPUT = 'x'
TWIN_INPUTS = ['x', 'mem', 'positions', 'g_mix', 'w_in', 'g_qa', 'w_qb', 'g_kva', 'w_kvb', 'g_qn_nope', 'g_qn_pe', 'g_kn_nope', 'g_kn_pe', 'conv_qk', 'b_if', 'g_hnorm', 'p_a', 'p_b', 'w_out', 'g_cross', 'g_mem', 'wq_c', 'wk_c', 'wv_c', 'g_cq', 'g_ck', 'wo_c', 'g_ffn', 'w_up', 'conv_ffn', 'b_conv_ffn', 'w_down', 'loss_target', 'm_g_mix', 'm_w_in', 'm_g_qa', 'm_w_qb', 'm_g_kva', 'm_w_kvb', 'm_g_qn_nope', 'm_g_qn_pe', 'm_g_kn_nope', 'm_g_kn_pe', 'm_conv_qk', 'm_b_if', 'm_g_hnorm', 'm_p_a', 'm_p_b', 'm_w_out', 'm_g_cross', 'm_g_mem', 'm_wq_c', 'm_wk_c', 'm_wv_c', 'm_g_cq', 'm_g_ck', 'm_wo_c', 'm_g_ffn', 'm_w_up', 'm_conv_ffn', 'm_b_conv_ffn', 'm_w_down', 'v_g_mix', 'v_w_in', 'v_g_qa', 'v_w_qb', 'v_g_kva', 'v_w_kvb', 'v_g_qn_nope', 'v_g_qn_pe', 'v_g_kn_nope', 'v_g_kn_pe', 'v_conv_qk', 'v_b_if', 'v_g_hnorm', 'v_p_a', 'v_p_b', 'v_w_out', 'v_g_cross', 'v_g_mem', 'v_wq_c', 'v_wk_c', 'v_wv_c', 'v_g_cq', 'v_g_ck', 'v_wo_c', 'v_g_ffn', 'v_w_up', 'v_conv_ffn', 'v_b_conv_ffn', 'v_w_down']
TWIN_OUTPUTS = ['loss', 'grad_x', 'grad_g_mix', 'grad_w_in', 'grad_g_qa', 'grad_w_qb', 'grad_g_kva', 'grad_w_kvb', 'grad_g_qn_nope', 'grad_g_qn_pe', 'grad_g_kn_nope', 'grad_g_kn_pe', 'grad_conv_qk', 'grad_b_if', 'grad_g_hnorm', 'grad_p_a', 'grad_p_b', 'grad_w_out', 'grad_g_cross', 'grad_g_mem', 'grad_wq_c', 'grad_wk_c', 'grad_wv_c', 'grad_g_cq', 'grad_g_ck', 'grad_wo_c', 'grad_g_ffn', 'grad_w_up', 'grad_conv_ffn', 'grad_b_conv_ffn', 'grad_w_down', 'delta_g_mix', 'delta_w_in', 'delta_g_qa', 'delta_w_qb', 'delta_g_kva', 'delta_w_kvb', 'delta_g_qn_nope', 'delta_g_qn_pe', 'delta_g_kn_nope', 'delta_g_kn_pe', 'delta_conv_qk', 'delta_b_if', 'delta_g_hnorm', 'delta_p_a', 'delta_p_b', 'delta_w_out', 'delta_g_cross', 'delta_g_mem', 'delta_wq_c', 'delta_wk_c', 'delta_wv_c', 'delta_g_cq', 'delta_g_ck', 'delta_wo_c', 'delta_g_ffn', 'delta_w_up', 'delta_conv_ffn', 'delta_b_conv_ffn', 'delta_w_down', 'new_m_g_mix', 'new_m_w_in', 'new_m_g_qa', 'new_m_w_qb', 'new_m_g_kva', 'new_m_w_kvb', 'new_m_g_qn_nope', 'new_m_g_qn_pe', 'new_m_g_kn_nope', 'new_m_g_kn_pe', 'new_m_conv_qk', 'new_m_b_if', 'new_m_g_hnorm', 'new_m_p_a', 'new_m_p_b', 'new_m_w_out', 'new_m_g_cross', 'new_m_g_mem', 'new_m_wq_c', 'new_m_wk_c', 'new_m_wv_c', 'new_m_g_cq', 'new_m_g_ck', 'new_m_wo_c', 'new_m_g_ffn', 'new_m_w_up', 'new_m_conv_ffn', 'new_m_b_conv_ffn', 'new_m_w_down', 'new_v_g_mix', 'new_v_w_in', 'new_v_g_qa', 'new_v_w_qb', 'new_v_g_kva', 'new_v_w_kvb', 'new_v_g_qn_nope', 'new_v_g_qn_pe', 'new_v_g_kn_nope', 'new_v_g_kn_pe', 'new_v_conv_qk', 'new_v_b_if', 'new_v_g_hnorm', 'new_v_p_a', 'new_v_p_b', 'new_v_w_out', 'new_v_g_cross', 'new_v_g_mem', 'new_v_wq_c', 'new_v_wk_c', 'new_v_wv_c', 'new_v_g_cq', 'new_v_g_ck', 'new_v_wo_c', 'new_v_g_ffn', 'new_v_w_up', 'new_v_conv_ffn', 'new_v_b_conv_ffn', 'new_v_w_down']
TWIN_LEAF_KINDS = {'loss': 'loss', 'grad_x': 'grad_x', 'grad_g_mix': 'grad_w', 'grad_w_in': 'grad_w', 'grad_g_qa': 'grad_w', 'grad_w_qb': 'grad_w', 'grad_g_kva': 'grad_w', 'grad_w_kvb': 'grad_w', 'grad_g_qn_nope': 'grad_w', 'grad_g_qn_pe': 'grad_w', 'grad_g_kn_nope': 'grad_w', 'grad_g_kn_pe': 'grad_w', 'grad_conv_qk': 'grad_w', 'grad_b_if': 'grad_w', 'grad_g_hnorm': 'grad_w', 'grad_p_a': 'grad_w', 'grad_p_b': 'grad_w', 'grad_w_out': 'grad_w', 'grad_g_cross': 'grad_w', 'grad_g_mem': 'grad_w', 'grad_wq_c': 'grad_w', 'grad_wk_c': 'grad_w', 'grad_wv_c': 'grad_w', 'grad_g_cq': 'grad_w', 'grad_g_ck': 'grad_w', 'grad_wo_c': 'grad_w', 'grad_g_ffn': 'grad_w', 'grad_w_up': 'grad_w', 'grad_conv_ffn': 'grad_w', 'grad_b_conv_ffn': 'grad_w', 'grad_w_down': 'grad_w', 'delta_g_mix': 'delta_w', 'delta_w_in': 'delta_w', 'delta_g_qa': 'delta_w', 'delta_w_qb': 'delta_w', 'delta_g_kva': 'delta_w', 'delta_w_kvb': 'delta_w', 'delta_g_qn_nope': 'delta_w', 'delta_g_qn_pe': 'delta_w', 'delta_g_kn_nope': 'delta_w', 'delta_g_kn_pe': 'delta_w', 'delta_conv_qk': 'delta_w', 'delta_b_if': 'delta_w', 'delta_g_hnorm': 'delta_w', 'delta_p_a': 'delta_w', 'delta_p_b': 'delta_w', 'delta_w_out': 'delta_w', 'delta_g_cross': 'delta_w', 'delta_g_mem': 'delta_w', 'delta_wq_c': 'delta_w', 'delta_wk_c': 'delta_w', 'delta_wv_c': 'delta_w', 'delta_g_cq': 'delta_w', 'delta_g_ck': 'delta_w', 'delta_wo_c': 'delta_w', 'delta_g_ffn': 'delta_w', 'delta_w_up': 'delta_w', 'delta_conv_ffn': 'delta_w', 'delta_b_conv_ffn': 'delta_w', 'delta_w_down': 'delta_w', 'new_m_g_mix': 'new_m', 'new_m_w_in': 'new_m', 'new_m_g_qa': 'new_m', 'new_m_w_qb': 'new_m', 'new_m_g_kva': 'new_m', 'new_m_w_kvb': 'new_m', 'new_m_g_qn_nope': 'new_m', 'new_m_g_qn_pe': 'new_m', 'new_m_g_kn_nope': 'new_m', 'new_m_g_kn_pe': 'new_m', 'new_m_conv_qk': 'new_m', 'new_m_b_if': 'new_m', 'new_m_g_hnorm': 'new_m', 'new_m_p_a': 'new_m', 'new_m_p_b': 'new_m', 'new_m_w_out': 'new_m', 'new_m_g_cross': 'new_m', 'new_m_g_mem': 'new_m', 'new_m_wq_c': 'new_m', 'new_m_wk_c': 'new_m', 'new_m_wv_c': 'new_m', 'new_m_g_cq': 'new_m', 'new_m_g_ck': 'new_m', 'new_m_wo_c': 'new_m', 'new_m_g_ffn': 'new_m', 'new_m_w_up': 'new_m', 'new_m_conv_ffn': 'new_m', 'new_m_b_conv_ffn': 'new_m', 'new_m_w_down': 'new_m', 'new_v_g_mix': 'new_v', 'new_v_w_in': 'new_v', 'new_v_g_qa': 'new_v', 'new_v_w_qb': 'new_v', 'new_v_g_kva': 'new_v', 'new_v_w_kvb': 'new_v', 'new_v_g_qn_nope': 'new_v', 'new_v_g_qn_pe': 'new_v', 'new_v_g_kn_nope': 'new_v', 'new_v_g_kn_pe': 'new_v', 'new_v_conv_qk': 'new_v', 'new_v_b_if': 'new_v', 'new_v_g_hnorm': 'new_v', 'new_v_p_a': 'new_v', 'new_v_p_b': 'new_v', 'new_v_w_out': 'new_v', 'new_v_g_cross': 'new_v', 'new_v_g_mem': 'new_v', 'new_v_wq_c': 'new_v', 'new_v_wk_c': 'new_v', 'new_v_wv_c': 'new_v', 'new_v_g_cq': 'new_v', 'new_v_g_ck': 'new_v', 'new_v_wo_c': 'new_v', 'new_v_g_ffn': 'new_v', 'new_v_w_up': 'new_v', 'new_v_conv_ffn': 'new_v', 'new_v_b_conv_ffn': 'new_v', 'new_v_w_down': 'new_v'}


def _forward(args):
    return _fwd_reference(*[args[k] for k in FWD_PARAMS])


def _output_shape():
    def fwd():
        inp = _fwd_setup_inputs(0)
        return _fwd_reference(*[inp[k] for k in FWD_PARAMS])
    out = _jax.eval_shape(fwd)
    return out.shape, out.dtype

N_MICROBATCH = 1
ADAM_LR = 0.001
ADAM_B1 = 0.9
ADAM_B2 = 0.999
ADAM_EPS = 1e-08
ADAM_WD = 0.01
ADAM_STEP = 10
PER_EXAMPLE_BATCH_AXIS = {'x': 0, 'mem': 0, 'positions': 0, 'loss_target': 0}
SHARED_INPUTS = []
_WEIGHT_DTYPES = {'g_mix': _jnp.float32, 'w_in': _jnp.float32, 'g_qa': _jnp.float32, 'w_qb': _jnp.float32, 'g_kva': _jnp.float32, 'w_kvb': _jnp.float32, 'g_qn_nope': _jnp.float32, 'g_qn_pe': _jnp.float32, 'g_kn_nope': _jnp.float32, 'g_kn_pe': _jnp.float32, 'conv_qk': _jnp.float32, 'b_if': _jnp.float32, 'g_hnorm': _jnp.float32, 'p_a': _jnp.float32, 'p_b': _jnp.float32, 'w_out': _jnp.float32, 'g_cross': _jnp.float32, 'g_mem': _jnp.float32, 'wq_c': _jnp.float32, 'wk_c': _jnp.float32, 'wv_c': _jnp.float32, 'g_cq': _jnp.float32, 'g_ck': _jnp.float32, 'wo_c': _jnp.float32, 'g_ffn': _jnp.float32, 'w_up': _jnp.float32, 'conv_ffn': _jnp.float32, 'b_conv_ffn': _jnp.float32, 'w_down': _jnp.float32}
MOMENT_SCALE = {'g_mix': 4.174574e-01, 'w_in': 1.077251e-01, 'g_qa': 3.661941e-02, 'w_qb': 1.491650e-02, 'g_kva': 1.274001e-01, 'w_kvb': 2.329702e-02, 'g_qn_nope': 2.737348e-01, 'g_qn_pe': 2.358829e-01, 'g_kn_nope': 2.720857e-01, 'g_kn_pe': 2.358105e-01, 'conv_qk': 7.503370e-02, 'b_if': 7.359579e-01, 'g_hnorm': 1.357462e+00, 'p_a': 2.891389e-02, 'p_b': 1.926769e-01, 'w_out': 1.944041e-01, 'g_cross': 2.839495e-02, 'g_mem': 2.035382e-01, 'wq_c': 5.327224e-02, 'wk_c': 5.367335e-02, 'wv_c': 2.376675e-01, 'g_cq': 2.463095e+00, 'g_ck': 2.481013e+00, 'wo_c': 1.170688e-01, 'g_ffn': 1.270760e+01, 'w_up': 1.319102e-01, 'conv_ffn': 1.799692e+00, 'b_conv_ffn': 1.583169e+00, 'w_down': 1.395972e-01}


def _to_microbatches(a, axis):
    t = _jnp.moveaxis(a, axis, 0)
    t = t.reshape((N_MICROBATCH, t.shape[0] // N_MICROBATCH) + t.shape[1:])
    return _jnp.moveaxis(t, 1, axis + 1)


def setup_inputs(seed: int = 0) -> dict:
    inp = _fwd_setup_inputs(seed)
    key = _jax.random.fold_in(_jax.random.key(seed), 7919)
    shape, _ = _output_shape()
    out = dict(inp)
    out["loss_target"] = _jax.random.normal(_jax.random.fold_in(key, 0), shape, _jnp.float32)
    for i, name in enumerate(TWIN_WEIGHTS):
        w = inp[name].astype(_jnp.float32)
        if MOMENT_SCALE is None:
            s = _jnp.sqrt(_jnp.mean(_jnp.square(w)) + 1e-30)
        else:
            s = MOMENT_SCALE[name]
        km, kv = _jax.random.split(_jax.random.fold_in(key, i + 1))
        out[name] = w
        out["m_" + name] = s * _jax.random.normal(km, w.shape, _jnp.float32)
        out["v_" + name] = (s * s) * _jax.random.uniform(kv, w.shape, _jnp.float32, 0.5, 1.5)
    if N_MICROBATCH > 1:
        for name, axis in PER_EXAMPLE_BATCH_AXIS.items():
            out[name] = _to_microbatches(out[name], axis)
    return {'x': out['x'], 'mem': out['mem'], 'positions': out['positions'], 'g_mix': out['g_mix'], 'w_in': out['w_in'], 'g_qa': out['g_qa'], 'w_qb': out['w_qb'], 'g_kva': out['g_kva'], 'w_kvb': out['w_kvb'], 'g_qn_nope': out['g_qn_nope'], 'g_qn_pe': out['g_qn_pe'], 'g_kn_nope': out['g_kn_nope'], 'g_kn_pe': out['g_kn_pe'], 'conv_qk': out['conv_qk'], 'b_if': out['b_if'], 'g_hnorm': out['g_hnorm'], 'p_a': out['p_a'], 'p_b': out['p_b'], 'w_out': out['w_out'], 'g_cross': out['g_cross'], 'g_mem': out['g_mem'], 'wq_c': out['wq_c'], 'wk_c': out['wk_c'], 'wv_c': out['wv_c'], 'g_cq': out['g_cq'], 'g_ck': out['g_ck'], 'wo_c': out['wo_c'], 'g_ffn': out['g_ffn'], 'w_up': out['w_up'], 'conv_ffn': out['conv_ffn'], 'b_conv_ffn': out['b_conv_ffn'], 'w_down': out['w_down'], 'loss_target': out['loss_target'], 'm_g_mix': out['m_g_mix'], 'm_w_in': out['m_w_in'], 'm_g_qa': out['m_g_qa'], 'm_w_qb': out['m_w_qb'], 'm_g_kva': out['m_g_kva'], 'm_w_kvb': out['m_w_kvb'], 'm_g_qn_nope': out['m_g_qn_nope'], 'm_g_qn_pe': out['m_g_qn_pe'], 'm_g_kn_nope': out['m_g_kn_nope'], 'm_g_kn_pe': out['m_g_kn_pe'], 'm_conv_qk': out['m_conv_qk'], 'm_b_if': out['m_b_if'], 'm_g_hnorm': out['m_g_hnorm'], 'm_p_a': out['m_p_a'], 'm_p_b': out['m_p_b'], 'm_w_out': out['m_w_out'], 'm_g_cross': out['m_g_cross'], 'm_g_mem': out['m_g_mem'], 'm_wq_c': out['m_wq_c'], 'm_wk_c': out['m_wk_c'], 'm_wv_c': out['m_wv_c'], 'm_g_cq': out['m_g_cq'], 'm_g_ck': out['m_g_ck'], 'm_wo_c': out['m_wo_c'], 'm_g_ffn': out['m_g_ffn'], 'm_w_up': out['m_w_up'], 'm_conv_ffn': out['m_conv_ffn'], 'm_b_conv_ffn': out['m_b_conv_ffn'], 'm_w_down': out['m_w_down'], 'v_g_mix': out['v_g_mix'], 'v_w_in': out['v_w_in'], 'v_g_qa': out['v_g_qa'], 'v_w_qb': out['v_w_qb'], 'v_g_kva': out['v_g_kva'], 'v_w_kvb': out['v_w_kvb'], 'v_g_qn_nope': out['v_g_qn_nope'], 'v_g_qn_pe': out['v_g_qn_pe'], 'v_g_kn_nope': out['v_g_kn_nope'], 'v_g_kn_pe': out['v_g_kn_pe'], 'v_conv_qk': out['v_conv_qk'], 'v_b_if': out['v_b_if'], 'v_g_hnorm': out['v_g_hnorm'], 'v_p_a': out['v_p_a'], 'v_p_b': out['v_p_b'], 'v_w_out': out['v_w_out'], 'v_g_cross': out['v_g_cross'], 'v_g_mem': out['v_g_mem'], 'v_wq_c': out['v_wq_c'], 'v_wk_c': out['v_wk_c'], 'v_wv_c': out['v_wv_c'], 'v_g_cq': out['v_g_cq'], 'v_g_ck': out['v_g_ck'], 'v_wo_c': out['v_wo_c'], 'v_g_ffn': out['v_g_ffn'], 'v_w_up': out['v_w_up'], 'v_conv_ffn': out['v_conv_ffn'], 'v_b_conv_ffn': out['v_b_conv_ffn'], 'v_w_down': out['v_w_down']}


def _loss(weights, diff, rest, loss_target):
    with _jax.named_scope("forward"):
        args = {**rest, TWIN_DIFF_INPUT: diff, **{k: w.astype(_WEIGHT_DTYPES[k]) for k, w in weights.items()}}
        y = _forward(args)
    with _jax.named_scope("loss_head"):
        err = _jnp.square(y.astype(_jnp.float32) - loss_target)
        return 0.5 * _jnp.sum(_jnp.mean(err, axis=-1)) if err.ndim else 0.5 * err


def _adamw(w, g, m, v):
    m = ADAM_B1 * m + (1.0 - ADAM_B1) * g
    v = ADAM_B2 * v + (1.0 - ADAM_B2) * _jnp.square(g)
    m_hat = m / (1.0 - ADAM_B1 ** ADAM_STEP)
    v_hat = v / (1.0 - ADAM_B2 ** ADAM_STEP)
    delta = -ADAM_LR * (m_hat / (_jnp.sqrt(v_hat) + ADAM_EPS) + ADAM_WD * w)
    return delta, m, v


def reference(x, mem, positions, g_mix, w_in, g_qa, w_qb, g_kva, w_kvb, g_qn_nope, g_qn_pe, g_kn_nope, g_kn_pe, conv_qk, b_if, g_hnorm, p_a, p_b, w_out, g_cross, g_mem, wq_c, wk_c, wv_c, g_cq, g_ck, wo_c, g_ffn, w_up, conv_ffn, b_conv_ffn, w_down, loss_target, m_g_mix, m_w_in, m_g_qa, m_w_qb, m_g_kva, m_w_kvb, m_g_qn_nope, m_g_qn_pe, m_g_kn_nope, m_g_kn_pe, m_conv_qk, m_b_if, m_g_hnorm, m_p_a, m_p_b, m_w_out, m_g_cross, m_g_mem, m_wq_c, m_wk_c, m_wv_c, m_g_cq, m_g_ck, m_wo_c, m_g_ffn, m_w_up, m_conv_ffn, m_b_conv_ffn, m_w_down, v_g_mix, v_w_in, v_g_qa, v_w_qb, v_g_kva, v_w_kvb, v_g_qn_nope, v_g_qn_pe, v_g_kn_nope, v_g_kn_pe, v_conv_qk, v_b_if, v_g_hnorm, v_p_a, v_p_b, v_w_out, v_g_cross, v_g_mem, v_wq_c, v_wk_c, v_wv_c, v_g_cq, v_g_ck, v_wo_c, v_g_ffn, v_w_up, v_conv_ffn, v_b_conv_ffn, v_w_down):
    given = dict(x=x, mem=mem, positions=positions, g_mix=g_mix, w_in=w_in, g_qa=g_qa, w_qb=w_qb, g_kva=g_kva, w_kvb=w_kvb, g_qn_nope=g_qn_nope, g_qn_pe=g_qn_pe, g_kn_nope=g_kn_nope, g_kn_pe=g_kn_pe, conv_qk=conv_qk, b_if=b_if, g_hnorm=g_hnorm, p_a=p_a, p_b=p_b, w_out=w_out, g_cross=g_cross, g_mem=g_mem, wq_c=wq_c, wk_c=wk_c, wv_c=wv_c, g_cq=g_cq, g_ck=g_ck, wo_c=wo_c, g_ffn=g_ffn, w_up=w_up, conv_ffn=conv_ffn, b_conv_ffn=b_conv_ffn, w_down=w_down, loss_target=loss_target, m_g_mix=m_g_mix, m_w_in=m_w_in, m_g_qa=m_g_qa, m_w_qb=m_w_qb, m_g_kva=m_g_kva, m_w_kvb=m_w_kvb, m_g_qn_nope=m_g_qn_nope, m_g_qn_pe=m_g_qn_pe, m_g_kn_nope=m_g_kn_nope, m_g_kn_pe=m_g_kn_pe, m_conv_qk=m_conv_qk, m_b_if=m_b_if, m_g_hnorm=m_g_hnorm, m_p_a=m_p_a, m_p_b=m_p_b, m_w_out=m_w_out, m_g_cross=m_g_cross, m_g_mem=m_g_mem, m_wq_c=m_wq_c, m_wk_c=m_wk_c, m_wv_c=m_wv_c, m_g_cq=m_g_cq, m_g_ck=m_g_ck, m_wo_c=m_wo_c, m_g_ffn=m_g_ffn, m_w_up=m_w_up, m_conv_ffn=m_conv_ffn, m_b_conv_ffn=m_b_conv_ffn, m_w_down=m_w_down, v_g_mix=v_g_mix, v_w_in=v_w_in, v_g_qa=v_g_qa, v_w_qb=v_w_qb, v_g_kva=v_g_kva, v_w_kvb=v_w_kvb, v_g_qn_nope=v_g_qn_nope, v_g_qn_pe=v_g_qn_pe, v_g_kn_nope=v_g_kn_nope, v_g_kn_pe=v_g_kn_pe, v_conv_qk=v_conv_qk, v_b_if=v_b_if, v_g_hnorm=v_g_hnorm, v_p_a=v_p_a, v_p_b=v_p_b, v_w_out=v_w_out, v_g_cross=v_g_cross, v_g_mem=v_g_mem, v_wq_c=v_wq_c, v_wk_c=v_wk_c, v_wv_c=v_wv_c, v_g_cq=v_g_cq, v_g_ck=v_g_ck, v_wo_c=v_wo_c, v_g_ffn=v_g_ffn, v_w_up=v_w_up, v_conv_ffn=v_conv_ffn, v_b_conv_ffn=v_b_conv_ffn, v_w_down=v_w_down)
    weights = {n: given[n] for n in TWIN_WEIGHTS}
    shared = {n: given[n] for n in SHARED_INPUTS}
    per_example = {n: given[n] for n in ['x', 'mem', 'positions']}
    grad_fn = _jax.value_and_grad(_loss, argnums=(0, 1))

    def one_microbatch(ex, loss_target):
        ex = dict(ex)
        diff = ex.pop(TWIN_DIFF_INPUT)
        return grad_fn(weights, diff, {**shared, **ex}, loss_target)

    if N_MICROBATCH == 1:
        loss, (grad_w, grad_x) = one_microbatch(per_example, given["loss_target"])
    else:
        def body(carry, xs):
            loss_sum, grad_sum = carry
            l_k, (gw_k, gx_k) = one_microbatch(xs[0], xs[1])
            with _jax.named_scope("update"):
                return (loss_sum + l_k, _jax.tree.map(_jnp.add, grad_sum, gw_k)), gx_k

        init = (_jnp.zeros((), _jnp.float32), _jax.tree.map(_jnp.zeros_like, weights))
        (loss, grad_w), grad_x = _jax.lax.scan(body, init, (per_example, given["loss_target"]))
    with _jax.named_scope("update"):
        delta_w, new_m, new_v = {}, {}, {}
        for n in TWIN_WEIGHTS:
            delta_w[n], new_m[n], new_v[n] = _adamw(weights[n], grad_w[n], given["m_" + n], given["v_" + n])
    return (loss, grad_x, *[grad_w[n] for n in TWIN_WEIGHTS], *[delta_w[n] for n in TWIN_WEIGHTS],
            *[new_m[n] for n in TWIN_WEIGHTS], *[new_v[n] for n in TWIN_WEIGHTS])
```

```python
import functools

import jax
import jax.numpy as jnp
from jax import lax
from jax.experimental import pallas as pl
from jax.experimental.pallas import tpu as pltpu

f32 = jnp.float32
bf16 = jnp.bfloat16

N_DEV = 8
EPS = 1e-6
CHUNK = 64
CHUNK_SHIFT = 6
assert 1 << CHUNK_SHIFT == CHUNK
MLA_HEADS = 16
Q_LORA = 512
KV_LORA = 512
NOPE = 128
ROPE = 64
V_HEAD = 128
ROPE_BASE = 10000.0
HEAD_PAD = 256
ML_HEADS = 8
ML_DK = 128
ML_DV = 256
ML_CONV = 4
ML_QK = ML_HEADS * ML_DK
ML_V = ML_HEADS * ML_DV
CR_HEADS = 4
CR_HD = 128
FFN_CONV = 3
ADAM_LR = 0.001
ADAM_B1 = 0.9
ADAM_B2 = 0.999
ADAM_EPS = 1e-08
ADAM_WD = 0.01
ADAM_STEP = 10
O_QA, O_KV, O_Q, O_K = 0, Q_LORA, Q_LORA + KV_LORA, Q_LORA + KV_LORA + ML_QK
O_V = O_K + ML_QK
O_O = O_V + ML_V
O_GA = O_O + ML_V
TAIL = 128
T_I, T_F = ROPE, ROPE + ML_HEADS
VMEM_LIMIT_V7X = 48 * 1024 * 1024
MESH = pl.DeviceIdType.MESH


def _call(body, name, grid, in_specs, out_specs, out_shape, scratch=()):
    return pl.pallas_call(
        body, name=name, grid=grid, in_specs=in_specs, out_specs=out_specs, out_shape=out_shape,
        scratch_shapes=list(scratch), compiler_params=pltpu.CompilerParams(vmem_limit_bytes=VMEM_LIMIT_V7X))


def _tile(n, cands):
    for c in cands:
        if n % c == 0:
            return c
    return n


def _sds(shape, dtype):
    return jax.ShapeDtypeStruct(tuple(shape), dtype)


def _bdot(a, b, ca, cb):
    return lax.dot_general(a.astype(bf16), b.astype(bf16), (((ca,), (cb,)), ((), ())), preferred_element_type=f32)


_BIG = (1024, 512, 256, 128)


def _col_tile(nb):
    return nb if nb <= 1536 else _tile(nb, _BIG)


def mm_nn(a, w3, out_dtype, name):
    m, k = a.shape
    nblk, k2, nb = w3.shape
    assert k == k2
    tm, tk, tn = _tile(m, _BIG), _tile(k, (512, 256, 128)), _col_tile(nb)
    per, nk = nb // tn, k // tk

    def body(a_ref, w_ref, o_ref, acc):
        kk = pl.program_id(2)

        @pl.when(kk == 0)
        def _():
            acc[...] = jnp.zeros_like(acc)

        acc[...] += _bdot(a_ref[...], w_ref[0], 1, 0)

        @pl.when(kk == nk - 1)
        def _():
            o_ref[...] = acc[...].astype(o_ref.dtype)

    return _call(body, name, (m // tm, nblk * per, nk),
                 [pl.BlockSpec((tm, tk), lambda i, j, kk: (i, kk)),
                  pl.BlockSpec((1, tk, tn), lambda i, j, kk: (j // per, kk, j % per))],
                 pl.BlockSpec((tm, tn), lambda i, j, kk: (i, j)), _sds((m, nblk * nb), out_dtype),
                 [pltpu.VMEM((tm, tn), f32)])(a, w3)


def mm_nt(a, w3, out_dtype, name):
    m, n = a.shape
    nblk, k, nb = w3.shape
    assert n == nblk * nb
    tm, tn = _tile(m, _BIG), _tile(k, _BIG)
    tc = nb if nb <= 1536 else _tile(nb, (512, 256, 128))
    per = nb // tc
    nk = nblk * per

    def body(a_ref, w_ref, o_ref, acc):
        kk = pl.program_id(2)

        @pl.when(kk == 0)
        def _():
            acc[...] = jnp.zeros_like(acc)

        acc[...] += _bdot(a_ref[...], w_ref[0], 1, 1)

        @pl.when(kk == nk - 1)
        def _():
            o_ref[...] = acc[...].astype(o_ref.dtype)

    return _call(body, name, (m // tm, k // tn, nk),
                 [pl.BlockSpec((tm, tc), lambda i, j, kk: (i, kk)),
                  pl.BlockSpec((1, tn, tc), lambda i, j, kk: (kk // per, j, kk % per))],
                 pl.BlockSpec((tm, tn), lambda i, j, kk: (i, j)), _sds((m, k), out_dtype),
                 [pltpu.VMEM((tm, tn), f32)])(a, w3)


def mm_tn(a, b, nblk, name):
    r, m = a.shape
    r2, n = b.shape
    assert r == r2 and n % nblk == 0
    nb = n // nblk
    tm, tk, tn = _tile(m, _BIG), _tile(r, (512, 256, 128)), _col_tile(nb)
    per, nk = nb // tn, r // tk

    def body(a_ref, b_ref, o_ref, acc):
        kk = pl.program_id(2)

        @pl.when(kk == 0)
        def _():
            acc[...] = jnp.zeros_like(acc)

        acc[...] += _bdot(a_ref[...], b_ref[...], 0, 0)

        @pl.when(kk == nk - 1)
        def _():
            o_ref[0] = acc[...]

    return _call(body, name, (m // tm, nblk * per, nk),
                 [pl.BlockSpec((tk, tm), lambda i, j, kk: (kk, i)),
                  pl.BlockSpec((tk, tn), lambda i, j, kk: (kk, j))],
                 pl.BlockSpec((1, tm, tn), lambda i, j, kk: (j // per, i, j % per)), _sds((nblk, m, nb), f32),
                 [pltpu.VMEM((tm, tn), f32)])(a, b)


def _rms(x, g):
    return x * lax.rsqrt(jnp.mean(x * x, axis=-1, keepdims=True) + EPS) * g


def _rms_pad(x, g, width):
    return x * lax.rsqrt(jnp.sum(x * x, axis=-1, keepdims=True) / width + EPS) * g


def _first(*ids):
    ok = ids[0] == 0
    for i in ids[1:]:
        ok = jnp.logical_and(ok, i == 0)
    return ok


def _acc_row(ref, val, first):
    @pl.when(first)
    def _():
        ref[...] = jnp.zeros_like(ref)

    ref[0:1, :] += val


def rms_fwd(x, g, name):
    r, w = x.shape
    tm = _tile(r, (256, 128, 64, 32, 16, 8))

    def body(x_ref, g_ref, o_ref):
        o_ref[...] = _rms(x_ref[...], g_ref[...]).astype(bf16)

    return _call(body, name, (r // tm,), [pl.BlockSpec((tm, w), lambda i: (i, 0)), pl.BlockSpec((1, w), lambda i: (0, 0))],
                 pl.BlockSpec((tm, w), lambda i: (i, 0)), _sds((r, w), bf16))(x, g)


def resid_rms(xa, xb, g, name):
    r, w = xa.shape
    tm = _tile(r, (256, 128, 64, 32, 16, 8))

    def body(a_ref, b_ref, g_ref, s_ref, u_ref):
        xs = a_ref[...] + b_ref[...]
        s_ref[...] = xs
        u_ref[...] = _rms(xs, g_ref[...]).astype(bf16)

    row = pl.BlockSpec((tm, w), lambda i: (i, 0))
    return _call(body, name, (r // tm,), [row, row, pl.BlockSpec((1, w), lambda i: (0, 0))], [row, row],
                 [_sds((r, w), f32), _sds((r, w), bf16)])(xa, xb, g)


def rms_bwd(x, g, dys, dres, name, want_dx=True):
    r, w = x.shape
    tm = _tile(r, (256, 128, 64, 32, 16, 8))
    nd = len(dys)

    def body(*refs):
        x_ref, g_ref = refs[0], refs[1]
        dy = refs[2][...]
        for j in range(1, nd):
            dy = dy + refs[2 + j][...]
        pos = 2 + nd
        _, vjp = jax.vjp(_rms, x_ref[...], g_ref[...])
        dx, dg = vjp(dy)
        if dres is not None:
            dx = dx + refs[pos][...]
            pos += 1
        if want_dx:
            refs[pos][...] = dx
            pos += 1
        _acc_row(refs[pos], dg, pl.program_id(0) == 0)

    row = pl.BlockSpec((tm, w), lambda i: (i, 0))
    ins = [x, g] + list(dys) + ([dres] if dres is not None else [])
    in_specs = [row, pl.BlockSpec((1, w), lambda i: (0, 0))] + [row] * (nd + (dres is not None))
    out_specs = ([row] if want_dx else []) + [pl.BlockSpec((8, w), lambda i: (0, 0))]
    out_shape = ([_sds((r, w), f32)] if want_dx else []) + [_sds((8, w), f32)]
    return _call(body, name, (r // tm,), in_specs, out_specs, out_shape)(*ins)


def lat_norm(z_main, g_qa, g_kva):
    t = z_main.shape[0]
    tm = _tile(t, (512, 256, 128, 64))

    def body(z_ref, gq_ref, gk_ref, q_ref, k_ref):
        q_ref[...] = _rms(z_ref[:, :Q_LORA], gq_ref[...]).astype(bf16)
        k_ref[...] = _rms(z_ref[:, Q_LORA:], gk_ref[...]).astype(bf16)

    return _call(body, "lat_norm", (t // tm,),
                 [pl.BlockSpec((tm, Q_LORA + KV_LORA), lambda i: (i, 0)), pl.BlockSpec((1, Q_LORA), lambda i: (0, 0)),
                  pl.BlockSpec((1, KV_LORA), lambda i: (0, 0))],
                 [pl.BlockSpec((tm, Q_LORA), lambda i: (i, 0)), pl.BlockSpec((tm, KV_LORA), lambda i: (i, 0))],
                 [_sds((t, Q_LORA), bf16), _sds((t, KV_LORA), bf16)])(z_main, g_qa, g_kva)


def lat_norm_bwd(z_main, g_qa, g_kva, dqa, dkv):
    t = z_main.shape[0]
    tm = _tile(t, (512, 256, 128, 64))

    def body(z_ref, gq_ref, gk_ref, dq_ref, dk_ref, dz_ref, dgq_ref, dgk_ref):
        first = pl.program_id(0) == 0
        _, vq = jax.vjp(_rms, z_ref[:, :Q_LORA], gq_ref[...])
        dx, dg = vq(dq_ref[...])
        dz_ref[:, :Q_LORA] = dx.astype(bf16)
        _acc_row(dgq_ref, dg, first)
        _, vk = jax.vjp(_rms, z_ref[:, Q_LORA:], gk_ref[...])
        dx, dg = vk(dk_ref[...])
        dz_ref[:, Q_LORA:] = dx.astype(bf16)
        _acc_row(dgk_ref, dg, first)

    return _call(body, "lat_norm_bwd", (t // tm,),
                 [pl.BlockSpec((tm, Q_LORA + KV_LORA), lambda i: (i, 0)), pl.BlockSpec((1, Q_LORA), lambda i: (0, 0)),
                  pl.BlockSpec((1, KV_LORA), lambda i: (0, 0)), pl.BlockSpec((tm, Q_LORA), lambda i: (i, 0)),
                  pl.BlockSpec((tm, KV_LORA), lambda i: (i, 0))],
                 [pl.BlockSpec((tm, Q_LORA + KV_LORA), lambda i: (i, 0)), pl.BlockSpec((8, Q_LORA), lambda i: (0, 0)),
                  pl.BlockSpec((8, KV_LORA), lambda i: (0, 0))],
                 [_sds((t, Q_LORA + KV_LORA), bf16), _sds((8, Q_LORA), f32), _sds((8, KV_LORA), f32)])(z_main, g_qa, g_kva, dqa, dkv)


def rope_tables(pos_col, inv_freq):
    t = pos_col.shape[0]
    tm = _tile(t, (512, 256, 128, 64))

    def body(p_ref, f_ref, c_ref, s_ref):
        ang = p_ref[...].astype(f32) * f_ref[...]
        lane = lax.broadcasted_iota(jnp.int32, ang.shape, 1)
        c_ref[...] = jnp.where(lane < ROPE, jnp.cos(ang), 0.0)
        sn = jnp.sin(ang)
        s_ref[...] = jnp.where(lane < ROPE // 2, -sn, jnp.where(lane < ROPE, sn, 0.0))

    return _call(body, "rope_tables", (t // tm,),
                 [pl.BlockSpec((tm, 1), lambda i: (i, 0)), pl.BlockSpec((1, TAIL), lambda i: (0, 0))],
                 [pl.BlockSpec((tm, TAIL), lambda i: (i, 0))] * 2, [_sds((t, TAIL), f32)] * 2)(pos_col, inv_freq)


def _swap_halves(n):
    lane = lax.broadcasted_iota(jnp.int32, n.shape, 1)
    return jnp.where(lane < ROPE // 2, pltpu.roll(n, TAIL - ROPE // 2, 1), pltpu.roll(n, ROPE // 2, 1))


def _rope(n, c, s):
    return n * c + _swap_halves(n) * s


def _rope_t(d, c, s):
    return d * c + _swap_halves(d * s)


def _prep_specs(tm):
    head = pl.BlockSpec((tm, HEAD_PAD), lambda i, h: (i, h))
    row = pl.BlockSpec((tm, TAIL), lambda i, h: (i, 0))
    gain = pl.BlockSpec((1, TAIL), lambda i, h: (0, 0))
    return head, row, gain


def _pe_in(zt):
    lane = lax.broadcasted_iota(jnp.int32, zt.shape, 1)
    return jnp.where(lane < ROPE, zt, 0.0)


def mla_prep(q_raw, kv_raw, z_tail, cos, sin, gqn, gqp, gkn, gkp):
    t = q_raw.shape[0]
    tm = _tile(t, (512, 256, 128, 64))

    def body(q_ref, kv_ref, zt_ref, c_ref, s_ref, gqn_ref, gqp_ref, gkn_ref, gkp_ref, qh_ref, kh_ref, vh_ref):
        c, s = c_ref[...], s_ref[...]
        qh_ref[:, :NOPE] = _rms(q_ref[:, :NOPE], gqn_ref[...]).astype(bf16)
        qh_ref[:, NOPE:] = _rope(_rms_pad(q_ref[:, NOPE:], gqp_ref[...], ROPE), c, s).astype(bf16)
        kh_ref[:, :NOPE] = _rms(kv_ref[:, :NOPE], gkn_ref[...]).astype(bf16)
        kh_ref[:, NOPE:] = _rope(_rms_pad(_pe_in(zt_ref[...]), gkp_ref[...], ROPE), c, s).astype(bf16)
        vh_ref[...] = kv_ref[:, NOPE:].astype(bf16)

    head, row, gain = _prep_specs(tm)
    return _call(body, "mla_prep", (t // tm, MLA_HEADS), [head, head, row, row, row, gain, gain, gain, gain],
                 [head, head, pl.BlockSpec((tm, V_HEAD), lambda i, h: (i, h))],
                 [_sds((t, MLA_HEADS * HEAD_PAD), bf16), _sds((t, MLA_HEADS * HEAD_PAD), bf16), _sds((t, MLA_HEADS * V_HEAD), bf16)],
                 )(q_raw, kv_raw, z_tail, cos, sin, gqn, gqp, gkn, gkp)


def mla_prep_bwd(q_raw, kv_raw, z_tail, cos, sin, gqn, gqp, gkn, gkp, dqh, dkh, dvh):
    t = q_raw.shape[0]
    tm = _tile(t, (512, 256, 128, 64))
    pad_norm = functools.partial(_rms_pad, width=ROPE)

    def body(q_ref, kv_ref, zt_ref, c_ref, s_ref, gqn_ref, gqp_ref, gkn_ref, gkp_ref, dqh_ref, dkh_ref, dvh_ref,
             dq_ref, dkv_ref, dzt_ref, dgqn_ref, dgqp_ref, dgkn_ref, dgkp_ref):
        i, h = pl.program_id(0), pl.program_id(1)
        first = _first(i, h)
        c, s = c_ref[...], s_ref[...]
        _, v1 = jax.vjp(_rms, q_ref[:, :NOPE], gqn_ref[...])
        dx, dg = v1(dqh_ref[:, :NOPE])
        dq_ref[:, :NOPE] = dx.astype(bf16)
        _acc_row(dgqn_ref, dg, first)
        _, v2 = jax.vjp(pad_norm, q_ref[:, NOPE:], gqp_ref[...])
        dx, dg = v2(_rope_t(dqh_ref[:, NOPE:], c, s))
        dq_ref[:, NOPE:] = dx.astype(bf16)
        _acc_row(dgqp_ref, dg, first)
        _, v3 = jax.vjp(_rms, kv_ref[:, :NOPE], gkn_ref[...])
        dx, dg = v3(dkh_ref[:, :NOPE])
        dkv_ref[:, :NOPE] = dx.astype(bf16)
        _acc_row(dgkn_ref, dg, first)
        dkv_ref[:, NOPE:] = dvh_ref[...].astype(bf16)
        _, v4 = jax.vjp(pad_norm, _pe_in(zt_ref[...]), gkp_ref[...])
        dx, dg = v4(_rope_t(dkh_ref[:, NOPE:], c, s))
        _acc_row(dgkp_ref, dg, first)

        @pl.when(h == 0)
        def _():
            dzt_ref[...] = jnp.zeros_like(dzt_ref)

        dzt_ref[...] += dx

    head, row, gain = _prep_specs(tm)
    acc = pl.BlockSpec((8, TAIL), lambda i, h: (0, 0))
    vspec = pl.BlockSpec((tm, V_HEAD), lambda i, h: (i, h))
    return _call(body, "mla_prep_bwd", (t // tm, MLA_HEADS),
                 [head, head, row, row, row, gain, gain, gain, gain, head, head, vspec],
                 [head, head, row, acc, acc, acc, acc],
                 [_sds((t, MLA_HEADS * HEAD_PAD), bf16), _sds((t, MLA_HEADS * HEAD_PAD), bf16), _sds((t, TAIL), f32)]
                 + [_sds((8, TAIL), f32)] * 4)(q_raw, kv_raw, z_tail, cos, sin, gqn, gqp, gkn, gkp, dqh, dkh, dvh)


ATT_BLOCK = 256
NEG = -1e30


def _att_scores(q, k, qi, ki):
    scale = (NOPE + ROPE) ** -0.5
    s = _bdot(q, k, 1, 1) * scale
    row = qi * ATT_BLOCK + lax.broadcasted_iota(jnp.int32, s.shape, 0)
    col = ki * ATT_BLOCK + lax.broadcasted_iota(jnp.int32, s.shape, 1)
    return jnp.where((col >> CHUNK_SHIFT) <= (row >> CHUNK_SHIFT), s, -jnp.inf)


def mla_fwd(qh, kh, vh):
    t = qh.shape[0]
    tb = min(ATT_BLOCK, t)
    nb = t // tb

    def body(q_ref, k_ref, v_ref, o_ref, lse_ref, m_s, l_s, acc):
        qi, ki = pl.program_id(1), pl.program_id(2)

        @pl.when(ki == 0)
        def _():
            m_s[...] = jnp.full_like(m_s, NEG)
            l_s[...] = jnp.zeros_like(l_s)
            acc[...] = jnp.zeros_like(acc)

        @pl.when(ki <= qi)
        def _():
            s = _att_scores(q_ref[...], k_ref[...], qi, ki)
            m_new = jnp.maximum(m_s[...], jnp.max(s, axis=1, keepdims=True))
            p = jnp.exp(s - m_new)
            alpha = jnp.exp(m_s[...] - m_new)
            l_s[...] = alpha * l_s[...] + jnp.sum(p, axis=1, keepdims=True)
            acc[...] = alpha * acc[...] + _bdot(p, v_ref[...], 1, 0)
            m_s[...] = m_new

        @pl.when(ki == qi)
        def _():
            o_ref[...] = acc[...] / l_s[...]
            lse_ref[0] = m_s[...] + jnp.log(l_s[...])

    kv = lambda h, qi, ki: (jnp.minimum(ki, qi), h)
    return _call(body, "mla_fwd", (MLA_HEADS, nb, nb),
                 [pl.BlockSpec((tb, HEAD_PAD), lambda h, qi, ki: (qi, h)), pl.BlockSpec((tb, HEAD_PAD), kv),
                  pl.BlockSpec((tb, V_HEAD), kv)],
                 [pl.BlockSpec((tb, V_HEAD), lambda h, qi, ki: (qi, h)), pl.BlockSpec((1, tb, 1), lambda h, qi, ki: (h, qi, 0))],
                 [_sds((t, MLA_HEADS * V_HEAD), f32), _sds((MLA_HEADS, t, 1), f32)],
                 [pltpu.VMEM((tb, 1), f32), pltpu.VMEM((tb, 1), f32), pltpu.VMEM((tb, V_HEAD), f32)])(qh, kh, vh)


def mla_delta(o, do):
    t = o.shape[0]
    tm = _tile(t, (512, 256, 128, 64))

    def body(o_ref, do_ref, d_ref):
        d_ref[0] = jnp.sum(o_ref[...] * do_ref[...], axis=1, keepdims=True)

    blk = pl.BlockSpec((tm, V_HEAD), lambda i, h: (i, h))
    return _call(body, "mla_delta", (t // tm, MLA_HEADS), [blk, blk], pl.BlockSpec((1, tm, 1), lambda i, h: (h, i, 0)),
                 _sds((MLA_HEADS, t, 1), f32))(o, do)


def _att_ds(q, k, v, do, lse, delta, qi, ki):
    scale = (NOPE + ROPE) ** -0.5
    p = jnp.exp(_att_scores(q, k, qi, ki) - lse)
    dp = _bdot(do, v, 1, 1)
    return p, p * (dp - delta) * scale


def mla_bwd_dq(qh, kh, vh, do, lse, delta):
    t = qh.shape[0]
    tb = min(ATT_BLOCK, t)
    nb = t // tb

    def body(q_ref, k_ref, v_ref, do_ref, lse_ref, dl_ref, dq_ref, acc):
        qi, ki = pl.program_id(1), pl.program_id(2)

        @pl.when(ki == 0)
        def _():
            acc[...] = jnp.zeros_like(acc)

        @pl.when(ki <= qi)
        def _():
            _, ds = _att_ds(q_ref[...], k_ref[...], v_ref[...], do_ref[...], lse_ref[0], dl_ref[0], qi, ki)
            acc[...] += _bdot(ds, k_ref[...], 1, 0)

        @pl.when(ki == qi)
        def _():
            dq_ref[...] = acc[...]

    kv = lambda h, qi, ki: (jnp.minimum(ki, qi), h)
    qs = lambda h, qi, ki: (qi, h)
    vec = pl.BlockSpec((1, tb, 1), lambda h, qi, ki: (h, qi, 0))
    return _call(body, "mla_bwd_dq", (MLA_HEADS, nb, nb),
                 [pl.BlockSpec((tb, HEAD_PAD), qs), pl.BlockSpec((tb, HEAD_PAD), kv), pl.BlockSpec((tb, V_HEAD), kv),
                  pl.BlockSpec((tb, V_HEAD), qs), vec, vec],
                 pl.BlockSpec((tb, HEAD_PAD), qs), _sds((t, MLA_HEADS * HEAD_PAD), f32),
                 [pltpu.VMEM((tb, HEAD_PAD), f32)])(qh, kh, vh, do, lse, delta)


def mla_bwd_dkv(qh, kh, vh, do, lse, delta):
    t = qh.shape[0]
    tb = min(ATT_BLOCK, t)
    nb = t // tb

    def body(q_ref, k_ref, v_ref, do_ref, lse_ref, dl_ref, dk_ref, dv_ref, dk_acc, dv_acc):
        ki, qi = pl.program_id(1), pl.program_id(2)

        @pl.when(qi == 0)
        def _():
            dk_acc[...] = jnp.zeros_like(dk_acc)
            dv_acc[...] = jnp.zeros_like(dv_acc)

        @pl.when(qi >= ki)
        def _():
            p, ds = _att_ds(q_ref[...], k_ref[...], v_ref[...], do_ref[...], lse_ref[0], dl_ref[0], qi, ki)
            dv_acc[...] += _bdot(p, do_ref[...], 0, 0)
            dk_acc[...] += _bdot(ds, q_ref[...], 0, 0)

        @pl.when(qi == nb - 1)
        def _():
            dk_ref[...] = dk_acc[...]
            dv_ref[...] = dv_acc[...]

    qs = lambda h, ki, qi: (jnp.maximum(qi, ki), h)
    ks = lambda h, ki, qi: (ki, h)
    vec = pl.BlockSpec((1, tb, 1), lambda h, ki, qi: (h, jnp.maximum(qi, ki), 0))
    return _call(body, "mla_bwd_dkv", (MLA_HEADS, nb, nb),
                 [pl.BlockSpec((tb, HEAD_PAD), qs), pl.BlockSpec((tb, HEAD_PAD), ks), pl.BlockSpec((tb, V_HEAD), ks),
                  pl.BlockSpec((tb, V_HEAD), qs), vec, vec],
                 [pl.BlockSpec((tb, HEAD_PAD), ks), pl.BlockSpec((tb, V_HEAD), ks)],
                 [_sds((t, MLA_HEADS * HEAD_PAD), f32), _sds((t, MLA_HEADS * V_HEAD), f32)],
                 [pltpu.VMEM((tb, HEAD_PAD), f32), pltpu.VMEM((tb, V_HEAD), f32)])(qh, kh, vh, do, lse, delta)


PAD = 8


def _conv_taps(pad_ref, w, width, t):
    y = pad_ref[PAD - width + 1:PAD - width + 1 + t, :] * w[0:1, :]
    for j in range(1, width):
        y = y + pad_ref[PAD - width + 1 + j:PAD - width + 1 + j + t, :] * w[j:j + 1, :]
    return y


def _conv_bwd(xpad_ref, dpad_ref, w, da, width, t):
    dpad_ref[0:t, :] = da
    dpad_ref[t:t + PAD, :] = jnp.zeros((PAD, da.shape[1]), f32)
    dx = dpad_ref[width - 1:width - 1 + t, :] * w[0:1, :]
    for j in range(1, width):
        dx = dx + dpad_ref[width - 1 - j:width - 1 - j + t, :] * w[j:j + 1, :]
    dws = [jnp.sum(da * xpad_ref[PAD - width + 1 + j:PAD - width + 1 + j + t, :], axis=0, keepdims=True) for j in range(width)]
    return dx, dws


def _load_pad(pad_ref, x, t):
    pad_ref[0:PAD, :] = jnp.zeros((PAD, x.shape[1]), f32)
    pad_ref[PAD:PAD + t, :] = x


def qk_conv(z_main, conv_qk):
    t = z_main.shape[0]
    nq = ML_QK // 128
    base = O_Q // 128

    def body(z_ref, w_ref, o_ref, pad):
        _load_pad(pad, z_ref[...], t)
        a = _conv_taps(pad, w_ref[...], ML_CONV, t)
        sc = jnp.where(pl.program_id(0) < nq, ML_DK ** -0.5, 1.0)
        o_ref[...] = jax.nn.silu(a) * sc

    return _call(body, "qk_conv", (2 * nq,),
                 [pl.BlockSpec((t, 128), lambda j: (0, base + j)), pl.BlockSpec((ML_CONV, 128), lambda j: (0, j))],
                 pl.BlockSpec((t, 128), lambda j: (0, j)), _sds((t, 2 * ML_QK), f32),
                 [pltpu.VMEM((t + PAD, 128), f32)])(z_main, conv_qk)


def qk_conv_bwd(z_main, conv_qk, dqk):
    t = z_main.shape[0]
    nq = ML_QK // 128
    base = O_Q // 128

    def body(z_ref, w_ref, d_ref, dz_ref, dw_ref, pad, dpad):
        _load_pad(pad, z_ref[...], t)
        w = w_ref[...]
        a = _conv_taps(pad, w, ML_CONV, t)
        sc = jnp.where(pl.program_id(0) < nq, ML_DK ** -0.5, 1.0)
        _, vjp = jax.vjp(jax.nn.silu, a)
        da, = vjp(d_ref[...] * sc)
        dx, dws = _conv_bwd(pad, dpad, w, da, ML_CONV, t)
        dz_ref[...] = dx.astype(bf16)
        for j in range(ML_CONV):
            dw_ref[j:j + 1, :] = dws[j]

    return _call(body, "qk_conv_bwd", (2 * nq,),
                 [pl.BlockSpec((t, 128), lambda j: (0, base + j)), pl.BlockSpec((ML_CONV, 128), lambda j: (0, j)),
                  pl.BlockSpec((t, 128), lambda j: (0, j))],
                 [pl.BlockSpec((t, 128), lambda j: (0, j)), pl.BlockSpec((ML_CONV, 128), lambda j: (0, j))],
                 [_sds((t, 2 * ML_QK), bf16), _sds((ML_CONV, 2 * ML_QK), f32)],
                 [pltpu.VMEM((t + PAD, 128), f32), pltpu.VMEM((t + PAD, 128), f32)])(z_main, conv_qk, dqk)


def glu_fwd(hup, conv_w, bias):
    t, f2 = hup.shape
    nf = f2 // 2 // 128

    def body(h1_ref, h2_ref, w1_ref, w2_ref, b1_ref, b2_ref, o_ref, pad):
        _load_pad(pad, h1_ref[...], t)
        a1 = _conv_taps(pad, w1_ref[...], FFN_CONV, t) + b1_ref[...]
        _load_pad(pad, h2_ref[...], t)
        a2 = _conv_taps(pad, w2_ref[...], FFN_CONV, t) + b2_ref[...]
        o_ref[...] = (jax.nn.silu(a1) * a2).astype(bf16)

    col = lambda off: pl.BlockSpec((t, 128), lambda j: (0, j + off))
    wsp = lambda off: pl.BlockSpec((FFN_CONV, 128), lambda j: (0, j + off))
    bsp = lambda off: pl.BlockSpec((1, 128), lambda j: (0, j + off))
    return _call(body, "glu_fwd", (nf,), [col(0), col(nf), wsp(0), wsp(nf), bsp(0), bsp(nf)], col(0), _sds((t, f2 // 2), bf16),
                 [pltpu.VMEM((t + PAD, 128), f32)])(hup, hup, conv_w, conv_w, bias, bias)


def glu_bwd(hup, conv_w, bias, dg):
    t, f2 = hup.shape
    nf = f2 // 2 // 128

    def body(hs_ref, hp_ref, ws_ref, wp_ref, bs_ref, bp_ref, dg_ref, dh_ref, dw_ref, db_ref, pad, ppad, dpad):
        _load_pad(pad, hs_ref[...], t)
        w = ws_ref[...]
        a_self = _conv_taps(pad, w, FFN_CONV, t) + bs_ref[...]
        _load_pad(ppad, hp_ref[...], t)
        a_part = _conv_taps(ppad, wp_ref[...], FFN_CONV, t) + bp_ref[...]
        d = dg_ref[...]
        _, vjp = jax.vjp(jax.nn.silu, a_self)
        d_first, = vjp(d * a_part)
        d_second = d * jax.nn.silu(a_part)
        da = jnp.where(pl.program_id(0) < nf, d_first, d_second)
        dx, dws = _conv_bwd(pad, dpad, w, da, FFN_CONV, t)
        dh_ref[...] = dx.astype(bf16)
        for j in range(FFN_CONV):
            dw_ref[j:j + 1, :] = dws[j]
        db_ref[...] = jnp.sum(da, axis=0, keepdims=True)

    part = lambda j: (j + nf) % (2 * nf)
    col = pl.BlockSpec((t, 128), lambda j: (0, j))
    pcol = pl.BlockSpec((t, 128), lambda j: (0, part(j)))
    wsp = pl.BlockSpec((FFN_CONV, 128), lambda j: (0, j))
    pwsp = pl.BlockSpec((FFN_CONV, 128), lambda j: (0, part(j)))
    bsp = pl.BlockSpec((1, 128), lambda j: (0, j))
    pbsp = pl.BlockSpec((1, 128), lambda j: (0, part(j)))
    return _call(body, "glu_bwd", (2 * nf,), [col, pcol, wsp, pwsp, bsp, pbsp, pl.BlockSpec((t, 128), lambda j: (0, j % nf))],
                 [col, wsp, bsp], [_sds((t, f2), bf16), _sds((FFN_CONV, f2), f32), _sds((1, f2), f32)],
                 [pltpu.VMEM((t + PAD, 128), f32)] * 3)(hup, hup, conv_w, conv_w, bias, bias, dg)


def gate_act(z_tail, b_tile):
    t = z_tail.shape[0]
    tm = _tile(t, (512, 256, 128, 64))

    def body(z_ref, b_ref, o_ref):
        x = z_ref[...] + b_ref[...]
        lane = lax.broadcasted_iota(jnp.int32, x.shape, 1)
        o_ref[...] = jnp.where(lane < T_F, x, jax.nn.log_sigmoid(x))

    row = pl.BlockSpec((tm, TAIL), lambda i: (i, 0))
    return _call(body, "gate_act", (t // tm,), [row, pl.BlockSpec((1, TAIL), lambda i: (0, 0))], row, _sds((t, TAIL), f32))(z_tail, b_tile)


def tail_bwd(z_tail, b_tile, dzt_pe, dgate):
    t = z_tail.shape[0]
    tm = _tile(t, (512, 256, 128, 64))

    def body(z_ref, b_ref, dpe_ref, dg_ref, dz_ref, db_ref):
        x = z_ref[...] + b_ref[...]
        lane = lax.broadcasted_iota(jnp.int32, x.shape, 1)
        _, vjp = jax.vjp(jax.nn.log_sigmoid, x)
        df, = vjp(dg_ref[...])
        dgates = jnp.where(lane < T_F, dg_ref[...], df)
        dgates = jnp.where(jnp.logical_and(lane >= T_I, lane < T_F + ML_HEADS), dgates, 0.0)
        dz_ref[...] = jnp.where(lane < ROPE, dpe_ref[...], dgates).astype(bf16)
        _acc_row(db_ref, jnp.sum(dgates, axis=0, keepdims=True), pl.program_id(0) == 0)

    row = pl.BlockSpec((tm, TAIL), lambda i: (i, 0))
    return _call(body, "tail_bwd", (t // tm,), [row, pl.BlockSpec((1, TAIL), lambda i: (0, 0)), row, row],
                 [row, pl.BlockSpec((8, TAIL), lambda i: (0, 0))], [_sds((t, TAIL), bf16), _sds((8, TAIL), f32)])(z_tail, b_tile, dzt_pe, dgate)


def _mlstm_step(q, k, v, igr, fgr, c_mat, n_vec, m):
    ln = CHUNK
    row = lax.broadcasted_iota(jnp.int32, (ln, ln), 0)
    col = lax.broadcasted_iota(jnp.int32, (ln, ln), 1)
    eye = row == col

    def to_col(r):
        return jnp.sum(jnp.where(eye, jnp.broadcast_to(r, (ln, ln)), 0.0), axis=1, keepdims=True)

    bc_r = jnp.sum(jnp.where(row <= col, jnp.broadcast_to(to_col(fgr), (ln, ln)), 0.0), axis=0, keepdims=True)
    bc_c = to_col(bc_r)
    logw = jnp.where(col <= row, bc_c - bc_r + igr, -jnp.inf)
    inter = bc_c + m
    m_t = jnp.maximum(inter, jnp.max(logw, axis=1, keepdims=True))
    w_intra = jnp.exp(logw - m_t)
    w_inter = jnp.exp(inter - m_t)
    sc = _bdot(q, k, 1, 1) * w_intra
    num = w_inter * _bdot(q, c_mat, 1, 0) + _bdot(sc, v, 1, 0)
    qn = jnp.sum(q.astype(bf16).astype(f32) * n_vec.astype(bf16).astype(f32), axis=1, keepdims=True)
    den = w_inter * qn + jnp.sum(sc, axis=1, keepdims=True)
    h = num / jnp.maximum(jnp.abs(den), jnp.exp(-m_t))
    lane = lax.broadcasted_iota(jnp.int32, (1, ln), 1)
    b_last = jnp.sum(jnp.where(lane == ln - 1, bc_r, 0.0), axis=1, keepdims=True)
    logu = b_last - bc_r + igr
    m_new = jnp.maximum(b_last + m, jnp.max(logu, axis=1, keepdims=True))
    decay = jnp.exp(b_last + m - m_new)
    u_c = to_col(jnp.exp(logu - m_new))
    c_new = decay * c_mat + _bdot(u_c * k, v, 0, 0)
    n_new = decay * n_vec + jnp.sum(u_c.astype(bf16).astype(f32) * k.astype(bf16).astype(f32), axis=0, keepdims=True)
    return h, c_new, n_new, m_new


def _ml_specs(nc, rev):
    cc = (lambda c: nc - 1 - c) if rev else (lambda c: c)
    kq = ML_QK // ML_DK
    q = pl.BlockSpec((CHUNK, ML_DK), lambda h, c: (cc(c), h))
    k = pl.BlockSpec((CHUNK, ML_DK), lambda h, c: (cc(c), kq + h))
    v = pl.BlockSpec((CHUNK, ML_DV), lambda h, c: (cc(c), O_V // ML_DV + h))
    hv = pl.BlockSpec((CHUNK, ML_DV), lambda h, c: (cc(c), h))
    gate = pl.BlockSpec((1, 1, 1, CHUNK), lambda h, c: (h, cc(c), 0, 0))
    cm = pl.BlockSpec((1, 1, ML_DK, ML_DV), lambda h, c: (h, cc(c), 0, 0))
    nv = pl.BlockSpec((1, 1, 1, ML_DK), lambda h, c: (h, cc(c), 0, 0))
    ms = pl.BlockSpec((1, 1, 1, 1), lambda h, c: (h, cc(c), 0, 0))
    return q, k, v, hv, gate, cm, nv, ms


_ML_STATE = [pltpu.VMEM((ML_DK, ML_DV), f32), pltpu.VMEM((1, ML_DK), f32), pltpu.VMEM((1, 1), f32)]


def mlstm_fwd(qk_act, z_main, ig, fg):
    t = qk_act.shape[0]
    nc = t // CHUNK

    def body(q_ref, k_ref, v_ref, ig_ref, fg_ref, h_ref, c_out, n_out, m_out, c_s, n_s, m_s):
        @pl.when(pl.program_id(1) == 0)
        def _():
            c_s[...] = jnp.zeros_like(c_s)
            n_s[...] = jnp.zeros_like(n_s)
            m_s[...] = jnp.zeros_like(m_s)

        c_out[0, 0] = c_s[...]
        n_out[0, 0] = n_s[...]
        m_out[0, 0] = m_s[...]
        h, c2, n2, m2 = _mlstm_step(q_ref[...], k_ref[...], v_ref[...], ig_ref[0, 0], fg_ref[0, 0], c_s[...], n_s[...], m_s[...])
        h_ref[...] = h
        c_s[...] = c2
        n_s[...] = n2
        m_s[...] = m2

    q, k, v, hv, gate, cm, nv, ms = _ml_specs(nc, False)
    return _call(body, "mlstm_fwd", (ML_HEADS, nc), [q, k, v, gate, gate], [hv, cm, nv, ms],
                 [_sds((t, ML_V), f32), _sds((ML_HEADS, nc, ML_DK, ML_DV), f32), _sds((ML_HEADS, nc, 1, ML_DK), f32),
                  _sds((ML_HEADS, nc, 1, 1), f32)], _ML_STATE)(qk_act, qk_act, z_main, ig, fg)


def mlstm_bwd(qk_act, z_main, ig, fg, c_all, n_all, m_all, dh):
    t = qk_act.shape[0]
    nc = t // CHUNK

    def body(q_ref, k_ref, v_ref, ig_ref, fg_ref, c_ref, n_ref, m_ref, dh_ref, dq_ref, dk_ref, dv_ref, dig_ref, dfg_ref,
             dc_s, dn_s, dm_s):
        @pl.when(pl.program_id(1) == 0)
        def _():
            dc_s[...] = jnp.zeros_like(dc_s)
            dn_s[...] = jnp.zeros_like(dn_s)
            dm_s[...] = jnp.zeros_like(dm_s)

        _, vjp = jax.vjp(_mlstm_step, q_ref[...], k_ref[...], v_ref[...], ig_ref[0, 0], fg_ref[0, 0],
                         c_ref[0, 0], n_ref[0, 0], m_ref[0, 0])
        dq, dk, dv, dig, dfg, dc, dn, dm = vjp((dh_ref[...], dc_s[...], dn_s[...], dm_s[...]))
        dq_ref[...] = dq
        dk_ref[...] = dk
        dv_ref[...] = dv.astype(bf16)
        dig_ref[0, 0] = dig
        dfg_ref[0, 0] = dfg
        dc_s[...] = dc
        dn_s[...] = dn
        dm_s[...] = dm

    q, k, v, hv, gate, cm, nv, ms = _ml_specs(nc, True)
    gshape = _sds((ML_HEADS, nc, 1, CHUNK), f32)
    return _call(body, "mlstm_bwd", (ML_HEADS, nc), [q, k, v, gate, gate, cm, nv, ms, hv], [q, q, hv, gate, gate],
                 [_sds((t, ML_QK), f32), _sds((t, ML_QK), f32), _sds((t, ML_V), bf16), gshape, gshape],
                 _ML_STATE)(qk_act, qk_act, z_main, ig, fg, c_all, n_all, m_all, dh)


def _ml_out(h, zo, g):
    return _rms(h, g) * jax.nn.sigmoid(zo)


def mlstm_out(h, z_main, g_hnorm):
    t = h.shape[0]
    tm = _tile(t, (512, 256, 128, 64))
    zo = O_O // ML_DV

    def body(h_ref, z_ref, g_ref, y_ref):
        y_ref[...] = _ml_out(h_ref[...], z_ref[...], g_ref[0]).astype(bf16)

    blk = pl.BlockSpec((tm, ML_DV), lambda i, hd: (i, hd))
    return _call(body, "mlstm_out", (t // tm, ML_HEADS),
                 [blk, pl.BlockSpec((tm, ML_DV), lambda i, hd: (i, zo + hd)), pl.BlockSpec((1, 1, ML_DV), lambda i, hd: (hd, 0, 0))],
                 blk, _sds((t, ML_V), bf16))(h, z_main, g_hnorm)


def mlstm_out_bwd(h, z_main, g_hnorm, dy):
    t = h.shape[0]
    tm = _tile(t, (512, 256, 128, 64))
    zo = O_O // ML_DV

    def body(h_ref, z_ref, g_ref, dy_ref, dh_ref, dzo_ref, dg_ref):
        _, vjp = jax.vjp(_ml_out, h_ref[...], z_ref[...], g_ref[0])
        dh, dz, dg = vjp(dy_ref[...])
        dh_ref[...] = dh
        dzo_ref[...] = dz.astype(bf16)

        @pl.when(pl.program_id(1) == 0)
        def _():
            dg_ref[...] = jnp.zeros_like(dg_ref)

        dg_ref[0, 0:1, :] += dg

    blk = pl.BlockSpec((tm, ML_DV), lambda hd, i: (i, hd))
    return _call(body, "mlstm_out_bwd", (ML_HEADS, t // tm),
                 [blk, pl.BlockSpec((tm, ML_DV), lambda hd, i: (i, zo + hd)), pl.BlockSpec((1, 1, ML_DV), lambda hd, i: (hd, 0, 0)), blk],
                 [blk, blk, pl.BlockSpec((1, 8, ML_DV), lambda hd, i: (hd, 0, 0))],
                 [_sds((t, ML_V), f32), _sds((t, ML_V), bf16), _sds((ML_HEADS, 8, ML_DV), f32)])(h, z_main, g_hnorm, dy)


def _merge(ga, gb, ya, yb):
    return jax.nn.sigmoid(ga) * ya + jax.nn.sigmoid(gb) * yb


def _merge_specs(t, d):
    tm = _tile(t, (512, 256, 128, 64))
    bw = _tile(d, (512, 256, 128))
    assert O_GA % bw == 0 and (O_GA + d) % bw == 0
    blk = pl.BlockSpec((tm, bw), lambda i, j: (i, j))
    ga = pl.BlockSpec((tm, bw), lambda i, j: (i, O_GA // bw + j))
    gb = pl.BlockSpec((tm, bw), lambda i, j: (i, (O_GA + d) // bw + j))
    return tm, bw, blk, ga, gb


def merge_fwd(z_main, ya, yb):
    t, d = ya.shape
    tm, bw, blk, ga, gb = _merge_specs(t, d)

    def body(ga_ref, gb_ref, ya_ref, yb_ref, o_ref):
        o_ref[...] = _merge(ga_ref[...], gb_ref[...], ya_ref[...], yb_ref[...]).astype(bf16)

    return _call(body, "merge_fwd", (t // tm, d // bw), [ga, gb, blk, blk], blk, _sds((t, d), bf16))(z_main, z_main, ya, yb)


def merge_bwd(z_main, ya, yb, dmerged):
    t, d = ya.shape
    tm, bw, blk, ga, gb = _merge_specs(t, d)

    def body(ga_ref, gb_ref, ya_ref, yb_ref, dm_ref, dga_ref, dgb_ref, dya_ref, dyb_ref):
        _, vjp = jax.vjp(_merge, ga_ref[...], gb_ref[...], ya_ref[...], yb_ref[...])
        dga, dgb, dya, dyb = vjp(dm_ref[...])
        dga_ref[...] = dga.astype(bf16)
        dgb_ref[...] = dgb.astype(bf16)
        dya_ref[...] = dya.astype(bf16)
        dyb_ref[...] = dyb.astype(bf16)

    return _call(body, "merge_bwd", (t // tm, d // bw), [ga, gb, blk, blk, blk], [blk] * 4, [_sds((t, d), bf16)] * 4)(
        z_main, z_main, ya, yb, dmerged)


def _cross(cq, ck, cv, gq, gk):
    outs = []
    for hd in range(CR_HEADS):
        sl = slice(hd * CR_HD, (hd + 1) * CR_HD)
        q = _rms(cq[:, sl], gq)
        k = _rms(ck[:, sl], gk)
        s = _bdot(q, k, 1, 1) * (CR_HD ** -0.5)
        p = jax.nn.softmax(s, axis=-1)
        outs.append(_bdot(p, cv[:, sl], 1, 0))
    return jnp.concatenate(outs, axis=1)


def cross_fwd(cq, ck, cv, gq, gk):
    t, w = cq.shape
    nm = ck.shape[0]
    tm = _tile(t, (512, 256, 128, 64))

    def body(q_ref, k_ref, v_ref, gq_ref, gk_ref, o_ref):
        o_ref[...] = _cross(q_ref[...], k_ref[...], v_ref[...], gq_ref[...], gk_ref[...]).astype(bf16)

    row = pl.BlockSpec((tm, w), lambda i: (i, 0))
    full = pl.BlockSpec((nm, w), lambda i: (0, 0))
    gain = pl.BlockSpec((1, CR_HD), lambda i: (0, 0))
    return _call(body, "cross_fwd", (t // tm,), [row, full, full, gain, gain], row, _sds((t, w), bf16))(cq, ck, cv, gq, gk)


def cross_bwd(cq, ck, cv, gq, gk, do):
    t, w = cq.shape
    nm = ck.shape[0]
    tm = _tile(t, (512, 256, 128, 64))

    def body(q_ref, k_ref, v_ref, gq_ref, gk_ref, do_ref, dq_ref, dk_ref, dv_ref, dgq_ref, dgk_ref):
        first = pl.program_id(0) == 0
        _, vjp = jax.vjp(_cross, q_ref[...], k_ref[...], v_ref[...], gq_ref[...], gk_ref[...])
        dq, dk, dv, dgq, dgk = vjp(do_ref[...])
        dq_ref[...] = dq.astype(bf16)

        @pl.when(first)
        def _():
            dk_ref[...] = jnp.zeros_like(dk_ref)
            dv_ref[...] = jnp.zeros_like(dv_ref)

        dk_ref[...] += dk
        dv_ref[...] += dv
        _acc_row(dgq_ref, dgq, first)
        _acc_row(dgk_ref, dgk, first)

    row = pl.BlockSpec((tm, w), lambda i: (i, 0))
    full = pl.BlockSpec((nm, w), lambda i: (0, 0))
    gain = pl.BlockSpec((1, CR_HD), lambda i: (0, 0))
    acc = pl.BlockSpec((8, CR_HD), lambda i: (0, 0))
    return _call(body, "cross_bwd", (t // tm,), [row, full, full, gain, gain, row], [row, full, full, acc, acc],
                 [_sds((t, w), bf16), _sds((nm, w), f32), _sds((nm, w), f32), _sds((8, CR_HD), f32), _sds((8, CR_HD), f32)])(
        cq, ck, cv, gq, gk, do)


def loss_head(x2, fo, target):
    t, d = x2.shape
    tm = _tile(t, (256, 128, 64, 32, 16, 8))

    def body(a_ref, b_ref, t_ref, dx_ref, l_ref):
        err = a_ref[...] + b_ref[...] - t_ref[...]
        dx_ref[...] = err / d
        part = 0.5 * jnp.sum(jnp.mean(err * err, axis=1, keepdims=True), axis=0, keepdims=True)
        _acc_row(l_ref, jnp.broadcast_to(part, (1, 128)), pl.program_id(0) == 0)

    row = pl.BlockSpec((tm, d), lambda i: (i, 0))
    return _call(body, "loss_head", (t // tm,), [row, row, row], [row, pl.BlockSpec((8, 128), lambda i: (0, 0))],
                 [_sds((t, d), f32), _sds((8, 128), f32)])(x2, fo, target)


def _place():
    x, y, c = lax.axis_index("x"), lax.axis_index("y"), lax.axis_index("c")
    peers = []
    for k in range(1, N_DEV):
        px = 1 - x if k & 4 else x
        py = 1 - y if k & 2 else y
        pc = 1 - c if k & 1 else c
        peers.append(((px, py, pc), 4 * px + 2 * py + pc))
    return 4 * x + 2 * y + c, peers


def _exchange(arrs, name, scatter):
    n = len(arrs)
    n_rel = N_DEV - 1

    def body(*refs):
        ins, outs = refs[:n], refs[n:2 * n]
        send_sems, recv_sems, local_sems = refs[2 * n:]
        me, peers = _place()

        def src(a, idx):
            return ins[a].at[idx] if scatter else ins[a]

        local = [pltpu.make_async_copy(src(a, me), outs[a].at[me], local_sems.at[a]) for a in range(n)]
        for cp in local:
            cp.start()

        def remote(a, k, src_idx, dst_idx):
            dev, _ = peers[k]
            return pltpu.make_async_remote_copy(
                src_ref=src(a, src_idx), dst_ref=outs[a].at[dst_idx], send_sem=send_sems.at[a * n_rel + k],
                recv_sem=recv_sems.at[a * n_rel + k], device_id=dev, device_id_type=MESH)

        sends = [remote(a, k, peers[k][1], me) for a in range(n) for k in range(n_rel)]
        for cp in sends:
            cp.start()
        for a in range(n):
            for k in range(n_rel):
                remote(a, k, me, peers[k][1]).wait_recv()
        for cp in sends:
            cp.wait_send()
        for cp in local:
            cp.wait()

    any_spec = pl.BlockSpec(memory_space=pl.ANY)
    out_shape = [_sds(a.shape if scatter else (N_DEV,) + a.shape, a.dtype) for a in arrs]
    return pl.pallas_call(
        body, name=name, in_specs=[any_spec] * n, out_specs=[any_spec] * n, out_shape=out_shape,
        scratch_shapes=[pltpu.SemaphoreType.DMA((n * n_rel,)), pltpu.SemaphoreType.DMA((n * n_rel,)), pltpu.SemaphoreType.DMA((n,))],
    )(*arrs)


def cast_bf16(w, name):
    r, c = w.shape
    tr = _tile(r, (256, 128, 64, 32, 16))

    def body(w_ref, o_ref):
        o_ref[...] = w_ref[...].astype(bf16)

    blk = pl.BlockSpec((tr, c), lambda i: (i, 0))
    return _call(body, name, (r // tr,), [blk], blk, _sds((r, c), bf16))(w)


def _adamw(w, g, m, v):
    m = ADAM_B1 * m + (1.0 - ADAM_B1) * g
    v = ADAM_B2 * v + (1.0 - ADAM_B2) * jnp.square(g)
    m_hat = m / (1.0 - ADAM_B1 ** ADAM_STEP)
    v_hat = v / (1.0 - ADAM_B2 ** ADAM_STEP)
    delta = -ADAM_LR * (m_hat / (jnp.sqrt(v_hat) + ADAM_EPS) + ADAM_WD * w)
    return delta, m, v


def adam_sum(parts, w, m, v, name):
    _, r, c = parts.shape
    budget = 4 * 1024 * 1024
    tr = r
    for cand in (1024, 512, 256, 128, 64, 32, 16, 8):
        if r % cand == 0 and N_DEV * cand * c * 4 <= budget:
            tr = cand
            break
    else:
        tr = 8 if r % 8 == 0 else r

    def body(p_ref, w_ref, m_ref, v_ref, g_ref, d_ref, m2_ref, v2_ref):
        g = p_ref[0]
        for k in range(1, N_DEV):
            g = g + p_ref[k]
        d, m2, v2 = _adamw(w_ref[...], g, m_ref[...], v_ref[...])
        g_ref[...] = g
        d_ref[...] = d
        m2_ref[...] = m2
        v2_ref[...] = v2

    blk = pl.BlockSpec((tr, c), lambda i: (i, 0))
    return _call(body, name, (r // tr,), [pl.BlockSpec((N_DEV, tr, c), lambda i: (0, i, 0)), blk, blk, blk], [blk] * 4,
                 [_sds((r, c), f32)] * 4)(parts, w, m, v)


def sum_parts(parts, name):
    _, r, c = parts.shape

    def body(p_ref, o_ref):
        g = p_ref[0]
        for k in range(1, N_DEV):
            g = g + p_ref[k]
        o_ref[...] = g

    return pl.pallas_call(body, name=name, out_shape=_sds((r, c), f32))(parts)


def adam_flat(w, g, m, v, name):
    def body(w_ref, g_ref, m_ref, v_ref, d_ref, m2_ref, v2_ref):
        d, m2, v2 = _adamw(w_ref[...], g_ref[...], m_ref[...], v_ref[...])
        d_ref[...] = d
        m2_ref[...] = m2
        v2_ref[...] = v2

    return pl.pallas_call(body, name=name, out_shape=[_sds(w.shape, f32)] * 3)(w, g, m, v)


def _pack(vecs, multiple):
    flat = jnp.concatenate([v.reshape(-1) for v in vecs])
    n = flat.shape[0]
    total = -(-n // multiple) * multiple
    return jnp.pad(flat, (0, total - n))


def _unpack(flat, shapes):
    out, pos = [], 0
    for s in shapes:
        n = 1
        for d in s:
            n *= d
        out.append(flat[pos:pos + n].reshape(s))
        pos += n
    return out


def _pad_lanes(v, width=TAIL):
    return jnp.pad(v, ((0, 0), (0, width - v.shape[1])))


def kernel(x, mem, positions, g_mix, w_in, g_qa, w_qb, g_kva, w_kvb, g_qn_nope, g_qn_pe, g_kn_nope, g_kn_pe, conv_qk, b_if, g_hnorm, p_a, p_b, w_out, g_cross, g_mem, wq_c, wk_c, wv_c, g_cq, g_ck, wo_c, g_ffn, w_up, conv_ffn, b_conv_ffn, w_down, loss_target, m_g_mix, m_w_in, m_g_qa, m_w_qb, m_g_kva, m_w_kvb, m_g_qn_nope, m_g_qn_pe, m_g_kn_nope, m_g_kn_pe, m_conv_qk, m_b_if, m_g_hnorm, m_p_a, m_p_b, m_w_out, m_g_cross, m_g_mem, m_wq_c, m_wk_c, m_wv_c, m_g_cq, m_g_ck, m_wo_c, m_g_ffn, m_w_up, m_conv_ffn, m_b_conv_ffn, m_w_down, v_g_mix, v_w_in, v_g_qa, v_w_qb, v_g_kva, v_w_kvb, v_g_qn_nope, v_g_qn_pe, v_g_kn_nope, v_g_kn_pe, v_conv_qk, v_b_if, v_g_hnorm, v_p_a, v_p_b, v_w_out, v_g_cross, v_g_mem, v_wq_c, v_wk_c, v_wv_c, v_g_cq, v_g_ck, v_wo_c, v_g_ffn, v_w_up, v_conv_ffn, v_b_conv_ffn, v_w_down):
    args = dict(locals())
    names = ['g_mix', 'w_in', 'g_qa', 'w_qb', 'g_kva', 'w_kvb', 'g_qn_nope', 'g_qn_pe', 'g_kn_nope', 'g_kn_pe', 'conv_qk', 'b_if',
             'g_hnorm', 'p_a', 'p_b', 'w_out', 'g_cross', 'g_mem', 'wq_c', 'wk_c', 'wv_c', 'g_cq', 'g_ck', 'wo_c', 'g_ffn', 'w_up',
             'conv_ffn', 'b_conv_ffn', 'w_down']
    big = ['w_in', 'w_qb', 'w_kvb', 'p_a', 'p_b', 'w_out', 'wq_c', 'wk_c', 'wv_c', 'wo_c', 'w_up', 'w_down']
    sharded_small = ['conv_qk', 'g_hnorm', 'conv_ffn']
    replicated = [n for n in names if n not in big and n not in sharded_small]

    t, d = x.shape[1], x.shape[2]
    x2d, tgt = x[0], loss_target[0]
    mem2d = mem[0]
    me = 4 * lax.axis_index("x") + 2 * lax.axis_index("y") + lax.axis_index("c")
    nc = t // CHUNK
    f2 = b_conv_ffn.shape[1]
    wmain = O_GA + 2 * d

    shards = {n: args[n][0] for n in big}
    small_local = _pack([args[n] for n in sharded_small], 128).reshape(1, -1)
    gathered = _exchange([cast_bf16(shards[n], "cast_" + n) for n in big] + [small_local], "comm_gather_weights", scatter=False)
    gw = dict(zip(big, gathered[:-1]))
    small_all = gathered[-1]
    small_shapes = [args[n].shape for n in sharded_small]
    per_dev = [_unpack(small_all[k, 0], small_shapes) for k in range(N_DEV)]
    conv_qk_f = jnp.concatenate([p[0] for p in per_dev], axis=-1)[0]
    g_hnorm_f = jnp.concatenate([p[1] for p in per_dev], axis=-1)[0]
    conv_ffn_f = jnp.concatenate([p[2] for p in per_dev], axis=-1)[0]

    w_in_f = gw['w_in'].transpose(1, 0, 2).reshape(d, -1)
    c_kpe, c_q, c_i, c_o = O_Q, O_Q + ROPE, O_Q + ROPE + 2 * ML_QK + ML_V, O_Q + ROPE + 2 * ML_QK + ML_V + 2 * ML_HEADS
    w_main = jnp.concatenate([w_in_f[:, :c_kpe], w_in_f[:, c_q:c_i], w_in_f[:, c_o:]], axis=1)[None]
    w_tail = jnp.concatenate([w_in_f[:, c_kpe:c_q], w_in_f[:, c_i:c_o],
                              jnp.zeros((d, TAIL - ROPE - 2 * ML_HEADS), bf16)], axis=1)[None]
    assert w_main.shape[2] == wmain
    qb = gw['w_qb'].transpose(1, 0, 2).reshape(Q_LORA, MLA_HEADS, NOPE + ROPE)
    w_qb_p = jnp.concatenate([qb, jnp.zeros((Q_LORA, MLA_HEADS, HEAD_PAD - NOPE - ROPE), bf16)], axis=2).reshape(1, Q_LORA, -1)
    w_kvb3 = gw['w_kvb']
    p_a3, p_b3, w_out3 = (gw[n].reshape(1, -1, d) for n in ('p_a', 'p_b', 'w_out'))
    wq_c3, wk_c3, wv_c3 = (gw[n].reshape(1, d, -1) for n in ('wq_c', 'wk_c', 'wv_c'))
    wo_c3, w_up3 = gw['wo_c'], gw['w_up']
    w_down3 = gw['w_down'].reshape(1, -1, d)

    inv_freq = ROPE_BASE ** (-jnp.arange(0, ROPE, 2, dtype=f32) / ROPE)
    inv_tile = _pad_lanes(jnp.concatenate([inv_freq, inv_freq])[None])
    cos, sin = rope_tables(positions.reshape(t, 1), inv_tile)
    gqp, gkp = _pad_lanes(g_qn_pe), _pad_lanes(g_kn_pe)
    b_tile = jnp.pad(b_if, ((0, 0), (T_I, TAIL - T_I - 2 * ML_HEADS)))

    u0 = rms_fwd(x2d, g_mix, "rms_mix")
    z_main = mm_nn(u0, w_main, f32, "mm_in_main")
    z_tail = mm_nn(u0, w_tail, f32, "mm_in_tail")
    qa_n, kv_n = lat_norm(z_main, g_qa, g_kva)
    q_raw = mm_nn(qa_n, w_qb_p, f32, "mm_qb")
    kv_raw = mm_nn(kv_n, w_kvb3, f32, "mm_kvb")
    qh, kh, vh = mla_prep(q_raw, kv_raw, z_tail, cos, sin, g_qn_nope, gqp, g_kn_nope, gkp)
    o_a, lse = mla_fwd(qh, kh, vh)

    qk_act = qk_conv(z_main, conv_qk_f)
    gates = gate_act(z_tail, b_tile)

    def to_rows(cols):
        return cols.T.reshape(ML_HEADS, nc, 1, CHUNK)

    ig, fg = to_rows(gates[:, T_I:T_F]), to_rows(gates[:, T_F:T_F + ML_HEADS])
    h_ml, c_all, n_all, m_all = mlstm_fwd(qk_act, z_main, ig, fg)
    g_hn3 = g_hnorm_f.reshape(ML_HEADS, 1, ML_DV)
    y_b = mlstm_out(h_ml, z_main, g_hn3)

    ya = mm_nn(o_a, p_a3, f32, "mm_pa")
    yb = mm_nn(y_b, p_b3, f32, "mm_pb")
    merged = merge_fwd(z_main, ya, yb)
    mo = mm_nn(merged, w_out3, f32, "mm_out")
    x1, uc = resid_rms(x2d, mo, g_cross, "resid_cross")
    mem_n = rms_fwd(mem2d, g_mem, "rms_mem")
    cq = mm_nn(uc, wq_c3, f32, "mm_cq")
    ck = mm_nn(mem_n, wk_c3, f32, "mm_ck")
    cv = mm_nn(mem_n, wv_c3, f32, "mm_cv")
    o_c = cross_fwd(cq, ck, cv, g_cq, g_ck)
    co = mm_nn(o_c, wo_c3, f32, "mm_oc")
    x2, u3 = resid_rms(x1, co, g_ffn, "resid_ffn")
    hup = mm_nn(u3, w_up3, f32, "mm_up")
    gl = glu_fwd(hup, conv_ffn_f, b_conv_ffn)
    fo = mm_nn(gl, w_down3, f32, "mm_down")
    dx3, loss_acc = loss_head(x2, fo, tgt)

    grads = {}
    grads['w_down'] = mm_tn(gl, dx3, 1, "mm_d_wdown").reshape(N_DEV, -1, d)
    dgl = mm_nt(dx3, w_down3, f32, "mm_d_gl")
    dhup, dconv_ffn, db_ffn = glu_bwd(hup, conv_ffn_f, b_conv_ffn, dgl)
    grads['w_up'] = mm_tn(u3, dhup, N_DEV, "mm_d_wup")
    du3 = mm_nt(dhup, w_up3, f32, "mm_d_u3")
    dx2, dg_ffn = rms_bwd(x2, g_ffn, [du3], dx3, "rms_bwd_ffn")
    grads['wo_c'] = mm_tn(o_c, dx2, N_DEV, "mm_d_woc")
    do_c = mm_nt(dx2, wo_c3, f32, "mm_d_oc")
    dcq, dck, dcv, dg_cq, dg_ck = cross_bwd(cq, ck, cv, g_cq, g_ck, do_c)
    grads['wq_c'] = mm_tn(uc, dcq, 1, "mm_d_wqc").reshape(N_DEV, -1, dcq.shape[1])
    grads['wk_c'] = mm_tn(mem_n, dck, 1, "mm_d_wkc").reshape(N_DEV, -1, dck.shape[1])
    grads['wv_c'] = mm_tn(mem_n, dcv, 1, "mm_d_wvc").reshape(N_DEV, -1, dcv.shape[1])
    duc = mm_nt(dcq, wq_c3, f32, "mm_d_uc")
    dmem_k = mm_nt(dck, wk_c3, f32, "mm_d_memk")
    dmem_v = mm_nt(dcv, wv_c3, f32, "mm_d_memv")
    dg_mem, = rms_bwd(mem2d, g_mem, [dmem_k, dmem_v], None, "rms_bwd_mem", want_dx=False)
    dx1, dg_cross = rms_bwd(x1, g_cross, [duc], dx2, "rms_bwd_cross")
    grads['w_out'] = mm_tn(merged, dx1, 1, "mm_d_wout").reshape(N_DEV, -1, d)
    dmerged = mm_nt(dx1, w_out3, f32, "mm_d_merged")
    dga, dgb, dya, dyb = merge_bwd(z_main, ya, yb, dmerged)
    grads['p_a'] = mm_tn(o_a, dya, 1, "mm_d_pa").reshape(N_DEV, -1, d)
    grads['p_b'] = mm_tn(y_b, dyb, 1, "mm_d_pb").reshape(N_DEV, -1, d)
    do_a = mm_nt(dya, p_a3, f32, "mm_d_oa")
    dy_b = mm_nt(dyb, p_b3, f32, "mm_d_yb")

    dh_ml, dzo, dg_hn = mlstm_out_bwd(h_ml, z_main, g_hn3, dy_b)
    dq_act, dk_act, dzv, dig, dfg = mlstm_bwd(qk_act, z_main, ig, fg, c_all, n_all, m_all, dh_ml)
    dqk = jnp.concatenate([dq_act, dk_act], axis=1)
    dzqk, dconv_qk = qk_conv_bwd(z_main, conv_qk_f, dqk)

    delta = mla_delta(o_a, do_a)
    dqh = mla_bwd_dq(qh, kh, vh, do_a, lse, delta)
    dkh, dvh = mla_bwd_dkv(qh, kh, vh, do_a, lse, delta)
    dq_raw, dkv_raw, dzt_pe, dg_qn, dg_qp, dg_kn, dg_kp = mla_prep_bwd(
        q_raw, kv_raw, z_tail, cos, sin, g_qn_nope, gqp, g_kn_nope, gkp, dqh, dkh, dvh)
    d_wqb_p = mm_tn(qa_n, dq_raw, 1, "mm_d_wqb")[0].reshape(Q_LORA, MLA_HEADS, HEAD_PAD)[:, :, :NOPE + ROPE]
    grads['w_qb'] = d_wqb_p.reshape(Q_LORA, N_DEV, -1).transpose(1, 0, 2)
    grads['w_kvb'] = mm_tn(kv_n, dkv_raw, N_DEV, "mm_d_wkvb")
    dqa = mm_nt(dq_raw, w_qb_p, f32, "mm_d_qa")
    dkvn = mm_nt(dkv_raw, w_kvb3, f32, "mm_d_kvn")
    dz_lat, dg_qa, dg_kva = lat_norm_bwd(z_main, g_qa, g_kva, dqa, dkvn)

    def to_cols(rows):
        return rows.reshape(ML_HEADS, t).T

    dgate = jnp.pad(jnp.concatenate([to_cols(dig), to_cols(dfg)], axis=1), ((0, 0), (T_I, TAIL - T_I - 2 * ML_HEADS)))
    dz_tail, db_if = tail_bwd(z_tail, b_tile, dzt_pe, dgate)
    dz_main = jnp.concatenate([dz_lat, dzqk, dzv, dzo, dga, dgb], axis=1)
    d_wmain = mm_tn(u0, dz_main, 1, "mm_d_wmain")[0]
    d_wtail = mm_tn(u0, dz_tail, 1, "mm_d_wtail")[0]
    du0_a = mm_nt(dz_main, w_main, f32, "mm_d_u0_main")
    du0_b = mm_nt(dz_tail, w_tail, f32, "mm_d_u0_tail")
    grad_x, dg_mix = rms_bwd(x2d, g_mix, [du0_a, du0_b], dx1, "rms_bwd_mix")
    d_win = jnp.concatenate([d_wmain[:, :O_Q], d_wtail[:, :ROPE], d_wmain[:, O_Q:O_O], d_wtail[:, T_I:T_I + 2 * ML_HEADS],
                             d_wmain[:, O_O:]], axis=1)
    grads['w_in'] = d_win.reshape(d, N_DEV, -1).transpose(1, 0, 2)

    parts = _exchange([grads[n] for n in big], "comm_scatter_grads", scatter=True)
    out_g, out_d, out_m, out_v = {}, {}, {}, {}
    for n, p in zip(big, parts):
        shp = args[n].shape
        g, dl, m2, v2 = adam_sum(p, args[n][0], args['m_' + n][0], args['v_' + n][0], "adam_" + n)
        out_g[n], out_d[n], out_m[n], out_v[n] = (a.reshape(shp) for a in (g, dl, m2, v2))

    small_full = {
        'g_mix': dg_mix[0], 'g_qa': dg_qa[0], 'g_kva': dg_kva[0], 'g_qn_nope': dg_qn[0], 'g_qn_pe': dg_qp[0, :ROPE],
        'g_kn_nope': dg_kn[0], 'g_kn_pe': dg_kp[0, :ROPE], 'conv_qk': dconv_qk, 'b_if': db_if[0, T_I:T_I + 2 * ML_HEADS],
        'g_hnorm': dg_hn[:, 0, :], 'g_cross': dg_cross[0], 'g_mem': dg_mem[0], 'g_cq': dg_cq[0], 'g_ck': dg_ck[0],
        'g_ffn': dg_ffn[0], 'conv_ffn': dconv_ffn, 'b_conv_ffn': db_ffn[0], 'loss': loss_acc[0, :1]}
    order = list(small_full)
    packed = _pack([small_full[n] for n in order], 8 * 128).reshape(1, -1)
    gathered_small, = _exchange([packed], "comm_gather_small", scatter=False)
    summed = sum_parts(gathered_small.reshape(N_DEV, -1, 128), "sum_small").reshape(-1)
    full_g = dict(zip(order, _unpack(summed, [small_full[n].shape for n in order])))
    loss = full_g['loss'][0]

    local_g = {}
    for n in replicated:
        local_g[n] = full_g[n].reshape(args[n].shape)
    for n in sharded_small:
        shp = args[n].shape
        full = full_g[n].reshape((1,) + full_g[n].shape)
        local_g[n] = lax.dynamic_slice_in_dim(full, me * shp[-1], shp[-1], axis=2)
    small = replicated + sharded_small
    dl_f, m_f, v_f = adam_flat(*[_pack([src[n] if pre == '' else args[pre + n] for n in small], 8 * 128).reshape(-1, 128)
                                 for pre, src in (('', args), ('', local_g), ('m_', None), ('v_', None))], "adam_small")
    shapes = [args[n].shape for n in small]
    for dst, flat in ((out_d, dl_f), (out_m, m_f), (out_v, v_f)):
        dst.update(zip(small, _unpack(flat.reshape(-1), shapes)))
    out_g.update(local_g)

    return (loss, grad_x[None], *[out_g[n] for n in names], *[out_d[n] for n in names],
            *[out_m[n] for n in names], *[out_v[n] for n in names])
```

```python
import functools

import jax
import jax.numpy as jnp
from jax import lax
from jax.experimental import pallas as pl
from jax.experimental.pallas import tpu as pltpu

f32 = jnp.float32
bf16 = jnp.bfloat16

N_DEV = 8
EPS = 1e-6
CHUNK = 64
CHUNK_SHIFT = 6
assert 1 << CHUNK_SHIFT == CHUNK
MLA_HEADS = 16
Q_LORA = 512
KV_LORA = 512
NOPE = 128
ROPE = 64
V_HEAD = 128
ROPE_BASE = 10000.0
HEAD_PAD = 256
ML_HEADS = 8
ML_DK = 128
ML_DV = 256
ML_CONV = 4
ML_QK = ML_HEADS * ML_DK
ML_V = ML_HEADS * ML_DV
CR_HEADS = 4
CR_HD = 128
FFN_CONV = 3
ADAM_LR = 0.001
ADAM_B1 = 0.9
ADAM_B2 = 0.999
ADAM_EPS = 1e-08
ADAM_WD = 0.01
ADAM_STEP = 10
O_QA, O_KV, O_Q, O_K = 0, Q_LORA, Q_LORA + KV_LORA, Q_LORA + KV_LORA + ML_QK
O_V = O_K + ML_QK
O_O = O_V + ML_V
O_GA = O_O + ML_V
TAIL = 128
T_I, T_F = ROPE, ROPE + ML_HEADS
VMEM_LIMIT_V7X = 48 * 1024 * 1024
MESH = pl.DeviceIdType.MESH


def _call(body, name, grid, in_specs, out_specs, out_shape, scratch=()):
    return pl.pallas_call(
        body, name=name, grid=grid, in_specs=in_specs, out_specs=out_specs, out_shape=out_shape,
        scratch_shapes=list(scratch), compiler_params=pltpu.CompilerParams(vmem_limit_bytes=VMEM_LIMIT_V7X))


def _tile(n, cands):
    for c in cands:
        if n % c == 0:
            return c
    return n


def _sds(shape, dtype):
    return jax.ShapeDtypeStruct(tuple(shape), dtype)


def _bdot(a, b, ca, cb):
    return lax.dot_general(a.astype(bf16), b.astype(bf16), (((ca,), (cb,)), ((), ())), preferred_element_type=f32)


_BIG = (1024, 512, 256, 128)


def _col_tile(nb):
    return nb if nb <= 1536 else _tile(nb, _BIG)


def mm_nn(a, w3, out_dtype, name):
    m, k = a.shape
    nblk, k2, nb = w3.shape
    assert k == k2
    tm, tk, tn = _tile(m, _BIG), _tile(k, (512, 256, 128)), _col_tile(nb)
    per, nk = nb // tn, k // tk

    def body(a_ref, w_ref, o_ref, acc):
        kk = pl.program_id(2)

        @pl.when(kk == 0)
        def _():
            acc[...] = jnp.zeros_like(acc)

        acc[...] += _bdot(a_ref[...], w_ref[0], 1, 0)

        @pl.when(kk == nk - 1)
        def _():
            o_ref[...] = acc[...].astype(o_ref.dtype)

    return _call(body, name, (m // tm, nblk * per, nk),
                 [pl.BlockSpec((tm, tk), lambda i, j, kk: (i, kk)),
                  pl.BlockSpec((1, tk, tn), lambda i, j, kk: (j // per, kk, j % per))],
                 pl.BlockSpec((tm, tn), lambda i, j, kk: (i, j)), _sds((m, nblk * nb), out_dtype),
                 [pltpu.VMEM((tm, tn), f32)])(a, w3)


def mm_nt(a, w3, out_dtype, name):
    m, n = a.shape
    nblk, k, nb = w3.shape
    assert n == nblk * nb
    tm, tn = _tile(m, _BIG), _tile(k, _BIG)
    tc = nb if nb <= 1536 else _tile(nb, (512, 256, 128))
    per = nb // tc
    nk = nblk * per

    def body(a_ref, w_ref, o_ref, acc):
        kk = pl.program_id(2)

        @pl.when(kk == 0)
        def _():
            acc[...] = jnp.zeros_like(acc)

        acc[...] += _bdot(a_ref[...], w_ref[0], 1, 1)

        @pl.when(kk == nk - 1)
        def _():
            o_ref[...] = acc[...].astype(o_ref.dtype)

    return _call(body, name, (m // tm, k // tn, nk),
                 [pl.BlockSpec((tm, tc), lambda i, j, kk: (i, kk)),
                  pl.BlockSpec((1, tn, tc), lambda i, j, kk: (kk // per, j, kk % per))],
                 pl.BlockSpec((tm, tn), lambda i, j, kk: (i, j)), _sds((m, k), out_dtype),
                 [pltpu.VMEM((tm, tn), f32)])(a, w3)


def mm_tn(a, b, nblk, name):
    r, m = a.shape
    r2, n = b.shape
    assert r == r2 and n % nblk == 0
    nb = n // nblk
    tm, tk, tn = _tile(m, _BIG), _tile(r, (512, 256, 128)), _col_tile(nb)
    per, nk = nb // tn, r // tk

    def body(a_ref, b_ref, o_ref, acc):
        kk = pl.program_id(2)

        @pl.when(kk == 0)
        def _():
            acc[...] = jnp.zeros_like(acc)

        acc[...] += _bdot(a_ref[...], b_ref[...], 0, 0)

        @pl.when(kk == nk - 1)
        def _():
            o_ref[0] = acc[...].astype(bf16)

    return _call(body, name, (m // tm, nblk * per, nk),
                 [pl.BlockSpec((tk, tm), lambda i, j, kk: (kk, i)),
                  pl.BlockSpec((tk, tn), lambda i, j, kk: (kk, j))],
                 pl.BlockSpec((1, tm, tn), lambda i, j, kk: (j // per, i, j % per)), _sds((nblk, m, nb), bf16),
                 [pltpu.VMEM((tm, tn), f32)])(a, b)


def _rms(x, g):
    return x * lax.rsqrt(jnp.mean(x * x, axis=-1, keepdims=True) + EPS) * g


def _rms_pad(x, g, width):
    return x * lax.rsqrt(jnp.sum(x * x, axis=-1, keepdims=True) / width + EPS) * g


def _first(*ids):
    ok = ids[0] == 0
    for i in ids[1:]:
        ok = jnp.logical_and(ok, i == 0)
    return ok


def _acc_row(ref, val, first):
    @pl.when(first)
    def _():
        ref[...] = jnp.zeros_like(ref)

    ref[0:1, :] += val


def rms_fwd(x, g, name):
    r, w = x.shape
    tm = _tile(r, (256, 128, 64, 32, 16, 8))

    def body(x_ref, g_ref, o_ref):
        o_ref[...] = _rms(x_ref[...], g_ref[...]).astype(bf16)

    return _call(body, name, (r // tm,), [pl.BlockSpec((tm, w), lambda i: (i, 0)), pl.BlockSpec((1, w), lambda i: (0, 0))],
                 pl.BlockSpec((tm, w), lambda i: (i, 0)), _sds((r, w), bf16))(x, g)


def resid_rms(xa, xb, g, name):
    r, w = xa.shape
    tm = _tile(r, (256, 128, 64, 32, 16, 8))

    def body(a_ref, b_ref, g_ref, s_ref, u_ref):
        xs = a_ref[...] + b_ref[...]
        s_ref[...] = xs
        u_ref[...] = _rms(xs, g_ref[...]).astype(bf16)

    row = pl.BlockSpec((tm, w), lambda i: (i, 0))
    return _call(body, name, (r // tm,), [row, row, pl.BlockSpec((1, w), lambda i: (0, 0))], [row, row],
                 [_sds((r, w), f32), _sds((r, w), bf16)])(xa, xb, g)


def rms_bwd(x, g, dys, dres, name, want_dx=True):
    r, w = x.shape
    tm = _tile(r, (256, 128, 64, 32, 16, 8))
    nd = len(dys)

    def body(*refs):
        x_ref, g_ref = refs[0], refs[1]
        dy = refs[2][...]
        for j in range(1, nd):
            dy = dy + refs[2 + j][...]
        pos = 2 + nd
        _, vjp = jax.vjp(_rms, x_ref[...], g_ref[...])
        dx, dg = vjp(dy)
        if dres is not None:
            dx = dx + refs[pos][...]
            pos += 1
        if want_dx:
            refs[pos][...] = dx
            pos += 1
        _acc_row(refs[pos], dg, pl.program_id(0) == 0)

    row = pl.BlockSpec((tm, w), lambda i: (i, 0))
    ins = [x, g] + list(dys) + ([dres] if dres is not None else [])
    in_specs = [row, pl.BlockSpec((1, w), lambda i: (0, 0))] + [row] * (nd + (dres is not None))
    out_specs = ([row] if want_dx else []) + [pl.BlockSpec((8, w), lambda i: (0, 0))]
    out_shape = ([_sds((r, w), f32)] if want_dx else []) + [_sds((8, w), f32)]
    return _call(body, name, (r // tm,), in_specs, out_specs, out_shape)(*ins)


def lat_norm(z_main, g_qa, g_kva):
    t = z_main.shape[0]
    tm = _tile(t, (512, 256, 128, 64))

    def body(z_ref, gq_ref, gk_ref, q_ref, k_ref):
        q_ref[...] = _rms(z_ref[:, :Q_LORA], gq_ref[...]).astype(bf16)
        k_ref[...] = _rms(z_ref[:, Q_LORA:], gk_ref[...]).astype(bf16)

    return _call(body, "lat_norm", (t // tm,),
                 [pl.BlockSpec((tm, Q_LORA + KV_LORA), lambda i: (i, 0)), pl.BlockSpec((1, Q_LORA), lambda i: (0, 0)),
                  pl.BlockSpec((1, KV_LORA), lambda i: (0, 0))],
                 [pl.BlockSpec((tm, Q_LORA), lambda i: (i, 0)), pl.BlockSpec((tm, KV_LORA), lambda i: (i, 0))],
                 [_sds((t, Q_LORA), bf16), _sds((t, KV_LORA), bf16)])(z_main, g_qa, g_kva)


def lat_norm_bwd(z_main, g_qa, g_kva, dqa, dkv):
    t = z_main.shape[0]
    tm = _tile(t, (512, 256, 128, 64))

    def body(z_ref, gq_ref, gk_ref, dq_ref, dk_ref, dz_ref, dgq_ref, dgk_ref):
        first = pl.program_id(0) == 0
        _, vq = jax.vjp(_rms, z_ref[:, :Q_LORA], gq_ref[...])
        dx, dg = vq(dq_ref[...])
        dz_ref[:, :Q_LORA] = dx.astype(bf16)
        _acc_row(dgq_ref, dg, first)
        _, vk = jax.vjp(_rms, z_ref[:, Q_LORA:], gk_ref[...])
        dx, dg = vk(dk_ref[...])
        dz_ref[:, Q_LORA:] = dx.astype(bf16)
        _acc_row(dgk_ref, dg, first)

    return _call(body, "lat_norm_bwd", (t // tm,),
                 [pl.BlockSpec((tm, Q_LORA + KV_LORA), lambda i: (i, 0)), pl.BlockSpec((1, Q_LORA), lambda i: (0, 0)),
                  pl.BlockSpec((1, KV_LORA), lambda i: (0, 0)), pl.BlockSpec((tm, Q_LORA), lambda i: (i, 0)),
                  pl.BlockSpec((tm, KV_LORA), lambda i: (i, 0))],
                 [pl.BlockSpec((tm, Q_LORA + KV_LORA), lambda i: (i, 0)), pl.BlockSpec((8, Q_LORA), lambda i: (0, 0)),
                  pl.BlockSpec((8, KV_LORA), lambda i: (0, 0))],
                 [_sds((t, Q_LORA + KV_LORA), bf16), _sds((8, Q_LORA), f32), _sds((8, KV_LORA), f32)])(z_main, g_qa, g_kva, dqa, dkv)


def rope_tables(pos_col, inv_freq):
    t = pos_col.shape[0]
    tm = _tile(t, (512, 256, 128, 64))

    def body(p_ref, f_ref, c_ref, s_ref):
        ang = p_ref[...].astype(f32) * f_ref[...]
        lane = lax.broadcasted_iota(jnp.int32, ang.shape, 1)
        c_ref[...] = jnp.where(lane < ROPE, jnp.cos(ang), 0.0)
        sn = jnp.sin(ang)
        s_ref[...] = jnp.where(lane < ROPE // 2, -sn, jnp.where(lane < ROPE, sn, 0.0))

    return _call(body, "rope_tables", (t // tm,),
                 [pl.BlockSpec((tm, 1), lambda i: (i, 0)), pl.BlockSpec((1, TAIL), lambda i: (0, 0))],
                 [pl.BlockSpec((tm, TAIL), lambda i: (i, 0))] * 2, [_sds((t, TAIL), f32)] * 2)(pos_col, inv_freq)


def _swap_halves(n):
    lane = lax.broadcasted_iota(jnp.int32, n.shape, 1)
    return jnp.where(lane < ROPE // 2, pltpu.roll(n, TAIL - ROPE // 2, 1), pltpu.roll(n, ROPE // 2, 1))


def _rope(n, c, s):
    return n * c + _swap_halves(n) * s


def _rope_t(d, c, s):
    return d * c + _swap_halves(d * s)


def _prep_specs(tm):
    head = pl.BlockSpec((tm, HEAD_PAD), lambda i, h: (i, h))
    row = pl.BlockSpec((tm, TAIL), lambda i, h: (i, 0))
    gain = pl.BlockSpec((1, TAIL), lambda i, h: (0, 0))
    return head, row, gain


def _pe_in(zt):
    lane = lax.broadcasted_iota(jnp.int32, zt.shape, 1)
    return jnp.where(lane < ROPE, zt, 0.0)


def mla_prep(q_raw, kv_raw, z_tail, cos, sin, gqn, gqp, gkn, gkp):
    t = q_raw.shape[0]
    tm = _tile(t, (512, 256, 128, 64))

    def body(q_ref, kv_ref, zt_ref, c_ref, s_ref, gqn_ref, gqp_ref, gkn_ref, gkp_ref, qh_ref, kh_ref, vh_ref):
        c, s = c_ref[...], s_ref[...]
        qh_ref[:, :NOPE] = _rms(q_ref[:, :NOPE], gqn_ref[...]).astype(bf16)
        qh_ref[:, NOPE:] = _rope(_rms_pad(q_ref[:, NOPE:], gqp_ref[...], ROPE), c, s).astype(bf16)
        kh_ref[:, :NOPE] = _rms(kv_ref[:, :NOPE], gkn_ref[...]).astype(bf16)
        kh_ref[:, NOPE:] = _rope(_rms_pad(_pe_in(zt_ref[...]), gkp_ref[...], ROPE), c, s).astype(bf16)
        vh_ref[...] = kv_ref[:, NOPE:].astype(bf16)

    head, row, gain = _prep_specs(tm)
    return _call(body, "mla_prep", (t // tm, MLA_HEADS), [head, head, row, row, row, gain, gain, gain, gain],
                 [head, head, pl.BlockSpec((tm, V_HEAD), lambda i, h: (i, h))],
                 [_sds((t, MLA_HEADS * HEAD_PAD), bf16), _sds((t, MLA_HEADS * HEAD_PAD), bf16), _sds((t, MLA_HEADS * V_HEAD), bf16)],
                 )(q_raw, kv_raw, z_tail, cos, sin, gqn, gqp, gkn, gkp)


def mla_prep_bwd(q_raw, kv_raw, z_tail, cos, sin, gqn, gqp, gkn, gkp, dqh, dkh, dvh):
    t = q_raw.shape[0]
    tm = _tile(t, (512, 256, 128, 64))
    pad_norm = functools.partial(_rms_pad, width=ROPE)

    def body(q_ref, kv_ref, zt_ref, c_ref, s_ref, gqn_ref, gqp_ref, gkn_ref, gkp_ref, dqh_ref, dkh_ref, dvh_ref,
             dq_ref, dkv_ref, dzt_ref, dgqn_ref, dgqp_ref, dgkn_ref, dgkp_ref):
        i, h = pl.program_id(0), pl.program_id(1)
        first = _first(i, h)
        c, s = c_ref[...], s_ref[...]
        _, v1 = jax.vjp(_rms, q_ref[:, :NOPE], gqn_ref[...])
        dx, dg = v1(dqh_ref[:, :NOPE])
        dq_ref[:, :NOPE] = dx.astype(bf16)
        _acc_row(dgqn_ref, dg, first)
        _, v2 = jax.vjp(pad_norm, q_ref[:, NOPE:], gqp_ref[...])
        dx, dg = v2(_rope_t(dqh_ref[:, NOPE:], c, s))
        dq_ref[:, NOPE:] = dx.astype(bf16)
        _acc_row(dgqp_ref, dg, first)
        _, v3 = jax.vjp(_rms, kv_ref[:, :NOPE], gkn_ref[...])
        dx, dg = v3(dkh_ref[:, :NOPE])
        dkv_ref[:, :NOPE] = dx.astype(bf16)
        _acc_row(dgkn_ref, dg, first)
        dkv_ref[:, NOPE:] = dvh_ref[...].astype(bf16)
        _, v4 = jax.vjp(pad_norm, _pe_in(zt_ref[...]), gkp_ref[...])
        dx, dg = v4(_rope_t(dkh_ref[:, NOPE:], c, s))
        _acc_row(dgkp_ref, dg, first)

        @pl.when(h == 0)
        def _():
            dzt_ref[...] = jnp.zeros_like(dzt_ref)

        dzt_ref[...] += dx

    head, row, gain = _prep_specs(tm)
    acc = pl.BlockSpec((8, TAIL), lambda i, h: (0, 0))
    vspec = pl.BlockSpec((tm, V_HEAD), lambda i, h: (i, h))
    return _call(body, "mla_prep_bwd", (t // tm, MLA_HEADS),
                 [head, head, row, row, row, gain, gain, gain, gain, head, head, vspec],
                 [head, head, row, acc, acc, acc, acc],
                 [_sds((t, MLA_HEADS * HEAD_PAD), bf16), _sds((t, MLA_HEADS * HEAD_PAD), bf16), _sds((t, TAIL), f32)]
                 + [_sds((8, TAIL), f32)] * 4)(q_raw, kv_raw, z_tail, cos, sin, gqn, gqp, gkn, gkp, dqh, dkh, dvh)


ATT_BLOCK = 512
NEG = -1e30
ATT_SCALE = (NOPE + ROPE) ** -0.5


def _chunk_visible(shape, key_axis):
    kc = lax.broadcasted_iota(jnp.int32, shape, key_axis) >> CHUNK_SHIFT
    qc = lax.broadcasted_iota(jnp.int32, shape, 1 - key_axis) >> CHUNK_SHIFT
    return kc <= qc


def mla_fwd(qh, kh, vh):
    t = qh.shape[0]
    tb = min(ATT_BLOCK, t)
    nb = t // tb

    def body(q_ref, k_ref, v_ref, o_ref, lse_ref, m_s, l_s, acc):
        qi, ki = pl.program_id(1), pl.program_id(2)

        @pl.when(ki == 0)
        def _():
            m_s[...] = jnp.full_like(m_s, NEG)
            l_s[...] = jnp.zeros_like(l_s)
            acc[...] = jnp.zeros_like(acc)

        def step(diagonal):
            s = _bdot(q_ref[...], k_ref[...], 1, 1) * ATT_SCALE
            if diagonal:
                s = jnp.where(_chunk_visible(s.shape, 1), s, -jnp.inf)
            m_new = jnp.maximum(m_s[...], jnp.max(s, axis=1, keepdims=True))
            p = jnp.exp(s - m_new)
            alpha = jnp.exp(m_s[...] - m_new)
            l_s[...] = alpha * l_s[...] + jnp.sum(p, axis=1, keepdims=True)
            acc[...] = alpha * acc[...] + _bdot(p, v_ref[...], 1, 0)
            m_s[...] = m_new

        @pl.when(ki < qi)
        def _():
            step(False)

        @pl.when(ki == qi)
        def _():
            step(True)
            o_ref[...] = acc[...] / l_s[...]
            lse_ref[0] = m_s[...] + jnp.log(l_s[...])

    kv = lambda h, qi, ki: (jnp.minimum(ki, qi), h)
    return _call(body, "mla_fwd", (MLA_HEADS, nb, nb),
                 [pl.BlockSpec((tb, HEAD_PAD), lambda h, qi, ki: (qi, h)), pl.BlockSpec((tb, HEAD_PAD), kv),
                  pl.BlockSpec((tb, V_HEAD), kv)],
                 [pl.BlockSpec((tb, V_HEAD), lambda h, qi, ki: (qi, h)), pl.BlockSpec((1, tb, 1), lambda h, qi, ki: (h, qi, 0))],
                 [_sds((t, MLA_HEADS * V_HEAD), f32), _sds((MLA_HEADS, t, 1), f32)],
                 [pltpu.VMEM((tb, 1), f32), pltpu.VMEM((tb, 1), f32), pltpu.VMEM((tb, V_HEAD), f32)])(qh, kh, vh)


def mla_delta(o, do):
    t = o.shape[0]
    tm = _tile(t, (512, 256, 128, 64))

    def body(o_ref, do_ref, d_ref):
        d_ref[0] = jnp.sum(o_ref[...] * do_ref[...], axis=1, keepdims=True)

    blk = pl.BlockSpec((tm, V_HEAD), lambda i, h: (i, h))
    return _call(body, "mla_delta", (t // tm, MLA_HEADS), [blk, blk], pl.BlockSpec((1, tm, 1), lambda i, h: (h, i, 0)),
                 _sds((MLA_HEADS, t, 1), f32))(o, do)


def mla_bwd(qh, kh, vh, do, lse_row, delta_row):
    t = qh.shape[0]
    tb = min(ATT_BLOCK, t)
    nb = t // tb

    def body(q_ref, k_ref, v_ref, do_ref, lse_ref, dl_ref, dq_ref, dk_ref, dv_ref, dk_acc, dv_acc):
        ki, qi = pl.program_id(1), pl.program_id(2)

        @pl.when(jnp.logical_and(ki == 0, qi == 0))
        def _():
            dq_ref[...] = jnp.zeros_like(dq_ref)

        @pl.when(qi == 0)
        def _():
            dk_acc[...] = jnp.zeros_like(dk_acc)
            dv_acc[...] = jnp.zeros_like(dv_acc)

        def step(diagonal):
            q, k, do_b = q_ref[...], k_ref[...], do_ref[...]
            s = _bdot(k, q, 1, 1) * ATT_SCALE
            if diagonal:
                s = jnp.where(_chunk_visible(s.shape, 0), s, -jnp.inf)
            p = jnp.exp(s - lse_ref[0])
            dp = _bdot(v_ref[...], do_b, 1, 1)
            ds = p * (dp - dl_ref[0]) * ATT_SCALE
            dv_acc[...] += _bdot(p, do_b, 1, 0)
            dk_acc[...] += _bdot(ds, q, 1, 0)
            rows = pl.ds(pl.multiple_of(qi * tb, tb), tb)
            dq_ref[rows, :] += _bdot(ds, k, 0, 0)

        @pl.when(qi > ki)
        def _():
            step(False)

        @pl.when(qi == ki)
        def _():
            step(True)

        @pl.when(qi == nb - 1)
        def _():
            dk_ref[...] = dk_acc[...]
            dv_ref[...] = dv_acc[...]

    qs = lambda h, ki, qi: (jnp.maximum(qi, ki), h)
    ks = lambda h, ki, qi: (ki, h)
    vec = pl.BlockSpec((1, 1, tb), lambda h, ki, qi: (h, 0, jnp.maximum(qi, ki)))
    return _call(body, "mla_bwd", (MLA_HEADS, nb, nb),
                 [pl.BlockSpec((tb, HEAD_PAD), qs), pl.BlockSpec((tb, HEAD_PAD), ks), pl.BlockSpec((tb, V_HEAD), ks),
                  pl.BlockSpec((tb, V_HEAD), qs), vec, vec],
                 [pl.BlockSpec((t, HEAD_PAD), lambda h, ki, qi: (0, h)), pl.BlockSpec((tb, HEAD_PAD), ks),
                  pl.BlockSpec((tb, V_HEAD), ks)],
                 [_sds((t, MLA_HEADS * HEAD_PAD), f32), _sds((t, MLA_HEADS * HEAD_PAD), f32), _sds((t, MLA_HEADS * V_HEAD), f32)],
                 [pltpu.VMEM((tb, HEAD_PAD), f32), pltpu.VMEM((tb, V_HEAD), f32)])(qh, kh, vh, do, lse_row, delta_row)


PAD = 8


def _conv_taps(pad_ref, w, width, t):
    y = pad_ref[PAD - width + 1:PAD - width + 1 + t, :] * w[0:1, :]
    for j in range(1, width):
        y = y + pad_ref[PAD - width + 1 + j:PAD - width + 1 + j + t, :] * w[j:j + 1, :]
    return y


def _conv_bwd(xpad_ref, dpad_ref, w, da, width, t):
    dpad_ref[0:t, :] = da
    dpad_ref[t:t + PAD, :] = jnp.zeros((PAD, da.shape[1]), f32)
    dx = dpad_ref[width - 1:width - 1 + t, :] * w[0:1, :]
    for j in range(1, width):
        dx = dx + dpad_ref[width - 1 - j:width - 1 - j + t, :] * w[j:j + 1, :]
    dws = [jnp.sum(da * xpad_ref[PAD - width + 1 + j:PAD - width + 1 + j + t, :], axis=0, keepdims=True) for j in range(width)]
    return dx, dws


def _load_pad(pad_ref, x, t):
    pad_ref[0:PAD, :] = jnp.zeros((PAD, x.shape[1]), f32)
    pad_ref[PAD:PAD + t, :] = x


def qk_conv(z_main, conv_qk):
    t = z_main.shape[0]
    nq = ML_QK // 128
    base = O_Q // 128

    def body(z_ref, w_ref, o_ref, pad):
        _load_pad(pad, z_ref[...], t)
        a = _conv_taps(pad, w_ref[...], ML_CONV, t)
        sc = jnp.where(pl.program_id(0) < nq, ML_DK ** -0.5, 1.0)
        o_ref[...] = jax.nn.silu(a) * sc

    return _call(body, "qk_conv", (2 * nq,),
                 [pl.BlockSpec((t, 128), lambda j: (0, base + j)), pl.BlockSpec((ML_CONV, 128), lambda j: (0, j))],
                 pl.BlockSpec((t, 128), lambda j: (0, j)), _sds((t, 2 * ML_QK), f32),
                 [pltpu.VMEM((t + PAD, 128), f32)])(z_main, conv_qk)


def qk_conv_bwd(z_main, conv_qk, dqk):
    t = z_main.shape[0]
    nq = ML_QK // 128
    base = O_Q // 128

    def body(z_ref, w_ref, d_ref, dz_ref, dw_ref, pad, dpad):
        _load_pad(pad, z_ref[...], t)
        w = w_ref[...]
        a = _conv_taps(pad, w, ML_CONV, t)
        sc = jnp.where(pl.program_id(0) < nq, ML_DK ** -0.5, 1.0)
        _, vjp = jax.vjp(jax.nn.silu, a)
        da, = vjp(d_ref[...] * sc)
        dx, dws = _conv_bwd(pad, dpad, w, da, ML_CONV, t)
        dz_ref[...] = dx.astype(bf16)
        for j in range(ML_CONV):
            dw_ref[j:j + 1, :] = dws[j]

    return _call(body, "qk_conv_bwd", (2 * nq,),
                 [pl.BlockSpec((t, 128), lambda j: (0, base + j)), pl.BlockSpec((ML_CONV, 128), lambda j: (0, j)),
                  pl.BlockSpec((t, 128), lambda j: (0, j))],
                 [pl.BlockSpec((t, 128), lambda j: (0, j)), pl.BlockSpec((ML_CONV, 128), lambda j: (0, j))],
                 [_sds((t, 2 * ML_QK), bf16), _sds((ML_CONV, 2 * ML_QK), f32)],
                 [pltpu.VMEM((t + PAD, 128), f32), pltpu.VMEM((t + PAD, 128), f32)])(z_main, conv_qk, dqk)


def glu_fwd(hup, conv_w, bias):
    t, f2 = hup.shape
    nf = f2 // 2 // 128

    def body(h1_ref, h2_ref, w1_ref, w2_ref, b1_ref, b2_ref, o_ref, pad):
        _load_pad(pad, h1_ref[...], t)
        a1 = _conv_taps(pad, w1_ref[...], FFN_CONV, t) + b1_ref[...]
        _load_pad(pad, h2_ref[...], t)
        a2 = _conv_taps(pad, w2_ref[...], FFN_CONV, t) + b2_ref[...]
        o_ref[...] = (jax.nn.silu(a1) * a2).astype(bf16)

    col = lambda off: pl.BlockSpec((t, 128), lambda j: (0, j + off))
    wsp = lambda off: pl.BlockSpec((FFN_CONV, 128), lambda j: (0, j + off))
    bsp = lambda off: pl.BlockSpec((1, 128), lambda j: (0, j + off))
    return _call(body, "glu_fwd", (nf,), [col(0), col(nf), wsp(0), wsp(nf), bsp(0), bsp(nf)], col(0), _sds((t, f2 // 2), bf16),
                 [pltpu.VMEM((t + PAD, 128), f32)])(hup, hup, conv_w, conv_w, bias, bias)


def glu_bwd(hup, conv_w, bias, dg):
    t, f2 = hup.shape
    nf = f2 // 2 // 128

    def body(hs_ref, hp_ref, ws_ref, wp_ref, bs_ref, bp_ref, dg_ref, dh_ref, dw_ref, db_ref, pad, ppad, dpad):
        _load_pad(pad, hs_ref[...], t)
        w = ws_ref[...]
        a_self = _conv_taps(pad, w, FFN_CONV, t) + bs_ref[...]
        _load_pad(ppad, hp_ref[...], t)
        a_part = _conv_taps(ppad, wp_ref[...], FFN_CONV, t) + bp_ref[...]
        d = dg_ref[...]
        _, vjp = jax.vjp(jax.nn.silu, a_self)
        d_first, = vjp(d * a_part)
        d_second = d * jax.nn.silu(a_part)
        da = jnp.where(pl.program_id(0) < nf, d_first, d_second)
        dx, dws = _conv_bwd(pad, dpad, w, da, FFN_CONV, t)
        dh_ref[...] = dx.astype(bf16)
        for j in range(FFN_CONV):
            dw_ref[j:j + 1, :] = dws[j]
        db_ref[...] = jnp.sum(da, axis=0, keepdims=True)

    part = lambda j: (j + nf) % (2 * nf)
    col = pl.BlockSpec((t, 128), lambda j: (0, j))
    pcol = pl.BlockSpec((t, 128), lambda j: (0, part(j)))
    wsp = pl.BlockSpec((FFN_CONV, 128), lambda j: (0, j))
    pwsp = pl.BlockSpec((FFN_CONV, 128), lambda j: (0, part(j)))
    bsp = pl.BlockSpec((1, 128), lambda j: (0, j))
    pbsp = pl.BlockSpec((1, 128), lambda j: (0, part(j)))
    return _call(body, "glu_bwd", (2 * nf,), [col, pcol, wsp, pwsp, bsp, pbsp, pl.BlockSpec((t, 128), lambda j: (0, j % nf))],
                 [col, wsp, bsp], [_sds((t, f2), bf16), _sds((FFN_CONV, f2), f32), _sds((1, f2), f32)],
                 [pltpu.VMEM((t + PAD, 128), f32)] * 3)(hup, hup, conv_w, conv_w, bias, bias, dg)


def gate_act(z_tail, b_tile):
    t = z_tail.shape[0]
    tm = _tile(t, (512, 256, 128, 64))

    def body(z_ref, b_ref, o_ref):
        x = z_ref[...] + b_ref[...]
        lane = lax.broadcasted_iota(jnp.int32, x.shape, 1)
        o_ref[...] = jnp.where(lane < T_F, x, jax.nn.log_sigmoid(x))

    row = pl.BlockSpec((tm, TAIL), lambda i: (i, 0))
    return _call(body, "gate_act", (t // tm,), [row, pl.BlockSpec((1, TAIL), lambda i: (0, 0))], row, _sds((t, TAIL), f32))(z_tail, b_tile)


def tail_bwd(z_tail, b_tile, dzt_pe, dgate):
    t = z_tail.shape[0]
    tm = _tile(t, (512, 256, 128, 64))

    def body(z_ref, b_ref, dpe_ref, dg_ref, dz_ref, db_ref):
        x = z_ref[...] + b_ref[...]
        lane = lax.broadcasted_iota(jnp.int32, x.shape, 1)
        _, vjp = jax.vjp(jax.nn.log_sigmoid, x)
        df, = vjp(dg_ref[...])
        dgates = jnp.where(lane < T_F, dg_ref[...], df)
        dgates = jnp.where(jnp.logical_and(lane >= T_I, lane < T_F + ML_HEADS), dgates, 0.0)
        dz_ref[...] = jnp.where(lane < ROPE, dpe_ref[...], dgates).astype(bf16)
        _acc_row(db_ref, jnp.sum(dgates, axis=0, keepdims=True), pl.program_id(0) == 0)

    row = pl.BlockSpec((tm, TAIL), lambda i: (i, 0))
    return _call(body, "tail_bwd", (t // tm,), [row, pl.BlockSpec((1, TAIL), lambda i: (0, 0)), row, row],
                 [row, pl.BlockSpec((8, TAIL), lambda i: (0, 0))], [_sds((t, TAIL), bf16), _sds((8, TAIL), f32)])(z_tail, b_tile, dzt_pe, dgate)


def _mlstm_step(q, k, v, igr, fgr, c_mat, n_vec, m):
    ln = CHUNK
    row = lax.broadcasted_iota(jnp.int32, (ln, ln), 0)
    col = lax.broadcasted_iota(jnp.int32, (ln, ln), 1)
    eye = row == col

    def to_col(r):
        return jnp.sum(jnp.where(eye, jnp.broadcast_to(r, (ln, ln)), 0.0), axis=1, keepdims=True)

    bc_r = jnp.sum(jnp.where(row <= col, jnp.broadcast_to(to_col(fgr), (ln, ln)), 0.0), axis=0, keepdims=True)
    bc_c = to_col(bc_r)
    logw = jnp.where(col <= row, bc_c - bc_r + igr, -jnp.inf)
    inter = bc_c + m
    m_t = jnp.maximum(inter, jnp.max(logw, axis=1, keepdims=True))
    w_intra = jnp.exp(logw - m_t)
    w_inter = jnp.exp(inter - m_t)
    sc = _bdot(q, k, 1, 1) * w_intra
    num = w_inter * _bdot(q, c_mat, 1, 0) + _bdot(sc, v, 1, 0)
    qn = jnp.sum(q.astype(bf16).astype(f32) * n_vec.astype(bf16).astype(f32), axis=1, keepdims=True)
    den = w_inter * qn + jnp.sum(sc, axis=1, keepdims=True)
    h = num / jnp.maximum(jnp.abs(den), jnp.exp(-m_t))
    lane = lax.broadcasted_iota(jnp.int32, (1, ln), 1)
    b_last = jnp.sum(jnp.where(lane == ln - 1, bc_r, 0.0), axis=1, keepdims=True)
    logu = b_last - bc_r + igr
    m_new = jnp.maximum(b_last + m, jnp.max(logu, axis=1, keepdims=True))
    decay = jnp.exp(b_last + m - m_new)
    u_c = to_col(jnp.exp(logu - m_new))
    c_new = decay * c_mat + _bdot(u_c * k, v, 0, 0)
    n_new = decay * n_vec + jnp.sum(u_c.astype(bf16).astype(f32) * k.astype(bf16).astype(f32), axis=0, keepdims=True)
    return h, c_new, n_new, m_new


def _ml_specs(nc, rev):
    cc = (lambda c: nc - 1 - c) if rev else (lambda c: c)
    kq = ML_QK // ML_DK
    q = pl.BlockSpec((CHUNK, ML_DK), lambda h, c: (cc(c), h))
    k = pl.BlockSpec((CHUNK, ML_DK), lambda h, c: (cc(c), kq + h))
    v = pl.BlockSpec((CHUNK, ML_DV), lambda h, c: (cc(c), O_V // ML_DV + h))
    hv = pl.BlockSpec((CHUNK, ML_DV), lambda h, c: (cc(c), h))
    gate = pl.BlockSpec((1, 1, 1, CHUNK), lambda h, c: (h, cc(c), 0, 0))
    cm = pl.BlockSpec((1, 1, ML_DK, ML_DV), lambda h, c: (h, cc(c), 0, 0))
    nv = pl.BlockSpec((1, 1, 1, ML_DK), lambda h, c: (h, cc(c), 0, 0))
    ms = pl.BlockSpec((1, 1, 1, 1), lambda h, c: (h, cc(c), 0, 0))
    return q, k, v, hv, gate, cm, nv, ms


_ML_STATE = [pltpu.VMEM((ML_DK, ML_DV), f32), pltpu.VMEM((1, ML_DK), f32), pltpu.VMEM((1, 1), f32)]


def mlstm_fwd(qk_act, z_main, ig, fg):
    t = qk_act.shape[0]
    nc = t // CHUNK

    def body(q_ref, k_ref, v_ref, ig_ref, fg_ref, h_ref, c_out, n_out, m_out, c_s, n_s, m_s):
        @pl.when(pl.program_id(1) == 0)
        def _():
            c_s[...] = jnp.zeros_like(c_s)
            n_s[...] = jnp.zeros_like(n_s)
            m_s[...] = jnp.zeros_like(m_s)

        c_out[0, 0] = c_s[...]
        n_out[0, 0] = n_s[...]
        m_out[0, 0] = m_s[...]
        h, c2, n2, m2 = _mlstm_step(q_ref[...], k_ref[...], v_ref[...], ig_ref[0, 0], fg_ref[0, 0], c_s[...], n_s[...], m_s[...])
        h_ref[...] = h
        c_s[...] = c2
        n_s[...] = n2
        m_s[...] = m2

    q, k, v, hv, gate, cm, nv, ms = _ml_specs(nc, False)
    return _call(body, "mlstm_fwd", (ML_HEADS, nc), [q, k, v, gate, gate], [hv, cm, nv, ms],
                 [_sds((t, ML_V), f32), _sds((ML_HEADS, nc, ML_DK, ML_DV), f32), _sds((ML_HEADS, nc, 1, ML_DK), f32),
                  _sds((ML_HEADS, nc, 1, 1), f32)], _ML_STATE)(qk_act, qk_act, z_main, ig, fg)


def mlstm_bwd(qk_act, z_main, ig, fg, c_all, n_all, m_all, dh):
    t = qk_act.shape[0]
    nc = t // CHUNK

    def body(q_ref, k_ref, v_ref, ig_ref, fg_ref, c_ref, n_ref, m_ref, dh_ref, dq_ref, dk_ref, dv_ref, dig_ref, dfg_ref,
             dc_s, dn_s, dm_s):
        @pl.when(pl.program_id(1) == 0)
        def _():
            dc_s[...] = jnp.zeros_like(dc_s)
            dn_s[...] = jnp.zeros_like(dn_s)
            dm_s[...] = jnp.zeros_like(dm_s)

        _, vjp = jax.vjp(_mlstm_step, q_ref[...], k_ref[...], v_ref[...], ig_ref[0, 0], fg_ref[0, 0],
                         c_ref[0, 0], n_ref[0, 0], m_ref[0, 0])
        dq, dk, dv, dig, dfg, dc, dn, dm = vjp((dh_ref[...], dc_s[...], dn_s[...], dm_s[...]))
        dq_ref[...] = dq
        dk_ref[...] = dk
        dv_ref[...] = dv.astype(bf16)
        dig_ref[0, 0] = dig
        dfg_ref[0, 0] = dfg
        dc_s[...] = dc
        dn_s[...] = dn
        dm_s[...] = dm

    q, k, v, hv, gate, cm, nv, ms = _ml_specs(nc, True)
    gshape = _sds((ML_HEADS, nc, 1, CHUNK), f32)
    return _call(body, "mlstm_bwd", (ML_HEADS, nc), [q, k, v, gate, gate, cm, nv, ms, hv], [q, q, hv, gate, gate],
                 [_sds((t, ML_QK), f32), _sds((t, ML_QK), f32), _sds((t, ML_V), bf16), gshape, gshape],
                 _ML_STATE)(qk_act, qk_act, z_main, ig, fg, c_all, n_all, m_all, dh)


def _ml_out(h, zo, g):
    return _rms(h, g) * jax.nn.sigmoid(zo)


def mlstm_out(h, z_main, g_hnorm):
    t = h.shape[0]
    tm = _tile(t, (512, 256, 128, 64))
    zo = O_O // ML_DV

    def body(h_ref, z_ref, g_ref, y_ref):
        y_ref[...] = _ml_out(h_ref[...], z_ref[...], g_ref[0]).astype(bf16)

    blk = pl.BlockSpec((tm, ML_DV), lambda i, hd: (i, hd))
    return _call(body, "mlstm_out", (t // tm, ML_HEADS),
                 [blk, pl.BlockSpec((tm, ML_DV), lambda i, hd: (i, zo + hd)), pl.BlockSpec((1, 1, ML_DV), lambda i, hd: (hd, 0, 0))],
                 blk, _sds((t, ML_V), bf16))(h, z_main, g_hnorm)


def mlstm_out_bwd(h, z_main, g_hnorm, dy):
    t = h.shape[0]
    tm = _tile(t, (512, 256, 128, 64))
    zo = O_O // ML_DV

    def body(h_ref, z_ref, g_ref, dy_ref, dh_ref, dzo_ref, dg_ref):
        _, vjp = jax.vjp(_ml_out, h_ref[...], z_ref[...], g_ref[0])
        dh, dz, dg = vjp(dy_ref[...])
        dh_ref[...] = dh
        dzo_ref[...] = dz.astype(bf16)

        @pl.when(pl.program_id(1) == 0)
        def _():
            dg_ref[...] = jnp.zeros_like(dg_ref)

        dg_ref[0, 0:1, :] += dg

    blk = pl.BlockSpec((tm, ML_DV), lambda hd, i: (i, hd))
    return _call(body, "mlstm_out_bwd", (ML_HEADS, t // tm),
                 [blk, pl.BlockSpec((tm, ML_DV), lambda hd, i: (i, zo + hd)), pl.BlockSpec((1, 1, ML_DV), lambda hd, i: (hd, 0, 0)), blk],
                 [blk, blk, pl.BlockSpec((1, 8, ML_DV), lambda hd, i: (hd, 0, 0))],
                 [_sds((t, ML_V), f32), _sds((t, ML_V), bf16), _sds((ML_HEADS, 8, ML_DV), f32)])(h, z_main, g_hnorm, dy)


def _merge(ga, gb, ya, yb):
    return jax.nn.sigmoid(ga) * ya + jax.nn.sigmoid(gb) * yb


def _merge_specs(t, d):
    tm = _tile(t, (512, 256, 128, 64))
    bw = _tile(d, (512, 256, 128))
    assert O_GA % bw == 0 and (O_GA + d) % bw == 0
    blk = pl.BlockSpec((tm, bw), lambda i, j: (i, j))
    ga = pl.BlockSpec((tm, bw), lambda i, j: (i, O_GA // bw + j))
    gb = pl.BlockSpec((tm, bw), lambda i, j: (i, (O_GA + d) // bw + j))
    return tm, bw, blk, ga, gb


def merge_fwd(z_main, ya, yb):
    t, d = ya.shape
    tm, bw, blk, ga, gb = _merge_specs(t, d)

    def body(ga_ref, gb_ref, ya_ref, yb_ref, o_ref):
        o_ref[...] = _merge(ga_ref[...], gb_ref[...], ya_ref[...], yb_ref[...]).astype(bf16)

    return _call(body, "merge_fwd", (t // tm, d // bw), [ga, gb, blk, blk], blk, _sds((t, d), bf16))(z_main, z_main, ya, yb)


def merge_bwd(z_main, ya, yb, dmerged):
    t, d = ya.shape
    tm, bw, blk, ga, gb = _merge_specs(t, d)

    def body(ga_ref, gb_ref, ya_ref, yb_ref, dm_ref, dga_ref, dgb_ref, dya_ref, dyb_ref):
        _, vjp = jax.vjp(_merge, ga_ref[...], gb_ref[...], ya_ref[...], yb_ref[...])
        dga, dgb, dya, dyb = vjp(dm_ref[...])
        dga_ref[...] = dga.astype(bf16)
        dgb_ref[...] = dgb.astype(bf16)
        dya_ref[...] = dya.astype(bf16)
        dyb_ref[...] = dyb.astype(bf16)

    return _call(body, "merge_bwd", (t // tm, d // bw), [ga, gb, blk, blk, blk], [blk] * 4, [_sds((t, d), bf16)] * 4)(
        z_main, z_main, ya, yb, dmerged)


def _cross(cq, ck, cv, gq, gk):
    outs = []
    for hd in range(CR_HEADS):
        sl = slice(hd * CR_HD, (hd + 1) * CR_HD)
        q = _rms(cq[:, sl], gq)
        k = _rms(ck[:, sl], gk)
        s = _bdot(q, k, 1, 1) * (CR_HD ** -0.5)
        p = jax.nn.softmax(s, axis=-1)
        outs.append(_bdot(p, cv[:, sl], 1, 0))
    return jnp.concatenate(outs, axis=1)


def cross_fwd(cq, ck, cv, gq, gk):
    t, w = cq.shape
    nm = ck.shape[0]
    tm = _tile(t, (512, 256, 128, 64))

    def body(q_ref, k_ref, v_ref, gq_ref, gk_ref, o_ref):
        o_ref[...] = _cross(q_ref[...], k_ref[...], v_ref[...], gq_ref[...], gk_ref[...]).astype(bf16)

    row = pl.BlockSpec((tm, w), lambda i: (i, 0))
    full = pl.BlockSpec((nm, w), lambda i: (0, 0))
    gain = pl.BlockSpec((1, CR_HD), lambda i: (0, 0))
    return _call(body, "cross_fwd", (t // tm,), [row, full, full, gain, gain], row, _sds((t, w), bf16))(cq, ck, cv, gq, gk)


def cross_bwd(cq, ck, cv, gq, gk, do):
    t, w = cq.shape
    nm = ck.shape[0]
    tm = _tile(t, (512, 256, 128, 64))

    def body(q_ref, k_ref, v_ref, gq_ref, gk_ref, do_ref, dq_ref, dk_ref, dv_ref, dgq_ref, dgk_ref):
        first = pl.program_id(0) == 0
        _, vjp = jax.vjp(_cross, q_ref[...], k_ref[...], v_ref[...], gq_ref[...], gk_ref[...])
        dq, dk, dv, dgq, dgk = vjp(do_ref[...])
        dq_ref[...] = dq.astype(bf16)

        @pl.when(first)
        def _():
            dk_ref[...] = jnp.zeros_like(dk_ref)
            dv_ref[...] = jnp.zeros_like(dv_ref)

        dk_ref[...] += dk
        dv_ref[...] += dv
        _acc_row(dgq_ref, dgq, first)
        _acc_row(dgk_ref, dgk, first)

    row = pl.BlockSpec((tm, w), lambda i: (i, 0))
    full = pl.BlockSpec((nm, w), lambda i: (0, 0))
    gain = pl.BlockSpec((1, CR_HD), lambda i: (0, 0))
    acc = pl.BlockSpec((8, CR_HD), lambda i: (0, 0))
    return _call(body, "cross_bwd", (t // tm,), [row, full, full, gain, gain, row], [row, full, full, acc, acc],
                 [_sds((t, w), bf16), _sds((nm, w), f32), _sds((nm, w), f32), _sds((8, CR_HD), f32), _sds((8, CR_HD), f32)])(
        cq, ck, cv, gq, gk, do)


def loss_head(x2, fo, target):
    t, d = x2.shape
    tm = _tile(t, (256, 128, 64, 32, 16, 8))

    def body(a_ref, b_ref, t_ref, dx_ref, l_ref):
        err = a_ref[...] + b_ref[...] - t_ref[...]
        dx_ref[...] = err / d
        part = 0.5 * jnp.sum(jnp.mean(err * err, axis=1, keepdims=True), axis=0, keepdims=True)
        _acc_row(l_ref, jnp.broadcast_to(part, (1, 128)), pl.program_id(0) == 0)

    row = pl.BlockSpec((tm, d), lambda i: (i, 0))
    return _call(body, "loss_head", (t // tm,), [row, row, row], [row, pl.BlockSpec((8, 128), lambda i: (0, 0))],
                 [_sds((t, d), f32), _sds((8, 128), f32)])(x2, fo, target)


def _place():
    x, y, c = lax.axis_index("x"), lax.axis_index("y"), lax.axis_index("c")
    peers = []
    for k in range(1, N_DEV):
        px = 1 - x if k & 4 else x
        py = 1 - y if k & 2 else y
        pc = 1 - c if k & 1 else c
        peers.append(((px, py, pc), 4 * px + 2 * py + pc))
    return 4 * x + 2 * y + c, peers


def _exchange(arrs, name, scatter):
    n = len(arrs)
    n_rel = N_DEV - 1

    def body(*refs):
        ins, outs = refs[:n], refs[n:2 * n]
        send_sems, recv_sems, local_sems = refs[2 * n:]
        me, peers = _place()

        def src(a, idx):
            return ins[a].at[idx] if scatter else ins[a]

        local = [pltpu.make_async_copy(src(a, me), outs[a].at[me], local_sems.at[a]) for a in range(n)]
        for cp in local:
            cp.start()

        def remote(a, k, src_idx, dst_idx):
            dev, _ = peers[k]
            return pltpu.make_async_remote_copy(
                src_ref=src(a, src_idx), dst_ref=outs[a].at[dst_idx], send_sem=send_sems.at[a * n_rel + k],
                recv_sem=recv_sems.at[a * n_rel + k], device_id=dev, device_id_type=MESH)

        sends = [remote(a, k, peers[k][1], me) for a in range(n) for k in range(n_rel)]
        for cp in sends:
            cp.start()
        for a in range(n):
            for k in range(n_rel):
                remote(a, k, me, peers[k][1]).wait_recv()
        for cp in sends:
            cp.wait_send()
        for cp in local:
            cp.wait()

    any_spec = pl.BlockSpec(memory_space=pl.ANY)
    out_shape = [_sds(a.shape if scatter else (N_DEV,) + a.shape, a.dtype) for a in arrs]
    return pl.pallas_call(
        body, name=name, in_specs=[any_spec] * n, out_specs=[any_spec] * n, out_shape=out_shape,
        scratch_shapes=[pltpu.SemaphoreType.DMA((n * n_rel,)), pltpu.SemaphoreType.DMA((n * n_rel,)), pltpu.SemaphoreType.DMA((n,))],
    )(*arrs)


def cast_bf16(w, name):
    r, c = w.shape
    tr = _tile(r, (256, 128, 64, 32, 16))

    def body(w_ref, o_ref):
        o_ref[...] = w_ref[...].astype(bf16)

    blk = pl.BlockSpec((tr, c), lambda i: (i, 0))
    return _call(body, name, (r // tr,), [blk], blk, _sds((r, c), bf16))(w)


def _adamw(w, g, m, v):
    m = ADAM_B1 * m + (1.0 - ADAM_B1) * g
    v = ADAM_B2 * v + (1.0 - ADAM_B2) * jnp.square(g)
    m_hat = m / (1.0 - ADAM_B1 ** ADAM_STEP)
    v_hat = v / (1.0 - ADAM_B2 ** ADAM_STEP)
    delta = -ADAM_LR * (m_hat / (jnp.sqrt(v_hat) + ADAM_EPS) + ADAM_WD * w)
    return delta, m, v


def adam_sum(parts, w, m, v, name):
    _, r, c = parts.shape
    budget = 4 * 1024 * 1024
    tr = r
    for cand in (1024, 512, 256, 128, 64, 32, 16):
        if r % cand == 0 and N_DEV * cand * c * 4 <= budget:
            tr = cand
            break

    def body(p_ref, w_ref, m_ref, v_ref, g_ref, d_ref, m2_ref, v2_ref):
        g = p_ref[0].astype(f32)
        for k in range(1, N_DEV):
            g = g + p_ref[k].astype(f32)
        d, m2, v2 = _adamw(w_ref[...], g, m_ref[...], v_ref[...])
        g_ref[...] = g
        d_ref[...] = d
        m2_ref[...] = m2
        v2_ref[...] = v2

    blk = pl.BlockSpec((tr, c), lambda i: (i, 0))
    return _call(body, name, (r // tr,), [pl.BlockSpec((N_DEV, tr, c), lambda i: (0, i, 0)), blk, blk, blk], [blk] * 4,
                 [_sds((r, c), f32)] * 4)(parts, w, m, v)


def sum_parts(parts, name):
    _, r, c = parts.shape

    def body(p_ref, o_ref):
        g = p_ref[0]
        for k in range(1, N_DEV):
            g = g + p_ref[k]
        o_ref[...] = g

    return pl.pallas_call(body, name=name, out_shape=_sds((r, c), f32))(parts)


def adam_flat(w, g, m, v, name):
    def body(w_ref, g_ref, m_ref, v_ref, d_ref, m2_ref, v2_ref):
        d, m2, v2 = _adamw(w_ref[...], g_ref[...], m_ref[...], v_ref[...])
        d_ref[...] = d
        m2_ref[...] = m2
        v2_ref[...] = v2

    return pl.pallas_call(body, name=name, out_shape=[_sds(w.shape, f32)] * 3)(w, g, m, v)


def _pack(vecs, multiple):
    flat = jnp.concatenate([v.reshape(-1) for v in vecs])
    n = flat.shape[0]
    total = -(-n // multiple) * multiple
    return jnp.pad(flat, (0, total - n))


def _unpack(flat, shapes):
    out, pos = [], 0
    for s in shapes:
        n = 1
        for d in s:
            n *= d
        out.append(flat[pos:pos + n].reshape(s))
        pos += n
    return out


def _pad_lanes(v, width=TAIL):
    return jnp.pad(v, ((0, 0), (0, width - v.shape[1])))


def kernel(x, mem, positions, g_mix, w_in, g_qa, w_qb, g_kva, w_kvb, g_qn_nope, g_qn_pe, g_kn_nope, g_kn_pe, conv_qk, b_if, g_hnorm, p_a, p_b, w_out, g_cross, g_mem, wq_c, wk_c, wv_c, g_cq, g_ck, wo_c, g_ffn, w_up, conv_ffn, b_conv_ffn, w_down, loss_target, m_g_mix, m_w_in, m_g_qa, m_w_qb, m_g_kva, m_w_kvb, m_g_qn_nope, m_g_qn_pe, m_g_kn_nope, m_g_kn_pe, m_conv_qk, m_b_if, m_g_hnorm, m_p_a, m_p_b, m_w_out, m_g_cross, m_g_mem, m_wq_c, m_wk_c, m_wv_c, m_g_cq, m_g_ck, m_wo_c, m_g_ffn, m_w_up, m_conv_ffn, m_b_conv_ffn, m_w_down, v_g_mix, v_w_in, v_g_qa, v_w_qb, v_g_kva, v_w_kvb, v_g_qn_nope, v_g_qn_pe, v_g_kn_nope, v_g_kn_pe, v_conv_qk, v_b_if, v_g_hnorm, v_p_a, v_p_b, v_w_out, v_g_cross, v_g_mem, v_wq_c, v_wk_c, v_wv_c, v_g_cq, v_g_ck, v_wo_c, v_g_ffn, v_w_up, v_conv_ffn, v_b_conv_ffn, v_w_down):
    args = dict(locals())
    names = ['g_mix', 'w_in', 'g_qa', 'w_qb', 'g_kva', 'w_kvb', 'g_qn_nope', 'g_qn_pe', 'g_kn_nope', 'g_kn_pe', 'conv_qk', 'b_if',
             'g_hnorm', 'p_a', 'p_b', 'w_out', 'g_cross', 'g_mem', 'wq_c', 'wk_c', 'wv_c', 'g_cq', 'g_ck', 'wo_c', 'g_ffn', 'w_up',
             'conv_ffn', 'b_conv_ffn', 'w_down']
    big = ['w_in', 'w_qb', 'w_kvb', 'p_a', 'p_b', 'w_out', 'wq_c', 'wk_c', 'wv_c', 'wo_c', 'w_up', 'w_down']
    sharded_small = ['conv_qk', 'g_hnorm', 'conv_ffn']
    replicated = [n for n in names if n not in big and n not in sharded_small]

    t, d = x.shape[1], x.shape[2]
    x2d, tgt = x[0], loss_target[0]
    mem2d = mem[0]
    me = 4 * lax.axis_index("x") + 2 * lax.axis_index("y") + lax.axis_index("c")
    nc = t // CHUNK
    f2 = b_conv_ffn.shape[1]
    wmain = O_GA + 2 * d

    shards = {n: args[n][0] for n in big}
    small_local = _pack([args[n] for n in sharded_small], 128).reshape(1, -1)
    gathered = _exchange([cast_bf16(shards[n], "cast_" + n) for n in big] + [small_local], "comm_gather_weights", scatter=False)
    gw = dict(zip(big, gathered[:-1]))
    small_all = gathered[-1]
    small_shapes = [args[n].shape for n in sharded_small]
    per_dev = [_unpack(small_all[k, 0], small_shapes) for k in range(N_DEV)]
    conv_qk_f = jnp.concatenate([p[0] for p in per_dev], axis=-1)[0]
    g_hnorm_f = jnp.concatenate([p[1] for p in per_dev], axis=-1)[0]
    conv_ffn_f = jnp.concatenate([p[2] for p in per_dev], axis=-1)[0]

    w_in_f = gw['w_in'].transpose(1, 0, 2).reshape(d, -1)
    c_kpe, c_q, c_i, c_o = O_Q, O_Q + ROPE, O_Q + ROPE + 2 * ML_QK + ML_V, O_Q + ROPE + 2 * ML_QK + ML_V + 2 * ML_HEADS
    w_main = jnp.concatenate([w_in_f[:, :c_kpe], w_in_f[:, c_q:c_i], w_in_f[:, c_o:]], axis=1)[None]
    w_tail = jnp.concatenate([w_in_f[:, c_kpe:c_q], w_in_f[:, c_i:c_o],
                              jnp.zeros((d, TAIL - ROPE - 2 * ML_HEADS), bf16)], axis=1)[None]
    assert w_main.shape[2] == wmain
    qb = gw['w_qb'].transpose(1, 0, 2).reshape(Q_LORA, MLA_HEADS, NOPE + ROPE)
    w_qb_p = jnp.concatenate([qb, jnp.zeros((Q_LORA, MLA_HEADS, HEAD_PAD - NOPE - ROPE), bf16)], axis=2).reshape(1, Q_LORA, -1)
    w_kvb3 = gw['w_kvb']
    p_a3, p_b3, w_out3 = (gw[n].reshape(1, -1, d) for n in ('p_a', 'p_b', 'w_out'))
    wq_c3, wk_c3, wv_c3 = (gw[n].reshape(1, d, -1) for n in ('wq_c', 'wk_c', 'wv_c'))
    wo_c3, w_up3 = gw['wo_c'], gw['w_up']
    w_down3 = gw['w_down'].reshape(1, -1, d)

    inv_freq = ROPE_BASE ** (-jnp.arange(0, ROPE, 2, dtype=f32) / ROPE)
    inv_tile = _pad_lanes(jnp.concatenate([inv_freq, inv_freq])[None])
    cos, sin = rope_tables(positions.reshape(t, 1), inv_tile)
    gqp, gkp = _pad_lanes(g_qn_pe), _pad_lanes(g_kn_pe)
    b_tile = jnp.pad(b_if, ((0, 0), (T_I, TAIL - T_I - 2 * ML_HEADS)))

    u0 = rms_fwd(x2d, g_mix, "rms_mix")
    z_main = mm_nn(u0, w_main, f32, "mm_in_main")
    z_tail = mm_nn(u0, w_tail, f32, "mm_in_tail")
    qa_n, kv_n = lat_norm(z_main, g_qa, g_kva)
    q_raw = mm_nn(qa_n, w_qb_p, f32, "mm_qb")
    kv_raw = mm_nn(kv_n, w_kvb3, f32, "mm_kvb")
    qh, kh, vh = mla_prep(q_raw, kv_raw, z_tail, cos, sin, g_qn_nope, gqp, g_kn_nope, gkp)
    o_a, lse = mla_fwd(qh, kh, vh)

    qk_act = qk_conv(z_main, conv_qk_f)
    gates = gate_act(z_tail, b_tile)

    def to_rows(cols):
        return cols.T.reshape(ML_HEADS, nc, 1, CHUNK)

    ig, fg = to_rows(gates[:, T_I:T_F]), to_rows(gates[:, T_F:T_F + ML_HEADS])
    h_ml, c_all, n_all, m_all = mlstm_fwd(qk_act, z_main, ig, fg)
    g_hn3 = g_hnorm_f.reshape(ML_HEADS, 1, ML_DV)
    y_b = mlstm_out(h_ml, z_main, g_hn3)

    ya = mm_nn(o_a, p_a3, f32, "mm_pa")
    yb = mm_nn(y_b, p_b3, f32, "mm_pb")
    merged = merge_fwd(z_main, ya, yb)
    mo = mm_nn(merged, w_out3, f32, "mm_out")
    x1, uc = resid_rms(x2d, mo, g_cross, "resid_cross")
    mem_n = rms_fwd(mem2d, g_mem, "rms_mem")
    cq = mm_nn(uc, wq_c3, f32, "mm_cq")
    ck = mm_nn(mem_n, wk_c3, f32, "mm_ck")
    cv = mm_nn(mem_n, wv_c3, f32, "mm_cv")
    o_c = cross_fwd(cq, ck, cv, g_cq, g_ck)
    co = mm_nn(o_c, wo_c3, f32, "mm_oc")
    x2, u3 = resid_rms(x1, co, g_ffn, "resid_ffn")
    hup = mm_nn(u3, w_up3, f32, "mm_up")
    gl = glu_fwd(hup, conv_ffn_f, b_conv_ffn)
    fo = mm_nn(gl, w_down3, f32, "mm_down")
    dx3, loss_acc = loss_head(x2, fo, tgt)

    grads = {}
    grads['w_down'] = mm_tn(gl, dx3, 1, "mm_d_wdown").reshape(N_DEV, -1, d)
    dgl = mm_nt(dx3, w_down3, f32, "mm_d_gl")
    dhup, dconv_ffn, db_ffn = glu_bwd(hup, conv_ffn_f, b_conv_ffn, dgl)
    grads['w_up'] = mm_tn(u3, dhup, N_DEV, "mm_d_wup")
    du3 = mm_nt(dhup, w_up3, f32, "mm_d_u3")
    dx2, dg_ffn = rms_bwd(x2, g_ffn, [du3], dx3, "rms_bwd_ffn")
    grads['wo_c'] = mm_tn(o_c, dx2, N_DEV, "mm_d_woc")
    do_c = mm_nt(dx2, wo_c3, f32, "mm_d_oc")
    dcq, dck, dcv, dg_cq, dg_ck = cross_bwd(cq, ck, cv, g_cq, g_ck, do_c)
    grads['wq_c'] = mm_tn(uc, dcq, 1, "mm_d_wqc").reshape(N_DEV, -1, dcq.shape[1])
    grads['wk_c'] = mm_tn(mem_n, dck, 1, "mm_d_wkc").reshape(N_DEV, -1, dck.shape[1])
    grads['wv_c'] = mm_tn(mem_n, dcv, 1, "mm_d_wvc").reshape(N_DEV, -1, dcv.shape[1])
    duc = mm_nt(dcq, wq_c3, f32, "mm_d_uc")
    dmem_k = mm_nt(dck, wk_c3, f32, "mm_d_memk")
    dmem_v = mm_nt(dcv, wv_c3, f32, "mm_d_memv")
    dg_mem, = rms_bwd(mem2d, g_mem, [dmem_k, dmem_v], None, "rms_bwd_mem", want_dx=False)
    dx1, dg_cross = rms_bwd(x1, g_cross, [duc], dx2, "rms_bwd_cross")
    grads['w_out'] = mm_tn(merged, dx1, 1, "mm_d_wout").reshape(N_DEV, -1, d)
    dmerged = mm_nt(dx1, w_out3, f32, "mm_d_merged")
    dga, dgb, dya, dyb = merge_bwd(z_main, ya, yb, dmerged)
    grads['p_a'] = mm_tn(o_a, dya, 1, "mm_d_pa").reshape(N_DEV, -1, d)
    grads['p_b'] = mm_tn(y_b, dyb, 1, "mm_d_pb").reshape(N_DEV, -1, d)
    do_a = mm_nt(dya, p_a3, f32, "mm_d_oa")
    dy_b = mm_nt(dyb, p_b3, f32, "mm_d_yb")

    dh_ml, dzo, dg_hn = mlstm_out_bwd(h_ml, z_main, g_hn3, dy_b)
    dq_act, dk_act, dzv, dig, dfg = mlstm_bwd(qk_act, z_main, ig, fg, c_all, n_all, m_all, dh_ml)
    dqk = jnp.concatenate([dq_act, dk_act], axis=1)
    dzqk, dconv_qk = qk_conv_bwd(z_main, conv_qk_f, dqk)

    delta = mla_delta(o_a, do_a)
    dqh, dkh, dvh = mla_bwd(qh, kh, vh, do_a, lse.reshape(MLA_HEADS, 1, t), delta.reshape(MLA_HEADS, 1, t))
    dq_raw, dkv_raw, dzt_pe, dg_qn, dg_qp, dg_kn, dg_kp = mla_prep_bwd(
        q_raw, kv_raw, z_tail, cos, sin, g_qn_nope, gqp, g_kn_nope, gkp, dqh, dkh, dvh)
    d_wqb_p = mm_tn(qa_n, dq_raw, 1, "mm_d_wqb")[0].reshape(Q_LORA, MLA_HEADS, HEAD_PAD)[:, :, :NOPE + ROPE]
    grads['w_qb'] = d_wqb_p.reshape(Q_LORA, N_DEV, -1).transpose(1, 0, 2)
    grads['w_kvb'] = mm_tn(kv_n, dkv_raw, N_DEV, "mm_d_wkvb")
    dqa = mm_nt(dq_raw, w_qb_p, f32, "mm_d_qa")
    dkvn = mm_nt(dkv_raw, w_kvb3, f32, "mm_d_kvn")
    dz_lat, dg_qa, dg_kva = lat_norm_bwd(z_main, g_qa, g_kva, dqa, dkvn)

    def to_cols(rows):
        return rows.reshape(ML_HEADS, t).T

    dgate = jnp.pad(jnp.concatenate([to_cols(dig), to_cols(dfg)], axis=1), ((0, 0), (T_I, TAIL - T_I - 2 * ML_HEADS)))
    dz_tail, db_if = tail_bwd(z_tail, b_tile, dzt_pe, dgate)
    dz_main = jnp.concatenate([dz_lat, dzqk, dzv, dzo, dga, dgb], axis=1)
    d_wmain = mm_tn(u0, dz_main, 1, "mm_d_wmain")[0]
    d_wtail = mm_tn(u0, dz_tail, 1, "mm_d_wtail")[0]
    du0_a = mm_nt(dz_main, w_main, f32, "mm_d_u0_main")
    du0_b = mm_nt(dz_tail, w_tail, f32, "mm_d_u0_tail")
    grad_x, dg_mix = rms_bwd(x2d, g_mix, [du0_a, du0_b], dx1, "rms_bwd_mix")
    d_win = jnp.concatenate([d_wmain[:, :O_Q], d_wtail[:, :ROPE], d_wmain[:, O_Q:O_O], d_wtail[:, T_I:T_I + 2 * ML_HEADS],
                             d_wmain[:, O_O:]], axis=1)
    grads['w_in'] = d_win.reshape(d, N_DEV, -1).transpose(1, 0, 2)

    parts = _exchange([grads[n] for n in big], "comm_scatter_grads", scatter=True)
    out_g, out_d, out_m, out_v = {}, {}, {}, {}
    for n, p in zip(big, parts):
        shp = args[n].shape
        g, dl, m2, v2 = adam_sum(p, args[n][0], args['m_' + n][0], args['v_' + n][0], "adam_" + n)
        out_g[n], out_d[n], out_m[n], out_v[n] = (a.reshape(shp) for a in (g, dl, m2, v2))

    small_full = {
        'g_mix': dg_mix[0], 'g_qa': dg_qa[0], 'g_kva': dg_kva[0], 'g_qn_nope': dg_qn[0], 'g_qn_pe': dg_qp[0, :ROPE],
        'g_kn_nope': dg_kn[0], 'g_kn_pe': dg_kp[0, :ROPE], 'conv_qk': dconv_qk, 'b_if': db_if[0, T_I:T_I + 2 * ML_HEADS],
        'g_hnorm': dg_hn[:, 0, :], 'g_cross': dg_cross[0], 'g_mem': dg_mem[0], 'g_cq': dg_cq[0], 'g_ck': dg_ck[0],
        'g_ffn': dg_ffn[0], 'conv_ffn': dconv_ffn, 'b_conv_ffn': db_ffn[0], 'loss': loss_acc[0, :1]}
    order = list(small_full)
    packed = _pack([small_full[n] for n in order], 8 * 128).reshape(1, -1)
    gathered_small, = _exchange([packed], "comm_gather_small", scatter=False)
    summed = sum_parts(gathered_small.reshape(N_DEV, -1, 128), "sum_small").reshape(-1)
    full_g = dict(zip(order, _unpack(summed, [small_full[n].shape for n in order])))
    loss = full_g['loss'][0]

    local_g = {}
    for n in replicated:
        local_g[n] = full_g[n].reshape(args[n].shape)
    for n in sharded_small:
        shp = args[n].shape
        full = full_g[n].reshape((1,) + full_g[n].shape)
        local_g[n] = lax.dynamic_slice_in_dim(full, me * shp[-1], shp[-1], axis=2)
    small = replicated + sharded_small
    dl_f, m_f, v_f = adam_flat(*[_pack([src[n] if pre == '' else args[pre + n] for n in small], 8 * 128).reshape(-1, 128)
                                 for pre, src in (('', args), ('', local_g), ('m_', None), ('v_', None))], "adam_small")
    shapes = [args[n].shape for n in small]
    for dst, flat in ((out_d, dl_f), (out_m, m_f), (out_v, v_f)):
        dst.update(zip(small, _unpack(flat.reshape(-1), shapes)))
    out_g.update(local_g)

    return (loss, grad_x[None], *[out_g[n] for n in names], *[out_d[n] for n in names],
            *[out_m[n] for n in names], *[out_v[n] for n in names])
```

```python
import functools

import jax
import jax.numpy as jnp
from jax import lax
from jax.experimental import pallas as pl
from jax.experimental.pallas import tpu as pltpu

f32 = jnp.float32
bf16 = jnp.bfloat16

N_DEV = 8
EPS = 1e-6
CHUNK = 64
CHUNK_SHIFT = 6
assert 1 << CHUNK_SHIFT == CHUNK
MLA_HEADS = 16
Q_LORA = 512
KV_LORA = 512
NOPE = 128
ROPE = 64
V_HEAD = 128
ROPE_BASE = 10000.0
HEAD_PAD = 256
ML_HEADS = 8
ML_DK = 128
ML_DV = 256
ML_CONV = 4
ML_QK = ML_HEADS * ML_DK
ML_V = ML_HEADS * ML_DV
CR_HEADS = 4
CR_HD = 128
FFN_CONV = 3
ADAM_LR = 0.001
ADAM_B1 = 0.9
ADAM_B2 = 0.999
ADAM_EPS = 1e-08
ADAM_WD = 0.01
ADAM_STEP = 10
O_QA, O_KV, O_Q, O_K = 0, Q_LORA, Q_LORA + KV_LORA, Q_LORA + KV_LORA + ML_QK
O_V = O_K + ML_QK
O_O = O_V + ML_V
O_GA = O_O + ML_V
TAIL = 128
T_I, T_F = ROPE, ROPE + ML_HEADS
VMEM_LIMIT_V7X = 48 * 1024 * 1024
MESH = pl.DeviceIdType.MESH


def _call(body, name, grid, in_specs, out_specs, out_shape, scratch=(), exchange=None):
    params = pltpu.CompilerParams(vmem_limit_bytes=VMEM_LIMIT_V7X)
    if exchange is None:
        return pl.pallas_call(body, name=name, grid=grid, in_specs=in_specs, out_specs=out_specs, out_shape=out_shape,
                              scratch_shapes=list(scratch), compiler_params=params)
    arrs, scatter = exchange
    single = not isinstance(out_specs, (list, tuple))
    o_specs = [out_specs] if single else list(out_specs)
    o_shape = [out_shape] if single else list(out_shape)
    n_in, n_out, n_sc, n = len(in_specs), len(o_specs), len(scratch), len(arrs)
    any_spec = pl.BlockSpec(memory_space=pl.ANY)

    def body_with_exchange(*refs):
        pos = [0]

        def take(k):
            pos[0] += k
            return refs[pos[0] - k:pos[0]]

        ins, ex_in, outs, ex_out, sc = take(n_in), take(n), take(n_out), take(n), take(n_sc)
        start, wait = _exchange_ops(ex_in, ex_out, refs[pos[0]:], scatter)
        ids = [pl.program_id(a) for a in range(len(grid))]
        pl.when(_first(*ids))(start)
        body(*ins, *outs, *sc)
        last = ids[0] == grid[0] - 1
        for a in range(1, len(grid)):
            last = jnp.logical_and(last, ids[a] == grid[a] - 1)
        pl.when(last)(wait)

    call = pl.pallas_call(body_with_exchange, name="comm_" + name, grid=grid, in_specs=list(in_specs) + [any_spec] * n,
                          out_specs=o_specs + [any_spec] * n, out_shape=o_shape + _exchange_shapes(arrs, scatter),
                          scratch_shapes=list(scratch) + _exchange_sems(n), compiler_params=params)

    def run(*operands):
        res = call(*operands, *arrs)
        return (res[0] if single else list(res[:n_out])), list(res[n_out:])

    return run


def _tile(n, cands):
    for c in cands:
        if n % c == 0:
            return c
    return n


def _sds(shape, dtype):
    return jax.ShapeDtypeStruct(tuple(shape), dtype)


def _bdot(a, b, ca, cb):
    return lax.dot_general(a.astype(bf16), b.astype(bf16), (((ca,), (cb,)), ((), ())), preferred_element_type=f32)


_BIG = (1024, 512, 256, 128)


def _col_tile(nb):
    return nb if nb <= 1536 else _tile(nb, _BIG)


def mm_nn(a, w3, out_dtype, name, exchange=None):
    m, k = a.shape
    nblk, k2, nb = w3.shape
    assert k == k2
    tm, tk, tn = _tile(m, _BIG), _tile(k, (512, 256, 128)), _col_tile(nb)
    per, nk = nb // tn, k // tk

    def body(a_ref, w_ref, o_ref, acc):
        kk = pl.program_id(2)

        @pl.when(kk == 0)
        def _():
            acc[...] = jnp.zeros_like(acc)

        acc[...] += _bdot(a_ref[...], w_ref[0], 1, 0)

        @pl.when(kk == nk - 1)
        def _():
            o_ref[...] = acc[...].astype(o_ref.dtype)

    return _call(body, name, (m // tm, nblk * per, nk),
                 [pl.BlockSpec((tm, tk), lambda i, j, kk: (i, kk)),
                  pl.BlockSpec((1, tk, tn), lambda i, j, kk: (j // per, kk, j % per))],
                 pl.BlockSpec((tm, tn), lambda i, j, kk: (i, j)), _sds((m, nblk * nb), out_dtype),
                 [pltpu.VMEM((tm, tn), f32)], exchange=exchange)(a, w3)


def mm_nt(a, w3, out_dtype, name, exchange=None):
    m, n = a.shape
    nblk, k, nb = w3.shape
    assert n == nblk * nb
    tm, tn = _tile(m, _BIG), _tile(k, _BIG)
    tc = nb if nb <= 1536 else _tile(nb, (512, 256, 128))
    per = nb // tc
    nk = nblk * per

    def body(a_ref, w_ref, o_ref, acc):
        kk = pl.program_id(2)

        @pl.when(kk == 0)
        def _():
            acc[...] = jnp.zeros_like(acc)

        acc[...] += _bdot(a_ref[...], w_ref[0], 1, 1)

        @pl.when(kk == nk - 1)
        def _():
            o_ref[...] = acc[...].astype(o_ref.dtype)

    return _call(body, name, (m // tm, k // tn, nk),
                 [pl.BlockSpec((tm, tc), lambda i, j, kk: (i, kk)),
                  pl.BlockSpec((1, tn, tc), lambda i, j, kk: (kk // per, j, kk % per))],
                 pl.BlockSpec((tm, tn), lambda i, j, kk: (i, j)), _sds((m, k), out_dtype),
                 [pltpu.VMEM((tm, tn), f32)], exchange=exchange)(a, w3)


def mm_tn(a, b, nblk, name):
    r, m = a.shape
    r2, n = b.shape
    assert r == r2 and n % nblk == 0
    nb = n // nblk
    tm, tk, tn = _tile(m, _BIG), _tile(r, (512, 256, 128)), _col_tile(nb)
    per, nk = nb // tn, r // tk

    def body(a_ref, b_ref, o_ref, acc):
        kk = pl.program_id(2)

        @pl.when(kk == 0)
        def _():
            acc[...] = jnp.zeros_like(acc)

        acc[...] += _bdot(a_ref[...], b_ref[...], 0, 0)

        @pl.when(kk == nk - 1)
        def _():
            o_ref[0] = acc[...].astype(bf16)

    return _call(body, name, (m // tm, nblk * per, nk),
                 [pl.BlockSpec((tk, tm), lambda i, j, kk: (kk, i)),
                  pl.BlockSpec((tk, tn), lambda i, j, kk: (kk, j))],
                 pl.BlockSpec((1, tm, tn), lambda i, j, kk: (j // per, i, j % per)), _sds((nblk, m, nb), bf16),
                 [pltpu.VMEM((tm, tn), f32)])(a, b)


def _rms(x, g):
    return x * lax.rsqrt(jnp.mean(x * x, axis=-1, keepdims=True) + EPS) * g


def _rms_pad(x, g, width):
    return x * lax.rsqrt(jnp.sum(x * x, axis=-1, keepdims=True) / width + EPS) * g


def _first(*ids):
    ok = ids[0] == 0
    for i in ids[1:]:
        ok = jnp.logical_and(ok, i == 0)
    return ok


def _acc_row(ref, val, first):
    @pl.when(first)
    def _():
        ref[...] = jnp.zeros_like(ref)

    ref[0:1, :] += val


def rms_fwd(x, g, name):
    r, w = x.shape
    tm = _tile(r, (256, 128, 64, 32, 16, 8))

    def body(x_ref, g_ref, o_ref):
        o_ref[...] = _rms(x_ref[...], g_ref[...]).astype(bf16)

    return _call(body, name, (r // tm,), [pl.BlockSpec((tm, w), lambda i: (i, 0)), pl.BlockSpec((1, w), lambda i: (0, 0))],
                 pl.BlockSpec((tm, w), lambda i: (i, 0)), _sds((r, w), bf16))(x, g)


def resid_rms(xa, xb, g, name):
    r, w = xa.shape
    tm = _tile(r, (256, 128, 64, 32, 16, 8))

    def body(a_ref, b_ref, g_ref, s_ref, u_ref):
        xs = a_ref[...] + b_ref[...]
        s_ref[...] = xs
        u_ref[...] = _rms(xs, g_ref[...]).astype(bf16)

    row = pl.BlockSpec((tm, w), lambda i: (i, 0))
    return _call(body, name, (r // tm,), [row, row, pl.BlockSpec((1, w), lambda i: (0, 0))], [row, row],
                 [_sds((r, w), f32), _sds((r, w), bf16)])(xa, xb, g)


def rms_bwd(x, g, dys, dres, name, want_dx=True):
    r, w = x.shape
    tm = _tile(r, (256, 128, 64, 32, 16, 8))
    nd = len(dys)

    def body(*refs):
        x_ref, g_ref = refs[0], refs[1]
        dy = refs[2][...]
        for j in range(1, nd):
            dy = dy + refs[2 + j][...]
        pos = 2 + nd
        _, vjp = jax.vjp(_rms, x_ref[...], g_ref[...])
        dx, dg = vjp(dy)
        if dres is not None:
            dx = dx + refs[pos][...]
            pos += 1
        if want_dx:
            refs[pos][...] = dx
            pos += 1
        _acc_row(refs[pos], dg, pl.program_id(0) == 0)

    row = pl.BlockSpec((tm, w), lambda i: (i, 0))
    ins = [x, g] + list(dys) + ([dres] if dres is not None else [])
    in_specs = [row, pl.BlockSpec((1, w), lambda i: (0, 0))] + [row] * (nd + (dres is not None))
    out_specs = ([row] if want_dx else []) + [pl.BlockSpec((8, w), lambda i: (0, 0))]
    out_shape = ([_sds((r, w), f32)] if want_dx else []) + [_sds((8, w), f32)]
    return _call(body, name, (r // tm,), in_specs, out_specs, out_shape)(*ins)


def lat_norm(z_main, g_qa, g_kva):
    t = z_main.shape[0]
    tm = _tile(t, (512, 256, 128, 64))

    def body(z_ref, gq_ref, gk_ref, q_ref, k_ref):
        q_ref[...] = _rms(z_ref[:, :Q_LORA], gq_ref[...]).astype(bf16)
        k_ref[...] = _rms(z_ref[:, Q_LORA:], gk_ref[...]).astype(bf16)

    return _call(body, "lat_norm", (t // tm,),
                 [pl.BlockSpec((tm, Q_LORA + KV_LORA), lambda i: (i, 0)), pl.BlockSpec((1, Q_LORA), lambda i: (0, 0)),
                  pl.BlockSpec((1, KV_LORA), lambda i: (0, 0))],
                 [pl.BlockSpec((tm, Q_LORA), lambda i: (i, 0)), pl.BlockSpec((tm, KV_LORA), lambda i: (i, 0))],
                 [_sds((t, Q_LORA), bf16), _sds((t, KV_LORA), bf16)])(z_main, g_qa, g_kva)


def lat_norm_bwd(z_main, g_qa, g_kva, dqa, dkv):
    t = z_main.shape[0]
    tm = _tile(t, (512, 256, 128, 64))

    def body(z_ref, gq_ref, gk_ref, dq_ref, dk_ref, dz_ref, dgq_ref, dgk_ref):
        first = pl.program_id(0) == 0
        _, vq = jax.vjp(_rms, z_ref[:, :Q_LORA], gq_ref[...])
        dx, dg = vq(dq_ref[...])
        dz_ref[:, :Q_LORA] = dx.astype(bf16)
        _acc_row(dgq_ref, dg, first)
        _, vk = jax.vjp(_rms, z_ref[:, Q_LORA:], gk_ref[...])
        dx, dg = vk(dk_ref[...])
        dz_ref[:, Q_LORA:] = dx.astype(bf16)
        _acc_row(dgk_ref, dg, first)

    return _call(body, "lat_norm_bwd", (t // tm,),
                 [pl.BlockSpec((tm, Q_LORA + KV_LORA), lambda i: (i, 0)), pl.BlockSpec((1, Q_LORA), lambda i: (0, 0)),
                  pl.BlockSpec((1, KV_LORA), lambda i: (0, 0)), pl.BlockSpec((tm, Q_LORA), lambda i: (i, 0)),
                  pl.BlockSpec((tm, KV_LORA), lambda i: (i, 0))],
                 [pl.BlockSpec((tm, Q_LORA + KV_LORA), lambda i: (i, 0)), pl.BlockSpec((8, Q_LORA), lambda i: (0, 0)),
                  pl.BlockSpec((8, KV_LORA), lambda i: (0, 0))],
                 [_sds((t, Q_LORA + KV_LORA), bf16), _sds((8, Q_LORA), f32), _sds((8, KV_LORA), f32)])(z_main, g_qa, g_kva, dqa, dkv)


def rope_tables(pos_col, inv_freq):
    t = pos_col.shape[0]
    tm = _tile(t, (512, 256, 128, 64))

    def body(p_ref, f_ref, c_ref, s_ref):
        ang = p_ref[...].astype(f32) * f_ref[...]
        lane = lax.broadcasted_iota(jnp.int32, ang.shape, 1)
        c_ref[...] = jnp.where(lane < ROPE, jnp.cos(ang), 0.0)
        sn = jnp.sin(ang)
        s_ref[...] = jnp.where(lane < ROPE // 2, -sn, jnp.where(lane < ROPE, sn, 0.0))

    return _call(body, "rope_tables", (t // tm,),
                 [pl.BlockSpec((tm, 1), lambda i: (i, 0)), pl.BlockSpec((1, TAIL), lambda i: (0, 0))],
                 [pl.BlockSpec((tm, TAIL), lambda i: (i, 0))] * 2, [_sds((t, TAIL), f32)] * 2)(pos_col, inv_freq)


def _swap_halves(n):
    lane = lax.broadcasted_iota(jnp.int32, n.shape, 1)
    return jnp.where(lane < ROPE // 2, pltpu.roll(n, TAIL - ROPE // 2, 1), pltpu.roll(n, ROPE // 2, 1))


def _rope(n, c, s):
    return n * c + _swap_halves(n) * s


def _rope_t(d, c, s):
    return d * c + _swap_halves(d * s)


def _prep_specs(tm):
    head = pl.BlockSpec((tm, HEAD_PAD), lambda i, h: (i, h))
    row = pl.BlockSpec((tm, TAIL), lambda i, h: (i, 0))
    gain = pl.BlockSpec((1, TAIL), lambda i, h: (0, 0))
    return head, row, gain


def _pe_in(zt):
    lane = lax.broadcasted_iota(jnp.int32, zt.shape, 1)
    return jnp.where(lane < ROPE, zt, 0.0)


def mla_prep(q_raw, kv_raw, z_tail, cos, sin, gqn, gqp, gkn, gkp):
    t = q_raw.shape[0]
    tm = _tile(t, (512, 256, 128, 64))

    def body(q_ref, kv_ref, zt_ref, c_ref, s_ref, gqn_ref, gqp_ref, gkn_ref, gkp_ref, qh_ref, kh_ref, vh_ref):
        c, s = c_ref[...], s_ref[...]
        qh_ref[:, :NOPE] = _rms(q_ref[:, :NOPE], gqn_ref[...]).astype(bf16)
        qh_ref[:, NOPE:] = _rope(_rms_pad(q_ref[:, NOPE:], gqp_ref[...], ROPE), c, s).astype(bf16)
        kh_ref[:, :NOPE] = _rms(kv_ref[:, :NOPE], gkn_ref[...]).astype(bf16)
        kh_ref[:, NOPE:] = _rope(_rms_pad(_pe_in(zt_ref[...]), gkp_ref[...], ROPE), c, s).astype(bf16)
        vh_ref[...] = kv_ref[:, NOPE:].astype(bf16)

    head, row, gain = _prep_specs(tm)
    return _call(body, "mla_prep", (t // tm, MLA_HEADS), [head, head, row, row, row, gain, gain, gain, gain],
                 [head, head, pl.BlockSpec((tm, V_HEAD), lambda i, h: (i, h))],
                 [_sds((t, MLA_HEADS * HEAD_PAD), bf16), _sds((t, MLA_HEADS * HEAD_PAD), bf16), _sds((t, MLA_HEADS * V_HEAD), bf16)],
                 )(q_raw, kv_raw, z_tail, cos, sin, gqn, gqp, gkn, gkp)


def mla_prep_bwd(q_raw, kv_raw, z_tail, cos, sin, gqn, gqp, gkn, gkp, dqh, dkh, dvh):
    t = q_raw.shape[0]
    tm = _tile(t, (512, 256, 128, 64))
    pad_norm = functools.partial(_rms_pad, width=ROPE)

    def body(q_ref, kv_ref, zt_ref, c_ref, s_ref, gqn_ref, gqp_ref, gkn_ref, gkp_ref, dqh_ref, dkh_ref, dvh_ref,
             dq_ref, dkv_ref, dzt_ref, dgqn_ref, dgqp_ref, dgkn_ref, dgkp_ref):
        i, h = pl.program_id(0), pl.program_id(1)
        first = _first(i, h)
        c, s = c_ref[...], s_ref[...]
        _, v1 = jax.vjp(_rms, q_ref[:, :NOPE], gqn_ref[...])
        dx, dg = v1(dqh_ref[:, :NOPE])
        dq_ref[:, :NOPE] = dx.astype(bf16)
        _acc_row(dgqn_ref, dg, first)
        _, v2 = jax.vjp(pad_norm, q_ref[:, NOPE:], gqp_ref[...])
        dx, dg = v2(_rope_t(dqh_ref[:, NOPE:], c, s))
        dq_ref[:, NOPE:] = dx.astype(bf16)
        _acc_row(dgqp_ref, dg, first)
        _, v3 = jax.vjp(_rms, kv_ref[:, :NOPE], gkn_ref[...])
        dx, dg = v3(dkh_ref[:, :NOPE])
        dkv_ref[:, :NOPE] = dx.astype(bf16)
        _acc_row(dgkn_ref, dg, first)
        dkv_ref[:, NOPE:] = dvh_ref[...].astype(bf16)
        _, v4 = jax.vjp(pad_norm, _pe_in(zt_ref[...]), gkp_ref[...])
        dx, dg = v4(_rope_t(dkh_ref[:, NOPE:], c, s))
        _acc_row(dgkp_ref, dg, first)

        @pl.when(h == 0)
        def _():
            dzt_ref[...] = jnp.zeros_like(dzt_ref)

        dzt_ref[...] += dx

    head, row, gain = _prep_specs(tm)
    acc = pl.BlockSpec((8, TAIL), lambda i, h: (0, 0))
    vspec = pl.BlockSpec((tm, V_HEAD), lambda i, h: (i, h))
    return _call(body, "mla_prep_bwd", (t // tm, MLA_HEADS),
                 [head, head, row, row, row, gain, gain, gain, gain, head, head, vspec],
                 [head, head, row, acc, acc, acc, acc],
                 [_sds((t, MLA_HEADS * HEAD_PAD), bf16), _sds((t, MLA_HEADS * HEAD_PAD), bf16), _sds((t, TAIL), f32)]
                 + [_sds((8, TAIL), f32)] * 4)(q_raw, kv_raw, z_tail, cos, sin, gqn, gqp, gkn, gkp, dqh, dkh, dvh)


ATT_BLOCK = 512
NEG = -1e30
ATT_SCALE = (NOPE + ROPE) ** -0.5


def _chunk_visible(shape, key_axis):
    kc = lax.broadcasted_iota(jnp.int32, shape, key_axis) >> CHUNK_SHIFT
    qc = lax.broadcasted_iota(jnp.int32, shape, 1 - key_axis) >> CHUNK_SHIFT
    return kc <= qc


def mla_fwd(qh, kh, vh, exchange=None):
    t = qh.shape[0]
    tb = min(ATT_BLOCK, t)
    nb = t // tb

    def body(q_ref, k_ref, v_ref, o_ref, lse_ref, m_s, l_s, acc):
        qi, ki = pl.program_id(1), pl.program_id(2)

        @pl.when(ki == 0)
        def _():
            m_s[...] = jnp.full_like(m_s, NEG)
            l_s[...] = jnp.zeros_like(l_s)
            acc[...] = jnp.zeros_like(acc)

        def step(diagonal):
            s = _bdot(k_ref[...], q_ref[...], 1, 1) * ATT_SCALE
            if diagonal:
                s = jnp.where(_chunk_visible(s.shape, 0), s, -jnp.inf)
            m_new = jnp.maximum(m_s[...], jnp.max(s, axis=0, keepdims=True))
            p = jnp.exp(s - m_new)
            alpha = jnp.exp(m_s[...] - m_new)
            l_s[...] = alpha * l_s[...] + jnp.sum(p, axis=0, keepdims=True)
            acc[...] = alpha * acc[...] + _bdot(v_ref[...], p, 0, 0)
            m_s[...] = m_new

        @pl.when(ki < qi)
        def _():
            step(False)

        @pl.when(ki == qi)
        def _():
            step(True)
            o_ref[...] = (acc[...] / l_s[...]).T
            lse_ref[0] = m_s[...] + jnp.log(l_s[...])

    kv = lambda h, qi, ki: (jnp.minimum(ki, qi), h)
    return _call(body, "mla_fwd", (MLA_HEADS, nb, nb),
                 [pl.BlockSpec((tb, HEAD_PAD), lambda h, qi, ki: (qi, h)), pl.BlockSpec((tb, HEAD_PAD), kv),
                  pl.BlockSpec((tb, V_HEAD), kv)],
                 [pl.BlockSpec((tb, V_HEAD), lambda h, qi, ki: (qi, h)), pl.BlockSpec((1, 1, tb), lambda h, qi, ki: (h, 0, qi))],
                 [_sds((t, MLA_HEADS * V_HEAD), f32), _sds((MLA_HEADS, 1, t), f32)],
                 [pltpu.VMEM((1, tb), f32), pltpu.VMEM((1, tb), f32), pltpu.VMEM((V_HEAD, tb), f32)], exchange=exchange)(qh, kh, vh)


def mla_delta(o, do):
    t = o.shape[0]
    tm = _tile(t, (512, 256, 128, 64))

    def body(o_ref, do_ref, d_ref):
        d_ref[0] = jnp.sum(o_ref[...] * do_ref[...], axis=1, keepdims=True)

    blk = pl.BlockSpec((tm, V_HEAD), lambda i, h: (i, h))
    return _call(body, "mla_delta", (t // tm, MLA_HEADS), [blk, blk], pl.BlockSpec((1, tm, 1), lambda i, h: (h, i, 0)),
                 _sds((MLA_HEADS, t, 1), f32))(o, do)


def mla_bwd(qh, kh, vh, do, lse_row, delta_row, exchange=None):
    t = qh.shape[0]
    tb = min(ATT_BLOCK, t)
    nb = t // tb

    def body(q_ref, k_ref, v_ref, do_ref, lse_ref, dl_ref, dq_ref, dk_ref, dv_ref, dk_acc, dv_acc):
        ki, qi = pl.program_id(1), pl.program_id(2)

        @pl.when(jnp.logical_and(ki == 0, qi == 0))
        def _():
            dq_ref[...] = jnp.zeros_like(dq_ref)

        @pl.when(qi == 0)
        def _():
            dk_acc[...] = jnp.zeros_like(dk_acc)
            dv_acc[...] = jnp.zeros_like(dv_acc)

        def step(diagonal):
            q, k, do_b = q_ref[...], k_ref[...], do_ref[...]
            s = _bdot(k, q, 1, 1) * ATT_SCALE
            if diagonal:
                s = jnp.where(_chunk_visible(s.shape, 0), s, -jnp.inf)
            p = jnp.exp(s - lse_ref[0])
            dp = _bdot(v_ref[...], do_b, 1, 1)
            ds = p * (dp - dl_ref[0]) * ATT_SCALE
            dv_acc[...] += _bdot(p, do_b, 1, 0)
            dk_acc[...] += _bdot(ds, q, 1, 0)
            rows = pl.ds(pl.multiple_of(qi * tb, tb), tb)
            dq_ref[rows, :] += _bdot(ds, k, 0, 0)

        @pl.when(qi > ki)
        def _():
            step(False)

        @pl.when(qi == ki)
        def _():
            step(True)

        @pl.when(qi == nb - 1)
        def _():
            dk_ref[...] = dk_acc[...]
            dv_ref[...] = dv_acc[...]

    qs = lambda h, ki, qi: (jnp.maximum(qi, ki), h)
    ks = lambda h, ki, qi: (ki, h)
    vec = pl.BlockSpec((1, 1, tb), lambda h, ki, qi: (h, 0, jnp.maximum(qi, ki)))
    return _call(body, "mla_bwd", (MLA_HEADS, nb, nb),
                 [pl.BlockSpec((tb, HEAD_PAD), qs), pl.BlockSpec((tb, HEAD_PAD), ks), pl.BlockSpec((tb, V_HEAD), ks),
                  pl.BlockSpec((tb, V_HEAD), qs), vec, vec],
                 [pl.BlockSpec((t, HEAD_PAD), lambda h, ki, qi: (0, h)), pl.BlockSpec((tb, HEAD_PAD), ks),
                  pl.BlockSpec((tb, V_HEAD), ks)],
                 [_sds((t, MLA_HEADS * HEAD_PAD), f32), _sds((t, MLA_HEADS * HEAD_PAD), f32), _sds((t, MLA_HEADS * V_HEAD), f32)],
                 [pltpu.VMEM((tb, HEAD_PAD), f32), pltpu.VMEM((tb, V_HEAD), f32)], exchange=exchange)(
        qh, kh, vh, do, lse_row, delta_row)


PAD = 8


def _conv_taps(pad_ref, w, width, t):
    y = pad_ref[PAD - width + 1:PAD - width + 1 + t, :] * w[0:1, :]
    for j in range(1, width):
        y = y + pad_ref[PAD - width + 1 + j:PAD - width + 1 + j + t, :] * w[j:j + 1, :]
    return y


def _conv_bwd(xpad_ref, dpad_ref, w, da, width, t):
    dpad_ref[0:t, :] = da
    dpad_ref[t:t + PAD, :] = jnp.zeros((PAD, da.shape[1]), f32)
    dx = dpad_ref[width - 1:width - 1 + t, :] * w[0:1, :]
    for j in range(1, width):
        dx = dx + dpad_ref[width - 1 - j:width - 1 - j + t, :] * w[j:j + 1, :]
    dws = [jnp.sum(da * xpad_ref[PAD - width + 1 + j:PAD - width + 1 + j + t, :], axis=0, keepdims=True) for j in range(width)]
    return dx, dws


def _load_pad(pad_ref, x, t):
    pad_ref[0:PAD, :] = jnp.zeros((PAD, x.shape[1]), f32)
    pad_ref[PAD:PAD + t, :] = x


def qk_conv(z_main, conv_qk):
    t = z_main.shape[0]
    nq = ML_QK // 128
    base = O_Q // 128

    def body(z_ref, w_ref, o_ref, pad):
        _load_pad(pad, z_ref[...], t)
        a = _conv_taps(pad, w_ref[...], ML_CONV, t)
        sc = jnp.where(pl.program_id(0) < nq, ML_DK ** -0.5, 1.0)
        o_ref[...] = jax.nn.silu(a) * sc

    return _call(body, "qk_conv", (2 * nq,),
                 [pl.BlockSpec((t, 128), lambda j: (0, base + j)), pl.BlockSpec((ML_CONV, 128), lambda j: (0, j))],
                 pl.BlockSpec((t, 128), lambda j: (0, j)), _sds((t, 2 * ML_QK), f32),
                 [pltpu.VMEM((t + PAD, 128), f32)])(z_main, conv_qk)


def qk_conv_bwd(z_main, conv_qk, dqk):
    t = z_main.shape[0]
    nq = ML_QK // 128
    base = O_Q // 128

    def body(z_ref, w_ref, d_ref, dz_ref, dw_ref, pad, dpad):
        _load_pad(pad, z_ref[...], t)
        w = w_ref[...]
        a = _conv_taps(pad, w, ML_CONV, t)
        sc = jnp.where(pl.program_id(0) < nq, ML_DK ** -0.5, 1.0)
        _, vjp = jax.vjp(jax.nn.silu, a)
        da, = vjp(d_ref[...] * sc)
        dx, dws = _conv_bwd(pad, dpad, w, da, ML_CONV, t)
        dz_ref[...] = dx.astype(bf16)
        for j in range(ML_CONV):
            dw_ref[j:j + 1, :] = dws[j]

    return _call(body, "qk_conv_bwd", (2 * nq,),
                 [pl.BlockSpec((t, 128), lambda j: (0, base + j)), pl.BlockSpec((ML_CONV, 128), lambda j: (0, j)),
                  pl.BlockSpec((t, 128), lambda j: (0, j))],
                 [pl.BlockSpec((t, 128), lambda j: (0, j)), pl.BlockSpec((ML_CONV, 128), lambda j: (0, j))],
                 [_sds((t, 2 * ML_QK), bf16), _sds((ML_CONV, 2 * ML_QK), f32)],
                 [pltpu.VMEM((t + PAD, 128), f32), pltpu.VMEM((t + PAD, 128), f32)])(z_main, conv_qk, dqk)


def glu_fwd(hup, conv_w, bias):
    t, f2 = hup.shape
    nf = f2 // 2 // 128

    def body(h1_ref, h2_ref, w1_ref, w2_ref, b1_ref, b2_ref, o_ref, pad):
        _load_pad(pad, h1_ref[...], t)
        a1 = _conv_taps(pad, w1_ref[...], FFN_CONV, t) + b1_ref[...]
        _load_pad(pad, h2_ref[...], t)
        a2 = _conv_taps(pad, w2_ref[...], FFN_CONV, t) + b2_ref[...]
        o_ref[...] = (jax.nn.silu(a1) * a2).astype(bf16)

    col = lambda off: pl.BlockSpec((t, 128), lambda j: (0, j + off))
    wsp = lambda off: pl.BlockSpec((FFN_CONV, 128), lambda j: (0, j + off))
    bsp = lambda off: pl.BlockSpec((1, 128), lambda j: (0, j + off))
    return _call(body, "glu_fwd", (nf,), [col(0), col(nf), wsp(0), wsp(nf), bsp(0), bsp(nf)], col(0), _sds((t, f2 // 2), bf16),
                 [pltpu.VMEM((t + PAD, 128), f32)])(hup, hup, conv_w, conv_w, bias, bias)


def glu_bwd(hup, conv_w, bias, dg, exchange=None):
    t, f2 = hup.shape
    nf = f2 // 2 // 128

    def body(hs_ref, hp_ref, ws_ref, wp_ref, bs_ref, bp_ref, dg_ref, dh_ref, dw_ref, db_ref, pad, ppad, dpad):
        _load_pad(pad, hs_ref[...], t)
        w = ws_ref[...]
        a_self = _conv_taps(pad, w, FFN_CONV, t) + bs_ref[...]
        _load_pad(ppad, hp_ref[...], t)
        a_part = _conv_taps(ppad, wp_ref[...], FFN_CONV, t) + bp_ref[...]
        d = dg_ref[...]
        _, vjp = jax.vjp(jax.nn.silu, a_self)
        d_first, = vjp(d * a_part)
        d_second = d * jax.nn.silu(a_part)
        da = jnp.where(pl.program_id(0) < nf, d_first, d_second)
        dx, dws = _conv_bwd(pad, dpad, w, da, FFN_CONV, t)
        dh_ref[...] = dx.astype(bf16)
        for j in range(FFN_CONV):
            dw_ref[j:j + 1, :] = dws[j]
        db_ref[...] = jnp.sum(da, axis=0, keepdims=True)

    part = lambda j: (j + nf) % (2 * nf)
    col = pl.BlockSpec((t, 128), lambda j: (0, j))
    pcol = pl.BlockSpec((t, 128), lambda j: (0, part(j)))
    wsp = pl.BlockSpec((FFN_CONV, 128), lambda j: (0, j))
    pwsp = pl.BlockSpec((FFN_CONV, 128), lambda j: (0, part(j)))
    bsp = pl.BlockSpec((1, 128), lambda j: (0, j))
    pbsp = pl.BlockSpec((1, 128), lambda j: (0, part(j)))
    return _call(body, "glu_bwd", (2 * nf,), [col, pcol, wsp, pwsp, bsp, pbsp, pl.BlockSpec((t, 128), lambda j: (0, j % nf))],
                 [col, wsp, bsp], [_sds((t, f2), bf16), _sds((FFN_CONV, f2), f32), _sds((1, f2), f32)],
                 [pltpu.VMEM((t + PAD, 128), f32)] * 3, exchange=exchange)(hup, hup, conv_w, conv_w, bias, bias, dg)


def gate_act(z_tail, b_tile):
    t = z_tail.shape[0]
    tm = _tile(t, (512, 256, 128, 64))

    def body(z_ref, b_ref, o_ref):
        x = z_ref[...] + b_ref[...]
        lane = lax.broadcasted_iota(jnp.int32, x.shape, 1)
        o_ref[...] = jnp.where(lane < T_F, x, jax.nn.log_sigmoid(x))

    row = pl.BlockSpec((tm, TAIL), lambda i: (i, 0))
    return _call(body, "gate_act", (t // tm,), [row, pl.BlockSpec((1, TAIL), lambda i: (0, 0))], row, _sds((t, TAIL), f32))(z_tail, b_tile)


def tail_bwd(z_tail, b_tile, dzt_pe, dgate):
    t = z_tail.shape[0]
    tm = _tile(t, (512, 256, 128, 64))

    def body(z_ref, b_ref, dpe_ref, dg_ref, dz_ref, db_ref):
        x = z_ref[...] + b_ref[...]
        lane = lax.broadcasted_iota(jnp.int32, x.shape, 1)
        _, vjp = jax.vjp(jax.nn.log_sigmoid, x)
        df, = vjp(dg_ref[...])
        dgates = jnp.where(lane < T_F, dg_ref[...], df)
        dgates = jnp.where(jnp.logical_and(lane >= T_I, lane < T_F + ML_HEADS), dgates, 0.0)
        dz_ref[...] = jnp.where(lane < ROPE, dpe_ref[...], dgates).astype(bf16)
        _acc_row(db_ref, jnp.sum(dgates, axis=0, keepdims=True), pl.program_id(0) == 0)

    row = pl.BlockSpec((tm, TAIL), lambda i: (i, 0))
    return _call(body, "tail_bwd", (t // tm,), [row, pl.BlockSpec((1, TAIL), lambda i: (0, 0)), row, row],
                 [row, pl.BlockSpec((8, TAIL), lambda i: (0, 0))], [_sds((t, TAIL), bf16), _sds((8, TAIL), f32)])(z_tail, b_tile, dzt_pe, dgate)


def _mlstm_step(q, k, v, igr, fgr, c_mat, n_vec, m):
    ln = CHUNK
    row = lax.broadcasted_iota(jnp.int32, (ln, ln), 0)
    col = lax.broadcasted_iota(jnp.int32, (ln, ln), 1)
    eye = row == col

    def to_col(r):
        return jnp.sum(jnp.where(eye, jnp.broadcast_to(r, (ln, ln)), 0.0), axis=1, keepdims=True)

    bc_r = jnp.sum(jnp.where(row <= col, jnp.broadcast_to(to_col(fgr), (ln, ln)), 0.0), axis=0, keepdims=True)
    bc_c = to_col(bc_r)
    logw = jnp.where(col <= row, bc_c - bc_r + igr, -jnp.inf)
    inter = bc_c + m
    m_t = jnp.maximum(inter, jnp.max(logw, axis=1, keepdims=True))
    w_intra = jnp.exp(logw - m_t)
    w_inter = jnp.exp(inter - m_t)
    sc = _bdot(q, k, 1, 1) * w_intra
    num = w_inter * _bdot(q, c_mat, 1, 0) + _bdot(sc, v, 1, 0)
    qn = jnp.sum(q.astype(bf16).astype(f32) * n_vec.astype(bf16).astype(f32), axis=1, keepdims=True)
    den = w_inter * qn + jnp.sum(sc, axis=1, keepdims=True)
    h = num / jnp.maximum(jnp.abs(den), jnp.exp(-m_t))
    lane = lax.broadcasted_iota(jnp.int32, (1, ln), 1)
    b_last = jnp.sum(jnp.where(lane == ln - 1, bc_r, 0.0), axis=1, keepdims=True)
    logu = b_last - bc_r + igr
    m_new = jnp.maximum(b_last + m, jnp.max(logu, axis=1, keepdims=True))
    decay = jnp.exp(b_last + m - m_new)
    u_c = to_col(jnp.exp(logu - m_new))
    c_new = decay * c_mat + _bdot(u_c * k, v, 0, 0)
    n_new = decay * n_vec + jnp.sum(u_c.astype(bf16).astype(f32) * k.astype(bf16).astype(f32), axis=0, keepdims=True)
    return h, c_new, n_new, m_new


ML_GROUP = 4
ML_NG = ML_HEADS // ML_GROUP


def _ml_specs(nc, rev):
    cc = (lambda c: nc - 1 - c) if rev else (lambda c: c)
    gq, gv = ML_GROUP * ML_DK, ML_GROUP * ML_DV
    assert ML_QK % gq == 0 and O_V % gv == 0
    q = pl.BlockSpec((CHUNK, gq), lambda g, c: (cc(c), g))
    k = pl.BlockSpec((CHUNK, gq), lambda g, c: (cc(c), ML_QK // gq + g))
    v = pl.BlockSpec((CHUNK, gv), lambda g, c: (cc(c), O_V // gv + g))
    hv = pl.BlockSpec((CHUNK, gv), lambda g, c: (cc(c), g))
    gate = pl.BlockSpec((ML_GROUP, 1, 1, CHUNK), lambda g, c: (g, cc(c), 0, 0))
    cm = pl.BlockSpec((ML_GROUP, 1, ML_DK, ML_DV), lambda g, c: (g, cc(c), 0, 0))
    nv = pl.BlockSpec((ML_GROUP, 1, 1, ML_DK), lambda g, c: (g, cc(c), 0, 0))
    ms = pl.BlockSpec((ML_GROUP, 1, 1, 1), lambda g, c: (g, cc(c), 0, 0))
    return q, k, v, hv, gate, cm, nv, ms


_ML_STATE = [pltpu.VMEM((ML_GROUP, ML_DK, ML_DV), f32), pltpu.VMEM((ML_GROUP, 1, ML_DK), f32), pltpu.VMEM((ML_GROUP, 1, 1), f32)]


def _ml_zero_state(c_s, n_s, m_s):
    @pl.when(pl.program_id(1) == 0)
    def _():
        c_s[...] = jnp.zeros_like(c_s)
        n_s[...] = jnp.zeros_like(n_s)
        m_s[...] = jnp.zeros_like(m_s)


def mlstm_fwd(qk_act, z_main, ig, fg, exchange=None):
    t = qk_act.shape[0]
    nc = t // CHUNK

    def body(q_ref, k_ref, v_ref, ig_ref, fg_ref, h_ref, c_out, n_out, m_out, c_s, n_s, m_s):
        _ml_zero_state(c_s, n_s, m_s)
        for j in range(ML_GROUP):
            qs, vs = slice(j * ML_DK, (j + 1) * ML_DK), slice(j * ML_DV, (j + 1) * ML_DV)
            c_out[j, 0] = c_s[j]
            n_out[j, 0] = n_s[j]
            m_out[j, 0] = m_s[j]
            h, c2, n2, m2 = _mlstm_step(q_ref[:, qs], k_ref[:, qs], v_ref[:, vs], ig_ref[j, 0], fg_ref[j, 0], c_s[j], n_s[j], m_s[j])
            h_ref[:, vs] = h
            c_s[j] = c2
            n_s[j] = n2
            m_s[j] = m2

    q, k, v, hv, gate, cm, nv, ms = _ml_specs(nc, False)
    return _call(body, "mlstm_fwd", (ML_NG, nc), [q, k, v, gate, gate], [hv, cm, nv, ms],
                 [_sds((t, ML_V), f32), _sds((ML_HEADS, nc, ML_DK, ML_DV), f32), _sds((ML_HEADS, nc, 1, ML_DK), f32),
                  _sds((ML_HEADS, nc, 1, 1), f32)], _ML_STATE, exchange=exchange)(qk_act, qk_act, z_main, ig, fg)


def mlstm_bwd(qk_act, z_main, ig, fg, c_all, n_all, m_all, dh, exchange=None):
    t = qk_act.shape[0]
    nc = t // CHUNK

    def body(q_ref, k_ref, v_ref, ig_ref, fg_ref, c_ref, n_ref, m_ref, dh_ref, dq_ref, dk_ref, dv_ref, dig_ref, dfg_ref,
             dc_s, dn_s, dm_s):
        _ml_zero_state(dc_s, dn_s, dm_s)
        for j in range(ML_GROUP):
            qs, vs = slice(j * ML_DK, (j + 1) * ML_DK), slice(j * ML_DV, (j + 1) * ML_DV)
            _, vjp = jax.vjp(_mlstm_step, q_ref[:, qs], k_ref[:, qs], v_ref[:, vs], ig_ref[j, 0], fg_ref[j, 0],
                             c_ref[j, 0], n_ref[j, 0], m_ref[j, 0])
            dq, dk, dv, dig, dfg, dc, dn, dm = vjp((dh_ref[:, vs], dc_s[j], dn_s[j], dm_s[j]))
            dq_ref[:, qs] = dq
            dk_ref[:, qs] = dk
            dv_ref[:, vs] = dv.astype(bf16)
            dig_ref[j, 0] = dig
            dfg_ref[j, 0] = dfg
            dc_s[j] = dc
            dn_s[j] = dn
            dm_s[j] = dm

    q, k, v, hv, gate, cm, nv, ms = _ml_specs(nc, True)
    gshape = _sds((ML_HEADS, nc, 1, CHUNK), f32)
    return _call(body, "mlstm_bwd", (ML_NG, nc), [q, k, v, gate, gate, cm, nv, ms, hv], [q, q, hv, gate, gate],
                 [_sds((t, ML_QK), f32), _sds((t, ML_QK), f32), _sds((t, ML_V), bf16), gshape, gshape],
                 _ML_STATE, exchange=exchange)(qk_act, qk_act, z_main, ig, fg, c_all, n_all, m_all, dh)


def _ml_out(h, zo, g):
    return _rms(h, g) * jax.nn.sigmoid(zo)


def mlstm_out(h, z_main, g_hnorm):
    t = h.shape[0]
    tm = _tile(t, (512, 256, 128, 64))
    zo = O_O // ML_DV

    def body(h_ref, z_ref, g_ref, y_ref):
        y_ref[...] = _ml_out(h_ref[...], z_ref[...], g_ref[0]).astype(bf16)

    blk = pl.BlockSpec((tm, ML_DV), lambda i, hd: (i, hd))
    return _call(body, "mlstm_out", (t // tm, ML_HEADS),
                 [blk, pl.BlockSpec((tm, ML_DV), lambda i, hd: (i, zo + hd)), pl.BlockSpec((1, 1, ML_DV), lambda i, hd: (hd, 0, 0))],
                 blk, _sds((t, ML_V), bf16))(h, z_main, g_hnorm)


def mlstm_out_bwd(h, z_main, g_hnorm, dy):
    t = h.shape[0]
    tm = _tile(t, (512, 256, 128, 64))
    zo = O_O // ML_DV

    def body(h_ref, z_ref, g_ref, dy_ref, dh_ref, dzo_ref, dg_ref):
        _, vjp = jax.vjp(_ml_out, h_ref[...], z_ref[...], g_ref[0])
        dh, dz, dg = vjp(dy_ref[...])
        dh_ref[...] = dh
        dzo_ref[...] = dz.astype(bf16)

        @pl.when(pl.program_id(1) == 0)
        def _():
            dg_ref[...] = jnp.zeros_like(dg_ref)

        dg_ref[0, 0:1, :] += dg

    blk = pl.BlockSpec((tm, ML_DV), lambda hd, i: (i, hd))
    return _call(body, "mlstm_out_bwd", (ML_HEADS, t // tm),
                 [blk, pl.BlockSpec((tm, ML_DV), lambda hd, i: (i, zo + hd)), pl.BlockSpec((1, 1, ML_DV), lambda hd, i: (hd, 0, 0)), blk],
                 [blk, blk, pl.BlockSpec((1, 8, ML_DV), lambda hd, i: (hd, 0, 0))],
                 [_sds((t, ML_V), f32), _sds((t, ML_V), bf16), _sds((ML_HEADS, 8, ML_DV), f32)])(h, z_main, g_hnorm, dy)


def _merge(ga, gb, ya, yb):
    return jax.nn.sigmoid(ga) * ya + jax.nn.sigmoid(gb) * yb


def _merge_specs(t, d):
    tm = _tile(t, (512, 256, 128, 64))
    bw = _tile(d, (512, 256, 128))
    assert O_GA % bw == 0 and (O_GA + d) % bw == 0
    blk = pl.BlockSpec((tm, bw), lambda i, j: (i, j))
    ga = pl.BlockSpec((tm, bw), lambda i, j: (i, O_GA // bw + j))
    gb = pl.BlockSpec((tm, bw), lambda i, j: (i, (O_GA + d) // bw + j))
    return tm, bw, blk, ga, gb


def merge_fwd(z_main, ya, yb):
    t, d = ya.shape
    tm, bw, blk, ga, gb = _merge_specs(t, d)

    def body(ga_ref, gb_ref, ya_ref, yb_ref, o_ref):
        o_ref[...] = _merge(ga_ref[...], gb_ref[...], ya_ref[...], yb_ref[...]).astype(bf16)

    return _call(body, "merge_fwd", (t // tm, d // bw), [ga, gb, blk, blk], blk, _sds((t, d), bf16))(z_main, z_main, ya, yb)


def merge_bwd(z_main, ya, yb, dmerged):
    t, d = ya.shape
    tm, bw, blk, ga, gb = _merge_specs(t, d)

    def body(ga_ref, gb_ref, ya_ref, yb_ref, dm_ref, dga_ref, dgb_ref, dya_ref, dyb_ref):
        _, vjp = jax.vjp(_merge, ga_ref[...], gb_ref[...], ya_ref[...], yb_ref[...])
        dga, dgb, dya, dyb = vjp(dm_ref[...])
        dga_ref[...] = dga.astype(bf16)
        dgb_ref[...] = dgb.astype(bf16)
        dya_ref[...] = dya.astype(bf16)
        dyb_ref[...] = dyb.astype(bf16)

    return _call(body, "merge_bwd", (t // tm, d // bw), [ga, gb, blk, blk, blk], [blk] * 4, [_sds((t, d), bf16)] * 4)(
        z_main, z_main, ya, yb, dmerged)


def _cross(cq, ck, cv, gq, gk):
    outs = []
    for hd in range(CR_HEADS):
        sl = slice(hd * CR_HD, (hd + 1) * CR_HD)
        q = _rms(cq[:, sl], gq)
        k = _rms(ck[:, sl], gk)
        s = _bdot(q, k, 1, 1) * (CR_HD ** -0.5)
        p = jax.nn.softmax(s, axis=-1)
        outs.append(_bdot(p, cv[:, sl], 1, 0))
    return jnp.concatenate(outs, axis=1)


def cross_fwd(cq, ck, cv, gq, gk):
    t, w = cq.shape
    nm = ck.shape[0]
    tm = _tile(t, (512, 256, 128, 64))

    def body(q_ref, k_ref, v_ref, gq_ref, gk_ref, o_ref):
        o_ref[...] = _cross(q_ref[...], k_ref[...], v_ref[...], gq_ref[...], gk_ref[...]).astype(bf16)

    row = pl.BlockSpec((tm, w), lambda i: (i, 0))
    full = pl.BlockSpec((nm, w), lambda i: (0, 0))
    gain = pl.BlockSpec((1, CR_HD), lambda i: (0, 0))
    return _call(body, "cross_fwd", (t // tm,), [row, full, full, gain, gain], row, _sds((t, w), bf16))(cq, ck, cv, gq, gk)


def cross_bwd(cq, ck, cv, gq, gk, do):
    t, w = cq.shape
    nm = ck.shape[0]
    tm = _tile(t, (512, 256, 128, 64))

    def body(q_ref, k_ref, v_ref, gq_ref, gk_ref, do_ref, dq_ref, dk_ref, dv_ref, dgq_ref, dgk_ref):
        first = pl.program_id(0) == 0
        _, vjp = jax.vjp(_cross, q_ref[...], k_ref[...], v_ref[...], gq_ref[...], gk_ref[...])
        dq, dk, dv, dgq, dgk = vjp(do_ref[...])
        dq_ref[...] = dq.astype(bf16)

        @pl.when(first)
        def _():
            dk_ref[...] = jnp.zeros_like(dk_ref)
            dv_ref[...] = jnp.zeros_like(dv_ref)

        dk_ref[...] += dk
        dv_ref[...] += dv
        _acc_row(dgq_ref, dgq, first)
        _acc_row(dgk_ref, dgk, first)

    row = pl.BlockSpec((tm, w), lambda i: (i, 0))
    full = pl.BlockSpec((nm, w), lambda i: (0, 0))
    gain = pl.BlockSpec((1, CR_HD), lambda i: (0, 0))
    acc = pl.BlockSpec((8, CR_HD), lambda i: (0, 0))
    return _call(body, "cross_bwd", (t // tm,), [row, full, full, gain, gain, row], [row, full, full, acc, acc],
                 [_sds((t, w), bf16), _sds((nm, w), f32), _sds((nm, w), f32), _sds((8, CR_HD), f32), _sds((8, CR_HD), f32)])(
        cq, ck, cv, gq, gk, do)


def loss_head(x2, fo, target):
    t, d = x2.shape
    tm = _tile(t, (256, 128, 64, 32, 16, 8))

    def body(a_ref, b_ref, t_ref, dx_ref, l_ref):
        err = a_ref[...] + b_ref[...] - t_ref[...]
        dx_ref[...] = err / d
        part = 0.5 * jnp.sum(jnp.mean(err * err, axis=1, keepdims=True), axis=0, keepdims=True)
        _acc_row(l_ref, jnp.broadcast_to(part, (1, 128)), pl.program_id(0) == 0)

    row = pl.BlockSpec((tm, d), lambda i: (i, 0))
    return _call(body, "loss_head", (t // tm,), [row, row, row], [row, pl.BlockSpec((8, 128), lambda i: (0, 0))],
                 [_sds((t, d), f32), _sds((8, 128), f32)])(x2, fo, target)


def _place():
    x, y, c = lax.axis_index("x"), lax.axis_index("y"), lax.axis_index("c")
    peers = []
    for k in range(1, N_DEV):
        px = 1 - x if k & 4 else x
        py = 1 - y if k & 2 else y
        pc = 1 - c if k & 1 else c
        peers.append(((px, py, pc), 4 * px + 2 * py + pc))
    return 4 * x + 2 * y + c, peers


N_REL = N_DEV - 1


def _exchange_ops(ins, outs, sems, scatter):
    n = len(ins)
    send_sems, recv_sems, local_sems = sems

    def copies(with_arrivals):
        me, peers = _place()

        def src(a, idx):
            return ins[a].at[idx] if scatter else ins[a]

        def remote(a, k, src_idx, dst_idx):
            return pltpu.make_async_remote_copy(
                src_ref=src(a, src_idx), dst_ref=outs[a].at[dst_idx], send_sem=send_sems.at[a * N_REL + k],
                recv_sem=recv_sems.at[a * N_REL + k], device_id=peers[k][0], device_id_type=MESH)

        local = [pltpu.make_async_copy(src(a, me), outs[a].at[me], local_sems.at[a]) for a in range(n)]
        sends = [remote(a, k, peers[k][1], me) for a in range(n) for k in range(N_REL)]
        arrivals = [remote(a, k, me, peers[k][1]) for a in range(n) for k in range(N_REL)] if with_arrivals else []
        return local, sends, arrivals

    def start():
        local, sends, _ = copies(False)
        for cp in local + sends:
            cp.start()

    def wait():
        local, sends, arrivals = copies(True)
        for cp in arrivals:
            cp.wait_recv()
        for cp in sends:
            cp.wait_send()
        for cp in local:
            cp.wait()

    return start, wait


def _exchange_shapes(arrs, scatter):
    return [_sds(a.shape if scatter else (N_DEV,) + a.shape, a.dtype) for a in arrs]


def _exchange_sems(n):
    return [pltpu.SemaphoreType.DMA((n * N_REL,)), pltpu.SemaphoreType.DMA((n * N_REL,)), pltpu.SemaphoreType.DMA((n,))]


def _exchange(arrs, name, scatter):
    n = len(arrs)

    def body(*refs):
        start, wait = _exchange_ops(refs[:n], refs[n:2 * n], refs[2 * n:], scatter)
        start()
        wait()

    any_spec = pl.BlockSpec(memory_space=pl.ANY)
    return pl.pallas_call(body, name=name, in_specs=[any_spec] * n, out_specs=[any_spec] * n,
                          out_shape=_exchange_shapes(arrs, scatter), scratch_shapes=_exchange_sems(n))(*arrs)


def cast_bf16(w, name):
    r, c = w.shape
    tr = _tile(r, (256, 128, 64, 32, 16))

    def body(w_ref, o_ref):
        o_ref[...] = w_ref[...].astype(bf16)

    blk = pl.BlockSpec((tr, c), lambda i: (i, 0))
    return _call(body, name, (r // tr,), [blk], blk, _sds((r, c), bf16))(w)


def _adamw(w, g, m, v):
    m = ADAM_B1 * m + (1.0 - ADAM_B1) * g
    v = ADAM_B2 * v + (1.0 - ADAM_B2) * jnp.square(g)
    m_hat = m / (1.0 - ADAM_B1 ** ADAM_STEP)
    v_hat = v / (1.0 - ADAM_B2 ** ADAM_STEP)
    delta = -ADAM_LR * (m_hat / (jnp.sqrt(v_hat) + ADAM_EPS) + ADAM_WD * w)
    return delta, m, v


def adam_sum(parts, w, m, v, name):
    _, r, c = parts.shape
    budget = 4 * 1024 * 1024
    tr = r
    for cand in (1024, 512, 256, 128, 64, 32, 16):
        if r % cand == 0 and N_DEV * cand * c * 4 <= budget:
            tr = cand
            break

    def body(p_ref, w_ref, m_ref, v_ref, g_ref, d_ref, m2_ref, v2_ref):
        g = p_ref[0].astype(f32)
        for k in range(1, N_DEV):
            g = g + p_ref[k].astype(f32)
        d, m2, v2 = _adamw(w_ref[...], g, m_ref[...], v_ref[...])
        g_ref[...] = g
        d_ref[...] = d
        m2_ref[...] = m2
        v2_ref[...] = v2

    blk = pl.BlockSpec((tr, c), lambda i: (i, 0))
    return _call(body, name, (r // tr,), [pl.BlockSpec((N_DEV, tr, c), lambda i: (0, i, 0)), blk, blk, blk], [blk] * 4,
                 [_sds((r, c), f32)] * 4)(parts, w, m, v)


def sum_parts(parts, name):
    _, r, c = parts.shape

    def body(p_ref, o_ref):
        g = p_ref[0]
        for k in range(1, N_DEV):
            g = g + p_ref[k]
        o_ref[...] = g

    return pl.pallas_call(body, name=name, out_shape=_sds((r, c), f32))(parts)


def adam_flat(w, g, m, v, name):
    def body(w_ref, g_ref, m_ref, v_ref, d_ref, m2_ref, v2_ref):
        d, m2, v2 = _adamw(w_ref[...], g_ref[...], m_ref[...], v_ref[...])
        d_ref[...] = d
        m2_ref[...] = m2
        v2_ref[...] = v2

    return pl.pallas_call(body, name=name, out_shape=[_sds(w.shape, f32)] * 3)(w, g, m, v)


def _pack(vecs, multiple):
    flat = jnp.concatenate([v.reshape(-1) for v in vecs])
    n = flat.shape[0]
    total = -(-n // multiple) * multiple
    return jnp.pad(flat, (0, total - n))


def _unpack(flat, shapes):
    out, pos = [], 0
    for s in shapes:
        n = 1
        for d in s:
            n *= d
        out.append(flat[pos:pos + n].reshape(s))
        pos += n
    return out


def _pad_lanes(v, width=TAIL):
    return jnp.pad(v, ((0, 0), (0, width - v.shape[1])))


def kernel(x, mem, positions, g_mix, w_in, g_qa, w_qb, g_kva, w_kvb, g_qn_nope, g_qn_pe, g_kn_nope, g_kn_pe, conv_qk, b_if, g_hnorm, p_a, p_b, w_out, g_cross, g_mem, wq_c, wk_c, wv_c, g_cq, g_ck, wo_c, g_ffn, w_up, conv_ffn, b_conv_ffn, w_down, loss_target, m_g_mix, m_w_in, m_g_qa, m_w_qb, m_g_kva, m_w_kvb, m_g_qn_nope, m_g_qn_pe, m_g_kn_nope, m_g_kn_pe, m_conv_qk, m_b_if, m_g_hnorm, m_p_a, m_p_b, m_w_out, m_g_cross, m_g_mem, m_wq_c, m_wk_c, m_wv_c, m_g_cq, m_g_ck, m_wo_c, m_g_ffn, m_w_up, m_conv_ffn, m_b_conv_ffn, m_w_down, v_g_mix, v_w_in, v_g_qa, v_w_qb, v_g_kva, v_w_kvb, v_g_qn_nope, v_g_qn_pe, v_g_kn_nope, v_g_kn_pe, v_conv_qk, v_b_if, v_g_hnorm, v_p_a, v_p_b, v_w_out, v_g_cross, v_g_mem, v_wq_c, v_wk_c, v_wv_c, v_g_cq, v_g_ck, v_wo_c, v_g_ffn, v_w_up, v_conv_ffn, v_b_conv_ffn, v_w_down):
    args = dict(locals())
    names = ['g_mix', 'w_in', 'g_qa', 'w_qb', 'g_kva', 'w_kvb', 'g_qn_nope', 'g_qn_pe', 'g_kn_nope', 'g_kn_pe', 'conv_qk', 'b_if',
             'g_hnorm', 'p_a', 'p_b', 'w_out', 'g_cross', 'g_mem', 'wq_c', 'wk_c', 'wv_c', 'g_cq', 'g_ck', 'wo_c', 'g_ffn', 'w_up',
             'conv_ffn', 'b_conv_ffn', 'w_down']
    big = ['w_in', 'w_qb', 'w_kvb', 'p_a', 'p_b', 'w_out', 'wq_c', 'wk_c', 'wv_c', 'wo_c', 'w_up', 'w_down']
    sharded_small = ['conv_qk', 'g_hnorm', 'conv_ffn']
    replicated = [n for n in names if n not in big and n not in sharded_small]

    t, d = x.shape[1], x.shape[2]
    x2d, tgt = x[0], loss_target[0]
    mem2d = mem[0]
    me = 4 * lax.axis_index("x") + 2 * lax.axis_index("y") + lax.axis_index("c")
    nc = t // CHUNK
    f2 = b_conv_ffn.shape[1]
    wmain = O_GA + 2 * d

    first = ['w_in', 'w_qb', 'w_kvb']
    behind_in = ['p_a', 'p_b', 'w_out', 'wq_c', 'wk_c', 'wv_c', 'wo_c']
    shards = {n: cast_bf16(args[n][0], "cast_" + n) for n in big}
    small_local = _pack([args[n] for n in sharded_small], 128).reshape(1, -1)
    gathered = _exchange([shards[n] for n in first] + [small_local], "comm_gather_first", scatter=False)
    gw = dict(zip(first, gathered[:-1]))
    small_all = gathered[-1]
    small_shapes = [args[n].shape for n in sharded_small]
    per_dev = [_unpack(small_all[k, 0], small_shapes) for k in range(N_DEV)]
    conv_qk_f = jnp.concatenate([p[0] for p in per_dev], axis=-1)[0]
    g_hnorm_f = jnp.concatenate([p[1] for p in per_dev], axis=-1)[0]
    conv_ffn_f = jnp.concatenate([p[2] for p in per_dev], axis=-1)[0]

    w_in_f = gw['w_in'].transpose(1, 0, 2).reshape(d, -1)
    c_kpe, c_q, c_i, c_o = O_Q, O_Q + ROPE, O_Q + ROPE + 2 * ML_QK + ML_V, O_Q + ROPE + 2 * ML_QK + ML_V + 2 * ML_HEADS
    w_main = jnp.concatenate([w_in_f[:, :c_kpe], w_in_f[:, c_q:c_i], w_in_f[:, c_o:]], axis=1)[None]
    w_tail = jnp.concatenate([w_in_f[:, c_kpe:c_q], w_in_f[:, c_i:c_o],
                              jnp.zeros((d, TAIL - ROPE - 2 * ML_HEADS), bf16)], axis=1)[None]
    assert w_main.shape[2] == wmain
    qb = gw['w_qb'].transpose(1, 0, 2).reshape(Q_LORA, MLA_HEADS, NOPE + ROPE)
    w_qb_p = jnp.concatenate([qb, jnp.zeros((Q_LORA, MLA_HEADS, HEAD_PAD - NOPE - ROPE), bf16)], axis=2).reshape(1, Q_LORA, -1)
    w_kvb3 = gw['w_kvb']

    inv_freq = ROPE_BASE ** (-jnp.arange(0, ROPE, 2, dtype=f32) / ROPE)
    inv_tile = _pad_lanes(jnp.concatenate([inv_freq, inv_freq])[None])
    cos, sin = rope_tables(positions.reshape(t, 1), inv_tile)
    gqp, gkp = _pad_lanes(g_qn_pe), _pad_lanes(g_kn_pe)
    b_tile = jnp.pad(b_if, ((0, 0), (T_I, TAIL - T_I - 2 * ML_HEADS)))

    u0 = rms_fwd(x2d, g_mix, "rms_mix")
    z_main, got = mm_nn(u0, w_main, f32, "mm_in_main", exchange=([shards[n] for n in behind_in], False))
    gw.update(zip(behind_in, got))
    p_a3, p_b3, w_out3 = (gw[n].reshape(1, -1, d) for n in ('p_a', 'p_b', 'w_out'))
    wq_c3, wk_c3, wv_c3 = (gw[n].reshape(1, d, -1) for n in ('wq_c', 'wk_c', 'wv_c'))
    wo_c3 = gw['wo_c']
    z_tail = mm_nn(u0, w_tail, f32, "mm_in_tail")
    qa_n, kv_n = lat_norm(z_main, g_qa, g_kva)
    q_raw = mm_nn(qa_n, w_qb_p, f32, "mm_qb")
    kv_raw = mm_nn(kv_n, w_kvb3, f32, "mm_kvb")
    qh, kh, vh = mla_prep(q_raw, kv_raw, z_tail, cos, sin, g_qn_nope, gqp, g_kn_nope, gkp)
    (o_a, lse), (w_up3,) = mla_fwd(qh, kh, vh, exchange=([shards['w_up']], False))

    qk_act = qk_conv(z_main, conv_qk_f)
    gates = gate_act(z_tail, b_tile)

    def to_rows(cols):
        return cols.T.reshape(ML_HEADS, nc, 1, CHUNK)

    ig, fg = to_rows(gates[:, T_I:T_F]), to_rows(gates[:, T_F:T_F + ML_HEADS])
    (h_ml, c_all, n_all, m_all), (w_down_g,) = mlstm_fwd(qk_act, z_main, ig, fg, exchange=([shards['w_down']], False))
    w_down3 = w_down_g.reshape(1, -1, d)
    g_hn3 = g_hnorm_f.reshape(ML_HEADS, 1, ML_DV)
    y_b = mlstm_out(h_ml, z_main, g_hn3)

    ya = mm_nn(o_a, p_a3, f32, "mm_pa")
    yb = mm_nn(y_b, p_b3, f32, "mm_pb")
    merged = merge_fwd(z_main, ya, yb)
    mo = mm_nn(merged, w_out3, f32, "mm_out")
    x1, uc = resid_rms(x2d, mo, g_cross, "resid_cross")
    mem_n = rms_fwd(mem2d, g_mem, "rms_mem")
    cq = mm_nn(uc, wq_c3, f32, "mm_cq")
    ck = mm_nn(mem_n, wk_c3, f32, "mm_ck")
    cv = mm_nn(mem_n, wv_c3, f32, "mm_cv")
    o_c = cross_fwd(cq, ck, cv, g_cq, g_ck)
    co = mm_nn(o_c, wo_c3, f32, "mm_oc")
    x2, u3 = resid_rms(x1, co, g_ffn, "resid_ffn")
    hup = mm_nn(u3, w_up3, f32, "mm_up")
    gl = glu_fwd(hup, conv_ffn_f, b_conv_ffn)
    fo = mm_nn(gl, w_down3, f32, "mm_down")
    dx3, loss_acc = loss_head(x2, fo, tgt)

    grads, parts = {}, {}
    grads['w_down'] = mm_tn(gl, dx3, 1, "mm_d_wdown").reshape(N_DEV, -1, d)
    dgl = mm_nt(dx3, w_down3, f32, "mm_d_gl")
    (dhup, dconv_ffn, db_ffn), (parts['w_down'],) = glu_bwd(hup, conv_ffn_f, b_conv_ffn, dgl, exchange=([grads['w_down']], True))
    grads['w_up'] = mm_tn(u3, dhup, N_DEV, "mm_d_wup")
    du3 = mm_nt(dhup, w_up3, f32, "mm_d_u3")
    dx2, dg_ffn = rms_bwd(x2, g_ffn, [du3], dx3, "rms_bwd_ffn")
    grads['wo_c'] = mm_tn(o_c, dx2, N_DEV, "mm_d_woc")
    do_c = mm_nt(dx2, wo_c3, f32, "mm_d_oc")
    dcq, dck, dcv, dg_cq, dg_ck = cross_bwd(cq, ck, cv, g_cq, g_ck, do_c)
    grads['wq_c'] = mm_tn(uc, dcq, 1, "mm_d_wqc").reshape(N_DEV, -1, dcq.shape[1])
    grads['wk_c'] = mm_tn(mem_n, dck, 1, "mm_d_wkc").reshape(N_DEV, -1, dck.shape[1])
    grads['wv_c'] = mm_tn(mem_n, dcv, 1, "mm_d_wvc").reshape(N_DEV, -1, dcv.shape[1])
    duc = mm_nt(dcq, wq_c3, f32, "mm_d_uc")
    dmem_k = mm_nt(dck, wk_c3, f32, "mm_d_memk")
    dmem_v = mm_nt(dcv, wv_c3, f32, "mm_d_memv")
    dg_mem, = rms_bwd(mem2d, g_mem, [dmem_k, dmem_v], None, "rms_bwd_mem", want_dx=False)
    dx1, dg_cross = rms_bwd(x1, g_cross, [duc], dx2, "rms_bwd_cross")
    grads['w_out'] = mm_tn(merged, dx1, 1, "mm_d_wout").reshape(N_DEV, -1, d)
    dmerged = mm_nt(dx1, w_out3, f32, "mm_d_merged")
    dga, dgb, dya, dyb = merge_bwd(z_main, ya, yb, dmerged)
    grads['p_a'] = mm_tn(o_a, dya, 1, "mm_d_pa").reshape(N_DEV, -1, d)
    grads['p_b'] = mm_tn(y_b, dyb, 1, "mm_d_pb").reshape(N_DEV, -1, d)
    do_a = mm_nt(dya, p_a3, f32, "mm_d_oa")
    dy_b = mm_nt(dyb, p_b3, f32, "mm_d_yb")

    dh_ml, dzo, dg_hn = mlstm_out_bwd(h_ml, z_main, g_hn3, dy_b)
    (dq_act, dk_act, dzv, dig, dfg), (parts['w_up'],) = mlstm_bwd(qk_act, z_main, ig, fg, c_all, n_all, m_all, dh_ml,
                                                                 exchange=([grads['w_up']], True))
    dqk = jnp.concatenate([dq_act, dk_act], axis=1)
    dzqk, dconv_qk = qk_conv_bwd(z_main, conv_qk_f, dqk)

    delta = mla_delta(o_a, do_a)
    (dqh, dkh, dvh), got = mla_bwd(qh, kh, vh, do_a, lse, delta.reshape(MLA_HEADS, 1, t),
                                   exchange=([grads[n] for n in behind_in], True))
    parts.update(zip(behind_in, got))
    dq_raw, dkv_raw, dzt_pe, dg_qn, dg_qp, dg_kn, dg_kp = mla_prep_bwd(
        q_raw, kv_raw, z_tail, cos, sin, g_qn_nope, gqp, g_kn_nope, gkp, dqh, dkh, dvh)
    d_wqb_p = mm_tn(qa_n, dq_raw, 1, "mm_d_wqb")[0].reshape(Q_LORA, MLA_HEADS, HEAD_PAD)[:, :, :NOPE + ROPE]
    grads['w_qb'] = d_wqb_p.reshape(Q_LORA, N_DEV, -1).transpose(1, 0, 2)
    grads['w_kvb'] = mm_tn(kv_n, dkv_raw, N_DEV, "mm_d_wkvb")
    dqa = mm_nt(dq_raw, w_qb_p, f32, "mm_d_qa")
    dkvn = mm_nt(dkv_raw, w_kvb3, f32, "mm_d_kvn")
    dz_lat, dg_qa, dg_kva = lat_norm_bwd(z_main, g_qa, g_kva, dqa, dkvn)

    def to_cols(rows):
        return rows.reshape(ML_HEADS, t).T

    dgate = jnp.pad(jnp.concatenate([to_cols(dig), to_cols(dfg)], axis=1), ((0, 0), (T_I, TAIL - T_I - 2 * ML_HEADS)))
    dz_tail, db_if = tail_bwd(z_tail, b_tile, dzt_pe, dgate)
    dz_main = jnp.concatenate([dz_lat, dzqk, dzv, dzo, dga, dgb], axis=1)
    d_wmain = mm_tn(u0, dz_main, 1, "mm_d_wmain")[0]
    d_wtail = mm_tn(u0, dz_tail, 1, "mm_d_wtail")[0]
    d_win = jnp.concatenate([d_wmain[:, :O_Q], d_wtail[:, :ROPE], d_wmain[:, O_Q:O_O], d_wtail[:, T_I:T_I + 2 * ML_HEADS],
                             d_wmain[:, O_O:]], axis=1)
    grads['w_in'] = d_win.reshape(d, N_DEV, -1).transpose(1, 0, 2)
    du0_a, got = mm_nt(dz_main, w_main, f32, "mm_d_u0_main", exchange=([grads[n] for n in first], True))
    parts.update(zip(first, got))
    du0_b = mm_nt(dz_tail, w_tail, f32, "mm_d_u0_tail")
    grad_x, dg_mix = rms_bwd(x2d, g_mix, [du0_a, du0_b], dx1, "rms_bwd_mix")

    out_g, out_d, out_m, out_v = {}, {}, {}, {}
    for n in big:
        shp = args[n].shape
        g, dl, m2, v2 = adam_sum(parts[n], args[n][0], args['m_' + n][0], args['v_' + n][0], "adam_" + n)
        out_g[n], out_d[n], out_m[n], out_v[n] = (a.reshape(shp) for a in (g, dl, m2, v2))

    small_full = {
        'g_mix': dg_mix[0], 'g_qa': dg_qa[0], 'g_kva': dg_kva[0], 'g_qn_nope': dg_qn[0], 'g_qn_pe': dg_qp[0, :ROPE],
        'g_kn_nope': dg_kn[0], 'g_kn_pe': dg_kp[0, :ROPE], 'conv_qk': dconv_qk, 'b_if': db_if[0, T_I:T_I + 2 * ML_HEADS],
        'g_hnorm': dg_hn[:, 0, :], 'g_cross': dg_cross[0], 'g_mem': dg_mem[0], 'g_cq': dg_cq[0], 'g_ck': dg_ck[0],
        'g_ffn': dg_ffn[0], 'conv_ffn': dconv_ffn, 'b_conv_ffn': db_ffn[0], 'loss': loss_acc[0, :1]}
    order = list(small_full)
    packed = _pack([small_full[n] for n in order], 8 * 128).reshape(1, -1)
    gathered_small, = _exchange([packed], "comm_gather_small", scatter=False)
    summed = sum_parts(gathered_small.reshape(N_DEV, -1, 128), "sum_small").reshape(-1)
    full_g = dict(zip(order, _unpack(summed, [small_full[n].shape for n in order])))
    loss = full_g['loss'][0]

    local_g = {}
    for n in replicated:
        local_g[n] = full_g[n].reshape(args[n].shape)
    for n in sharded_small:
        shp = args[n].shape
        full = full_g[n].reshape((1,) + full_g[n].shape)
        local_g[n] = lax.dynamic_slice_in_dim(full, me * shp[-1], shp[-1], axis=2)
    small = replicated + sharded_small
    dl_f, m_f, v_f = adam_flat(*[_pack([src[n] if pre == '' else args[pre + n] for n in small], 8 * 128).reshape(-1, 128)
                                 for pre, src in (('', args), ('', local_g), ('m_', None), ('v_', None))], "adam_small")
    shapes = [args[n].shape for n in small]
    for dst, flat in ((out_d, dl_f), (out_m, m_f), (out_v, v_f)):
        dst.update(zip(small, _unpack(flat.reshape(-1), shapes)))
    out_g.update(local_g)

    return (loss, grad_x[None], *[out_g[n] for n in names], *[out_d[n] for n in names],
            *[out_m[n] for n in names], *[out_v[n] for n in names])
```

```python
import functools

import jax
import jax.numpy as jnp
from jax import lax
from jax.experimental import pallas as pl
from jax.experimental.pallas import tpu as pltpu

f32 = jnp.float32
bf16 = jnp.bfloat16

N_DEV = 8
EPS = 1e-6
CHUNK = 64
CHUNK_SHIFT = 6
assert 1 << CHUNK_SHIFT == CHUNK
MLA_HEADS = 16
Q_LORA = 512
KV_LORA = 512
NOPE = 128
ROPE = 64
V_HEAD = 128
ROPE_BASE = 10000.0
HEAD_PAD = 256
ML_HEADS = 8
ML_DK = 128
ML_DV = 256
ML_CONV = 4
ML_QK = ML_HEADS * ML_DK
ML_V = ML_HEADS * ML_DV
CR_HEADS = 4
CR_HD = 128
FFN_CONV = 3
ADAM_LR = 0.001
ADAM_B1 = 0.9
ADAM_B2 = 0.999
ADAM_EPS = 1e-08
ADAM_WD = 0.01
ADAM_STEP = 10
O_QA, O_KV, O_Q, O_K = 0, Q_LORA, Q_LORA + KV_LORA, Q_LORA + KV_LORA + ML_QK
O_V = O_K + ML_QK
O_O = O_V + ML_V
O_GA = O_O + ML_V
TAIL = 128
T_I, T_F = ROPE, ROPE + ML_HEADS
VMEM_LIMIT_V7X = 48 * 1024 * 1024
MESH = pl.DeviceIdType.MESH


def _call(body, name, grid, in_specs, out_specs, out_shape, scratch=(), exchange=None):
    params = pltpu.CompilerParams(vmem_limit_bytes=VMEM_LIMIT_V7X)
    if exchange is None:
        return pl.pallas_call(body, name=name, grid=grid, in_specs=in_specs, out_specs=out_specs, out_shape=out_shape,
                              scratch_shapes=list(scratch), compiler_params=params)
    arrs, scatter = exchange
    single = not isinstance(out_specs, (list, tuple))
    o_specs = [out_specs] if single else list(out_specs)
    o_shape = [out_shape] if single else list(out_shape)
    n_in, n_out, n_sc, n = len(in_specs), len(o_specs), len(scratch), len(arrs)
    any_spec = pl.BlockSpec(memory_space=pl.ANY)

    def body_with_exchange(*refs):
        pos = [0]

        def take(k):
            pos[0] += k
            return refs[pos[0] - k:pos[0]]

        ins, ex_in, outs, ex_out, sc = take(n_in), take(n), take(n_out), take(n), take(n_sc)
        start, wait = _exchange_ops(ex_in, ex_out, refs[pos[0]:], scatter)
        ids = [pl.program_id(a) for a in range(len(grid))]
        pl.when(_first(*ids))(start)
        body(*ins, *outs, *sc)
        last = ids[0] == grid[0] - 1
        for a in range(1, len(grid)):
            last = jnp.logical_and(last, ids[a] == grid[a] - 1)
        pl.when(last)(wait)

    call = pl.pallas_call(body_with_exchange, name="comm_" + name, grid=grid, in_specs=list(in_specs) + [any_spec] * n,
                          out_specs=o_specs + [any_spec] * n, out_shape=o_shape + _exchange_shapes(arrs, scatter),
                          scratch_shapes=list(scratch) + _exchange_sems(n), compiler_params=params)

    def run(*operands):
        res = call(*operands, *arrs)
        return (res[0] if single else list(res[:n_out])), list(res[n_out:])

    return run


def _tile(n, cands):
    for c in cands:
        if n % c == 0:
            return c
    return n


def _sds(shape, dtype):
    return jax.ShapeDtypeStruct(tuple(shape), dtype)


def _bdot(a, b, ca, cb):
    return lax.dot_general(a.astype(bf16), b.astype(bf16), (((ca,), (cb,)), ((), ())), preferred_element_type=f32)


_BIG = (1024, 512, 256, 128)


def _col_tile(nb):
    return nb if nb <= 1536 else _tile(nb, _BIG)


def mm_nn(a, w3, out_dtype, name, exchange=None):
    m, k = a.shape
    nblk, k2, nb = w3.shape
    assert k == k2
    tm, tk, tn = _tile(m, _BIG), _tile(k, (512, 256, 128)), _col_tile(nb)
    per, nk = nb // tn, k // tk

    def body(a_ref, w_ref, o_ref, acc):
        kk = pl.program_id(2)

        @pl.when(kk == 0)
        def _():
            acc[...] = jnp.zeros_like(acc)

        acc[...] += _bdot(a_ref[...], w_ref[0], 1, 0)

        @pl.when(kk == nk - 1)
        def _():
            o_ref[...] = acc[...].astype(o_ref.dtype)

    return _call(body, name, (m // tm, nblk * per, nk),
                 [pl.BlockSpec((tm, tk), lambda i, j, kk: (i, kk)),
                  pl.BlockSpec((1, tk, tn), lambda i, j, kk: (j // per, kk, j % per))],
                 pl.BlockSpec((tm, tn), lambda i, j, kk: (i, j)), _sds((m, nblk * nb), out_dtype),
                 [pltpu.VMEM((tm, tn), f32)], exchange=exchange)(a, w3)


def mm_nt(a, w3, out_dtype, name, exchange=None):
    m, n = a.shape
    nblk, k, nb = w3.shape
    assert n == nblk * nb
    tm, tn = _tile(m, _BIG), _tile(k, _BIG)
    tc = nb if nb <= 1536 else _tile(nb, (512, 256, 128))
    per = nb // tc
    nk = nblk * per

    def body(a_ref, w_ref, o_ref, acc):
        kk = pl.program_id(2)

        @pl.when(kk == 0)
        def _():
            acc[...] = jnp.zeros_like(acc)

        acc[...] += _bdot(a_ref[...], w_ref[0], 1, 1)

        @pl.when(kk == nk - 1)
        def _():
            o_ref[...] = acc[...].astype(o_ref.dtype)

    return _call(body, name, (m // tm, k // tn, nk),
                 [pl.BlockSpec((tm, tc), lambda i, j, kk: (i, kk)),
                  pl.BlockSpec((1, tn, tc), lambda i, j, kk: (kk // per, j, kk % per))],
                 pl.BlockSpec((tm, tn), lambda i, j, kk: (i, j)), _sds((m, k), out_dtype),
                 [pltpu.VMEM((tm, tn), f32)], exchange=exchange)(a, w3)


def mm_tn(a, b, nblk, name):
    r, m = a.shape
    r2, n = b.shape
    assert r == r2 and n % nblk == 0
    nb = n // nblk
    tm, tk, tn = _tile(m, _BIG), _tile(r, (512, 256, 128)), _col_tile(nb)
    per, nk = nb // tn, r // tk

    def body(a_ref, b_ref, o_ref, acc):
        kk = pl.program_id(2)

        @pl.when(kk == 0)
        def _():
            acc[...] = jnp.zeros_like(acc)

        acc[...] += _bdot(a_ref[...], b_ref[...], 0, 0)

        @pl.when(kk == nk - 1)
        def _():
            o_ref[0] = acc[...].astype(bf16)

    return _call(body, name, (m // tm, nblk * per, nk),
                 [pl.BlockSpec((tk, tm), lambda i, j, kk: (kk, i)),
                  pl.BlockSpec((tk, tn), lambda i, j, kk: (kk, j))],
                 pl.BlockSpec((1, tm, tn), lambda i, j, kk: (j // per, i, j % per)), _sds((nblk, m, nb), bf16),
                 [pltpu.VMEM((tm, tn), f32)])(a, b)


def _rms(x, g):
    return x * lax.rsqrt(jnp.mean(x * x, axis=-1, keepdims=True) + EPS) * g


def _rms_pad(x, g, width):
    return x * lax.rsqrt(jnp.sum(x * x, axis=-1, keepdims=True) / width + EPS) * g


def _first(*ids):
    ok = ids[0] == 0
    for i in ids[1:]:
        ok = jnp.logical_and(ok, i == 0)
    return ok


def _acc_row(ref, val, first):
    @pl.when(first)
    def _():
        ref[...] = jnp.zeros_like(ref)

    ref[0:1, :] += val


def rms_fwd(x, g, name):
    r, w = x.shape
    tm = _tile(r, (256, 128, 64, 32, 16, 8))

    def body(x_ref, g_ref, o_ref):
        o_ref[...] = _rms(x_ref[...], g_ref[...]).astype(bf16)

    return _call(body, name, (r // tm,), [pl.BlockSpec((tm, w), lambda i: (i, 0)), pl.BlockSpec((1, w), lambda i: (0, 0))],
                 pl.BlockSpec((tm, w), lambda i: (i, 0)), _sds((r, w), bf16))(x, g)


def resid_rms(xa, xb, g, name):
    r, w = xa.shape
    tm = _tile(r, (256, 128, 64, 32, 16, 8))

    def body(a_ref, b_ref, g_ref, s_ref, u_ref):
        xs = a_ref[...] + b_ref[...]
        s_ref[...] = xs
        u_ref[...] = _rms(xs, g_ref[...]).astype(bf16)

    row = pl.BlockSpec((tm, w), lambda i: (i, 0))
    return _call(body, name, (r // tm,), [row, row, pl.BlockSpec((1, w), lambda i: (0, 0))], [row, row],
                 [_sds((r, w), f32), _sds((r, w), bf16)])(xa, xb, g)


def rms_bwd(x, g, dys, dres, name, want_dx=True, want_b16=False):
    r, w = x.shape
    tm = _tile(r, (256, 128, 64, 32, 16, 8))
    nd = len(dys)

    def body(*refs):
        x_ref, g_ref = refs[0], refs[1]
        dy = refs[2][...]
        for j in range(1, nd):
            dy = dy + refs[2 + j][...]
        pos = 2 + nd
        _, vjp = jax.vjp(_rms, x_ref[...], g_ref[...])
        dx, dg = vjp(dy)
        if dres is not None:
            dx = dx + refs[pos][...]
            pos += 1
        if want_dx:
            refs[pos][...] = dx
            pos += 1
        if want_b16:
            refs[pos][...] = dx.astype(bf16)
            pos += 1
        _acc_row(refs[pos], dg, pl.program_id(0) == 0)

    row = pl.BlockSpec((tm, w), lambda i: (i, 0))
    ins = [x, g] + list(dys) + ([dres] if dres is not None else [])
    in_specs = [row, pl.BlockSpec((1, w), lambda i: (0, 0))] + [row] * (nd + (dres is not None))
    out_specs = [row] * (want_dx + want_b16) + [pl.BlockSpec((8, w), lambda i: (0, 0))]
    out_shape = ([_sds((r, w), f32)] if want_dx else []) + ([_sds((r, w), bf16)] if want_b16 else []) + [_sds((8, w), f32)]
    return _call(body, name, (r // tm,), in_specs, out_specs, out_shape)(*ins)


def lat_norm(z_main, g_qa, g_kva):
    t = z_main.shape[0]
    tm = _tile(t, (512, 256, 128, 64))

    def body(z_ref, gq_ref, gk_ref, q_ref, k_ref):
        q_ref[...] = _rms(z_ref[:, :Q_LORA], gq_ref[...]).astype(bf16)
        k_ref[...] = _rms(z_ref[:, Q_LORA:], gk_ref[...]).astype(bf16)

    return _call(body, "lat_norm", (t // tm,),
                 [pl.BlockSpec((tm, Q_LORA + KV_LORA), lambda i: (i, 0)), pl.BlockSpec((1, Q_LORA), lambda i: (0, 0)),
                  pl.BlockSpec((1, KV_LORA), lambda i: (0, 0))],
                 [pl.BlockSpec((tm, Q_LORA), lambda i: (i, 0)), pl.BlockSpec((tm, KV_LORA), lambda i: (i, 0))],
                 [_sds((t, Q_LORA), bf16), _sds((t, KV_LORA), bf16)])(z_main, g_qa, g_kva)


def lat_norm_bwd(z_main, g_qa, g_kva, dqa, dkv):
    t = z_main.shape[0]
    tm = _tile(t, (512, 256, 128, 64))

    def body(z_ref, gq_ref, gk_ref, dq_ref, dk_ref, dz_ref, dgq_ref, dgk_ref):
        first = pl.program_id(0) == 0
        _, vq = jax.vjp(_rms, z_ref[:, :Q_LORA], gq_ref[...])
        dx, dg = vq(dq_ref[...])
        dz_ref[:, :Q_LORA] = dx.astype(bf16)
        _acc_row(dgq_ref, dg, first)
        _, vk = jax.vjp(_rms, z_ref[:, Q_LORA:], gk_ref[...])
        dx, dg = vk(dk_ref[...])
        dz_ref[:, Q_LORA:] = dx.astype(bf16)
        _acc_row(dgk_ref, dg, first)

    return _call(body, "lat_norm_bwd", (t // tm,),
                 [pl.BlockSpec((tm, Q_LORA + KV_LORA), lambda i: (i, 0)), pl.BlockSpec((1, Q_LORA), lambda i: (0, 0)),
                  pl.BlockSpec((1, KV_LORA), lambda i: (0, 0)), pl.BlockSpec((tm, Q_LORA), lambda i: (i, 0)),
                  pl.BlockSpec((tm, KV_LORA), lambda i: (i, 0))],
                 [pl.BlockSpec((tm, Q_LORA + KV_LORA), lambda i: (i, 0)), pl.BlockSpec((8, Q_LORA), lambda i: (0, 0)),
                  pl.BlockSpec((8, KV_LORA), lambda i: (0, 0))],
                 [_sds((t, Q_LORA + KV_LORA), bf16), _sds((8, Q_LORA), f32), _sds((8, KV_LORA), f32)])(z_main, g_qa, g_kva, dqa, dkv)


def rope_tables(pos_col, inv_freq):
    t = pos_col.shape[0]
    tm = _tile(t, (512, 256, 128, 64))

    def body(p_ref, f_ref, c_ref, s_ref):
        ang = p_ref[...].astype(f32) * f_ref[...]
        lane = lax.broadcasted_iota(jnp.int32, ang.shape, 1)
        c_ref[...] = jnp.where(lane < ROPE, jnp.cos(ang), 0.0)
        sn = jnp.sin(ang)
        s_ref[...] = jnp.where(lane < ROPE // 2, -sn, jnp.where(lane < ROPE, sn, 0.0))

    return _call(body, "rope_tables", (t // tm,),
                 [pl.BlockSpec((tm, 1), lambda i: (i, 0)), pl.BlockSpec((1, TAIL), lambda i: (0, 0))],
                 [pl.BlockSpec((tm, TAIL), lambda i: (i, 0))] * 2, [_sds((t, TAIL), f32)] * 2)(pos_col, inv_freq)


def _swap_halves(n):
    lane = lax.broadcasted_iota(jnp.int32, n.shape, 1)
    return jnp.where(lane < ROPE // 2, pltpu.roll(n, TAIL - ROPE // 2, 1), pltpu.roll(n, ROPE // 2, 1))


def _rope(n, c, s):
    return n * c + _swap_halves(n) * s


def _rope_t(d, c, s):
    return d * c + _swap_halves(d * s)


def _prep_specs(tm):
    head = pl.BlockSpec((tm, HEAD_PAD), lambda i, h: (i, h))
    row = pl.BlockSpec((tm, TAIL), lambda i, h: (i, 0))
    gain = pl.BlockSpec((1, TAIL), lambda i, h: (0, 0))
    return head, row, gain


def _pe_in(zt):
    lane = lax.broadcasted_iota(jnp.int32, zt.shape, 1)
    return jnp.where(lane < ROPE, zt, 0.0)


def mla_prep(q_raw, kv_raw, z_tail, cos, sin, gqn, gqp, gkn, gkp):
    t = q_raw.shape[0]
    tm = _tile(t, (512, 256, 128, 64))

    def body(q_ref, kv_ref, zt_ref, c_ref, s_ref, gqn_ref, gqp_ref, gkn_ref, gkp_ref, qh_ref, kh_ref, vh_ref):
        c, s = c_ref[...], s_ref[...]
        qh_ref[:, :NOPE] = _rms(q_ref[:, :NOPE], gqn_ref[...]).astype(bf16)
        qh_ref[:, NOPE:] = _rope(_rms_pad(q_ref[:, NOPE:], gqp_ref[...], ROPE), c, s).astype(bf16)
        kh_ref[:, :NOPE] = _rms(kv_ref[:, :NOPE], gkn_ref[...]).astype(bf16)
        kh_ref[:, NOPE:] = _rope(_rms_pad(_pe_in(zt_ref[...]), gkp_ref[...], ROPE), c, s).astype(bf16)
        vh_ref[...] = kv_ref[:, NOPE:].astype(bf16)

    head, row, gain = _prep_specs(tm)
    return _call(body, "mla_prep", (t // tm, MLA_HEADS), [head, head, row, row, row, gain, gain, gain, gain],
                 [head, head, pl.BlockSpec((tm, V_HEAD), lambda i, h: (i, h))],
                 [_sds((t, MLA_HEADS * HEAD_PAD), bf16), _sds((t, MLA_HEADS * HEAD_PAD), bf16), _sds((t, MLA_HEADS * V_HEAD), bf16)],
                 )(q_raw, kv_raw, z_tail, cos, sin, gqn, gqp, gkn, gkp)


def mla_prep_bwd(q_raw, kv_raw, z_tail, cos, sin, gqn, gqp, gkn, gkp, dqh, dkh, dvh):
    t = q_raw.shape[0]
    tm = _tile(t, (512, 256, 128, 64))
    pad_norm = functools.partial(_rms_pad, width=ROPE)

    def body(q_ref, kv_ref, zt_ref, c_ref, s_ref, gqn_ref, gqp_ref, gkn_ref, gkp_ref, dqh_ref, dkh_ref, dvh_ref,
             dq_ref, dkv_ref, dzt_ref, dgqn_ref, dgqp_ref, dgkn_ref, dgkp_ref):
        i, h = pl.program_id(0), pl.program_id(1)
        first = _first(i, h)
        c, s = c_ref[...], s_ref[...]
        _, v1 = jax.vjp(_rms, q_ref[:, :NOPE], gqn_ref[...])
        dx, dg = v1(dqh_ref[:, :NOPE])
        dq_ref[:, :NOPE] = dx.astype(bf16)
        _acc_row(dgqn_ref, dg, first)
        _, v2 = jax.vjp(pad_norm, q_ref[:, NOPE:], gqp_ref[...])
        dx, dg = v2(_rope_t(dqh_ref[:, NOPE:], c, s))
        dq_ref[:, NOPE:] = dx.astype(bf16)
        _acc_row(dgqp_ref, dg, first)
        _, v3 = jax.vjp(_rms, kv_ref[:, :NOPE], gkn_ref[...])
        dx, dg = v3(dkh_ref[:, :NOPE])
        dkv_ref[:, :NOPE] = dx.astype(bf16)
        _acc_row(dgkn_ref, dg, first)
        dkv_ref[:, NOPE:] = dvh_ref[...].astype(bf16)
        _, v4 = jax.vjp(pad_norm, _pe_in(zt_ref[...]), gkp_ref[...])
        dx, dg = v4(_rope_t(dkh_ref[:, NOPE:], c, s))
        _acc_row(dgkp_ref, dg, first)

        @pl.when(h == 0)
        def _():
            dzt_ref[...] = jnp.zeros_like(dzt_ref)

        dzt_ref[...] += dx

    head, row, gain = _prep_specs(tm)
    acc = pl.BlockSpec((8, TAIL), lambda i, h: (0, 0))
    vspec = pl.BlockSpec((tm, V_HEAD), lambda i, h: (i, h))
    return _call(body, "mla_prep_bwd", (t // tm, MLA_HEADS),
                 [head, head, row, row, row, gain, gain, gain, gain, head, head, vspec],
                 [head, head, row, acc, acc, acc, acc],
                 [_sds((t, MLA_HEADS * HEAD_PAD), bf16), _sds((t, MLA_HEADS * HEAD_PAD), bf16), _sds((t, TAIL), f32)]
                 + [_sds((8, TAIL), f32)] * 4)(q_raw, kv_raw, z_tail, cos, sin, gqn, gqp, gkn, gkp, dqh, dkh, dvh)


ATT_BLOCK = 512
NEG = -1e30
ATT_SCALE = (NOPE + ROPE) ** -0.5
ATT_HEADS = 2


def _chunk_visible(shape, key_axis):
    kc = lax.broadcasted_iota(jnp.int32, shape, key_axis) >> CHUNK_SHIFT
    qc = lax.broadcasted_iota(jnp.int32, shape, 1 - key_axis) >> CHUNK_SHIFT
    return kc <= qc


def mla_fwd(qh, kh, vh, exchange=None):
    t = qh.shape[0]
    tb = min(ATT_BLOCK, t)
    nb = t // tb

    hp = ATT_HEADS

    def body(q_ref, k_ref, v_ref, o_ref, ob_ref, lse_ref, m_s, l_s, acc):
        qi, ki = pl.program_id(1), pl.program_id(2)

        @pl.when(ki == 0)
        def _():
            m_s[...] = jnp.full_like(m_s, NEG)
            l_s[...] = jnp.zeros_like(l_s)
            acc[...] = jnp.zeros_like(acc)

        def step(diagonal):
            new = []
            for j in range(hp):
                q, k = q_ref[:, j * HEAD_PAD:(j + 1) * HEAD_PAD], k_ref[:, j * HEAD_PAD:(j + 1) * HEAD_PAD]
                s = _bdot(k, q, 1, 1) * ATT_SCALE
                if diagonal:
                    s = jnp.where(_chunk_visible(s.shape, 0), s, -jnp.inf)
                m_old = m_s[j]
                m_new = jnp.maximum(m_old, jnp.max(s, axis=0, keepdims=True))
                p = jnp.exp(s - m_new)
                alpha = jnp.exp(m_old - m_new)
                l_new = alpha * l_s[j] + jnp.sum(p, axis=0, keepdims=True)
                acc_new = alpha * acc[j] + _bdot(v_ref[:, j * V_HEAD:(j + 1) * V_HEAD], p, 0, 0)
                new.append((m_new, l_new, acc_new))
            for j, (m_new, l_new, acc_new) in enumerate(new):
                m_s[j] = m_new
                l_s[j] = l_new
                acc[j] = acc_new
            return new

        @pl.when(ki < qi)
        def _():
            step(False)

        @pl.when(ki == qi)
        def _():
            for j, (m_new, l_new, acc_new) in enumerate(step(True)):
                o = (acc_new / l_new).T
                o_ref[:, j * V_HEAD:(j + 1) * V_HEAD] = o
                ob_ref[:, j * V_HEAD:(j + 1) * V_HEAD] = o.astype(bf16)
                lse_ref[j] = m_new + jnp.log(l_new)

    kv = lambda g, qi, ki: (jnp.minimum(ki, qi), g)
    o_spec = pl.BlockSpec((tb, hp * V_HEAD), lambda g, qi, ki: (qi, g))
    return _call(body, "mla_fwd", (MLA_HEADS // hp, nb, nb),
                 [pl.BlockSpec((tb, hp * HEAD_PAD), lambda g, qi, ki: (qi, g)), pl.BlockSpec((tb, hp * HEAD_PAD), kv),
                  pl.BlockSpec((tb, hp * V_HEAD), kv)],
                 [o_spec, o_spec, pl.BlockSpec((hp, 1, tb), lambda g, qi, ki: (g, 0, qi))],
                 [_sds((t, MLA_HEADS * V_HEAD), f32), _sds((t, MLA_HEADS * V_HEAD), bf16), _sds((MLA_HEADS, 1, t), f32)],
                 [pltpu.VMEM((hp, 1, tb), f32), pltpu.VMEM((hp, 1, tb), f32), pltpu.VMEM((hp, V_HEAD, tb), f32)],
                 exchange=exchange)(qh, kh, vh)


def mla_delta(o, do):
    t = o.shape[0]
    tm = _tile(t, (512, 256, 128, 64))

    def body(o_ref, do_ref, d_ref):
        d_ref[0] = jnp.sum(o_ref[...] * do_ref[...], axis=1, keepdims=True)

    blk = pl.BlockSpec((tm, V_HEAD), lambda i, h: (i, h))
    return _call(body, "mla_delta", (t // tm, MLA_HEADS), [blk, blk], pl.BlockSpec((1, tm, 1), lambda i, h: (h, i, 0)),
                 _sds((MLA_HEADS, t, 1), f32))(o, do)


def mla_bwd(qh, kh, vh, do, lse_row, delta_row, exchange=None):
    t = qh.shape[0]
    tb = min(ATT_BLOCK, t)
    nb = t // tb

    hp = ATT_HEADS

    def body(q_ref, k_ref, v_ref, do_ref, lse_ref, dl_ref, dq_ref, dk_ref, dv_ref, dk_acc, dv_acc):
        ki, qi = pl.program_id(1), pl.program_id(2)

        @pl.when(jnp.logical_and(ki == 0, qi == 0))
        def _():
            dq_ref[...] = jnp.zeros_like(dq_ref)

        @pl.when(qi == 0)
        def _():
            dk_acc[...] = jnp.zeros_like(dk_acc)
            dv_acc[...] = jnp.zeros_like(dv_acc)

        def step(diagonal):
            rows = pl.ds(pl.multiple_of(qi * tb, tb), tb)
            new = []
            for j in range(hp):
                qc, vc = slice(j * HEAD_PAD, (j + 1) * HEAD_PAD), slice(j * V_HEAD, (j + 1) * V_HEAD)
                q, k, do_b = q_ref[:, qc], k_ref[:, qc], do_ref[:, vc]
                s = _bdot(k, q, 1, 1) * ATT_SCALE
                if diagonal:
                    s = jnp.where(_chunk_visible(s.shape, 0), s, -jnp.inf)
                p = jnp.exp(s - lse_ref[j])
                dp = _bdot(v_ref[:, vc], do_b, 1, 1)
                ds = p * (dp - dl_ref[j]) * ATT_SCALE
                new.append((dv_acc[:, vc] + _bdot(p, do_b, 1, 0), dk_acc[:, qc] + _bdot(ds, q, 1, 0),
                            dq_ref[rows, qc] + _bdot(ds, k, 0, 0)))
            for j, (dv, dk, dq) in enumerate(new):
                dv_acc[:, j * V_HEAD:(j + 1) * V_HEAD] = dv
                dk_acc[:, j * HEAD_PAD:(j + 1) * HEAD_PAD] = dk
                dq_ref[rows, j * HEAD_PAD:(j + 1) * HEAD_PAD] = dq

        @pl.when(qi > ki)
        def _():
            step(False)

        @pl.when(qi == ki)
        def _():
            step(True)

        @pl.when(qi == nb - 1)
        def _():
            dk_ref[...] = dk_acc[...]
            dv_ref[...] = dv_acc[...]

    qs = lambda g, ki, qi: (jnp.maximum(qi, ki), g)
    ks = lambda g, ki, qi: (ki, g)
    vec = pl.BlockSpec((hp, 1, tb), lambda g, ki, qi: (g, 0, jnp.maximum(qi, ki)))
    return _call(body, "mla_bwd", (MLA_HEADS // hp, nb, nb),
                 [pl.BlockSpec((tb, hp * HEAD_PAD), qs), pl.BlockSpec((tb, hp * HEAD_PAD), ks), pl.BlockSpec((tb, hp * V_HEAD), ks),
                  pl.BlockSpec((tb, hp * V_HEAD), qs), vec, vec],
                 [pl.BlockSpec((t, hp * HEAD_PAD), lambda g, ki, qi: (0, g)), pl.BlockSpec((tb, hp * HEAD_PAD), ks),
                  pl.BlockSpec((tb, hp * V_HEAD), ks)],
                 [_sds((t, MLA_HEADS * HEAD_PAD), f32), _sds((t, MLA_HEADS * HEAD_PAD), f32), _sds((t, MLA_HEADS * V_HEAD), f32)],
                 [pltpu.VMEM((tb, hp * HEAD_PAD), f32), pltpu.VMEM((tb, hp * V_HEAD), f32)], exchange=exchange)(
        qh, kh, vh, do, lse_row, delta_row)


PAD = 8


def _conv_taps(pad_ref, w, width, t):
    y = pad_ref[PAD - width + 1:PAD - width + 1 + t, :] * w[0:1, :]
    for j in range(1, width):
        y = y + pad_ref[PAD - width + 1 + j:PAD - width + 1 + j + t, :] * w[j:j + 1, :]
    return y


def _conv_bwd(xpad_ref, dpad_ref, w, da, width, t):
    dpad_ref[0:t, :] = da
    dpad_ref[t:t + PAD, :] = jnp.zeros((PAD, da.shape[1]), f32)
    dx = dpad_ref[width - 1:width - 1 + t, :] * w[0:1, :]
    for j in range(1, width):
        dx = dx + dpad_ref[width - 1 - j:width - 1 - j + t, :] * w[j:j + 1, :]
    dws = [jnp.sum(da * xpad_ref[PAD - width + 1 + j:PAD - width + 1 + j + t, :], axis=0, keepdims=True) for j in range(width)]
    return dx, dws


def _load_pad(pad_ref, x, t):
    pad_ref[0:PAD, :] = jnp.zeros((PAD, x.shape[1]), f32)
    pad_ref[PAD:PAD + t, :] = x


def qk_conv(z_main, conv_qk):
    t = z_main.shape[0]
    nq = ML_QK // 128
    base = O_Q // 128

    def body(z_ref, w_ref, o_ref, pad):
        _load_pad(pad, z_ref[...], t)
        a = _conv_taps(pad, w_ref[...], ML_CONV, t)
        sc = jnp.where(pl.program_id(0) < nq, ML_DK ** -0.5, 1.0)
        o_ref[...] = jax.nn.silu(a) * sc

    return _call(body, "qk_conv", (2 * nq,),
                 [pl.BlockSpec((t, 128), lambda j: (0, base + j)), pl.BlockSpec((ML_CONV, 128), lambda j: (0, j))],
                 pl.BlockSpec((t, 128), lambda j: (0, j)), _sds((t, 2 * ML_QK), f32),
                 [pltpu.VMEM((t + PAD, 128), f32)])(z_main, conv_qk)


def qk_conv_bwd(z_main, conv_qk, dqk):
    t = z_main.shape[0]
    nq = ML_QK // 128
    base = O_Q // 128

    def body(z_ref, w_ref, d_ref, dz_ref, dw_ref, pad, dpad):
        _load_pad(pad, z_ref[...], t)
        w = w_ref[...]
        a = _conv_taps(pad, w, ML_CONV, t)
        sc = jnp.where(pl.program_id(0) < nq, ML_DK ** -0.5, 1.0)
        _, vjp = jax.vjp(jax.nn.silu, a)
        da, = vjp(d_ref[...] * sc)
        dx, dws = _conv_bwd(pad, dpad, w, da, ML_CONV, t)
        dz_ref[...] = dx.astype(bf16)
        for j in range(ML_CONV):
            dw_ref[j:j + 1, :] = dws[j]

    return _call(body, "qk_conv_bwd", (2 * nq,),
                 [pl.BlockSpec((t, 128), lambda j: (0, base + j)), pl.BlockSpec((ML_CONV, 128), lambda j: (0, j)),
                  pl.BlockSpec((t, 128), lambda j: (0, j))],
                 [pl.BlockSpec((t, 128), lambda j: (0, j)), pl.BlockSpec((ML_CONV, 128), lambda j: (0, j))],
                 [_sds((t, 2 * ML_QK), bf16), _sds((ML_CONV, 2 * ML_QK), f32)],
                 [pltpu.VMEM((t + PAD, 128), f32), pltpu.VMEM((t + PAD, 128), f32)])(z_main, conv_qk, dqk)


def glu_fwd(hup, conv_w, bias):
    t, f2 = hup.shape
    nf = f2 // 2 // 128

    def body(h1_ref, h2_ref, w1_ref, w2_ref, b1_ref, b2_ref, o_ref, pad):
        _load_pad(pad, h1_ref[...], t)
        a1 = _conv_taps(pad, w1_ref[...], FFN_CONV, t) + b1_ref[...]
        _load_pad(pad, h2_ref[...], t)
        a2 = _conv_taps(pad, w2_ref[...], FFN_CONV, t) + b2_ref[...]
        o_ref[...] = (jax.nn.silu(a1) * a2).astype(bf16)

    col = lambda off: pl.BlockSpec((t, 128), lambda j: (0, j + off))
    wsp = lambda off: pl.BlockSpec((FFN_CONV, 128), lambda j: (0, j + off))
    bsp = lambda off: pl.BlockSpec((1, 128), lambda j: (0, j + off))
    return _call(body, "glu_fwd", (nf,), [col(0), col(nf), wsp(0), wsp(nf), bsp(0), bsp(nf)], col(0), _sds((t, f2 // 2), bf16),
                 [pltpu.VMEM((t + PAD, 128), f32)])(hup, hup, conv_w, conv_w, bias, bias)


def glu_bwd(hup, conv_w, bias, dg, exchange=None):
    t, f2 = hup.shape
    f = f2 // 2
    nf = f // 128

    def body(h1_ref, h2_ref, w1_ref, w2_ref, b1_ref, b2_ref, dg_ref, dh1_ref, dh2_ref, dw1_ref, dw2_ref, db1_ref, db2_ref,
             pad1, pad2, dpad):
        _load_pad(pad1, h1_ref[...], t)
        _load_pad(pad2, h2_ref[...], t)
        w1, w2 = w1_ref[...], w2_ref[...]
        a1 = _conv_taps(pad1, w1, FFN_CONV, t) + b1_ref[...]
        a2 = _conv_taps(pad2, w2, FFN_CONV, t) + b2_ref[...]
        d = dg_ref[...]
        _, vjp = jax.vjp(jax.nn.silu, a1)
        da1, = vjp(d * a2)
        da2 = d * jax.nn.silu(a1)
        for da, pad, w, dh_ref, dw_ref, db_ref in ((da1, pad1, w1, dh1_ref, dw1_ref, db1_ref), (da2, pad2, w2, dh2_ref, dw2_ref, db2_ref)):
            dx, dws = _conv_bwd(pad, dpad, w, da, FFN_CONV, t)
            dh_ref[...] = dx.astype(bf16)
            for j in range(FFN_CONV):
                dw_ref[j:j + 1, :] = dws[j]
            db_ref[...] = jnp.sum(da, axis=0, keepdims=True)

    col = lambda off: pl.BlockSpec((t, 128), lambda j: (0, j + off))
    wsp = lambda off: pl.BlockSpec((FFN_CONV, 128), lambda j: (0, j + off))
    bsp = lambda off: pl.BlockSpec((1, 128), lambda j: (0, j + off))
    return _call(body, "glu_bwd", (nf,), [col(0), col(nf), wsp(0), wsp(nf), bsp(0), bsp(nf), col(0)],
                 [col(0), col(0), wsp(0), wsp(0), bsp(0), bsp(0)],
                 [_sds((t, f), bf16)] * 2 + [_sds((FFN_CONV, f), f32)] * 2 + [_sds((1, f), f32)] * 2,
                 [pltpu.VMEM((t + PAD, 128), f32)] * 3, exchange=exchange)(hup, hup, conv_w, conv_w, bias, bias, dg)


def gate_act(z_tail, b_tile):
    t = z_tail.shape[0]
    tm = _tile(t, (512, 256, 128, 64))

    def body(z_ref, b_ref, o_ref):
        x = z_ref[...] + b_ref[...]
        lane = lax.broadcasted_iota(jnp.int32, x.shape, 1)
        o_ref[...] = jnp.where(lane < T_F, x, jax.nn.log_sigmoid(x))

    row = pl.BlockSpec((tm, TAIL), lambda i: (i, 0))
    return _call(body, "gate_act", (t // tm,), [row, pl.BlockSpec((1, TAIL), lambda i: (0, 0))], row, _sds((t, TAIL), f32))(z_tail, b_tile)


def tail_bwd(z_tail, b_tile, dzt_pe, dgate):
    t = z_tail.shape[0]
    tm = _tile(t, (512, 256, 128, 64))

    def body(z_ref, b_ref, dpe_ref, dg_ref, dz_ref, db_ref):
        x = z_ref[...] + b_ref[...]
        lane = lax.broadcasted_iota(jnp.int32, x.shape, 1)
        _, vjp = jax.vjp(jax.nn.log_sigmoid, x)
        df, = vjp(dg_ref[...])
        dgates = jnp.where(lane < T_F, dg_ref[...], df)
        dgates = jnp.where(jnp.logical_and(lane >= T_I, lane < T_F + ML_HEADS), dgates, 0.0)
        dz_ref[...] = jnp.where(lane < ROPE, dpe_ref[...], dgates).astype(bf16)
        _acc_row(db_ref, jnp.sum(dgates, axis=0, keepdims=True), pl.program_id(0) == 0)

    row = pl.BlockSpec((tm, TAIL), lambda i: (i, 0))
    return _call(body, "tail_bwd", (t // tm,), [row, pl.BlockSpec((1, TAIL), lambda i: (0, 0)), row, row],
                 [row, pl.BlockSpec((8, TAIL), lambda i: (0, 0))], [_sds((t, TAIL), bf16), _sds((8, TAIL), f32)])(z_tail, b_tile, dzt_pe, dgate)


def _mlstm_step(q, k, v, igr, fgr, c_mat, n_vec, m):
    ln = CHUNK
    row = lax.broadcasted_iota(jnp.int32, (ln, ln), 0)
    col = lax.broadcasted_iota(jnp.int32, (ln, ln), 1)
    eye = row == col

    def to_col(r):
        return jnp.sum(jnp.where(eye, jnp.broadcast_to(r, (ln, ln)), 0.0), axis=1, keepdims=True)

    bc_r = jnp.sum(jnp.where(row <= col, jnp.broadcast_to(to_col(fgr), (ln, ln)), 0.0), axis=0, keepdims=True)
    bc_c = to_col(bc_r)
    logw = jnp.where(col <= row, bc_c - bc_r + igr, -jnp.inf)
    inter = bc_c + m
    m_t = jnp.maximum(inter, jnp.max(logw, axis=1, keepdims=True))
    w_intra = jnp.exp(logw - m_t)
    w_inter = jnp.exp(inter - m_t)
    sc = _bdot(q, k, 1, 1) * w_intra
    num = w_inter * _bdot(q, c_mat, 1, 0) + _bdot(sc, v, 1, 0)
    qn = jnp.sum(q.astype(bf16).astype(f32) * n_vec.astype(bf16).astype(f32), axis=1, keepdims=True)
    den = w_inter * qn + jnp.sum(sc, axis=1, keepdims=True)
    h = num / jnp.maximum(jnp.abs(den), jnp.exp(-m_t))
    lane = lax.broadcasted_iota(jnp.int32, (1, ln), 1)
    b_last = jnp.sum(jnp.where(lane == ln - 1, bc_r, 0.0), axis=1, keepdims=True)
    logu = b_last - bc_r + igr
    m_new = jnp.maximum(b_last + m, jnp.max(logu, axis=1, keepdims=True))
    decay = jnp.exp(b_last + m - m_new)
    u_c = to_col(jnp.exp(logu - m_new))
    c_new = decay * c_mat + _bdot(u_c * k, v, 0, 0)
    n_new = decay * n_vec + jnp.sum(u_c.astype(bf16).astype(f32) * k.astype(bf16).astype(f32), axis=0, keepdims=True)
    return h, c_new, n_new, m_new


ML_GROUP = 4
ML_NG = ML_HEADS // ML_GROUP


def _ml_specs(nc, rev):
    cc = (lambda c: nc - 1 - c) if rev else (lambda c: c)
    gq, gv = ML_GROUP * ML_DK, ML_GROUP * ML_DV
    assert ML_QK % gq == 0 and O_V % gv == 0
    q = pl.BlockSpec((CHUNK, gq), lambda g, c: (cc(c), g))
    k = pl.BlockSpec((CHUNK, gq), lambda g, c: (cc(c), ML_QK // gq + g))
    v = pl.BlockSpec((CHUNK, gv), lambda g, c: (cc(c), O_V // gv + g))
    hv = pl.BlockSpec((CHUNK, gv), lambda g, c: (cc(c), g))
    gate = pl.BlockSpec((ML_GROUP, 1, 1, CHUNK), lambda g, c: (g, cc(c), 0, 0))
    cm = pl.BlockSpec((ML_GROUP, 1, ML_DK, ML_DV), lambda g, c: (g, cc(c), 0, 0))
    nv = pl.BlockSpec((ML_GROUP, 1, 1, ML_DK), lambda g, c: (g, cc(c), 0, 0))
    ms = pl.BlockSpec((ML_GROUP, 1, 1, 1), lambda g, c: (g, cc(c), 0, 0))
    return q, k, v, hv, gate, cm, nv, ms


_ML_STATE = [pltpu.VMEM((ML_GROUP, ML_DK, ML_DV), f32), pltpu.VMEM((ML_GROUP, 1, ML_DK), f32), pltpu.VMEM((ML_GROUP, 1, 1), f32)]


def _ml_zero_state(c_s, n_s, m_s):
    @pl.when(pl.program_id(1) == 0)
    def _():
        c_s[...] = jnp.zeros_like(c_s)
        n_s[...] = jnp.zeros_like(n_s)
        m_s[...] = jnp.zeros_like(m_s)


def mlstm_fwd(qk_act, z_main, ig, fg, exchange=None):
    t = qk_act.shape[0]
    nc = t // CHUNK

    def body(q_ref, k_ref, v_ref, ig_ref, fg_ref, h_ref, c_out, n_out, m_out, c_s, n_s, m_s):
        _ml_zero_state(c_s, n_s, m_s)
        res = []
        for j in range(ML_GROUP):
            qs, vs = slice(j * ML_DK, (j + 1) * ML_DK), slice(j * ML_DV, (j + 1) * ML_DV)
            state = (c_s[j], n_s[j], m_s[j])
            res.append(state + _mlstm_step(q_ref[:, qs], k_ref[:, qs], v_ref[:, vs], ig_ref[j, 0], fg_ref[j, 0], *state))
        for j, (c0, n0, m0, h, c2, n2, m2) in enumerate(res):
            c_out[j, 0] = c0
            n_out[j, 0] = n0
            m_out[j, 0] = m0
            h_ref[:, j * ML_DV:(j + 1) * ML_DV] = h
            c_s[j] = c2
            n_s[j] = n2
            m_s[j] = m2

    q, k, v, hv, gate, cm, nv, ms = _ml_specs(nc, False)
    return _call(body, "mlstm_fwd", (ML_NG, nc), [q, k, v, gate, gate], [hv, cm, nv, ms],
                 [_sds((t, ML_V), f32), _sds((ML_HEADS, nc, ML_DK, ML_DV), f32), _sds((ML_HEADS, nc, 1, ML_DK), f32),
                  _sds((ML_HEADS, nc, 1, 1), f32)], _ML_STATE, exchange=exchange)(qk_act, qk_act, z_main, ig, fg)


def mlstm_bwd(qk_act, z_main, ig, fg, c_all, n_all, m_all, dh, exchange=None):
    t = qk_act.shape[0]
    nc = t // CHUNK

    def body(q_ref, k_ref, v_ref, ig_ref, fg_ref, c_ref, n_ref, m_ref, dh_ref, dq_ref, dk_ref, dv_ref, dig_ref, dfg_ref,
             dc_s, dn_s, dm_s):
        _ml_zero_state(dc_s, dn_s, dm_s)
        res = []
        for j in range(ML_GROUP):
            qs, vs = slice(j * ML_DK, (j + 1) * ML_DK), slice(j * ML_DV, (j + 1) * ML_DV)
            _, vjp = jax.vjp(_mlstm_step, q_ref[:, qs], k_ref[:, qs], v_ref[:, vs], ig_ref[j, 0], fg_ref[j, 0],
                             c_ref[j, 0], n_ref[j, 0], m_ref[j, 0])
            res.append(vjp((dh_ref[:, vs], dc_s[j], dn_s[j], dm_s[j])))
        for j, (dq, dk, dv, dig, dfg, dc, dn, dm) in enumerate(res):
            qs, vs = slice(j * ML_DK, (j + 1) * ML_DK), slice(j * ML_DV, (j + 1) * ML_DV)
            dq_ref[:, qs] = dq
            dk_ref[:, qs] = dk
            dv_ref[:, vs] = dv.astype(bf16)
            dig_ref[j, 0] = dig
            dfg_ref[j, 0] = dfg
            dc_s[j] = dc
            dn_s[j] = dn
            dm_s[j] = dm

    q, k, v, hv, gate, cm, nv, ms = _ml_specs(nc, True)
    gshape = _sds((ML_HEADS, nc, 1, CHUNK), f32)
    return _call(body, "mlstm_bwd", (ML_NG, nc), [q, k, v, gate, gate, cm, nv, ms, hv], [q, q, hv, gate, gate],
                 [_sds((t, ML_QK), f32), _sds((t, ML_QK), f32), _sds((t, ML_V), bf16), gshape, gshape],
                 _ML_STATE, exchange=exchange)(qk_act, qk_act, z_main, ig, fg, c_all, n_all, m_all, dh)


def _ml_out(h, zo, g):
    return _rms(h, g) * jax.nn.sigmoid(zo)


def mlstm_out(h, z_main, g_hnorm):
    t = h.shape[0]
    tm = _tile(t, (512, 256, 128, 64))
    zo = O_O // ML_DV

    def body(h_ref, z_ref, g_ref, y_ref):
        y_ref[...] = _ml_out(h_ref[...], z_ref[...], g_ref[0]).astype(bf16)

    blk = pl.BlockSpec((tm, ML_DV), lambda i, hd: (i, hd))
    return _call(body, "mlstm_out", (t // tm, ML_HEADS),
                 [blk, pl.BlockSpec((tm, ML_DV), lambda i, hd: (i, zo + hd)), pl.BlockSpec((1, 1, ML_DV), lambda i, hd: (hd, 0, 0))],
                 blk, _sds((t, ML_V), bf16))(h, z_main, g_hnorm)


def mlstm_out_bwd(h, z_main, g_hnorm, dy):
    t = h.shape[0]
    tm = _tile(t, (512, 256, 128, 64))
    zo = O_O // ML_DV

    def body(h_ref, z_ref, g_ref, dy_ref, dh_ref, dzo_ref, dg_ref):
        _, vjp = jax.vjp(_ml_out, h_ref[...], z_ref[...], g_ref[0])
        dh, dz, dg = vjp(dy_ref[...])
        dh_ref[...] = dh
        dzo_ref[...] = dz.astype(bf16)

        @pl.when(pl.program_id(1) == 0)
        def _():
            dg_ref[...] = jnp.zeros_like(dg_ref)

        dg_ref[0, 0:1, :] += dg

    blk = pl.BlockSpec((tm, ML_DV), lambda hd, i: (i, hd))
    return _call(body, "mlstm_out_bwd", (ML_HEADS, t // tm),
                 [blk, pl.BlockSpec((tm, ML_DV), lambda hd, i: (i, zo + hd)), pl.BlockSpec((1, 1, ML_DV), lambda hd, i: (hd, 0, 0)), blk],
                 [blk, blk, pl.BlockSpec((1, 8, ML_DV), lambda hd, i: (hd, 0, 0))],
                 [_sds((t, ML_V), f32), _sds((t, ML_V), bf16), _sds((ML_HEADS, 8, ML_DV), f32)])(h, z_main, g_hnorm, dy)


def _merge(ga, gb, ya, yb):
    return jax.nn.sigmoid(ga) * ya + jax.nn.sigmoid(gb) * yb


def _merge_specs(t, d):
    tm = _tile(t, (512, 256, 128, 64))
    bw = _tile(d, (512, 256, 128))
    assert O_GA % bw == 0 and (O_GA + d) % bw == 0
    blk = pl.BlockSpec((tm, bw), lambda i, j: (i, j))
    ga = pl.BlockSpec((tm, bw), lambda i, j: (i, O_GA // bw + j))
    gb = pl.BlockSpec((tm, bw), lambda i, j: (i, (O_GA + d) // bw + j))
    return tm, bw, blk, ga, gb


def merge_fwd(z_main, ya, yb):
    t, d = ya.shape
    tm, bw, blk, ga, gb = _merge_specs(t, d)

    def body(ga_ref, gb_ref, ya_ref, yb_ref, o_ref):
        o_ref[...] = _merge(ga_ref[...], gb_ref[...], ya_ref[...], yb_ref[...]).astype(bf16)

    return _call(body, "merge_fwd", (t // tm, d // bw), [ga, gb, blk, blk], blk, _sds((t, d), bf16))(z_main, z_main, ya, yb)


def merge_bwd(z_main, ya, yb, dmerged):
    t, d = ya.shape
    tm, bw, blk, ga, gb = _merge_specs(t, d)

    def body(ga_ref, gb_ref, ya_ref, yb_ref, dm_ref, dga_ref, dgb_ref, dya_ref, dyb_ref):
        _, vjp = jax.vjp(_merge, ga_ref[...], gb_ref[...], ya_ref[...], yb_ref[...])
        dga, dgb, dya, dyb = vjp(dm_ref[...])
        dga_ref[...] = dga.astype(bf16)
        dgb_ref[...] = dgb.astype(bf16)
        dya_ref[...] = dya.astype(bf16)
        dyb_ref[...] = dyb.astype(bf16)

    return _call(body, "merge_bwd", (t // tm, d // bw), [ga, gb, blk, blk, blk], [blk] * 4, [_sds((t, d), bf16)] * 4)(
        z_main, z_main, ya, yb, dmerged)


def _cross(cq, ck, cv, gq, gk):
    outs = []
    for hd in range(CR_HEADS):
        sl = slice(hd * CR_HD, (hd + 1) * CR_HD)
        q = _rms(cq[:, sl], gq)
        k = _rms(ck[:, sl], gk)
        s = _bdot(q, k, 1, 1) * (CR_HD ** -0.5)
        p = jax.nn.softmax(s, axis=-1)
        outs.append(_bdot(p, cv[:, sl], 1, 0))
    return jnp.concatenate(outs, axis=1)


def cross_fwd(cq, ck, cv, gq, gk):
    t, w = cq.shape
    nm = ck.shape[0]
    tm = _tile(t, (512, 256, 128, 64))

    def body(q_ref, k_ref, v_ref, gq_ref, gk_ref, o_ref):
        o_ref[...] = _cross(q_ref[...], k_ref[...], v_ref[...], gq_ref[...], gk_ref[...]).astype(bf16)

    row = pl.BlockSpec((tm, w), lambda i: (i, 0))
    full = pl.BlockSpec((nm, w), lambda i: (0, 0))
    gain = pl.BlockSpec((1, CR_HD), lambda i: (0, 0))
    return _call(body, "cross_fwd", (t // tm,), [row, full, full, gain, gain], row, _sds((t, w), bf16))(cq, ck, cv, gq, gk)


def cross_bwd(cq, ck, cv, gq, gk, do):
    t, w = cq.shape
    nm = ck.shape[0]
    tm = _tile(t, (512, 256, 128, 64))

    def body(q_ref, k_ref, v_ref, gq_ref, gk_ref, do_ref, dq_ref, dk_ref, dv_ref, dgq_ref, dgk_ref):
        first = pl.program_id(0) == 0
        _, vjp = jax.vjp(_cross, q_ref[...], k_ref[...], v_ref[...], gq_ref[...], gk_ref[...])
        dq, dk, dv, dgq, dgk = vjp(do_ref[...])
        dq_ref[...] = dq.astype(bf16)

        @pl.when(first)
        def _():
            dk_ref[...] = jnp.zeros_like(dk_ref)
            dv_ref[...] = jnp.zeros_like(dv_ref)

        dk_ref[...] += dk
        dv_ref[...] += dv
        _acc_row(dgq_ref, dgq, first)
        _acc_row(dgk_ref, dgk, first)

    row = pl.BlockSpec((tm, w), lambda i: (i, 0))
    full = pl.BlockSpec((nm, w), lambda i: (0, 0))
    gain = pl.BlockSpec((1, CR_HD), lambda i: (0, 0))
    acc = pl.BlockSpec((8, CR_HD), lambda i: (0, 0))
    return _call(body, "cross_bwd", (t // tm,), [row, full, full, gain, gain, row], [row, full, full, acc, acc],
                 [_sds((t, w), bf16), _sds((nm, w), f32), _sds((nm, w), f32), _sds((8, CR_HD), f32), _sds((8, CR_HD), f32)])(
        cq, ck, cv, gq, gk, do)


def loss_head(x2, fo, target):
    t, d = x2.shape
    tm = _tile(t, (256, 128, 64, 32, 16, 8))

    def body(a_ref, b_ref, t_ref, dx_ref, dxb_ref, l_ref):
        err = a_ref[...] + b_ref[...] - t_ref[...]
        dx = err / d
        dx_ref[...] = dx
        dxb_ref[...] = dx.astype(bf16)
        part = 0.5 * jnp.sum(jnp.mean(err * err, axis=1, keepdims=True), axis=0, keepdims=True)
        _acc_row(l_ref, jnp.broadcast_to(part, (1, 128)), pl.program_id(0) == 0)

    row = pl.BlockSpec((tm, d), lambda i: (i, 0))
    return _call(body, "loss_head", (t // tm,), [row, row, row], [row, row, pl.BlockSpec((8, 128), lambda i: (0, 0))],
                 [_sds((t, d), f32), _sds((t, d), bf16), _sds((8, 128), f32)])(x2, fo, target)


def _place():
    x, y, c = lax.axis_index("x"), lax.axis_index("y"), lax.axis_index("c")
    peers = []
    for k in range(1, N_DEV):
        px = 1 - x if k & 4 else x
        py = 1 - y if k & 2 else y
        pc = 1 - c if k & 1 else c
        peers.append(((px, py, pc), 4 * px + 2 * py + pc))
    return 4 * x + 2 * y + c, peers


N_REL = N_DEV - 1


def _exchange_ops(ins, outs, sems, scatter):
    n = len(ins)
    send_sems, recv_sems, local_sems = sems

    def copies(with_arrivals):
        me, peers = _place()

        def src(a, idx):
            return ins[a].at[idx] if scatter else ins[a]

        def remote(a, k, src_idx, dst_idx):
            return pltpu.make_async_remote_copy(
                src_ref=src(a, src_idx), dst_ref=outs[a].at[dst_idx], send_sem=send_sems.at[a * N_REL + k],
                recv_sem=recv_sems.at[a * N_REL + k], device_id=peers[k][0], device_id_type=MESH)

        local = [pltpu.make_async_copy(src(a, me), outs[a].at[me], local_sems.at[a]) for a in range(n)]
        sends = [remote(a, k, peers[k][1], me) for a in range(n) for k in range(N_REL)]
        arrivals = [remote(a, k, me, peers[k][1]) for a in range(n) for k in range(N_REL)] if with_arrivals else []
        return local, sends, arrivals

    def start():
        local, sends, _ = copies(False)
        for cp in local + sends:
            cp.start()

    def wait():
        local, sends, arrivals = copies(True)
        for cp in arrivals:
            cp.wait_recv()
        for cp in sends:
            cp.wait_send()
        for cp in local:
            cp.wait()

    return start, wait


def _exchange_shapes(arrs, scatter):
    return [_sds(a.shape if scatter else (N_DEV,) + a.shape, a.dtype) for a in arrs]


def _exchange_sems(n):
    return [pltpu.SemaphoreType.DMA((n * N_REL,)), pltpu.SemaphoreType.DMA((n * N_REL,)), pltpu.SemaphoreType.DMA((n,))]


def _exchange(arrs, name, scatter):
    n = len(arrs)

    def body(*refs):
        start, wait = _exchange_ops(refs[:n], refs[n:2 * n], refs[2 * n:], scatter)
        start()
        wait()

    any_spec = pl.BlockSpec(memory_space=pl.ANY)
    return pl.pallas_call(body, name=name, in_specs=[any_spec] * n, out_specs=[any_spec] * n,
                          out_shape=_exchange_shapes(arrs, scatter), scratch_shapes=_exchange_sems(n))(*arrs)


def cast_bf16(w, name):
    r, c = w.shape
    tr = _tile(r, (256, 128, 64, 32, 16))

    def body(w_ref, o_ref):
        o_ref[...] = w_ref[...].astype(bf16)

    blk = pl.BlockSpec((tr, c), lambda i: (i, 0))
    return _call(body, name, (r // tr,), [blk], blk, _sds((r, c), bf16))(w)


def _adamw(w, g, m, v):
    m = ADAM_B1 * m + (1.0 - ADAM_B1) * g
    v = ADAM_B2 * v + (1.0 - ADAM_B2) * jnp.square(g)
    m_hat = m / (1.0 - ADAM_B1 ** ADAM_STEP)
    v_hat = v / (1.0 - ADAM_B2 ** ADAM_STEP)
    delta = -ADAM_LR * (m_hat / (jnp.sqrt(v_hat) + ADAM_EPS) + ADAM_WD * w)
    return delta, m, v


def adam_sum(parts, w, m, v, name):
    _, r, c = parts.shape
    budget = 4 * 1024 * 1024
    tr = r
    for cand in (1024, 512, 256, 128, 64, 32, 16):
        if r % cand == 0 and N_DEV * cand * c * 4 <= budget:
            tr = cand
            break

    def body(p_ref, w_ref, m_ref, v_ref, g_ref, d_ref, m2_ref, v2_ref):
        g = p_ref[0].astype(f32)
        for k in range(1, N_DEV):
            g = g + p_ref[k].astype(f32)
        d, m2, v2 = _adamw(w_ref[...], g, m_ref[...], v_ref[...])
        g_ref[...] = g
        d_ref[...] = d
        m2_ref[...] = m2
        v2_ref[...] = v2

    blk = pl.BlockSpec((tr, c), lambda i: (i, 0))
    return _call(body, name, (r // tr,), [pl.BlockSpec((N_DEV, tr, c), lambda i: (0, i, 0)), blk, blk, blk], [blk] * 4,
                 [_sds((r, c), f32)] * 4)(parts, w, m, v)


def sum_parts(parts, name):
    _, r, c = parts.shape

    def body(p_ref, o_ref):
        g = p_ref[0]
        for k in range(1, N_DEV):
            g = g + p_ref[k]
        o_ref[...] = g

    return pl.pallas_call(body, name=name, out_shape=_sds((r, c), f32))(parts)


def adam_flat(w, g, m, v, name):
    def body(w_ref, g_ref, m_ref, v_ref, d_ref, m2_ref, v2_ref):
        d, m2, v2 = _adamw(w_ref[...], g_ref[...], m_ref[...], v_ref[...])
        d_ref[...] = d
        m2_ref[...] = m2
        v2_ref[...] = v2

    return pl.pallas_call(body, name=name, out_shape=[_sds(w.shape, f32)] * 3)(w, g, m, v)


def _pack(vecs, multiple):
    flat = jnp.concatenate([v.reshape(-1) for v in vecs])
    n = flat.shape[0]
    total = -(-n // multiple) * multiple
    return jnp.pad(flat, (0, total - n))


def _unpack(flat, shapes):
    out, pos = [], 0
    for s in shapes:
        n = 1
        for d in s:
            n *= d
        out.append(flat[pos:pos + n].reshape(s))
        pos += n
    return out


def _pad_lanes(v, width=TAIL):
    return jnp.pad(v, ((0, 0), (0, width - v.shape[1])))


def kernel(x, mem, positions, g_mix, w_in, g_qa, w_qb, g_kva, w_kvb, g_qn_nope, g_qn_pe, g_kn_nope, g_kn_pe, conv_qk, b_if, g_hnorm, p_a, p_b, w_out, g_cross, g_mem, wq_c, wk_c, wv_c, g_cq, g_ck, wo_c, g_ffn, w_up, conv_ffn, b_conv_ffn, w_down, loss_target, m_g_mix, m_w_in, m_g_qa, m_w_qb, m_g_kva, m_w_kvb, m_g_qn_nope, m_g_qn_pe, m_g_kn_nope, m_g_kn_pe, m_conv_qk, m_b_if, m_g_hnorm, m_p_a, m_p_b, m_w_out, m_g_cross, m_g_mem, m_wq_c, m_wk_c, m_wv_c, m_g_cq, m_g_ck, m_wo_c, m_g_ffn, m_w_up, m_conv_ffn, m_b_conv_ffn, m_w_down, v_g_mix, v_w_in, v_g_qa, v_w_qb, v_g_kva, v_w_kvb, v_g_qn_nope, v_g_qn_pe, v_g_kn_nope, v_g_kn_pe, v_conv_qk, v_b_if, v_g_hnorm, v_p_a, v_p_b, v_w_out, v_g_cross, v_g_mem, v_wq_c, v_wk_c, v_wv_c, v_g_cq, v_g_ck, v_wo_c, v_g_ffn, v_w_up, v_conv_ffn, v_b_conv_ffn, v_w_down):
    args = dict(locals())
    names = ['g_mix', 'w_in', 'g_qa', 'w_qb', 'g_kva', 'w_kvb', 'g_qn_nope', 'g_qn_pe', 'g_kn_nope', 'g_kn_pe', 'conv_qk', 'b_if',
             'g_hnorm', 'p_a', 'p_b', 'w_out', 'g_cross', 'g_mem', 'wq_c', 'wk_c', 'wv_c', 'g_cq', 'g_ck', 'wo_c', 'g_ffn', 'w_up',
             'conv_ffn', 'b_conv_ffn', 'w_down']
    big = ['w_in', 'w_qb', 'w_kvb', 'p_a', 'p_b', 'w_out', 'wq_c', 'wk_c', 'wv_c', 'wo_c', 'w_up', 'w_down']
    sharded_small = ['conv_qk', 'g_hnorm', 'conv_ffn']
    replicated = [n for n in names if n not in big and n not in sharded_small]

    t, d = x.shape[1], x.shape[2]
    x2d, tgt = x[0], loss_target[0]
    mem2d = mem[0]
    me = 4 * lax.axis_index("x") + 2 * lax.axis_index("y") + lax.axis_index("c")
    nc = t // CHUNK
    f2 = b_conv_ffn.shape[1]
    wmain = O_GA + 2 * d

    first = ['w_in', 'w_qb', 'w_kvb']
    behind_in = ['p_a', 'p_b', 'w_out', 'wq_c', 'wk_c', 'wv_c', 'wo_c']
    shards = {n: cast_bf16(args[n][0], "cast_" + n) for n in big}
    small_local = _pack([args[n] for n in sharded_small], 128).reshape(1, -1)
    gathered = _exchange([shards[n] for n in first] + [small_local], "comm_gather_first", scatter=False)
    gw = dict(zip(first, gathered[:-1]))
    small_all = gathered[-1]
    small_shapes = [args[n].shape for n in sharded_small]
    per_dev = [_unpack(small_all[k, 0], small_shapes) for k in range(N_DEV)]
    conv_qk_f = jnp.concatenate([p[0] for p in per_dev], axis=-1)[0]
    g_hnorm_f = jnp.concatenate([p[1] for p in per_dev], axis=-1)[0]
    conv_ffn_f = jnp.concatenate([p[2] for p in per_dev], axis=-1)[0]

    w_in_f = gw['w_in'].transpose(1, 0, 2).reshape(d, -1)
    c_kpe, c_q, c_i, c_o = O_Q, O_Q + ROPE, O_Q + ROPE + 2 * ML_QK + ML_V, O_Q + ROPE + 2 * ML_QK + ML_V + 2 * ML_HEADS
    w_main = jnp.concatenate([w_in_f[:, :c_kpe], w_in_f[:, c_q:c_i], w_in_f[:, c_o:]], axis=1)[None]
    w_tail = jnp.concatenate([w_in_f[:, c_kpe:c_q], w_in_f[:, c_i:c_o],
                              jnp.zeros((d, TAIL - ROPE - 2 * ML_HEADS), bf16)], axis=1)[None]
    assert w_main.shape[2] == wmain
    qb = gw['w_qb'].transpose(1, 0, 2).reshape(Q_LORA, MLA_HEADS, NOPE + ROPE)
    w_qb_p = jnp.concatenate([qb, jnp.zeros((Q_LORA, MLA_HEADS, HEAD_PAD - NOPE - ROPE), bf16)], axis=2).reshape(1, Q_LORA, -1)
    w_kvb3 = gw['w_kvb']

    inv_freq = ROPE_BASE ** (-jnp.arange(0, ROPE, 2, dtype=f32) / ROPE)
    inv_tile = _pad_lanes(jnp.concatenate([inv_freq, inv_freq])[None])
    cos, sin = rope_tables(positions.reshape(t, 1), inv_tile)
    gqp, gkp = _pad_lanes(g_qn_pe), _pad_lanes(g_kn_pe)
    b_tile = jnp.pad(b_if, ((0, 0), (T_I, TAIL - T_I - 2 * ML_HEADS)))

    u0 = rms_fwd(x2d, g_mix, "rms_mix")
    z_main, got = mm_nn(u0, w_main, f32, "mm_in_main", exchange=([shards[n] for n in behind_in], False))
    gw.update(zip(behind_in, got))
    p_a3, p_b3, w_out3 = (gw[n].reshape(1, -1, d) for n in ('p_a', 'p_b', 'w_out'))
    wq_c3, wk_c3, wv_c3 = (gw[n].reshape(1, d, -1) for n in ('wq_c', 'wk_c', 'wv_c'))
    wo_c3 = gw['wo_c']
    z_tail = mm_nn(u0, w_tail, f32, "mm_in_tail")
    qa_n, kv_n = lat_norm(z_main, g_qa, g_kva)
    q_raw = mm_nn(qa_n, w_qb_p, f32, "mm_qb")
    kv_raw = mm_nn(kv_n, w_kvb3, f32, "mm_kvb")
    qh, kh, vh = mla_prep(q_raw, kv_raw, z_tail, cos, sin, g_qn_nope, gqp, g_kn_nope, gkp)
    (o_a, o_ab, lse), (w_up3,) = mla_fwd(qh, kh, vh, exchange=([shards['w_up']], False))

    qk_act = qk_conv(z_main, conv_qk_f)
    gates = gate_act(z_tail, b_tile)

    def to_rows(cols):
        return cols.T.reshape(ML_HEADS, nc, 1, CHUNK)

    ig, fg = to_rows(gates[:, T_I:T_F]), to_rows(gates[:, T_F:T_F + ML_HEADS])
    (h_ml, c_all, n_all, m_all), (w_down_g,) = mlstm_fwd(qk_act, z_main, ig, fg, exchange=([shards['w_down']], False))
    w_down3 = w_down_g.reshape(1, -1, d)
    g_hn3 = g_hnorm_f.reshape(ML_HEADS, 1, ML_DV)
    y_b = mlstm_out(h_ml, z_main, g_hn3)

    ya = mm_nn(o_ab, p_a3, f32, "mm_pa")
    yb = mm_nn(y_b, p_b3, f32, "mm_pb")
    merged = merge_fwd(z_main, ya, yb)
    mo = mm_nn(merged, w_out3, f32, "mm_out")
    x1, uc = resid_rms(x2d, mo, g_cross, "resid_cross")
    mem_n = rms_fwd(mem2d, g_mem, "rms_mem")
    cq = mm_nn(uc, wq_c3, f32, "mm_cq")
    ck = mm_nn(mem_n, wk_c3, f32, "mm_ck")
    cv = mm_nn(mem_n, wv_c3, f32, "mm_cv")
    o_c = cross_fwd(cq, ck, cv, g_cq, g_ck)
    co = mm_nn(o_c, wo_c3, f32, "mm_oc")
    x2, u3 = resid_rms(x1, co, g_ffn, "resid_ffn")
    hup = mm_nn(u3, w_up3, f32, "mm_up")
    gl = glu_fwd(hup, conv_ffn_f, b_conv_ffn)
    fo = mm_nn(gl, w_down3, f32, "mm_down")
    dx3, dx3_b, loss_acc = loss_head(x2, fo, tgt)

    grads, parts = {}, {}
    grads['w_down'] = mm_tn(gl, dx3_b, 1, "mm_d_wdown").reshape(N_DEV, -1, d)
    dgl = mm_nt(dx3_b, w_down3, f32, "mm_d_gl")
    (dh1, dh2, dcw1, dcw2, db1, db2), (parts['w_down'],) = glu_bwd(hup, conv_ffn_f, b_conv_ffn, dgl,
                                                                    exchange=([grads['w_down']], True))
    dhup, dconv_ffn, db_ffn = (jnp.concatenate(pair, axis=1) for pair in ((dh1, dh2), (dcw1, dcw2), (db1, db2)))
    grads['w_up'] = mm_tn(u3, dhup, N_DEV, "mm_d_wup")
    du3 = mm_nt(dhup, w_up3, f32, "mm_d_u3")
    dx2, dx2_b, dg_ffn = rms_bwd(x2, g_ffn, [du3], dx3, "rms_bwd_ffn", want_b16=True)
    grads['wo_c'] = mm_tn(o_c, dx2_b, N_DEV, "mm_d_woc")
    do_c = mm_nt(dx2_b, wo_c3, f32, "mm_d_oc")
    dcq, dck, dcv, dg_cq, dg_ck = cross_bwd(cq, ck, cv, g_cq, g_ck, do_c)
    grads['wq_c'] = mm_tn(uc, dcq, 1, "mm_d_wqc").reshape(N_DEV, -1, dcq.shape[1])
    grads['wk_c'] = mm_tn(mem_n, dck, 1, "mm_d_wkc").reshape(N_DEV, -1, dck.shape[1])
    grads['wv_c'] = mm_tn(mem_n, dcv, 1, "mm_d_wvc").reshape(N_DEV, -1, dcv.shape[1])
    duc = mm_nt(dcq, wq_c3, f32, "mm_d_uc")
    dmem_k = mm_nt(dck, wk_c3, f32, "mm_d_memk")
    dmem_v = mm_nt(dcv, wv_c3, f32, "mm_d_memv")
    dg_mem, = rms_bwd(mem2d, g_mem, [dmem_k, dmem_v], None, "rms_bwd_mem", want_dx=False)
    dx1, dx1_b, dg_cross = rms_bwd(x1, g_cross, [duc], dx2, "rms_bwd_cross", want_b16=True)
    grads['w_out'] = mm_tn(merged, dx1_b, 1, "mm_d_wout").reshape(N_DEV, -1, d)
    dmerged = mm_nt(dx1_b, w_out3, f32, "mm_d_merged")
    dga, dgb, dya, dyb = merge_bwd(z_main, ya, yb, dmerged)
    grads['p_a'] = mm_tn(o_ab, dya, 1, "mm_d_pa").reshape(N_DEV, -1, d)
    grads['p_b'] = mm_tn(y_b, dyb, 1, "mm_d_pb").reshape(N_DEV, -1, d)
    do_a = mm_nt(dya, p_a3, f32, "mm_d_oa")
    dy_b = mm_nt(dyb, p_b3, f32, "mm_d_yb")

    dh_ml, dzo, dg_hn = mlstm_out_bwd(h_ml, z_main, g_hn3, dy_b)
    (dq_act, dk_act, dzv, dig, dfg), (parts['w_up'],) = mlstm_bwd(qk_act, z_main, ig, fg, c_all, n_all, m_all, dh_ml,
                                                                 exchange=([grads['w_up']], True))
    dqk = jnp.concatenate([dq_act, dk_act], axis=1)
    dzqk, dconv_qk = qk_conv_bwd(z_main, conv_qk_f, dqk)

    delta = mla_delta(o_a, do_a)
    (dqh, dkh, dvh), got = mla_bwd(qh, kh, vh, do_a, lse, delta.reshape(MLA_HEADS, 1, t),
                                   exchange=([grads[n] for n in behind_in], True))
    parts.update(zip(behind_in, got))
    dq_raw, dkv_raw, dzt_pe, dg_qn, dg_qp, dg_kn, dg_kp = mla_prep_bwd(
        q_raw, kv_raw, z_tail, cos, sin, g_qn_nope, gqp, g_kn_nope, gkp, dqh, dkh, dvh)
    d_wqb_p = mm_tn(qa_n, dq_raw, 1, "mm_d_wqb")[0].reshape(Q_LORA, MLA_HEADS, HEAD_PAD)[:, :, :NOPE + ROPE]
    grads['w_qb'] = d_wqb_p.reshape(Q_LORA, N_DEV, -1).transpose(1, 0, 2)
    grads['w_kvb'] = mm_tn(kv_n, dkv_raw, N_DEV, "mm_d_wkvb")
    dqa = mm_nt(dq_raw, w_qb_p, f32, "mm_d_qa")
    dkvn = mm_nt(dkv_raw, w_kvb3, f32, "mm_d_kvn")
    dz_lat, dg_qa, dg_kva = lat_norm_bwd(z_main, g_qa, g_kva, dqa, dkvn)

    def to_cols(rows):
        return rows.reshape(ML_HEADS, t).T

    dgate = jnp.pad(jnp.concatenate([to_cols(dig), to_cols(dfg)], axis=1), ((0, 0), (T_I, TAIL - T_I - 2 * ML_HEADS)))
    dz_tail, db_if = tail_bwd(z_tail, b_tile, dzt_pe, dgate)
    dz_main = jnp.concatenate([dz_lat, dzqk, dzv, dzo, dga, dgb], axis=1)
    d_wmain = mm_tn(u0, dz_main, 1, "mm_d_wmain")[0]
    d_wtail = mm_tn(u0, dz_tail, 1, "mm_d_wtail")[0]
    d_win = jnp.concatenate([d_wmain[:, :O_Q], d_wtail[:, :ROPE], d_wmain[:, O_Q:O_O], d_wtail[:, T_I:T_I + 2 * ML_HEADS],
                             d_wmain[:, O_O:]], axis=1)
    grads['w_in'] = d_win.reshape(d, N_DEV, -1).transpose(1, 0, 2)
    du0_a, got = mm_nt(dz_main, w_main, f32, "mm_d_u0_main", exchange=([grads[n] for n in first], True))
    parts.update(zip(first, got))
    du0_b = mm_nt(dz_tail, w_tail, f32, "mm_d_u0_tail")
    grad_x, dg_mix = rms_bwd(x2d, g_mix, [du0_a, du0_b], dx1, "rms_bwd_mix")

    out_g, out_d, out_m, out_v = {}, {}, {}, {}
    for n in big:
        shp = args[n].shape
        g, dl, m2, v2 = adam_sum(parts[n], args[n][0], args['m_' + n][0], args['v_' + n][0], "adam_" + n)
        out_g[n], out_d[n], out_m[n], out_v[n] = (a.reshape(shp) for a in (g, dl, m2, v2))

    small_full = {
        'g_mix': dg_mix[0], 'g_qa': dg_qa[0], 'g_kva': dg_kva[0], 'g_qn_nope': dg_qn[0], 'g_qn_pe': dg_qp[0, :ROPE],
        'g_kn_nope': dg_kn[0], 'g_kn_pe': dg_kp[0, :ROPE], 'conv_qk': dconv_qk, 'b_if': db_if[0, T_I:T_I + 2 * ML_HEADS],
        'g_hnorm': dg_hn[:, 0, :], 'g_cross': dg_cross[0], 'g_mem': dg_mem[0], 'g_cq': dg_cq[0], 'g_ck': dg_ck[0],
        'g_ffn': dg_ffn[0], 'conv_ffn': dconv_ffn, 'b_conv_ffn': db_ffn[0], 'loss': loss_acc[0, :1]}
    order = list(small_full)
    packed = _pack([small_full[n] for n in order], 8 * 128).reshape(1, -1)
    gathered_small, = _exchange([packed], "comm_gather_small", scatter=False)
    summed = sum_parts(gathered_small.reshape(N_DEV, -1, 128), "sum_small").reshape(-1)
    full_g = dict(zip(order, _unpack(summed, [small_full[n].shape for n in order])))
    loss = full_g['loss'][0]

    local_g = {}
    for n in replicated:
        local_g[n] = full_g[n].reshape(args[n].shape)
    for n in sharded_small:
        shp = args[n].shape
        full = full_g[n].reshape((1,) + full_g[n].shape)
        local_g[n] = lax.dynamic_slice_in_dim(full, me * shp[-1], shp[-1], axis=2)
    small = replicated + sharded_small
    dl_f, m_f, v_f = adam_flat(*[_pack([src[n] if pre == '' else args[pre + n] for n in small], 8 * 128).reshape(-1, 128)
                                 for pre, src in (('', args), ('', local_g), ('m_', None), ('v_', None))], "adam_small")
    shapes = [args[n].shape for n in small]
    for dst, flat in ((out_d, dl_f), (out_m, m_f), (out_v, v_f)):
        dst.update(zip(small, _unpack(flat.reshape(-1), shapes)))
    out_g.update(local_g)

    return (loss, grad_x[None], *[out_g[n] for n in names], *[out_d[n] for n in names],
            *[out_m[n] for n in names], *[out_v[n] for n in names])
```

```python
import functools

import jax
import jax.numpy as jnp
from jax import lax
from jax.experimental import pallas as pl
from jax.experimental.pallas import tpu as pltpu

f32 = jnp.float32
bf16 = jnp.bfloat16

N_DEV = 8
EPS = 1e-6
CHUNK = 64
CHUNK_SHIFT = 6
assert 1 << CHUNK_SHIFT == CHUNK
MLA_HEADS = 16
Q_LORA = 512
KV_LORA = 512
NOPE = 128
ROPE = 64
V_HEAD = 128
ROPE_BASE = 10000.0
HEAD_PAD = 256
ML_HEADS = 8
ML_DK = 128
ML_DV = 256
ML_CONV = 4
ML_QK = ML_HEADS * ML_DK
ML_V = ML_HEADS * ML_DV
CR_HEADS = 4
CR_HD = 128
FFN_CONV = 3
ADAM_LR = 0.001
ADAM_B1 = 0.9
ADAM_B2 = 0.999
ADAM_EPS = 1e-08
ADAM_WD = 0.01
ADAM_STEP = 10
O_QA, O_KV, O_Q, O_K = 0, Q_LORA, Q_LORA + KV_LORA, Q_LORA + KV_LORA + ML_QK
O_V = O_K + ML_QK
O_O = O_V + ML_V
O_GA = O_O + ML_V
TAIL = 128
T_I, T_F = ROPE, ROPE + ML_HEADS
VMEM_LIMIT_V7X = 48 * 1024 * 1024
MESH = pl.DeviceIdType.MESH


def _call(body, name, grid, in_specs, out_specs, out_shape, scratch=(), exchange=None):
    params = pltpu.CompilerParams(vmem_limit_bytes=VMEM_LIMIT_V7X)
    if exchange is None:
        return pl.pallas_call(body, name=name, grid=grid, in_specs=in_specs, out_specs=out_specs, out_shape=out_shape,
                              scratch_shapes=list(scratch), compiler_params=params)
    arrs, scatter = exchange
    single = not isinstance(out_specs, (list, tuple))
    o_specs = [out_specs] if single else list(out_specs)
    o_shape = [out_shape] if single else list(out_shape)
    n_in, n_out, n_sc, n = len(in_specs), len(o_specs), len(scratch), len(arrs)
    any_spec = pl.BlockSpec(memory_space=pl.ANY)

    def body_with_exchange(*refs):
        pos = [0]

        def take(k):
            pos[0] += k
            return refs[pos[0] - k:pos[0]]

        ins, ex_in, outs, ex_out, sc = take(n_in), take(n), take(n_out), take(n), take(n_sc)
        start, wait = _exchange_ops(ex_in, ex_out, refs[pos[0]:], scatter)
        ids = [pl.program_id(a) for a in range(len(grid))]
        pl.when(_first(*ids))(start)
        body(*ins, *outs, *sc)
        last = ids[0] == grid[0] - 1
        for a in range(1, len(grid)):
            last = jnp.logical_and(last, ids[a] == grid[a] - 1)
        pl.when(last)(wait)

    call = pl.pallas_call(body_with_exchange, name="comm_" + name, grid=grid, in_specs=list(in_specs) + [any_spec] * n,
                          out_specs=o_specs + [any_spec] * n, out_shape=o_shape + _exchange_shapes(arrs, scatter),
                          scratch_shapes=list(scratch) + _exchange_sems(n), compiler_params=params)

    def run(*operands):
        res = call(*operands, *arrs)
        return (res[0] if single else list(res[:n_out])), list(res[n_out:])

    return run


def _tile(n, cands):
    for c in cands:
        if n % c == 0:
            return c
    return n


def _sds(shape, dtype):
    return jax.ShapeDtypeStruct(tuple(shape), dtype)


def _bdot(a, b, ca, cb):
    return lax.dot_general(a.astype(bf16), b.astype(bf16), (((ca,), (cb,)), ((), ())), preferred_element_type=f32)


_BIG = (1024, 512, 256, 128)


def _col_tile(nb):
    return nb if nb <= 1536 else _tile(nb, _BIG)


_DEEP = (2048, 1024, 512, 256, 128)


def _mm_call(name, grid, in_specs, out_spec, out_shape, tile, nk, ca, cb, exchange, operands):
    def dot(a_ref, w_ref):
        return _bdot(a_ref[...], w_ref[0] if len(w_ref.shape) == 3 else w_ref[...], ca, cb)

    def store(o_ref, val):
        if len(o_ref.shape) == 3:
            o_ref[0] = val.astype(o_ref.dtype)
        else:
            o_ref[...] = val.astype(o_ref.dtype)

    if nk == 1:
        def body(a_ref, w_ref, o_ref):
            store(o_ref, dot(a_ref, w_ref))

        scratch = []
    else:
        def body(a_ref, w_ref, o_ref, acc):
            kk = pl.program_id(2)

            @pl.when(kk == 0)
            def _():
                acc[...] = jnp.zeros_like(acc)

            acc[...] += dot(a_ref, w_ref)

            @pl.when(kk == nk - 1)
            def _():
                store(o_ref, acc[...])

        scratch = [pltpu.VMEM(tile, f32)]
    return _call(body, name, grid, in_specs, out_spec, out_shape, scratch, exchange=exchange)(*operands)


def mm_nn(a, w3, out_dtype, name, exchange=None):
    m, k = a.shape
    nblk, k2, nb = w3.shape
    assert k == k2
    tm, tk, tn = _tile(m, _BIG), _tile(k, _DEEP), _col_tile(nb)
    per, nk = nb // tn, k // tk
    return _mm_call(name, (m // tm, nblk * per, nk),
                    [pl.BlockSpec((tm, tk), lambda i, j, kk: (i, kk)),
                     pl.BlockSpec((1, tk, tn), lambda i, j, kk: (j // per, kk, j % per))],
                    pl.BlockSpec((tm, tn), lambda i, j, kk: (i, j)), _sds((m, nblk * nb), out_dtype),
                    (tm, tn), nk, 1, 0, exchange, (a, w3))


def mm_nt(a, w3, out_dtype, name, exchange=None):
    m, n = a.shape
    nblk, k, nb = w3.shape
    assert n == nblk * nb
    tm, tn = _tile(m, _BIG), _tile(k, _BIG)
    tc = nb if nb <= 1536 else _tile(nb, _DEEP)
    per = nb // tc
    nk = nblk * per
    return _mm_call(name, (m // tm, k // tn, nk),
                    [pl.BlockSpec((tm, tc), lambda i, j, kk: (i, kk)),
                     pl.BlockSpec((1, tn, tc), lambda i, j, kk: (kk // per, j, kk % per))],
                    pl.BlockSpec((tm, tn), lambda i, j, kk: (i, j)), _sds((m, k), out_dtype),
                    (tm, tn), nk, 1, 1, exchange, (a, w3))


def mm_tn(a, b, nblk, name):
    r, m = a.shape
    r2, n = b.shape
    assert r == r2 and n % nblk == 0
    nb = n // nblk
    tm, tk, tn = _tile(m, _BIG), _tile(r, _DEEP), _col_tile(nb)
    per, nk = nb // tn, r // tk
    return _mm_call(name, (m // tm, nblk * per, nk),
                    [pl.BlockSpec((tk, tm), lambda i, j, kk: (kk, i)),
                     pl.BlockSpec((tk, tn), lambda i, j, kk: (kk, j))],
                    pl.BlockSpec((1, tm, tn), lambda i, j, kk: (j // per, i, j % per)), _sds((nblk, m, nb), bf16),
                    (tm, tn), nk, 0, 0, None, (a, b))


def _rms(x, g):
    return x * lax.rsqrt(jnp.mean(x * x, axis=-1, keepdims=True) + EPS) * g


def _rms_pad(x, g, width):
    return x * lax.rsqrt(jnp.sum(x * x, axis=-1, keepdims=True) / width + EPS) * g


def _first(*ids):
    ok = ids[0] == 0
    for i in ids[1:]:
        ok = jnp.logical_and(ok, i == 0)
    return ok


def _acc_row(ref, val, first):
    @pl.when(first)
    def _():
        ref[...] = jnp.zeros_like(ref)

    ref[0:1, :] += val


def rms_fwd(x, g, name):
    r, w = x.shape
    tm = _tile(r, (256, 128, 64, 32, 16, 8))

    def body(x_ref, g_ref, o_ref):
        o_ref[...] = _rms(x_ref[...], g_ref[...]).astype(bf16)

    return _call(body, name, (r // tm,), [pl.BlockSpec((tm, w), lambda i: (i, 0)), pl.BlockSpec((1, w), lambda i: (0, 0))],
                 pl.BlockSpec((tm, w), lambda i: (i, 0)), _sds((r, w), bf16))(x, g)


def resid_rms(xa, xb, g, name):
    r, w = xa.shape
    tm = _tile(r, (256, 128, 64, 32, 16, 8))

    def body(a_ref, b_ref, g_ref, s_ref, u_ref):
        xs = a_ref[...] + b_ref[...]
        s_ref[...] = xs
        u_ref[...] = _rms(xs, g_ref[...]).astype(bf16)

    row = pl.BlockSpec((tm, w), lambda i: (i, 0))
    return _call(body, name, (r // tm,), [row, row, pl.BlockSpec((1, w), lambda i: (0, 0))], [row, row],
                 [_sds((r, w), f32), _sds((r, w), bf16)])(xa, xb, g)


def rms_bwd(x, g, dys, dres, name, want_dx=True, want_b16=False):
    r, w = x.shape
    tm = _tile(r, (256, 128, 64, 32, 16, 8))
    nd = len(dys)

    def body(*refs):
        x_ref, g_ref = refs[0], refs[1]
        dy = refs[2][...]
        for j in range(1, nd):
            dy = dy + refs[2 + j][...]
        pos = 2 + nd
        _, vjp = jax.vjp(_rms, x_ref[...], g_ref[...])
        dx, dg = vjp(dy)
        if dres is not None:
            dx = dx + refs[pos][...]
            pos += 1
        if want_dx:
            refs[pos][...] = dx
            pos += 1
        if want_b16:
            refs[pos][...] = dx.astype(bf16)
            pos += 1
        _acc_row(refs[pos], dg, pl.program_id(0) == 0)

    row = pl.BlockSpec((tm, w), lambda i: (i, 0))
    ins = [x, g] + list(dys) + ([dres] if dres is not None else [])
    in_specs = [row, pl.BlockSpec((1, w), lambda i: (0, 0))] + [row] * (nd + (dres is not None))
    out_specs = [row] * (want_dx + want_b16) + [pl.BlockSpec((8, w), lambda i: (0, 0))]
    out_shape = ([_sds((r, w), f32)] if want_dx else []) + ([_sds((r, w), bf16)] if want_b16 else []) + [_sds((8, w), f32)]
    return _call(body, name, (r // tm,), in_specs, out_specs, out_shape)(*ins)


def lat_norm(z_main, g_qa, g_kva):
    t = z_main.shape[0]
    tm = _tile(t, (512, 256, 128, 64))

    def body(z_ref, gq_ref, gk_ref, q_ref, k_ref):
        q_ref[...] = _rms(z_ref[:, :Q_LORA], gq_ref[...]).astype(bf16)
        k_ref[...] = _rms(z_ref[:, Q_LORA:], gk_ref[...]).astype(bf16)

    return _call(body, "lat_norm", (t // tm,),
                 [pl.BlockSpec((tm, Q_LORA + KV_LORA), lambda i: (i, 0)), pl.BlockSpec((1, Q_LORA), lambda i: (0, 0)),
                  pl.BlockSpec((1, KV_LORA), lambda i: (0, 0))],
                 [pl.BlockSpec((tm, Q_LORA), lambda i: (i, 0)), pl.BlockSpec((tm, KV_LORA), lambda i: (i, 0))],
                 [_sds((t, Q_LORA), bf16), _sds((t, KV_LORA), bf16)])(z_main, g_qa, g_kva)


def lat_norm_bwd(z_main, g_qa, g_kva, dqa, dkv):
    t = z_main.shape[0]
    tm = _tile(t, (512, 256, 128, 64))

    def body(z_ref, gq_ref, gk_ref, dq_ref, dk_ref, dz_ref, dgq_ref, dgk_ref):
        first = pl.program_id(0) == 0
        _, vq = jax.vjp(_rms, z_ref[:, :Q_LORA], gq_ref[...])
        dx, dg = vq(dq_ref[...])
        dz_ref[:, :Q_LORA] = dx.astype(bf16)
        _acc_row(dgq_ref, dg, first)
        _, vk = jax.vjp(_rms, z_ref[:, Q_LORA:], gk_ref[...])
        dx, dg = vk(dk_ref[...])
        dz_ref[:, Q_LORA:] = dx.astype(bf16)
        _acc_row(dgk_ref, dg, first)

    return _call(body, "lat_norm_bwd", (t // tm,),
                 [pl.BlockSpec((tm, Q_LORA + KV_LORA), lambda i: (i, 0)), pl.BlockSpec((1, Q_LORA), lambda i: (0, 0)),
                  pl.BlockSpec((1, KV_LORA), lambda i: (0, 0)), pl.BlockSpec((tm, Q_LORA), lambda i: (i, 0)),
                  pl.BlockSpec((tm, KV_LORA), lambda i: (i, 0))],
                 [pl.BlockSpec((tm, Q_LORA + KV_LORA), lambda i: (i, 0)), pl.BlockSpec((8, Q_LORA), lambda i: (0, 0)),
                  pl.BlockSpec((8, KV_LORA), lambda i: (0, 0))],
                 [_sds((t, Q_LORA + KV_LORA), bf16), _sds((8, Q_LORA), f32), _sds((8, KV_LORA), f32)])(z_main, g_qa, g_kva, dqa, dkv)


def rope_tables(pos_col, inv_freq):
    t = pos_col.shape[0]
    tm = _tile(t, (512, 256, 128, 64))

    def body(p_ref, f_ref, c_ref, s_ref):
        ang = p_ref[...].astype(f32) * f_ref[...]
        lane = lax.broadcasted_iota(jnp.int32, ang.shape, 1)
        c_ref[...] = jnp.where(lane < ROPE, jnp.cos(ang), 0.0)
        sn = jnp.sin(ang)
        s_ref[...] = jnp.where(lane < ROPE // 2, -sn, jnp.where(lane < ROPE, sn, 0.0))

    return _call(body, "rope_tables", (t // tm,),
                 [pl.BlockSpec((tm, 1), lambda i: (i, 0)), pl.BlockSpec((1, TAIL), lambda i: (0, 0))],
                 [pl.BlockSpec((tm, TAIL), lambda i: (i, 0))] * 2, [_sds((t, TAIL), f32)] * 2)(pos_col, inv_freq)


def _swap_halves(n):
    lane = lax.broadcasted_iota(jnp.int32, n.shape, 1)
    return jnp.where(lane < ROPE // 2, pltpu.roll(n, TAIL - ROPE // 2, 1), pltpu.roll(n, ROPE // 2, 1))


def _rope(n, c, s):
    return n * c + _swap_halves(n) * s


def _rope_t(d, c, s):
    return d * c + _swap_halves(d * s)


def _prep_specs(tm):
    head = pl.BlockSpec((tm, HEAD_PAD), lambda i, h: (i, h))
    row = pl.BlockSpec((tm, TAIL), lambda i, h: (i, 0))
    gain = pl.BlockSpec((1, TAIL), lambda i, h: (0, 0))
    return head, row, gain


def _pe_in(zt):
    lane = lax.broadcasted_iota(jnp.int32, zt.shape, 1)
    return jnp.where(lane < ROPE, zt, 0.0)


def mla_prep(q_raw, kv_raw, z_tail, cos, sin, gqn, gqp, gkn, gkp):
    t = q_raw.shape[0]
    tm = _tile(t, (512, 256, 128, 64))

    def body(q_ref, kv_ref, zt_ref, c_ref, s_ref, gqn_ref, gqp_ref, gkn_ref, gkp_ref, qh_ref, kh_ref, vh_ref):
        c, s = c_ref[...], s_ref[...]
        qh_ref[:, :NOPE] = _rms(q_ref[:, :NOPE], gqn_ref[...]).astype(bf16)
        qh_ref[:, NOPE:] = _rope(_rms_pad(q_ref[:, NOPE:], gqp_ref[...], ROPE), c, s).astype(bf16)
        kh_ref[:, :NOPE] = _rms(kv_ref[:, :NOPE], gkn_ref[...]).astype(bf16)
        kh_ref[:, NOPE:] = _rope(_rms_pad(_pe_in(zt_ref[...]), gkp_ref[...], ROPE), c, s).astype(bf16)
        vh_ref[...] = kv_ref[:, NOPE:].astype(bf16)

    head, row, gain = _prep_specs(tm)
    return _call(body, "mla_prep", (t // tm, MLA_HEADS), [head, head, row, row, row, gain, gain, gain, gain],
                 [head, head, pl.BlockSpec((tm, V_HEAD), lambda i, h: (i, h))],
                 [_sds((t, MLA_HEADS * HEAD_PAD), bf16), _sds((t, MLA_HEADS * HEAD_PAD), bf16), _sds((t, MLA_HEADS * V_HEAD), bf16)],
                 )(q_raw, kv_raw, z_tail, cos, sin, gqn, gqp, gkn, gkp)


def mla_prep_bwd(q_raw, kv_raw, z_tail, cos, sin, gqn, gqp, gkn, gkp, dqh, dkh, dvh):
    t = q_raw.shape[0]
    tm = _tile(t, (512, 256, 128, 64))
    pad_norm = functools.partial(_rms_pad, width=ROPE)

    def body(q_ref, kv_ref, zt_ref, c_ref, s_ref, gqn_ref, gqp_ref, gkn_ref, gkp_ref, dqh_ref, dkh_ref, dvh_ref,
             dq_ref, dkv_ref, dzt_ref, dgqn_ref, dgqp_ref, dgkn_ref, dgkp_ref):
        i, h = pl.program_id(0), pl.program_id(1)
        first = _first(i, h)
        c, s = c_ref[...], s_ref[...]
        _, v1 = jax.vjp(_rms, q_ref[:, :NOPE], gqn_ref[...])
        dx, dg = v1(dqh_ref[:, :NOPE])
        dq_ref[:, :NOPE] = dx.astype(bf16)
        _acc_row(dgqn_ref, dg, first)
        _, v2 = jax.vjp(pad_norm, q_ref[:, NOPE:], gqp_ref[...])
        dx, dg = v2(_rope_t(dqh_ref[:, NOPE:], c, s))
        dq_ref[:, NOPE:] = dx.astype(bf16)
        _acc_row(dgqp_ref, dg, first)
        _, v3 = jax.vjp(_rms, kv_ref[:, :NOPE], gkn_ref[...])
        dx, dg = v3(dkh_ref[:, :NOPE])
        dkv_ref[:, :NOPE] = dx.astype(bf16)
        _acc_row(dgkn_ref, dg, first)
        dkv_ref[:, NOPE:] = dvh_ref[...].astype(bf16)
        _, v4 = jax.vjp(pad_norm, _pe_in(zt_ref[...]), gkp_ref[...])
        dx, dg = v4(_rope_t(dkh_ref[:, NOPE:], c, s))
        _acc_row(dgkp_ref, dg, first)

        @pl.when(h == 0)
        def _():
            dzt_ref[...] = jnp.zeros_like(dzt_ref)

        dzt_ref[...] += dx

    head, row, gain = _prep_specs(tm)
    acc = pl.BlockSpec((8, TAIL), lambda i, h: (0, 0))
    vspec = pl.BlockSpec((tm, V_HEAD), lambda i, h: (i, h))
    return _call(body, "mla_prep_bwd", (t // tm, MLA_HEADS),
                 [head, head, row, row, row, gain, gain, gain, gain, head, head, vspec],
                 [head, head, row, acc, acc, acc, acc],
                 [_sds((t, MLA_HEADS * HEAD_PAD), bf16), _sds((t, MLA_HEADS * HEAD_PAD), bf16), _sds((t, TAIL), f32)]
                 + [_sds((8, TAIL), f32)] * 4)(q_raw, kv_raw, z_tail, cos, sin, gqn, gqp, gkn, gkp, dqh, dkh, dvh)


ATT_BLOCK = 512
NEG = -1e30
ATT_SCALE = (NOPE + ROPE) ** -0.5
ATT_HEADS = 2


def _chunk_visible(shape, key_axis):
    kc = lax.broadcasted_iota(jnp.int32, shape, key_axis) >> CHUNK_SHIFT
    qc = lax.broadcasted_iota(jnp.int32, shape, 1 - key_axis) >> CHUNK_SHIFT
    return kc <= qc


def mla_fwd(qh, kh, vh, exchange=None):
    t = qh.shape[0]
    tb = min(ATT_BLOCK, t)
    nb = t // tb

    hp = ATT_HEADS

    def body(q_ref, k_ref, v_ref, o_ref, ob_ref, lse_ref, m_s, l_s, acc):
        qi, ki = pl.program_id(1), pl.program_id(2)

        @pl.when(ki == 0)
        def _():
            m_s[...] = jnp.full_like(m_s, NEG)
            l_s[...] = jnp.zeros_like(l_s)
            acc[...] = jnp.zeros_like(acc)

        def step(diagonal):
            new = []
            for j in range(hp):
                q, k = q_ref[:, j * HEAD_PAD:(j + 1) * HEAD_PAD], k_ref[:, j * HEAD_PAD:(j + 1) * HEAD_PAD]
                s = _bdot(k, q, 1, 1) * ATT_SCALE
                if diagonal:
                    s = jnp.where(_chunk_visible(s.shape, 0), s, -jnp.inf)
                m_old = m_s[j]
                m_new = jnp.maximum(m_old, jnp.max(s, axis=0, keepdims=True))
                p = jnp.exp(s - m_new)
                alpha = jnp.exp(m_old - m_new)
                l_new = alpha * l_s[j] + jnp.sum(p, axis=0, keepdims=True)
                acc_new = alpha * acc[j] + _bdot(v_ref[:, j * V_HEAD:(j + 1) * V_HEAD], p, 0, 0)
                new.append((m_new, l_new, acc_new))
            for j, (m_new, l_new, acc_new) in enumerate(new):
                m_s[j] = m_new
                l_s[j] = l_new
                acc[j] = acc_new
            return new

        @pl.when(ki < qi)
        def _():
            step(False)

        @pl.when(ki == qi)
        def _():
            for j, (m_new, l_new, acc_new) in enumerate(step(True)):
                o = (acc_new / l_new).T
                o_ref[:, j * V_HEAD:(j + 1) * V_HEAD] = o
                ob_ref[:, j * V_HEAD:(j + 1) * V_HEAD] = o.astype(bf16)
                lse_ref[j] = m_new + jnp.log(l_new)

    kv = lambda g, qi, ki: (jnp.minimum(ki, qi), g)
    o_spec = pl.BlockSpec((tb, hp * V_HEAD), lambda g, qi, ki: (qi, g))
    return _call(body, "mla_fwd", (MLA_HEADS // hp, nb, nb),
                 [pl.BlockSpec((tb, hp * HEAD_PAD), lambda g, qi, ki: (qi, g)), pl.BlockSpec((tb, hp * HEAD_PAD), kv),
                  pl.BlockSpec((tb, hp * V_HEAD), kv)],
                 [o_spec, o_spec, pl.BlockSpec((hp, 1, tb), lambda g, qi, ki: (g, 0, qi))],
                 [_sds((t, MLA_HEADS * V_HEAD), f32), _sds((t, MLA_HEADS * V_HEAD), bf16), _sds((MLA_HEADS, 1, t), f32)],
                 [pltpu.VMEM((hp, 1, tb), f32), pltpu.VMEM((hp, 1, tb), f32), pltpu.VMEM((hp, V_HEAD, tb), f32)],
                 exchange=exchange)(qh, kh, vh)


def mla_delta(o, do):
    t = o.shape[0]
    tm = _tile(t, (512, 256, 128, 64))

    def body(o_ref, do_ref, d_ref):
        d_ref[0] = jnp.sum(o_ref[...] * do_ref[...], axis=1, keepdims=True)

    blk = pl.BlockSpec((tm, V_HEAD), lambda i, h: (i, h))
    return _call(body, "mla_delta", (t // tm, MLA_HEADS), [blk, blk], pl.BlockSpec((1, tm, 1), lambda i, h: (h, i, 0)),
                 _sds((MLA_HEADS, t, 1), f32))(o, do)


def mla_bwd(qh, kh, vh, do, lse_row, delta_row, exchange=None):
    t = qh.shape[0]
    tb = min(ATT_BLOCK, t)
    nb = t // tb

    hp = ATT_HEADS

    def body(q_ref, k_ref, v_ref, do_ref, lse_ref, dl_ref, dq_ref, dk_ref, dv_ref, dk_acc, dv_acc):
        ki, qi = pl.program_id(1), pl.program_id(2)

        @pl.when(jnp.logical_and(ki == 0, qi == 0))
        def _():
            dq_ref[...] = jnp.zeros_like(dq_ref)

        @pl.when(qi == 0)
        def _():
            dk_acc[...] = jnp.zeros_like(dk_acc)
            dv_acc[...] = jnp.zeros_like(dv_acc)

        def step(diagonal):
            rows = pl.ds(pl.multiple_of(qi * tb, tb), tb)
            new = []
            for j in range(hp):
                qc, vc = slice(j * HEAD_PAD, (j + 1) * HEAD_PAD), slice(j * V_HEAD, (j + 1) * V_HEAD)
                q, k, do_b = q_ref[:, qc], k_ref[:, qc], do_ref[:, vc]
                s = _bdot(k, q, 1, 1) * ATT_SCALE
                if diagonal:
                    s = jnp.where(_chunk_visible(s.shape, 0), s, -jnp.inf)
                p = jnp.exp(s - lse_ref[j])
                dp = _bdot(v_ref[:, vc], do_b, 1, 1)
                ds = p * (dp - dl_ref[j]) * ATT_SCALE
                new.append((dv_acc[:, vc] + _bdot(p, do_b, 1, 0), dk_acc[:, qc] + _bdot(ds, q, 1, 0),
                            dq_ref[rows, qc] + _bdot(ds, k, 0, 0)))
            for j, (dv, dk, dq) in enumerate(new):
                dv_acc[:, j * V_HEAD:(j + 1) * V_HEAD] = dv
                dk_acc[:, j * HEAD_PAD:(j + 1) * HEAD_PAD] = dk
                dq_ref[rows, j * HEAD_PAD:(j + 1) * HEAD_PAD] = dq

        @pl.when(qi > ki)
        def _():
            step(False)

        @pl.when(qi == ki)
        def _():
            step(True)

        @pl.when(qi == nb - 1)
        def _():
            dk_ref[...] = dk_acc[...]
            dv_ref[...] = dv_acc[...]

    qs = lambda g, ki, qi: (jnp.maximum(qi, ki), g)
    ks = lambda g, ki, qi: (ki, g)
    vec = pl.BlockSpec((hp, 1, tb), lambda g, ki, qi: (g, 0, jnp.maximum(qi, ki)))
    return _call(body, "mla_bwd", (MLA_HEADS // hp, nb, nb),
                 [pl.BlockSpec((tb, hp * HEAD_PAD), qs), pl.BlockSpec((tb, hp * HEAD_PAD), ks), pl.BlockSpec((tb, hp * V_HEAD), ks),
                  pl.BlockSpec((tb, hp * V_HEAD), qs), vec, vec],
                 [pl.BlockSpec((t, hp * HEAD_PAD), lambda g, ki, qi: (0, g)), pl.BlockSpec((tb, hp * HEAD_PAD), ks),
                  pl.BlockSpec((tb, hp * V_HEAD), ks)],
                 [_sds((t, MLA_HEADS * HEAD_PAD), f32), _sds((t, MLA_HEADS * HEAD_PAD), f32), _sds((t, MLA_HEADS * V_HEAD), f32)],
                 [pltpu.VMEM((tb, hp * HEAD_PAD), f32), pltpu.VMEM((tb, hp * V_HEAD), f32)], exchange=exchange)(
        qh, kh, vh, do, lse_row, delta_row)


PAD = 8


def _conv_taps(pad_ref, w, width, t):
    y = pad_ref[PAD - width + 1:PAD - width + 1 + t, :] * w[0:1, :]
    for j in range(1, width):
        y = y + pad_ref[PAD - width + 1 + j:PAD - width + 1 + j + t, :] * w[j:j + 1, :]
    return y


def _conv_bwd(xpad_ref, dpad_ref, w, da, width, t):
    dpad_ref[0:t, :] = da
    dpad_ref[t:t + PAD, :] = jnp.zeros((PAD, da.shape[1]), f32)
    dx = dpad_ref[width - 1:width - 1 + t, :] * w[0:1, :]
    for j in range(1, width):
        dx = dx + dpad_ref[width - 1 - j:width - 1 - j + t, :] * w[j:j + 1, :]
    dws = [jnp.sum(da * xpad_ref[PAD - width + 1 + j:PAD - width + 1 + j + t, :], axis=0, keepdims=True) for j in range(width)]
    return dx, dws


def _load_pad(pad_ref, x, t):
    pad_ref[0:PAD, :] = jnp.zeros((PAD, x.shape[1]), f32)
    pad_ref[PAD:PAD + t, :] = x


assert ML_DK == 128


def qk_conv(z_main, conv_qk):
    t = z_main.shape[0]
    base = O_Q // ML_DK

    def body(z_ref, w_ref, o_ref, pad):
        _load_pad(pad, z_ref[...], t)
        a = _conv_taps(pad, w_ref[...], ML_CONV, t)
        sc = jnp.where(pl.program_id(0) < ML_HEADS, ML_DK ** -0.5, 1.0)
        o_ref[0] = jax.nn.silu(a) * sc

    return _call(body, "qk_conv", (2 * ML_HEADS,),
                 [pl.BlockSpec((t, ML_DK), lambda j: (0, base + j)), pl.BlockSpec((ML_CONV, ML_DK), lambda j: (0, j))],
                 pl.BlockSpec((1, t, ML_DK), lambda j: (j, 0, 0)), _sds((2 * ML_HEADS, t, ML_DK), f32),
                 [pltpu.VMEM((t + PAD, ML_DK), f32)])(z_main, conv_qk)


def qk_conv_bwd(z_main, conv_qk, dq, dk):
    t = z_main.shape[0]
    base = O_Q // ML_DK

    def body(z_ref, w_ref, dq_ref, dk_ref, dz_ref, dw_ref, pad, dpad):
        _load_pad(pad, z_ref[...], t)
        w = w_ref[...]
        a = _conv_taps(pad, w, ML_CONV, t)
        is_q = pl.program_id(0) < ML_HEADS
        d = jnp.where(is_q, dq_ref[0] * (ML_DK ** -0.5), dk_ref[0])
        _, vjp = jax.vjp(jax.nn.silu, a)
        da, = vjp(d)
        dx, dws = _conv_bwd(pad, dpad, w, da, ML_CONV, t)
        dz_ref[...] = dx.astype(bf16)
        for j in range(ML_CONV):
            dw_ref[j:j + 1, :] = dws[j]

    head = lambda pick: pl.BlockSpec((1, t, ML_DK), lambda j: (pick(j), 0, 0))
    return _call(body, "qk_conv_bwd", (2 * ML_HEADS,),
                 [pl.BlockSpec((t, ML_DK), lambda j: (0, base + j)), pl.BlockSpec((ML_CONV, ML_DK), lambda j: (0, j)),
                  head(lambda j: jnp.minimum(j, ML_HEADS - 1)), head(lambda j: jnp.maximum(j - ML_HEADS, 0))],
                 [pl.BlockSpec((t, ML_DK), lambda j: (0, j)), pl.BlockSpec((ML_CONV, ML_DK), lambda j: (0, j))],
                 [_sds((t, 2 * ML_QK), bf16), _sds((ML_CONV, 2 * ML_QK), f32)],
                 [pltpu.VMEM((t + PAD, ML_DK), f32), pltpu.VMEM((t + PAD, ML_DK), f32)])(z_main, conv_qk, dq, dk)


def glu_fwd(hup, conv_w, bias):
    t, f2 = hup.shape
    nf = f2 // 2 // 128

    def body(h1_ref, h2_ref, w1_ref, w2_ref, b1_ref, b2_ref, o_ref, pad):
        _load_pad(pad, h1_ref[...], t)
        a1 = _conv_taps(pad, w1_ref[...], FFN_CONV, t) + b1_ref[...]
        _load_pad(pad, h2_ref[...], t)
        a2 = _conv_taps(pad, w2_ref[...], FFN_CONV, t) + b2_ref[...]
        o_ref[...] = (jax.nn.silu(a1) * a2).astype(bf16)

    col = lambda off: pl.BlockSpec((t, 128), lambda j: (0, j + off))
    wsp = lambda off: pl.BlockSpec((FFN_CONV, 128), lambda j: (0, j + off))
    bsp = lambda off: pl.BlockSpec((1, 128), lambda j: (0, j + off))
    return _call(body, "glu_fwd", (nf,), [col(0), col(nf), wsp(0), wsp(nf), bsp(0), bsp(nf)], col(0), _sds((t, f2 // 2), bf16),
                 [pltpu.VMEM((t + PAD, 128), f32)])(hup, hup, conv_w, conv_w, bias, bias)


def glu_bwd(hup, conv_w, bias, dg, exchange=None):
    t, f2 = hup.shape
    f = f2 // 2
    nf = f // 128

    def body(h1_ref, h2_ref, w1_ref, w2_ref, b1_ref, b2_ref, dg_ref, dh1_ref, dh2_ref, dw1_ref, dw2_ref, db1_ref, db2_ref,
             pad1, pad2, dpad):
        _load_pad(pad1, h1_ref[...], t)
        _load_pad(pad2, h2_ref[...], t)
        w1, w2 = w1_ref[...], w2_ref[...]
        a1 = _conv_taps(pad1, w1, FFN_CONV, t) + b1_ref[...]
        a2 = _conv_taps(pad2, w2, FFN_CONV, t) + b2_ref[...]
        d = dg_ref[...]
        _, vjp = jax.vjp(jax.nn.silu, a1)
        da1, = vjp(d * a2)
        da2 = d * jax.nn.silu(a1)
        for da, pad, w, dh_ref, dw_ref, db_ref in ((da1, pad1, w1, dh1_ref, dw1_ref, db1_ref), (da2, pad2, w2, dh2_ref, dw2_ref, db2_ref)):
            dx, dws = _conv_bwd(pad, dpad, w, da, FFN_CONV, t)
            dh_ref[...] = dx.astype(bf16)
            for j in range(FFN_CONV):
                dw_ref[j:j + 1, :] = dws[j]
            db_ref[...] = jnp.sum(da, axis=0, keepdims=True)

    col = lambda off: pl.BlockSpec((t, 128), lambda j: (0, j + off))
    wsp = lambda off: pl.BlockSpec((FFN_CONV, 128), lambda j: (0, j + off))
    bsp = lambda off: pl.BlockSpec((1, 128), lambda j: (0, j + off))
    return _call(body, "glu_bwd", (nf,), [col(0), col(nf), wsp(0), wsp(nf), bsp(0), bsp(nf), col(0)],
                 [col(0), col(0), wsp(0), wsp(0), bsp(0), bsp(0)],
                 [_sds((t, f), bf16)] * 2 + [_sds((FFN_CONV, f), f32)] * 2 + [_sds((1, f), f32)] * 2,
                 [pltpu.VMEM((t + PAD, 128), f32)] * 3, exchange=exchange)(hup, hup, conv_w, conv_w, bias, bias, dg)


def gate_act(z_tail, b_tile):
    t = z_tail.shape[0]
    tm = _tile(t, (512, 256, 128, 64))

    def body(z_ref, b_ref, o_ref):
        x = z_ref[...] + b_ref[...]
        lane = lax.broadcasted_iota(jnp.int32, x.shape, 1)
        o_ref[...] = jnp.where(lane < T_F, x, jax.nn.log_sigmoid(x))

    row = pl.BlockSpec((tm, TAIL), lambda i: (i, 0))
    return _call(body, "gate_act", (t // tm,), [row, pl.BlockSpec((1, TAIL), lambda i: (0, 0))], row, _sds((t, TAIL), f32))(z_tail, b_tile)


def tail_bwd(z_tail, b_tile, dzt_pe, dgate):
    t = z_tail.shape[0]
    tm = _tile(t, (512, 256, 128, 64))

    def body(z_ref, b_ref, dpe_ref, dg_ref, dz_ref, db_ref):
        x = z_ref[...] + b_ref[...]
        lane = lax.broadcasted_iota(jnp.int32, x.shape, 1)
        _, vjp = jax.vjp(jax.nn.log_sigmoid, x)
        df, = vjp(dg_ref[...])
        dgates = jnp.where(lane < T_F, dg_ref[...], df)
        dgates = jnp.where(jnp.logical_and(lane >= T_I, lane < T_F + ML_HEADS), dgates, 0.0)
        dz_ref[...] = jnp.where(lane < ROPE, dpe_ref[...], dgates).astype(bf16)
        _acc_row(db_ref, jnp.sum(dgates, axis=0, keepdims=True), pl.program_id(0) == 0)

    row = pl.BlockSpec((tm, TAIL), lambda i: (i, 0))
    return _call(body, "tail_bwd", (t // tm,), [row, pl.BlockSpec((1, TAIL), lambda i: (0, 0)), row, row],
                 [row, pl.BlockSpec((8, TAIL), lambda i: (0, 0))], [_sds((t, TAIL), bf16), _sds((8, TAIL), f32)])(z_tail, b_tile, dzt_pe, dgate)


def _hdot(a, b, ca, cb):
    return lax.dot_general(a.astype(bf16), b.astype(bf16), (((ca,), (cb,)), ((0,), (0,))), preferred_element_type=f32)


def _mlstm_step(q, k, v, igr, fgr, c_mat, n_vec, m):
    nh, ln = q.shape[0], CHUNK
    sq = (nh, ln, ln)
    row = lax.broadcasted_iota(jnp.int32, sq, 1)
    col = lax.broadcasted_iota(jnp.int32, sq, 2)
    eye = row == col

    def to_col(r):
        return jnp.sum(jnp.where(eye, jnp.broadcast_to(r, sq), 0.0), axis=2, keepdims=True)

    bc_r = jnp.sum(jnp.where(row <= col, jnp.broadcast_to(to_col(fgr), sq), 0.0), axis=1, keepdims=True)
    bc_c = to_col(bc_r)
    logw = jnp.where(col <= row, bc_c - bc_r + igr, -jnp.inf)
    inter = bc_c + m
    m_t = jnp.maximum(inter, jnp.max(logw, axis=2, keepdims=True))
    w_intra = jnp.exp(logw - m_t)
    w_inter = jnp.exp(inter - m_t)
    sc = _hdot(q, k, 2, 2) * w_intra
    num = w_inter * _hdot(q, c_mat, 2, 1) + _hdot(sc, v, 2, 1)
    qn = jnp.sum(q.astype(bf16).astype(f32) * n_vec.astype(bf16).astype(f32), axis=2, keepdims=True)
    den = w_inter * qn + jnp.sum(sc, axis=2, keepdims=True)
    h = num / jnp.maximum(jnp.abs(den), jnp.exp(-m_t))
    lane = lax.broadcasted_iota(jnp.int32, (nh, 1, ln), 2)
    b_last = jnp.sum(jnp.where(lane == ln - 1, bc_r, 0.0), axis=2, keepdims=True)
    logu = b_last - bc_r + igr
    m_new = jnp.maximum(b_last + m, jnp.max(logu, axis=2, keepdims=True))
    decay = jnp.exp(b_last + m - m_new)
    u_c = to_col(jnp.exp(logu - m_new))
    c_new = decay * c_mat + _hdot(u_c * k, v, 1, 1)
    n_new = decay * n_vec + jnp.sum(u_c.astype(bf16).astype(f32) * k.astype(bf16).astype(f32), axis=1, keepdims=True)
    return h, c_new, n_new, m_new


ML_VHALF = ML_V // 2
assert O_V % ML_VHALF == 0 and ML_HEADS % 2 == 0


def _ml_specs(nc, rev):
    cc = (lambda c: nc - 1 - c) if rev else (lambda c: c)
    q = pl.BlockSpec((ML_HEADS, CHUNK, ML_DK), lambda c: (0, cc(c), 0))
    k = pl.BlockSpec((ML_HEADS, CHUNK, ML_DK), lambda c: (1, cc(c), 0))
    v_lo = pl.BlockSpec((CHUNK, ML_VHALF), lambda c: (cc(c), O_V // ML_VHALF))
    v_hi = pl.BlockSpec((CHUNK, ML_VHALF), lambda c: (cc(c), O_V // ML_VHALF + 1))
    hv = pl.BlockSpec((ML_HEADS, CHUNK, ML_DV), lambda c: (0, cc(c), 0))
    gate = pl.BlockSpec((ML_HEADS, 1, 1, CHUNK), lambda c: (0, cc(c), 0, 0))
    cm = pl.BlockSpec((ML_HEADS, 1, ML_DK, ML_DV), lambda c: (0, cc(c), 0, 0))
    nv = pl.BlockSpec((ML_HEADS, 1, 1, ML_DK), lambda c: (0, cc(c), 0, 0))
    ms = pl.BlockSpec((ML_HEADS, 1, 1, 1), lambda c: (0, cc(c), 0, 0))
    return q, k, v_lo, v_hi, hv, gate, cm, nv, ms


_ML_STATE = [pltpu.VMEM((ML_HEADS, ML_DK, ML_DV), f32), pltpu.VMEM((ML_HEADS, 1, ML_DK), f32), pltpu.VMEM((ML_HEADS, 1, 1), f32)]


def _ml_zero_state(c_s, n_s, m_s):
    @pl.when(pl.program_id(0) == 0)
    def _():
        c_s[...] = jnp.zeros_like(c_s)
        n_s[...] = jnp.zeros_like(n_s)
        m_s[...] = jnp.zeros_like(m_s)


def _ml_heads_of(v_lo_ref, v_hi_ref):
    half = ML_HEADS // 2
    return jnp.stack([r[:, j * ML_DV:(j + 1) * ML_DV] for r in (v_lo_ref, v_hi_ref) for j in range(half)])


def mlstm_fwd(qk_act, z_main, ig, fg):
    t = qk_act.shape[1]
    nc = t // CHUNK

    def body(q_ref, k_ref, vl_ref, vh_ref, ig_ref, fg_ref, h_ref, c_out, n_out, m_out, c_s, n_s, m_s):
        _ml_zero_state(c_s, n_s, m_s)
        c0, n0, m0 = c_s[...], n_s[...], m_s[...]
        c_out[:, 0] = c0
        n_out[:, 0] = n0
        m_out[:, 0] = m0
        h, c2, n2, m2 = _mlstm_step(q_ref[...], k_ref[...], _ml_heads_of(vl_ref, vh_ref), ig_ref[:, 0], fg_ref[:, 0], c0, n0, m0)
        h_ref[...] = h
        c_s[...] = c2
        n_s[...] = n2
        m_s[...] = m2

    q, k, v_lo, v_hi, hv, gate, cm, nv, ms = _ml_specs(nc, False)
    return _call(body, "mlstm_fwd", (nc,), [q, k, v_lo, v_hi, gate, gate], [hv, cm, nv, ms],
                 [_sds((ML_HEADS, t, ML_DV), f32), _sds((ML_HEADS, nc, ML_DK, ML_DV), f32), _sds((ML_HEADS, nc, 1, ML_DK), f32),
                  _sds((ML_HEADS, nc, 1, 1), f32)], _ML_STATE)(qk_act, qk_act, z_main, z_main, ig, fg)


def mlstm_bwd(qk_act, z_main, ig, fg, c_all, n_all, m_all, dh, exchange=None):
    t = qk_act.shape[1]
    nc = t // CHUNK

    def body(q_ref, k_ref, vl_ref, vh_ref, ig_ref, fg_ref, c_ref, n_ref, m_ref, dh_ref, dq_ref, dk_ref, dv_ref, dig_ref, dfg_ref,
             dc_s, dn_s, dm_s):
        _ml_zero_state(dc_s, dn_s, dm_s)
        _, vjp = jax.vjp(_mlstm_step, q_ref[...], k_ref[...], _ml_heads_of(vl_ref, vh_ref), ig_ref[:, 0], fg_ref[:, 0],
                         c_ref[:, 0], n_ref[:, 0], m_ref[:, 0])
        dq, dk, dv, dig, dfg, dc, dn, dm = vjp((dh_ref[...], dc_s[...], dn_s[...], dm_s[...]))
        dq_ref[...] = dq
        dk_ref[...] = dk
        for j in range(ML_HEADS):
            dv_ref[:, j * ML_DV:(j + 1) * ML_DV] = dv[j].astype(bf16)
        dig_ref[:, 0] = dig
        dfg_ref[:, 0] = dfg
        dc_s[...] = dc
        dn_s[...] = dn
        dm_s[...] = dm

    q, k, v_lo, v_hi, hv, gate, cm, nv, ms = _ml_specs(nc, True)
    gshape = _sds((ML_HEADS, nc, 1, CHUNK), f32)
    return _call(body, "mlstm_bwd", (nc,), [q, k, v_lo, v_hi, gate, gate, cm, nv, ms, hv],
                 [q, q, pl.BlockSpec((CHUNK, ML_V), lambda c: (nc - 1 - c, 0)), gate, gate],
                 [_sds((ML_HEADS, t, ML_DK), f32), _sds((ML_HEADS, t, ML_DK), f32), _sds((t, ML_V), bf16), gshape, gshape],
                 _ML_STATE, exchange=exchange)(qk_act, qk_act, z_main, z_main, ig, fg, c_all, n_all, m_all, dh)


def _ml_out(h, zo, g):
    return _rms(h, g) * jax.nn.sigmoid(zo)


def mlstm_out(h, z_main, g_hnorm):
    t = h.shape[1]
    tm = _tile(t, (512, 256, 128, 64))
    zo = O_O // ML_DV

    def body(h_ref, z_ref, g_ref, y_ref):
        y_ref[...] = _ml_out(h_ref[0], z_ref[...], g_ref[0]).astype(bf16)

    return _call(body, "mlstm_out", (t // tm, ML_HEADS),
                 [pl.BlockSpec((1, tm, ML_DV), lambda i, hd: (hd, i, 0)), pl.BlockSpec((tm, ML_DV), lambda i, hd: (i, zo + hd)),
                  pl.BlockSpec((1, 1, ML_DV), lambda i, hd: (hd, 0, 0))],
                 pl.BlockSpec((tm, ML_DV), lambda i, hd: (i, hd)), _sds((t, ML_V), bf16))(h, z_main, g_hnorm)


def mlstm_out_bwd(h, z_main, g_hnorm, dy):
    t = h.shape[1]
    tm = _tile(t, (512, 256, 128, 64))
    zo = O_O // ML_DV

    def body(h_ref, z_ref, g_ref, dy_ref, dh_ref, dzo_ref, dg_ref):
        _, vjp = jax.vjp(_ml_out, h_ref[0], z_ref[...], g_ref[0])
        dh, dz, dg = vjp(dy_ref[...])
        dh_ref[0] = dh
        dzo_ref[...] = dz.astype(bf16)

        @pl.when(pl.program_id(1) == 0)
        def _():
            dg_ref[...] = jnp.zeros_like(dg_ref)

        dg_ref[0, 0:1, :] += dg

    head = pl.BlockSpec((1, tm, ML_DV), lambda hd, i: (hd, i, 0))
    blk = pl.BlockSpec((tm, ML_DV), lambda hd, i: (i, hd))
    return _call(body, "mlstm_out_bwd", (ML_HEADS, t // tm),
                 [head, pl.BlockSpec((tm, ML_DV), lambda hd, i: (i, zo + hd)), pl.BlockSpec((1, 1, ML_DV), lambda hd, i: (hd, 0, 0)), blk],
                 [head, blk, pl.BlockSpec((1, 8, ML_DV), lambda hd, i: (hd, 0, 0))],
                 [_sds((ML_HEADS, t, ML_DV), f32), _sds((t, ML_V), bf16), _sds((ML_HEADS, 8, ML_DV), f32)])(h, z_main, g_hnorm, dy)


def _merge(ga, gb, ya, yb):
    return jax.nn.sigmoid(ga) * ya + jax.nn.sigmoid(gb) * yb


def _merge_specs(t, d):
    tm = _tile(t, (512, 256, 128, 64))
    bw = _tile(d, (512, 256, 128))
    assert O_GA % bw == 0 and (O_GA + d) % bw == 0
    blk = pl.BlockSpec((tm, bw), lambda i, j: (i, j))
    ga = pl.BlockSpec((tm, bw), lambda i, j: (i, O_GA // bw + j))
    gb = pl.BlockSpec((tm, bw), lambda i, j: (i, (O_GA + d) // bw + j))
    return tm, bw, blk, ga, gb


def merge_fwd(z_main, ya, yb):
    t, d = ya.shape
    tm, bw, blk, ga, gb = _merge_specs(t, d)

    def body(ga_ref, gb_ref, ya_ref, yb_ref, o_ref):
        o_ref[...] = _merge(ga_ref[...], gb_ref[...], ya_ref[...], yb_ref[...]).astype(bf16)

    return _call(body, "merge_fwd", (t // tm, d // bw), [ga, gb, blk, blk], blk, _sds((t, d), bf16))(z_main, z_main, ya, yb)


def merge_bwd(z_main, ya, yb, dmerged):
    t, d = ya.shape
    tm, bw, blk, ga, gb = _merge_specs(t, d)

    def body(ga_ref, gb_ref, ya_ref, yb_ref, dm_ref, dga_ref, dgb_ref, dya_ref, dyb_ref):
        _, vjp = jax.vjp(_merge, ga_ref[...], gb_ref[...], ya_ref[...], yb_ref[...])
        dga, dgb, dya, dyb = vjp(dm_ref[...])
        dga_ref[...] = dga.astype(bf16)
        dgb_ref[...] = dgb.astype(bf16)
        dya_ref[...] = dya.astype(bf16)
        dyb_ref[...] = dyb.astype(bf16)

    return _call(body, "merge_bwd", (t // tm, d // bw), [ga, gb, blk, blk, blk], [blk] * 4, [_sds((t, d), bf16)] * 4)(
        z_main, z_main, ya, yb, dmerged)


def _cross(cq, ck, cv, gq, gk):
    outs = []
    for hd in range(CR_HEADS):
        sl = slice(hd * CR_HD, (hd + 1) * CR_HD)
        q = _rms(cq[:, sl], gq)
        k = _rms(ck[:, sl], gk)
        s = _bdot(q, k, 1, 1) * (CR_HD ** -0.5)
        p = jax.nn.softmax(s, axis=-1)
        outs.append(_bdot(p, cv[:, sl], 1, 0))
    return jnp.concatenate(outs, axis=1)


def cross_fwd(cq, ck, cv, gq, gk):
    t, w = cq.shape
    nm = ck.shape[0]
    tm = _tile(t, (512, 256, 128, 64))

    def body(q_ref, k_ref, v_ref, gq_ref, gk_ref, o_ref):
        o_ref[...] = _cross(q_ref[...], k_ref[...], v_ref[...], gq_ref[...], gk_ref[...]).astype(bf16)

    row = pl.BlockSpec((tm, w), lambda i: (i, 0))
    full = pl.BlockSpec((nm, w), lambda i: (0, 0))
    gain = pl.BlockSpec((1, CR_HD), lambda i: (0, 0))
    return _call(body, "cross_fwd", (t // tm,), [row, full, full, gain, gain], row, _sds((t, w), bf16))(cq, ck, cv, gq, gk)


def cross_bwd(cq, ck, cv, gq, gk, do):
    t, w = cq.shape
    nm = ck.shape[0]
    tm = _tile(t, (512, 256, 128, 64))

    def body(q_ref, k_ref, v_ref, gq_ref, gk_ref, do_ref, dq_ref, dk_ref, dv_ref, dgq_ref, dgk_ref):
        first = pl.program_id(0) == 0
        _, vjp = jax.vjp(_cross, q_ref[...], k_ref[...], v_ref[...], gq_ref[...], gk_ref[...])
        dq, dk, dv, dgq, dgk = vjp(do_ref[...])
        dq_ref[...] = dq.astype(bf16)

        @pl.when(first)
        def _():
            dk_ref[...] = jnp.zeros_like(dk_ref)
            dv_ref[...] = jnp.zeros_like(dv_ref)

        dk_ref[...] += dk
        dv_ref[...] += dv
        _acc_row(dgq_ref, dgq, first)
        _acc_row(dgk_ref, dgk, first)

    row = pl.BlockSpec((tm, w), lambda i: (i, 0))
    full = pl.BlockSpec((nm, w), lambda i: (0, 0))
    gain = pl.BlockSpec((1, CR_HD), lambda i: (0, 0))
    acc = pl.BlockSpec((8, CR_HD), lambda i: (0, 0))
    return _call(body, "cross_bwd", (t // tm,), [row, full, full, gain, gain, row], [row, full, full, acc, acc],
                 [_sds((t, w), bf16), _sds((nm, w), f32), _sds((nm, w), f32), _sds((8, CR_HD), f32), _sds((8, CR_HD), f32)])(
        cq, ck, cv, gq, gk, do)


def loss_head(x2, fo, target):
    t, d = x2.shape
    tm = _tile(t, (256, 128, 64, 32, 16, 8))

    def body(a_ref, b_ref, t_ref, dx_ref, dxb_ref, l_ref):
        err = a_ref[...] + b_ref[...] - t_ref[...]
        dx = err / d
        dx_ref[...] = dx
        dxb_ref[...] = dx.astype(bf16)
        part = 0.5 * jnp.sum(jnp.mean(err * err, axis=1, keepdims=True), axis=0, keepdims=True)
        _acc_row(l_ref, jnp.broadcast_to(part, (1, 128)), pl.program_id(0) == 0)

    row = pl.BlockSpec((tm, d), lambda i: (i, 0))
    return _call(body, "loss_head", (t // tm,), [row, row, row], [row, row, pl.BlockSpec((8, 128), lambda i: (0, 0))],
                 [_sds((t, d), f32), _sds((t, d), bf16), _sds((8, 128), f32)])(x2, fo, target)


def _place():
    x, y, c = lax.axis_index("x"), lax.axis_index("y"), lax.axis_index("c")
    peers = []
    for k in range(1, N_DEV):
        px = 1 - x if k & 4 else x
        py = 1 - y if k & 2 else y
        pc = 1 - c if k & 1 else c
        peers.append(((px, py, pc), 4 * px + 2 * py + pc))
    return 4 * x + 2 * y + c, peers


N_REL = N_DEV - 1


def _exchange_ops(ins, outs, sems, scatter):
    n = len(ins)
    send_sems, recv_sems, local_sems = sems

    def copies(with_arrivals):
        me, peers = _place()

        def src(a, idx):
            return ins[a].at[idx] if scatter else ins[a]

        def remote(a, k, src_idx, dst_idx):
            return pltpu.make_async_remote_copy(
                src_ref=src(a, src_idx), dst_ref=outs[a].at[dst_idx], send_sem=send_sems.at[a * N_REL + k],
                recv_sem=recv_sems.at[a * N_REL + k], device_id=peers[k][0], device_id_type=MESH)

        local = [pltpu.make_async_copy(src(a, me), outs[a].at[me], local_sems.at[a]) for a in range(n)]
        sends = [remote(a, k, peers[k][1], me) for a in range(n) for k in range(N_REL)]
        arrivals = [remote(a, k, me, peers[k][1]) for a in range(n) for k in range(N_REL)] if with_arrivals else []
        return local, sends, arrivals

    def start():
        local, sends, _ = copies(False)
        for cp in local + sends:
            cp.start()

    def wait():
        local, sends, arrivals = copies(True)
        for cp in arrivals:
            cp.wait_recv()
        for cp in sends:
            cp.wait_send()
        for cp in local:
            cp.wait()

    return start, wait


def _exchange_shapes(arrs, scatter):
    return [_sds(a.shape if scatter else (N_DEV,) + a.shape, a.dtype) for a in arrs]


def _exchange_sems(n):
    return [pltpu.SemaphoreType.DMA((n * N_REL,)), pltpu.SemaphoreType.DMA((n * N_REL,)), pltpu.SemaphoreType.DMA((n,))]


def _exchange(arrs, name, scatter):
    n = len(arrs)

    def body(*refs):
        start, wait = _exchange_ops(refs[:n], refs[n:2 * n], refs[2 * n:], scatter)
        start()
        wait()

    any_spec = pl.BlockSpec(memory_space=pl.ANY)
    return pl.pallas_call(body, name=name, in_specs=[any_spec] * n, out_specs=[any_spec] * n,
                          out_shape=_exchange_shapes(arrs, scatter), scratch_shapes=_exchange_sems(n))(*arrs)


def cast_bf16(w, name):
    r, c = w.shape
    tr = _tile(r, (256, 128, 64, 32, 16))

    def body(w_ref, o_ref):
        o_ref[...] = w_ref[...].astype(bf16)

    blk = pl.BlockSpec((tr, c), lambda i: (i, 0))
    return _call(body, name, (r // tr,), [blk], blk, _sds((r, c), bf16))(w)


def _adamw(w, g, m, v):
    m = ADAM_B1 * m + (1.0 - ADAM_B1) * g
    v = ADAM_B2 * v + (1.0 - ADAM_B2) * jnp.square(g)
    m_hat = m / (1.0 - ADAM_B1 ** ADAM_STEP)
    v_hat = v / (1.0 - ADAM_B2 ** ADAM_STEP)
    delta = -ADAM_LR * (m_hat / (jnp.sqrt(v_hat) + ADAM_EPS) + ADAM_WD * w)
    return delta, m, v


def adam_sum(parts, w, m, v, name):
    _, r, c = parts.shape
    budget = 4 * 1024 * 1024
    tr = r
    for cand in (1024, 512, 256, 128, 64, 32, 16):
        if r % cand == 0 and N_DEV * cand * c * 4 <= budget:
            tr = cand
            break

    def body(p_ref, w_ref, m_ref, v_ref, g_ref, d_ref, m2_ref, v2_ref):
        g = p_ref[0].astype(f32)
        for k in range(1, N_DEV):
            g = g + p_ref[k].astype(f32)
        d, m2, v2 = _adamw(w_ref[...], g, m_ref[...], v_ref[...])
        g_ref[...] = g
        d_ref[...] = d
        m2_ref[...] = m2
        v2_ref[...] = v2

    blk = pl.BlockSpec((tr, c), lambda i: (i, 0))
    return _call(body, name, (r // tr,), [pl.BlockSpec((N_DEV, tr, c), lambda i: (0, i, 0)), blk, blk, blk], [blk] * 4,
                 [_sds((r, c), f32)] * 4)(parts, w, m, v)


def sum_parts(parts, name):
    _, r, c = parts.shape

    def body(p_ref, o_ref):
        g = p_ref[0]
        for k in range(1, N_DEV):
            g = g + p_ref[k]
        o_ref[...] = g

    return pl.pallas_call(body, name=name, out_shape=_sds((r, c), f32))(parts)


def adam_flat(w, g, m, v, name):
    def body(w_ref, g_ref, m_ref, v_ref, d_ref, m2_ref, v2_ref):
        d, m2, v2 = _adamw(w_ref[...], g_ref[...], m_ref[...], v_ref[...])
        d_ref[...] = d
        m2_ref[...] = m2
        v2_ref[...] = v2

    return pl.pallas_call(body, name=name, out_shape=[_sds(w.shape, f32)] * 3)(w, g, m, v)


def _pack(vecs, multiple):
    flat = jnp.concatenate([v.reshape(-1) for v in vecs])
    n = flat.shape[0]
    total = -(-n // multiple) * multiple
    return jnp.pad(flat, (0, total - n))


def _unpack(flat, shapes):
    out, pos = [], 0
    for s in shapes:
        n = 1
        for d in s:
            n *= d
        out.append(flat[pos:pos + n].reshape(s))
        pos += n
    return out


def _pad_lanes(v, width=TAIL):
    return jnp.pad(v, ((0, 0), (0, width - v.shape[1])))


def kernel(x, mem, positions, g_mix, w_in, g_qa, w_qb, g_kva, w_kvb, g_qn_nope, g_qn_pe, g_kn_nope, g_kn_pe, conv_qk, b_if, g_hnorm, p_a, p_b, w_out, g_cross, g_mem, wq_c, wk_c, wv_c, g_cq, g_ck, wo_c, g_ffn, w_up, conv_ffn, b_conv_ffn, w_down, loss_target, m_g_mix, m_w_in, m_g_qa, m_w_qb, m_g_kva, m_w_kvb, m_g_qn_nope, m_g_qn_pe, m_g_kn_nope, m_g_kn_pe, m_conv_qk, m_b_if, m_g_hnorm, m_p_a, m_p_b, m_w_out, m_g_cross, m_g_mem, m_wq_c, m_wk_c, m_wv_c, m_g_cq, m_g_ck, m_wo_c, m_g_ffn, m_w_up, m_conv_ffn, m_b_conv_ffn, m_w_down, v_g_mix, v_w_in, v_g_qa, v_w_qb, v_g_kva, v_w_kvb, v_g_qn_nope, v_g_qn_pe, v_g_kn_nope, v_g_kn_pe, v_conv_qk, v_b_if, v_g_hnorm, v_p_a, v_p_b, v_w_out, v_g_cross, v_g_mem, v_wq_c, v_wk_c, v_wv_c, v_g_cq, v_g_ck, v_wo_c, v_g_ffn, v_w_up, v_conv_ffn, v_b_conv_ffn, v_w_down):
    args = dict(locals())
    names = ['g_mix', 'w_in', 'g_qa', 'w_qb', 'g_kva', 'w_kvb', 'g_qn_nope', 'g_qn_pe', 'g_kn_nope', 'g_kn_pe', 'conv_qk', 'b_if',
             'g_hnorm', 'p_a', 'p_b', 'w_out', 'g_cross', 'g_mem', 'wq_c', 'wk_c', 'wv_c', 'g_cq', 'g_ck', 'wo_c', 'g_ffn', 'w_up',
             'conv_ffn', 'b_conv_ffn', 'w_down']
    big = ['w_in', 'w_qb', 'w_kvb', 'p_a', 'p_b', 'w_out', 'wq_c', 'wk_c', 'wv_c', 'wo_c', 'w_up', 'w_down']
    sharded_small = ['conv_qk', 'g_hnorm', 'conv_ffn']
    replicated = [n for n in names if n not in big and n not in sharded_small]

    t, d = x.shape[1], x.shape[2]
    x2d, tgt = x[0], loss_target[0]
    mem2d = mem[0]
    me = 4 * lax.axis_index("x") + 2 * lax.axis_index("y") + lax.axis_index("c")
    nc = t // CHUNK
    f2 = b_conv_ffn.shape[1]
    wmain = O_GA + 2 * d

    first = ['w_in', 'w_qb', 'w_kvb']
    behind_in = ['p_a', 'p_b', 'w_out', 'wq_c', 'wk_c', 'wv_c', 'wo_c']
    shards = {n: cast_bf16(args[n][0], "cast_" + n) for n in big}
    small_local = _pack([args[n] for n in sharded_small], 128).reshape(1, -1)
    gathered = _exchange([shards[n] for n in first] + [small_local], "comm_gather_first", scatter=False)
    gw = dict(zip(first, gathered[:-1]))
    small_all = gathered[-1]
    small_shapes = [args[n].shape for n in sharded_small]
    per_dev = [_unpack(small_all[k, 0], small_shapes) for k in range(N_DEV)]
    conv_qk_f = jnp.concatenate([p[0] for p in per_dev], axis=-1)[0]
    g_hnorm_f = jnp.concatenate([p[1] for p in per_dev], axis=-1)[0]
    conv_ffn_f = jnp.concatenate([p[2] for p in per_dev], axis=-1)[0]

    w_in_f = gw['w_in'].transpose(1, 0, 2).reshape(d, -1)
    c_kpe, c_q, c_i, c_o = O_Q, O_Q + ROPE, O_Q + ROPE + 2 * ML_QK + ML_V, O_Q + ROPE + 2 * ML_QK + ML_V + 2 * ML_HEADS
    w_main = jnp.concatenate([w_in_f[:, :c_kpe], w_in_f[:, c_q:c_i], w_in_f[:, c_o:]], axis=1)[None]
    w_tail = jnp.concatenate([w_in_f[:, c_kpe:c_q], w_in_f[:, c_i:c_o],
                              jnp.zeros((d, TAIL - ROPE - 2 * ML_HEADS), bf16)], axis=1)[None]
    assert w_main.shape[2] == wmain
    qb = gw['w_qb'].transpose(1, 0, 2).reshape(Q_LORA, MLA_HEADS, NOPE + ROPE)
    w_qb_p = jnp.concatenate([qb, jnp.zeros((Q_LORA, MLA_HEADS, HEAD_PAD - NOPE - ROPE), bf16)], axis=2).reshape(1, Q_LORA, -1)
    w_kvb3 = gw['w_kvb']

    inv_freq = ROPE_BASE ** (-jnp.arange(0, ROPE, 2, dtype=f32) / ROPE)
    inv_tile = _pad_lanes(jnp.concatenate([inv_freq, inv_freq])[None])
    cos, sin = rope_tables(positions.reshape(t, 1), inv_tile)
    gqp, gkp = _pad_lanes(g_qn_pe), _pad_lanes(g_kn_pe)
    b_tile = jnp.pad(b_if, ((0, 0), (T_I, TAIL - T_I - 2 * ML_HEADS)))

    u0 = rms_fwd(x2d, g_mix, "rms_mix")
    z_main, got = mm_nn(u0, w_main, f32, "mm_in_main", exchange=([shards[n] for n in behind_in], False))
    gw.update(zip(behind_in, got))
    p_a3, p_b3, w_out3 = (gw[n].reshape(1, -1, d) for n in ('p_a', 'p_b', 'w_out'))
    wq_c3, wk_c3, wv_c3 = (gw[n].reshape(1, d, -1) for n in ('wq_c', 'wk_c', 'wv_c'))
    wo_c3 = gw['wo_c']
    z_tail = mm_nn(u0, w_tail, f32, "mm_in_tail")
    qa_n, kv_n = lat_norm(z_main, g_qa, g_kva)
    q_raw = mm_nn(qa_n, w_qb_p, f32, "mm_qb")
    kv_raw = mm_nn(kv_n, w_kvb3, f32, "mm_kvb")
    qh, kh, vh = mla_prep(q_raw, kv_raw, z_tail, cos, sin, g_qn_nope, gqp, g_kn_nope, gkp)
    (o_a, o_ab, lse), (w_up3,) = mla_fwd(qh, kh, vh, exchange=([shards['w_up']], False))

    qk_act = qk_conv(z_main, conv_qk_f)
    gates = gate_act(z_tail, b_tile)

    def to_rows(cols):
        return cols.T.reshape(ML_HEADS, nc, 1, CHUNK)

    ig, fg = to_rows(gates[:, T_I:T_F]), to_rows(gates[:, T_F:T_F + ML_HEADS])
    h_ml, c_all, n_all, m_all = mlstm_fwd(qk_act, z_main, ig, fg)
    g_hn3 = g_hnorm_f.reshape(ML_HEADS, 1, ML_DV)
    y_b = mlstm_out(h_ml, z_main, g_hn3)

    ya = mm_nn(o_ab, p_a3, f32, "mm_pa")
    yb = mm_nn(y_b, p_b3, f32, "mm_pb")
    merged = merge_fwd(z_main, ya, yb)
    mo = mm_nn(merged, w_out3, f32, "mm_out")
    x1, uc = resid_rms(x2d, mo, g_cross, "resid_cross")
    mem_n = rms_fwd(mem2d, g_mem, "rms_mem")
    cq = mm_nn(uc, wq_c3, f32, "mm_cq")
    ck = mm_nn(mem_n, wk_c3, f32, "mm_ck")
    cv = mm_nn(mem_n, wv_c3, f32, "mm_cv")
    o_c = cross_fwd(cq, ck, cv, g_cq, g_ck)
    co = mm_nn(o_c, wo_c3, f32, "mm_oc")
    x2, u3 = resid_rms(x1, co, g_ffn, "resid_ffn")
    hup, (w_down_g,) = mm_nn(u3, w_up3, f32, "mm_up", exchange=([shards['w_down']], False))
    w_down3 = w_down_g.reshape(1, -1, d)
    gl = glu_fwd(hup, conv_ffn_f, b_conv_ffn)
    fo = mm_nn(gl, w_down3, f32, "mm_down")
    dx3, dx3_b, loss_acc = loss_head(x2, fo, tgt)

    grads, parts = {}, {}
    grads['w_down'] = mm_tn(gl, dx3_b, 1, "mm_d_wdown").reshape(N_DEV, -1, d)
    dgl = mm_nt(dx3_b, w_down3, f32, "mm_d_gl")
    (dh1, dh2, dcw1, dcw2, db1, db2), (parts['w_down'],) = glu_bwd(hup, conv_ffn_f, b_conv_ffn, dgl,
                                                                    exchange=([grads['w_down']], True))
    dhup, dconv_ffn, db_ffn = (jnp.concatenate(pair, axis=1) for pair in ((dh1, dh2), (dcw1, dcw2), (db1, db2)))
    grads['w_up'] = mm_tn(u3, dhup, N_DEV, "mm_d_wup")
    du3 = mm_nt(dhup, w_up3, f32, "mm_d_u3")
    dx2, dx2_b, dg_ffn = rms_bwd(x2, g_ffn, [du3], dx3, "rms_bwd_ffn", want_b16=True)
    grads['wo_c'] = mm_tn(o_c, dx2_b, N_DEV, "mm_d_woc")
    do_c = mm_nt(dx2_b, wo_c3, f32, "mm_d_oc")
    dcq, dck, dcv, dg_cq, dg_ck = cross_bwd(cq, ck, cv, g_cq, g_ck, do_c)
    grads['wq_c'] = mm_tn(uc, dcq, 1, "mm_d_wqc").reshape(N_DEV, -1, dcq.shape[1])
    grads['wk_c'] = mm_tn(mem_n, dck, 1, "mm_d_wkc").reshape(N_DEV, -1, dck.shape[1])
    grads['wv_c'] = mm_tn(mem_n, dcv, 1, "mm_d_wvc").reshape(N_DEV, -1, dcv.shape[1])
    duc = mm_nt(dcq, wq_c3, f32, "mm_d_uc")
    dmem_k = mm_nt(dck, wk_c3, f32, "mm_d_memk")
    dmem_v = mm_nt(dcv, wv_c3, f32, "mm_d_memv")
    dg_mem, = rms_bwd(mem2d, g_mem, [dmem_k, dmem_v], None, "rms_bwd_mem", want_dx=False)
    dx1, dx1_b, dg_cross = rms_bwd(x1, g_cross, [duc], dx2, "rms_bwd_cross", want_b16=True)
    grads['w_out'] = mm_tn(merged, dx1_b, 1, "mm_d_wout").reshape(N_DEV, -1, d)
    dmerged = mm_nt(dx1_b, w_out3, f32, "mm_d_merged")
    dga, dgb, dya, dyb = merge_bwd(z_main, ya, yb, dmerged)
    grads['p_a'] = mm_tn(o_ab, dya, 1, "mm_d_pa").reshape(N_DEV, -1, d)
    grads['p_b'] = mm_tn(y_b, dyb, 1, "mm_d_pb").reshape(N_DEV, -1, d)
    do_a = mm_nt(dya, p_a3, f32, "mm_d_oa")
    dy_b = mm_nt(dyb, p_b3, f32, "mm_d_yb")

    dh_ml, dzo, dg_hn = mlstm_out_bwd(h_ml, z_main, g_hn3, dy_b)
    (dq_act, dk_act, dzv, dig, dfg), got = mlstm_bwd(qk_act, z_main, ig, fg, c_all, n_all, m_all, dh_ml,
                                                     exchange=([grads[n] for n in behind_in], True))
    parts.update(zip(behind_in, got))
    dzqk, dconv_qk = qk_conv_bwd(z_main, conv_qk_f, dq_act, dk_act)

    delta = mla_delta(o_a, do_a)
    (dqh, dkh, dvh), (parts['w_up'],) = mla_bwd(qh, kh, vh, do_a, lse, delta.reshape(MLA_HEADS, 1, t),
                                                exchange=([grads['w_up']], True))
    dq_raw, dkv_raw, dzt_pe, dg_qn, dg_qp, dg_kn, dg_kp = mla_prep_bwd(
        q_raw, kv_raw, z_tail, cos, sin, g_qn_nope, gqp, g_kn_nope, gkp, dqh, dkh, dvh)
    d_wqb_p = mm_tn(qa_n, dq_raw, 1, "mm_d_wqb")[0].reshape(Q_LORA, MLA_HEADS, HEAD_PAD)[:, :, :NOPE + ROPE]
    grads['w_qb'] = d_wqb_p.reshape(Q_LORA, N_DEV, -1).transpose(1, 0, 2)
    grads['w_kvb'] = mm_tn(kv_n, dkv_raw, N_DEV, "mm_d_wkvb")
    dqa = mm_nt(dq_raw, w_qb_p, f32, "mm_d_qa")
    dkvn = mm_nt(dkv_raw, w_kvb3, f32, "mm_d_kvn")
    dz_lat, dg_qa, dg_kva = lat_norm_bwd(z_main, g_qa, g_kva, dqa, dkvn)

    def to_cols(rows):
        return rows.reshape(ML_HEADS, t).T

    dgate = jnp.pad(jnp.concatenate([to_cols(dig), to_cols(dfg)], axis=1), ((0, 0), (T_I, TAIL - T_I - 2 * ML_HEADS)))
    dz_tail, db_if = tail_bwd(z_tail, b_tile, dzt_pe, dgate)
    dz_main = jnp.concatenate([dz_lat, dzqk, dzv, dzo, dga, dgb], axis=1)
    d_wmain = mm_tn(u0, dz_main, 1, "mm_d_wmain")[0]
    d_wtail = mm_tn(u0, dz_tail, 1, "mm_d_wtail")[0]
    d_win = jnp.concatenate([d_wmain[:, :O_Q], d_wtail[:, :ROPE], d_wmain[:, O_Q:O_O], d_wtail[:, T_I:T_I + 2 * ML_HEADS],
                             d_wmain[:, O_O:]], axis=1)
    grads['w_in'] = d_win.reshape(d, N_DEV, -1).transpose(1, 0, 2)
    du0_a, got = mm_nt(dz_main, w_main, f32, "mm_d_u0_main", exchange=([grads[n] for n in first], True))
    parts.update(zip(first, got))
    du0_b = mm_nt(dz_tail, w_tail, f32, "mm_d_u0_tail")
    grad_x, dg_mix = rms_bwd(x2d, g_mix, [du0_a, du0_b], dx1, "rms_bwd_mix")

    out_g, out_d, out_m, out_v = {}, {}, {}, {}
    for n in big:
        shp = args[n].shape
        g, dl, m2, v2 = adam_sum(parts[n], args[n][0], args['m_' + n][0], args['v_' + n][0], "adam_" + n)
        out_g[n], out_d[n], out_m[n], out_v[n] = (a.reshape(shp) for a in (g, dl, m2, v2))

    small_full = {
        'g_mix': dg_mix[0], 'g_qa': dg_qa[0], 'g_kva': dg_kva[0], 'g_qn_nope': dg_qn[0], 'g_qn_pe': dg_qp[0, :ROPE],
        'g_kn_nope': dg_kn[0], 'g_kn_pe': dg_kp[0, :ROPE], 'conv_qk': dconv_qk, 'b_if': db_if[0, T_I:T_I + 2 * ML_HEADS],
        'g_hnorm': dg_hn[:, 0, :], 'g_cross': dg_cross[0], 'g_mem': dg_mem[0], 'g_cq': dg_cq[0], 'g_ck': dg_ck[0],
        'g_ffn': dg_ffn[0], 'conv_ffn': dconv_ffn, 'b_conv_ffn': db_ffn[0], 'loss': loss_acc[0, :1]}
    order = list(small_full)
    packed = _pack([small_full[n] for n in order], 8 * 128).reshape(1, -1)
    gathered_small, = _exchange([packed], "comm_gather_small", scatter=False)
    summed = sum_parts(gathered_small.reshape(N_DEV, -1, 128), "sum_small").reshape(-1)
    full_g = dict(zip(order, _unpack(summed, [small_full[n].shape for n in order])))
    loss = full_g['loss'][0]

    local_g = {}
    for n in replicated:
        local_g[n] = full_g[n].reshape(args[n].shape)
    for n in sharded_small:
        shp = args[n].shape
        full = full_g[n].reshape((1,) + full_g[n].shape)
        local_g[n] = lax.dynamic_slice_in_dim(full, me * shp[-1], shp[-1], axis=2)
    small = replicated + sharded_small
    dl_f, m_f, v_f = adam_flat(*[_pack([src[n] if pre == '' else args[pre + n] for n in small], 8 * 128).reshape(-1, 128)
                                 for pre, src in (('', args), ('', local_g), ('m_', None), ('v_', None))], "adam_small")
    shapes = [args[n].shape for n in small]
    for dst, flat in ((out_d, dl_f), (out_m, m_f), (out_v, v_f)):
        dst.update(zip(small, _unpack(flat.reshape(-1), shapes)))
    out_g.update(local_g)

    return (loss, grad_x[None], *[out_g[n] for n in names], *[out_d[n] for n in names],
            *[out_m[n] for n in names], *[out_v[n] for n in names])
```

```python
import functools

import jax
import jax.numpy as jnp
from jax import lax
from jax.experimental import pallas as pl
from jax.experimental.pallas import tpu as pltpu

f32 = jnp.float32
bf16 = jnp.bfloat16

N_DEV = 8
EPS = 1e-6
CHUNK = 64
CHUNK_SHIFT = 6
assert 1 << CHUNK_SHIFT == CHUNK
MLA_HEADS = 16
Q_LORA = 512
KV_LORA = 512
NOPE = 128
ROPE = 64
V_HEAD = 128
ROPE_BASE = 10000.0
HEAD_PAD = 256
ML_HEADS = 8
ML_DK = 128
ML_DV = 256
ML_CONV = 4
ML_QK = ML_HEADS * ML_DK
ML_V = ML_HEADS * ML_DV
CR_HEADS = 4
CR_HD = 128
FFN_CONV = 3
ADAM_LR = 0.001
ADAM_B1 = 0.9
ADAM_B2 = 0.999
ADAM_EPS = 1e-08
ADAM_WD = 0.01
ADAM_STEP = 10
O_QA, O_KV, O_Q, O_K = 0, Q_LORA, Q_LORA + KV_LORA, Q_LORA + KV_LORA + ML_QK
O_V = O_K + ML_QK
O_O = O_V + ML_V
O_GA = O_O + ML_V
TAIL = 128
T_I, T_F = ROPE, ROPE + ML_HEADS
VMEM_LIMIT_V7X = 48 * 1024 * 1024
MESH = pl.DeviceIdType.MESH


def _call(body, name, grid, in_specs, out_specs, out_shape, scratch=(), exchange=None):
    params = pltpu.CompilerParams(vmem_limit_bytes=VMEM_LIMIT_V7X)
    if exchange is None:
        return pl.pallas_call(body, name=name, grid=grid, in_specs=in_specs, out_specs=out_specs, out_shape=out_shape,
                              scratch_shapes=list(scratch), compiler_params=params)
    arrs, scatter = exchange
    single = not isinstance(out_specs, (list, tuple))
    o_specs = [out_specs] if single else list(out_specs)
    o_shape = [out_shape] if single else list(out_shape)
    n_in, n_out, n_sc, n = len(in_specs), len(o_specs), len(scratch), len(arrs)
    any_spec = pl.BlockSpec(memory_space=pl.ANY)

    def body_with_exchange(*refs):
        pos = [0]

        def take(k):
            pos[0] += k
            return refs[pos[0] - k:pos[0]]

        ins, ex_in, outs, ex_out, sc = take(n_in), take(n), take(n_out), take(n), take(n_sc)
        start, middle, wait = _exchange_ops(ex_in, ex_out, refs[pos[0]:], scatter)
        step, total = 0, 1
        for a in range(len(grid)):
            step = step * grid[a] + pl.program_id(a)
            total *= grid[a]
        pl.when(step == 0)(start)
        body(*ins, *outs, *sc)
        if middle is not None:
            pl.when(step == total // 2)(middle)
        pl.when(step == total - 1)(wait)

    call = pl.pallas_call(body_with_exchange, name="comm_" + name, grid=grid, in_specs=list(in_specs) + [any_spec] * n,
                          out_specs=o_specs + [any_spec] * n, out_shape=o_shape + _exchange_shapes(arrs, scatter),
                          scratch_shapes=list(scratch) + _exchange_sems(n), compiler_params=params)

    def run(*operands):
        res = call(*operands, *arrs)
        return (res[0] if single else list(res[:n_out])), list(res[n_out:])

    return run


def _tile(n, cands):
    for c in cands:
        if n % c == 0:
            return c
    return n


def _sds(shape, dtype):
    return jax.ShapeDtypeStruct(tuple(shape), dtype)


def _bdot(a, b, ca, cb):
    return lax.dot_general(a.astype(bf16), b.astype(bf16), (((ca,), (cb,)), ((), ())), preferred_element_type=f32)


_BIG = (1024, 512, 256, 128)


def _col_tile(nb):
    return nb if nb <= 1536 else _tile(nb, _BIG)


_DEEP = (2048, 1024, 512, 256, 128)


def _mm_call(name, grid, in_specs, out_spec, out_shape, tile, nk, ca, cb, exchange, operands):
    def dot(a_ref, w_ref):
        return _bdot(a_ref[...], w_ref[0] if len(w_ref.shape) == 3 else w_ref[...], ca, cb)

    def store(o_ref, val):
        if len(o_ref.shape) == 3:
            o_ref[0] = val.astype(o_ref.dtype)
        else:
            o_ref[...] = val.astype(o_ref.dtype)

    if nk == 1:
        def body(a_ref, w_ref, o_ref):
            store(o_ref, dot(a_ref, w_ref))

        scratch = []
    else:
        def body(a_ref, w_ref, o_ref, acc):
            kk = pl.program_id(2)

            @pl.when(kk == 0)
            def _():
                acc[...] = jnp.zeros_like(acc)

            acc[...] += dot(a_ref, w_ref)

            @pl.when(kk == nk - 1)
            def _():
                store(o_ref, acc[...])

        scratch = [pltpu.VMEM(tile, f32)]
    return _call(body, name, grid, in_specs, out_spec, out_shape, scratch, exchange=exchange)(*operands)


def mm_nn(a, w3, out_dtype, name, exchange=None):
    m, k = a.shape
    nblk, k2, nb = w3.shape
    assert k == k2
    tm, tk, tn = _tile(m, _BIG), _tile(k, _DEEP), _col_tile(nb)
    per, nk = nb // tn, k // tk
    return _mm_call(name, (m // tm, nblk * per, nk),
                    [pl.BlockSpec((tm, tk), lambda i, j, kk: (i, kk)),
                     pl.BlockSpec((1, tk, tn), lambda i, j, kk: (j // per, kk, j % per))],
                    pl.BlockSpec((tm, tn), lambda i, j, kk: (i, j)), _sds((m, nblk * nb), out_dtype),
                    (tm, tn), nk, 1, 0, exchange, (a, w3))


def mm_nt(a, w3, out_dtype, name, exchange=None):
    m, n = a.shape
    nblk, k, nb = w3.shape
    assert n == nblk * nb
    tm, tn = _tile(m, _BIG), _tile(k, _BIG)
    tc = nb if nb <= 1536 else _tile(nb, _DEEP)
    per = nb // tc
    nk = nblk * per
    return _mm_call(name, (m // tm, k // tn, nk),
                    [pl.BlockSpec((tm, tc), lambda i, j, kk: (i, kk)),
                     pl.BlockSpec((1, tn, tc), lambda i, j, kk: (kk // per, j, kk % per))],
                    pl.BlockSpec((tm, tn), lambda i, j, kk: (i, j)), _sds((m, k), out_dtype),
                    (tm, tn), nk, 1, 1, exchange, (a, w3))


def mm_tn(a, b, nblk, name):
    r, m = a.shape
    r2, n = b.shape
    assert r == r2 and n % nblk == 0
    nb = n // nblk
    tm, tk, tn = _tile(m, _BIG), _tile(r, _DEEP), _col_tile(nb)
    per, nk = nb // tn, r // tk
    return _mm_call(name, (m // tm, nblk * per, nk),
                    [pl.BlockSpec((tk, tm), lambda i, j, kk: (kk, i)),
                     pl.BlockSpec((tk, tn), lambda i, j, kk: (kk, j))],
                    pl.BlockSpec((1, tm, tn), lambda i, j, kk: (j // per, i, j % per)), _sds((nblk, m, nb), bf16),
                    (tm, tn), nk, 0, 0, None, (a, b))


def _rms(x, g):
    return x * lax.rsqrt(jnp.mean(x * x, axis=-1, keepdims=True) + EPS) * g


def _rms_pad(x, g, width):
    return x * lax.rsqrt(jnp.sum(x * x, axis=-1, keepdims=True) / width + EPS) * g


def _first(*ids):
    ok = ids[0] == 0
    for i in ids[1:]:
        ok = jnp.logical_and(ok, i == 0)
    return ok


def _acc_row(ref, val, first):
    @pl.when(first)
    def _():
        ref[...] = jnp.zeros_like(ref)

    ref[0:1, :] += val


def rms_fwd(x, g, name):
    r, w = x.shape
    tm = _tile(r, (256, 128, 64, 32, 16, 8))

    def body(x_ref, g_ref, o_ref):
        o_ref[...] = _rms(x_ref[...], g_ref[...]).astype(bf16)

    return _call(body, name, (r // tm,), [pl.BlockSpec((tm, w), lambda i: (i, 0)), pl.BlockSpec((1, w), lambda i: (0, 0))],
                 pl.BlockSpec((tm, w), lambda i: (i, 0)), _sds((r, w), bf16))(x, g)


def resid_rms(xa, xb, g, name):
    r, w = xa.shape
    tm = _tile(r, (256, 128, 64, 32, 16, 8))

    def body(a_ref, b_ref, g_ref, s_ref, u_ref):
        xs = a_ref[...] + b_ref[...]
        s_ref[...] = xs
        u_ref[...] = _rms(xs, g_ref[...]).astype(bf16)

    row = pl.BlockSpec((tm, w), lambda i: (i, 0))
    return _call(body, name, (r // tm,), [row, row, pl.BlockSpec((1, w), lambda i: (0, 0))], [row, row],
                 [_sds((r, w), f32), _sds((r, w), bf16)])(xa, xb, g)


def rms_bwd(x, g, dys, dres, name, want_dx=True, want_b16=False):
    r, w = x.shape
    tm = _tile(r, (256, 128, 64, 32, 16, 8))
    nd = len(dys)

    def body(*refs):
        x_ref, g_ref = refs[0], refs[1]
        dy = refs[2][...]
        for j in range(1, nd):
            dy = dy + refs[2 + j][...]
        pos = 2 + nd
        _, vjp = jax.vjp(_rms, x_ref[...], g_ref[...])
        dx, dg = vjp(dy)
        if dres is not None:
            dx = dx + refs[pos][...]
            pos += 1
        if want_dx:
            refs[pos][...] = dx
            pos += 1
        if want_b16:
            refs[pos][...] = dx.astype(bf16)
            pos += 1
        _acc_row(refs[pos], dg, pl.program_id(0) == 0)

    row = pl.BlockSpec((tm, w), lambda i: (i, 0))
    ins = [x, g] + list(dys) + ([dres] if dres is not None else [])
    in_specs = [row, pl.BlockSpec((1, w), lambda i: (0, 0))] + [row] * (nd + (dres is not None))
    out_specs = [row] * (want_dx + want_b16) + [pl.BlockSpec((8, w), lambda i: (0, 0))]
    out_shape = ([_sds((r, w), f32)] if want_dx else []) + ([_sds((r, w), bf16)] if want_b16 else []) + [_sds((8, w), f32)]
    return _call(body, name, (r // tm,), in_specs, out_specs, out_shape)(*ins)


def lat_norm(z_main, g_qa, g_kva):
    t = z_main.shape[0]
    tm = _tile(t, (512, 256, 128, 64))

    def body(z_ref, gq_ref, gk_ref, q_ref, k_ref):
        q_ref[...] = _rms(z_ref[:, :Q_LORA], gq_ref[...]).astype(bf16)
        k_ref[...] = _rms(z_ref[:, Q_LORA:], gk_ref[...]).astype(bf16)

    return _call(body, "lat_norm", (t // tm,),
                 [pl.BlockSpec((tm, Q_LORA + KV_LORA), lambda i: (i, 0)), pl.BlockSpec((1, Q_LORA), lambda i: (0, 0)),
                  pl.BlockSpec((1, KV_LORA), lambda i: (0, 0))],
                 [pl.BlockSpec((tm, Q_LORA), lambda i: (i, 0)), pl.BlockSpec((tm, KV_LORA), lambda i: (i, 0))],
                 [_sds((t, Q_LORA), bf16), _sds((t, KV_LORA), bf16)])(z_main, g_qa, g_kva)


def lat_norm_bwd(z_main, g_qa, g_kva, dqa, dkv):
    t = z_main.shape[0]
    tm = _tile(t, (512, 256, 128, 64))

    def body(z_ref, gq_ref, gk_ref, dq_ref, dk_ref, dz_ref, dgq_ref, dgk_ref):
        first = pl.program_id(0) == 0
        _, vq = jax.vjp(_rms, z_ref[:, :Q_LORA], gq_ref[...])
        dx, dg = vq(dq_ref[...])
        dz_ref[:, :Q_LORA] = dx.astype(bf16)
        _acc_row(dgq_ref, dg, first)
        _, vk = jax.vjp(_rms, z_ref[:, Q_LORA:], gk_ref[...])
        dx, dg = vk(dk_ref[...])
        dz_ref[:, Q_LORA:] = dx.astype(bf16)
        _acc_row(dgk_ref, dg, first)

    return _call(body, "lat_norm_bwd", (t // tm,),
                 [pl.BlockSpec((tm, Q_LORA + KV_LORA), lambda i: (i, 0)), pl.BlockSpec((1, Q_LORA), lambda i: (0, 0)),
                  pl.BlockSpec((1, KV_LORA), lambda i: (0, 0)), pl.BlockSpec((tm, Q_LORA), lambda i: (i, 0)),
                  pl.BlockSpec((tm, KV_LORA), lambda i: (i, 0))],
                 [pl.BlockSpec((tm, Q_LORA + KV_LORA), lambda i: (i, 0)), pl.BlockSpec((8, Q_LORA), lambda i: (0, 0)),
                  pl.BlockSpec((8, KV_LORA), lambda i: (0, 0))],
                 [_sds((t, Q_LORA + KV_LORA), bf16), _sds((8, Q_LORA), f32), _sds((8, KV_LORA), f32)])(z_main, g_qa, g_kva, dqa, dkv)


def rope_tables(pos_col, inv_freq):
    t = pos_col.shape[0]
    tm = _tile(t, (512, 256, 128, 64))

    def body(p_ref, f_ref, c_ref, s_ref):
        ang = p_ref[...].astype(f32) * f_ref[...]
        lane = lax.broadcasted_iota(jnp.int32, ang.shape, 1)
        c_ref[...] = jnp.where(lane < ROPE, jnp.cos(ang), 0.0)
        sn = jnp.sin(ang)
        s_ref[...] = jnp.where(lane < ROPE // 2, -sn, jnp.where(lane < ROPE, sn, 0.0))

    return _call(body, "rope_tables", (t // tm,),
                 [pl.BlockSpec((tm, 1), lambda i: (i, 0)), pl.BlockSpec((1, TAIL), lambda i: (0, 0))],
                 [pl.BlockSpec((tm, TAIL), lambda i: (i, 0))] * 2, [_sds((t, TAIL), f32)] * 2)(pos_col, inv_freq)


def _swap_halves(n):
    lane = lax.broadcasted_iota(jnp.int32, n.shape, 1)
    return jnp.where(lane < ROPE // 2, pltpu.roll(n, TAIL - ROPE // 2, 1), pltpu.roll(n, ROPE // 2, 1))


def _rope(n, c, s):
    return n * c + _swap_halves(n) * s


def _rope_t(d, c, s):
    return d * c + _swap_halves(d * s)


def _prep_specs(tm):
    head = pl.BlockSpec((tm, HEAD_PAD), lambda i, h: (i, h))
    row = pl.BlockSpec((tm, TAIL), lambda i, h: (i, 0))
    gain = pl.BlockSpec((1, TAIL), lambda i, h: (0, 0))
    return head, row, gain


def _pe_in(zt):
    lane = lax.broadcasted_iota(jnp.int32, zt.shape, 1)
    return jnp.where(lane < ROPE, zt, 0.0)


def mla_prep(q_raw, kv_raw, z_tail, cos, sin, gqn, gqp, gkn, gkp):
    t = q_raw.shape[0]
    tm = _tile(t, (512, 256, 128, 64))

    def body(q_ref, kv_ref, zt_ref, c_ref, s_ref, gqn_ref, gqp_ref, gkn_ref, gkp_ref, qh_ref, kh_ref, vh_ref):
        c, s = c_ref[...], s_ref[...]
        qh_ref[:, :NOPE] = _rms(q_ref[:, :NOPE], gqn_ref[...]).astype(bf16)
        qh_ref[:, NOPE:] = _rope(_rms_pad(q_ref[:, NOPE:], gqp_ref[...], ROPE), c, s).astype(bf16)
        kh_ref[:, :NOPE] = _rms(kv_ref[:, :NOPE], gkn_ref[...]).astype(bf16)
        kh_ref[:, NOPE:] = _rope(_rms_pad(_pe_in(zt_ref[...]), gkp_ref[...], ROPE), c, s).astype(bf16)
        vh_ref[...] = kv_ref[:, NOPE:].astype(bf16)

    head, row, gain = _prep_specs(tm)
    return _call(body, "mla_prep", (t // tm, MLA_HEADS), [head, head, row, row, row, gain, gain, gain, gain],
                 [head, head, pl.BlockSpec((tm, V_HEAD), lambda i, h: (i, h))],
                 [_sds((t, MLA_HEADS * HEAD_PAD), bf16), _sds((t, MLA_HEADS * HEAD_PAD), bf16), _sds((t, MLA_HEADS * V_HEAD), bf16)],
                 )(q_raw, kv_raw, z_tail, cos, sin, gqn, gqp, gkn, gkp)


def mla_prep_bwd(q_raw, kv_raw, z_tail, cos, sin, gqn, gqp, gkn, gkp, dqh, dkh, dvh):
    t = q_raw.shape[0]
    tm = _tile(t, (512, 256, 128, 64))
    pad_norm = functools.partial(_rms_pad, width=ROPE)

    def body(q_ref, kv_ref, zt_ref, c_ref, s_ref, gqn_ref, gqp_ref, gkn_ref, gkp_ref, dqh_ref, dkh_ref, dvh_ref,
             dq_ref, dkv_ref, dzt_ref, dgqn_ref, dgqp_ref, dgkn_ref, dgkp_ref):
        i, h = pl.program_id(0), pl.program_id(1)
        first = _first(i, h)
        c, s = c_ref[...], s_ref[...]
        _, v1 = jax.vjp(_rms, q_ref[:, :NOPE], gqn_ref[...])
        dx, dg = v1(dqh_ref[:, :NOPE])
        dq_ref[:, :NOPE] = dx.astype(bf16)
        _acc_row(dgqn_ref, dg, first)
        _, v2 = jax.vjp(pad_norm, q_ref[:, NOPE:], gqp_ref[...])
        dx, dg = v2(_rope_t(dqh_ref[:, NOPE:], c, s))
        dq_ref[:, NOPE:] = dx.astype(bf16)
        _acc_row(dgqp_ref, dg, first)
        _, v3 = jax.vjp(_rms, kv_ref[:, :NOPE], gkn_ref[...])
        dx, dg = v3(dkh_ref[:, :NOPE])
        dkv_ref[:, :NOPE] = dx.astype(bf16)
        _acc_row(dgkn_ref, dg, first)
        dkv_ref[:, NOPE:] = dvh_ref[...].astype(bf16)
        _, v4 = jax.vjp(pad_norm, _pe_in(zt_ref[...]), gkp_ref[...])
        dx, dg = v4(_rope_t(dkh_ref[:, NOPE:], c, s))
        _acc_row(dgkp_ref, dg, first)

        @pl.when(h == 0)
        def _():
            dzt_ref[...] = jnp.zeros_like(dzt_ref)

        dzt_ref[...] += dx

    head, row, gain = _prep_specs(tm)
    acc = pl.BlockSpec((8, TAIL), lambda i, h: (0, 0))
    vspec = pl.BlockSpec((tm, V_HEAD), lambda i, h: (i, h))
    return _call(body, "mla_prep_bwd", (t // tm, MLA_HEADS),
                 [head, head, row, row, row, gain, gain, gain, gain, head, head, vspec],
                 [head, head, row, acc, acc, acc, acc],
                 [_sds((t, MLA_HEADS * HEAD_PAD), bf16), _sds((t, MLA_HEADS * HEAD_PAD), bf16), _sds((t, TAIL), f32)]
                 + [_sds((8, TAIL), f32)] * 4)(q_raw, kv_raw, z_tail, cos, sin, gqn, gqp, gkn, gkp, dqh, dkh, dvh)


ATT_BLOCK = 512
NEG = -1e30
ATT_SCALE = (NOPE + ROPE) ** -0.5
ATT_HEADS = 2


def _chunk_visible(shape, key_axis):
    kc = lax.broadcasted_iota(jnp.int32, shape, key_axis) >> CHUNK_SHIFT
    qc = lax.broadcasted_iota(jnp.int32, shape, 1 - key_axis) >> CHUNK_SHIFT
    return kc <= qc


def mla_fwd(qh, kh, vh, exchange=None):
    t = qh.shape[0]
    tb = min(ATT_BLOCK, t)
    nb = t // tb

    hp = ATT_HEADS

    def body(q_ref, k_ref, v_ref, o_ref, ob_ref, lse_ref, m_s, l_s, acc):
        qi, ki = pl.program_id(1), pl.program_id(2)

        @pl.when(ki == 0)
        def _():
            m_s[...] = jnp.full_like(m_s, NEG)
            l_s[...] = jnp.zeros_like(l_s)
            acc[...] = jnp.zeros_like(acc)

        def step(diagonal):
            new = []
            for j in range(hp):
                q, k = q_ref[:, j * HEAD_PAD:(j + 1) * HEAD_PAD], k_ref[:, j * HEAD_PAD:(j + 1) * HEAD_PAD]
                s = _bdot(k, q, 1, 1) * ATT_SCALE
                if diagonal:
                    s = jnp.where(_chunk_visible(s.shape, 0), s, -jnp.inf)
                m_old = m_s[j]
                m_new = jnp.maximum(m_old, jnp.max(s, axis=0, keepdims=True))
                p = jnp.exp(s - m_new)
                alpha = jnp.exp(m_old - m_new)
                l_new = alpha * l_s[j] + jnp.sum(p, axis=0, keepdims=True)
                acc_new = alpha * acc[j] + _bdot(v_ref[:, j * V_HEAD:(j + 1) * V_HEAD], p, 0, 0)
                new.append((m_new, l_new, acc_new))
            for j, (m_new, l_new, acc_new) in enumerate(new):
                m_s[j] = m_new
                l_s[j] = l_new
                acc[j] = acc_new
            return new

        @pl.when(ki < qi)
        def _():
            step(False)

        @pl.when(ki == qi)
        def _():
            for j, (m_new, l_new, acc_new) in enumerate(step(True)):
                o = (acc_new / l_new).T
                o_ref[:, j * V_HEAD:(j + 1) * V_HEAD] = o
                ob_ref[:, j * V_HEAD:(j + 1) * V_HEAD] = o.astype(bf16)
                lse_ref[j] = m_new + jnp.log(l_new)

    kv = lambda g, qi, ki: (jnp.minimum(ki, qi), g)
    o_spec = pl.BlockSpec((tb, hp * V_HEAD), lambda g, qi, ki: (qi, g))
    return _call(body, "mla_fwd", (MLA_HEADS // hp, nb, nb),
                 [pl.BlockSpec((tb, hp * HEAD_PAD), lambda g, qi, ki: (qi, g)), pl.BlockSpec((tb, hp * HEAD_PAD), kv),
                  pl.BlockSpec((tb, hp * V_HEAD), kv)],
                 [o_spec, o_spec, pl.BlockSpec((hp, 1, tb), lambda g, qi, ki: (g, 0, qi))],
                 [_sds((t, MLA_HEADS * V_HEAD), f32), _sds((t, MLA_HEADS * V_HEAD), bf16), _sds((MLA_HEADS, 1, t), f32)],
                 [pltpu.VMEM((hp, 1, tb), f32), pltpu.VMEM((hp, 1, tb), f32), pltpu.VMEM((hp, V_HEAD, tb), f32)],
                 exchange=exchange)(qh, kh, vh)


def mla_delta(o, do):
    t = o.shape[0]
    tm = _tile(t, (512, 256, 128, 64))

    def body(o_ref, do_ref, d_ref):
        d_ref[0] = jnp.sum(o_ref[...] * do_ref[...], axis=1, keepdims=True)

    blk = pl.BlockSpec((tm, V_HEAD), lambda i, h: (i, h))
    return _call(body, "mla_delta", (t // tm, MLA_HEADS), [blk, blk], pl.BlockSpec((1, tm, 1), lambda i, h: (h, i, 0)),
                 _sds((MLA_HEADS, t, 1), f32))(o, do)


def mla_bwd(qh, kh, vh, do, lse_row, delta_row, exchange=None):
    t = qh.shape[0]
    tb = min(ATT_BLOCK, t)
    nb = t // tb

    hp = ATT_HEADS

    def body(q_ref, k_ref, v_ref, do_ref, lse_ref, dl_ref, dq_ref, dk_ref, dv_ref, dk_acc, dv_acc):
        ki, qi = pl.program_id(1), pl.program_id(2)

        @pl.when(jnp.logical_and(ki == 0, qi == 0))
        def _():
            dq_ref[...] = jnp.zeros_like(dq_ref)

        @pl.when(qi == 0)
        def _():
            dk_acc[...] = jnp.zeros_like(dk_acc)
            dv_acc[...] = jnp.zeros_like(dv_acc)

        def step(diagonal):
            rows = pl.ds(pl.multiple_of(qi * tb, tb), tb)
            new = []
            for j in range(hp):
                qc, vc = slice(j * HEAD_PAD, (j + 1) * HEAD_PAD), slice(j * V_HEAD, (j + 1) * V_HEAD)
                q, k, do_b = q_ref[:, qc], k_ref[:, qc], do_ref[:, vc]
                s = _bdot(k, q, 1, 1) * ATT_SCALE
                if diagonal:
                    s = jnp.where(_chunk_visible(s.shape, 0), s, -jnp.inf)
                p = jnp.exp(s - lse_ref[j])
                dp = _bdot(v_ref[:, vc], do_b, 1, 1)
                ds = p * (dp - dl_ref[j]) * ATT_SCALE
                new.append((dv_acc[:, vc] + _bdot(p, do_b, 1, 0), dk_acc[:, qc] + _bdot(ds, q, 1, 0),
                            dq_ref[rows, qc] + _bdot(ds, k, 0, 0)))
            for j, (dv, dk, dq) in enumerate(new):
                dv_acc[:, j * V_HEAD:(j + 1) * V_HEAD] = dv
                dk_acc[:, j * HEAD_PAD:(j + 1) * HEAD_PAD] = dk
                dq_ref[rows, j * HEAD_PAD:(j + 1) * HEAD_PAD] = dq

        @pl.when(qi > ki)
        def _():
            step(False)

        @pl.when(qi == ki)
        def _():
            step(True)

        @pl.when(qi == nb - 1)
        def _():
            dk_ref[...] = dk_acc[...]
            dv_ref[...] = dv_acc[...]

    qs = lambda g, ki, qi: (jnp.maximum(qi, ki), g)
    ks = lambda g, ki, qi: (ki, g)
    vec = pl.BlockSpec((hp, 1, tb), lambda g, ki, qi: (g, 0, jnp.maximum(qi, ki)))
    return _call(body, "mla_bwd", (MLA_HEADS // hp, nb, nb),
                 [pl.BlockSpec((tb, hp * HEAD_PAD), qs), pl.BlockSpec((tb, hp * HEAD_PAD), ks), pl.BlockSpec((tb, hp * V_HEAD), ks),
                  pl.BlockSpec((tb, hp * V_HEAD), qs), vec, vec],
                 [pl.BlockSpec((t, hp * HEAD_PAD), lambda g, ki, qi: (0, g)), pl.BlockSpec((tb, hp * HEAD_PAD), ks),
                  pl.BlockSpec((tb, hp * V_HEAD), ks)],
                 [_sds((t, MLA_HEADS * HEAD_PAD), f32), _sds((t, MLA_HEADS * HEAD_PAD), f32), _sds((t, MLA_HEADS * V_HEAD), f32)],
                 [pltpu.VMEM((tb, hp * HEAD_PAD), f32), pltpu.VMEM((tb, hp * V_HEAD), f32)], exchange=exchange)(
        qh, kh, vh, do, lse_row, delta_row)


PAD = 8


def _conv_taps(pad_ref, w, width, t):
    y = pad_ref[PAD - width + 1:PAD - width + 1 + t, :] * w[0:1, :]
    for j in range(1, width):
        y = y + pad_ref[PAD - width + 1 + j:PAD - width + 1 + j + t, :] * w[j:j + 1, :]
    return y


def _conv_bwd(xpad_ref, dpad_ref, w, da, width, t):
    dpad_ref[0:t, :] = da
    dpad_ref[t:t + PAD, :] = jnp.zeros((PAD, da.shape[1]), f32)
    dx = dpad_ref[width - 1:width - 1 + t, :] * w[0:1, :]
    for j in range(1, width):
        dx = dx + dpad_ref[width - 1 - j:width - 1 - j + t, :] * w[j:j + 1, :]
    dws = [jnp.sum(da * xpad_ref[PAD - width + 1 + j:PAD - width + 1 + j + t, :], axis=0, keepdims=True) for j in range(width)]
    return dx, dws


def _load_pad(pad_ref, x, t):
    pad_ref[0:PAD, :] = jnp.zeros((PAD, x.shape[1]), f32)
    pad_ref[PAD:PAD + t, :] = x


assert ML_DK == 128


def qk_conv(z_main, conv_qk):
    t = z_main.shape[0]
    base = O_Q // ML_DK

    def body(z_ref, w_ref, o_ref, pad):
        _load_pad(pad, z_ref[...], t)
        a = _conv_taps(pad, w_ref[...], ML_CONV, t)
        sc = jnp.where(pl.program_id(0) < ML_HEADS, ML_DK ** -0.5, 1.0)
        o_ref[0] = jax.nn.silu(a) * sc

    return _call(body, "qk_conv", (2 * ML_HEADS,),
                 [pl.BlockSpec((t, ML_DK), lambda j: (0, base + j)), pl.BlockSpec((ML_CONV, ML_DK), lambda j: (0, j))],
                 pl.BlockSpec((1, t, ML_DK), lambda j: (j, 0, 0)), _sds((2 * ML_HEADS, t, ML_DK), f32),
                 [pltpu.VMEM((t + PAD, ML_DK), f32)])(z_main, conv_qk)


def qk_conv_bwd(z_main, conv_qk, dq, dk):
    t = z_main.shape[0]
    base = O_Q // ML_DK

    def body(z_ref, w_ref, dq_ref, dk_ref, dz_ref, dw_ref, pad, dpad):
        _load_pad(pad, z_ref[...], t)
        w = w_ref[...]
        a = _conv_taps(pad, w, ML_CONV, t)
        is_q = pl.program_id(0) < ML_HEADS
        d = jnp.where(is_q, dq_ref[0] * (ML_DK ** -0.5), dk_ref[0])
        _, vjp = jax.vjp(jax.nn.silu, a)
        da, = vjp(d)
        dx, dws = _conv_bwd(pad, dpad, w, da, ML_CONV, t)
        dz_ref[...] = dx.astype(bf16)
        for j in range(ML_CONV):
            dw_ref[j:j + 1, :] = dws[j]

    head = lambda pick: pl.BlockSpec((1, t, ML_DK), lambda j: (pick(j), 0, 0))
    return _call(body, "qk_conv_bwd", (2 * ML_HEADS,),
                 [pl.BlockSpec((t, ML_DK), lambda j: (0, base + j)), pl.BlockSpec((ML_CONV, ML_DK), lambda j: (0, j)),
                  head(lambda j: jnp.minimum(j, ML_HEADS - 1)), head(lambda j: jnp.maximum(j - ML_HEADS, 0))],
                 [pl.BlockSpec((t, ML_DK), lambda j: (0, j)), pl.BlockSpec((ML_CONV, ML_DK), lambda j: (0, j))],
                 [_sds((t, 2 * ML_QK), bf16), _sds((ML_CONV, 2 * ML_QK), f32)],
                 [pltpu.VMEM((t + PAD, ML_DK), f32), pltpu.VMEM((t + PAD, ML_DK), f32)])(z_main, conv_qk, dq, dk)


def glu_fwd(hup, conv_w, bias):
    t, f2 = hup.shape
    nf = f2 // 2 // 128

    def body(h1_ref, h2_ref, w1_ref, w2_ref, b1_ref, b2_ref, o_ref, pad):
        _load_pad(pad, h1_ref[...], t)
        a1 = _conv_taps(pad, w1_ref[...], FFN_CONV, t) + b1_ref[...]
        _load_pad(pad, h2_ref[...], t)
        a2 = _conv_taps(pad, w2_ref[...], FFN_CONV, t) + b2_ref[...]
        o_ref[...] = (jax.nn.silu(a1) * a2).astype(bf16)

    col = lambda off: pl.BlockSpec((t, 128), lambda j: (0, j + off))
    wsp = lambda off: pl.BlockSpec((FFN_CONV, 128), lambda j: (0, j + off))
    bsp = lambda off: pl.BlockSpec((1, 128), lambda j: (0, j + off))
    return _call(body, "glu_fwd", (nf,), [col(0), col(nf), wsp(0), wsp(nf), bsp(0), bsp(nf)], col(0), _sds((t, f2 // 2), bf16),
                 [pltpu.VMEM((t + PAD, 128), f32)])(hup, hup, conv_w, conv_w, bias, bias)


def glu_bwd(hup, conv_w, bias, dg, exchange=None):
    t, f2 = hup.shape
    f = f2 // 2
    nf = f // 128

    def body(h1_ref, h2_ref, w1_ref, w2_ref, b1_ref, b2_ref, dg_ref, dh1_ref, dh2_ref, dw1_ref, dw2_ref, db1_ref, db2_ref,
             pad1, pad2, dpad):
        _load_pad(pad1, h1_ref[...], t)
        _load_pad(pad2, h2_ref[...], t)
        w1, w2 = w1_ref[...], w2_ref[...]
        a1 = _conv_taps(pad1, w1, FFN_CONV, t) + b1_ref[...]
        a2 = _conv_taps(pad2, w2, FFN_CONV, t) + b2_ref[...]
        d = dg_ref[...]
        _, vjp = jax.vjp(jax.nn.silu, a1)
        da1, = vjp(d * a2)
        da2 = d * jax.nn.silu(a1)
        for da, pad, w, dh_ref, dw_ref, db_ref in ((da1, pad1, w1, dh1_ref, dw1_ref, db1_ref), (da2, pad2, w2, dh2_ref, dw2_ref, db2_ref)):
            dx, dws = _conv_bwd(pad, dpad, w, da, FFN_CONV, t)
            dh_ref[...] = dx.astype(bf16)
            for j in range(FFN_CONV):
                dw_ref[j:j + 1, :] = dws[j]
            db_ref[...] = jnp.sum(da, axis=0, keepdims=True)

    col = lambda off: pl.BlockSpec((t, 128), lambda j: (0, j + off))
    wsp = lambda off: pl.BlockSpec((FFN_CONV, 128), lambda j: (0, j + off))
    bsp = lambda off: pl.BlockSpec((1, 128), lambda j: (0, j + off))
    return _call(body, "glu_bwd", (nf,), [col(0), col(nf), wsp(0), wsp(nf), bsp(0), bsp(nf), col(0)],
                 [col(0), col(0), wsp(0), wsp(0), bsp(0), bsp(0)],
                 [_sds((t, f), bf16)] * 2 + [_sds((FFN_CONV, f), f32)] * 2 + [_sds((1, f), f32)] * 2,
                 [pltpu.VMEM((t + PAD, 128), f32)] * 3, exchange=exchange)(hup, hup, conv_w, conv_w, bias, bias, dg)


def gate_act(z_tail, b_tile):
    t = z_tail.shape[0]
    tm = _tile(t, (512, 256, 128, 64))

    def body(z_ref, b_ref, o_ref):
        x = z_ref[...] + b_ref[...]
        lane = lax.broadcasted_iota(jnp.int32, x.shape, 1)
        o_ref[...] = jnp.where(lane < T_F, x, jax.nn.log_sigmoid(x))

    row = pl.BlockSpec((tm, TAIL), lambda i: (i, 0))
    return _call(body, "gate_act", (t // tm,), [row, pl.BlockSpec((1, TAIL), lambda i: (0, 0))], row, _sds((t, TAIL), f32))(z_tail, b_tile)


def tail_bwd(z_tail, b_tile, dzt_pe, dgate):
    t = z_tail.shape[0]
    tm = _tile(t, (512, 256, 128, 64))

    def body(z_ref, b_ref, dpe_ref, dg_ref, dz_ref, db_ref):
        x = z_ref[...] + b_ref[...]
        lane = lax.broadcasted_iota(jnp.int32, x.shape, 1)
        _, vjp = jax.vjp(jax.nn.log_sigmoid, x)
        df, = vjp(dg_ref[...])
        dgates = jnp.where(lane < T_F, dg_ref[...], df)
        dgates = jnp.where(jnp.logical_and(lane >= T_I, lane < T_F + ML_HEADS), dgates, 0.0)
        dz_ref[...] = jnp.where(lane < ROPE, dpe_ref[...], dgates).astype(bf16)
        _acc_row(db_ref, jnp.sum(dgates, axis=0, keepdims=True), pl.program_id(0) == 0)

    row = pl.BlockSpec((tm, TAIL), lambda i: (i, 0))
    return _call(body, "tail_bwd", (t // tm,), [row, pl.BlockSpec((1, TAIL), lambda i: (0, 0)), row, row],
                 [row, pl.BlockSpec((8, TAIL), lambda i: (0, 0))], [_sds((t, TAIL), bf16), _sds((8, TAIL), f32)])(z_tail, b_tile, dzt_pe, dgate)


def _hdot(a, b, ca, cb):
    return lax.dot_general(a.astype(bf16), b.astype(bf16), (((ca,), (cb,)), ((0,), (0,))), preferred_element_type=f32)


def _mlstm_step(q, k, v, igr, fgr, c_mat, n_vec, m):
    nh, ln = q.shape[0], CHUNK
    sq = (nh, ln, ln)
    row = lax.broadcasted_iota(jnp.int32, sq, 1)
    col = lax.broadcasted_iota(jnp.int32, sq, 2)
    eye = row == col

    def to_col(r):
        return jnp.sum(jnp.where(eye, jnp.broadcast_to(r, sq), 0.0), axis=2, keepdims=True)

    bc_r = jnp.sum(jnp.where(row <= col, jnp.broadcast_to(to_col(fgr), sq), 0.0), axis=1, keepdims=True)
    bc_c = to_col(bc_r)
    logw = jnp.where(col <= row, bc_c - bc_r + igr, -jnp.inf)
    inter = bc_c + m
    m_t = jnp.maximum(inter, jnp.max(logw, axis=2, keepdims=True))
    w_intra = jnp.exp(logw - m_t)
    w_inter = jnp.exp(inter - m_t)
    sc = _hdot(q, k, 2, 2) * w_intra
    num = w_inter * _hdot(q, c_mat, 2, 1) + _hdot(sc, v, 2, 1)
    qn = jnp.sum(q.astype(bf16).astype(f32) * n_vec.astype(bf16).astype(f32), axis=2, keepdims=True)
    den = w_inter * qn + jnp.sum(sc, axis=2, keepdims=True)
    h = num / jnp.maximum(jnp.abs(den), jnp.exp(-m_t))
    lane = lax.broadcasted_iota(jnp.int32, (nh, 1, ln), 2)
    b_last = jnp.sum(jnp.where(lane == ln - 1, bc_r, 0.0), axis=2, keepdims=True)
    logu = b_last - bc_r + igr
    m_new = jnp.maximum(b_last + m, jnp.max(logu, axis=2, keepdims=True))
    decay = jnp.exp(b_last + m - m_new)
    u_c = to_col(jnp.exp(logu - m_new))
    c_new = decay * c_mat + _hdot(u_c * k, v, 1, 1)
    n_new = decay * n_vec + jnp.sum(u_c.astype(bf16).astype(f32) * k.astype(bf16).astype(f32), axis=1, keepdims=True)
    return h, c_new, n_new, m_new


ML_VHALF = ML_V // 2
assert O_V % ML_VHALF == 0 and ML_HEADS % 2 == 0


def _ml_specs(nc, rev):
    cc = (lambda c: nc - 1 - c) if rev else (lambda c: c)
    q = pl.BlockSpec((ML_HEADS, CHUNK, ML_DK), lambda c: (0, cc(c), 0))
    k = pl.BlockSpec((ML_HEADS, CHUNK, ML_DK), lambda c: (1, cc(c), 0))
    v_lo = pl.BlockSpec((CHUNK, ML_VHALF), lambda c: (cc(c), O_V // ML_VHALF))
    v_hi = pl.BlockSpec((CHUNK, ML_VHALF), lambda c: (cc(c), O_V // ML_VHALF + 1))
    hv = pl.BlockSpec((ML_HEADS, CHUNK, ML_DV), lambda c: (0, cc(c), 0))
    gate = pl.BlockSpec((ML_HEADS, 1, 1, CHUNK), lambda c: (0, cc(c), 0, 0))
    cm = pl.BlockSpec((ML_HEADS, 1, ML_DK, ML_DV), lambda c: (0, cc(c), 0, 0))
    nv = pl.BlockSpec((ML_HEADS, 1, 1, ML_DK), lambda c: (0, cc(c), 0, 0))
    ms = pl.BlockSpec((ML_HEADS, 1, 1, 1), lambda c: (0, cc(c), 0, 0))
    return q, k, v_lo, v_hi, hv, gate, cm, nv, ms


_ML_STATE = [pltpu.VMEM((ML_HEADS, ML_DK, ML_DV), f32), pltpu.VMEM((ML_HEADS, 1, ML_DK), f32), pltpu.VMEM((ML_HEADS, 1, 1), f32)]


def _ml_zero_state(c_s, n_s, m_s):
    @pl.when(pl.program_id(0) == 0)
    def _():
        c_s[...] = jnp.zeros_like(c_s)
        n_s[...] = jnp.zeros_like(n_s)
        m_s[...] = jnp.zeros_like(m_s)


def _ml_heads_of(v_lo_ref, v_hi_ref):
    half = ML_HEADS // 2
    return jnp.stack([r[:, j * ML_DV:(j + 1) * ML_DV] for r in (v_lo_ref, v_hi_ref) for j in range(half)])


def mlstm_fwd(qk_act, z_main, ig, fg):
    t = qk_act.shape[1]
    nc = t // CHUNK

    def body(q_ref, k_ref, vl_ref, vh_ref, ig_ref, fg_ref, h_ref, c_out, n_out, m_out, c_s, n_s, m_s):
        _ml_zero_state(c_s, n_s, m_s)
        c0, n0, m0 = c_s[...], n_s[...], m_s[...]
        c_out[:, 0] = c0
        n_out[:, 0] = n0
        m_out[:, 0] = m0
        h, c2, n2, m2 = _mlstm_step(q_ref[...], k_ref[...], _ml_heads_of(vl_ref, vh_ref), ig_ref[:, 0], fg_ref[:, 0], c0, n0, m0)
        h_ref[...] = h
        c_s[...] = c2
        n_s[...] = n2
        m_s[...] = m2

    q, k, v_lo, v_hi, hv, gate, cm, nv, ms = _ml_specs(nc, False)
    return _call(body, "mlstm_fwd", (nc,), [q, k, v_lo, v_hi, gate, gate], [hv, cm, nv, ms],
                 [_sds((ML_HEADS, t, ML_DV), f32), _sds((ML_HEADS, nc, ML_DK, ML_DV), f32), _sds((ML_HEADS, nc, 1, ML_DK), f32),
                  _sds((ML_HEADS, nc, 1, 1), f32)], _ML_STATE)(qk_act, qk_act, z_main, z_main, ig, fg)


def mlstm_bwd(qk_act, z_main, ig, fg, c_all, n_all, m_all, dh, exchange=None):
    t = qk_act.shape[1]
    nc = t // CHUNK

    def body(q_ref, k_ref, vl_ref, vh_ref, ig_ref, fg_ref, c_ref, n_ref, m_ref, dh_ref, dq_ref, dk_ref, dv_ref, dig_ref, dfg_ref,
             dc_s, dn_s, dm_s):
        _ml_zero_state(dc_s, dn_s, dm_s)
        _, vjp = jax.vjp(_mlstm_step, q_ref[...], k_ref[...], _ml_heads_of(vl_ref, vh_ref), ig_ref[:, 0], fg_ref[:, 0],
                         c_ref[:, 0], n_ref[:, 0], m_ref[:, 0])
        dq, dk, dv, dig, dfg, dc, dn, dm = vjp((dh_ref[...], dc_s[...], dn_s[...], dm_s[...]))
        dq_ref[...] = dq
        dk_ref[...] = dk
        for j in range(ML_HEADS):
            dv_ref[:, j * ML_DV:(j + 1) * ML_DV] = dv[j].astype(bf16)
        dig_ref[:, 0] = dig
        dfg_ref[:, 0] = dfg
        dc_s[...] = dc
        dn_s[...] = dn
        dm_s[...] = dm

    q, k, v_lo, v_hi, hv, gate, cm, nv, ms = _ml_specs(nc, True)
    gshape = _sds((ML_HEADS, nc, 1, CHUNK), f32)
    return _call(body, "mlstm_bwd", (nc,), [q, k, v_lo, v_hi, gate, gate, cm, nv, ms, hv],
                 [q, q, pl.BlockSpec((CHUNK, ML_V), lambda c: (nc - 1 - c, 0)), gate, gate],
                 [_sds((ML_HEADS, t, ML_DK), f32), _sds((ML_HEADS, t, ML_DK), f32), _sds((t, ML_V), bf16), gshape, gshape],
                 _ML_STATE, exchange=exchange)(qk_act, qk_act, z_main, z_main, ig, fg, c_all, n_all, m_all, dh)


def _ml_out(h, zo, g):
    return _rms(h, g) * jax.nn.sigmoid(zo)


def mlstm_out(h, z_main, g_hnorm):
    t = h.shape[1]
    tm = _tile(t, (512, 256, 128, 64))
    zo = O_O // ML_DV

    def body(h_ref, z_ref, g_ref, y_ref):
        y_ref[...] = _ml_out(h_ref[0], z_ref[...], g_ref[0]).astype(bf16)

    return _call(body, "mlstm_out", (t // tm, ML_HEADS),
                 [pl.BlockSpec((1, tm, ML_DV), lambda i, hd: (hd, i, 0)), pl.BlockSpec((tm, ML_DV), lambda i, hd: (i, zo + hd)),
                  pl.BlockSpec((1, 1, ML_DV), lambda i, hd: (hd, 0, 0))],
                 pl.BlockSpec((tm, ML_DV), lambda i, hd: (i, hd)), _sds((t, ML_V), bf16))(h, z_main, g_hnorm)


def mlstm_out_bwd(h, z_main, g_hnorm, dy):
    t = h.shape[1]
    tm = _tile(t, (512, 256, 128, 64))
    zo = O_O // ML_DV

    def body(h_ref, z_ref, g_ref, dy_ref, dh_ref, dzo_ref, dg_ref):
        _, vjp = jax.vjp(_ml_out, h_ref[0], z_ref[...], g_ref[0])
        dh, dz, dg = vjp(dy_ref[...])
        dh_ref[0] = dh
        dzo_ref[...] = dz.astype(bf16)

        @pl.when(pl.program_id(1) == 0)
        def _():
            dg_ref[...] = jnp.zeros_like(dg_ref)

        dg_ref[0, 0:1, :] += dg

    head = pl.BlockSpec((1, tm, ML_DV), lambda hd, i: (hd, i, 0))
    blk = pl.BlockSpec((tm, ML_DV), lambda hd, i: (i, hd))
    return _call(body, "mlstm_out_bwd", (ML_HEADS, t // tm),
                 [head, pl.BlockSpec((tm, ML_DV), lambda hd, i: (i, zo + hd)), pl.BlockSpec((1, 1, ML_DV), lambda hd, i: (hd, 0, 0)), blk],
                 [head, blk, pl.BlockSpec((1, 8, ML_DV), lambda hd, i: (hd, 0, 0))],
                 [_sds((ML_HEADS, t, ML_DV), f32), _sds((t, ML_V), bf16), _sds((ML_HEADS, 8, ML_DV), f32)])(h, z_main, g_hnorm, dy)


def _merge(ga, gb, ya, yb):
    return jax.nn.sigmoid(ga) * ya + jax.nn.sigmoid(gb) * yb


def _merge_specs(t, d):
    tm = _tile(t, (512, 256, 128, 64))
    bw = _tile(d, (512, 256, 128))
    assert O_GA % bw == 0 and (O_GA + d) % bw == 0
    blk = pl.BlockSpec((tm, bw), lambda i, j: (i, j))
    ga = pl.BlockSpec((tm, bw), lambda i, j: (i, O_GA // bw + j))
    gb = pl.BlockSpec((tm, bw), lambda i, j: (i, (O_GA + d) // bw + j))
    return tm, bw, blk, ga, gb


def merge_fwd(z_main, ya, yb):
    t, d = ya.shape
    tm, bw, blk, ga, gb = _merge_specs(t, d)

    def body(ga_ref, gb_ref, ya_ref, yb_ref, o_ref):
        o_ref[...] = _merge(ga_ref[...], gb_ref[...], ya_ref[...], yb_ref[...]).astype(bf16)

    return _call(body, "merge_fwd", (t // tm, d // bw), [ga, gb, blk, blk], blk, _sds((t, d), bf16))(z_main, z_main, ya, yb)


def merge_bwd(z_main, ya, yb, dmerged):
    t, d = ya.shape
    tm, bw, blk, ga, gb = _merge_specs(t, d)

    def body(ga_ref, gb_ref, ya_ref, yb_ref, dm_ref, dga_ref, dgb_ref, dya_ref, dyb_ref):
        _, vjp = jax.vjp(_merge, ga_ref[...], gb_ref[...], ya_ref[...], yb_ref[...])
        dga, dgb, dya, dyb = vjp(dm_ref[...])
        dga_ref[...] = dga.astype(bf16)
        dgb_ref[...] = dgb.astype(bf16)
        dya_ref[...] = dya.astype(bf16)
        dyb_ref[...] = dyb.astype(bf16)

    return _call(body, "merge_bwd", (t // tm, d // bw), [ga, gb, blk, blk, blk], [blk] * 4, [_sds((t, d), bf16)] * 4)(
        z_main, z_main, ya, yb, dmerged)


def _cross(cq, ck, cv, gq, gk):
    outs = []
    for hd in range(CR_HEADS):
        sl = slice(hd * CR_HD, (hd + 1) * CR_HD)
        q = _rms(cq[:, sl], gq)
        k = _rms(ck[:, sl], gk)
        s = _bdot(q, k, 1, 1) * (CR_HD ** -0.5)
        p = jax.nn.softmax(s, axis=-1)
        outs.append(_bdot(p, cv[:, sl], 1, 0))
    return jnp.concatenate(outs, axis=1)


def cross_fwd(cq, ck, cv, gq, gk):
    t, w = cq.shape
    nm = ck.shape[0]
    tm = _tile(t, (512, 256, 128, 64))

    def body(q_ref, k_ref, v_ref, gq_ref, gk_ref, o_ref):
        o_ref[...] = _cross(q_ref[...], k_ref[...], v_ref[...], gq_ref[...], gk_ref[...]).astype(bf16)

    row = pl.BlockSpec((tm, w), lambda i: (i, 0))
    full = pl.BlockSpec((nm, w), lambda i: (0, 0))
    gain = pl.BlockSpec((1, CR_HD), lambda i: (0, 0))
    return _call(body, "cross_fwd", (t // tm,), [row, full, full, gain, gain], row, _sds((t, w), bf16))(cq, ck, cv, gq, gk)


def cross_bwd(cq, ck, cv, gq, gk, do):
    t, w = cq.shape
    nm = ck.shape[0]
    tm = _tile(t, (512, 256, 128, 64))

    def body(q_ref, k_ref, v_ref, gq_ref, gk_ref, do_ref, dq_ref, dk_ref, dv_ref, dgq_ref, dgk_ref):
        first = pl.program_id(0) == 0
        _, vjp = jax.vjp(_cross, q_ref[...], k_ref[...], v_ref[...], gq_ref[...], gk_ref[...])
        dq, dk, dv, dgq, dgk = vjp(do_ref[...])
        dq_ref[...] = dq.astype(bf16)

        @pl.when(first)
        def _():
            dk_ref[...] = jnp.zeros_like(dk_ref)
            dv_ref[...] = jnp.zeros_like(dv_ref)

        dk_ref[...] += dk
        dv_ref[...] += dv
        _acc_row(dgq_ref, dgq, first)
        _acc_row(dgk_ref, dgk, first)

    row = pl.BlockSpec((tm, w), lambda i: (i, 0))
    full = pl.BlockSpec((nm, w), lambda i: (0, 0))
    gain = pl.BlockSpec((1, CR_HD), lambda i: (0, 0))
    acc = pl.BlockSpec((8, CR_HD), lambda i: (0, 0))
    return _call(body, "cross_bwd", (t // tm,), [row, full, full, gain, gain, row], [row, full, full, acc, acc],
                 [_sds((t, w), bf16), _sds((nm, w), f32), _sds((nm, w), f32), _sds((8, CR_HD), f32), _sds((8, CR_HD), f32)])(
        cq, ck, cv, gq, gk, do)


def loss_head(x2, fo, target):
    t, d = x2.shape
    tm = _tile(t, (256, 128, 64, 32, 16, 8))

    def body(a_ref, b_ref, t_ref, dx_ref, dxb_ref, l_ref):
        err = a_ref[...] + b_ref[...] - t_ref[...]
        dx = err / d
        dx_ref[...] = dx
        dxb_ref[...] = dx.astype(bf16)
        part = 0.5 * jnp.sum(jnp.mean(err * err, axis=1, keepdims=True), axis=0, keepdims=True)
        _acc_row(l_ref, jnp.broadcast_to(part, (1, 128)), pl.program_id(0) == 0)

    row = pl.BlockSpec((tm, d), lambda i: (i, 0))
    return _call(body, "loss_head", (t // tm,), [row, row, row], [row, row, pl.BlockSpec((8, 128), lambda i: (0, 0))],
                 [_sds((t, d), f32), _sds((t, d), bf16), _sds((8, 128), f32)])(x2, fo, target)


def _place():
    x, y, c = lax.axis_index("x"), lax.axis_index("y"), lax.axis_index("c")
    peers = {}
    for r in range(1, N_DEV):
        px = 1 - x if r & 4 else x
        py = 1 - y if r & 2 else y
        pc = 1 - c if r & 1 else c
        peers[r] = ((px, py, pc), 4 * px + 2 * py + pc)
    return 4 * x + 2 * y + c, peers


N_REL = N_DEV - 1
RELATIONS = tuple(range(1, N_DEV))
SIBLING = 1
OTHER_CHIPS = (2, 4, 6)
PASSED_ON = (3, 5, 7)


def _exchange_ops(ins, outs, sems, scatter):
    n = len(ins)
    send_sems, recv_sems, local_sems = sems

    def tools():
        me, peers = _place()

        def copy(a, r, src, dst_idx, to):
            return pltpu.make_async_remote_copy(
                src_ref=src, dst_ref=outs[a].at[dst_idx], send_sem=send_sems.at[a * N_REL + r - 1],
                recv_sem=recv_sems.at[a * N_REL + r - 1], device_id=peers[to][0], device_id_type=MESH)

        def local(a):
            return pltpu.make_async_copy(ins[a].at[me] if scatter else ins[a], outs[a].at[me], local_sems.at[a])

        def arrival(a, r):
            return copy(a, r, ins[a].at[me] if scatter else ins[a], peers[r][1], r)

        return me, peers, copy, local, arrival

    if scatter:
        def sends():
            me, peers, copy, local, _ = tools()
            return [local(a) for a in range(n)], [copy(a, r, ins[a].at[peers[r][1]], me, r) for a in range(n) for r in RELATIONS]

        def start():
            loc, out = sends()
            for cp in loc + out:
                cp.start()

        middle = None
        waited_last = RELATIONS
    else:
        def sends():
            me, peers, copy, local, _ = tools()
            own = [copy(a, r, ins[a], me, r) for a in range(n) for r in (SIBLING,) + OTHER_CHIPS]
            return [local(a) for a in range(n)], own

        def passes():
            me, peers, copy, _, _ = tools()
            return [copy(a, r, outs[a].at[peers[r - 1][1]], peers[r - 1][1], SIBLING) for a in range(n) for r in PASSED_ON]

        def start():
            loc, out = sends()
            for cp in loc + out:
                cp.start()

        def middle():
            _, _, _, _, arrival = tools()
            fwd = passes()
            for a in range(n):
                for i, r in enumerate(PASSED_ON):
                    arrival(a, r - 1).wait_recv()
                    fwd[a * len(PASSED_ON) + i].start()

        waited_last = (SIBLING,) + PASSED_ON

    def wait():
        _, _, _, _, arrival = tools()
        for a in range(n):
            for r in waited_last:
                arrival(a, r).wait_recv()
        loc, out = sends()
        for cp in out + ([] if scatter else passes()):
            cp.wait_send()
        for cp in loc:
            cp.wait()

    return start, middle, wait


def _exchange_shapes(arrs, scatter):
    return [_sds(a.shape if scatter else (N_DEV,) + a.shape, a.dtype) for a in arrs]


def _exchange_sems(n):
    return [pltpu.SemaphoreType.DMA((n * N_REL,)), pltpu.SemaphoreType.DMA((n * N_REL,)), pltpu.SemaphoreType.DMA((n,))]


def _exchange(arrs, name, scatter):
    n = len(arrs)

    def body(*refs):
        start, middle, wait = _exchange_ops(refs[:n], refs[n:2 * n], refs[2 * n:], scatter)
        start()
        if middle is not None:
            middle()
        wait()

    any_spec = pl.BlockSpec(memory_space=pl.ANY)
    return pl.pallas_call(body, name=name, in_specs=[any_spec] * n, out_specs=[any_spec] * n,
                          out_shape=_exchange_shapes(arrs, scatter), scratch_shapes=_exchange_sems(n))(*arrs)


def cast_bf16(w, name):
    r, c = w.shape
    tr = _tile(r, (256, 128, 64, 32, 16))

    def body(w_ref, o_ref):
        o_ref[...] = w_ref[...].astype(bf16)

    blk = pl.BlockSpec((tr, c), lambda i: (i, 0))
    return _call(body, name, (r // tr,), [blk], blk, _sds((r, c), bf16))(w)


def _adamw(w, g, m, v):
    m = ADAM_B1 * m + (1.0 - ADAM_B1) * g
    v = ADAM_B2 * v + (1.0 - ADAM_B2) * jnp.square(g)
    m_hat = m / (1.0 - ADAM_B1 ** ADAM_STEP)
    v_hat = v / (1.0 - ADAM_B2 ** ADAM_STEP)
    delta = -ADAM_LR * (m_hat / (jnp.sqrt(v_hat) + ADAM_EPS) + ADAM_WD * w)
    return delta, m, v


def adam_sum(parts, w, m, v, name):
    _, r, c = parts.shape
    budget = 4 * 1024 * 1024
    tr = r
    for cand in (1024, 512, 256, 128, 64, 32, 16):
        if r % cand == 0 and N_DEV * cand * c * 4 <= budget:
            tr = cand
            break

    def body(p_ref, w_ref, m_ref, v_ref, g_ref, d_ref, m2_ref, v2_ref):
        g = p_ref[0].astype(f32)
        for k in range(1, N_DEV):
            g = g + p_ref[k].astype(f32)
        d, m2, v2 = _adamw(w_ref[...], g, m_ref[...], v_ref[...])
        g_ref[...] = g
        d_ref[...] = d
        m2_ref[...] = m2
        v2_ref[...] = v2

    blk = pl.BlockSpec((tr, c), lambda i: (i, 0))
    return _call(body, name, (r // tr,), [pl.BlockSpec((N_DEV, tr, c), lambda i: (0, i, 0)), blk, blk, blk], [blk] * 4,
                 [_sds((r, c), f32)] * 4)(parts, w, m, v)


def sum_parts(parts, name):
    _, r, c = parts.shape

    def body(p_ref, o_ref):
        g = p_ref[0]
        for k in range(1, N_DEV):
            g = g + p_ref[k]
        o_ref[...] = g

    return pl.pallas_call(body, name=name, out_shape=_sds((r, c), f32))(parts)


def adam_flat(w, g, m, v, name):
    def body(w_ref, g_ref, m_ref, v_ref, d_ref, m2_ref, v2_ref):
        d, m2, v2 = _adamw(w_ref[...], g_ref[...], m_ref[...], v_ref[...])
        d_ref[...] = d
        m2_ref[...] = m2
        v2_ref[...] = v2

    return pl.pallas_call(body, name=name, out_shape=[_sds(w.shape, f32)] * 3)(w, g, m, v)


def _pack(vecs, multiple):
    flat = jnp.concatenate([v.reshape(-1) for v in vecs])
    n = flat.shape[0]
    total = -(-n // multiple) * multiple
    return jnp.pad(flat, (0, total - n))


def _unpack(flat, shapes):
    out, pos = [], 0
    for s in shapes:
        n = 1
        for d in s:
            n *= d
        out.append(flat[pos:pos + n].reshape(s))
        pos += n
    return out


def _pad_lanes(v, width=TAIL):
    return jnp.pad(v, ((0, 0), (0, width - v.shape[1])))


def kernel(x, mem, positions, g_mix, w_in, g_qa, w_qb, g_kva, w_kvb, g_qn_nope, g_qn_pe, g_kn_nope, g_kn_pe, conv_qk, b_if, g_hnorm, p_a, p_b, w_out, g_cross, g_mem, wq_c, wk_c, wv_c, g_cq, g_ck, wo_c, g_ffn, w_up, conv_ffn, b_conv_ffn, w_down, loss_target, m_g_mix, m_w_in, m_g_qa, m_w_qb, m_g_kva, m_w_kvb, m_g_qn_nope, m_g_qn_pe, m_g_kn_nope, m_g_kn_pe, m_conv_qk, m_b_if, m_g_hnorm, m_p_a, m_p_b, m_w_out, m_g_cross, m_g_mem, m_wq_c, m_wk_c, m_wv_c, m_g_cq, m_g_ck, m_wo_c, m_g_ffn, m_w_up, m_conv_ffn, m_b_conv_ffn, m_w_down, v_g_mix, v_w_in, v_g_qa, v_w_qb, v_g_kva, v_w_kvb, v_g_qn_nope, v_g_qn_pe, v_g_kn_nope, v_g_kn_pe, v_conv_qk, v_b_if, v_g_hnorm, v_p_a, v_p_b, v_w_out, v_g_cross, v_g_mem, v_wq_c, v_wk_c, v_wv_c, v_g_cq, v_g_ck, v_wo_c, v_g_ffn, v_w_up, v_conv_ffn, v_b_conv_ffn, v_w_down):
    args = dict(locals())
    names = ['g_mix', 'w_in', 'g_qa', 'w_qb', 'g_kva', 'w_kvb', 'g_qn_nope', 'g_qn_pe', 'g_kn_nope', 'g_kn_pe', 'conv_qk', 'b_if',
             'g_hnorm', 'p_a', 'p_b', 'w_out', 'g_cross', 'g_mem', 'wq_c', 'wk_c', 'wv_c', 'g_cq', 'g_ck', 'wo_c', 'g_ffn', 'w_up',
             'conv_ffn', 'b_conv_ffn', 'w_down']
    big = ['w_in', 'w_qb', 'w_kvb', 'p_a', 'p_b', 'w_out', 'wq_c', 'wk_c', 'wv_c', 'wo_c', 'w_up', 'w_down']
    sharded_small = ['conv_qk', 'g_hnorm', 'conv_ffn']
    replicated = [n for n in names if n not in big and n not in sharded_small]

    t, d = x.shape[1], x.shape[2]
    x2d, tgt = x[0], loss_target[0]
    mem2d = mem[0]
    me = 4 * lax.axis_index("x") + 2 * lax.axis_index("y") + lax.axis_index("c")
    nc = t // CHUNK
    f2 = b_conv_ffn.shape[1]
    wmain = O_GA + 2 * d

    first = ['w_in', 'w_qb', 'w_kvb']
    behind_in = ['p_a', 'p_b', 'w_out', 'wq_c', 'wk_c', 'wv_c', 'wo_c']
    shards = {n: cast_bf16(args[n][0], "cast_" + n) for n in big}
    small_local = _pack([args[n] for n in sharded_small], 128).reshape(1, -1)
    gathered = _exchange([shards[n] for n in first] + [small_local], "comm_gather_first", scatter=False)
    gw = dict(zip(first, gathered[:-1]))
    small_all = gathered[-1]
    small_shapes = [args[n].shape for n in sharded_small]
    per_dev = [_unpack(small_all[k, 0], small_shapes) for k in range(N_DEV)]
    conv_qk_f = jnp.concatenate([p[0] for p in per_dev], axis=-1)[0]
    g_hnorm_f = jnp.concatenate([p[1] for p in per_dev], axis=-1)[0]
    conv_ffn_f = jnp.concatenate([p[2] for p in per_dev], axis=-1)[0]

    w_in_f = gw['w_in'].transpose(1, 0, 2).reshape(d, -1)
    c_kpe, c_q, c_i, c_o = O_Q, O_Q + ROPE, O_Q + ROPE + 2 * ML_QK + ML_V, O_Q + ROPE + 2 * ML_QK + ML_V + 2 * ML_HEADS
    w_main = jnp.concatenate([w_in_f[:, :c_kpe], w_in_f[:, c_q:c_i], w_in_f[:, c_o:]], axis=1)[None]
    w_tail = jnp.concatenate([w_in_f[:, c_kpe:c_q], w_in_f[:, c_i:c_o],
                              jnp.zeros((d, TAIL - ROPE - 2 * ML_HEADS), bf16)], axis=1)[None]
    assert w_main.shape[2] == wmain
    qb = gw['w_qb'].transpose(1, 0, 2).reshape(Q_LORA, MLA_HEADS, NOPE + ROPE)
    w_qb_p = jnp.concatenate([qb, jnp.zeros((Q_LORA, MLA_HEADS, HEAD_PAD - NOPE - ROPE), bf16)], axis=2).reshape(1, Q_LORA, -1)
    w_kvb3 = gw['w_kvb']

    inv_freq = ROPE_BASE ** (-jnp.arange(0, ROPE, 2, dtype=f32) / ROPE)
    inv_tile = _pad_lanes(jnp.concatenate([inv_freq, inv_freq])[None])
    cos, sin = rope_tables(positions.reshape(t, 1), inv_tile)
    gqp, gkp = _pad_lanes(g_qn_pe), _pad_lanes(g_kn_pe)
    b_tile = jnp.pad(b_if, ((0, 0), (T_I, TAIL - T_I - 2 * ML_HEADS)))

    u0 = rms_fwd(x2d, g_mix, "rms_mix")
    z_main, got = mm_nn(u0, w_main, f32, "mm_in_main", exchange=([shards[n] for n in behind_in], False))
    gw.update(zip(behind_in, got))
    p_a3, p_b3, w_out3 = (gw[n].reshape(1, -1, d) for n in ('p_a', 'p_b', 'w_out'))
    wq_c3, wk_c3, wv_c3 = (gw[n].reshape(1, d, -1) for n in ('wq_c', 'wk_c', 'wv_c'))
    wo_c3 = gw['wo_c']
    z_tail = mm_nn(u0, w_tail, f32, "mm_in_tail")
    qa_n, kv_n = lat_norm(z_main, g_qa, g_kva)
    q_raw = mm_nn(qa_n, w_qb_p, f32, "mm_qb")
    kv_raw = mm_nn(kv_n, w_kvb3, f32, "mm_kvb")
    qh, kh, vh = mla_prep(q_raw, kv_raw, z_tail, cos, sin, g_qn_nope, gqp, g_kn_nope, gkp)
    (o_a, o_ab, lse), (w_up3,) = mla_fwd(qh, kh, vh, exchange=([shards['w_up']], False))

    qk_act = qk_conv(z_main, conv_qk_f)
    gates = gate_act(z_tail, b_tile)

    def to_rows(cols):
        return cols.T.reshape(ML_HEADS, nc, 1, CHUNK)

    ig, fg = to_rows(gates[:, T_I:T_F]), to_rows(gates[:, T_F:T_F + ML_HEADS])
    h_ml, c_all, n_all, m_all = mlstm_fwd(qk_act, z_main, ig, fg)
    g_hn3 = g_hnorm_f.reshape(ML_HEADS, 1, ML_DV)
    y_b = mlstm_out(h_ml, z_main, g_hn3)

    ya = mm_nn(o_ab, p_a3, f32, "mm_pa")
    yb = mm_nn(y_b, p_b3, f32, "mm_pb")
    merged = merge_fwd(z_main, ya, yb)
    mo = mm_nn(merged, w_out3, f32, "mm_out")
    x1, uc = resid_rms(x2d, mo, g_cross, "resid_cross")
    mem_n = rms_fwd(mem2d, g_mem, "rms_mem")
    cq = mm_nn(uc, wq_c3, f32, "mm_cq")
    ck = mm_nn(mem_n, wk_c3, f32, "mm_ck")
    cv = mm_nn(mem_n, wv_c3, f32, "mm_cv")
    o_c = cross_fwd(cq, ck, cv, g_cq, g_ck)
    co = mm_nn(o_c, wo_c3, f32, "mm_oc")
    x2, u3 = resid_rms(x1, co, g_ffn, "resid_ffn")
    hup, (w_down_g,) = mm_nn(u3, w_up3, f32, "mm_up", exchange=([shards['w_down']], False))
    w_down3 = w_down_g.reshape(1, -1, d)
    gl = glu_fwd(hup, conv_ffn_f, b_conv_ffn)
    fo = mm_nn(gl, w_down3, f32, "mm_down")
    dx3, dx3_b, loss_acc = loss_head(x2, fo, tgt)

    grads, parts = {}, {}
    grads['w_down'] = mm_tn(gl, dx3_b, 1, "mm_d_wdown").reshape(N_DEV, -1, d)
    dgl = mm_nt(dx3_b, w_down3, f32, "mm_d_gl")
    (dh1, dh2, dcw1, dcw2, db1, db2), (parts['w_down'],) = glu_bwd(hup, conv_ffn_f, b_conv_ffn, dgl,
                                                                    exchange=([grads['w_down']], True))
    dhup, dconv_ffn, db_ffn = (jnp.concatenate(pair, axis=1) for pair in ((dh1, dh2), (dcw1, dcw2), (db1, db2)))
    grads['w_up'] = mm_tn(u3, dhup, N_DEV, "mm_d_wup")
    du3 = mm_nt(dhup, w_up3, f32, "mm_d_u3")
    dx2, dx2_b, dg_ffn = rms_bwd(x2, g_ffn, [du3], dx3, "rms_bwd_ffn", want_b16=True)
    grads['wo_c'] = mm_tn(o_c, dx2_b, N_DEV, "mm_d_woc")
    do_c = mm_nt(dx2_b, wo_c3, f32, "mm_d_oc")
    dcq, dck, dcv, dg_cq, dg_ck = cross_bwd(cq, ck, cv, g_cq, g_ck, do_c)
    grads['wq_c'] = mm_tn(uc, dcq, 1, "mm_d_wqc").reshape(N_DEV, -1, dcq.shape[1])
    grads['wk_c'] = mm_tn(mem_n, dck, 1, "mm_d_wkc").reshape(N_DEV, -1, dck.shape[1])
    grads['wv_c'] = mm_tn(mem_n, dcv, 1, "mm_d_wvc").reshape(N_DEV, -1, dcv.shape[1])
    duc = mm_nt(dcq, wq_c3, f32, "mm_d_uc")
    dmem_k = mm_nt(dck, wk_c3, f32, "mm_d_memk")
    dmem_v = mm_nt(dcv, wv_c3, f32, "mm_d_memv")
    dg_mem, = rms_bwd(mem2d, g_mem, [dmem_k, dmem_v], None, "rms_bwd_mem", want_dx=False)
    dx1, dx1_b, dg_cross = rms_bwd(x1, g_cross, [duc], dx2, "rms_bwd_cross", want_b16=True)
    grads['w_out'] = mm_tn(merged, dx1_b, 1, "mm_d_wout").reshape(N_DEV, -1, d)
    dmerged = mm_nt(dx1_b, w_out3, f32, "mm_d_merged")
    dga, dgb, dya, dyb = merge_bwd(z_main, ya, yb, dmerged)
    grads['p_a'] = mm_tn(o_ab, dya, 1, "mm_d_pa").reshape(N_DEV, -1, d)
    grads['p_b'] = mm_tn(y_b, dyb, 1, "mm_d_pb").reshape(N_DEV, -1, d)
    do_a = mm_nt(dya, p_a3, f32, "mm_d_oa")
    dy_b = mm_nt(dyb, p_b3, f32, "mm_d_yb")

    dh_ml, dzo, dg_hn = mlstm_out_bwd(h_ml, z_main, g_hn3, dy_b)
    (dq_act, dk_act, dzv, dig, dfg), got = mlstm_bwd(qk_act, z_main, ig, fg, c_all, n_all, m_all, dh_ml,
                                                     exchange=([grads[n] for n in behind_in], True))
    parts.update(zip(behind_in, got))
    dzqk, dconv_qk = qk_conv_bwd(z_main, conv_qk_f, dq_act, dk_act)

    delta = mla_delta(o_a, do_a)
    (dqh, dkh, dvh), (parts['w_up'],) = mla_bwd(qh, kh, vh, do_a, lse, delta.reshape(MLA_HEADS, 1, t),
                                                exchange=([grads['w_up']], True))
    dq_raw, dkv_raw, dzt_pe, dg_qn, dg_qp, dg_kn, dg_kp = mla_prep_bwd(
        q_raw, kv_raw, z_tail, cos, sin, g_qn_nope, gqp, g_kn_nope, gkp, dqh, dkh, dvh)
    d_wqb_p = mm_tn(qa_n, dq_raw, 1, "mm_d_wqb")[0].reshape(Q_LORA, MLA_HEADS, HEAD_PAD)[:, :, :NOPE + ROPE]
    grads['w_qb'] = d_wqb_p.reshape(Q_LORA, N_DEV, -1).transpose(1, 0, 2)
    grads['w_kvb'] = mm_tn(kv_n, dkv_raw, N_DEV, "mm_d_wkvb")
    dqa = mm_nt(dq_raw, w_qb_p, f32, "mm_d_qa")
    dkvn = mm_nt(dkv_raw, w_kvb3, f32, "mm_d_kvn")
    dz_lat, dg_qa, dg_kva = lat_norm_bwd(z_main, g_qa, g_kva, dqa, dkvn)

    def to_cols(rows):
        return rows.reshape(ML_HEADS, t).T

    dgate = jnp.pad(jnp.concatenate([to_cols(dig), to_cols(dfg)], axis=1), ((0, 0), (T_I, TAIL - T_I - 2 * ML_HEADS)))
    dz_tail, db_if = tail_bwd(z_tail, b_tile, dzt_pe, dgate)
    dz_main = jnp.concatenate([dz_lat, dzqk, dzv, dzo, dga, dgb], axis=1)
    d_wmain = mm_tn(u0, dz_main, 1, "mm_d_wmain")[0]
    d_wtail = mm_tn(u0, dz_tail, 1, "mm_d_wtail")[0]
    d_win = jnp.concatenate([d_wmain[:, :O_Q], d_wtail[:, :ROPE], d_wmain[:, O_Q:O_O], d_wtail[:, T_I:T_I + 2 * ML_HEADS],
                             d_wmain[:, O_O:]], axis=1)
    grads['w_in'] = d_win.reshape(d, N_DEV, -1).transpose(1, 0, 2)
    du0_a, got = mm_nt(dz_main, w_main, f32, "mm_d_u0_main", exchange=([grads[n] for n in first], True))
    parts.update(zip(first, got))
    du0_b = mm_nt(dz_tail, w_tail, f32, "mm_d_u0_tail")
    grad_x, dg_mix = rms_bwd(x2d, g_mix, [du0_a, du0_b], dx1, "rms_bwd_mix")

    out_g, out_d, out_m, out_v = {}, {}, {}, {}
    for n in big:
        shp = args[n].shape
        g, dl, m2, v2 = adam_sum(parts[n], args[n][0], args['m_' + n][0], args['v_' + n][0], "adam_" + n)
        out_g[n], out_d[n], out_m[n], out_v[n] = (a.reshape(shp) for a in (g, dl, m2, v2))

    small_full = {
        'g_mix': dg_mix[0], 'g_qa': dg_qa[0], 'g_kva': dg_kva[0], 'g_qn_nope': dg_qn[0], 'g_qn_pe': dg_qp[0, :ROPE],
        'g_kn_nope': dg_kn[0], 'g_kn_pe': dg_kp[0, :ROPE], 'conv_qk': dconv_qk, 'b_if': db_if[0, T_I:T_I + 2 * ML_HEADS],
        'g_hnorm': dg_hn[:, 0, :], 'g_cross': dg_cross[0], 'g_mem': dg_mem[0], 'g_cq': dg_cq[0], 'g_ck': dg_ck[0],
        'g_ffn': dg_ffn[0], 'conv_ffn': dconv_ffn, 'b_conv_ffn': db_ffn[0], 'loss': loss_acc[0, :1]}
    order = list(small_full)
    packed = _pack([small_full[n] for n in order], 8 * 128).reshape(1, -1)
    gathered_small, = _exchange([packed], "comm_gather_small", scatter=False)
    summed = sum_parts(gathered_small.reshape(N_DEV, -1, 128), "sum_small").reshape(-1)
    full_g = dict(zip(order, _unpack(summed, [small_full[n].shape for n in order])))
    loss = full_g['loss'][0]

    local_g = {}
    for n in replicated:
        local_g[n] = full_g[n].reshape(args[n].shape)
    for n in sharded_small:
        shp = args[n].shape
        full = full_g[n].reshape((1,) + full_g[n].shape)
        local_g[n] = lax.dynamic_slice_in_dim(full, me * shp[-1], shp[-1], axis=2)
    small = replicated + sharded_small
    dl_f, m_f, v_f = adam_flat(*[_pack([src[n] if pre == '' else args[pre + n] for n in small], 8 * 128).reshape(-1, 128)
                                 for pre, src in (('', args), ('', local_g), ('m_', None), ('v_', None))], "adam_small")
    shapes = [args[n].shape for n in small]
    for dst, flat in ((out_d, dl_f), (out_m, m_f), (out_v, v_f)):
        dst.update(zip(small, _unpack(flat.reshape(-1), shapes)))
    out_g.update(local_g)

    return (loss, grad_x[None], *[out_g[n] for n in names], *[out_d[n] for n in names],
            *[out_m[n] for n in names], *[out_v[n] for n in names])
```

```python
import functools

import jax
import jax.numpy as jnp
from jax import lax
from jax.experimental import pallas as pl
from jax.experimental.pallas import tpu as pltpu

f32 = jnp.float32
bf16 = jnp.bfloat16

N_DEV = 8
EPS = 1e-6
CHUNK = 64
CHUNK_SHIFT = 6
assert 1 << CHUNK_SHIFT == CHUNK
MLA_HEADS = 16
Q_LORA = 512
KV_LORA = 512
NOPE = 128
ROPE = 64
V_HEAD = 128
ROPE_BASE = 10000.0
HEAD_PAD = 256
ML_HEADS = 8
ML_DK = 128
ML_DV = 256
ML_CONV = 4
ML_QK = ML_HEADS * ML_DK
ML_V = ML_HEADS * ML_DV
CR_HEADS = 4
CR_HD = 128
FFN_CONV = 3
ADAM_LR = 0.001
ADAM_B1 = 0.9
ADAM_B2 = 0.999
ADAM_EPS = 1e-08
ADAM_WD = 0.01
ADAM_STEP = 10
O_QA, O_KV, O_Q, O_K = 0, Q_LORA, Q_LORA + KV_LORA, Q_LORA + KV_LORA + ML_QK
O_V = O_K + ML_QK
O_O = O_V + ML_V
O_GA = O_O + ML_V
TAIL = 128
T_I, T_F = ROPE, ROPE + ML_HEADS
VMEM_LIMIT_V7X = 48 * 1024 * 1024
MESH = pl.DeviceIdType.MESH


def _call(body, name, grid, in_specs, out_specs, out_shape, scratch=(), exchange=None):
    params = pltpu.CompilerParams(vmem_limit_bytes=VMEM_LIMIT_V7X)
    if exchange is None:
        return pl.pallas_call(body, name=name, grid=grid, in_specs=in_specs, out_specs=out_specs, out_shape=out_shape,
                              scratch_shapes=list(scratch), compiler_params=params)
    arrs, scatter = exchange
    single = not isinstance(out_specs, (list, tuple))
    o_specs = [out_specs] if single else list(out_specs)
    o_shape = [out_shape] if single else list(out_shape)
    n_in, n_out, n_sc, n = len(in_specs), len(o_specs), len(scratch), len(arrs)
    any_spec = pl.BlockSpec(memory_space=pl.ANY)

    def body_with_exchange(*refs):
        pos = [0]

        def take(k):
            pos[0] += k
            return refs[pos[0] - k:pos[0]]

        ins, ex_in, outs, ex_out, sc = take(n_in), take(n), take(n_out), take(n), take(n_sc)
        start, middle, wait = _exchange_ops(ex_in, ex_out, refs[pos[0]:], scatter)
        step, total = 0, 1
        for a in range(len(grid)):
            step = step * grid[a] + pl.program_id(a)
            total *= grid[a]
        pl.when(step == 0)(start)
        body(*ins, *outs, *sc)
        if middle is not None:
            pl.when(step == total // 2)(middle)
        pl.when(step == total - 1)(wait)

    call = pl.pallas_call(body_with_exchange, name="comm_" + name, grid=grid, in_specs=list(in_specs) + [any_spec] * n,
                          out_specs=o_specs + [any_spec] * n, out_shape=o_shape + _exchange_shapes(arrs, scatter),
                          scratch_shapes=list(scratch) + _exchange_sems(n), compiler_params=params)

    def run(*operands):
        res = call(*operands, *arrs)
        return (res[0] if single else list(res[:n_out])), list(res[n_out:])

    return run


def _tile(n, cands):
    for c in cands:
        if n % c == 0:
            return c
    return n


def _sds(shape, dtype):
    return jax.ShapeDtypeStruct(tuple(shape), dtype)


def _bdot(a, b, ca, cb):
    return lax.dot_general(a.astype(bf16), b.astype(bf16), (((ca,), (cb,)), ((), ())), preferred_element_type=f32)


_BIG = (1024, 512, 256, 128)


def _col_tile(nb):
    return nb if nb <= 1536 else _tile(nb, _BIG)


_DEEP = (2048, 1024, 512, 256, 128)


def _mm_call(name, grid, in_specs, out_spec, out_shape, tile, nk, ca, cb, exchange, operands):
    def dot(a_ref, w_ref):
        return _bdot(a_ref[...], w_ref[0] if len(w_ref.shape) == 3 else w_ref[...], ca, cb)

    def store(o_ref, val):
        if len(o_ref.shape) == 3:
            o_ref[0] = val.astype(o_ref.dtype)
        else:
            o_ref[...] = val.astype(o_ref.dtype)

    if nk == 1:
        def body(a_ref, w_ref, o_ref):
            store(o_ref, dot(a_ref, w_ref))

        scratch = []
    else:
        def body(a_ref, w_ref, o_ref, acc):
            kk = pl.program_id(2)

            @pl.when(kk == 0)
            def _():
                acc[...] = jnp.zeros_like(acc)

            acc[...] += dot(a_ref, w_ref)

            @pl.when(kk == nk - 1)
            def _():
                store(o_ref, acc[...])

        scratch = [pltpu.VMEM(tile, f32)]
    return _call(body, name, grid, in_specs, out_spec, out_shape, scratch, exchange=exchange)(*operands)


def mm_nn(a, w3, out_dtype, name, exchange=None):
    m, k = a.shape
    nblk, k2, nb = w3.shape
    assert k == k2
    tm, tk, tn = _tile(m, _BIG), _tile(k, _DEEP), _col_tile(nb)
    per, nk = nb // tn, k // tk
    return _mm_call(name, (m // tm, nblk * per, nk),
                    [pl.BlockSpec((tm, tk), lambda i, j, kk: (i, kk)),
                     pl.BlockSpec((1, tk, tn), lambda i, j, kk: (j // per, kk, j % per))],
                    pl.BlockSpec((tm, tn), lambda i, j, kk: (i, j)), _sds((m, nblk * nb), out_dtype),
                    (tm, tn), nk, 1, 0, exchange, (a, w3))


def mm_nt(a, w3, out_dtype, name, exchange=None):
    m, n = a.shape
    nblk, k, nb = w3.shape
    assert n == nblk * nb
    tm, tn = _tile(m, _BIG), _tile(k, _BIG)
    tc = nb if nb <= 1536 else _tile(nb, _DEEP)
    per = nb // tc
    nk = nblk * per
    return _mm_call(name, (m // tm, k // tn, nk),
                    [pl.BlockSpec((tm, tc), lambda i, j, kk: (i, kk)),
                     pl.BlockSpec((1, tn, tc), lambda i, j, kk: (kk // per, j, kk % per))],
                    pl.BlockSpec((tm, tn), lambda i, j, kk: (i, j)), _sds((m, k), out_dtype),
                    (tm, tn), nk, 1, 1, exchange, (a, w3))


def mm_tn(a, b, nblk, name):
    r, m = a.shape
    r2, n = b.shape
    assert r == r2 and n % nblk == 0
    nb = n // nblk
    tm, tk, tn = _tile(m, _BIG), _tile(r, _DEEP), _col_tile(nb)
    per, nk = nb // tn, r // tk
    return _mm_call(name, (m // tm, nblk * per, nk),
                    [pl.BlockSpec((tk, tm), lambda i, j, kk: (kk, i)),
                     pl.BlockSpec((tk, tn), lambda i, j, kk: (kk, j))],
                    pl.BlockSpec((1, tm, tn), lambda i, j, kk: (j // per, i, j % per)), _sds((nblk, m, nb), bf16),
                    (tm, tn), nk, 0, 0, None, (a, b))


def _rms(x, g):
    return x * lax.rsqrt(jnp.mean(x * x, axis=-1, keepdims=True) + EPS) * g


def _rms_pad(x, g, width):
    return x * lax.rsqrt(jnp.sum(x * x, axis=-1, keepdims=True) / width + EPS) * g


def _first(*ids):
    ok = ids[0] == 0
    for i in ids[1:]:
        ok = jnp.logical_and(ok, i == 0)
    return ok


def _acc_row(ref, val, first):
    @pl.when(first)
    def _():
        ref[...] = jnp.zeros_like(ref)

    ref[0:1, :] += val


def rms_fwd(x, g, name):
    r, w = x.shape
    tm = _tile(r, (256, 128, 64, 32, 16, 8))

    def body(x_ref, g_ref, o_ref):
        o_ref[...] = _rms(x_ref[...], g_ref[...]).astype(bf16)

    return _call(body, name, (r // tm,), [pl.BlockSpec((tm, w), lambda i: (i, 0)), pl.BlockSpec((1, w), lambda i: (0, 0))],
                 pl.BlockSpec((tm, w), lambda i: (i, 0)), _sds((r, w), bf16))(x, g)


def resid_rms(xa, xb, g, name):
    r, w = xa.shape
    tm = _tile(r, (256, 128, 64, 32, 16, 8))

    def body(a_ref, b_ref, g_ref, s_ref, u_ref):
        xs = a_ref[...] + b_ref[...]
        s_ref[...] = xs
        u_ref[...] = _rms(xs, g_ref[...]).astype(bf16)

    row = pl.BlockSpec((tm, w), lambda i: (i, 0))
    return _call(body, name, (r // tm,), [row, row, pl.BlockSpec((1, w), lambda i: (0, 0))], [row, row],
                 [_sds((r, w), f32), _sds((r, w), bf16)])(xa, xb, g)


def rms_bwd(x, g, dys, dres, name, want_dx=True, want_b16=False):
    r, w = x.shape
    tm = _tile(r, (256, 128, 64, 32, 16, 8))
    nd = len(dys)

    def body(*refs):
        x_ref, g_ref = refs[0], refs[1]
        dy = refs[2][...]
        for j in range(1, nd):
            dy = dy + refs[2 + j][...]
        pos = 2 + nd
        _, vjp = jax.vjp(_rms, x_ref[...], g_ref[...])
        dx, dg = vjp(dy)
        if dres is not None:
            dx = dx + refs[pos][...]
            pos += 1
        if want_dx:
            refs[pos][...] = dx
            pos += 1
        if want_b16:
            refs[pos][...] = dx.astype(bf16)
            pos += 1
        _acc_row(refs[pos], dg, pl.program_id(0) == 0)

    row = pl.BlockSpec((tm, w), lambda i: (i, 0))
    ins = [x, g] + list(dys) + ([dres] if dres is not None else [])
    in_specs = [row, pl.BlockSpec((1, w), lambda i: (0, 0))] + [row] * (nd + (dres is not None))
    out_specs = [row] * (want_dx + want_b16) + [pl.BlockSpec((8, w), lambda i: (0, 0))]
    out_shape = ([_sds((r, w), f32)] if want_dx else []) + ([_sds((r, w), bf16)] if want_b16 else []) + [_sds((8, w), f32)]
    return _call(body, name, (r // tm,), in_specs, out_specs, out_shape)(*ins)


def lat_norm(z_main, g_qa, g_kva):
    t = z_main.shape[0]
    tm = _tile(t, (512, 256, 128, 64))

    def body(z_ref, gq_ref, gk_ref, q_ref, k_ref):
        q_ref[...] = _rms(z_ref[:, :Q_LORA], gq_ref[...]).astype(bf16)
        k_ref[...] = _rms(z_ref[:, Q_LORA:], gk_ref[...]).astype(bf16)

    return _call(body, "lat_norm", (t // tm,),
                 [pl.BlockSpec((tm, Q_LORA + KV_LORA), lambda i: (i, 0)), pl.BlockSpec((1, Q_LORA), lambda i: (0, 0)),
                  pl.BlockSpec((1, KV_LORA), lambda i: (0, 0))],
                 [pl.BlockSpec((tm, Q_LORA), lambda i: (i, 0)), pl.BlockSpec((tm, KV_LORA), lambda i: (i, 0))],
                 [_sds((t, Q_LORA), bf16), _sds((t, KV_LORA), bf16)])(z_main, g_qa, g_kva)


def lat_norm_bwd(z_main, g_qa, g_kva, dqa, dkv):
    t = z_main.shape[0]
    tm = _tile(t, (512, 256, 128, 64))

    def body(z_ref, gq_ref, gk_ref, dq_ref, dk_ref, dz_ref, dgq_ref, dgk_ref):
        first = pl.program_id(0) == 0
        _, vq = jax.vjp(_rms, z_ref[:, :Q_LORA], gq_ref[...])
        dx, dg = vq(dq_ref[...])
        dz_ref[:, :Q_LORA] = dx.astype(bf16)
        _acc_row(dgq_ref, dg, first)
        _, vk = jax.vjp(_rms, z_ref[:, Q_LORA:], gk_ref[...])
        dx, dg = vk(dk_ref[...])
        dz_ref[:, Q_LORA:] = dx.astype(bf16)
        _acc_row(dgk_ref, dg, first)

    return _call(body, "lat_norm_bwd", (t // tm,),
                 [pl.BlockSpec((tm, Q_LORA + KV_LORA), lambda i: (i, 0)), pl.BlockSpec((1, Q_LORA), lambda i: (0, 0)),
                  pl.BlockSpec((1, KV_LORA), lambda i: (0, 0)), pl.BlockSpec((tm, Q_LORA), lambda i: (i, 0)),
                  pl.BlockSpec((tm, KV_LORA), lambda i: (i, 0))],
                 [pl.BlockSpec((tm, Q_LORA + KV_LORA), lambda i: (i, 0)), pl.BlockSpec((8, Q_LORA), lambda i: (0, 0)),
                  pl.BlockSpec((8, KV_LORA), lambda i: (0, 0))],
                 [_sds((t, Q_LORA + KV_LORA), bf16), _sds((8, Q_LORA), f32), _sds((8, KV_LORA), f32)])(z_main, g_qa, g_kva, dqa, dkv)


def rope_tables(pos_col, inv_freq):
    t = pos_col.shape[0]
    tm = _tile(t, (512, 256, 128, 64))

    def body(p_ref, f_ref, c_ref, s_ref):
        ang = p_ref[...].astype(f32) * f_ref[...]
        lane = lax.broadcasted_iota(jnp.int32, ang.shape, 1)
        c_ref[...] = jnp.where(lane < ROPE, jnp.cos(ang), 0.0)
        sn = jnp.sin(ang)
        s_ref[...] = jnp.where(lane < ROPE // 2, -sn, jnp.where(lane < ROPE, sn, 0.0))

    return _call(body, "rope_tables", (t // tm,),
                 [pl.BlockSpec((tm, 1), lambda i: (i, 0)), pl.BlockSpec((1, TAIL), lambda i: (0, 0))],
                 [pl.BlockSpec((tm, TAIL), lambda i: (i, 0))] * 2, [_sds((t, TAIL), f32)] * 2)(pos_col, inv_freq)


def _swap_halves(n):
    lane = lax.broadcasted_iota(jnp.int32, n.shape, 1)
    return jnp.where(lane < ROPE // 2, pltpu.roll(n, TAIL - ROPE // 2, 1), pltpu.roll(n, ROPE // 2, 1))


def _rope(n, c, s):
    return n * c + _swap_halves(n) * s


def _rope_t(d, c, s):
    return d * c + _swap_halves(d * s)


def _prep_specs(tm):
    head = pl.BlockSpec((tm, HEAD_PAD), lambda i, h: (i, h))
    row = pl.BlockSpec((tm, TAIL), lambda i, h: (i, 0))
    gain = pl.BlockSpec((1, TAIL), lambda i, h: (0, 0))
    return head, row, gain


def _pe_in(zt):
    lane = lax.broadcasted_iota(jnp.int32, zt.shape, 1)
    return jnp.where(lane < ROPE, zt, 0.0)


def mla_prep(q_raw, kv_raw, z_tail, cos, sin, gqn, gqp, gkn, gkp):
    t = q_raw.shape[0]
    tm = _tile(t, (1024, 512, 256, 128, 64))

    def body(q_ref, kv_ref, zt_ref, c_ref, s_ref, gqn_ref, gqp_ref, gkn_ref, gkp_ref, qh_ref, kh_ref, vh_ref):
        c, s = c_ref[...], s_ref[...]
        qh_ref[:, :NOPE] = _rms(q_ref[:, :NOPE], gqn_ref[...]).astype(bf16)
        qh_ref[:, NOPE:] = _rope(_rms_pad(q_ref[:, NOPE:], gqp_ref[...], ROPE), c, s).astype(bf16)
        kh_ref[:, :NOPE] = _rms(kv_ref[:, :NOPE], gkn_ref[...]).astype(bf16)
        kh_ref[:, NOPE:] = _rope(_rms_pad(_pe_in(zt_ref[...]), gkp_ref[...], ROPE), c, s).astype(bf16)
        vh_ref[...] = kv_ref[:, NOPE:].astype(bf16)

    head, row, gain = _prep_specs(tm)
    return _call(body, "mla_prep", (t // tm, MLA_HEADS), [head, head, row, row, row, gain, gain, gain, gain],
                 [head, head, pl.BlockSpec((tm, V_HEAD), lambda i, h: (i, h))],
                 [_sds((t, MLA_HEADS * HEAD_PAD), bf16), _sds((t, MLA_HEADS * HEAD_PAD), bf16), _sds((t, MLA_HEADS * V_HEAD), bf16)],
                 )(q_raw, kv_raw, z_tail, cos, sin, gqn, gqp, gkn, gkp)


def mla_prep_bwd(q_raw, kv_raw, z_tail, cos, sin, gqn, gqp, gkn, gkp, dqh, dkh, dvh):
    t = q_raw.shape[0]
    tm = _tile(t, (1024, 512, 256, 128, 64))
    pad_norm = functools.partial(_rms_pad, width=ROPE)

    def body(q_ref, kv_ref, zt_ref, c_ref, s_ref, gqn_ref, gqp_ref, gkn_ref, gkp_ref, dqh_ref, dkh_ref, dvh_ref,
             dq_ref, dkv_ref, dzt_ref, dgqn_ref, dgqp_ref, dgkn_ref, dgkp_ref):
        i, h = pl.program_id(0), pl.program_id(1)
        first = _first(i, h)
        c, s = c_ref[...], s_ref[...]
        _, v1 = jax.vjp(_rms, q_ref[:, :NOPE], gqn_ref[...])
        dx, dg = v1(dqh_ref[:, :NOPE])
        dq_ref[:, :NOPE] = dx.astype(bf16)
        _acc_row(dgqn_ref, dg, first)
        _, v2 = jax.vjp(pad_norm, q_ref[:, NOPE:], gqp_ref[...])
        dx, dg = v2(_rope_t(dqh_ref[:, NOPE:], c, s))
        dq_ref[:, NOPE:] = dx.astype(bf16)
        _acc_row(dgqp_ref, dg, first)
        _, v3 = jax.vjp(_rms, kv_ref[:, :NOPE], gkn_ref[...])
        dx, dg = v3(dkh_ref[:, :NOPE])
        dkv_ref[:, :NOPE] = dx.astype(bf16)
        _acc_row(dgkn_ref, dg, first)
        dkv_ref[:, NOPE:] = dvh_ref[...].astype(bf16)
        _, v4 = jax.vjp(pad_norm, _pe_in(zt_ref[...]), gkp_ref[...])
        dx, dg = v4(_rope_t(dkh_ref[:, NOPE:], c, s))
        _acc_row(dgkp_ref, dg, first)

        @pl.when(h == 0)
        def _():
            dzt_ref[...] = jnp.zeros_like(dzt_ref)

        dzt_ref[...] += dx

    head, row, gain = _prep_specs(tm)
    acc = pl.BlockSpec((8, TAIL), lambda i, h: (0, 0))
    vspec = pl.BlockSpec((tm, V_HEAD), lambda i, h: (i, h))
    return _call(body, "mla_prep_bwd", (t // tm, MLA_HEADS),
                 [head, head, row, row, row, gain, gain, gain, gain, head, head, vspec],
                 [head, head, row, acc, acc, acc, acc],
                 [_sds((t, MLA_HEADS * HEAD_PAD), bf16), _sds((t, MLA_HEADS * HEAD_PAD), bf16), _sds((t, TAIL), f32)]
                 + [_sds((8, TAIL), f32)] * 4)(q_raw, kv_raw, z_tail, cos, sin, gqn, gqp, gkn, gkp, dqh, dkh, dvh)


ATT_BLOCK = 512
NEG = -1e30
ATT_SCALE = (NOPE + ROPE) ** -0.5
ATT_HEADS = 2


def _chunk_visible(shape, key_axis):
    kc = lax.broadcasted_iota(jnp.int32, shape, key_axis) >> CHUNK_SHIFT
    qc = lax.broadcasted_iota(jnp.int32, shape, 1 - key_axis) >> CHUNK_SHIFT
    return kc <= qc


def mla_fwd(qh, kh, vh, exchange=None):
    t = qh.shape[0]
    tb = min(ATT_BLOCK, t)
    nb = t // tb

    hp = ATT_HEADS

    def body(q_ref, k_ref, v_ref, o_ref, ob_ref, lse_ref, m_s, l_s, acc):
        qi, ki = pl.program_id(1), pl.program_id(2)

        @pl.when(ki == 0)
        def _():
            m_s[...] = jnp.full_like(m_s, NEG)
            l_s[...] = jnp.zeros_like(l_s)
            acc[...] = jnp.zeros_like(acc)

        def step(diagonal):
            new = []
            for j in range(hp):
                q, k = q_ref[:, j * HEAD_PAD:(j + 1) * HEAD_PAD], k_ref[:, j * HEAD_PAD:(j + 1) * HEAD_PAD]
                s = _bdot(k, q, 1, 1) * ATT_SCALE
                if diagonal:
                    s = jnp.where(_chunk_visible(s.shape, 0), s, -jnp.inf)
                m_old = m_s[j]
                m_new = jnp.maximum(m_old, jnp.max(s, axis=0, keepdims=True))
                p = jnp.exp(s - m_new)
                alpha = jnp.exp(m_old - m_new)
                l_new = alpha * l_s[j] + jnp.sum(p, axis=0, keepdims=True)
                acc_new = alpha * acc[j] + _bdot(v_ref[:, j * V_HEAD:(j + 1) * V_HEAD], p, 0, 0)
                new.append((m_new, l_new, acc_new))
            for j, (m_new, l_new, acc_new) in enumerate(new):
                m_s[j] = m_new
                l_s[j] = l_new
                acc[j] = acc_new
            return new

        @pl.when(ki < qi)
        def _():
            step(False)

        @pl.when(ki == qi)
        def _():
            for j, (m_new, l_new, acc_new) in enumerate(step(True)):
                o = (acc_new / l_new).T
                o_ref[:, j * V_HEAD:(j + 1) * V_HEAD] = o
                ob_ref[:, j * V_HEAD:(j + 1) * V_HEAD] = o.astype(bf16)
                lse_ref[j] = m_new + jnp.log(l_new)

    kv = lambda g, qi, ki: (jnp.minimum(ki, qi), g)
    o_spec = pl.BlockSpec((tb, hp * V_HEAD), lambda g, qi, ki: (qi, g))
    return _call(body, "mla_fwd", (MLA_HEADS // hp, nb, nb),
                 [pl.BlockSpec((tb, hp * HEAD_PAD), lambda g, qi, ki: (qi, g)), pl.BlockSpec((tb, hp * HEAD_PAD), kv),
                  pl.BlockSpec((tb, hp * V_HEAD), kv)],
                 [o_spec, o_spec, pl.BlockSpec((hp, 1, tb), lambda g, qi, ki: (g, 0, qi))],
                 [_sds((t, MLA_HEADS * V_HEAD), f32), _sds((t, MLA_HEADS * V_HEAD), bf16), _sds((MLA_HEADS, 1, t), f32)],
                 [pltpu.VMEM((hp, 1, tb), f32), pltpu.VMEM((hp, 1, tb), f32), pltpu.VMEM((hp, V_HEAD, tb), f32)],
                 exchange=exchange)(qh, kh, vh)


def mla_delta(o, do):
    t = o.shape[0]
    tm = _tile(t, (512, 256, 128, 64))

    def body(o_ref, do_ref, d_ref):
        for h in range(MLA_HEADS):
            cols = slice(h * V_HEAD, (h + 1) * V_HEAD)
            d_ref[h] = jnp.sum(o_ref[:, cols] * do_ref[:, cols], axis=1, keepdims=True)

    blk = pl.BlockSpec((tm, MLA_HEADS * V_HEAD), lambda i: (i, 0))
    return _call(body, "mla_delta", (t // tm,), [blk, blk], pl.BlockSpec((MLA_HEADS, tm, 1), lambda i: (0, i, 0)),
                 _sds((MLA_HEADS, t, 1), f32))(o, do)


def mla_bwd(qh, kh, vh, do, lse_row, delta_row, exchange=None):
    t = qh.shape[0]
    tb = min(ATT_BLOCK, t)
    nb = t // tb

    hp = ATT_HEADS

    def body(q_ref, k_ref, v_ref, do_ref, lse_ref, dl_ref, dq_ref, dk_ref, dv_ref, dk_acc, dv_acc):
        ki, qi = pl.program_id(1), pl.program_id(2)

        @pl.when(jnp.logical_and(ki == 0, qi == 0))
        def _():
            dq_ref[...] = jnp.zeros_like(dq_ref)

        @pl.when(qi == 0)
        def _():
            dk_acc[...] = jnp.zeros_like(dk_acc)
            dv_acc[...] = jnp.zeros_like(dv_acc)

        def step(diagonal):
            rows = pl.ds(pl.multiple_of(qi * tb, tb), tb)
            new = []
            for j in range(hp):
                qc, vc = slice(j * HEAD_PAD, (j + 1) * HEAD_PAD), slice(j * V_HEAD, (j + 1) * V_HEAD)
                q, k, do_b = q_ref[:, qc], k_ref[:, qc], do_ref[:, vc]
                s = _bdot(k, q, 1, 1) * ATT_SCALE
                if diagonal:
                    s = jnp.where(_chunk_visible(s.shape, 0), s, -jnp.inf)
                p = jnp.exp(s - lse_ref[j])
                dp = _bdot(v_ref[:, vc], do_b, 1, 1)
                ds = p * (dp - dl_ref[j]) * ATT_SCALE
                new.append((dv_acc[:, vc] + _bdot(p, do_b, 1, 0), dk_acc[:, qc] + _bdot(ds, q, 1, 0),
                            dq_ref[rows, qc] + _bdot(ds, k, 0, 0)))
            for j, (dv, dk, dq) in enumerate(new):
                dv_acc[:, j * V_HEAD:(j + 1) * V_HEAD] = dv
                dk_acc[:, j * HEAD_PAD:(j + 1) * HEAD_PAD] = dk
                dq_ref[rows, j * HEAD_PAD:(j + 1) * HEAD_PAD] = dq

        @pl.when(qi > ki)
        def _():
            step(False)

        @pl.when(qi == ki)
        def _():
            step(True)

        @pl.when(qi == nb - 1)
        def _():
            dk_ref[...] = dk_acc[...]
            dv_ref[...] = dv_acc[...]

    qs = lambda g, ki, qi: (jnp.maximum(qi, ki), g)
    ks = lambda g, ki, qi: (ki, g)
    vec = pl.BlockSpec((hp, 1, tb), lambda g, ki, qi: (g, 0, jnp.maximum(qi, ki)))
    return _call(body, "mla_bwd", (MLA_HEADS // hp, nb, nb),
                 [pl.BlockSpec((tb, hp * HEAD_PAD), qs), pl.BlockSpec((tb, hp * HEAD_PAD), ks), pl.BlockSpec((tb, hp * V_HEAD), ks),
                  pl.BlockSpec((tb, hp * V_HEAD), qs), vec, vec],
                 [pl.BlockSpec((t, hp * HEAD_PAD), lambda g, ki, qi: (0, g)), pl.BlockSpec((tb, hp * HEAD_PAD), ks),
                  pl.BlockSpec((tb, hp * V_HEAD), ks)],
                 [_sds((t, MLA_HEADS * HEAD_PAD), f32), _sds((t, MLA_HEADS * HEAD_PAD), f32), _sds((t, MLA_HEADS * V_HEAD), f32)],
                 [pltpu.VMEM((tb, hp * HEAD_PAD), f32), pltpu.VMEM((tb, hp * V_HEAD), f32)], exchange=exchange)(
        qh, kh, vh, do, lse_row, delta_row)


PAD = 8


def _conv_taps(pad_ref, w, width, t):
    y = pad_ref[PAD - width + 1:PAD - width + 1 + t, :] * w[0:1, :]
    for j in range(1, width):
        y = y + pad_ref[PAD - width + 1 + j:PAD - width + 1 + j + t, :] * w[j:j + 1, :]
    return y


def _conv_bwd(xpad_ref, dpad_ref, w, da, width, t):
    dpad_ref[0:t, :] = da
    dpad_ref[t:t + PAD, :] = jnp.zeros((PAD, da.shape[1]), f32)
    dx = dpad_ref[width - 1:width - 1 + t, :] * w[0:1, :]
    for j in range(1, width):
        dx = dx + dpad_ref[width - 1 - j:width - 1 - j + t, :] * w[j:j + 1, :]
    dws = [jnp.sum(da * xpad_ref[PAD - width + 1 + j:PAD - width + 1 + j + t, :], axis=0, keepdims=True) for j in range(width)]
    return dx, dws


def _load_pad(pad_ref, x, t):
    pad_ref[0:PAD, :] = jnp.zeros((PAD, x.shape[1]), f32)
    pad_ref[PAD:PAD + t, :] = x


assert ML_DK == 128


def qk_conv(z_main, conv_qk):
    t = z_main.shape[0]
    base = O_Q // ML_DK

    def body(z_ref, w_ref, o_ref, pad):
        _load_pad(pad, z_ref[...], t)
        a = _conv_taps(pad, w_ref[...], ML_CONV, t)
        sc = jnp.where(pl.program_id(0) < ML_HEADS, ML_DK ** -0.5, 1.0)
        o_ref[0] = jax.nn.silu(a) * sc

    return _call(body, "qk_conv", (2 * ML_HEADS,),
                 [pl.BlockSpec((t, ML_DK), lambda j: (0, base + j)), pl.BlockSpec((ML_CONV, ML_DK), lambda j: (0, j))],
                 pl.BlockSpec((1, t, ML_DK), lambda j: (j, 0, 0)), _sds((2 * ML_HEADS, t, ML_DK), f32),
                 [pltpu.VMEM((t + PAD, ML_DK), f32)])(z_main, conv_qk)


def qk_conv_bwd(z_main, conv_qk, dq, dk):
    t = z_main.shape[0]
    base = O_Q // ML_DK

    def body(z_ref, w_ref, dq_ref, dk_ref, dz_ref, dw_ref, pad, dpad):
        _load_pad(pad, z_ref[...], t)
        w = w_ref[...]
        a = _conv_taps(pad, w, ML_CONV, t)
        is_q = pl.program_id(0) < ML_HEADS
        d = jnp.where(is_q, dq_ref[0] * (ML_DK ** -0.5), dk_ref[0])
        _, vjp = jax.vjp(jax.nn.silu, a)
        da, = vjp(d)
        dx, dws = _conv_bwd(pad, dpad, w, da, ML_CONV, t)
        dz_ref[...] = dx.astype(bf16)
        for j in range(ML_CONV):
            dw_ref[j:j + 1, :] = dws[j]

    head = lambda pick: pl.BlockSpec((1, t, ML_DK), lambda j: (pick(j), 0, 0))
    return _call(body, "qk_conv_bwd", (2 * ML_HEADS,),
                 [pl.BlockSpec((t, ML_DK), lambda j: (0, base + j)), pl.BlockSpec((ML_CONV, ML_DK), lambda j: (0, j)),
                  head(lambda j: jnp.minimum(j, ML_HEADS - 1)), head(lambda j: jnp.maximum(j - ML_HEADS, 0))],
                 [pl.BlockSpec((t, ML_DK), lambda j: (0, j)), pl.BlockSpec((ML_CONV, ML_DK), lambda j: (0, j))],
                 [_sds((t, 2 * ML_QK), bf16), _sds((ML_CONV, 2 * ML_QK), f32)],
                 [pltpu.VMEM((t + PAD, ML_DK), f32), pltpu.VMEM((t + PAD, ML_DK), f32)])(z_main, conv_qk, dq, dk)


def glu_fwd(hup, conv_w, bias):
    t, f2 = hup.shape
    nf = f2 // 2 // 128

    def body(h1_ref, h2_ref, w1_ref, w2_ref, b1_ref, b2_ref, o_ref, pad):
        _load_pad(pad, h1_ref[...], t)
        a1 = _conv_taps(pad, w1_ref[...], FFN_CONV, t) + b1_ref[...]
        _load_pad(pad, h2_ref[...], t)
        a2 = _conv_taps(pad, w2_ref[...], FFN_CONV, t) + b2_ref[...]
        o_ref[...] = (jax.nn.silu(a1) * a2).astype(bf16)

    col = lambda off: pl.BlockSpec((t, 128), lambda j: (0, j + off))
    wsp = lambda off: pl.BlockSpec((FFN_CONV, 128), lambda j: (0, j + off))
    bsp = lambda off: pl.BlockSpec((1, 128), lambda j: (0, j + off))
    return _call(body, "glu_fwd", (nf,), [col(0), col(nf), wsp(0), wsp(nf), bsp(0), bsp(nf)], col(0), _sds((t, f2 // 2), bf16),
                 [pltpu.VMEM((t + PAD, 128), f32)])(hup, hup, conv_w, conv_w, bias, bias)


def glu_bwd(hup, conv_w, bias, dg, exchange=None):
    t, f2 = hup.shape
    f = f2 // 2
    nf = f // 128

    def body(h1_ref, h2_ref, w1_ref, w2_ref, b1_ref, b2_ref, dg_ref, dh1_ref, dh2_ref, dw1_ref, dw2_ref, db1_ref, db2_ref,
             pad1, pad2, dpad):
        _load_pad(pad1, h1_ref[...], t)
        _load_pad(pad2, h2_ref[...], t)
        w1, w2 = w1_ref[...], w2_ref[...]
        a1 = _conv_taps(pad1, w1, FFN_CONV, t) + b1_ref[...]
        a2 = _conv_taps(pad2, w2, FFN_CONV, t) + b2_ref[...]
        d = dg_ref[...]
        _, vjp = jax.vjp(jax.nn.silu, a1)
        da1, = vjp(d * a2)
        da2 = d * jax.nn.silu(a1)
        for da, pad, w, dh_ref, dw_ref, db_ref in ((da1, pad1, w1, dh1_ref, dw1_ref, db1_ref), (da2, pad2, w2, dh2_ref, dw2_ref, db2_ref)):
            dx, dws = _conv_bwd(pad, dpad, w, da, FFN_CONV, t)
            dh_ref[...] = dx.astype(bf16)
            for j in range(FFN_CONV):
                dw_ref[j:j + 1, :] = dws[j]
            db_ref[...] = jnp.sum(da, axis=0, keepdims=True)

    col = lambda off: pl.BlockSpec((t, 128), lambda j: (0, j + off))
    wsp = lambda off: pl.BlockSpec((FFN_CONV, 128), lambda j: (0, j + off))
    bsp = lambda off: pl.BlockSpec((1, 128), lambda j: (0, j + off))
    return _call(body, "glu_bwd", (nf,), [col(0), col(nf), wsp(0), wsp(nf), bsp(0), bsp(nf), col(0)],
                 [col(0), col(0), wsp(0), wsp(0), bsp(0), bsp(0)],
                 [_sds((t, f), bf16)] * 2 + [_sds((FFN_CONV, f), f32)] * 2 + [_sds((1, f), f32)] * 2,
                 [pltpu.VMEM((t + PAD, 128), f32)] * 3, exchange=exchange)(hup, hup, conv_w, conv_w, bias, bias, dg)


def gate_act(z_tail, b_tile):
    t = z_tail.shape[0]
    tm = _tile(t, (512, 256, 128, 64))

    def body(z_ref, b_ref, o_ref):
        x = z_ref[...] + b_ref[...]
        lane = lax.broadcasted_iota(jnp.int32, x.shape, 1)
        o_ref[...] = jnp.where(lane < T_F, x, jax.nn.log_sigmoid(x))

    row = pl.BlockSpec((tm, TAIL), lambda i: (i, 0))
    return _call(body, "gate_act", (t // tm,), [row, pl.BlockSpec((1, TAIL), lambda i: (0, 0))], row, _sds((t, TAIL), f32))(z_tail, b_tile)


def tail_bwd(z_tail, b_tile, dzt_pe, dgate):
    t = z_tail.shape[0]
    tm = _tile(t, (512, 256, 128, 64))

    def body(z_ref, b_ref, dpe_ref, dg_ref, dz_ref, db_ref):
        x = z_ref[...] + b_ref[...]
        lane = lax.broadcasted_iota(jnp.int32, x.shape, 1)
        _, vjp = jax.vjp(jax.nn.log_sigmoid, x)
        df, = vjp(dg_ref[...])
        dgates = jnp.where(lane < T_F, dg_ref[...], df)
        dgates = jnp.where(jnp.logical_and(lane >= T_I, lane < T_F + ML_HEADS), dgates, 0.0)
        dz_ref[...] = jnp.where(lane < ROPE, dpe_ref[...], dgates).astype(bf16)
        _acc_row(db_ref, jnp.sum(dgates, axis=0, keepdims=True), pl.program_id(0) == 0)

    row = pl.BlockSpec((tm, TAIL), lambda i: (i, 0))
    return _call(body, "tail_bwd", (t // tm,), [row, pl.BlockSpec((1, TAIL), lambda i: (0, 0)), row, row],
                 [row, pl.BlockSpec((8, TAIL), lambda i: (0, 0))], [_sds((t, TAIL), bf16), _sds((8, TAIL), f32)])(z_tail, b_tile, dzt_pe, dgate)


def _hdot(a, b, ca, cb):
    return lax.dot_general(a.astype(bf16), b.astype(bf16), (((ca,), (cb,)), ((0,), (0,))), preferred_element_type=f32)


def _mlstm_step(q, k, v, igr, fgr, c_mat, n_vec, m):
    nh, ln = q.shape[0], CHUNK
    sq = (nh, ln, ln)
    row = lax.broadcasted_iota(jnp.int32, sq, 1)
    col = lax.broadcasted_iota(jnp.int32, sq, 2)
    eye = row == col

    def to_col(r):
        return jnp.sum(jnp.where(eye, jnp.broadcast_to(r, sq), 0.0), axis=2, keepdims=True)

    bc_r = jnp.sum(jnp.where(row <= col, jnp.broadcast_to(to_col(fgr), sq), 0.0), axis=1, keepdims=True)
    bc_c = to_col(bc_r)
    logw = jnp.where(col <= row, bc_c - bc_r + igr, -jnp.inf)
    inter = bc_c + m
    m_t = jnp.maximum(inter, jnp.max(logw, axis=2, keepdims=True))
    w_intra = jnp.exp(logw - m_t)
    w_inter = jnp.exp(inter - m_t)
    sc = _hdot(q, k, 2, 2) * w_intra
    num = w_inter * _hdot(q, c_mat, 2, 1) + _hdot(sc, v, 2, 1)
    qn = jnp.sum(q.astype(bf16).astype(f32) * n_vec.astype(bf16).astype(f32), axis=2, keepdims=True)
    den = w_inter * qn + jnp.sum(sc, axis=2, keepdims=True)
    h = num / jnp.maximum(jnp.abs(den), jnp.exp(-m_t))
    lane = lax.broadcasted_iota(jnp.int32, (nh, 1, ln), 2)
    b_last = jnp.sum(jnp.where(lane == ln - 1, bc_r, 0.0), axis=2, keepdims=True)
    logu = b_last - bc_r + igr
    m_new = jnp.maximum(b_last + m, jnp.max(logu, axis=2, keepdims=True))
    decay = jnp.exp(b_last + m - m_new)
    u_c = to_col(jnp.exp(logu - m_new))
    c_new = decay * c_mat + _hdot(u_c * k, v, 1, 1)
    n_new = decay * n_vec + jnp.sum(u_c.astype(bf16).astype(f32) * k.astype(bf16).astype(f32), axis=1, keepdims=True)
    return h, c_new, n_new, m_new


ML_VHALF = ML_V // 2
assert O_V % ML_VHALF == 0 and ML_HEADS % 2 == 0


def _ml_specs(nc, rev):
    cc = (lambda c: nc - 1 - c) if rev else (lambda c: c)
    q = pl.BlockSpec((ML_HEADS, CHUNK, ML_DK), lambda c: (0, cc(c), 0))
    k = pl.BlockSpec((ML_HEADS, CHUNK, ML_DK), lambda c: (1, cc(c), 0))
    v_lo = pl.BlockSpec((CHUNK, ML_VHALF), lambda c: (cc(c), O_V // ML_VHALF))
    v_hi = pl.BlockSpec((CHUNK, ML_VHALF), lambda c: (cc(c), O_V // ML_VHALF + 1))
    hv = pl.BlockSpec((ML_HEADS, CHUNK, ML_DV), lambda c: (0, cc(c), 0))
    gate = pl.BlockSpec((ML_HEADS, 1, 1, CHUNK), lambda c: (0, cc(c), 0, 0))
    cm = pl.BlockSpec((ML_HEADS, 1, ML_DK, ML_DV), lambda c: (0, cc(c), 0, 0))
    nv = pl.BlockSpec((ML_HEADS, 1, 1, ML_DK), lambda c: (0, cc(c), 0, 0))
    ms = pl.BlockSpec((ML_HEADS, 1, 1, 1), lambda c: (0, cc(c), 0, 0))
    return q, k, v_lo, v_hi, hv, gate, cm, nv, ms


_ML_STATE = [pltpu.VMEM((ML_HEADS, ML_DK, ML_DV), f32), pltpu.VMEM((ML_HEADS, 1, ML_DK), f32), pltpu.VMEM((ML_HEADS, 1, 1), f32)]


def _ml_zero_state(c_s, n_s, m_s):
    @pl.when(pl.program_id(0) == 0)
    def _():
        c_s[...] = jnp.zeros_like(c_s)
        n_s[...] = jnp.zeros_like(n_s)
        m_s[...] = jnp.zeros_like(m_s)


def _ml_heads_of(v_lo_ref, v_hi_ref):
    half = ML_HEADS // 2
    return jnp.stack([r[:, j * ML_DV:(j + 1) * ML_DV] for r in (v_lo_ref, v_hi_ref) for j in range(half)])


def mlstm_fwd(qk_act, z_main, ig, fg):
    t = qk_act.shape[1]
    nc = t // CHUNK

    def body(q_ref, k_ref, vl_ref, vh_ref, ig_ref, fg_ref, h_ref, c_out, n_out, m_out, c_s, n_s, m_s):
        _ml_zero_state(c_s, n_s, m_s)
        c0, n0, m0 = c_s[...], n_s[...], m_s[...]
        c_out[:, 0] = c0
        n_out[:, 0] = n0
        m_out[:, 0] = m0
        h, c2, n2, m2 = _mlstm_step(q_ref[...], k_ref[...], _ml_heads_of(vl_ref, vh_ref), ig_ref[:, 0], fg_ref[:, 0], c0, n0, m0)
        h_ref[...] = h
        c_s[...] = c2
        n_s[...] = n2
        m_s[...] = m2

    q, k, v_lo, v_hi, hv, gate, cm, nv, ms = _ml_specs(nc, False)
    return _call(body, "mlstm_fwd", (nc,), [q, k, v_lo, v_hi, gate, gate], [hv, cm, nv, ms],
                 [_sds((ML_HEADS, t, ML_DV), f32), _sds((ML_HEADS, nc, ML_DK, ML_DV), f32), _sds((ML_HEADS, nc, 1, ML_DK), f32),
                  _sds((ML_HEADS, nc, 1, 1), f32)], _ML_STATE)(qk_act, qk_act, z_main, z_main, ig, fg)


def mlstm_bwd(qk_act, z_main, ig, fg, c_all, n_all, m_all, dh, exchange=None):
    t = qk_act.shape[1]
    nc = t // CHUNK

    def body(q_ref, k_ref, vl_ref, vh_ref, ig_ref, fg_ref, c_ref, n_ref, m_ref, dh_ref, dq_ref, dk_ref, dv_ref, dig_ref, dfg_ref,
             dc_s, dn_s, dm_s):
        _ml_zero_state(dc_s, dn_s, dm_s)
        _, vjp = jax.vjp(_mlstm_step, q_ref[...], k_ref[...], _ml_heads_of(vl_ref, vh_ref), ig_ref[:, 0], fg_ref[:, 0],
                         c_ref[:, 0], n_ref[:, 0], m_ref[:, 0])
        dq, dk, dv, dig, dfg, dc, dn, dm = vjp((dh_ref[...], dc_s[...], dn_s[...], dm_s[...]))
        dq_ref[...] = dq
        dk_ref[...] = dk
        for j in range(ML_HEADS):
            dv_ref[:, j * ML_DV:(j + 1) * ML_DV] = dv[j].astype(bf16)
        dig_ref[:, 0] = dig
        dfg_ref[:, 0] = dfg
        dc_s[...] = dc
        dn_s[...] = dn
        dm_s[...] = dm

    q, k, v_lo, v_hi, hv, gate, cm, nv, ms = _ml_specs(nc, True)
    gshape = _sds((ML_HEADS, nc, 1, CHUNK), f32)
    return _call(body, "mlstm_bwd", (nc,), [q, k, v_lo, v_hi, gate, gate, cm, nv, ms, hv],
                 [q, q, pl.BlockSpec((CHUNK, ML_V), lambda c: (nc - 1 - c, 0)), gate, gate],
                 [_sds((ML_HEADS, t, ML_DK), f32), _sds((ML_HEADS, t, ML_DK), f32), _sds((t, ML_V), bf16), gshape, gshape],
                 _ML_STATE, exchange=exchange)(qk_act, qk_act, z_main, z_main, ig, fg, c_all, n_all, m_all, dh)


def _ml_out(h, zo, g):
    return _rms(h, g) * jax.nn.sigmoid(zo)


def mlstm_out(h, z_main, g_hnorm):
    t = h.shape[1]
    tm = _tile(t, (512, 256, 128, 64))
    zo = O_O // ML_DV

    def body(h_ref, z_ref, g_ref, y_ref):
        y_ref[...] = _ml_out(h_ref[0], z_ref[...], g_ref[0]).astype(bf16)

    return _call(body, "mlstm_out", (t // tm, ML_HEADS),
                 [pl.BlockSpec((1, tm, ML_DV), lambda i, hd: (hd, i, 0)), pl.BlockSpec((tm, ML_DV), lambda i, hd: (i, zo + hd)),
                  pl.BlockSpec((1, 1, ML_DV), lambda i, hd: (hd, 0, 0))],
                 pl.BlockSpec((tm, ML_DV), lambda i, hd: (i, hd)), _sds((t, ML_V), bf16))(h, z_main, g_hnorm)


def mlstm_out_bwd(h, z_main, g_hnorm, dy):
    t = h.shape[1]
    tm = _tile(t, (512, 256, 128, 64))
    zo = O_O // ML_DV

    def body(h_ref, z_ref, g_ref, dy_ref, dh_ref, dzo_ref, dg_ref):
        _, vjp = jax.vjp(_ml_out, h_ref[0], z_ref[...], g_ref[0])
        dh, dz, dg = vjp(dy_ref[...])
        dh_ref[0] = dh
        dzo_ref[...] = dz.astype(bf16)

        @pl.when(pl.program_id(1) == 0)
        def _():
            dg_ref[...] = jnp.zeros_like(dg_ref)

        dg_ref[0, 0:1, :] += dg

    head = pl.BlockSpec((1, tm, ML_DV), lambda hd, i: (hd, i, 0))
    blk = pl.BlockSpec((tm, ML_DV), lambda hd, i: (i, hd))
    return _call(body, "mlstm_out_bwd", (ML_HEADS, t // tm),
                 [head, pl.BlockSpec((tm, ML_DV), lambda hd, i: (i, zo + hd)), pl.BlockSpec((1, 1, ML_DV), lambda hd, i: (hd, 0, 0)), blk],
                 [head, blk, pl.BlockSpec((1, 8, ML_DV), lambda hd, i: (hd, 0, 0))],
                 [_sds((ML_HEADS, t, ML_DV), f32), _sds((t, ML_V), bf16), _sds((ML_HEADS, 8, ML_DV), f32)])(h, z_main, g_hnorm, dy)


def _merge(ga, gb, ya, yb):
    return jax.nn.sigmoid(ga) * ya + jax.nn.sigmoid(gb) * yb


def _merge_specs(t, d):
    tm = _tile(t, (512, 256, 128, 64))
    bw = _tile(d, (512, 256, 128))
    assert O_GA % bw == 0 and (O_GA + d) % bw == 0
    blk = pl.BlockSpec((tm, bw), lambda i, j: (i, j))
    ga = pl.BlockSpec((tm, bw), lambda i, j: (i, O_GA // bw + j))
    gb = pl.BlockSpec((tm, bw), lambda i, j: (i, (O_GA + d) // bw + j))
    return tm, bw, blk, ga, gb


def merge_fwd(z_main, ya, yb):
    t, d = ya.shape
    tm, bw, blk, ga, gb = _merge_specs(t, d)

    def body(ga_ref, gb_ref, ya_ref, yb_ref, o_ref):
        o_ref[...] = _merge(ga_ref[...], gb_ref[...], ya_ref[...], yb_ref[...]).astype(bf16)

    return _call(body, "merge_fwd", (t // tm, d // bw), [ga, gb, blk, blk], blk, _sds((t, d), bf16))(z_main, z_main, ya, yb)


def merge_bwd(z_main, ya, yb, dmerged):
    t, d = ya.shape
    tm, bw, blk, ga, gb = _merge_specs(t, d)

    def body(ga_ref, gb_ref, ya_ref, yb_ref, dm_ref, dga_ref, dgb_ref, dya_ref, dyb_ref):
        _, vjp = jax.vjp(_merge, ga_ref[...], gb_ref[...], ya_ref[...], yb_ref[...])
        dga, dgb, dya, dyb = vjp(dm_ref[...])
        dga_ref[...] = dga.astype(bf16)
        dgb_ref[...] = dgb.astype(bf16)
        dya_ref[...] = dya.astype(bf16)
        dyb_ref[...] = dyb.astype(bf16)

    return _call(body, "merge_bwd", (t // tm, d // bw), [ga, gb, blk, blk, blk], [blk] * 4, [_sds((t, d), bf16)] * 4)(
        z_main, z_main, ya, yb, dmerged)


def _cross(cq, ck, cv, gq, gk):
    outs = []
    for hd in range(CR_HEADS):
        sl = slice(hd * CR_HD, (hd + 1) * CR_HD)
        q = _rms(cq[:, sl], gq)
        k = _rms(ck[:, sl], gk)
        s = _bdot(q, k, 1, 1) * (CR_HD ** -0.5)
        p = jax.nn.softmax(s, axis=-1)
        outs.append(_bdot(p, cv[:, sl], 1, 0))
    return jnp.concatenate(outs, axis=1)


def cross_fwd(cq, ck, cv, gq, gk):
    t, w = cq.shape
    nm = ck.shape[0]
    tm = _tile(t, (512, 256, 128, 64))

    def body(q_ref, k_ref, v_ref, gq_ref, gk_ref, o_ref):
        o_ref[...] = _cross(q_ref[...], k_ref[...], v_ref[...], gq_ref[...], gk_ref[...]).astype(bf16)

    row = pl.BlockSpec((tm, w), lambda i: (i, 0))
    full = pl.BlockSpec((nm, w), lambda i: (0, 0))
    gain = pl.BlockSpec((1, CR_HD), lambda i: (0, 0))
    return _call(body, "cross_fwd", (t // tm,), [row, full, full, gain, gain], row, _sds((t, w), bf16))(cq, ck, cv, gq, gk)


def cross_bwd(cq, ck, cv, gq, gk, do):
    t, w = cq.shape
    nm = ck.shape[0]
    tm = _tile(t, (512, 256, 128, 64))

    def body(q_ref, k_ref, v_ref, gq_ref, gk_ref, do_ref, dq_ref, dk_ref, dv_ref, dgq_ref, dgk_ref):
        first = pl.program_id(0) == 0
        _, vjp = jax.vjp(_cross, q_ref[...], k_ref[...], v_ref[...], gq_ref[...], gk_ref[...])
        dq, dk, dv, dgq, dgk = vjp(do_ref[...])
        dq_ref[...] = dq.astype(bf16)

        @pl.when(first)
        def _():
            dk_ref[...] = jnp.zeros_like(dk_ref)
            dv_ref[...] = jnp.zeros_like(dv_ref)

        dk_ref[...] += dk
        dv_ref[...] += dv
        _acc_row(dgq_ref, dgq, first)
        _acc_row(dgk_ref, dgk, first)

    row = pl.BlockSpec((tm, w), lambda i: (i, 0))
    full = pl.BlockSpec((nm, w), lambda i: (0, 0))
    gain = pl.BlockSpec((1, CR_HD), lambda i: (0, 0))
    acc = pl.BlockSpec((8, CR_HD), lambda i: (0, 0))
    return _call(body, "cross_bwd", (t // tm,), [row, full, full, gain, gain, row], [row, full, full, acc, acc],
                 [_sds((t, w), bf16), _sds((nm, w), f32), _sds((nm, w), f32), _sds((8, CR_HD), f32), _sds((8, CR_HD), f32)])(
        cq, ck, cv, gq, gk, do)


def loss_head(x2, fo, target):
    t, d = x2.shape
    tm = _tile(t, (256, 128, 64, 32, 16, 8))

    def body(a_ref, b_ref, t_ref, dx_ref, dxb_ref, l_ref):
        err = a_ref[...] + b_ref[...] - t_ref[...]
        dx = err / d
        dx_ref[...] = dx
        dxb_ref[...] = dx.astype(bf16)
        part = 0.5 * jnp.sum(jnp.mean(err * err, axis=1, keepdims=True), axis=0, keepdims=True)
        _acc_row(l_ref, jnp.broadcast_to(part, (1, 128)), pl.program_id(0) == 0)

    row = pl.BlockSpec((tm, d), lambda i: (i, 0))
    return _call(body, "loss_head", (t // tm,), [row, row, row], [row, row, pl.BlockSpec((8, 128), lambda i: (0, 0))],
                 [_sds((t, d), f32), _sds((t, d), bf16), _sds((8, 128), f32)])(x2, fo, target)


def _place():
    x, y, c = lax.axis_index("x"), lax.axis_index("y"), lax.axis_index("c")
    peers = {}
    for r in range(1, N_DEV):
        px = 1 - x if r & 4 else x
        py = 1 - y if r & 2 else y
        pc = 1 - c if r & 1 else c
        peers[r] = ((px, py, pc), 4 * px + 2 * py + pc)
    return 4 * x + 2 * y + c, peers


N_REL = N_DEV - 1
RELATIONS = tuple(range(1, N_DEV))
SIBLING = 1
OTHER_CHIPS = (2, 4, 6)
PASSED_ON = (3, 5, 7)


def _exchange_ops(ins, outs, sems, scatter):
    n = len(ins)
    send_sems, recv_sems, local_sems = sems

    def tools():
        me, peers = _place()

        def copy(a, r, src, dst_idx, to):
            return pltpu.make_async_remote_copy(
                src_ref=src, dst_ref=outs[a].at[dst_idx], send_sem=send_sems.at[a * N_REL + r - 1],
                recv_sem=recv_sems.at[a * N_REL + r - 1], device_id=peers[to][0], device_id_type=MESH)

        def local(a):
            return pltpu.make_async_copy(ins[a].at[me] if scatter else ins[a], outs[a].at[me], local_sems.at[a])

        def arrival(a, r):
            return copy(a, r, ins[a].at[me] if scatter else ins[a], peers[r][1], r)

        return me, peers, copy, local, arrival

    if scatter:
        def sends():
            me, peers, copy, local, _ = tools()
            return [local(a) for a in range(n)], [copy(a, r, ins[a].at[peers[r][1]], me, r) for a in range(n) for r in RELATIONS]

        def start():
            loc, out = sends()
            for cp in loc + out:
                cp.start()

        middle = None
        waited_last = RELATIONS
    else:
        def sends():
            me, peers, copy, local, _ = tools()
            own = [copy(a, r, ins[a], me, r) for a in range(n) for r in (SIBLING,) + OTHER_CHIPS]
            return [local(a) for a in range(n)], own

        def passes():
            me, peers, copy, _, _ = tools()
            return [copy(a, r, outs[a].at[peers[r - 1][1]], peers[r - 1][1], SIBLING) for a in range(n) for r in PASSED_ON]

        def start():
            loc, out = sends()
            for cp in loc + out:
                cp.start()

        def middle():
            _, _, _, _, arrival = tools()
            fwd = passes()
            for a in range(n):
                for i, r in enumerate(PASSED_ON):
                    arrival(a, r - 1).wait_recv()
                    fwd[a * len(PASSED_ON) + i].start()

        waited_last = (SIBLING,) + PASSED_ON

    def wait():
        _, _, _, _, arrival = tools()
        for a in range(n):
            for r in waited_last:
                arrival(a, r).wait_recv()
        loc, out = sends()
        for cp in out + ([] if scatter else passes()):
            cp.wait_send()
        for cp in loc:
            cp.wait()

    return start, middle, wait


def _exchange_shapes(arrs, scatter):
    return [_sds(a.shape if scatter else (N_DEV,) + a.shape, a.dtype) for a in arrs]


def _exchange_sems(n):
    return [pltpu.SemaphoreType.DMA((n * N_REL,)), pltpu.SemaphoreType.DMA((n * N_REL,)), pltpu.SemaphoreType.DMA((n,))]


def _exchange(arrs, name, scatter):
    n = len(arrs)

    def body(*refs):
        start, middle, wait = _exchange_ops(refs[:n], refs[n:2 * n], refs[2 * n:], scatter)
        start()
        if middle is not None:
            middle()
        wait()

    any_spec = pl.BlockSpec(memory_space=pl.ANY)
    return pl.pallas_call(body, name=name, in_specs=[any_spec] * n, out_specs=[any_spec] * n,
                          out_shape=_exchange_shapes(arrs, scatter), scratch_shapes=_exchange_sems(n))(*arrs)


def cast_bf16(w, name):
    _, r, c = w.shape
    tr = _tile(r, (256, 128, 64, 32, 16))

    def body(w_ref, o_ref):
        o_ref[...] = w_ref[0].astype(bf16)

    return _call(body, name, (r // tr,), [pl.BlockSpec((1, tr, c), lambda i: (0, i, 0))], pl.BlockSpec((tr, c), lambda i: (i, 0)),
                 _sds((r, c), bf16))(w)


def _adamw(w, g, m, v):
    m = ADAM_B1 * m + (1.0 - ADAM_B1) * g
    v = ADAM_B2 * v + (1.0 - ADAM_B2) * jnp.square(g)
    m_hat = m / (1.0 - ADAM_B1 ** ADAM_STEP)
    v_hat = v / (1.0 - ADAM_B2 ** ADAM_STEP)
    delta = -ADAM_LR * (m_hat / (jnp.sqrt(v_hat) + ADAM_EPS) + ADAM_WD * w)
    return delta, m, v


def adam_sum(parts, w, m, v, name):
    _, r, c = parts.shape
    budget = 4 * 1024 * 1024
    tr = r
    for cand in (1024, 512, 256, 128, 64, 32, 16):
        if r % cand == 0 and N_DEV * cand * c * 4 <= budget:
            tr = cand
            break

    def body(p_ref, w_ref, m_ref, v_ref, g_ref, d_ref, m2_ref, v2_ref):
        g = p_ref[0].astype(f32)
        for k in range(1, N_DEV):
            g = g + p_ref[k].astype(f32)
        d, m2, v2 = _adamw(w_ref[0], g, m_ref[0], v_ref[0])
        g_ref[...] = g
        d_ref[...] = d
        m2_ref[...] = m2
        v2_ref[...] = v2

    blk = pl.BlockSpec((1, tr, c), lambda i: (0, i, 0))
    out = pl.BlockSpec((tr, c), lambda i: (i, 0))
    return _call(body, name, (r // tr,), [pl.BlockSpec((N_DEV, tr, c), lambda i: (0, i, 0)), blk, blk, blk], [out] * 4,
                 [_sds((r, c), f32)] * 4)(parts, w, m, v)


def sum_parts(parts, name):
    _, r, c = parts.shape

    def body(p_ref, o_ref):
        g = p_ref[0]
        for k in range(1, N_DEV):
            g = g + p_ref[k]
        o_ref[...] = g

    return pl.pallas_call(body, name=name, out_shape=_sds((r, c), f32))(parts)


def adam_flat(w, g, m, v, name):
    def body(w_ref, g_ref, m_ref, v_ref, d_ref, m2_ref, v2_ref):
        d, m2, v2 = _adamw(w_ref[...], g_ref[...], m_ref[...], v_ref[...])
        d_ref[...] = d
        m2_ref[...] = m2
        v2_ref[...] = v2

    return pl.pallas_call(body, name=name, out_shape=[_sds(w.shape, f32)] * 3)(w, g, m, v)


def _pack(vecs, multiple):
    flat = jnp.concatenate([v.reshape(-1) for v in vecs])
    n = flat.shape[0]
    total = -(-n // multiple) * multiple
    return jnp.pad(flat, (0, total - n))


def _unpack(flat, shapes):
    out, pos = [], 0
    for s in shapes:
        n = 1
        for d in s:
            n *= d
        out.append(flat[pos:pos + n].reshape(s))
        pos += n
    return out


def _pad_lanes(v, width=TAIL):
    return jnp.pad(v, ((0, 0), (0, width - v.shape[1])))


def kernel(x, mem, positions, g_mix, w_in, g_qa, w_qb, g_kva, w_kvb, g_qn_nope, g_qn_pe, g_kn_nope, g_kn_pe, conv_qk, b_if, g_hnorm, p_a, p_b, w_out, g_cross, g_mem, wq_c, wk_c, wv_c, g_cq, g_ck, wo_c, g_ffn, w_up, conv_ffn, b_conv_ffn, w_down, loss_target, m_g_mix, m_w_in, m_g_qa, m_w_qb, m_g_kva, m_w_kvb, m_g_qn_nope, m_g_qn_pe, m_g_kn_nope, m_g_kn_pe, m_conv_qk, m_b_if, m_g_hnorm, m_p_a, m_p_b, m_w_out, m_g_cross, m_g_mem, m_wq_c, m_wk_c, m_wv_c, m_g_cq, m_g_ck, m_wo_c, m_g_ffn, m_w_up, m_conv_ffn, m_b_conv_ffn, m_w_down, v_g_mix, v_w_in, v_g_qa, v_w_qb, v_g_kva, v_w_kvb, v_g_qn_nope, v_g_qn_pe, v_g_kn_nope, v_g_kn_pe, v_conv_qk, v_b_if, v_g_hnorm, v_p_a, v_p_b, v_w_out, v_g_cross, v_g_mem, v_wq_c, v_wk_c, v_wv_c, v_g_cq, v_g_ck, v_wo_c, v_g_ffn, v_w_up, v_conv_ffn, v_b_conv_ffn, v_w_down):
    args = dict(locals())
    names = ['g_mix', 'w_in', 'g_qa', 'w_qb', 'g_kva', 'w_kvb', 'g_qn_nope', 'g_qn_pe', 'g_kn_nope', 'g_kn_pe', 'conv_qk', 'b_if',
             'g_hnorm', 'p_a', 'p_b', 'w_out', 'g_cross', 'g_mem', 'wq_c', 'wk_c', 'wv_c', 'g_cq', 'g_ck', 'wo_c', 'g_ffn', 'w_up',
             'conv_ffn', 'b_conv_ffn', 'w_down']
    big = ['w_in', 'w_qb', 'w_kvb', 'p_a', 'p_b', 'w_out', 'wq_c', 'wk_c', 'wv_c', 'wo_c', 'w_up', 'w_down']
    sharded_small = ['conv_qk', 'g_hnorm', 'conv_ffn']
    replicated = [n for n in names if n not in big and n not in sharded_small]

    t, d = x.shape[1], x.shape[2]
    x2d, tgt = x[0], loss_target[0]
    mem2d = mem[0]
    me = 4 * lax.axis_index("x") + 2 * lax.axis_index("y") + lax.axis_index("c")
    nc = t // CHUNK
    f2 = b_conv_ffn.shape[1]
    wmain = O_GA + 2 * d

    first = ['w_in', 'w_qb', 'w_kvb']
    behind_in = ['p_a', 'p_b', 'w_out', 'wq_c', 'wk_c', 'wv_c', 'wo_c']
    shards = {n: cast_bf16(args[n], "cast_" + n) for n in big}
    small_local = _pack([args[n] for n in sharded_small], 128).reshape(1, -1)
    gathered = _exchange([shards[n] for n in first] + [small_local], "comm_gather_first", scatter=False)
    gw = dict(zip(first, gathered[:-1]))
    small_all = gathered[-1]
    full_small, pos = [], 0
    for n in sharded_small:
        _, rows, cols = args[n].shape
        piece = small_all[:, 0, pos:pos + rows * cols].reshape(N_DEV, rows, cols)
        full_small.append(piece.transpose(1, 0, 2).reshape(rows, N_DEV * cols))
        pos += rows * cols
    conv_qk_f, g_hnorm_f, conv_ffn_f = full_small

    w_in_f = gw['w_in'].transpose(1, 0, 2).reshape(d, -1)
    c_kpe, c_q, c_i, c_o = O_Q, O_Q + ROPE, O_Q + ROPE + 2 * ML_QK + ML_V, O_Q + ROPE + 2 * ML_QK + ML_V + 2 * ML_HEADS
    w_main = jnp.concatenate([w_in_f[:, :c_kpe], w_in_f[:, c_q:c_i], w_in_f[:, c_o:]], axis=1)[None]
    w_tail = jnp.concatenate([w_in_f[:, c_kpe:c_q], w_in_f[:, c_i:c_o],
                              jnp.zeros((d, TAIL - ROPE - 2 * ML_HEADS), bf16)], axis=1)[None]
    assert w_main.shape[2] == wmain
    qb = gw['w_qb'].transpose(1, 0, 2).reshape(Q_LORA, MLA_HEADS, NOPE + ROPE)
    w_qb_p = jnp.concatenate([qb, jnp.zeros((Q_LORA, MLA_HEADS, HEAD_PAD - NOPE - ROPE), bf16)], axis=2).reshape(1, Q_LORA, -1)
    w_kvb3 = gw['w_kvb']

    inv_freq = ROPE_BASE ** (-jnp.arange(0, ROPE, 2, dtype=f32) / ROPE)
    inv_tile = _pad_lanes(jnp.concatenate([inv_freq, inv_freq])[None])
    cos, sin = rope_tables(positions.reshape(t, 1), inv_tile)
    gqp, gkp = _pad_lanes(g_qn_pe), _pad_lanes(g_kn_pe)
    b_tile = jnp.pad(b_if, ((0, 0), (T_I, TAIL - T_I - 2 * ML_HEADS)))

    u0 = rms_fwd(x2d, g_mix, "rms_mix")
    z_main, got = mm_nn(u0, w_main, f32, "mm_in_main", exchange=([shards[n] for n in behind_in], False))
    gw.update(zip(behind_in, got))
    p_a3, p_b3, w_out3 = (gw[n].reshape(1, -1, d) for n in ('p_a', 'p_b', 'w_out'))
    wq_c3, wk_c3, wv_c3 = (gw[n].reshape(1, d, -1) for n in ('wq_c', 'wk_c', 'wv_c'))
    wo_c3 = gw['wo_c']
    z_tail = mm_nn(u0, w_tail, f32, "mm_in_tail")
    qa_n, kv_n = lat_norm(z_main, g_qa, g_kva)
    q_raw = mm_nn(qa_n, w_qb_p, f32, "mm_qb")
    kv_raw = mm_nn(kv_n, w_kvb3, f32, "mm_kvb")
    qh, kh, vh = mla_prep(q_raw, kv_raw, z_tail, cos, sin, g_qn_nope, gqp, g_kn_nope, gkp)
    (o_a, o_ab, lse), (w_up3,) = mla_fwd(qh, kh, vh, exchange=([shards['w_up']], False))

    qk_act = qk_conv(z_main, conv_qk_f)
    gates = gate_act(z_tail, b_tile)

    def to_rows(cols):
        return cols.T.reshape(ML_HEADS, nc, 1, CHUNK)

    ig, fg = to_rows(gates[:, T_I:T_F]), to_rows(gates[:, T_F:T_F + ML_HEADS])
    h_ml, c_all, n_all, m_all = mlstm_fwd(qk_act, z_main, ig, fg)
    g_hn3 = g_hnorm_f.reshape(ML_HEADS, 1, ML_DV)
    y_b = mlstm_out(h_ml, z_main, g_hn3)

    ya = mm_nn(o_ab, p_a3, f32, "mm_pa")
    yb = mm_nn(y_b, p_b3, f32, "mm_pb")
    merged = merge_fwd(z_main, ya, yb)
    mo = mm_nn(merged, w_out3, f32, "mm_out")
    x1, uc = resid_rms(x2d, mo, g_cross, "resid_cross")
    mem_n = rms_fwd(mem2d, g_mem, "rms_mem")
    cq = mm_nn(uc, wq_c3, f32, "mm_cq")
    ck = mm_nn(mem_n, wk_c3, f32, "mm_ck")
    cv = mm_nn(mem_n, wv_c3, f32, "mm_cv")
    o_c = cross_fwd(cq, ck, cv, g_cq, g_ck)
    co = mm_nn(o_c, wo_c3, f32, "mm_oc")
    x2, u3 = resid_rms(x1, co, g_ffn, "resid_ffn")
    hup, (w_down_g,) = mm_nn(u3, w_up3, f32, "mm_up", exchange=([shards['w_down']], False))
    w_down3 = w_down_g.reshape(1, -1, d)
    gl = glu_fwd(hup, conv_ffn_f, b_conv_ffn)
    fo = mm_nn(gl, w_down3, f32, "mm_down")
    dx3, dx3_b, loss_acc = loss_head(x2, fo, tgt)

    grads, parts = {}, {}
    grads['w_down'] = mm_tn(gl, dx3_b, 1, "mm_d_wdown").reshape(N_DEV, -1, d)
    dgl = mm_nt(dx3_b, w_down3, f32, "mm_d_gl")
    (dh1, dh2, dcw1, dcw2, db1, db2), (parts['w_down'],) = glu_bwd(hup, conv_ffn_f, b_conv_ffn, dgl,
                                                                    exchange=([grads['w_down']], True))
    dhup, dconv_ffn, db_ffn = (jnp.concatenate(pair, axis=1) for pair in ((dh1, dh2), (dcw1, dcw2), (db1, db2)))
    grads['w_up'] = mm_tn(u3, dhup, N_DEV, "mm_d_wup")
    du3 = mm_nt(dhup, w_up3, f32, "mm_d_u3")
    dx2, dx2_b, dg_ffn = rms_bwd(x2, g_ffn, [du3], dx3, "rms_bwd_ffn", want_b16=True)
    grads['wo_c'] = mm_tn(o_c, dx2_b, N_DEV, "mm_d_woc")
    do_c = mm_nt(dx2_b, wo_c3, f32, "mm_d_oc")
    dcq, dck, dcv, dg_cq, dg_ck = cross_bwd(cq, ck, cv, g_cq, g_ck, do_c)
    grads['wq_c'] = mm_tn(uc, dcq, 1, "mm_d_wqc").reshape(N_DEV, -1, dcq.shape[1])
    grads['wk_c'] = mm_tn(mem_n, dck, 1, "mm_d_wkc").reshape(N_DEV, -1, dck.shape[1])
    grads['wv_c'] = mm_tn(mem_n, dcv, 1, "mm_d_wvc").reshape(N_DEV, -1, dcv.shape[1])
    duc = mm_nt(dcq, wq_c3, f32, "mm_d_uc")
    dmem_k = mm_nt(dck, wk_c3, f32, "mm_d_memk")
    dmem_v = mm_nt(dcv, wv_c3, f32, "mm_d_memv")
    dg_mem, = rms_bwd(mem2d, g_mem, [dmem_k, dmem_v], None, "rms_bwd_mem", want_dx=False)
    dx1, dx1_b, dg_cross = rms_bwd(x1, g_cross, [duc], dx2, "rms_bwd_cross", want_b16=True)
    grads['w_out'] = mm_tn(merged, dx1_b, 1, "mm_d_wout").reshape(N_DEV, -1, d)
    dmerged = mm_nt(dx1_b, w_out3, f32, "mm_d_merged")
    dga, dgb, dya, dyb = merge_bwd(z_main, ya, yb, dmerged)
    grads['p_a'] = mm_tn(o_ab, dya, 1, "mm_d_pa").reshape(N_DEV, -1, d)
    grads['p_b'] = mm_tn(y_b, dyb, 1, "mm_d_pb").reshape(N_DEV, -1, d)
    do_a = mm_nt(dya, p_a3, f32, "mm_d_oa")
    dy_b = mm_nt(dyb, p_b3, f32, "mm_d_yb")

    dh_ml, dzo, dg_hn = mlstm_out_bwd(h_ml, z_main, g_hn3, dy_b)
    (dq_act, dk_act, dzv, dig, dfg), got = mlstm_bwd(qk_act, z_main, ig, fg, c_all, n_all, m_all, dh_ml,
                                                     exchange=([grads[n] for n in behind_in], True))
    parts.update(zip(behind_in, got))
    dzqk, dconv_qk = qk_conv_bwd(z_main, conv_qk_f, dq_act, dk_act)

    delta = mla_delta(o_a, do_a)
    (dqh, dkh, dvh), (parts['w_up'],) = mla_bwd(qh, kh, vh, do_a, lse, delta.reshape(MLA_HEADS, 1, t),
                                                exchange=([grads['w_up']], True))
    dq_raw, dkv_raw, dzt_pe, dg_qn, dg_qp, dg_kn, dg_kp = mla_prep_bwd(
        q_raw, kv_raw, z_tail, cos, sin, g_qn_nope, gqp, g_kn_nope, gkp, dqh, dkh, dvh)
    d_wqb_p = mm_tn(qa_n, dq_raw, 1, "mm_d_wqb")[0].reshape(Q_LORA, MLA_HEADS, HEAD_PAD)[:, :, :NOPE + ROPE]
    grads['w_qb'] = d_wqb_p.reshape(Q_LORA, N_DEV, -1).transpose(1, 0, 2)
    grads['w_kvb'] = mm_tn(kv_n, dkv_raw, N_DEV, "mm_d_wkvb")
    dqa = mm_nt(dq_raw, w_qb_p, f32, "mm_d_qa")
    dkvn = mm_nt(dkv_raw, w_kvb3, f32, "mm_d_kvn")
    dz_lat, dg_qa, dg_kva = lat_norm_bwd(z_main, g_qa, g_kva, dqa, dkvn)

    def to_cols(rows):
        return rows.reshape(ML_HEADS, t).T

    dgate = jnp.pad(jnp.concatenate([to_cols(dig), to_cols(dfg)], axis=1), ((0, 0), (T_I, TAIL - T_I - 2 * ML_HEADS)))
    dz_tail, db_if = tail_bwd(z_tail, b_tile, dzt_pe, dgate)
    dz_main = jnp.concatenate([dz_lat, dzqk, dzv, dzo, dga, dgb], axis=1)
    d_wmain = mm_tn(u0, dz_main, 1, "mm_d_wmain")[0]
    d_wtail = mm_tn(u0, dz_tail, 1, "mm_d_wtail")[0]
    d_win = jnp.concatenate([d_wmain[:, :O_Q], d_wtail[:, :ROPE], d_wmain[:, O_Q:O_O], d_wtail[:, T_I:T_I + 2 * ML_HEADS],
                             d_wmain[:, O_O:]], axis=1)
    grads['w_in'] = d_win.reshape(d, N_DEV, -1).transpose(1, 0, 2)
    du0_a, got = mm_nt(dz_main, w_main, f32, "mm_d_u0_main", exchange=([grads[n] for n in first], True))
    parts.update(zip(first, got))
    du0_b = mm_nt(dz_tail, w_tail, f32, "mm_d_u0_tail")
    grad_x, dg_mix = rms_bwd(x2d, g_mix, [du0_a, du0_b], dx1, "rms_bwd_mix")

    out_g, out_d, out_m, out_v = {}, {}, {}, {}
    for n in big:
        res = adam_sum(parts[n], args[n], args['m_' + n], args['v_' + n], "adam_" + n)
        out_g[n], out_d[n], out_m[n], out_v[n] = (a.reshape(args[n].shape) for a in res)

    small_full = {
        'g_mix': dg_mix[0], 'g_qa': dg_qa[0], 'g_kva': dg_kva[0], 'g_qn_nope': dg_qn[0], 'g_qn_pe': dg_qp[0, :ROPE],
        'g_kn_nope': dg_kn[0], 'g_kn_pe': dg_kp[0, :ROPE], 'conv_qk': dconv_qk, 'b_if': db_if[0, T_I:T_I + 2 * ML_HEADS],
        'g_hnorm': dg_hn[:, 0, :], 'g_cross': dg_cross[0], 'g_mem': dg_mem[0], 'g_cq': dg_cq[0], 'g_ck': dg_ck[0],
        'g_ffn': dg_ffn[0], 'conv_ffn': dconv_ffn, 'b_conv_ffn': db_ffn[0], 'loss': loss_acc[0, :1]}
    order = list(small_full)
    packed = _pack([small_full[n] for n in order], 8 * 128).reshape(1, -1)
    gathered_small, = _exchange([packed], "comm_gather_small", scatter=False)
    summed = sum_parts(gathered_small.reshape(N_DEV, -1, 128), "sum_small").reshape(-1)
    full_g = dict(zip(order, _unpack(summed, [small_full[n].shape for n in order])))
    loss = full_g['loss'][0]

    local_g = {}
    for n in replicated:
        local_g[n] = full_g[n].reshape(args[n].shape)
    for n in sharded_small:
        shp = args[n].shape
        full = full_g[n].reshape((1,) + full_g[n].shape)
        local_g[n] = lax.dynamic_slice_in_dim(full, me * shp[-1], shp[-1], axis=2)
    small = replicated + sharded_small
    dl_f, m_f, v_f = adam_flat(*[_pack([src[n] if pre == '' else args[pre + n] for n in small], 8 * 128).reshape(-1, 128)
                                 for pre, src in (('', args), ('', local_g), ('m_', None), ('v_', None))], "adam_small")
    shapes = [args[n].shape for n in small]
    for dst, flat in ((out_d, dl_f), (out_m, m_f), (out_v, v_f)):
        dst.update(zip(small, _unpack(flat.reshape(-1), shapes)))
    out_g.update(local_g)

    return (loss, grad_x[None], *[out_g[n] for n in names], *[out_d[n] for n in names],
            *[out_m[n] for n in names], *[out_v[n] for n in names])
```

```python
import functools

import jax
import jax.numpy as jnp
from jax import lax
from jax.experimental import pallas as pl
from jax.experimental.pallas import tpu as pltpu

f32 = jnp.float32
bf16 = jnp.bfloat16

N_DEV = 8
EPS = 1e-6
CHUNK = 64
CHUNK_SHIFT = 6
assert 1 << CHUNK_SHIFT == CHUNK
MLA_HEADS = 16
Q_LORA = 512
KV_LORA = 512
NOPE = 128
ROPE = 64
V_HEAD = 128
ROPE_BASE = 10000.0
HEAD_PAD = 256
ML_HEADS = 8
ML_DK = 128
ML_DV = 256
ML_CONV = 4
ML_QK = ML_HEADS * ML_DK
ML_V = ML_HEADS * ML_DV
CR_HEADS = 4
CR_HD = 128
FFN_CONV = 3
ADAM_LR = 0.001
ADAM_B1 = 0.9
ADAM_B2 = 0.999
ADAM_EPS = 1e-08
ADAM_WD = 0.01
ADAM_STEP = 10
O_QA, O_KV, O_Q, O_K = 0, Q_LORA, Q_LORA + KV_LORA, Q_LORA + KV_LORA + ML_QK
O_V = O_K + ML_QK
O_O = O_V + ML_V
O_GA = O_O + ML_V
TAIL = 128
T_I, T_F = ROPE, ROPE + ML_HEADS
VMEM_LIMIT_V7X = 48 * 1024 * 1024
MESH = pl.DeviceIdType.MESH


def _call(body, name, grid, in_specs, out_specs, out_shape, scratch=(), exchange=None):
    params = pltpu.CompilerParams(vmem_limit_bytes=VMEM_LIMIT_V7X)
    if exchange is None:
        return pl.pallas_call(body, name=name, grid=grid, in_specs=in_specs, out_specs=out_specs, out_shape=out_shape,
                              scratch_shapes=list(scratch), compiler_params=params)
    arrs, scatter = exchange
    single = not isinstance(out_specs, (list, tuple))
    o_specs = [out_specs] if single else list(out_specs)
    o_shape = [out_shape] if single else list(out_shape)
    n_in, n_out, n_sc, n = len(in_specs), len(o_specs), len(scratch), len(arrs)
    any_spec = pl.BlockSpec(memory_space=pl.ANY)

    def body_with_exchange(*refs):
        pos = [0]

        def take(k):
            pos[0] += k
            return refs[pos[0] - k:pos[0]]

        ins, ex_in, outs, ex_out, sc = take(n_in), take(n), take(n_out), take(n), take(n_sc)
        start, middle, wait = _exchange_ops(ex_in, ex_out, refs[pos[0]:], scatter)
        step, total = 0, 1
        for a in range(len(grid)):
            step = step * grid[a] + pl.program_id(a)
            total *= grid[a]
        pl.when(step == 0)(start)
        body(*ins, *outs, *sc)
        if middle is not None:
            pl.when(step == total // 2)(middle)
        pl.when(step == total - 1)(wait)

    call = pl.pallas_call(body_with_exchange, name="comm_" + name, grid=grid, in_specs=list(in_specs) + [any_spec] * n,
                          out_specs=o_specs + [any_spec] * n, out_shape=o_shape + _exchange_shapes(arrs, scatter),
                          scratch_shapes=list(scratch) + _exchange_sems(n), compiler_params=params)

    def run(*operands):
        res = call(*operands, *arrs)
        return (res[0] if single else list(res[:n_out])), list(res[n_out:])

    return run


def _tile(n, cands):
    for c in cands:
        if n % c == 0:
            return c
    return n


def _sds(shape, dtype):
    return jax.ShapeDtypeStruct(tuple(shape), dtype)


def _bdot(a, b, ca, cb):
    return lax.dot_general(a.astype(bf16), b.astype(bf16), (((ca,), (cb,)), ((), ())), preferred_element_type=f32)


_BIG = (1024, 512, 256, 128)


def _col_tile(nb):
    return nb if nb <= 1536 else _tile(nb, _BIG)


_DEEP = (2048, 1024, 512, 256, 128)


def _mm_call(name, grid, in_specs, out_spec, out_shape, tile, nk, ca, cb, exchange, operands):
    def dot(a_ref, w_ref):
        return _bdot(a_ref[...], w_ref[0] if len(w_ref.shape) == 3 else w_ref[...], ca, cb)

    def store(o_ref, val):
        if len(o_ref.shape) == 3:
            o_ref[0] = val.astype(o_ref.dtype)
        else:
            o_ref[...] = val.astype(o_ref.dtype)

    if nk == 1:
        def body(a_ref, w_ref, o_ref):
            store(o_ref, dot(a_ref, w_ref))

        scratch = []
    else:
        def body(a_ref, w_ref, o_ref, acc):
            kk = pl.program_id(2)

            @pl.when(kk == 0)
            def _():
                acc[...] = jnp.zeros_like(acc)

            acc[...] += dot(a_ref, w_ref)

            @pl.when(kk == nk - 1)
            def _():
                store(o_ref, acc[...])

        scratch = [pltpu.VMEM(tile, f32)]
    return _call(body, name, grid, in_specs, out_spec, out_shape, scratch, exchange=exchange)(*operands)


def mm_nn(a, w3, out_dtype, name, exchange=None):
    m, k = a.shape
    nblk, k2, nb = w3.shape
    assert k == k2
    tm, tk, tn = _tile(m, _BIG), _tile(k, _DEEP), _col_tile(nb)
    per, nk = nb // tn, k // tk
    return _mm_call(name, (m // tm, nblk * per, nk),
                    [pl.BlockSpec((tm, tk), lambda i, j, kk: (i, kk)),
                     pl.BlockSpec((1, tk, tn), lambda i, j, kk: (j // per, kk, j % per))],
                    pl.BlockSpec((tm, tn), lambda i, j, kk: (i, j)), _sds((m, nblk * nb), out_dtype),
                    (tm, tn), nk, 1, 0, exchange, (a, w3))


def mm_nt(a, w3, out_dtype, name, exchange=None):
    m, n = a.shape
    nblk, k, nb = w3.shape
    assert n == nblk * nb
    tm, tn = _tile(m, _BIG), _tile(k, _BIG)
    tc = nb if nb <= 1536 else _tile(nb, _DEEP)
    per = nb // tc
    nk = nblk * per
    return _mm_call(name, (m // tm, k // tn, nk),
                    [pl.BlockSpec((tm, tc), lambda i, j, kk: (i, kk)),
                     pl.BlockSpec((1, tn, tc), lambda i, j, kk: (kk // per, j, kk % per))],
                    pl.BlockSpec((tm, tn), lambda i, j, kk: (i, j)), _sds((m, k), out_dtype),
                    (tm, tn), nk, 1, 1, exchange, (a, w3))


def mm_tn(a, b, nblk, name, exchange=None):
    r, m = a.shape
    r2, n = b.shape
    assert r == r2 and n % nblk == 0
    nb = n // nblk
    tm, tk, tn = _tile(m, _BIG), _tile(r, _DEEP), _col_tile(nb)
    per, nk = nb // tn, r // tk
    return _mm_call(name, (m // tm, nblk * per, nk),
                    [pl.BlockSpec((tk, tm), lambda i, j, kk: (kk, i)),
                     pl.BlockSpec((tk, tn), lambda i, j, kk: (kk, j))],
                    pl.BlockSpec((1, tm, tn), lambda i, j, kk: (j // per, i, j % per)), _sds((nblk, m, nb), bf16),
                    (tm, tn), nk, 0, 0, exchange, (a, b))


def _rms(x, g):
    return x * lax.rsqrt(jnp.mean(x * x, axis=-1, keepdims=True) + EPS) * g


def _rms_pad(x, g, width):
    return x * lax.rsqrt(jnp.sum(x * x, axis=-1, keepdims=True) / width + EPS) * g


def _first(*ids):
    ok = ids[0] == 0
    for i in ids[1:]:
        ok = jnp.logical_and(ok, i == 0)
    return ok


def _acc_row(ref, val, first):
    @pl.when(first)
    def _():
        ref[...] = jnp.zeros_like(ref)

    ref[0:1, :] += val


def rms_fwd(x, g, name):
    r, w = x.shape
    tm = _tile(r, (256, 128, 64, 32, 16, 8))

    def body(x_ref, g_ref, o_ref):
        o_ref[...] = _rms(x_ref[...], g_ref[...]).astype(bf16)

    return _call(body, name, (r // tm,), [pl.BlockSpec((tm, w), lambda i: (i, 0)), pl.BlockSpec((1, w), lambda i: (0, 0))],
                 pl.BlockSpec((tm, w), lambda i: (i, 0)), _sds((r, w), bf16))(x, g)


def resid_rms(xa, xb, g, name):
    r, w = xa.shape
    tm = _tile(r, (256, 128, 64, 32, 16, 8))

    def body(a_ref, b_ref, g_ref, s_ref, u_ref):
        xs = a_ref[...] + b_ref[...]
        s_ref[...] = xs
        u_ref[...] = _rms(xs, g_ref[...]).astype(bf16)

    row = pl.BlockSpec((tm, w), lambda i: (i, 0))
    return _call(body, name, (r // tm,), [row, row, pl.BlockSpec((1, w), lambda i: (0, 0))], [row, row],
                 [_sds((r, w), f32), _sds((r, w), bf16)])(xa, xb, g)


def rms_bwd(x, g, dys, dres, name, want_dx=True, want_b16=False):
    r, w = x.shape
    tm = _tile(r, (256, 128, 64, 32, 16, 8))
    nd = len(dys)

    def body(*refs):
        x_ref, g_ref = refs[0], refs[1]
        dy = refs[2][...]
        for j in range(1, nd):
            dy = dy + refs[2 + j][...]
        pos = 2 + nd
        _, vjp = jax.vjp(_rms, x_ref[...], g_ref[...])
        dx, dg = vjp(dy)
        if dres is not None:
            dx = dx + refs[pos][...]
            pos += 1
        if want_dx:
            refs[pos][...] = dx
            pos += 1
        if want_b16:
            refs[pos][...] = dx.astype(bf16)
            pos += 1
        _acc_row(refs[pos], dg, pl.program_id(0) == 0)

    row = pl.BlockSpec((tm, w), lambda i: (i, 0))
    ins = [x, g] + list(dys) + ([dres] if dres is not None else [])
    in_specs = [row, pl.BlockSpec((1, w), lambda i: (0, 0))] + [row] * (nd + (dres is not None))
    out_specs = [row] * (want_dx + want_b16) + [pl.BlockSpec((8, w), lambda i: (0, 0))]
    out_shape = ([_sds((r, w), f32)] if want_dx else []) + ([_sds((r, w), bf16)] if want_b16 else []) + [_sds((8, w), f32)]
    return _call(body, name, (r // tm,), in_specs, out_specs, out_shape)(*ins)


def lat_norm(z_main, g_qa, g_kva):
    t = z_main.shape[0]
    tm = _tile(t, (512, 256, 128, 64))

    def body(z_ref, gq_ref, gk_ref, q_ref, k_ref):
        q_ref[...] = _rms(z_ref[:, :Q_LORA], gq_ref[...]).astype(bf16)
        k_ref[...] = _rms(z_ref[:, Q_LORA:], gk_ref[...]).astype(bf16)

    return _call(body, "lat_norm", (t // tm,),
                 [pl.BlockSpec((tm, Q_LORA + KV_LORA), lambda i: (i, 0)), pl.BlockSpec((1, Q_LORA), lambda i: (0, 0)),
                  pl.BlockSpec((1, KV_LORA), lambda i: (0, 0))],
                 [pl.BlockSpec((tm, Q_LORA), lambda i: (i, 0)), pl.BlockSpec((tm, KV_LORA), lambda i: (i, 0))],
                 [_sds((t, Q_LORA), bf16), _sds((t, KV_LORA), bf16)])(z_main, g_qa, g_kva)


def lat_norm_bwd(z_main, g_qa, g_kva, dqa, dkv):
    t = z_main.shape[0]
    tm = _tile(t, (512, 256, 128, 64))

    def body(z_ref, gq_ref, gk_ref, dq_ref, dk_ref, dz_ref, dgq_ref, dgk_ref):
        first = pl.program_id(0) == 0
        _, vq = jax.vjp(_rms, z_ref[:, :Q_LORA], gq_ref[...])
        dx, dg = vq(dq_ref[...])
        dz_ref[:, :Q_LORA] = dx.astype(bf16)
        _acc_row(dgq_ref, dg, first)
        _, vk = jax.vjp(_rms, z_ref[:, Q_LORA:], gk_ref[...])
        dx, dg = vk(dk_ref[...])
        dz_ref[:, Q_LORA:] = dx.astype(bf16)
        _acc_row(dgk_ref, dg, first)

    return _call(body, "lat_norm_bwd", (t // tm,),
                 [pl.BlockSpec((tm, Q_LORA + KV_LORA), lambda i: (i, 0)), pl.BlockSpec((1, Q_LORA), lambda i: (0, 0)),
                  pl.BlockSpec((1, KV_LORA), lambda i: (0, 0)), pl.BlockSpec((tm, Q_LORA), lambda i: (i, 0)),
                  pl.BlockSpec((tm, KV_LORA), lambda i: (i, 0))],
                 [pl.BlockSpec((tm, Q_LORA + KV_LORA), lambda i: (i, 0)), pl.BlockSpec((8, Q_LORA), lambda i: (0, 0)),
                  pl.BlockSpec((8, KV_LORA), lambda i: (0, 0))],
                 [_sds((t, Q_LORA + KV_LORA), bf16), _sds((8, Q_LORA), f32), _sds((8, KV_LORA), f32)])(z_main, g_qa, g_kva, dqa, dkv)


def rope_tables(pos_col, inv_freq):
    t = pos_col.shape[0]
    tm = _tile(t, (512, 256, 128, 64))

    def body(p_ref, f_ref, c_ref, s_ref):
        ang = p_ref[...].astype(f32) * f_ref[...]
        lane = lax.broadcasted_iota(jnp.int32, ang.shape, 1)
        c_ref[...] = jnp.where(lane < ROPE, jnp.cos(ang), 0.0)
        sn = jnp.sin(ang)
        s_ref[...] = jnp.where(lane < ROPE // 2, -sn, jnp.where(lane < ROPE, sn, 0.0))

    return _call(body, "rope_tables", (t // tm,),
                 [pl.BlockSpec((tm, 1), lambda i: (i, 0)), pl.BlockSpec((1, TAIL), lambda i: (0, 0))],
                 [pl.BlockSpec((tm, TAIL), lambda i: (i, 0))] * 2, [_sds((t, TAIL), f32)] * 2)(pos_col, inv_freq)


def _swap_halves(n):
    lane = lax.broadcasted_iota(jnp.int32, n.shape, 1)
    return jnp.where(lane < ROPE // 2, pltpu.roll(n, TAIL - ROPE // 2, 1), pltpu.roll(n, ROPE // 2, 1))


def _rope(n, c, s):
    return n * c + _swap_halves(n) * s


def _rope_t(d, c, s):
    return d * c + _swap_halves(d * s)


def _prep_specs(tm):
    head = pl.BlockSpec((tm, HEAD_PAD), lambda i, h: (i, h))
    row = pl.BlockSpec((tm, TAIL), lambda i, h: (i, 0))
    gain = pl.BlockSpec((1, TAIL), lambda i, h: (0, 0))
    return head, row, gain


def _pe_in(zt):
    lane = lax.broadcasted_iota(jnp.int32, zt.shape, 1)
    return jnp.where(lane < ROPE, zt, 0.0)


def mla_prep(q_raw, kv_raw, z_tail, cos, sin, gqn, gqp, gkn, gkp):
    t = q_raw.shape[0]
    tm = _tile(t, (1024, 512, 256, 128, 64))

    def body(q_ref, kv_ref, zt_ref, c_ref, s_ref, gqn_ref, gqp_ref, gkn_ref, gkp_ref, qh_ref, kh_ref, vh_ref):
        c, s = c_ref[...], s_ref[...]
        qh_ref[:, :NOPE] = _rms(q_ref[:, :NOPE], gqn_ref[...]).astype(bf16)
        qh_ref[:, NOPE:] = _rope(_rms_pad(q_ref[:, NOPE:], gqp_ref[...], ROPE), c, s).astype(bf16)
        kh_ref[:, :NOPE] = _rms(kv_ref[:, :NOPE], gkn_ref[...]).astype(bf16)
        kh_ref[:, NOPE:] = _rope(_rms_pad(_pe_in(zt_ref[...]), gkp_ref[...], ROPE), c, s).astype(bf16)
        vh_ref[...] = kv_ref[:, NOPE:].astype(bf16)

    head, row, gain = _prep_specs(tm)
    return _call(body, "mla_prep", (t // tm, MLA_HEADS), [head, head, row, row, row, gain, gain, gain, gain],
                 [head, head, pl.BlockSpec((tm, V_HEAD), lambda i, h: (i, h))],
                 [_sds((t, MLA_HEADS * HEAD_PAD), bf16), _sds((t, MLA_HEADS * HEAD_PAD), bf16), _sds((t, MLA_HEADS * V_HEAD), bf16)],
                 )(q_raw, kv_raw, z_tail, cos, sin, gqn, gqp, gkn, gkp)


def mla_prep_bwd(q_raw, kv_raw, z_tail, cos, sin, gqn, gqp, gkn, gkp, dqh, dkh, dvh):
    t = q_raw.shape[0]
    tm = _tile(t, (1024, 512, 256, 128, 64))
    pad_norm = functools.partial(_rms_pad, width=ROPE)

    def body(q_ref, kv_ref, zt_ref, c_ref, s_ref, gqn_ref, gqp_ref, gkn_ref, gkp_ref, dqh_ref, dkh_ref, dvh_ref,
             dq_ref, dkv_ref, dzt_ref, dgqn_ref, dgqp_ref, dgkn_ref, dgkp_ref):
        i, h = pl.program_id(0), pl.program_id(1)
        first = _first(i, h)
        c, s = c_ref[...], s_ref[...]
        _, v1 = jax.vjp(_rms, q_ref[:, :NOPE], gqn_ref[...])
        dx, dg = v1(dqh_ref[:, :NOPE])
        dq_ref[:, :NOPE] = dx.astype(bf16)
        _acc_row(dgqn_ref, dg, first)
        _, v2 = jax.vjp(pad_norm, q_ref[:, NOPE:], gqp_ref[...])
        dx, dg = v2(_rope_t(dqh_ref[:, NOPE:], c, s))
        dq_ref[:, NOPE:] = dx.astype(bf16)
        _acc_row(dgqp_ref, dg, first)
        _, v3 = jax.vjp(_rms, kv_ref[:, :NOPE], gkn_ref[...])
        dx, dg = v3(dkh_ref[:, :NOPE])
        dkv_ref[:, :NOPE] = dx.astype(bf16)
        _acc_row(dgkn_ref, dg, first)
        dkv_ref[:, NOPE:] = dvh_ref[...].astype(bf16)
        _, v4 = jax.vjp(pad_norm, _pe_in(zt_ref[...]), gkp_ref[...])
        dx, dg = v4(_rope_t(dkh_ref[:, NOPE:], c, s))
        _acc_row(dgkp_ref, dg, first)

        @pl.when(h == 0)
        def _():
            dzt_ref[...] = jnp.zeros_like(dzt_ref)

        dzt_ref[...] += dx

    head, row, gain = _prep_specs(tm)
    acc = pl.BlockSpec((8, TAIL), lambda i, h: (0, 0))
    vspec = pl.BlockSpec((tm, V_HEAD), lambda i, h: (i, h))
    return _call(body, "mla_prep_bwd", (t // tm, MLA_HEADS),
                 [head, head, row, row, row, gain, gain, gain, gain, head, head, vspec],
                 [head, head, row, acc, acc, acc, acc],
                 [_sds((t, MLA_HEADS * HEAD_PAD), bf16), _sds((t, MLA_HEADS * HEAD_PAD), bf16), _sds((t, TAIL), f32)]
                 + [_sds((8, TAIL), f32)] * 4)(q_raw, kv_raw, z_tail, cos, sin, gqn, gqp, gkn, gkp, dqh, dkh, dvh)


ATT_BLOCK = 512
NEG = -1e30
ATT_SCALE = (NOPE + ROPE) ** -0.5
ATT_HEADS = 2
ATT_HEADS_FWD = 4


def _chunk_visible(shape, key_axis):
    kc = lax.broadcasted_iota(jnp.int32, shape, key_axis) >> CHUNK_SHIFT
    qc = lax.broadcasted_iota(jnp.int32, shape, 1 - key_axis) >> CHUNK_SHIFT
    return kc <= qc


def mla_fwd(qh, kh, vh, exchange=None):
    t = qh.shape[0]
    tb = min(ATT_BLOCK, t)
    nb = t // tb

    hp = ATT_HEADS_FWD

    def body(q_ref, k_ref, v_ref, o_ref, ob_ref, lse_ref, m_s, l_s, acc):
        qi, ki = pl.program_id(1), pl.program_id(2)

        @pl.when(ki == 0)
        def _():
            m_s[...] = jnp.full_like(m_s, NEG)
            l_s[...] = jnp.zeros_like(l_s)
            acc[...] = jnp.zeros_like(acc)

        def step(diagonal):
            new = []
            for j in range(hp):
                q, k = q_ref[:, j * HEAD_PAD:(j + 1) * HEAD_PAD], k_ref[:, j * HEAD_PAD:(j + 1) * HEAD_PAD]
                s = _bdot(k, q, 1, 1) * ATT_SCALE
                if diagonal:
                    s = jnp.where(_chunk_visible(s.shape, 0), s, -jnp.inf)
                m_old = m_s[j]
                m_new = jnp.maximum(m_old, jnp.max(s, axis=0, keepdims=True))
                p = jnp.exp(s - m_new)
                alpha = jnp.exp(m_old - m_new)
                l_new = alpha * l_s[j] + jnp.sum(p, axis=0, keepdims=True)
                acc_new = alpha * acc[j] + _bdot(v_ref[:, j * V_HEAD:(j + 1) * V_HEAD], p, 0, 0)
                new.append((m_new, l_new, acc_new))
            for j, (m_new, l_new, acc_new) in enumerate(new):
                m_s[j] = m_new
                l_s[j] = l_new
                acc[j] = acc_new
            return new

        @pl.when(ki < qi)
        def _():
            step(False)

        @pl.when(ki == qi)
        def _():
            for j, (m_new, l_new, acc_new) in enumerate(step(True)):
                o = (acc_new / l_new).T
                o_ref[:, j * V_HEAD:(j + 1) * V_HEAD] = o
                ob_ref[:, j * V_HEAD:(j + 1) * V_HEAD] = o.astype(bf16)
                lse_ref[j] = m_new + jnp.log(l_new)

    kv = lambda g, qi, ki: (jnp.minimum(ki, qi), g)
    o_spec = pl.BlockSpec((tb, hp * V_HEAD), lambda g, qi, ki: (qi, g))
    return _call(body, "mla_fwd", (MLA_HEADS // hp, nb, nb),
                 [pl.BlockSpec((tb, hp * HEAD_PAD), lambda g, qi, ki: (qi, g)), pl.BlockSpec((tb, hp * HEAD_PAD), kv),
                  pl.BlockSpec((tb, hp * V_HEAD), kv)],
                 [o_spec, o_spec, pl.BlockSpec((hp, 1, tb), lambda g, qi, ki: (g, 0, qi))],
                 [_sds((t, MLA_HEADS * V_HEAD), f32), _sds((t, MLA_HEADS * V_HEAD), bf16), _sds((MLA_HEADS, 1, t), f32)],
                 [pltpu.VMEM((hp, 1, tb), f32), pltpu.VMEM((hp, 1, tb), f32), pltpu.VMEM((hp, V_HEAD, tb), f32)],
                 exchange=exchange)(qh, kh, vh)


def mla_delta(o, do):
    t = o.shape[0]
    tm = _tile(t, (512, 256, 128, 64))

    def body(o_ref, do_ref, d_ref):
        for h in range(MLA_HEADS):
            cols = slice(h * V_HEAD, (h + 1) * V_HEAD)
            d_ref[h] = jnp.sum(o_ref[:, cols] * do_ref[:, cols], axis=1, keepdims=True)

    blk = pl.BlockSpec((tm, MLA_HEADS * V_HEAD), lambda i: (i, 0))
    return _call(body, "mla_delta", (t // tm,), [blk, blk], pl.BlockSpec((MLA_HEADS, tm, 1), lambda i: (0, i, 0)),
                 _sds((MLA_HEADS, t, 1), f32))(o, do)


def mla_bwd(qh, kh, vh, do, lse_row, delta_row, exchange=None):
    t = qh.shape[0]
    tb = min(ATT_BLOCK, t)
    nb = t // tb

    hp = ATT_HEADS

    def body(q_ref, k_ref, v_ref, do_ref, lse_ref, dl_ref, dq_ref, dk_ref, dv_ref, dk_acc, dv_acc):
        ki, qi = pl.program_id(1), pl.program_id(2)

        @pl.when(jnp.logical_and(ki == 0, qi == 0))
        def _():
            dq_ref[...] = jnp.zeros_like(dq_ref)

        @pl.when(qi == 0)
        def _():
            dk_acc[...] = jnp.zeros_like(dk_acc)
            dv_acc[...] = jnp.zeros_like(dv_acc)

        def step(diagonal):
            rows = pl.ds(pl.multiple_of(qi * tb, tb), tb)
            new = []
            for j in range(hp):
                qc, vc = slice(j * HEAD_PAD, (j + 1) * HEAD_PAD), slice(j * V_HEAD, (j + 1) * V_HEAD)
                q, k, do_b = q_ref[:, qc], k_ref[:, qc], do_ref[:, vc]
                s = _bdot(k, q, 1, 1) * ATT_SCALE
                if diagonal:
                    s = jnp.where(_chunk_visible(s.shape, 0), s, -jnp.inf)
                p = jnp.exp(s - lse_ref[j])
                dp = _bdot(v_ref[:, vc], do_b, 1, 1)
                ds = p * (dp - dl_ref[j]) * ATT_SCALE
                new.append((dv_acc[:, vc] + _bdot(p, do_b, 1, 0), dk_acc[:, qc] + _bdot(ds, q, 1, 0),
                            dq_ref[rows, qc] + _bdot(ds, k, 0, 0)))
            for j, (dv, dk, dq) in enumerate(new):
                dv_acc[:, j * V_HEAD:(j + 1) * V_HEAD] = dv
                dk_acc[:, j * HEAD_PAD:(j + 1) * HEAD_PAD] = dk
                dq_ref[rows, j * HEAD_PAD:(j + 1) * HEAD_PAD] = dq

        @pl.when(qi > ki)
        def _():
            step(False)

        @pl.when(qi == ki)
        def _():
            step(True)

        @pl.when(qi == nb - 1)
        def _():
            dk_ref[...] = dk_acc[...]
            dv_ref[...] = dv_acc[...]

    qs = lambda g, ki, qi: (jnp.maximum(qi, ki), g)
    ks = lambda g, ki, qi: (ki, g)
    vec = pl.BlockSpec((hp, 1, tb), lambda g, ki, qi: (g, 0, jnp.maximum(qi, ki)))
    return _call(body, "mla_bwd", (MLA_HEADS // hp, nb, nb),
                 [pl.BlockSpec((tb, hp * HEAD_PAD), qs), pl.BlockSpec((tb, hp * HEAD_PAD), ks), pl.BlockSpec((tb, hp * V_HEAD), ks),
                  pl.BlockSpec((tb, hp * V_HEAD), qs), vec, vec],
                 [pl.BlockSpec((t, hp * HEAD_PAD), lambda g, ki, qi: (0, g)), pl.BlockSpec((tb, hp * HEAD_PAD), ks),
                  pl.BlockSpec((tb, hp * V_HEAD), ks)],
                 [_sds((t, MLA_HEADS * HEAD_PAD), f32), _sds((t, MLA_HEADS * HEAD_PAD), f32), _sds((t, MLA_HEADS * V_HEAD), f32)],
                 [pltpu.VMEM((tb, hp * HEAD_PAD), f32), pltpu.VMEM((tb, hp * V_HEAD), f32)], exchange=exchange)(
        qh, kh, vh, do, lse_row, delta_row)


PAD = 8


def _conv_taps(pad_ref, w, width, t):
    y = pad_ref[PAD - width + 1:PAD - width + 1 + t, :] * w[0:1, :]
    for j in range(1, width):
        y = y + pad_ref[PAD - width + 1 + j:PAD - width + 1 + j + t, :] * w[j:j + 1, :]
    return y


def _conv_bwd(xpad_ref, dpad_ref, w, da, width, t):
    dpad_ref[0:t, :] = da
    dpad_ref[t:t + PAD, :] = jnp.zeros((PAD, da.shape[1]), f32)
    dx = dpad_ref[width - 1:width - 1 + t, :] * w[0:1, :]
    for j in range(1, width):
        dx = dx + dpad_ref[width - 1 - j:width - 1 - j + t, :] * w[j:j + 1, :]
    dws = [jnp.sum(da * xpad_ref[PAD - width + 1 + j:PAD - width + 1 + j + t, :], axis=0, keepdims=True) for j in range(width)]
    return dx, dws


def _load_pad(pad_ref, x, t):
    pad_ref[0:PAD, :] = jnp.zeros((PAD, x.shape[1]), f32)
    pad_ref[PAD:PAD + t, :] = x


assert ML_DK == 128


def qk_conv(z_main, conv_qk):
    t = z_main.shape[0]
    base = O_Q // ML_DK

    def body(z_ref, w_ref, o_ref, pad):
        _load_pad(pad, z_ref[...], t)
        a = _conv_taps(pad, w_ref[...], ML_CONV, t)
        sc = jnp.where(pl.program_id(0) < ML_HEADS, ML_DK ** -0.5, 1.0)
        o_ref[0] = jax.nn.silu(a) * sc

    return _call(body, "qk_conv", (2 * ML_HEADS,),
                 [pl.BlockSpec((t, ML_DK), lambda j: (0, base + j)), pl.BlockSpec((ML_CONV, ML_DK), lambda j: (0, j))],
                 pl.BlockSpec((1, t, ML_DK), lambda j: (j, 0, 0)), _sds((2 * ML_HEADS, t, ML_DK), f32),
                 [pltpu.VMEM((t + PAD, ML_DK), f32)])(z_main, conv_qk)


def qk_conv_bwd(z_main, conv_qk, dq, dk):
    t = z_main.shape[0]
    base = O_Q // ML_DK

    def body(z_ref, w_ref, dq_ref, dk_ref, dz_ref, dw_ref, pad, dpad):
        _load_pad(pad, z_ref[...], t)
        w = w_ref[...]
        a = _conv_taps(pad, w, ML_CONV, t)
        is_q = pl.program_id(0) < ML_HEADS
        d = jnp.where(is_q, dq_ref[0] * (ML_DK ** -0.5), dk_ref[0])
        _, vjp = jax.vjp(jax.nn.silu, a)
        da, = vjp(d)
        dx, dws = _conv_bwd(pad, dpad, w, da, ML_CONV, t)
        dz_ref[...] = dx.astype(bf16)
        for j in range(ML_CONV):
            dw_ref[j:j + 1, :] = dws[j]

    head = lambda pick: pl.BlockSpec((1, t, ML_DK), lambda j: (pick(j), 0, 0))
    return _call(body, "qk_conv_bwd", (2 * ML_HEADS,),
                 [pl.BlockSpec((t, ML_DK), lambda j: (0, base + j)), pl.BlockSpec((ML_CONV, ML_DK), lambda j: (0, j)),
                  head(lambda j: jnp.minimum(j, ML_HEADS - 1)), head(lambda j: jnp.maximum(j - ML_HEADS, 0))],
                 [pl.BlockSpec((t, ML_DK), lambda j: (0, j)), pl.BlockSpec((ML_CONV, ML_DK), lambda j: (0, j))],
                 [_sds((t, 2 * ML_QK), bf16), _sds((ML_CONV, 2 * ML_QK), f32)],
                 [pltpu.VMEM((t + PAD, ML_DK), f32), pltpu.VMEM((t + PAD, ML_DK), f32)])(z_main, conv_qk, dq, dk)


def glu_fwd(hup, conv_w, bias):
    t, f2 = hup.shape
    nf = f2 // 2 // 128

    def body(h1_ref, h2_ref, w1_ref, w2_ref, b1_ref, b2_ref, o_ref, pad):
        _load_pad(pad, h1_ref[...], t)
        a1 = _conv_taps(pad, w1_ref[...], FFN_CONV, t) + b1_ref[...]
        _load_pad(pad, h2_ref[...], t)
        a2 = _conv_taps(pad, w2_ref[...], FFN_CONV, t) + b2_ref[...]
        o_ref[...] = (jax.nn.silu(a1) * a2).astype(bf16)

    col = lambda off: pl.BlockSpec((t, 128), lambda j: (0, j + off))
    wsp = lambda off: pl.BlockSpec((FFN_CONV, 128), lambda j: (0, j + off))
    bsp = lambda off: pl.BlockSpec((1, 128), lambda j: (0, j + off))
    return _call(body, "glu_fwd", (nf,), [col(0), col(nf), wsp(0), wsp(nf), bsp(0), bsp(nf)], col(0), _sds((t, f2 // 2), bf16),
                 [pltpu.VMEM((t + PAD, 128), f32)])(hup, hup, conv_w, conv_w, bias, bias)


def glu_bwd(hup, conv_w, bias, dg, exchange=None):
    t, f2 = hup.shape
    f = f2 // 2
    nf = f // 128

    def body(h1_ref, h2_ref, w1_ref, w2_ref, b1_ref, b2_ref, dg_ref, dh1_ref, dh2_ref, dw1_ref, dw2_ref, db1_ref, db2_ref,
             pad1, pad2, dpad):
        _load_pad(pad1, h1_ref[...], t)
        _load_pad(pad2, h2_ref[...], t)
        w1, w2 = w1_ref[...], w2_ref[...]
        a1 = _conv_taps(pad1, w1, FFN_CONV, t) + b1_ref[...]
        a2 = _conv_taps(pad2, w2, FFN_CONV, t) + b2_ref[...]
        d = dg_ref[...]
        _, vjp = jax.vjp(jax.nn.silu, a1)
        da1, = vjp(d * a2)
        da2 = d * jax.nn.silu(a1)
        for da, pad, w, dh_ref, dw_ref, db_ref in ((da1, pad1, w1, dh1_ref, dw1_ref, db1_ref), (da2, pad2, w2, dh2_ref, dw2_ref, db2_ref)):
            dx, dws = _conv_bwd(pad, dpad, w, da, FFN_CONV, t)
            dh_ref[...] = dx.astype(bf16)
            for j in range(FFN_CONV):
                dw_ref[j:j + 1, :] = dws[j]
            db_ref[...] = jnp.sum(da, axis=0, keepdims=True)

    col = lambda off: pl.BlockSpec((t, 128), lambda j: (0, j + off))
    wsp = lambda off: pl.BlockSpec((FFN_CONV, 128), lambda j: (0, j + off))
    bsp = lambda off: pl.BlockSpec((1, 128), lambda j: (0, j + off))
    return _call(body, "glu_bwd", (nf,), [col(0), col(nf), wsp(0), wsp(nf), bsp(0), bsp(nf), col(0)],
                 [col(0), col(0), wsp(0), wsp(0), bsp(0), bsp(0)],
                 [_sds((t, f), bf16)] * 2 + [_sds((FFN_CONV, f), f32)] * 2 + [_sds((1, f), f32)] * 2,
                 [pltpu.VMEM((t + PAD, 128), f32)] * 3, exchange=exchange)(hup, hup, conv_w, conv_w, bias, bias, dg)


def gate_act(z_tail, b_tile):
    t = z_tail.shape[0]
    tm = _tile(t, (512, 256, 128, 64))

    def body(z_ref, b_ref, o_ref):
        x = z_ref[...] + b_ref[...]
        lane = lax.broadcasted_iota(jnp.int32, x.shape, 1)
        o_ref[...] = jnp.where(lane < T_F, x, jax.nn.log_sigmoid(x))

    row = pl.BlockSpec((tm, TAIL), lambda i: (i, 0))
    return _call(body, "gate_act", (t // tm,), [row, pl.BlockSpec((1, TAIL), lambda i: (0, 0))], row, _sds((t, TAIL), f32))(z_tail, b_tile)


def tail_bwd(z_tail, b_tile, dzt_pe, dgate):
    t = z_tail.shape[0]
    tm = _tile(t, (512, 256, 128, 64))

    def body(z_ref, b_ref, dpe_ref, dg_ref, dz_ref, db_ref):
        x = z_ref[...] + b_ref[...]
        lane = lax.broadcasted_iota(jnp.int32, x.shape, 1)
        _, vjp = jax.vjp(jax.nn.log_sigmoid, x)
        df, = vjp(dg_ref[...])
        dgates = jnp.where(lane < T_F, dg_ref[...], df)
        dgates = jnp.where(jnp.logical_and(lane >= T_I, lane < T_F + ML_HEADS), dgates, 0.0)
        dz_ref[...] = jnp.where(lane < ROPE, dpe_ref[...], dgates).astype(bf16)
        _acc_row(db_ref, jnp.sum(dgates, axis=0, keepdims=True), pl.program_id(0) == 0)

    row = pl.BlockSpec((tm, TAIL), lambda i: (i, 0))
    return _call(body, "tail_bwd", (t // tm,), [row, pl.BlockSpec((1, TAIL), lambda i: (0, 0)), row, row],
                 [row, pl.BlockSpec((8, TAIL), lambda i: (0, 0))], [_sds((t, TAIL), bf16), _sds((8, TAIL), f32)])(z_tail, b_tile, dzt_pe, dgate)


def _hdot(a, b, ca, cb):
    return lax.dot_general(a.astype(bf16), b.astype(bf16), (((ca,), (cb,)), ((0,), (0,))), preferred_element_type=f32)


def _mlstm_step(q, k, v, igr, fgr, c_mat, n_vec, m):
    nh, ln = q.shape[0], CHUNK
    sq = (nh, ln, ln)
    row = lax.broadcasted_iota(jnp.int32, sq, 1)
    col = lax.broadcasted_iota(jnp.int32, sq, 2)
    eye = row == col

    def to_col(r):
        return jnp.sum(jnp.where(eye, jnp.broadcast_to(r, sq), 0.0), axis=2, keepdims=True)

    bc_r = jnp.sum(jnp.where(row <= col, jnp.broadcast_to(to_col(fgr), sq), 0.0), axis=1, keepdims=True)
    bc_c = to_col(bc_r)
    logw = jnp.where(col <= row, bc_c - bc_r + igr, -jnp.inf)
    inter = bc_c + m
    m_t = jnp.maximum(inter, jnp.max(logw, axis=2, keepdims=True))
    w_intra = jnp.exp(logw - m_t)
    w_inter = jnp.exp(inter - m_t)
    sc = _hdot(q, k, 2, 2) * w_intra
    num = w_inter * _hdot(q, c_mat, 2, 1) + _hdot(sc, v, 2, 1)
    qn = jnp.sum(q.astype(bf16).astype(f32) * n_vec.astype(bf16).astype(f32), axis=2, keepdims=True)
    den = w_inter * qn + jnp.sum(sc, axis=2, keepdims=True)
    h = num / jnp.maximum(jnp.abs(den), jnp.exp(-m_t))
    lane = lax.broadcasted_iota(jnp.int32, (nh, 1, ln), 2)
    b_last = jnp.sum(jnp.where(lane == ln - 1, bc_r, 0.0), axis=2, keepdims=True)
    logu = b_last - bc_r + igr
    m_new = jnp.maximum(b_last + m, jnp.max(logu, axis=2, keepdims=True))
    decay = jnp.exp(b_last + m - m_new)
    u_c = to_col(jnp.exp(logu - m_new))
    c_new = decay * c_mat + _hdot(u_c * k, v, 1, 1)
    n_new = decay * n_vec + jnp.sum(u_c.astype(bf16).astype(f32) * k.astype(bf16).astype(f32), axis=1, keepdims=True)
    return h, c_new, n_new, m_new


ML_VHALF = ML_V // 2
assert O_V % ML_VHALF == 0 and ML_HEADS % 2 == 0


def _ml_specs(nc, rev):
    cc = (lambda c: nc - 1 - c) if rev else (lambda c: c)
    q = pl.BlockSpec((ML_HEADS, CHUNK, ML_DK), lambda c: (0, cc(c), 0))
    k = pl.BlockSpec((ML_HEADS, CHUNK, ML_DK), lambda c: (1, cc(c), 0))
    v_lo = pl.BlockSpec((CHUNK, ML_VHALF), lambda c: (cc(c), O_V // ML_VHALF))
    v_hi = pl.BlockSpec((CHUNK, ML_VHALF), lambda c: (cc(c), O_V // ML_VHALF + 1))
    hv = pl.BlockSpec((ML_HEADS, CHUNK, ML_DV), lambda c: (0, cc(c), 0))
    gate = pl.BlockSpec((ML_HEADS, 1, 1, CHUNK), lambda c: (0, cc(c), 0, 0))
    cm = pl.BlockSpec((ML_HEADS, 1, ML_DK, ML_DV), lambda c: (0, cc(c), 0, 0))
    nv = pl.BlockSpec((ML_HEADS, 1, 1, ML_DK), lambda c: (0, cc(c), 0, 0))
    ms = pl.BlockSpec((ML_HEADS, 1, 1, 1), lambda c: (0, cc(c), 0, 0))
    return q, k, v_lo, v_hi, hv, gate, cm, nv, ms


_ML_STATE = [pltpu.VMEM((ML_HEADS, ML_DK, ML_DV), f32), pltpu.VMEM((ML_HEADS, 1, ML_DK), f32), pltpu.VMEM((ML_HEADS, 1, 1), f32)]


def _ml_zero_state(c_s, n_s, m_s):
    @pl.when(pl.program_id(0) == 0)
    def _():
        c_s[...] = jnp.zeros_like(c_s)
        n_s[...] = jnp.zeros_like(n_s)
        m_s[...] = jnp.zeros_like(m_s)


def _ml_heads_of(v_lo_ref, v_hi_ref):
    half = ML_HEADS // 2
    return jnp.stack([r[:, j * ML_DV:(j + 1) * ML_DV] for r in (v_lo_ref, v_hi_ref) for j in range(half)])


def mlstm_fwd(qk_act, z_main, ig, fg):
    t = qk_act.shape[1]
    nc = t // CHUNK

    def body(q_ref, k_ref, vl_ref, vh_ref, ig_ref, fg_ref, h_ref, c_out, n_out, m_out, c_s, n_s, m_s):
        _ml_zero_state(c_s, n_s, m_s)
        c0, n0, m0 = c_s[...], n_s[...], m_s[...]
        c_out[:, 0] = c0
        n_out[:, 0] = n0
        m_out[:, 0] = m0
        h, c2, n2, m2 = _mlstm_step(q_ref[...], k_ref[...], _ml_heads_of(vl_ref, vh_ref), ig_ref[:, 0], fg_ref[:, 0], c0, n0, m0)
        h_ref[...] = h
        c_s[...] = c2
        n_s[...] = n2
        m_s[...] = m2

    q, k, v_lo, v_hi, hv, gate, cm, nv, ms = _ml_specs(nc, False)
    return _call(body, "mlstm_fwd", (nc,), [q, k, v_lo, v_hi, gate, gate], [hv, cm, nv, ms],
                 [_sds((ML_HEADS, t, ML_DV), f32), _sds((ML_HEADS, nc, ML_DK, ML_DV), f32), _sds((ML_HEADS, nc, 1, ML_DK), f32),
                  _sds((ML_HEADS, nc, 1, 1), f32)], _ML_STATE)(qk_act, qk_act, z_main, z_main, ig, fg)


def mlstm_bwd(qk_act, z_main, ig, fg, c_all, n_all, m_all, dh, exchange=None):
    t = qk_act.shape[1]
    nc = t // CHUNK

    def body(q_ref, k_ref, vl_ref, vh_ref, ig_ref, fg_ref, c_ref, n_ref, m_ref, dh_ref, dq_ref, dk_ref, dv_ref, dig_ref, dfg_ref,
             dc_s, dn_s, dm_s):
        _ml_zero_state(dc_s, dn_s, dm_s)
        _, vjp = jax.vjp(_mlstm_step, q_ref[...], k_ref[...], _ml_heads_of(vl_ref, vh_ref), ig_ref[:, 0], fg_ref[:, 0],
                         c_ref[:, 0], n_ref[:, 0], m_ref[:, 0])
        dq, dk, dv, dig, dfg, dc, dn, dm = vjp((dh_ref[...], dc_s[...], dn_s[...], dm_s[...]))
        dq_ref[...] = dq
        dk_ref[...] = dk
        for j in range(ML_HEADS):
            dv_ref[:, j * ML_DV:(j + 1) * ML_DV] = dv[j].astype(bf16)
        dig_ref[:, 0] = dig
        dfg_ref[:, 0] = dfg
        dc_s[...] = dc
        dn_s[...] = dn
        dm_s[...] = dm

    q, k, v_lo, v_hi, hv, gate, cm, nv, ms = _ml_specs(nc, True)
    gshape = _sds((ML_HEADS, nc, 1, CHUNK), f32)
    return _call(body, "mlstm_bwd", (nc,), [q, k, v_lo, v_hi, gate, gate, cm, nv, ms, hv],
                 [q, q, pl.BlockSpec((CHUNK, ML_V), lambda c: (nc - 1 - c, 0)), gate, gate],
                 [_sds((ML_HEADS, t, ML_DK), f32), _sds((ML_HEADS, t, ML_DK), f32), _sds((t, ML_V), bf16), gshape, gshape],
                 _ML_STATE, exchange=exchange)(qk_act, qk_act, z_main, z_main, ig, fg, c_all, n_all, m_all, dh)


def _ml_out(h, zo, g):
    return _rms(h, g) * jax.nn.sigmoid(zo)


def mlstm_out(h, z_main, g_hnorm):
    t = h.shape[1]
    tm = _tile(t, (512, 256, 128, 64))
    zo = O_O // ML_DV

    def body(h_ref, z_ref, g_ref, y_ref):
        y_ref[...] = _ml_out(h_ref[0], z_ref[...], g_ref[0]).astype(bf16)

    return _call(body, "mlstm_out", (t // tm, ML_HEADS),
                 [pl.BlockSpec((1, tm, ML_DV), lambda i, hd: (hd, i, 0)), pl.BlockSpec((tm, ML_DV), lambda i, hd: (i, zo + hd)),
                  pl.BlockSpec((1, 1, ML_DV), lambda i, hd: (hd, 0, 0))],
                 pl.BlockSpec((tm, ML_DV), lambda i, hd: (i, hd)), _sds((t, ML_V), bf16))(h, z_main, g_hnorm)


def mlstm_out_bwd(h, z_main, g_hnorm, dy):
    t = h.shape[1]
    tm = _tile(t, (512, 256, 128, 64))
    zo = O_O // ML_DV

    def body(h_ref, z_ref, g_ref, dy_ref, dh_ref, dzo_ref, dg_ref):
        _, vjp = jax.vjp(_ml_out, h_ref[0], z_ref[...], g_ref[0])
        dh, dz, dg = vjp(dy_ref[...])
        dh_ref[0] = dh
        dzo_ref[...] = dz.astype(bf16)

        @pl.when(pl.program_id(1) == 0)
        def _():
            dg_ref[...] = jnp.zeros_like(dg_ref)

        dg_ref[0, 0:1, :] += dg

    head = pl.BlockSpec((1, tm, ML_DV), lambda hd, i: (hd, i, 0))
    blk = pl.BlockSpec((tm, ML_DV), lambda hd, i: (i, hd))
    return _call(body, "mlstm_out_bwd", (ML_HEADS, t // tm),
                 [head, pl.BlockSpec((tm, ML_DV), lambda hd, i: (i, zo + hd)), pl.BlockSpec((1, 1, ML_DV), lambda hd, i: (hd, 0, 0)), blk],
                 [head, blk, pl.BlockSpec((1, 8, ML_DV), lambda hd, i: (hd, 0, 0))],
                 [_sds((ML_HEADS, t, ML_DV), f32), _sds((t, ML_V), bf16), _sds((ML_HEADS, 8, ML_DV), f32)])(h, z_main, g_hnorm, dy)


def _merge(ga, gb, ya, yb):
    return jax.nn.sigmoid(ga) * ya + jax.nn.sigmoid(gb) * yb


def _merge_specs(t, d):
    tm = _tile(t, (512, 256, 128, 64))
    bw = _tile(d, (512, 256, 128))
    assert O_GA % bw == 0 and (O_GA + d) % bw == 0
    blk = pl.BlockSpec((tm, bw), lambda i, j: (i, j))
    ga = pl.BlockSpec((tm, bw), lambda i, j: (i, O_GA // bw + j))
    gb = pl.BlockSpec((tm, bw), lambda i, j: (i, (O_GA + d) // bw + j))
    return tm, bw, blk, ga, gb


def merge_fwd(z_main, ya, yb):
    t, d = ya.shape
    tm, bw, blk, ga, gb = _merge_specs(t, d)

    def body(ga_ref, gb_ref, ya_ref, yb_ref, o_ref):
        o_ref[...] = _merge(ga_ref[...], gb_ref[...], ya_ref[...], yb_ref[...]).astype(bf16)

    return _call(body, "merge_fwd", (t // tm, d // bw), [ga, gb, blk, blk], blk, _sds((t, d), bf16))(z_main, z_main, ya, yb)


def merge_bwd(z_main, ya, yb, dmerged):
    t, d = ya.shape
    tm, bw, blk, ga, gb = _merge_specs(t, d)

    def body(ga_ref, gb_ref, ya_ref, yb_ref, dm_ref, dga_ref, dgb_ref, dya_ref, dyb_ref):
        _, vjp = jax.vjp(_merge, ga_ref[...], gb_ref[...], ya_ref[...], yb_ref[...])
        dga, dgb, dya, dyb = vjp(dm_ref[...])
        dga_ref[...] = dga.astype(bf16)
        dgb_ref[...] = dgb.astype(bf16)
        dya_ref[...] = dya.astype(bf16)
        dyb_ref[...] = dyb.astype(bf16)

    return _call(body, "merge_bwd", (t // tm, d // bw), [ga, gb, blk, blk, blk], [blk] * 4, [_sds((t, d), bf16)] * 4)(
        z_main, z_main, ya, yb, dmerged)


def _cross(cq, ck, cv, gq, gk):
    outs = []
    for hd in range(CR_HEADS):
        sl = slice(hd * CR_HD, (hd + 1) * CR_HD)
        q = _rms(cq[:, sl], gq)
        k = _rms(ck[:, sl], gk)
        s = _bdot(q, k, 1, 1) * (CR_HD ** -0.5)
        p = jax.nn.softmax(s, axis=-1)
        outs.append(_bdot(p, cv[:, sl], 1, 0))
    return jnp.concatenate(outs, axis=1)


def cross_fwd(cq, ck, cv, gq, gk):
    t, w = cq.shape
    nm = ck.shape[0]
    tm = _tile(t, (512, 256, 128, 64))

    def body(q_ref, k_ref, v_ref, gq_ref, gk_ref, o_ref):
        o_ref[...] = _cross(q_ref[...], k_ref[...], v_ref[...], gq_ref[...], gk_ref[...]).astype(bf16)

    row = pl.BlockSpec((tm, w), lambda i: (i, 0))
    full = pl.BlockSpec((nm, w), lambda i: (0, 0))
    gain = pl.BlockSpec((1, CR_HD), lambda i: (0, 0))
    return _call(body, "cross_fwd", (t // tm,), [row, full, full, gain, gain], row, _sds((t, w), bf16))(cq, ck, cv, gq, gk)


def cross_bwd(cq, ck, cv, gq, gk, do):
    t, w = cq.shape
    nm = ck.shape[0]
    tm = _tile(t, (512, 256, 128, 64))

    def body(q_ref, k_ref, v_ref, gq_ref, gk_ref, do_ref, dq_ref, dk_ref, dv_ref, dgq_ref, dgk_ref):
        first = pl.program_id(0) == 0
        _, vjp = jax.vjp(_cross, q_ref[...], k_ref[...], v_ref[...], gq_ref[...], gk_ref[...])
        dq, dk, dv, dgq, dgk = vjp(do_ref[...])
        dq_ref[...] = dq.astype(bf16)

        @pl.when(first)
        def _():
            dk_ref[...] = jnp.zeros_like(dk_ref)
            dv_ref[...] = jnp.zeros_like(dv_ref)

        dk_ref[...] += dk
        dv_ref[...] += dv
        _acc_row(dgq_ref, dgq, first)
        _acc_row(dgk_ref, dgk, first)

    row = pl.BlockSpec((tm, w), lambda i: (i, 0))
    full = pl.BlockSpec((nm, w), lambda i: (0, 0))
    gain = pl.BlockSpec((1, CR_HD), lambda i: (0, 0))
    acc = pl.BlockSpec((8, CR_HD), lambda i: (0, 0))
    return _call(body, "cross_bwd", (t // tm,), [row, full, full, gain, gain, row], [row, full, full, acc, acc],
                 [_sds((t, w), bf16), _sds((nm, w), f32), _sds((nm, w), f32), _sds((8, CR_HD), f32), _sds((8, CR_HD), f32)])(
        cq, ck, cv, gq, gk, do)


def loss_head(x2, fo, target):
    t, d = x2.shape
    tm = _tile(t, (256, 128, 64, 32, 16, 8))

    def body(a_ref, b_ref, t_ref, dx_ref, dxb_ref, l_ref):
        err = a_ref[...] + b_ref[...] - t_ref[...]
        dx = err / d
        dx_ref[...] = dx
        dxb_ref[...] = dx.astype(bf16)
        part = 0.5 * jnp.sum(jnp.mean(err * err, axis=1, keepdims=True), axis=0, keepdims=True)
        _acc_row(l_ref, jnp.broadcast_to(part, (1, 128)), pl.program_id(0) == 0)

    row = pl.BlockSpec((tm, d), lambda i: (i, 0))
    return _call(body, "loss_head", (t // tm,), [row, row, row], [row, row, pl.BlockSpec((8, 128), lambda i: (0, 0))],
                 [_sds((t, d), f32), _sds((t, d), bf16), _sds((8, 128), f32)])(x2, fo, target)


def _place():
    x, y, c = lax.axis_index("x"), lax.axis_index("y"), lax.axis_index("c")
    peers = {}
    for r in range(1, N_DEV):
        px = 1 - x if r & 4 else x
        py = 1 - y if r & 2 else y
        pc = 1 - c if r & 1 else c
        peers[r] = ((px, py, pc), 4 * px + 2 * py + pc)
    return 4 * x + 2 * y + c, peers


N_REL = N_DEV - 1
RELATIONS = tuple(range(1, N_DEV))
SIBLING = 1
OTHER_CHIPS = (2, 4, 6)
PASSED_ON = (3, 5, 7)


def _exchange_ops(ins, outs, sems, scatter):
    n = len(ins)
    send_sems, recv_sems, local_sems = sems

    def tools():
        me, peers = _place()

        def copy(a, r, src, dst_idx, to):
            return pltpu.make_async_remote_copy(
                src_ref=src, dst_ref=outs[a].at[dst_idx], send_sem=send_sems.at[a * N_REL + r - 1],
                recv_sem=recv_sems.at[a * N_REL + r - 1], device_id=peers[to][0], device_id_type=MESH)

        def local(a):
            return pltpu.make_async_copy(ins[a].at[me] if scatter else ins[a], outs[a].at[me], local_sems.at[a])

        def arrival(a, r):
            return copy(a, r, ins[a].at[me] if scatter else ins[a], peers[r][1], r)

        return me, peers, copy, local, arrival

    if scatter:
        def sends():
            me, peers, copy, local, _ = tools()
            return [local(a) for a in range(n)], [copy(a, r, ins[a].at[peers[r][1]], me, r) for a in range(n) for r in RELATIONS]

        def start():
            loc, out = sends()
            for cp in loc + out:
                cp.start()

        middle = None
        waited_last = RELATIONS
    else:
        def sends():
            me, peers, copy, local, _ = tools()
            own = [copy(a, r, ins[a], me, r) for a in range(n) for r in (SIBLING,) + OTHER_CHIPS]
            return [local(a) for a in range(n)], own

        def passes():
            me, peers, copy, _, _ = tools()
            return [copy(a, r, outs[a].at[peers[r - 1][1]], peers[r - 1][1], SIBLING) for a in range(n) for r in PASSED_ON]

        def start():
            loc, out = sends()
            for cp in loc + out:
                cp.start()

        def middle():
            _, _, _, _, arrival = tools()
            fwd = passes()
            for a in range(n):
                for i, r in enumerate(PASSED_ON):
                    arrival(a, r - 1).wait_recv()
                    fwd[a * len(PASSED_ON) + i].start()

        waited_last = (SIBLING,) + PASSED_ON

    def wait():
        _, _, _, _, arrival = tools()
        for a in range(n):
            for r in waited_last:
                arrival(a, r).wait_recv()
        loc, out = sends()
        for cp in out + ([] if scatter else passes()):
            cp.wait_send()
        for cp in loc:
            cp.wait()

    return start, middle, wait


def _exchange_shapes(arrs, scatter):
    return [_sds(a.shape if scatter else (N_DEV,) + a.shape, a.dtype) for a in arrs]


def _exchange_sems(n):
    return [pltpu.SemaphoreType.DMA((n * N_REL,)), pltpu.SemaphoreType.DMA((n * N_REL,)), pltpu.SemaphoreType.DMA((n,))]


def _exchange(arrs, name, scatter):
    n = len(arrs)

    def body(*refs):
        start, middle, wait = _exchange_ops(refs[:n], refs[n:2 * n], refs[2 * n:], scatter)
        start()
        if middle is not None:
            middle()
        wait()

    any_spec = pl.BlockSpec(memory_space=pl.ANY)
    return pl.pallas_call(body, name=name, in_specs=[any_spec] * n, out_specs=[any_spec] * n,
                          out_shape=_exchange_shapes(arrs, scatter), scratch_shapes=_exchange_sems(n))(*arrs)


def cast_bf16(w, name):
    _, r, c = w.shape
    tr = _tile(r, (256, 128, 64, 32, 16))

    def body(w_ref, o_ref):
        o_ref[...] = w_ref[0].astype(bf16)

    return _call(body, name, (r // tr,), [pl.BlockSpec((1, tr, c), lambda i: (0, i, 0))], pl.BlockSpec((tr, c), lambda i: (i, 0)),
                 _sds((r, c), bf16))(w)


def _adamw(w, g, m, v):
    m = ADAM_B1 * m + (1.0 - ADAM_B1) * g
    v = ADAM_B2 * v + (1.0 - ADAM_B2) * jnp.square(g)
    m_hat = m / (1.0 - ADAM_B1 ** ADAM_STEP)
    v_hat = v / (1.0 - ADAM_B2 ** ADAM_STEP)
    delta = -ADAM_LR * (m_hat / (jnp.sqrt(v_hat) + ADAM_EPS) + ADAM_WD * w)
    return delta, m, v


def adam_sum(parts, w, m, v, name):
    _, r, c = parts.shape
    budget = 4 * 1024 * 1024
    tr = r
    for cand in (1024, 512, 256, 128, 64, 32, 16):
        if r % cand == 0 and N_DEV * cand * c * 4 <= budget:
            tr = cand
            break

    def body(p_ref, w_ref, m_ref, v_ref, g_ref, d_ref, m2_ref, v2_ref):
        g = p_ref[0].astype(f32)
        for k in range(1, N_DEV):
            g = g + p_ref[k].astype(f32)
        d, m2, v2 = _adamw(w_ref[0], g, m_ref[0], v_ref[0])
        g_ref[...] = g
        d_ref[...] = d
        m2_ref[...] = m2
        v2_ref[...] = v2

    blk = pl.BlockSpec((1, tr, c), lambda i: (0, i, 0))
    out = pl.BlockSpec((tr, c), lambda i: (i, 0))
    return _call(body, name, (r // tr,), [pl.BlockSpec((N_DEV, tr, c), lambda i: (0, i, 0)), blk, blk, blk], [out] * 4,
                 [_sds((r, c), f32)] * 4)(parts, w, m, v)


def sum_parts(parts, name):
    _, r, c = parts.shape

    def body(p_ref, o_ref):
        g = p_ref[0]
        for k in range(1, N_DEV):
            g = g + p_ref[k]
        o_ref[...] = g

    return pl.pallas_call(body, name=name, out_shape=_sds((r, c), f32))(parts)


def adam_flat(w, g, m, v, name):
    def body(w_ref, g_ref, m_ref, v_ref, d_ref, m2_ref, v2_ref):
        d, m2, v2 = _adamw(w_ref[...], g_ref[...], m_ref[...], v_ref[...])
        d_ref[...] = d
        m2_ref[...] = m2
        v2_ref[...] = v2

    return pl.pallas_call(body, name=name, out_shape=[_sds(w.shape, f32)] * 3)(w, g, m, v)


def _pack(vecs, multiple):
    flat = jnp.concatenate([v.reshape(-1) for v in vecs])
    n = flat.shape[0]
    total = -(-n // multiple) * multiple
    return jnp.pad(flat, (0, total - n))


def _unpack(flat, shapes):
    out, pos = [], 0
    for s in shapes:
        n = 1
        for d in s:
            n *= d
        out.append(flat[pos:pos + n].reshape(s))
        pos += n
    return out


def _pad_lanes(v, width=TAIL):
    return jnp.pad(v, ((0, 0), (0, width - v.shape[1])))


def kernel(x, mem, positions, g_mix, w_in, g_qa, w_qb, g_kva, w_kvb, g_qn_nope, g_qn_pe, g_kn_nope, g_kn_pe, conv_qk, b_if, g_hnorm, p_a, p_b, w_out, g_cross, g_mem, wq_c, wk_c, wv_c, g_cq, g_ck, wo_c, g_ffn, w_up, conv_ffn, b_conv_ffn, w_down, loss_target, m_g_mix, m_w_in, m_g_qa, m_w_qb, m_g_kva, m_w_kvb, m_g_qn_nope, m_g_qn_pe, m_g_kn_nope, m_g_kn_pe, m_conv_qk, m_b_if, m_g_hnorm, m_p_a, m_p_b, m_w_out, m_g_cross, m_g_mem, m_wq_c, m_wk_c, m_wv_c, m_g_cq, m_g_ck, m_wo_c, m_g_ffn, m_w_up, m_conv_ffn, m_b_conv_ffn, m_w_down, v_g_mix, v_w_in, v_g_qa, v_w_qb, v_g_kva, v_w_kvb, v_g_qn_nope, v_g_qn_pe, v_g_kn_nope, v_g_kn_pe, v_conv_qk, v_b_if, v_g_hnorm, v_p_a, v_p_b, v_w_out, v_g_cross, v_g_mem, v_wq_c, v_wk_c, v_wv_c, v_g_cq, v_g_ck, v_wo_c, v_g_ffn, v_w_up, v_conv_ffn, v_b_conv_ffn, v_w_down):
    args = dict(locals())
    names = ['g_mix', 'w_in', 'g_qa', 'w_qb', 'g_kva', 'w_kvb', 'g_qn_nope', 'g_qn_pe', 'g_kn_nope', 'g_kn_pe', 'conv_qk', 'b_if',
             'g_hnorm', 'p_a', 'p_b', 'w_out', 'g_cross', 'g_mem', 'wq_c', 'wk_c', 'wv_c', 'g_cq', 'g_ck', 'wo_c', 'g_ffn', 'w_up',
             'conv_ffn', 'b_conv_ffn', 'w_down']
    big = ['w_in', 'w_qb', 'w_kvb', 'p_a', 'p_b', 'w_out', 'wq_c', 'wk_c', 'wv_c', 'wo_c', 'w_up', 'w_down']
    sharded_small = ['conv_qk', 'g_hnorm', 'conv_ffn']
    replicated = [n for n in names if n not in big and n not in sharded_small]

    t, d = x.shape[1], x.shape[2]
    x2d, tgt = x[0], loss_target[0]
    mem2d = mem[0]
    me = 4 * lax.axis_index("x") + 2 * lax.axis_index("y") + lax.axis_index("c")
    nc = t // CHUNK
    f2 = b_conv_ffn.shape[1]
    wmain = O_GA + 2 * d

    first = ['w_in']
    behind_in = ['w_qb', 'w_kvb', 'p_a', 'p_b', 'w_out', 'wq_c', 'wk_c', 'wv_c', 'wo_c']
    shards = {n: cast_bf16(args[n], "cast_" + n) for n in big}
    small_local = _pack([args[n] for n in sharded_small], 128).reshape(1, -1)
    gathered = _exchange([shards[n] for n in first] + [small_local], "comm_gather_first", scatter=False)
    gw = dict(zip(first, gathered[:-1]))
    small_all = gathered[-1]
    full_small, pos = [], 0
    for n in sharded_small:
        _, rows, cols = args[n].shape
        piece = small_all[:, 0, pos:pos + rows * cols].reshape(N_DEV, rows, cols)
        full_small.append(piece.transpose(1, 0, 2).reshape(rows, N_DEV * cols))
        pos += rows * cols
    conv_qk_f, g_hnorm_f, conv_ffn_f = full_small

    w_in_f = gw['w_in'].transpose(1, 0, 2).reshape(d, -1)
    c_kpe, c_q, c_i, c_o = O_Q, O_Q + ROPE, O_Q + ROPE + 2 * ML_QK + ML_V, O_Q + ROPE + 2 * ML_QK + ML_V + 2 * ML_HEADS
    w_main = jnp.concatenate([w_in_f[:, :c_kpe], w_in_f[:, c_q:c_i], w_in_f[:, c_o:]], axis=1)[None]
    w_tail = jnp.concatenate([w_in_f[:, c_kpe:c_q], w_in_f[:, c_i:c_o],
                              jnp.zeros((d, TAIL - ROPE - 2 * ML_HEADS), bf16)], axis=1)[None]
    assert w_main.shape[2] == wmain

    inv_freq = ROPE_BASE ** (-jnp.arange(0, ROPE, 2, dtype=f32) / ROPE)
    inv_tile = _pad_lanes(jnp.concatenate([inv_freq, inv_freq])[None])
    cos, sin = rope_tables(positions.reshape(t, 1), inv_tile)
    gqp, gkp = _pad_lanes(g_qn_pe), _pad_lanes(g_kn_pe)
    b_tile = jnp.pad(b_if, ((0, 0), (T_I, TAIL - T_I - 2 * ML_HEADS)))

    u0 = rms_fwd(x2d, g_mix, "rms_mix")
    z_main, got = mm_nn(u0, w_main, f32, "mm_in_main", exchange=([shards[n] for n in behind_in], False))
    gw.update(zip(behind_in, got))
    qb = gw['w_qb'].transpose(1, 0, 2).reshape(Q_LORA, MLA_HEADS, NOPE + ROPE)
    w_qb_p = jnp.concatenate([qb, jnp.zeros((Q_LORA, MLA_HEADS, HEAD_PAD - NOPE - ROPE), bf16)], axis=2).reshape(1, Q_LORA, -1)
    w_kvb3 = gw['w_kvb']
    p_a3, p_b3, w_out3 = (gw[n].reshape(1, -1, d) for n in ('p_a', 'p_b', 'w_out'))
    wq_c3, wk_c3, wv_c3 = (gw[n].reshape(1, d, -1) for n in ('wq_c', 'wk_c', 'wv_c'))
    wo_c3 = gw['wo_c']
    z_tail = mm_nn(u0, w_tail, f32, "mm_in_tail")
    qa_n, kv_n = lat_norm(z_main, g_qa, g_kva)
    q_raw = mm_nn(qa_n, w_qb_p, f32, "mm_qb")
    kv_raw = mm_nn(kv_n, w_kvb3, f32, "mm_kvb")
    qh, kh, vh = mla_prep(q_raw, kv_raw, z_tail, cos, sin, g_qn_nope, gqp, g_kn_nope, gkp)
    (o_a, o_ab, lse), (w_up3,) = mla_fwd(qh, kh, vh, exchange=([shards['w_up']], False))

    qk_act = qk_conv(z_main, conv_qk_f)
    gates = gate_act(z_tail, b_tile)

    def to_rows(cols):
        return cols.T.reshape(ML_HEADS, nc, 1, CHUNK)

    ig, fg = to_rows(gates[:, T_I:T_F]), to_rows(gates[:, T_F:T_F + ML_HEADS])
    h_ml, c_all, n_all, m_all = mlstm_fwd(qk_act, z_main, ig, fg)
    g_hn3 = g_hnorm_f.reshape(ML_HEADS, 1, ML_DV)
    y_b = mlstm_out(h_ml, z_main, g_hn3)

    ya = mm_nn(o_ab, p_a3, f32, "mm_pa")
    yb = mm_nn(y_b, p_b3, f32, "mm_pb")
    merged = merge_fwd(z_main, ya, yb)
    mo = mm_nn(merged, w_out3, f32, "mm_out")
    x1, uc = resid_rms(x2d, mo, g_cross, "resid_cross")
    mem_n = rms_fwd(mem2d, g_mem, "rms_mem")
    cq = mm_nn(uc, wq_c3, f32, "mm_cq")
    ck = mm_nn(mem_n, wk_c3, f32, "mm_ck")
    cv = mm_nn(mem_n, wv_c3, f32, "mm_cv")
    o_c = cross_fwd(cq, ck, cv, g_cq, g_ck)
    co = mm_nn(o_c, wo_c3, f32, "mm_oc")
    x2, u3 = resid_rms(x1, co, g_ffn, "resid_ffn")
    hup, (w_down_g,) = mm_nn(u3, w_up3, f32, "mm_up", exchange=([shards['w_down']], False))
    w_down3 = w_down_g.reshape(1, -1, d)
    gl = glu_fwd(hup, conv_ffn_f, b_conv_ffn)
    fo = mm_nn(gl, w_down3, f32, "mm_down")
    dx3, dx3_b, loss_acc = loss_head(x2, fo, tgt)

    grads, parts = {}, {}
    grads['w_down'] = mm_tn(gl, dx3_b, 1, "mm_d_wdown").reshape(N_DEV, -1, d)
    dgl = mm_nt(dx3_b, w_down3, f32, "mm_d_gl")
    (dh1, dh2, dcw1, dcw2, db1, db2), (parts['w_down'],) = glu_bwd(hup, conv_ffn_f, b_conv_ffn, dgl,
                                                                    exchange=([grads['w_down']], True))
    dhup, dconv_ffn, db_ffn = (jnp.concatenate(pair, axis=1) for pair in ((dh1, dh2), (dcw1, dcw2), (db1, db2)))
    grads['w_up'] = mm_tn(u3, dhup, N_DEV, "mm_d_wup")
    du3 = mm_nt(dhup, w_up3, f32, "mm_d_u3")
    dx2, dx2_b, dg_ffn = rms_bwd(x2, g_ffn, [du3], dx3, "rms_bwd_ffn", want_b16=True)
    grads['wo_c'] = mm_tn(o_c, dx2_b, N_DEV, "mm_d_woc")
    do_c = mm_nt(dx2_b, wo_c3, f32, "mm_d_oc")
    dcq, dck, dcv, dg_cq, dg_ck = cross_bwd(cq, ck, cv, g_cq, g_ck, do_c)
    grads['wq_c'] = mm_tn(uc, dcq, 1, "mm_d_wqc").reshape(N_DEV, -1, dcq.shape[1])
    grads['wk_c'] = mm_tn(mem_n, dck, 1, "mm_d_wkc").reshape(N_DEV, -1, dck.shape[1])
    grads['wv_c'] = mm_tn(mem_n, dcv, 1, "mm_d_wvc").reshape(N_DEV, -1, dcv.shape[1])
    duc = mm_nt(dcq, wq_c3, f32, "mm_d_uc")
    dmem_k = mm_nt(dck, wk_c3, f32, "mm_d_memk")
    dmem_v = mm_nt(dcv, wv_c3, f32, "mm_d_memv")
    dg_mem, = rms_bwd(mem2d, g_mem, [dmem_k, dmem_v], None, "rms_bwd_mem", want_dx=False)
    dx1, dx1_b, dg_cross = rms_bwd(x1, g_cross, [duc], dx2, "rms_bwd_cross", want_b16=True)
    grads['w_out'] = mm_tn(merged, dx1_b, 1, "mm_d_wout").reshape(N_DEV, -1, d)
    dmerged = mm_nt(dx1_b, w_out3, f32, "mm_d_merged")
    dga, dgb, dya, dyb = merge_bwd(z_main, ya, yb, dmerged)
    grads['p_a'] = mm_tn(o_ab, dya, 1, "mm_d_pa").reshape(N_DEV, -1, d)
    grads['p_b'] = mm_tn(y_b, dyb, 1, "mm_d_pb").reshape(N_DEV, -1, d)
    do_a = mm_nt(dya, p_a3, f32, "mm_d_oa")
    dy_b = mm_nt(dyb, p_b3, f32, "mm_d_yb")

    dh_ml, dzo, dg_hn = mlstm_out_bwd(h_ml, z_main, g_hn3, dy_b)
    mixers = ['p_a', 'p_b', 'w_out']
    (dq_act, dk_act, dzv, dig, dfg), got = mlstm_bwd(qk_act, z_main, ig, fg, c_all, n_all, m_all, dh_ml,
                                                     exchange=([grads[n] for n in mixers], True))
    parts.update(zip(mixers, got))
    dzqk, dconv_qk = qk_conv_bwd(z_main, conv_qk_f, dq_act, dk_act)

    delta = mla_delta(o_a, do_a)
    (dqh, dkh, dvh), (parts['w_up'],) = mla_bwd(qh, kh, vh, do_a, lse, delta.reshape(MLA_HEADS, 1, t),
                                                exchange=([grads['w_up']], True))
    dq_raw, dkv_raw, dzt_pe, dg_qn, dg_qp, dg_kn, dg_kp = mla_prep_bwd(
        q_raw, kv_raw, z_tail, cos, sin, g_qn_nope, gqp, g_kn_nope, gkp, dqh, dkh, dvh)
    d_wqb_p = mm_tn(qa_n, dq_raw, 1, "mm_d_wqb")[0].reshape(Q_LORA, MLA_HEADS, HEAD_PAD)[:, :, :NOPE + ROPE]
    grads['w_qb'] = d_wqb_p.reshape(Q_LORA, N_DEV, -1).transpose(1, 0, 2)
    grads['w_kvb'] = mm_tn(kv_n, dkv_raw, N_DEV, "mm_d_wkvb")
    dqa = mm_nt(dq_raw, w_qb_p, f32, "mm_d_qa")
    dkvn = mm_nt(dkv_raw, w_kvb3, f32, "mm_d_kvn")
    dz_lat, dg_qa, dg_kva = lat_norm_bwd(z_main, g_qa, g_kva, dqa, dkvn)

    def to_cols(rows):
        return rows.reshape(ML_HEADS, t).T

    dgate = jnp.pad(jnp.concatenate([to_cols(dig), to_cols(dfg)], axis=1), ((0, 0), (T_I, TAIL - T_I - 2 * ML_HEADS)))
    dz_tail, db_if = tail_bwd(z_tail, b_tile, dzt_pe, dgate)
    dz_main = jnp.concatenate([dz_lat, dzqk, dzv, dzo, dga, dgb], axis=1)
    small_mats = ['wq_c', 'wk_c', 'wv_c', 'wo_c', 'w_qb', 'w_kvb']
    d_wmain3, got = mm_tn(u0, dz_main, 1, "mm_d_wmain", exchange=([grads[n] for n in small_mats], True))
    parts.update(zip(small_mats, got))
    d_wmain = d_wmain3[0]
    d_wtail = mm_tn(u0, dz_tail, 1, "mm_d_wtail")[0]
    d_win = jnp.concatenate([d_wmain[:, :O_Q], d_wtail[:, :ROPE], d_wmain[:, O_Q:O_O], d_wtail[:, T_I:T_I + 2 * ML_HEADS],
                             d_wmain[:, O_O:]], axis=1)
    grads['w_in'] = d_win.reshape(d, N_DEV, -1).transpose(1, 0, 2)
    du0_a, (parts['w_in'],) = mm_nt(dz_main, w_main, f32, "mm_d_u0_main", exchange=([grads['w_in']], True))
    du0_b = mm_nt(dz_tail, w_tail, f32, "mm_d_u0_tail")
    grad_x, dg_mix = rms_bwd(x2d, g_mix, [du0_a, du0_b], dx1, "rms_bwd_mix")

    out_g, out_d, out_m, out_v = {}, {}, {}, {}
    for n in big:
        res = adam_sum(parts[n], args[n], args['m_' + n], args['v_' + n], "adam_" + n)
        out_g[n], out_d[n], out_m[n], out_v[n] = (a.reshape(args[n].shape) for a in res)

    small_full = {
        'g_mix': dg_mix[0], 'g_qa': dg_qa[0], 'g_kva': dg_kva[0], 'g_qn_nope': dg_qn[0], 'g_qn_pe': dg_qp[0, :ROPE],
        'g_kn_nope': dg_kn[0], 'g_kn_pe': dg_kp[0, :ROPE], 'conv_qk': dconv_qk, 'b_if': db_if[0, T_I:T_I + 2 * ML_HEADS],
        'g_hnorm': dg_hn[:, 0, :], 'g_cross': dg_cross[0], 'g_mem': dg_mem[0], 'g_cq': dg_cq[0], 'g_ck': dg_ck[0],
        'g_ffn': dg_ffn[0], 'conv_ffn': dconv_ffn, 'b_conv_ffn': db_ffn[0], 'loss': loss_acc[0, :1]}
    order = list(small_full)
    packed = _pack([small_full[n] for n in order], 8 * 128).reshape(1, -1)
    gathered_small, = _exchange([packed], "comm_gather_small", scatter=False)
    summed = sum_parts(gathered_small.reshape(N_DEV, -1, 128), "sum_small").reshape(-1)
    full_g = dict(zip(order, _unpack(summed, [small_full[n].shape for n in order])))
    loss = full_g['loss'][0]

    local_g = {}
    for n in replicated:
        local_g[n] = full_g[n].reshape(args[n].shape)
    for n in sharded_small:
        shp = args[n].shape
        full = full_g[n].reshape((1,) + full_g[n].shape)
        local_g[n] = lax.dynamic_slice_in_dim(full, me * shp[-1], shp[-1], axis=2)
    small = replicated + sharded_small
    dl_f, m_f, v_f = adam_flat(*[_pack([src[n] if pre == '' else args[pre + n] for n in small], 8 * 128).reshape(-1, 128)
                                 for pre, src in (('', args), ('', local_g), ('m_', None), ('v_', None))], "adam_small")
    shapes = [args[n].shape for n in small]
    for dst, flat in ((out_d, dl_f), (out_m, m_f), (out_v, v_f)):
        dst.update(zip(small, _unpack(flat.reshape(-1), shapes)))
    out_g.update(local_g)

    return (loss, grad_x[None], *[out_g[n] for n in names], *[out_d[n] for n in names],
            *[out_m[n] for n in names], *[out_v[n] for n in names])
```

```python
import functools

import jax
import jax.numpy as jnp
from jax import lax
from jax.experimental import pallas as pl
from jax.experimental.pallas import tpu as pltpu

f32 = jnp.float32
bf16 = jnp.bfloat16

N_DEV = 8
EPS = 1e-6
CHUNK = 64
CHUNK_SHIFT = 6
assert 1 << CHUNK_SHIFT == CHUNK
MLA_HEADS = 16
Q_LORA = 512
KV_LORA = 512
NOPE = 128
ROPE = 64
V_HEAD = 128
ROPE_BASE = 10000.0
HEAD_PAD = 256
ML_HEADS = 8
ML_DK = 128
ML_DV = 256
ML_CONV = 4
ML_QK = ML_HEADS * ML_DK
ML_V = ML_HEADS * ML_DV
CR_HEADS = 4
CR_HD = 128
FFN_CONV = 3
ADAM_LR = 0.001
ADAM_B1 = 0.9
ADAM_B2 = 0.999
ADAM_EPS = 1e-08
ADAM_WD = 0.01
ADAM_STEP = 10
O_QA, O_KV, O_Q, O_K = 0, Q_LORA, Q_LORA + KV_LORA, Q_LORA + KV_LORA + ML_QK
O_V = O_K + ML_QK
O_O = O_V + ML_V
O_GA = O_O + ML_V
TAIL = 128
T_I, T_F = ROPE, ROPE + ML_HEADS
VMEM_LIMIT_V7X = 48 * 1024 * 1024
MESH = pl.DeviceIdType.MESH


def _call(body, name, grid, in_specs, out_specs, out_shape, scratch=(), exchange=None):
    params = pltpu.CompilerParams(vmem_limit_bytes=VMEM_LIMIT_V7X)
    if exchange is None:
        return pl.pallas_call(body, name=name, grid=grid, in_specs=in_specs, out_specs=out_specs, out_shape=out_shape,
                              scratch_shapes=list(scratch), compiler_params=params)
    arrs, scatter = exchange
    single = not isinstance(out_specs, (list, tuple))
    o_specs = [out_specs] if single else list(out_specs)
    o_shape = [out_shape] if single else list(out_shape)
    n_in, n_out, n_sc, n = len(in_specs), len(o_specs), len(scratch), len(arrs)
    any_spec = pl.BlockSpec(memory_space=pl.ANY)

    def body_with_exchange(*refs):
        pos = [0]

        def take(k):
            pos[0] += k
            return refs[pos[0] - k:pos[0]]

        ins, ex_in, outs, ex_out, sc = take(n_in), take(n), take(n_out), take(n), take(n_sc)
        start, middle, wait = _exchange_ops(ex_in, ex_out, refs[pos[0]:], scatter)
        step, total = 0, 1
        for a in range(len(grid)):
            step = step * grid[a] + pl.program_id(a)
            total *= grid[a]
        pl.when(step == 0)(start)
        body(*ins, *outs, *sc)
        if middle is not None:
            pl.when(step == total // 2)(middle)
        pl.when(step == total - 1)(wait)

    call = pl.pallas_call(body_with_exchange, name="comm_" + name, grid=grid, in_specs=list(in_specs) + [any_spec] * n,
                          out_specs=o_specs + [any_spec] * n, out_shape=o_shape + _exchange_shapes(arrs, scatter),
                          scratch_shapes=list(scratch) + _exchange_sems(n), compiler_params=params)

    def run(*operands):
        res = call(*operands, *arrs)
        return (res[0] if single else list(res[:n_out])), list(res[n_out:])

    return run


def _tile(n, cands):
    for c in cands:
        if n % c == 0:
            return c
    return n


def _sds(shape, dtype):
    return jax.ShapeDtypeStruct(tuple(shape), dtype)


def _bdot(a, b, ca, cb):
    return lax.dot_general(a.astype(bf16), b.astype(bf16), (((ca,), (cb,)), ((), ())), preferred_element_type=f32)


_BIG = (1024, 512, 256, 128)


def _col_tile(nb):
    return nb if nb <= 1536 else _tile(nb, _BIG)


_DEEP = (2048, 1024, 512, 256, 128)


def _mm_call(name, grid, in_specs, out_spec, out_shape, tile, nk, ca, cb, exchange, operands):
    def dot(a_ref, w_ref):
        return _bdot(a_ref[...], w_ref[0] if len(w_ref.shape) == 3 else w_ref[...], ca, cb)

    def store(o_ref, val):
        if len(o_ref.shape) == 3:
            o_ref[0] = val.astype(o_ref.dtype)
        else:
            o_ref[...] = val.astype(o_ref.dtype)

    if nk == 1:
        def body(a_ref, w_ref, o_ref):
            store(o_ref, dot(a_ref, w_ref))

        scratch = []
    else:
        def body(a_ref, w_ref, o_ref, acc):
            kk = pl.program_id(2)

            @pl.when(kk == 0)
            def _():
                acc[...] = jnp.zeros_like(acc)

            acc[...] += dot(a_ref, w_ref)

            @pl.when(kk == nk - 1)
            def _():
                store(o_ref, acc[...])

        scratch = [pltpu.VMEM(tile, f32)]
    return _call(body, name, grid, in_specs, out_spec, out_shape, scratch, exchange=exchange)(*operands)


def mm_nn(a, w3, out_dtype, name, exchange=None):
    m, k = a.shape
    nblk, k2, nb = w3.shape
    assert k == k2
    tm, tk, tn = _tile(m, _BIG), _tile(k, _DEEP), _col_tile(nb)
    per, nk = nb // tn, k // tk
    return _mm_call(name, (m // tm, nblk * per, nk),
                    [pl.BlockSpec((tm, tk), lambda i, j, kk: (i, kk)),
                     pl.BlockSpec((1, tk, tn), lambda i, j, kk: (j // per, kk, j % per))],
                    pl.BlockSpec((tm, tn), lambda i, j, kk: (i, j)), _sds((m, nblk * nb), out_dtype),
                    (tm, tn), nk, 1, 0, exchange, (a, w3))


def mm_nt(a, w3, out_dtype, name, exchange=None):
    m, n = a.shape
    nblk, k, nb = w3.shape
    assert n == nblk * nb
    tm, tn = _tile(m, _BIG), _tile(k, _BIG)
    tc = nb if nb <= 1536 else _tile(nb, _DEEP)
    per = nb // tc
    nk = nblk * per
    return _mm_call(name, (m // tm, k // tn, nk),
                    [pl.BlockSpec((tm, tc), lambda i, j, kk: (i, kk)),
                     pl.BlockSpec((1, tn, tc), lambda i, j, kk: (kk // per, j, kk % per))],
                    pl.BlockSpec((tm, tn), lambda i, j, kk: (i, j)), _sds((m, k), out_dtype),
                    (tm, tn), nk, 1, 1, exchange, (a, w3))


def mm_tn(a, b, nblk, name, exchange=None):
    r, m = a.shape
    r2, n = b.shape
    assert r == r2 and n % nblk == 0
    nb = n // nblk
    tm, tk, tn = _tile(m, _BIG), _tile(r, _DEEP), _col_tile(nb)
    per, nk = nb // tn, r // tk
    return _mm_call(name, (m // tm, nblk * per, nk),
                    [pl.BlockSpec((tk, tm), lambda i, j, kk: (kk, i)),
                     pl.BlockSpec((tk, tn), lambda i, j, kk: (kk, j))],
                    pl.BlockSpec((1, tm, tn), lambda i, j, kk: (j // per, i, j % per)), _sds((nblk, m, nb), bf16),
                    (tm, tn), nk, 0, 0, exchange, (a, b))


def _section_tiles(sections, cands):
    widths = [s.shape[1] for s in sections]
    tile = next(c for c in cands if all(w % c == 0 for w in widths))
    counts = [w // tile for w in widths]
    firsts = [sum(counts[:i]) for i in range(len(counts))]
    return tile, firsts, counts


def mm_tn_cols(a, sections, name, exchange=None):
    r, m = a.shape
    tn, firsts, counts = _section_tiles(sections, _BIG)
    n = tn * sum(counts)
    tm, tk = _tile(m, _BIG), _tile(r, _BIG)
    nk = r // tk

    def body(a_ref, *refs):
        b_refs, o_ref, acc = refs[:len(sections)], refs[len(sections)], refs[len(sections) + 1]
        j, kk = pl.program_id(1), pl.program_id(2)

        @pl.when(kk == 0)
        def _():
            acc[...] = jnp.zeros_like(acc)

        for b_ref, lo, cnt in zip(b_refs, firsts, counts):
            @pl.when(jnp.logical_and(j >= lo, j < lo + cnt))
            def _(b_ref=b_ref):
                acc[...] += _bdot(a_ref[...], b_ref[...], 0, 0)

        @pl.when(kk == nk - 1)
        def _():
            o_ref[0] = acc[...].astype(bf16)

    def spec(lo, cnt):
        def index(i, j, kk):
            return jnp.where(j < lo, 0, jnp.where(j >= lo + cnt, nk - 1, kk)), jnp.clip(j - lo, 0, cnt - 1)
        return pl.BlockSpec((tk, tn), index)

    return _call(body, name, (m // tm, n // tn, nk),
                 [pl.BlockSpec((tk, tm), lambda i, j, kk: (kk, i))] + [spec(lo, cnt) for lo, cnt in zip(firsts, counts)],
                 pl.BlockSpec((1, tm, tn), lambda i, j, kk: (0, i, j)), _sds((1, m, n), bf16),
                 [pltpu.VMEM((tm, tn), f32)], exchange=exchange)(a, *sections)


def mm_nt_cols(sections, w3, out_dtype, name, exchange=None):
    m = sections[0].shape[0]
    nblk, k, n = w3.shape
    tc, firsts, counts = _section_tiles(sections, _DEEP)
    assert nblk == 1 and n == tc * sum(counts)
    tm, tn = _tile(m, _BIG), _tile(k, _BIG)
    nk = n // tc

    def body(*refs):
        a_refs, w_ref, o_ref, acc = refs[:len(sections)], refs[len(sections)], refs[len(sections) + 1], refs[len(sections) + 2]
        kk = pl.program_id(2)

        @pl.when(kk == 0)
        def _():
            acc[...] = jnp.zeros_like(acc)

        for a_ref, lo, cnt in zip(a_refs, firsts, counts):
            @pl.when(jnp.logical_and(kk >= lo, kk < lo + cnt))
            def _(a_ref=a_ref):
                acc[...] += _bdot(a_ref[...], w_ref[0], 1, 1)

        @pl.when(kk == nk - 1)
        def _():
            o_ref[...] = acc[...].astype(o_ref.dtype)

    def spec(lo, cnt):
        return pl.BlockSpec((tm, tc), lambda i, j, kk: (i, jnp.clip(kk - lo, 0, cnt - 1)))

    return _call(body, name, (m // tm, k // tn, nk),
                 [spec(lo, cnt) for lo, cnt in zip(firsts, counts)] + [pl.BlockSpec((1, tn, tc), lambda i, j, kk: (0, j, kk))],
                 pl.BlockSpec((tm, tn), lambda i, j, kk: (i, j)), _sds((m, k), out_dtype),
                 [pltpu.VMEM((tm, tn), f32)], exchange=exchange)(*sections, w3)


def _rms(x, g):
    return x * lax.rsqrt(jnp.mean(x * x, axis=-1, keepdims=True) + EPS) * g


def _rms_pad(x, g, width):
    return x * lax.rsqrt(jnp.sum(x * x, axis=-1, keepdims=True) / width + EPS) * g


def _first(*ids):
    ok = ids[0] == 0
    for i in ids[1:]:
        ok = jnp.logical_and(ok, i == 0)
    return ok


def _acc_row(ref, val, first):
    @pl.when(first)
    def _():
        ref[...] = jnp.zeros_like(ref)

    ref[0:1, :] += val


def rms_fwd(x, g, name):
    r, w = x.shape
    tm = _tile(r, (256, 128, 64, 32, 16, 8))

    def body(x_ref, g_ref, o_ref):
        o_ref[...] = _rms(x_ref[...], g_ref[...]).astype(bf16)

    return _call(body, name, (r // tm,), [pl.BlockSpec((tm, w), lambda i: (i, 0)), pl.BlockSpec((1, w), lambda i: (0, 0))],
                 pl.BlockSpec((tm, w), lambda i: (i, 0)), _sds((r, w), bf16))(x, g)


def resid_rms(xa, xb, g, name):
    r, w = xa.shape
    tm = _tile(r, (256, 128, 64, 32, 16, 8))

    def body(a_ref, b_ref, g_ref, s_ref, u_ref):
        xs = a_ref[...] + b_ref[...]
        s_ref[...] = xs
        u_ref[...] = _rms(xs, g_ref[...]).astype(bf16)

    row = pl.BlockSpec((tm, w), lambda i: (i, 0))
    return _call(body, name, (r // tm,), [row, row, pl.BlockSpec((1, w), lambda i: (0, 0))], [row, row],
                 [_sds((r, w), f32), _sds((r, w), bf16)])(xa, xb, g)


def rms_bwd(x, g, dys, dres, name, want_dx=True, want_b16=False):
    r, w = x.shape
    tm = _tile(r, (256, 128, 64, 32, 16, 8))
    nd = len(dys)

    def body(*refs):
        x_ref, g_ref = refs[0], refs[1]
        dy = refs[2][...]
        for j in range(1, nd):
            dy = dy + refs[2 + j][...]
        pos = 2 + nd
        _, vjp = jax.vjp(_rms, x_ref[...], g_ref[...])
        dx, dg = vjp(dy)
        if dres is not None:
            dx = dx + refs[pos][...]
            pos += 1
        if want_dx:
            refs[pos][...] = dx
            pos += 1
        if want_b16:
            refs[pos][...] = dx.astype(bf16)
            pos += 1
        _acc_row(refs[pos], dg, pl.program_id(0) == 0)

    row = pl.BlockSpec((tm, w), lambda i: (i, 0))
    ins = [x, g] + list(dys) + ([dres] if dres is not None else [])
    in_specs = [row, pl.BlockSpec((1, w), lambda i: (0, 0))] + [row] * (nd + (dres is not None))
    out_specs = [row] * (want_dx + want_b16) + [pl.BlockSpec((8, w), lambda i: (0, 0))]
    out_shape = ([_sds((r, w), f32)] if want_dx else []) + ([_sds((r, w), bf16)] if want_b16 else []) + [_sds((8, w), f32)]
    return _call(body, name, (r // tm,), in_specs, out_specs, out_shape)(*ins)


def lat_norm(z_main, g_qa, g_kva):
    t = z_main.shape[0]
    tm = _tile(t, (512, 256, 128, 64))

    def body(z_ref, gq_ref, gk_ref, q_ref, k_ref):
        q_ref[...] = _rms(z_ref[:, :Q_LORA], gq_ref[...]).astype(bf16)
        k_ref[...] = _rms(z_ref[:, Q_LORA:], gk_ref[...]).astype(bf16)

    return _call(body, "lat_norm", (t // tm,),
                 [pl.BlockSpec((tm, Q_LORA + KV_LORA), lambda i: (i, 0)), pl.BlockSpec((1, Q_LORA), lambda i: (0, 0)),
                  pl.BlockSpec((1, KV_LORA), lambda i: (0, 0))],
                 [pl.BlockSpec((tm, Q_LORA), lambda i: (i, 0)), pl.BlockSpec((tm, KV_LORA), lambda i: (i, 0))],
                 [_sds((t, Q_LORA), bf16), _sds((t, KV_LORA), bf16)])(z_main, g_qa, g_kva)


def lat_norm_bwd(z_main, g_qa, g_kva, dqa, dkv):
    t = z_main.shape[0]
    tm = _tile(t, (512, 256, 128, 64))

    def body(z_ref, gq_ref, gk_ref, dq_ref, dk_ref, dz_ref, dgq_ref, dgk_ref):
        first = pl.program_id(0) == 0
        _, vq = jax.vjp(_rms, z_ref[:, :Q_LORA], gq_ref[...])
        dx, dg = vq(dq_ref[...])
        dz_ref[:, :Q_LORA] = dx.astype(bf16)
        _acc_row(dgq_ref, dg, first)
        _, vk = jax.vjp(_rms, z_ref[:, Q_LORA:], gk_ref[...])
        dx, dg = vk(dk_ref[...])
        dz_ref[:, Q_LORA:] = dx.astype(bf16)
        _acc_row(dgk_ref, dg, first)

    return _call(body, "lat_norm_bwd", (t // tm,),
                 [pl.BlockSpec((tm, Q_LORA + KV_LORA), lambda i: (i, 0)), pl.BlockSpec((1, Q_LORA), lambda i: (0, 0)),
                  pl.BlockSpec((1, KV_LORA), lambda i: (0, 0)), pl.BlockSpec((tm, Q_LORA), lambda i: (i, 0)),
                  pl.BlockSpec((tm, KV_LORA), lambda i: (i, 0))],
                 [pl.BlockSpec((tm, Q_LORA + KV_LORA), lambda i: (i, 0)), pl.BlockSpec((8, Q_LORA), lambda i: (0, 0)),
                  pl.BlockSpec((8, KV_LORA), lambda i: (0, 0))],
                 [_sds((t, Q_LORA + KV_LORA), bf16), _sds((8, Q_LORA), f32), _sds((8, KV_LORA), f32)])(z_main, g_qa, g_kva, dqa, dkv)


def rope_tables(pos_col, inv_freq):
    t = pos_col.shape[0]
    tm = _tile(t, (512, 256, 128, 64))

    def body(p_ref, f_ref, c_ref, s_ref):
        ang = p_ref[...].astype(f32) * f_ref[...]
        lane = lax.broadcasted_iota(jnp.int32, ang.shape, 1)
        c_ref[...] = jnp.where(lane < ROPE, jnp.cos(ang), 0.0)
        sn = jnp.sin(ang)
        s_ref[...] = jnp.where(lane < ROPE // 2, -sn, jnp.where(lane < ROPE, sn, 0.0))

    return _call(body, "rope_tables", (t // tm,),
                 [pl.BlockSpec((tm, 1), lambda i: (i, 0)), pl.BlockSpec((1, TAIL), lambda i: (0, 0))],
                 [pl.BlockSpec((tm, TAIL), lambda i: (i, 0))] * 2, [_sds((t, TAIL), f32)] * 2)(pos_col, inv_freq)


def _swap_halves(n):
    lane = lax.broadcasted_iota(jnp.int32, n.shape, 1)
    return jnp.where(lane < ROPE // 2, pltpu.roll(n, TAIL - ROPE // 2, 1), pltpu.roll(n, ROPE // 2, 1))


def _rope(n, c, s):
    return n * c + _swap_halves(n) * s


def _rope_t(d, c, s):
    return d * c + _swap_halves(d * s)


def _prep_specs(tm):
    head = pl.BlockSpec((tm, HEAD_PAD), lambda i, h: (i, h))
    row = pl.BlockSpec((tm, TAIL), lambda i, h: (i, 0))
    gain = pl.BlockSpec((1, TAIL), lambda i, h: (0, 0))
    return head, row, gain


def _pe_in(zt):
    lane = lax.broadcasted_iota(jnp.int32, zt.shape, 1)
    return jnp.where(lane < ROPE, zt, 0.0)


def mla_prep(q_raw, kv_raw, z_tail, cos, sin, gqn, gqp, gkn, gkp):
    t = q_raw.shape[0]
    tm = _tile(t, (1024, 512, 256, 128, 64))

    def body(q_ref, kv_ref, zt_ref, c_ref, s_ref, gqn_ref, gqp_ref, gkn_ref, gkp_ref, qh_ref, kh_ref, vh_ref):
        c, s = c_ref[...], s_ref[...]
        qh_ref[:, :NOPE] = _rms(q_ref[:, :NOPE], gqn_ref[...]).astype(bf16)
        qh_ref[:, NOPE:] = _rope(_rms_pad(q_ref[:, NOPE:], gqp_ref[...], ROPE), c, s).astype(bf16)
        kh_ref[:, :NOPE] = _rms(kv_ref[:, :NOPE], gkn_ref[...]).astype(bf16)
        kh_ref[:, NOPE:] = _rope(_rms_pad(_pe_in(zt_ref[...]), gkp_ref[...], ROPE), c, s).astype(bf16)
        vh_ref[...] = kv_ref[:, NOPE:].astype(bf16)

    head, row, gain = _prep_specs(tm)
    return _call(body, "mla_prep", (t // tm, MLA_HEADS), [head, head, row, row, row, gain, gain, gain, gain],
                 [head, head, pl.BlockSpec((tm, V_HEAD), lambda i, h: (i, h))],
                 [_sds((t, MLA_HEADS * HEAD_PAD), bf16), _sds((t, MLA_HEADS * HEAD_PAD), bf16), _sds((t, MLA_HEADS * V_HEAD), bf16)],
                 )(q_raw, kv_raw, z_tail, cos, sin, gqn, gqp, gkn, gkp)


def mla_prep_bwd(q_raw, kv_raw, z_tail, cos, sin, gqn, gqp, gkn, gkp, dqh, dkh, dvh):
    t = q_raw.shape[0]
    tm = _tile(t, (1024, 512, 256, 128, 64))
    pad_norm = functools.partial(_rms_pad, width=ROPE)

    def body(q_ref, kv_ref, zt_ref, c_ref, s_ref, gqn_ref, gqp_ref, gkn_ref, gkp_ref, dqh_ref, dkh_ref, dvh_ref,
             dq_ref, dkv_ref, dzt_ref, dgqn_ref, dgqp_ref, dgkn_ref, dgkp_ref):
        i, h = pl.program_id(0), pl.program_id(1)
        first = _first(i, h)
        c, s = c_ref[...], s_ref[...]
        _, v1 = jax.vjp(_rms, q_ref[:, :NOPE], gqn_ref[...])
        dx, dg = v1(dqh_ref[:, :NOPE])
        dq_ref[:, :NOPE] = dx.astype(bf16)
        _acc_row(dgqn_ref, dg, first)
        _, v2 = jax.vjp(pad_norm, q_ref[:, NOPE:], gqp_ref[...])
        dx, dg = v2(_rope_t(dqh_ref[:, NOPE:], c, s))
        dq_ref[:, NOPE:] = dx.astype(bf16)
        _acc_row(dgqp_ref, dg, first)
        _, v3 = jax.vjp(_rms, kv_ref[:, :NOPE], gkn_ref[...])
        dx, dg = v3(dkh_ref[:, :NOPE])
        dkv_ref[:, :NOPE] = dx.astype(bf16)
        _acc_row(dgkn_ref, dg, first)
        dkv_ref[:, NOPE:] = dvh_ref[...].astype(bf16)
        _, v4 = jax.vjp(pad_norm, _pe_in(zt_ref[...]), gkp_ref[...])
        dx, dg = v4(_rope_t(dkh_ref[:, NOPE:], c, s))
        _acc_row(dgkp_ref, dg, first)

        @pl.when(h == 0)
        def _():
            dzt_ref[...] = jnp.zeros_like(dzt_ref)

        dzt_ref[...] += dx

    head, row, gain = _prep_specs(tm)
    acc = pl.BlockSpec((8, TAIL), lambda i, h: (0, 0))
    vspec = pl.BlockSpec((tm, V_HEAD), lambda i, h: (i, h))
    return _call(body, "mla_prep_bwd", (t // tm, MLA_HEADS),
                 [head, head, row, row, row, gain, gain, gain, gain, head, head, vspec],
                 [head, head, row, acc, acc, acc, acc],
                 [_sds((t, MLA_HEADS * HEAD_PAD), bf16), _sds((t, MLA_HEADS * HEAD_PAD), bf16), _sds((t, TAIL), f32)]
                 + [_sds((8, TAIL), f32)] * 4)(q_raw, kv_raw, z_tail, cos, sin, gqn, gqp, gkn, gkp, dqh, dkh, dvh)


ATT_BLOCK = 512
NEG = -1e30
ATT_SCALE = (NOPE + ROPE) ** -0.5
ATT_HEADS = 2
ATT_HEADS_FWD = 4


def _chunk_visible(shape, key_axis):
    kc = lax.broadcasted_iota(jnp.int32, shape, key_axis) >> CHUNK_SHIFT
    qc = lax.broadcasted_iota(jnp.int32, shape, 1 - key_axis) >> CHUNK_SHIFT
    return kc <= qc


def mla_fwd(qh, kh, vh, exchange=None):
    t = qh.shape[0]
    tb = min(ATT_BLOCK, t)
    nb = t // tb

    hp = ATT_HEADS_FWD

    def body(q_ref, k_ref, v_ref, o_ref, ob_ref, lse_ref, m_s, l_s, acc):
        qi, ki = pl.program_id(1), pl.program_id(2)

        @pl.when(ki == 0)
        def _():
            m_s[...] = jnp.full_like(m_s, NEG)
            l_s[...] = jnp.zeros_like(l_s)
            acc[...] = jnp.zeros_like(acc)

        def step(diagonal):
            new = []
            for j in range(hp):
                q, k = q_ref[:, j * HEAD_PAD:(j + 1) * HEAD_PAD], k_ref[:, j * HEAD_PAD:(j + 1) * HEAD_PAD]
                s = _bdot(k, q, 1, 1) * ATT_SCALE
                if diagonal:
                    s = jnp.where(_chunk_visible(s.shape, 0), s, -jnp.inf)
                m_old = m_s[j]
                m_new = jnp.maximum(m_old, jnp.max(s, axis=0, keepdims=True))
                p = jnp.exp(s - m_new)
                alpha = jnp.exp(m_old - m_new)
                l_new = alpha * l_s[j] + jnp.sum(p, axis=0, keepdims=True)
                acc_new = alpha * acc[j] + _bdot(v_ref[:, j * V_HEAD:(j + 1) * V_HEAD], p, 0, 0)
                new.append((m_new, l_new, acc_new))
            for j, (m_new, l_new, acc_new) in enumerate(new):
                m_s[j] = m_new
                l_s[j] = l_new
                acc[j] = acc_new
            return new

        @pl.when(ki < qi)
        def _():
            step(False)

        @pl.when(ki == qi)
        def _():
            for j, (m_new, l_new, acc_new) in enumerate(step(True)):
                o = (acc_new / l_new).T
                o_ref[:, j * V_HEAD:(j + 1) * V_HEAD] = o
                ob_ref[:, j * V_HEAD:(j + 1) * V_HEAD] = o.astype(bf16)
                lse_ref[j] = m_new + jnp.log(l_new)

    kv = lambda g, qi, ki: (jnp.minimum(ki, qi), g)
    o_spec = pl.BlockSpec((tb, hp * V_HEAD), lambda g, qi, ki: (qi, g))
    return _call(body, "mla_fwd", (MLA_HEADS // hp, nb, nb),
                 [pl.BlockSpec((tb, hp * HEAD_PAD), lambda g, qi, ki: (qi, g)), pl.BlockSpec((tb, hp * HEAD_PAD), kv),
                  pl.BlockSpec((tb, hp * V_HEAD), kv)],
                 [o_spec, o_spec, pl.BlockSpec((hp, 1, tb), lambda g, qi, ki: (g, 0, qi))],
                 [_sds((t, MLA_HEADS * V_HEAD), f32), _sds((t, MLA_HEADS * V_HEAD), bf16), _sds((MLA_HEADS, 1, t), f32)],
                 [pltpu.VMEM((hp, 1, tb), f32), pltpu.VMEM((hp, 1, tb), f32), pltpu.VMEM((hp, V_HEAD, tb), f32)],
                 exchange=exchange)(qh, kh, vh)


def mla_delta(o, do):
    t = o.shape[0]
    tm = _tile(t, (512, 256, 128, 64))

    def body(o_ref, do_ref, d_ref):
        for h in range(MLA_HEADS):
            cols = slice(h * V_HEAD, (h + 1) * V_HEAD)
            d_ref[h] = jnp.sum(o_ref[:, cols] * do_ref[:, cols], axis=1, keepdims=True)

    blk = pl.BlockSpec((tm, MLA_HEADS * V_HEAD), lambda i: (i, 0))
    return _call(body, "mla_delta", (t // tm,), [blk, blk], pl.BlockSpec((MLA_HEADS, tm, 1), lambda i: (0, i, 0)),
                 _sds((MLA_HEADS, t, 1), f32))(o, do)


def mla_bwd(qh, kh, vh, do, lse_row, delta_row, exchange=None):
    t = qh.shape[0]
    tb = min(ATT_BLOCK, t)
    nb = t // tb

    hp = ATT_HEADS

    def body(q_ref, k_ref, v_ref, do_ref, lse_ref, dl_ref, dq_ref, dk_ref, dv_ref, dk_acc, dv_acc):
        ki, qi = pl.program_id(1), pl.program_id(2)

        @pl.when(jnp.logical_and(ki == 0, qi == 0))
        def _():
            dq_ref[...] = jnp.zeros_like(dq_ref)

        @pl.when(qi == 0)
        def _():
            dk_acc[...] = jnp.zeros_like(dk_acc)
            dv_acc[...] = jnp.zeros_like(dv_acc)

        def step(diagonal):
            rows = pl.ds(pl.multiple_of(qi * tb, tb), tb)
            new = []
            for j in range(hp):
                qc, vc = slice(j * HEAD_PAD, (j + 1) * HEAD_PAD), slice(j * V_HEAD, (j + 1) * V_HEAD)
                q, k, do_b = q_ref[:, qc], k_ref[:, qc], do_ref[:, vc]
                s = _bdot(k, q, 1, 1) * ATT_SCALE
                if diagonal:
                    s = jnp.where(_chunk_visible(s.shape, 0), s, -jnp.inf)
                p = jnp.exp(s - lse_ref[j])
                dp = _bdot(v_ref[:, vc], do_b, 1, 1)
                ds = p * (dp - dl_ref[j]) * ATT_SCALE
                new.append((dv_acc[:, vc] + _bdot(p, do_b, 1, 0), dk_acc[:, qc] + _bdot(ds, q, 1, 0),
                            dq_ref[rows, qc] + _bdot(ds, k, 0, 0)))
            for j, (dv, dk, dq) in enumerate(new):
                dv_acc[:, j * V_HEAD:(j + 1) * V_HEAD] = dv
                dk_acc[:, j * HEAD_PAD:(j + 1) * HEAD_PAD] = dk
                dq_ref[rows, j * HEAD_PAD:(j + 1) * HEAD_PAD] = dq

        @pl.when(qi > ki)
        def _():
            step(False)

        @pl.when(qi == ki)
        def _():
            step(True)

        @pl.when(qi == nb - 1)
        def _():
            dk_ref[...] = dk_acc[...]
            dv_ref[...] = dv_acc[...]

    qs = lambda g, ki, qi: (jnp.maximum(qi, ki), g)
    ks = lambda g, ki, qi: (ki, g)
    vec = pl.BlockSpec((hp, 1, tb), lambda g, ki, qi: (g, 0, jnp.maximum(qi, ki)))
    return _call(body, "mla_bwd", (MLA_HEADS // hp, nb, nb),
                 [pl.BlockSpec((tb, hp * HEAD_PAD), qs), pl.BlockSpec((tb, hp * HEAD_PAD), ks), pl.BlockSpec((tb, hp * V_HEAD), ks),
                  pl.BlockSpec((tb, hp * V_HEAD), qs), vec, vec],
                 [pl.BlockSpec((t, hp * HEAD_PAD), lambda g, ki, qi: (0, g)), pl.BlockSpec((tb, hp * HEAD_PAD), ks),
                  pl.BlockSpec((tb, hp * V_HEAD), ks)],
                 [_sds((t, MLA_HEADS * HEAD_PAD), f32), _sds((t, MLA_HEADS * HEAD_PAD), f32), _sds((t, MLA_HEADS * V_HEAD), f32)],
                 [pltpu.VMEM((tb, hp * HEAD_PAD), f32), pltpu.VMEM((tb, hp * V_HEAD), f32)], exchange=exchange)(
        qh, kh, vh, do, lse_row, delta_row)


PAD = 8


def _conv_taps(pad_ref, w, width, t):
    y = pad_ref[PAD - width + 1:PAD - width + 1 + t, :] * w[0:1, :]
    for j in range(1, width):
        y = y + pad_ref[PAD - width + 1 + j:PAD - width + 1 + j + t, :] * w[j:j + 1, :]
    return y


def _conv_bwd(xpad_ref, dpad_ref, w, da, width, t):
    dpad_ref[0:t, :] = da
    dpad_ref[t:t + PAD, :] = jnp.zeros((PAD, da.shape[1]), f32)
    dx = dpad_ref[width - 1:width - 1 + t, :] * w[0:1, :]
    for j in range(1, width):
        dx = dx + dpad_ref[width - 1 - j:width - 1 - j + t, :] * w[j:j + 1, :]
    dws = [jnp.sum(da * xpad_ref[PAD - width + 1 + j:PAD - width + 1 + j + t, :], axis=0, keepdims=True) for j in range(width)]
    return dx, dws


def _load_pad(pad_ref, x, t):
    pad_ref[0:PAD, :] = jnp.zeros((PAD, x.shape[1]), f32)
    pad_ref[PAD:PAD + t, :] = x


assert ML_DK == 128


def qk_conv(z_main, conv_qk):
    t = z_main.shape[0]
    base = O_Q // ML_DK

    def body(z_ref, w_ref, o_ref, pad):
        _load_pad(pad, z_ref[...], t)
        a = _conv_taps(pad, w_ref[...], ML_CONV, t)
        sc = jnp.where(pl.program_id(0) < ML_HEADS, ML_DK ** -0.5, 1.0)
        o_ref[0] = jax.nn.silu(a) * sc

    return _call(body, "qk_conv", (2 * ML_HEADS,),
                 [pl.BlockSpec((t, ML_DK), lambda j: (0, base + j)), pl.BlockSpec((ML_CONV, ML_DK), lambda j: (0, j))],
                 pl.BlockSpec((1, t, ML_DK), lambda j: (j, 0, 0)), _sds((2 * ML_HEADS, t, ML_DK), f32),
                 [pltpu.VMEM((t + PAD, ML_DK), f32)])(z_main, conv_qk)


def qk_conv_bwd(z_main, conv_qk, dq, dk):
    t = z_main.shape[0]
    base = O_Q // ML_DK

    def body(z_ref, w_ref, dq_ref, dk_ref, dz_ref, dw_ref, pad, dpad):
        _load_pad(pad, z_ref[...], t)
        w = w_ref[...]
        a = _conv_taps(pad, w, ML_CONV, t)
        is_q = pl.program_id(0) < ML_HEADS
        d = jnp.where(is_q, dq_ref[0] * (ML_DK ** -0.5), dk_ref[0])
        _, vjp = jax.vjp(jax.nn.silu, a)
        da, = vjp(d)
        dx, dws = _conv_bwd(pad, dpad, w, da, ML_CONV, t)
        dz_ref[...] = dx.astype(bf16)
        for j in range(ML_CONV):
            dw_ref[j:j + 1, :] = dws[j]

    head = lambda pick: pl.BlockSpec((1, t, ML_DK), lambda j: (pick(j), 0, 0))
    return _call(body, "qk_conv_bwd", (2 * ML_HEADS,),
                 [pl.BlockSpec((t, ML_DK), lambda j: (0, base + j)), pl.BlockSpec((ML_CONV, ML_DK), lambda j: (0, j)),
                  head(lambda j: jnp.minimum(j, ML_HEADS - 1)), head(lambda j: jnp.maximum(j - ML_HEADS, 0))],
                 [pl.BlockSpec((t, ML_DK), lambda j: (0, j)), pl.BlockSpec((ML_CONV, ML_DK), lambda j: (0, j))],
                 [_sds((t, 2 * ML_QK), bf16), _sds((ML_CONV, 2 * ML_QK), f32)],
                 [pltpu.VMEM((t + PAD, ML_DK), f32), pltpu.VMEM((t + PAD, ML_DK), f32)])(z_main, conv_qk, dq, dk)


def glu_fwd(hup, conv_w, bias):
    t, f2 = hup.shape
    nf = f2 // 2 // 128

    def body(h1_ref, h2_ref, w1_ref, w2_ref, b1_ref, b2_ref, o_ref, pad):
        _load_pad(pad, h1_ref[...], t)
        a1 = _conv_taps(pad, w1_ref[...], FFN_CONV, t) + b1_ref[...]
        _load_pad(pad, h2_ref[...], t)
        a2 = _conv_taps(pad, w2_ref[...], FFN_CONV, t) + b2_ref[...]
        o_ref[...] = (jax.nn.silu(a1) * a2).astype(bf16)

    col = lambda off: pl.BlockSpec((t, 128), lambda j: (0, j + off))
    wsp = lambda off: pl.BlockSpec((FFN_CONV, 128), lambda j: (0, j + off))
    bsp = lambda off: pl.BlockSpec((1, 128), lambda j: (0, j + off))
    return _call(body, "glu_fwd", (nf,), [col(0), col(nf), wsp(0), wsp(nf), bsp(0), bsp(nf)], col(0), _sds((t, f2 // 2), bf16),
                 [pltpu.VMEM((t + PAD, 128), f32)])(hup, hup, conv_w, conv_w, bias, bias)


def glu_bwd(hup, conv_w, bias, dg, exchange=None):
    t, f2 = hup.shape
    f = f2 // 2
    nf = f // 128

    def body(h1_ref, h2_ref, w1_ref, w2_ref, b1_ref, b2_ref, dg_ref, dh1_ref, dh2_ref, dw1_ref, dw2_ref, db1_ref, db2_ref,
             pad1, pad2, dpad):
        _load_pad(pad1, h1_ref[...], t)
        _load_pad(pad2, h2_ref[...], t)
        w1, w2 = w1_ref[...], w2_ref[...]
        a1 = _conv_taps(pad1, w1, FFN_CONV, t) + b1_ref[...]
        a2 = _conv_taps(pad2, w2, FFN_CONV, t) + b2_ref[...]
        d = dg_ref[...]
        _, vjp = jax.vjp(jax.nn.silu, a1)
        da1, = vjp(d * a2)
        da2 = d * jax.nn.silu(a1)
        for da, pad, w, dh_ref, dw_ref, db_ref in ((da1, pad1, w1, dh1_ref, dw1_ref, db1_ref), (da2, pad2, w2, dh2_ref, dw2_ref, db2_ref)):
            dx, dws = _conv_bwd(pad, dpad, w, da, FFN_CONV, t)
            dh_ref[...] = dx.astype(bf16)
            for j in range(FFN_CONV):
                dw_ref[j:j + 1, :] = dws[j]
            db_ref[...] = jnp.sum(da, axis=0, keepdims=True)

    col = lambda off: pl.BlockSpec((t, 128), lambda j: (0, j + off))
    wsp = lambda off: pl.BlockSpec((FFN_CONV, 128), lambda j: (0, j + off))
    bsp = lambda off: pl.BlockSpec((1, 128), lambda j: (0, j + off))
    return _call(body, "glu_bwd", (nf,), [col(0), col(nf), wsp(0), wsp(nf), bsp(0), bsp(nf), col(0)],
                 [col(0), col(0), wsp(0), wsp(0), bsp(0), bsp(0)],
                 [_sds((t, f), bf16)] * 2 + [_sds((FFN_CONV, f), f32)] * 2 + [_sds((1, f), f32)] * 2,
                 [pltpu.VMEM((t + PAD, 128), f32)] * 3, exchange=exchange)(hup, hup, conv_w, conv_w, bias, bias, dg)


def gate_act(z_tail, b_tile):
    t = z_tail.shape[0]
    tm = _tile(t, (512, 256, 128, 64))

    def body(z_ref, b_ref, o_ref):
        x = z_ref[...] + b_ref[...]
        lane = lax.broadcasted_iota(jnp.int32, x.shape, 1)
        o_ref[...] = jnp.where(lane < T_F, x, jax.nn.log_sigmoid(x))

    row = pl.BlockSpec((tm, TAIL), lambda i: (i, 0))
    return _call(body, "gate_act", (t // tm,), [row, pl.BlockSpec((1, TAIL), lambda i: (0, 0))], row, _sds((t, TAIL), f32))(z_tail, b_tile)


def tail_bwd(z_tail, b_tile, dzt_pe, dgate):
    t = z_tail.shape[0]
    tm = _tile(t, (512, 256, 128, 64))

    def body(z_ref, b_ref, dpe_ref, dg_ref, dz_ref, db_ref):
        x = z_ref[...] + b_ref[...]
        lane = lax.broadcasted_iota(jnp.int32, x.shape, 1)
        _, vjp = jax.vjp(jax.nn.log_sigmoid, x)
        df, = vjp(dg_ref[...])
        dgates = jnp.where(lane < T_F, dg_ref[...], df)
        dgates = jnp.where(jnp.logical_and(lane >= T_I, lane < T_F + ML_HEADS), dgates, 0.0)
        dz_ref[...] = jnp.where(lane < ROPE, dpe_ref[...], dgates).astype(bf16)
        _acc_row(db_ref, jnp.sum(dgates, axis=0, keepdims=True), pl.program_id(0) == 0)

    row = pl.BlockSpec((tm, TAIL), lambda i: (i, 0))
    return _call(body, "tail_bwd", (t // tm,), [row, pl.BlockSpec((1, TAIL), lambda i: (0, 0)), row, row],
                 [row, pl.BlockSpec((8, TAIL), lambda i: (0, 0))], [_sds((t, TAIL), bf16), _sds((8, TAIL), f32)])(z_tail, b_tile, dzt_pe, dgate)


def _hdot(a, b, ca, cb):
    return lax.dot_general(a.astype(bf16), b.astype(bf16), (((ca,), (cb,)), ((0,), (0,))), preferred_element_type=f32)


def _mlstm_step(q, k, v, igr, fgr, c_mat, n_vec, m):
    nh, ln = q.shape[0], CHUNK
    sq = (nh, ln, ln)
    row = lax.broadcasted_iota(jnp.int32, sq, 1)
    col = lax.broadcasted_iota(jnp.int32, sq, 2)
    eye = row == col

    def to_col(r):
        return jnp.sum(jnp.where(eye, jnp.broadcast_to(r, sq), 0.0), axis=2, keepdims=True)

    bc_r = jnp.sum(jnp.where(row <= col, jnp.broadcast_to(to_col(fgr), sq), 0.0), axis=1, keepdims=True)
    bc_c = to_col(bc_r)
    logw = jnp.where(col <= row, bc_c - bc_r + igr, -jnp.inf)
    inter = bc_c + m
    m_t = jnp.maximum(inter, jnp.max(logw, axis=2, keepdims=True))
    w_intra = jnp.exp(logw - m_t)
    w_inter = jnp.exp(inter - m_t)
    sc = _hdot(q, k, 2, 2) * w_intra
    num = w_inter * _hdot(q, c_mat, 2, 1) + _hdot(sc, v, 2, 1)
    qn = jnp.sum(q.astype(bf16).astype(f32) * n_vec.astype(bf16).astype(f32), axis=2, keepdims=True)
    den = w_inter * qn + jnp.sum(sc, axis=2, keepdims=True)
    h = num / jnp.maximum(jnp.abs(den), jnp.exp(-m_t))
    lane = lax.broadcasted_iota(jnp.int32, (nh, 1, ln), 2)
    b_last = jnp.sum(jnp.where(lane == ln - 1, bc_r, 0.0), axis=2, keepdims=True)
    logu = b_last - bc_r + igr
    m_new = jnp.maximum(b_last + m, jnp.max(logu, axis=2, keepdims=True))
    decay = jnp.exp(b_last + m - m_new)
    u_c = to_col(jnp.exp(logu - m_new))
    c_new = decay * c_mat + _hdot(u_c * k, v, 1, 1)
    n_new = decay * n_vec + jnp.sum(u_c.astype(bf16).astype(f32) * k.astype(bf16).astype(f32), axis=1, keepdims=True)
    return h, c_new, n_new, m_new


ML_VHALF = ML_V // 2
assert O_V % ML_VHALF == 0 and ML_HEADS % 2 == 0


def _ml_specs(nc, rev):
    cc = (lambda c: nc - 1 - c) if rev else (lambda c: c)
    q = pl.BlockSpec((ML_HEADS, CHUNK, ML_DK), lambda c: (0, cc(c), 0))
    k = pl.BlockSpec((ML_HEADS, CHUNK, ML_DK), lambda c: (1, cc(c), 0))
    v_lo = pl.BlockSpec((CHUNK, ML_VHALF), lambda c: (cc(c), O_V // ML_VHALF))
    v_hi = pl.BlockSpec((CHUNK, ML_VHALF), lambda c: (cc(c), O_V // ML_VHALF + 1))
    hv = pl.BlockSpec((ML_HEADS, CHUNK, ML_DV), lambda c: (0, cc(c), 0))
    gate = pl.BlockSpec((ML_HEADS, 1, 1, CHUNK), lambda c: (0, cc(c), 0, 0))
    cm = pl.BlockSpec((ML_HEADS, 1, ML_DK, ML_DV), lambda c: (0, cc(c), 0, 0))
    nv = pl.BlockSpec((ML_HEADS, 1, 1, ML_DK), lambda c: (0, cc(c), 0, 0))
    ms = pl.BlockSpec((ML_HEADS, 1, 1, 1), lambda c: (0, cc(c), 0, 0))
    return q, k, v_lo, v_hi, hv, gate, cm, nv, ms


_ML_STATE = [pltpu.VMEM((ML_HEADS, ML_DK, ML_DV), f32), pltpu.VMEM((ML_HEADS, 1, ML_DK), f32), pltpu.VMEM((ML_HEADS, 1, 1), f32)]


def _ml_zero_state(c_s, n_s, m_s):
    @pl.when(pl.program_id(0) == 0)
    def _():
        c_s[...] = jnp.zeros_like(c_s)
        n_s[...] = jnp.zeros_like(n_s)
        m_s[...] = jnp.zeros_like(m_s)


def _ml_heads_of(v_lo_ref, v_hi_ref):
    half = ML_HEADS // 2
    return jnp.stack([r[:, j * ML_DV:(j + 1) * ML_DV] for r in (v_lo_ref, v_hi_ref) for j in range(half)])


def mlstm_fwd(qk_act, z_main, ig, fg):
    t = qk_act.shape[1]
    nc = t // CHUNK

    def body(q_ref, k_ref, vl_ref, vh_ref, ig_ref, fg_ref, h_ref, c_out, n_out, m_out, c_s, n_s, m_s):
        _ml_zero_state(c_s, n_s, m_s)
        c0, n0, m0 = c_s[...], n_s[...], m_s[...]
        c_out[:, 0] = c0
        n_out[:, 0] = n0
        m_out[:, 0] = m0
        h, c2, n2, m2 = _mlstm_step(q_ref[...], k_ref[...], _ml_heads_of(vl_ref, vh_ref), ig_ref[:, 0], fg_ref[:, 0], c0, n0, m0)
        h_ref[...] = h
        c_s[...] = c2
        n_s[...] = n2
        m_s[...] = m2

    q, k, v_lo, v_hi, hv, gate, cm, nv, ms = _ml_specs(nc, False)
    return _call(body, "mlstm_fwd", (nc,), [q, k, v_lo, v_hi, gate, gate], [hv, cm, nv, ms],
                 [_sds((ML_HEADS, t, ML_DV), f32), _sds((ML_HEADS, nc, ML_DK, ML_DV), f32), _sds((ML_HEADS, nc, 1, ML_DK), f32),
                  _sds((ML_HEADS, nc, 1, 1), f32)], _ML_STATE)(qk_act, qk_act, z_main, z_main, ig, fg)


def mlstm_bwd(qk_act, z_main, ig, fg, c_all, n_all, m_all, dh, exchange=None):
    t = qk_act.shape[1]
    nc = t // CHUNK

    def body(q_ref, k_ref, vl_ref, vh_ref, ig_ref, fg_ref, c_ref, n_ref, m_ref, dh_ref, dq_ref, dk_ref, dv_ref, dig_ref, dfg_ref,
             dc_s, dn_s, dm_s):
        _ml_zero_state(dc_s, dn_s, dm_s)
        _, vjp = jax.vjp(_mlstm_step, q_ref[...], k_ref[...], _ml_heads_of(vl_ref, vh_ref), ig_ref[:, 0], fg_ref[:, 0],
                         c_ref[:, 0], n_ref[:, 0], m_ref[:, 0])
        dq, dk, dv, dig, dfg, dc, dn, dm = vjp((dh_ref[...], dc_s[...], dn_s[...], dm_s[...]))
        dq_ref[...] = dq
        dk_ref[...] = dk
        for j in range(ML_HEADS):
            dv_ref[:, j * ML_DV:(j + 1) * ML_DV] = dv[j].astype(bf16)
        dig_ref[:, 0] = dig
        dfg_ref[:, 0] = dfg
        dc_s[...] = dc
        dn_s[...] = dn
        dm_s[...] = dm

    q, k, v_lo, v_hi, hv, gate, cm, nv, ms = _ml_specs(nc, True)
    gshape = _sds((ML_HEADS, nc, 1, CHUNK), f32)
    return _call(body, "mlstm_bwd", (nc,), [q, k, v_lo, v_hi, gate, gate, cm, nv, ms, hv],
                 [q, q, pl.BlockSpec((CHUNK, ML_V), lambda c: (nc - 1 - c, 0)), gate, gate],
                 [_sds((ML_HEADS, t, ML_DK), f32), _sds((ML_HEADS, t, ML_DK), f32), _sds((t, ML_V), bf16), gshape, gshape],
                 _ML_STATE, exchange=exchange)(qk_act, qk_act, z_main, z_main, ig, fg, c_all, n_all, m_all, dh)


def _ml_out(h, zo, g):
    return _rms(h, g) * jax.nn.sigmoid(zo)


def mlstm_out(h, z_main, g_hnorm):
    t = h.shape[1]
    tm = _tile(t, (512, 256, 128, 64))
    zo = O_O // ML_DV

    def body(h_ref, z_ref, g_ref, y_ref):
        y_ref[...] = _ml_out(h_ref[0], z_ref[...], g_ref[0]).astype(bf16)

    return _call(body, "mlstm_out", (t // tm, ML_HEADS),
                 [pl.BlockSpec((1, tm, ML_DV), lambda i, hd: (hd, i, 0)), pl.BlockSpec((tm, ML_DV), lambda i, hd: (i, zo + hd)),
                  pl.BlockSpec((1, 1, ML_DV), lambda i, hd: (hd, 0, 0))],
                 pl.BlockSpec((tm, ML_DV), lambda i, hd: (i, hd)), _sds((t, ML_V), bf16))(h, z_main, g_hnorm)


def mlstm_out_bwd(h, z_main, g_hnorm, dy):
    t = h.shape[1]
    tm = _tile(t, (512, 256, 128, 64))
    zo = O_O // ML_DV

    def body(h_ref, z_ref, g_ref, dy_ref, dh_ref, dzo_ref, dg_ref):
        _, vjp = jax.vjp(_ml_out, h_ref[0], z_ref[...], g_ref[0])
        dh, dz, dg = vjp(dy_ref[...])
        dh_ref[0] = dh
        dzo_ref[...] = dz.astype(bf16)

        @pl.when(pl.program_id(1) == 0)
        def _():
            dg_ref[...] = jnp.zeros_like(dg_ref)

        dg_ref[0, 0:1, :] += dg

    head = pl.BlockSpec((1, tm, ML_DV), lambda hd, i: (hd, i, 0))
    blk = pl.BlockSpec((tm, ML_DV), lambda hd, i: (i, hd))
    return _call(body, "mlstm_out_bwd", (ML_HEADS, t // tm),
                 [head, pl.BlockSpec((tm, ML_DV), lambda hd, i: (i, zo + hd)), pl.BlockSpec((1, 1, ML_DV), lambda hd, i: (hd, 0, 0)), blk],
                 [head, blk, pl.BlockSpec((1, 8, ML_DV), lambda hd, i: (hd, 0, 0))],
                 [_sds((ML_HEADS, t, ML_DV), f32), _sds((t, ML_V), bf16), _sds((ML_HEADS, 8, ML_DV), f32)])(h, z_main, g_hnorm, dy)


def _merge(ga, gb, ya, yb):
    return jax.nn.sigmoid(ga) * ya + jax.nn.sigmoid(gb) * yb


def _merge_specs(t, d):
    tm = _tile(t, (512, 256, 128, 64))
    bw = _tile(d, (512, 256, 128))
    assert O_GA % bw == 0 and (O_GA + d) % bw == 0
    blk = pl.BlockSpec((tm, bw), lambda i, j: (i, j))
    ga = pl.BlockSpec((tm, bw), lambda i, j: (i, O_GA // bw + j))
    gb = pl.BlockSpec((tm, bw), lambda i, j: (i, (O_GA + d) // bw + j))
    return tm, bw, blk, ga, gb


def merge_fwd(z_main, ya, yb):
    t, d = ya.shape
    tm, bw, blk, ga, gb = _merge_specs(t, d)

    def body(ga_ref, gb_ref, ya_ref, yb_ref, o_ref):
        o_ref[...] = _merge(ga_ref[...], gb_ref[...], ya_ref[...], yb_ref[...]).astype(bf16)

    return _call(body, "merge_fwd", (t // tm, d // bw), [ga, gb, blk, blk], blk, _sds((t, d), bf16))(z_main, z_main, ya, yb)


def merge_bwd(z_main, ya, yb, dmerged):
    t, d = ya.shape
    tm, bw, blk, ga, gb = _merge_specs(t, d)

    def body(ga_ref, gb_ref, ya_ref, yb_ref, dm_ref, dga_ref, dgb_ref, dya_ref, dyb_ref):
        _, vjp = jax.vjp(_merge, ga_ref[...], gb_ref[...], ya_ref[...], yb_ref[...])
        dga, dgb, dya, dyb = vjp(dm_ref[...])
        dga_ref[...] = dga.astype(bf16)
        dgb_ref[...] = dgb.astype(bf16)
        dya_ref[...] = dya.astype(bf16)
        dyb_ref[...] = dyb.astype(bf16)

    return _call(body, "merge_bwd", (t // tm, d // bw), [ga, gb, blk, blk, blk], [blk] * 4, [_sds((t, d), bf16)] * 4)(
        z_main, z_main, ya, yb, dmerged)


def _cross(cq, ck, cv, gq, gk):
    outs = []
    for hd in range(CR_HEADS):
        sl = slice(hd * CR_HD, (hd + 1) * CR_HD)
        q = _rms(cq[:, sl], gq)
        k = _rms(ck[:, sl], gk)
        s = _bdot(q, k, 1, 1) * (CR_HD ** -0.5)
        p = jax.nn.softmax(s, axis=-1)
        outs.append(_bdot(p, cv[:, sl], 1, 0))
    return jnp.concatenate(outs, axis=1)


def cross_fwd(cq, ck, cv, gq, gk):
    t, w = cq.shape
    nm = ck.shape[0]
    tm = _tile(t, (512, 256, 128, 64))

    def body(q_ref, k_ref, v_ref, gq_ref, gk_ref, o_ref):
        o_ref[...] = _cross(q_ref[...], k_ref[...], v_ref[...], gq_ref[...], gk_ref[...]).astype(bf16)

    row = pl.BlockSpec((tm, w), lambda i: (i, 0))
    full = pl.BlockSpec((nm, w), lambda i: (0, 0))
    gain = pl.BlockSpec((1, CR_HD), lambda i: (0, 0))
    return _call(body, "cross_fwd", (t // tm,), [row, full, full, gain, gain], row, _sds((t, w), bf16))(cq, ck, cv, gq, gk)


def cross_bwd(cq, ck, cv, gq, gk, do):
    t, w = cq.shape
    nm = ck.shape[0]
    tm = _tile(t, (512, 256, 128, 64))

    def body(q_ref, k_ref, v_ref, gq_ref, gk_ref, do_ref, dq_ref, dk_ref, dv_ref, dgq_ref, dgk_ref):
        first = pl.program_id(0) == 0
        _, vjp = jax.vjp(_cross, q_ref[...], k_ref[...], v_ref[...], gq_ref[...], gk_ref[...])
        dq, dk, dv, dgq, dgk = vjp(do_ref[...])
        dq_ref[...] = dq.astype(bf16)

        @pl.when(first)
        def _():
            dk_ref[...] = jnp.zeros_like(dk_ref)
            dv_ref[...] = jnp.zeros_like(dv_ref)

        dk_ref[...] += dk
        dv_ref[...] += dv
        _acc_row(dgq_ref, dgq, first)
        _acc_row(dgk_ref, dgk, first)

    row = pl.BlockSpec((tm, w), lambda i: (i, 0))
    full = pl.BlockSpec((nm, w), lambda i: (0, 0))
    gain = pl.BlockSpec((1, CR_HD), lambda i: (0, 0))
    acc = pl.BlockSpec((8, CR_HD), lambda i: (0, 0))
    return _call(body, "cross_bwd", (t // tm,), [row, full, full, gain, gain, row], [row, full, full, acc, acc],
                 [_sds((t, w), bf16), _sds((nm, w), f32), _sds((nm, w), f32), _sds((8, CR_HD), f32), _sds((8, CR_HD), f32)])(
        cq, ck, cv, gq, gk, do)


def loss_head(x2, fo, target):
    t, d = x2.shape
    tm = _tile(t, (256, 128, 64, 32, 16, 8))

    def body(a_ref, b_ref, t_ref, dx_ref, dxb_ref, l_ref):
        err = a_ref[...] + b_ref[...] - t_ref[...]
        dx = err / d
        dx_ref[...] = dx
        dxb_ref[...] = dx.astype(bf16)
        part = 0.5 * jnp.sum(jnp.mean(err * err, axis=1, keepdims=True), axis=0, keepdims=True)
        _acc_row(l_ref, jnp.broadcast_to(part, (1, 128)), pl.program_id(0) == 0)

    row = pl.BlockSpec((tm, d), lambda i: (i, 0))
    return _call(body, "loss_head", (t // tm,), [row, row, row], [row, row, pl.BlockSpec((8, 128), lambda i: (0, 0))],
                 [_sds((t, d), f32), _sds((t, d), bf16), _sds((8, 128), f32)])(x2, fo, target)


def _place():
    x, y, c = lax.axis_index("x"), lax.axis_index("y"), lax.axis_index("c")
    peers = {}
    for r in range(1, N_DEV):
        px = 1 - x if r & 4 else x
        py = 1 - y if r & 2 else y
        pc = 1 - c if r & 1 else c
        peers[r] = ((px, py, pc), 4 * px + 2 * py + pc)
    return 4 * x + 2 * y + c, peers


N_REL = N_DEV - 1
RELATIONS = tuple(range(1, N_DEV))
SIBLING = 1
OTHER_CHIPS = (2, 4, 6)
PASSED_ON = (3, 5, 7)


def _exchange_ops(ins, outs, sems, scatter):
    n = len(ins)
    send_sems, recv_sems, local_sems = sems

    def tools():
        me, peers = _place()

        def copy(a, r, src, dst_idx, to):
            return pltpu.make_async_remote_copy(
                src_ref=src, dst_ref=outs[a].at[dst_idx], send_sem=send_sems.at[a * N_REL + r - 1],
                recv_sem=recv_sems.at[a * N_REL + r - 1], device_id=peers[to][0], device_id_type=MESH)

        def local(a):
            return pltpu.make_async_copy(ins[a].at[me] if scatter else ins[a], outs[a].at[me], local_sems.at[a])

        def arrival(a, r):
            return copy(a, r, ins[a].at[me] if scatter else ins[a], peers[r][1], r)

        return me, peers, copy, local, arrival

    if scatter:
        def sends():
            me, peers, copy, local, _ = tools()
            return [local(a) for a in range(n)], [copy(a, r, ins[a].at[peers[r][1]], me, r) for a in range(n) for r in RELATIONS]

        def start():
            loc, out = sends()
            for cp in loc + out:
                cp.start()

        middle = None
        waited_last = RELATIONS
    else:
        def sends():
            me, peers, copy, local, _ = tools()
            own = [copy(a, r, ins[a], me, r) for a in range(n) for r in (SIBLING,) + OTHER_CHIPS]
            return [local(a) for a in range(n)], own

        def passes():
            me, peers, copy, _, _ = tools()
            return [copy(a, r, outs[a].at[peers[r - 1][1]], peers[r - 1][1], SIBLING) for a in range(n) for r in PASSED_ON]

        def start():
            loc, out = sends()
            for cp in loc + out:
                cp.start()

        def middle():
            _, _, _, _, arrival = tools()
            fwd = passes()
            for a in range(n):
                for i, r in enumerate(PASSED_ON):
                    arrival(a, r - 1).wait_recv()
                    fwd[a * len(PASSED_ON) + i].start()

        waited_last = (SIBLING,) + PASSED_ON

    def wait():
        _, _, _, _, arrival = tools()
        for a in range(n):
            for r in waited_last:
                arrival(a, r).wait_recv()
        loc, out = sends()
        for cp in out + ([] if scatter else passes()):
            cp.wait_send()
        for cp in loc:
            cp.wait()

    return start, middle, wait


def _exchange_shapes(arrs, scatter):
    return [_sds(a.shape if scatter else (N_DEV,) + a.shape, a.dtype) for a in arrs]


def _exchange_sems(n):
    return [pltpu.SemaphoreType.DMA((n * N_REL,)), pltpu.SemaphoreType.DMA((n * N_REL,)), pltpu.SemaphoreType.DMA((n,))]


def _exchange(arrs, name, scatter):
    n = len(arrs)

    def body(*refs):
        start, middle, wait = _exchange_ops(refs[:n], refs[n:2 * n], refs[2 * n:], scatter)
        start()
        if middle is not None:
            middle()
        wait()

    any_spec = pl.BlockSpec(memory_space=pl.ANY)
    return pl.pallas_call(body, name=name, in_specs=[any_spec] * n, out_specs=[any_spec] * n,
                          out_shape=_exchange_shapes(arrs, scatter), scratch_shapes=_exchange_sems(n))(*arrs)


def cast_bf16(w, name):
    _, r, c = w.shape
    tr = _tile(r, (256, 128, 64, 32, 16))

    def body(w_ref, o_ref):
        o_ref[...] = w_ref[0].astype(bf16)

    return _call(body, name, (r // tr,), [pl.BlockSpec((1, tr, c), lambda i: (0, i, 0))], pl.BlockSpec((tr, c), lambda i: (i, 0)),
                 _sds((r, c), bf16))(w)


def _adamw(w, g, m, v):
    m = ADAM_B1 * m + (1.0 - ADAM_B1) * g
    v = ADAM_B2 * v + (1.0 - ADAM_B2) * jnp.square(g)
    m_hat = m / (1.0 - ADAM_B1 ** ADAM_STEP)
    v_hat = v / (1.0 - ADAM_B2 ** ADAM_STEP)
    delta = -ADAM_LR * (m_hat / (jnp.sqrt(v_hat) + ADAM_EPS) + ADAM_WD * w)
    return delta, m, v


def adam_sum(parts, w, m, v, name):
    _, r, c = parts.shape
    budget = 4 * 1024 * 1024
    tr = r
    for cand in (1024, 512, 256, 128, 64, 32, 16):
        if r % cand == 0 and N_DEV * cand * c * 4 <= budget:
            tr = cand
            break

    def body(p_ref, w_ref, m_ref, v_ref, g_ref, d_ref, m2_ref, v2_ref):
        g = p_ref[0].astype(f32)
        for k in range(1, N_DEV):
            g = g + p_ref[k].astype(f32)
        d, m2, v2 = _adamw(w_ref[0], g, m_ref[0], v_ref[0])
        g_ref[...] = g
        d_ref[...] = d
        m2_ref[...] = m2
        v2_ref[...] = v2

    blk = pl.BlockSpec((1, tr, c), lambda i: (0, i, 0))
    out = pl.BlockSpec((tr, c), lambda i: (i, 0))
    return _call(body, name, (r // tr,), [pl.BlockSpec((N_DEV, tr, c), lambda i: (0, i, 0)), blk, blk, blk], [out] * 4,
                 [_sds((r, c), f32)] * 4)(parts, w, m, v)


def sum_parts(parts, name):
    _, r, c = parts.shape

    def body(p_ref, o_ref):
        g = p_ref[0]
        for k in range(1, N_DEV):
            g = g + p_ref[k]
        o_ref[...] = g

    return pl.pallas_call(body, name=name, out_shape=_sds((r, c), f32))(parts)


def adam_flat(w, g, m, v, name):
    def body(w_ref, g_ref, m_ref, v_ref, d_ref, m2_ref, v2_ref):
        d, m2, v2 = _adamw(w_ref[...], g_ref[...], m_ref[...], v_ref[...])
        d_ref[...] = d
        m2_ref[...] = m2
        v2_ref[...] = v2

    return pl.pallas_call(body, name=name, out_shape=[_sds(w.shape, f32)] * 3)(w, g, m, v)


def _pack(vecs, multiple):
    flat = jnp.concatenate([v.reshape(-1) for v in vecs])
    n = flat.shape[0]
    total = -(-n // multiple) * multiple
    return jnp.pad(flat, (0, total - n))


def _unpack(flat, shapes):
    out, pos = [], 0
    for s in shapes:
        n = 1
        for d in s:
            n *= d
        out.append(flat[pos:pos + n].reshape(s))
        pos += n
    return out


def _pad_lanes(v, width=TAIL):
    return jnp.pad(v, ((0, 0), (0, width - v.shape[1])))


def kernel(x, mem, positions, g_mix, w_in, g_qa, w_qb, g_kva, w_kvb, g_qn_nope, g_qn_pe, g_kn_nope, g_kn_pe, conv_qk, b_if, g_hnorm, p_a, p_b, w_out, g_cross, g_mem, wq_c, wk_c, wv_c, g_cq, g_ck, wo_c, g_ffn, w_up, conv_ffn, b_conv_ffn, w_down, loss_target, m_g_mix, m_w_in, m_g_qa, m_w_qb, m_g_kva, m_w_kvb, m_g_qn_nope, m_g_qn_pe, m_g_kn_nope, m_g_kn_pe, m_conv_qk, m_b_if, m_g_hnorm, m_p_a, m_p_b, m_w_out, m_g_cross, m_g_mem, m_wq_c, m_wk_c, m_wv_c, m_g_cq, m_g_ck, m_wo_c, m_g_ffn, m_w_up, m_conv_ffn, m_b_conv_ffn, m_w_down, v_g_mix, v_w_in, v_g_qa, v_w_qb, v_g_kva, v_w_kvb, v_g_qn_nope, v_g_qn_pe, v_g_kn_nope, v_g_kn_pe, v_conv_qk, v_b_if, v_g_hnorm, v_p_a, v_p_b, v_w_out, v_g_cross, v_g_mem, v_wq_c, v_wk_c, v_wv_c, v_g_cq, v_g_ck, v_wo_c, v_g_ffn, v_w_up, v_conv_ffn, v_b_conv_ffn, v_w_down):
    args = dict(locals())
    names = ['g_mix', 'w_in', 'g_qa', 'w_qb', 'g_kva', 'w_kvb', 'g_qn_nope', 'g_qn_pe', 'g_kn_nope', 'g_kn_pe', 'conv_qk', 'b_if',
             'g_hnorm', 'p_a', 'p_b', 'w_out', 'g_cross', 'g_mem', 'wq_c', 'wk_c', 'wv_c', 'g_cq', 'g_ck', 'wo_c', 'g_ffn', 'w_up',
             'conv_ffn', 'b_conv_ffn', 'w_down']
    big = ['w_in', 'w_qb', 'w_kvb', 'p_a', 'p_b', 'w_out', 'wq_c', 'wk_c', 'wv_c', 'wo_c', 'w_up', 'w_down']
    sharded_small = ['conv_qk', 'g_hnorm', 'conv_ffn']
    replicated = [n for n in names if n not in big and n not in sharded_small]

    t, d = x.shape[1], x.shape[2]
    x2d, tgt = x[0], loss_target[0]
    mem2d = mem[0]
    me = 4 * lax.axis_index("x") + 2 * lax.axis_index("y") + lax.axis_index("c")
    nc = t // CHUNK
    f2 = b_conv_ffn.shape[1]
    wmain = O_GA + 2 * d

    first = ['w_in', 'w_qb', 'w_kvb']
    behind_in = ['p_a', 'p_b', 'w_out', 'wq_c', 'wk_c', 'wv_c', 'wo_c']
    shards = {n: cast_bf16(args[n], "cast_" + n) for n in big}
    small_local = _pack([args[n] for n in sharded_small], 128).reshape(1, -1)
    gathered = _exchange([shards[n] for n in first] + [small_local], "comm_gather_first", scatter=False)
    gw = dict(zip(first, gathered[:-1]))
    small_all = gathered[-1]
    full_small, pos = [], 0
    for n in sharded_small:
        _, rows, cols = args[n].shape
        piece = small_all[:, 0, pos:pos + rows * cols].reshape(N_DEV, rows, cols)
        full_small.append(piece.transpose(1, 0, 2).reshape(rows, N_DEV * cols))
        pos += rows * cols
    conv_qk_f, g_hnorm_f, conv_ffn_f = full_small

    w_in_f = gw['w_in'].transpose(1, 0, 2).reshape(d, -1)
    c_kpe, c_q, c_i, c_o = O_Q, O_Q + ROPE, O_Q + ROPE + 2 * ML_QK + ML_V, O_Q + ROPE + 2 * ML_QK + ML_V + 2 * ML_HEADS
    w_main = jnp.concatenate([w_in_f[:, :c_kpe], w_in_f[:, c_q:c_i], w_in_f[:, c_o:]], axis=1)[None]
    w_tail = jnp.concatenate([w_in_f[:, c_kpe:c_q], w_in_f[:, c_i:c_o],
                              jnp.zeros((d, TAIL - ROPE - 2 * ML_HEADS), bf16)], axis=1)[None]
    assert w_main.shape[2] == wmain

    inv_freq = ROPE_BASE ** (-jnp.arange(0, ROPE, 2, dtype=f32) / ROPE)
    inv_tile = _pad_lanes(jnp.concatenate([inv_freq, inv_freq])[None])
    cos, sin = rope_tables(positions.reshape(t, 1), inv_tile)
    gqp, gkp = _pad_lanes(g_qn_pe), _pad_lanes(g_kn_pe)
    b_tile = jnp.pad(b_if, ((0, 0), (T_I, TAIL - T_I - 2 * ML_HEADS)))

    u0 = rms_fwd(x2d, g_mix, "rms_mix")
    z_main, got = mm_nn(u0, w_main, f32, "mm_in_main", exchange=([shards[n] for n in behind_in], False))
    gw.update(zip(behind_in, got))
    qb = gw['w_qb'].transpose(1, 0, 2).reshape(Q_LORA, MLA_HEADS, NOPE + ROPE)
    w_qb_p = jnp.concatenate([qb, jnp.zeros((Q_LORA, MLA_HEADS, HEAD_PAD - NOPE - ROPE), bf16)], axis=2).reshape(1, Q_LORA, -1)
    w_kvb3 = gw['w_kvb']
    p_a3, p_b3, w_out3 = (gw[n].reshape(1, -1, d) for n in ('p_a', 'p_b', 'w_out'))
    wq_c3, wk_c3, wv_c3 = (gw[n].reshape(1, d, -1) for n in ('wq_c', 'wk_c', 'wv_c'))
    wo_c3 = gw['wo_c']
    z_tail = mm_nn(u0, w_tail, f32, "mm_in_tail")
    qa_n, kv_n = lat_norm(z_main, g_qa, g_kva)
    q_raw = mm_nn(qa_n, w_qb_p, f32, "mm_qb")
    kv_raw = mm_nn(kv_n, w_kvb3, f32, "mm_kvb")
    qh, kh, vh = mla_prep(q_raw, kv_raw, z_tail, cos, sin, g_qn_nope, gqp, g_kn_nope, gkp)
    (o_a, o_ab, lse), (w_up3,) = mla_fwd(qh, kh, vh, exchange=([shards['w_up']], False))

    qk_act = qk_conv(z_main, conv_qk_f)
    gates = gate_act(z_tail, b_tile)

    def to_rows(cols):
        return cols.T.reshape(ML_HEADS, nc, 1, CHUNK)

    ig, fg = to_rows(gates[:, T_I:T_F]), to_rows(gates[:, T_F:T_F + ML_HEADS])
    h_ml, c_all, n_all, m_all = mlstm_fwd(qk_act, z_main, ig, fg)
    g_hn3 = g_hnorm_f.reshape(ML_HEADS, 1, ML_DV)
    y_b = mlstm_out(h_ml, z_main, g_hn3)

    ya = mm_nn(o_ab, p_a3, f32, "mm_pa")
    yb = mm_nn(y_b, p_b3, f32, "mm_pb")
    merged = merge_fwd(z_main, ya, yb)
    mo = mm_nn(merged, w_out3, f32, "mm_out")
    x1, uc = resid_rms(x2d, mo, g_cross, "resid_cross")
    mem_n = rms_fwd(mem2d, g_mem, "rms_mem")
    cq = mm_nn(uc, wq_c3, f32, "mm_cq")
    ck = mm_nn(mem_n, wk_c3, f32, "mm_ck")
    cv = mm_nn(mem_n, wv_c3, f32, "mm_cv")
    o_c = cross_fwd(cq, ck, cv, g_cq, g_ck)
    co = mm_nn(o_c, wo_c3, f32, "mm_oc")
    x2, u3 = resid_rms(x1, co, g_ffn, "resid_ffn")
    hup, (w_down_g,) = mm_nn(u3, w_up3, f32, "mm_up", exchange=([shards['w_down']], False))
    w_down3 = w_down_g.reshape(1, -1, d)
    gl = glu_fwd(hup, conv_ffn_f, b_conv_ffn)
    fo = mm_nn(gl, w_down3, f32, "mm_down")
    dx3, dx3_b, loss_acc = loss_head(x2, fo, tgt)

    grads, parts = {}, {}
    grads['w_down'] = mm_tn(gl, dx3_b, 1, "mm_d_wdown").reshape(N_DEV, -1, d)
    dgl = mm_nt(dx3_b, w_down3, f32, "mm_d_gl")
    (dh1, dh2, dcw1, dcw2, db1, db2), (parts['w_down'],) = glu_bwd(hup, conv_ffn_f, b_conv_ffn, dgl,
                                                                    exchange=([grads['w_down']], True))
    dhup, dconv_ffn, db_ffn = (jnp.concatenate(pair, axis=1) for pair in ((dh1, dh2), (dcw1, dcw2), (db1, db2)))
    grads['w_up'] = mm_tn(u3, dhup, N_DEV, "mm_d_wup")
    du3 = mm_nt(dhup, w_up3, f32, "mm_d_u3")
    dx2, dx2_b, dg_ffn = rms_bwd(x2, g_ffn, [du3], dx3, "rms_bwd_ffn", want_b16=True)
    grads['wo_c'] = mm_tn(o_c, dx2_b, N_DEV, "mm_d_woc")
    do_c = mm_nt(dx2_b, wo_c3, f32, "mm_d_oc")
    dcq, dck, dcv, dg_cq, dg_ck = cross_bwd(cq, ck, cv, g_cq, g_ck, do_c)
    grads['wq_c'] = mm_tn(uc, dcq, 1, "mm_d_wqc").reshape(N_DEV, -1, dcq.shape[1])
    grads['wk_c'] = mm_tn(mem_n, dck, 1, "mm_d_wkc").reshape(N_DEV, -1, dck.shape[1])
    grads['wv_c'] = mm_tn(mem_n, dcv, 1, "mm_d_wvc").reshape(N_DEV, -1, dcv.shape[1])
    duc = mm_nt(dcq, wq_c3, f32, "mm_d_uc")
    dmem_k = mm_nt(dck, wk_c3, f32, "mm_d_memk")
    dmem_v = mm_nt(dcv, wv_c3, f32, "mm_d_memv")
    dg_mem, = rms_bwd(mem2d, g_mem, [dmem_k, dmem_v], None, "rms_bwd_mem", want_dx=False)
    dx1, dx1_b, dg_cross = rms_bwd(x1, g_cross, [duc], dx2, "rms_bwd_cross", want_b16=True)
    grads['w_out'] = mm_tn(merged, dx1_b, 1, "mm_d_wout").reshape(N_DEV, -1, d)
    dmerged = mm_nt(dx1_b, w_out3, f32, "mm_d_merged")
    dga, dgb, dya, dyb = merge_bwd(z_main, ya, yb, dmerged)
    grads['p_a'] = mm_tn(o_ab, dya, 1, "mm_d_pa").reshape(N_DEV, -1, d)
    grads['p_b'] = mm_tn(y_b, dyb, 1, "mm_d_pb").reshape(N_DEV, -1, d)
    do_a = mm_nt(dya, p_a3, f32, "mm_d_oa")
    dy_b = mm_nt(dyb, p_b3, f32, "mm_d_yb")

    dh_ml, dzo, dg_hn = mlstm_out_bwd(h_ml, z_main, g_hn3, dy_b)
    mixers = ['p_a', 'p_b', 'w_out']
    (dq_act, dk_act, dzv, dig, dfg), got = mlstm_bwd(qk_act, z_main, ig, fg, c_all, n_all, m_all, dh_ml,
                                                     exchange=([grads[n] for n in mixers], True))
    parts.update(zip(mixers, got))
    dzqk, dconv_qk = qk_conv_bwd(z_main, conv_qk_f, dq_act, dk_act)

    delta = mla_delta(o_a, do_a)
    (dqh, dkh, dvh), (parts['w_up'],) = mla_bwd(qh, kh, vh, do_a, lse, delta.reshape(MLA_HEADS, 1, t),
                                                exchange=([grads['w_up']], True))
    dq_raw, dkv_raw, dzt_pe, dg_qn, dg_qp, dg_kn, dg_kp = mla_prep_bwd(
        q_raw, kv_raw, z_tail, cos, sin, g_qn_nope, gqp, g_kn_nope, gkp, dqh, dkh, dvh)
    d_wqb_p = mm_tn(qa_n, dq_raw, 1, "mm_d_wqb")[0].reshape(Q_LORA, MLA_HEADS, HEAD_PAD)[:, :, :NOPE + ROPE]
    grads['w_qb'] = d_wqb_p.reshape(Q_LORA, N_DEV, -1).transpose(1, 0, 2)
    grads['w_kvb'] = mm_tn(kv_n, dkv_raw, N_DEV, "mm_d_wkvb")
    dqa = mm_nt(dq_raw, w_qb_p, f32, "mm_d_qa")
    dkvn = mm_nt(dkv_raw, w_kvb3, f32, "mm_d_kvn")
    dz_lat, dg_qa, dg_kva = lat_norm_bwd(z_main, g_qa, g_kva, dqa, dkvn)

    def to_cols(rows):
        return rows.reshape(ML_HEADS, t).T

    dgate = jnp.pad(jnp.concatenate([to_cols(dig), to_cols(dfg)], axis=1), ((0, 0), (T_I, TAIL - T_I - 2 * ML_HEADS)))
    dz_tail, db_if = tail_bwd(z_tail, b_tile, dzt_pe, dgate)
    dz_main = [dz_lat, dzqk, dzv, dzo, dga, dgb]
    small_mats = ['wq_c', 'wk_c', 'wv_c', 'wo_c', 'w_qb', 'w_kvb']
    d_wmain3, got = mm_tn_cols(u0, dz_main, "mm_d_wmain", exchange=([grads[n] for n in small_mats], True))
    parts.update(zip(small_mats, got))
    d_wmain = d_wmain3[0]
    d_wtail = mm_tn(u0, dz_tail, 1, "mm_d_wtail")[0]
    d_win = jnp.concatenate([d_wmain[:, :O_Q], d_wtail[:, :ROPE], d_wmain[:, O_Q:O_O], d_wtail[:, T_I:T_I + 2 * ML_HEADS],
                             d_wmain[:, O_O:]], axis=1)
    grads['w_in'] = d_win.reshape(d, N_DEV, -1).transpose(1, 0, 2)
    du0_a, (parts['w_in'],) = mm_nt_cols(dz_main, w_main, f32, "mm_d_u0_main", exchange=([grads['w_in']], True))
    du0_b = mm_nt(dz_tail, w_tail, f32, "mm_d_u0_tail")
    grad_x, dg_mix = rms_bwd(x2d, g_mix, [du0_a, du0_b], dx1, "rms_bwd_mix")

    out_g, out_d, out_m, out_v = {}, {}, {}, {}
    for n in big:
        res = adam_sum(parts[n], args[n], args['m_' + n], args['v_' + n], "adam_" + n)
        out_g[n], out_d[n], out_m[n], out_v[n] = (a.reshape(args[n].shape) for a in res)

    small_full = {
        'g_mix': dg_mix[0], 'g_qa': dg_qa[0], 'g_kva': dg_kva[0], 'g_qn_nope': dg_qn[0], 'g_qn_pe': dg_qp[0, :ROPE],
        'g_kn_nope': dg_kn[0], 'g_kn_pe': dg_kp[0, :ROPE], 'conv_qk': dconv_qk, 'b_if': db_if[0, T_I:T_I + 2 * ML_HEADS],
        'g_hnorm': dg_hn[:, 0, :], 'g_cross': dg_cross[0], 'g_mem': dg_mem[0], 'g_cq': dg_cq[0], 'g_ck': dg_ck[0],
        'g_ffn': dg_ffn[0], 'conv_ffn': dconv_ffn, 'b_conv_ffn': db_ffn[0], 'loss': loss_acc[0, :1]}
    order = list(small_full)
    packed = _pack([small_full[n] for n in order], 8 * 128).reshape(1, -1)
    gathered_small, = _exchange([packed], "comm_gather_small", scatter=False)
    summed = sum_parts(gathered_small.reshape(N_DEV, -1, 128), "sum_small").reshape(-1)
    full_g = dict(zip(order, _unpack(summed, [small_full[n].shape for n in order])))
    loss = full_g['loss'][0]

    local_g = {}
    for n in replicated:
        local_g[n] = full_g[n].reshape(args[n].shape)
    for n in sharded_small:
        shp = args[n].shape
        full = full_g[n].reshape((1,) + full_g[n].shape)
        local_g[n] = lax.dynamic_slice_in_dim(full, me * shp[-1], shp[-1], axis=2)
    small = replicated + sharded_small
    dl_f, m_f, v_f = adam_flat(*[_pack([src[n] if pre == '' else args[pre + n] for n in small], 8 * 128).reshape(-1, 128)
                                 for pre, src in (('', args), ('', local_g), ('m_', None), ('v_', None))], "adam_small")
    shapes = [args[n].shape for n in small]
    for dst, flat in ((out_d, dl_f), (out_m, m_f), (out_v, v_f)):
        dst.update(zip(small, _unpack(flat.reshape(-1), shapes)))
    out_g.update(local_g)

    return (loss, grad_x[None], *[out_g[n] for n in names], *[out_d[n] for n in names],
            *[out_m[n] for n in names], *[out_v[n] for n in names])
```

```python
import functools

import jax
import jax.numpy as jnp
from jax import lax
from jax.experimental import pallas as pl
from jax.experimental.pallas import tpu as pltpu

f32 = jnp.float32
bf16 = jnp.bfloat16

N_DEV = 8
EPS = 1e-6
CHUNK = 64
CHUNK_SHIFT = 6
assert 1 << CHUNK_SHIFT == CHUNK
MLA_HEADS = 16
Q_LORA = 512
KV_LORA = 512
NOPE = 128
ROPE = 64
V_HEAD = 128
ROPE_BASE = 10000.0
HEAD_PAD = 256
ML_HEADS = 8
ML_DK = 128
ML_DV = 256
ML_CONV = 4
ML_QK = ML_HEADS * ML_DK
ML_V = ML_HEADS * ML_DV
CR_HEADS = 4
CR_HD = 128
FFN_CONV = 3
ADAM_LR = 0.001
ADAM_B1 = 0.9
ADAM_B2 = 0.999
ADAM_EPS = 1e-08
ADAM_WD = 0.01
ADAM_STEP = 10
O_QA, O_KV, O_Q, O_K = 0, Q_LORA, Q_LORA + KV_LORA, Q_LORA + KV_LORA + ML_QK
O_V = O_K + ML_QK
O_O = O_V + ML_V
O_GA = O_O + ML_V
TAIL = 128
T_I, T_F = ROPE, ROPE + ML_HEADS
VMEM_LIMIT_V7X = 48 * 1024 * 1024
MESH = pl.DeviceIdType.MESH


def _call(body, name, grid, in_specs, out_specs, out_shape, scratch=(), exchange=None):
    params = pltpu.CompilerParams(vmem_limit_bytes=VMEM_LIMIT_V7X)
    if exchange is None:
        return pl.pallas_call(body, name=name, grid=grid, in_specs=in_specs, out_specs=out_specs, out_shape=out_shape,
                              scratch_shapes=list(scratch), compiler_params=params)
    arrs, scatter = exchange
    single = not isinstance(out_specs, (list, tuple))
    o_specs = [out_specs] if single else list(out_specs)
    o_shape = [out_shape] if single else list(out_shape)
    n_in, n_out, n_sc, n = len(in_specs), len(o_specs), len(scratch), len(arrs)
    any_spec = pl.BlockSpec(memory_space=pl.ANY)

    def body_with_exchange(*refs):
        pos = [0]

        def take(k):
            pos[0] += k
            return refs[pos[0] - k:pos[0]]

        ins, ex_in, outs, ex_out, sc = take(n_in), take(n), take(n_out), take(n), take(n_sc)
        start, middle, wait = _exchange_ops(ex_in, ex_out, refs[pos[0]:], scatter)
        step, total = 0, 1
        for a in range(len(grid)):
            step = step * grid[a] + pl.program_id(a)
            total *= grid[a]
        pl.when(step == 0)(start)
        body(*ins, *outs, *sc)
        if middle is not None:
            pl.when(step == total // 2)(middle)
        pl.when(step == total - 1)(wait)

    call = pl.pallas_call(body_with_exchange, name="comm_" + name, grid=grid, in_specs=list(in_specs) + [any_spec] * n,
                          out_specs=o_specs + [any_spec] * n, out_shape=o_shape + _exchange_shapes(arrs, scatter),
                          scratch_shapes=list(scratch) + _exchange_sems(n), compiler_params=params)

    def run(*operands):
        res = call(*operands, *arrs)
        return (res[0] if single else list(res[:n_out])), list(res[n_out:])

    return run


def _tile(n, cands):
    for c in cands:
        if n % c == 0:
            return c
    return n


def _sds(shape, dtype):
    return jax.ShapeDtypeStruct(tuple(shape), dtype)


def _bdot(a, b, ca, cb):
    return lax.dot_general(a.astype(bf16), b.astype(bf16), (((ca,), (cb,)), ((), ())), preferred_element_type=f32)


_BIG = (1024, 512, 256, 128)


def _col_tile(nb):
    return nb if nb <= 1536 else _tile(nb, _BIG)


_DEEP = (2048, 1024, 512, 256, 128)


def _mm_call(name, grid, in_specs, out_spec, out_shape, tile, nk, ca, cb, exchange, operands):
    def dot(a_ref, w_ref):
        return _bdot(a_ref[...], w_ref[0] if len(w_ref.shape) == 3 else w_ref[...], ca, cb)

    def store(o_ref, val):
        if len(o_ref.shape) == 3:
            o_ref[0] = val.astype(o_ref.dtype)
        else:
            o_ref[...] = val.astype(o_ref.dtype)

    if nk == 1:
        def body(a_ref, w_ref, o_ref):
            store(o_ref, dot(a_ref, w_ref))

        scratch = []
    else:
        def body(a_ref, w_ref, o_ref, acc):
            kk = pl.program_id(2)

            @pl.when(kk == 0)
            def _():
                acc[...] = jnp.zeros_like(acc)

            acc[...] += dot(a_ref, w_ref)

            @pl.when(kk == nk - 1)
            def _():
                store(o_ref, acc[...])

        scratch = [pltpu.VMEM(tile, f32)]
    return _call(body, name, grid, in_specs, out_spec, out_shape, scratch, exchange=exchange)(*operands)


def mm_nn(a, w3, out_dtype, name, exchange=None):
    m, k = a.shape
    nblk, k2, nb = w3.shape
    assert k == k2
    tm, tk, tn = _tile(m, _BIG), _tile(k, _DEEP), _col_tile(nb)
    per, nk = nb // tn, k // tk
    return _mm_call(name, (m // tm, nblk * per, nk),
                    [pl.BlockSpec((tm, tk), lambda i, j, kk: (i, kk)),
                     pl.BlockSpec((1, tk, tn), lambda i, j, kk: (j // per, kk, j % per))],
                    pl.BlockSpec((tm, tn), lambda i, j, kk: (i, j)), _sds((m, nblk * nb), out_dtype),
                    (tm, tn), nk, 1, 0, exchange, (a, w3))


def mm_nt(a, w3, out_dtype, name, exchange=None):
    m, n = a.shape
    nblk, k, nb = w3.shape
    assert n == nblk * nb
    tm, tn = _tile(m, _BIG), _tile(k, _BIG)
    tc = nb if nb <= 1536 else _tile(nb, _DEEP)
    per = nb // tc
    nk = nblk * per
    return _mm_call(name, (m // tm, k // tn, nk),
                    [pl.BlockSpec((tm, tc), lambda i, j, kk: (i, kk)),
                     pl.BlockSpec((1, tn, tc), lambda i, j, kk: (kk // per, j, kk % per))],
                    pl.BlockSpec((tm, tn), lambda i, j, kk: (i, j)), _sds((m, k), out_dtype),
                    (tm, tn), nk, 1, 1, exchange, (a, w3))


def mm_tn(a, b, nblk, name, exchange=None):
    r, m = a.shape
    r2, n = b.shape
    assert r == r2 and n % nblk == 0
    nb = n // nblk
    tm, tk, tn = _tile(m, _BIG), _tile(r, _DEEP), _col_tile(nb)
    per, nk = nb // tn, r // tk
    return _mm_call(name, (m // tm, nblk * per, nk),
                    [pl.BlockSpec((tk, tm), lambda i, j, kk: (kk, i)),
                     pl.BlockSpec((tk, tn), lambda i, j, kk: (kk, j))],
                    pl.BlockSpec((1, tm, tn), lambda i, j, kk: (j // per, i, j % per)), _sds((nblk, m, nb), bf16),
                    (tm, tn), nk, 0, 0, exchange, (a, b))


def _section_tiles(sections, cands):
    widths = [s.shape[1] for s in sections]
    tile = next(c for c in cands if all(w % c == 0 for w in widths))
    counts = [w // tile for w in widths]
    firsts = [sum(counts[:i]) for i in range(len(counts))]
    return tile, firsts, counts


SECTION_VMEM_BYTES = 24 * 1024 * 1024


def mm_tn_cols(a, sections, nblk, name, exchange=None):
    r, m = a.shape
    n = sum(s.shape[1] for s in sections)
    nb = n // nblk
    tn, firsts, counts = _section_tiles(sections, (nb,) if nb <= 1536 else _BIG)
    per = nb // tn
    tm = _tile(m, _BIG)
    tk = next(c for c in _DEEP if r % c == 0 and len(sections) * c * tn * 4 <= SECTION_VMEM_BYTES)
    nk = r // tk

    def body(a_ref, *refs):
        b_refs, o_ref, acc = refs[:len(sections)], refs[len(sections)], refs[len(sections) + 1]
        j, kk = pl.program_id(1), pl.program_id(2)

        @pl.when(kk == 0)
        def _():
            acc[...] = jnp.zeros_like(acc)

        for b_ref, lo, cnt in zip(b_refs, firsts, counts):
            @pl.when(jnp.logical_and(j >= lo, j < lo + cnt))
            def _(b_ref=b_ref):
                acc[...] += _bdot(a_ref[...], b_ref[...], 0, 0)

        @pl.when(kk == nk - 1)
        def _():
            o_ref[0] = acc[...].astype(bf16)

    def spec(lo, cnt):
        def index(i, j, kk):
            return jnp.where(j < lo, 0, jnp.where(j >= lo + cnt, nk - 1, kk)), jnp.clip(j - lo, 0, cnt - 1)
        return pl.BlockSpec((tk, tn), index)

    return _call(body, name, (m // tm, n // tn, nk),
                 [pl.BlockSpec((tk, tm), lambda i, j, kk: (kk, i))] + [spec(lo, cnt) for lo, cnt in zip(firsts, counts)],
                 pl.BlockSpec((1, tm, tn), lambda i, j, kk: (j // per, i, j % per)), _sds((nblk, m, nb), bf16),
                 [pltpu.VMEM((tm, tn), f32)], exchange=exchange)(a, *sections)


def mm_nt_cols(sections, w3, out_dtype, name, exchange=None):
    m = sections[0].shape[0]
    nblk, k, nb = w3.shape
    tc, firsts, counts = _section_tiles(sections, (nb,) if nb <= 1536 else _DEEP)
    assert nblk * nb == tc * sum(counts) and nb % tc == 0
    per = nb // tc
    tm, tn = _tile(m, _BIG), _tile(k, _BIG)
    nk = nblk * per

    def body(*refs):
        a_refs, w_ref, o_ref, acc = refs[:len(sections)], refs[len(sections)], refs[len(sections) + 1], refs[len(sections) + 2]
        kk = pl.program_id(2)

        @pl.when(kk == 0)
        def _():
            acc[...] = jnp.zeros_like(acc)

        for a_ref, lo, cnt in zip(a_refs, firsts, counts):
            @pl.when(jnp.logical_and(kk >= lo, kk < lo + cnt))
            def _(a_ref=a_ref):
                acc[...] += _bdot(a_ref[...], w_ref[0], 1, 1)

        @pl.when(kk == nk - 1)
        def _():
            o_ref[...] = acc[...].astype(o_ref.dtype)

    def spec(lo, cnt):
        return pl.BlockSpec((tm, tc), lambda i, j, kk: (i, jnp.clip(kk - lo, 0, cnt - 1)))

    return _call(body, name, (m // tm, k // tn, nk),
                 [spec(lo, cnt) for lo, cnt in zip(firsts, counts)] + [pl.BlockSpec((1, tn, tc), lambda i, j, kk: (kk // per, j, kk % per))],
                 pl.BlockSpec((tm, tn), lambda i, j, kk: (i, j)), _sds((m, k), out_dtype),
                 [pltpu.VMEM((tm, tn), f32)], exchange=exchange)(*sections, w3)


def _rms(x, g):
    return x * lax.rsqrt(jnp.mean(x * x, axis=-1, keepdims=True) + EPS) * g


def _rms_pad(x, g, width):
    return x * lax.rsqrt(jnp.sum(x * x, axis=-1, keepdims=True) / width + EPS) * g


def _first(*ids):
    ok = ids[0] == 0
    for i in ids[1:]:
        ok = jnp.logical_and(ok, i == 0)
    return ok


def _acc_row(ref, val, first):
    @pl.when(first)
    def _():
        ref[...] = jnp.zeros_like(ref)

    ref[0:1, :] += val


def rms_fwd(x, g, name):
    r, w = x.shape
    tm = _tile(r, (256, 128, 64, 32, 16, 8))

    def body(x_ref, g_ref, o_ref):
        o_ref[...] = _rms(x_ref[...], g_ref[...]).astype(bf16)

    return _call(body, name, (r // tm,), [pl.BlockSpec((tm, w), lambda i: (i, 0)), pl.BlockSpec((1, w), lambda i: (0, 0))],
                 pl.BlockSpec((tm, w), lambda i: (i, 0)), _sds((r, w), bf16))(x, g)


def resid_rms(xa, xb, g, name):
    r, w = xa.shape
    tm = _tile(r, (256, 128, 64, 32, 16, 8))

    def body(a_ref, b_ref, g_ref, s_ref, u_ref):
        xs = a_ref[...] + b_ref[...]
        s_ref[...] = xs
        u_ref[...] = _rms(xs, g_ref[...]).astype(bf16)

    row = pl.BlockSpec((tm, w), lambda i: (i, 0))
    return _call(body, name, (r // tm,), [row, row, pl.BlockSpec((1, w), lambda i: (0, 0))], [row, row],
                 [_sds((r, w), f32), _sds((r, w), bf16)])(xa, xb, g)


def rms_bwd(x, g, dys, dres, name, want_dx=True, want_b16=False):
    r, w = x.shape
    tm = _tile(r, (256, 128, 64, 32, 16, 8))
    nd = len(dys)

    def body(*refs):
        x_ref, g_ref = refs[0], refs[1]
        dy = refs[2][...]
        for j in range(1, nd):
            dy = dy + refs[2 + j][...]
        pos = 2 + nd
        _, vjp = jax.vjp(_rms, x_ref[...], g_ref[...])
        dx, dg = vjp(dy)
        if dres is not None:
            dx = dx + refs[pos][...]
            pos += 1
        if want_dx:
            refs[pos][...] = dx
            pos += 1
        if want_b16:
            refs[pos][...] = dx.astype(bf16)
            pos += 1
        _acc_row(refs[pos], dg, pl.program_id(0) == 0)

    row = pl.BlockSpec((tm, w), lambda i: (i, 0))
    ins = [x, g] + list(dys) + ([dres] if dres is not None else [])
    in_specs = [row, pl.BlockSpec((1, w), lambda i: (0, 0))] + [row] * (nd + (dres is not None))
    out_specs = [row] * (want_dx + want_b16) + [pl.BlockSpec((8, w), lambda i: (0, 0))]
    out_shape = ([_sds((r, w), f32)] if want_dx else []) + ([_sds((r, w), bf16)] if want_b16 else []) + [_sds((8, w), f32)]
    return _call(body, name, (r // tm,), in_specs, out_specs, out_shape)(*ins)


def lat_norm(z_main, g_qa, g_kva):
    t = z_main.shape[0]
    tm = _tile(t, (512, 256, 128, 64))

    def body(z_ref, gq_ref, gk_ref, q_ref, k_ref):
        q_ref[...] = _rms(z_ref[:, :Q_LORA], gq_ref[...]).astype(bf16)
        k_ref[...] = _rms(z_ref[:, Q_LORA:], gk_ref[...]).astype(bf16)

    return _call(body, "lat_norm", (t // tm,),
                 [pl.BlockSpec((tm, Q_LORA + KV_LORA), lambda i: (i, 0)), pl.BlockSpec((1, Q_LORA), lambda i: (0, 0)),
                  pl.BlockSpec((1, KV_LORA), lambda i: (0, 0))],
                 [pl.BlockSpec((tm, Q_LORA), lambda i: (i, 0)), pl.BlockSpec((tm, KV_LORA), lambda i: (i, 0))],
                 [_sds((t, Q_LORA), bf16), _sds((t, KV_LORA), bf16)])(z_main, g_qa, g_kva)


def lat_norm_bwd(z_main, g_qa, g_kva, dqa, dkv):
    t = z_main.shape[0]
    tm = _tile(t, (512, 256, 128, 64))

    def body(z_ref, gq_ref, gk_ref, dq_ref, dk_ref, dz_ref, dgq_ref, dgk_ref):
        first = pl.program_id(0) == 0
        _, vq = jax.vjp(_rms, z_ref[:, :Q_LORA], gq_ref[...])
        dx, dg = vq(dq_ref[...])
        dz_ref[:, :Q_LORA] = dx.astype(bf16)
        _acc_row(dgq_ref, dg, first)
        _, vk = jax.vjp(_rms, z_ref[:, Q_LORA:], gk_ref[...])
        dx, dg = vk(dk_ref[...])
        dz_ref[:, Q_LORA:] = dx.astype(bf16)
        _acc_row(dgk_ref, dg, first)

    return _call(body, "lat_norm_bwd", (t // tm,),
                 [pl.BlockSpec((tm, Q_LORA + KV_LORA), lambda i: (i, 0)), pl.BlockSpec((1, Q_LORA), lambda i: (0, 0)),
                  pl.BlockSpec((1, KV_LORA), lambda i: (0, 0)), pl.BlockSpec((tm, Q_LORA), lambda i: (i, 0)),
                  pl.BlockSpec((tm, KV_LORA), lambda i: (i, 0))],
                 [pl.BlockSpec((tm, Q_LORA + KV_LORA), lambda i: (i, 0)), pl.BlockSpec((8, Q_LORA), lambda i: (0, 0)),
                  pl.BlockSpec((8, KV_LORA), lambda i: (0, 0))],
                 [_sds((t, Q_LORA + KV_LORA), bf16), _sds((8, Q_LORA), f32), _sds((8, KV_LORA), f32)])(z_main, g_qa, g_kva, dqa, dkv)


def rope_tables(pos_col, inv_freq):
    t = pos_col.shape[0]
    tm = _tile(t, (512, 256, 128, 64))

    def body(p_ref, f_ref, c_ref, s_ref):
        ang = p_ref[...].astype(f32) * f_ref[...]
        lane = lax.broadcasted_iota(jnp.int32, ang.shape, 1)
        c_ref[...] = jnp.where(lane < ROPE, jnp.cos(ang), 0.0)
        sn = jnp.sin(ang)
        s_ref[...] = jnp.where(lane < ROPE // 2, -sn, jnp.where(lane < ROPE, sn, 0.0))

    return _call(body, "rope_tables", (t // tm,),
                 [pl.BlockSpec((tm, 1), lambda i: (i, 0)), pl.BlockSpec((1, TAIL), lambda i: (0, 0))],
                 [pl.BlockSpec((tm, TAIL), lambda i: (i, 0))] * 2, [_sds((t, TAIL), f32)] * 2)(pos_col, inv_freq)


def _swap_halves(n):
    lane = lax.broadcasted_iota(jnp.int32, n.shape, 1)
    return jnp.where(lane < ROPE // 2, pltpu.roll(n, TAIL - ROPE // 2, 1), pltpu.roll(n, ROPE // 2, 1))


def _rope(n, c, s):
    return n * c + _swap_halves(n) * s


def _rope_t(d, c, s):
    return d * c + _swap_halves(d * s)


def _prep_specs(tm):
    head = pl.BlockSpec((tm, HEAD_PAD), lambda i, h: (i, h))
    row = pl.BlockSpec((tm, TAIL), lambda i, h: (i, 0))
    gain = pl.BlockSpec((1, TAIL), lambda i, h: (0, 0))
    return head, row, gain


def _pe_in(zt):
    lane = lax.broadcasted_iota(jnp.int32, zt.shape, 1)
    return jnp.where(lane < ROPE, zt, 0.0)


def mla_prep(q_raw, kv_raw, z_tail, cos, sin, gqn, gqp, gkn, gkp):
    t = q_raw.shape[0]
    tm = _tile(t, (1024, 512, 256, 128, 64))

    def body(q_ref, kv_ref, zt_ref, c_ref, s_ref, gqn_ref, gqp_ref, gkn_ref, gkp_ref, qh_ref, kh_ref, vh_ref):
        c, s = c_ref[...], s_ref[...]
        qh_ref[:, :NOPE] = _rms(q_ref[:, :NOPE], gqn_ref[...]).astype(bf16)
        qh_ref[:, NOPE:] = _rope(_rms_pad(q_ref[:, NOPE:], gqp_ref[...], ROPE), c, s).astype(bf16)
        kh_ref[:, :NOPE] = _rms(kv_ref[:, :NOPE], gkn_ref[...]).astype(bf16)
        kh_ref[:, NOPE:] = _rope(_rms_pad(_pe_in(zt_ref[...]), gkp_ref[...], ROPE), c, s).astype(bf16)
        vh_ref[...] = kv_ref[:, NOPE:].astype(bf16)

    head, row, gain = _prep_specs(tm)
    return _call(body, "mla_prep", (t // tm, MLA_HEADS), [head, head, row, row, row, gain, gain, gain, gain],
                 [head, head, pl.BlockSpec((tm, V_HEAD), lambda i, h: (i, h))],
                 [_sds((t, MLA_HEADS * HEAD_PAD), bf16), _sds((t, MLA_HEADS * HEAD_PAD), bf16), _sds((t, MLA_HEADS * V_HEAD), bf16)],
                 )(q_raw, kv_raw, z_tail, cos, sin, gqn, gqp, gkn, gkp)


def mla_prep_bwd(q_raw, kv_raw, z_tail, cos, sin, gqn, gqp, gkn, gkp, dqh, dkh, dvh):
    t = q_raw.shape[0]
    tm = _tile(t, (1024, 512, 256, 128, 64))
    pad_norm = functools.partial(_rms_pad, width=ROPE)

    def body(q_ref, kv_ref, zt_ref, c_ref, s_ref, gqn_ref, gqp_ref, gkn_ref, gkp_ref, dqh_ref, dkh_ref, dvh_ref,
             dq_ref, dkv_ref, dzt_ref, dgqn_ref, dgqp_ref, dgkn_ref, dgkp_ref):
        i, h = pl.program_id(0), pl.program_id(1)
        first = _first(i, h)
        c, s = c_ref[...], s_ref[...]
        _, v1 = jax.vjp(_rms, q_ref[:, :NOPE], gqn_ref[...])
        dx, dg = v1(dqh_ref[:, :NOPE])
        dq_ref[:, :NOPE] = dx.astype(bf16)
        _acc_row(dgqn_ref, dg, first)
        _, v2 = jax.vjp(pad_norm, q_ref[:, NOPE:], gqp_ref[...])
        dx, dg = v2(_rope_t(dqh_ref[:, NOPE:], c, s))
        dq_ref[:, NOPE:] = dx.astype(bf16)
        _acc_row(dgqp_ref, dg, first)
        _, v3 = jax.vjp(_rms, kv_ref[:, :NOPE], gkn_ref[...])
        dx, dg = v3(dkh_ref[:, :NOPE])
        dkv_ref[:, :NOPE] = dx.astype(bf16)
        _acc_row(dgkn_ref, dg, first)
        dkv_ref[:, NOPE:] = dvh_ref[...].astype(bf16)
        _, v4 = jax.vjp(pad_norm, _pe_in(zt_ref[...]), gkp_ref[...])
        dx, dg = v4(_rope_t(dkh_ref[:, NOPE:], c, s))
        _acc_row(dgkp_ref, dg, first)

        @pl.when(h == 0)
        def _():
            dzt_ref[...] = jnp.zeros_like(dzt_ref)

        dzt_ref[...] += dx

    head, row, gain = _prep_specs(tm)
    acc = pl.BlockSpec((8, TAIL), lambda i, h: (0, 0))
    vspec = pl.BlockSpec((tm, V_HEAD), lambda i, h: (i, h))
    return _call(body, "mla_prep_bwd", (t // tm, MLA_HEADS),
                 [head, head, row, row, row, gain, gain, gain, gain, head, head, vspec],
                 [head, head, row, acc, acc, acc, acc],
                 [_sds((t, MLA_HEADS * HEAD_PAD), bf16), _sds((t, MLA_HEADS * HEAD_PAD), bf16), _sds((t, TAIL), f32)]
                 + [_sds((8, TAIL), f32)] * 4)(q_raw, kv_raw, z_tail, cos, sin, gqn, gqp, gkn, gkp, dqh, dkh, dvh)


ATT_BLOCK = 512
NEG = -1e30
ATT_SCALE = (NOPE + ROPE) ** -0.5
ATT_HEADS = 2
ATT_HEADS_FWD = 4


def _chunk_visible(shape, key_axis):
    kc = lax.broadcasted_iota(jnp.int32, shape, key_axis) >> CHUNK_SHIFT
    qc = lax.broadcasted_iota(jnp.int32, shape, 1 - key_axis) >> CHUNK_SHIFT
    return kc <= qc


def mla_fwd(qh, kh, vh, exchange=None):
    t = qh.shape[0]
    tb = min(ATT_BLOCK, t)
    nb = t // tb

    hp = ATT_HEADS_FWD

    def body(q_ref, k_ref, v_ref, o_ref, ob_ref, lse_ref, m_s, l_s, acc):
        qi, ki = pl.program_id(1), pl.program_id(2)

        @pl.when(ki == 0)
        def _():
            m_s[...] = jnp.full_like(m_s, NEG)
            l_s[...] = jnp.zeros_like(l_s)
            acc[...] = jnp.zeros_like(acc)

        def step(diagonal):
            new = []
            for j in range(hp):
                q, k = q_ref[:, j * HEAD_PAD:(j + 1) * HEAD_PAD], k_ref[:, j * HEAD_PAD:(j + 1) * HEAD_PAD]
                s = _bdot(k, q, 1, 1) * ATT_SCALE
                if diagonal:
                    s = jnp.where(_chunk_visible(s.shape, 0), s, -jnp.inf)
                m_old = m_s[j]
                m_new = jnp.maximum(m_old, jnp.max(s, axis=0, keepdims=True))
                p = jnp.exp(s - m_new)
                alpha = jnp.exp(m_old - m_new)
                l_new = alpha * l_s[j] + jnp.sum(p, axis=0, keepdims=True)
                acc_new = alpha * acc[j] + _bdot(v_ref[:, j * V_HEAD:(j + 1) * V_HEAD], p, 0, 0)
                new.append((m_new, l_new, acc_new))
            for j, (m_new, l_new, acc_new) in enumerate(new):
                m_s[j] = m_new
                l_s[j] = l_new
                acc[j] = acc_new
            return new

        @pl.when(ki < qi)
        def _():
            step(False)

        @pl.when(ki == qi)
        def _():
            for j, (m_new, l_new, acc_new) in enumerate(step(True)):
                o = (acc_new / l_new).T
                o_ref[:, j * V_HEAD:(j + 1) * V_HEAD] = o
                ob_ref[:, j * V_HEAD:(j + 1) * V_HEAD] = o.astype(bf16)
                lse_ref[j] = m_new + jnp.log(l_new)

    kv = lambda g, qi, ki: (jnp.minimum(ki, qi), g)
    o_spec = pl.BlockSpec((tb, hp * V_HEAD), lambda g, qi, ki: (qi, g))
    return _call(body, "mla_fwd", (MLA_HEADS // hp, nb, nb),
                 [pl.BlockSpec((tb, hp * HEAD_PAD), lambda g, qi, ki: (qi, g)), pl.BlockSpec((tb, hp * HEAD_PAD), kv),
                  pl.BlockSpec((tb, hp * V_HEAD), kv)],
                 [o_spec, o_spec, pl.BlockSpec((hp, 1, tb), lambda g, qi, ki: (g, 0, qi))],
                 [_sds((t, MLA_HEADS * V_HEAD), f32), _sds((t, MLA_HEADS * V_HEAD), bf16), _sds((MLA_HEADS, 1, t), f32)],
                 [pltpu.VMEM((hp, 1, tb), f32), pltpu.VMEM((hp, 1, tb), f32), pltpu.VMEM((hp, V_HEAD, tb), f32)],
                 exchange=exchange)(qh, kh, vh)


def mla_delta(o, do):
    t = o.shape[0]
    tm = _tile(t, (512, 256, 128, 64))

    def body(o_ref, do_ref, d_ref):
        for h in range(MLA_HEADS):
            cols = slice(h * V_HEAD, (h + 1) * V_HEAD)
            d_ref[h] = jnp.sum(o_ref[:, cols] * do_ref[:, cols], axis=1, keepdims=True)

    blk = pl.BlockSpec((tm, MLA_HEADS * V_HEAD), lambda i: (i, 0))
    return _call(body, "mla_delta", (t // tm,), [blk, blk], pl.BlockSpec((MLA_HEADS, tm, 1), lambda i: (0, i, 0)),
                 _sds((MLA_HEADS, t, 1), f32))(o, do)


def mla_bwd(qh, kh, vh, do, lse_row, delta_row, exchange=None):
    t = qh.shape[0]
    tb = min(ATT_BLOCK, t)
    nb = t // tb

    hp = ATT_HEADS

    def body(q_ref, k_ref, v_ref, do_ref, lse_ref, dl_ref, dq_ref, dk_ref, dv_ref, dk_acc, dv_acc):
        ki, qi = pl.program_id(1), pl.program_id(2)

        @pl.when(jnp.logical_and(ki == 0, qi == 0))
        def _():
            dq_ref[...] = jnp.zeros_like(dq_ref)

        @pl.when(qi == 0)
        def _():
            dk_acc[...] = jnp.zeros_like(dk_acc)
            dv_acc[...] = jnp.zeros_like(dv_acc)

        def step(diagonal):
            rows = pl.ds(pl.multiple_of(qi * tb, tb), tb)
            new = []
            for j in range(hp):
                qc, vc = slice(j * HEAD_PAD, (j + 1) * HEAD_PAD), slice(j * V_HEAD, (j + 1) * V_HEAD)
                q, k, do_b = q_ref[:, qc], k_ref[:, qc], do_ref[:, vc]
                s = _bdot(k, q, 1, 1) * ATT_SCALE
                if diagonal:
                    s = jnp.where(_chunk_visible(s.shape, 0), s, -jnp.inf)
                p = jnp.exp(s - lse_ref[j])
                dp = _bdot(v_ref[:, vc], do_b, 1, 1)
                ds = p * (dp - dl_ref[j]) * ATT_SCALE
                new.append((dv_acc[:, vc] + _bdot(p, do_b, 1, 0), dk_acc[:, qc] + _bdot(ds, q, 1, 0),
                            dq_ref[rows, qc] + _bdot(ds, k, 0, 0)))
            for j, (dv, dk, dq) in enumerate(new):
                dv_acc[:, j * V_HEAD:(j + 1) * V_HEAD] = dv
                dk_acc[:, j * HEAD_PAD:(j + 1) * HEAD_PAD] = dk
                dq_ref[rows, j * HEAD_PAD:(j + 1) * HEAD_PAD] = dq

        @pl.when(qi > ki)
        def _():
            step(False)

        @pl.when(qi == ki)
        def _():
            step(True)

        @pl.when(qi == nb - 1)
        def _():
            dk_ref[...] = dk_acc[...]
            dv_ref[...] = dv_acc[...]

    qs = lambda g, ki, qi: (jnp.maximum(qi, ki), g)
    ks = lambda g, ki, qi: (ki, g)
    vec = pl.BlockSpec((hp, 1, tb), lambda g, ki, qi: (g, 0, jnp.maximum(qi, ki)))
    return _call(body, "mla_bwd", (MLA_HEADS // hp, nb, nb),
                 [pl.BlockSpec((tb, hp * HEAD_PAD), qs), pl.BlockSpec((tb, hp * HEAD_PAD), ks), pl.BlockSpec((tb, hp * V_HEAD), ks),
                  pl.BlockSpec((tb, hp * V_HEAD), qs), vec, vec],
                 [pl.BlockSpec((t, hp * HEAD_PAD), lambda g, ki, qi: (0, g)), pl.BlockSpec((tb, hp * HEAD_PAD), ks),
                  pl.BlockSpec((tb, hp * V_HEAD), ks)],
                 [_sds((t, MLA_HEADS * HEAD_PAD), f32), _sds((t, MLA_HEADS * HEAD_PAD), f32), _sds((t, MLA_HEADS * V_HEAD), f32)],
                 [pltpu.VMEM((tb, hp * HEAD_PAD), f32), pltpu.VMEM((tb, hp * V_HEAD), f32)], exchange=exchange)(
        qh, kh, vh, do, lse_row, delta_row)


PAD = 8


def _conv_taps(pad_ref, w, width, t):
    y = pad_ref[PAD - width + 1:PAD - width + 1 + t, :] * w[0:1, :]
    for j in range(1, width):
        y = y + pad_ref[PAD - width + 1 + j:PAD - width + 1 + j + t, :] * w[j:j + 1, :]
    return y


def _conv_bwd(xpad_ref, dpad_ref, w, da, width, t):
    dpad_ref[0:t, :] = da
    dpad_ref[t:t + PAD, :] = jnp.zeros((PAD, da.shape[1]), f32)
    dx = dpad_ref[width - 1:width - 1 + t, :] * w[0:1, :]
    for j in range(1, width):
        dx = dx + dpad_ref[width - 1 - j:width - 1 - j + t, :] * w[j:j + 1, :]
    dws = [jnp.sum(da * xpad_ref[PAD - width + 1 + j:PAD - width + 1 + j + t, :], axis=0, keepdims=True) for j in range(width)]
    return dx, dws


def _load_pad(pad_ref, x, t):
    pad_ref[0:PAD, :] = jnp.zeros((PAD, x.shape[1]), f32)
    pad_ref[PAD:PAD + t, :] = x


assert ML_DK == 128


def qk_conv(z_main, conv_qk):
    t = z_main.shape[0]
    base = O_Q // ML_DK

    def body(z_ref, w_ref, o_ref, pad):
        _load_pad(pad, z_ref[...], t)
        a = _conv_taps(pad, w_ref[...], ML_CONV, t)
        sc = jnp.where(pl.program_id(0) < ML_HEADS, ML_DK ** -0.5, 1.0)
        o_ref[0] = jax.nn.silu(a) * sc

    return _call(body, "qk_conv", (2 * ML_HEADS,),
                 [pl.BlockSpec((t, ML_DK), lambda j: (0, base + j)), pl.BlockSpec((ML_CONV, ML_DK), lambda j: (0, j))],
                 pl.BlockSpec((1, t, ML_DK), lambda j: (j, 0, 0)), _sds((2 * ML_HEADS, t, ML_DK), f32),
                 [pltpu.VMEM((t + PAD, ML_DK), f32)])(z_main, conv_qk)


def qk_conv_bwd(z_main, conv_qk, dq, dk):
    t = z_main.shape[0]
    base = O_Q // ML_DK

    def body(z_ref, w_ref, dq_ref, dk_ref, dz_ref, dw_ref, pad, dpad):
        _load_pad(pad, z_ref[...], t)
        w = w_ref[...]
        a = _conv_taps(pad, w, ML_CONV, t)
        is_q = pl.program_id(0) < ML_HEADS
        d = jnp.where(is_q, dq_ref[0] * (ML_DK ** -0.5), dk_ref[0])
        _, vjp = jax.vjp(jax.nn.silu, a)
        da, = vjp(d)
        dx, dws = _conv_bwd(pad, dpad, w, da, ML_CONV, t)
        dz_ref[...] = dx.astype(bf16)
        for j in range(ML_CONV):
            dw_ref[j:j + 1, :] = dws[j]

    head = lambda pick: pl.BlockSpec((1, t, ML_DK), lambda j: (pick(j), 0, 0))
    return _call(body, "qk_conv_bwd", (2 * ML_HEADS,),
                 [pl.BlockSpec((t, ML_DK), lambda j: (0, base + j)), pl.BlockSpec((ML_CONV, ML_DK), lambda j: (0, j)),
                  head(lambda j: jnp.minimum(j, ML_HEADS - 1)), head(lambda j: jnp.maximum(j - ML_HEADS, 0))],
                 [pl.BlockSpec((t, ML_DK), lambda j: (0, j)), pl.BlockSpec((ML_CONV, ML_DK), lambda j: (0, j))],
                 [_sds((t, 2 * ML_QK), bf16), _sds((ML_CONV, 2 * ML_QK), f32)],
                 [pltpu.VMEM((t + PAD, ML_DK), f32), pltpu.VMEM((t + PAD, ML_DK), f32)])(z_main, conv_qk, dq, dk)


def glu_fwd(hup, conv_w, bias):
    t, f2 = hup.shape
    nf = f2 // 2 // 128

    def body(h1_ref, h2_ref, w1_ref, w2_ref, b1_ref, b2_ref, o_ref, pad):
        _load_pad(pad, h1_ref[...], t)
        a1 = _conv_taps(pad, w1_ref[...], FFN_CONV, t) + b1_ref[...]
        _load_pad(pad, h2_ref[...], t)
        a2 = _conv_taps(pad, w2_ref[...], FFN_CONV, t) + b2_ref[...]
        o_ref[...] = (jax.nn.silu(a1) * a2).astype(bf16)

    col = lambda off: pl.BlockSpec((t, 128), lambda j: (0, j + off))
    wsp = lambda off: pl.BlockSpec((FFN_CONV, 128), lambda j: (0, j + off))
    bsp = lambda off: pl.BlockSpec((1, 128), lambda j: (0, j + off))
    return _call(body, "glu_fwd", (nf,), [col(0), col(nf), wsp(0), wsp(nf), bsp(0), bsp(nf)], col(0), _sds((t, f2 // 2), bf16),
                 [pltpu.VMEM((t + PAD, 128), f32)])(hup, hup, conv_w, conv_w, bias, bias)


def glu_bwd(hup, conv_w, bias, dg, exchange=None):
    t, f2 = hup.shape
    f = f2 // 2
    nf = f // 128

    def body(h1_ref, h2_ref, w1_ref, w2_ref, b1_ref, b2_ref, dg_ref, dh1_ref, dh2_ref, dw1_ref, dw2_ref, db1_ref, db2_ref,
             pad1, pad2, dpad):
        _load_pad(pad1, h1_ref[...], t)
        _load_pad(pad2, h2_ref[...], t)
        w1, w2 = w1_ref[...], w2_ref[...]
        a1 = _conv_taps(pad1, w1, FFN_CONV, t) + b1_ref[...]
        a2 = _conv_taps(pad2, w2, FFN_CONV, t) + b2_ref[...]
        d = dg_ref[...]
        _, vjp = jax.vjp(jax.nn.silu, a1)
        da1, = vjp(d * a2)
        da2 = d * jax.nn.silu(a1)
        for da, pad, w, dh_ref, dw_ref, db_ref in ((da1, pad1, w1, dh1_ref, dw1_ref, db1_ref), (da2, pad2, w2, dh2_ref, dw2_ref, db2_ref)):
            dx, dws = _conv_bwd(pad, dpad, w, da, FFN_CONV, t)
            dh_ref[...] = dx.astype(bf16)
            for j in range(FFN_CONV):
                dw_ref[j:j + 1, :] = dws[j]
            db_ref[...] = jnp.sum(da, axis=0, keepdims=True)

    col = lambda off: pl.BlockSpec((t, 128), lambda j: (0, j + off))
    wsp = lambda off: pl.BlockSpec((FFN_CONV, 128), lambda j: (0, j + off))
    bsp = lambda off: pl.BlockSpec((1, 128), lambda j: (0, j + off))
    return _call(body, "glu_bwd", (nf,), [col(0), col(nf), wsp(0), wsp(nf), bsp(0), bsp(nf), col(0)],
                 [col(0), col(0), wsp(0), wsp(0), bsp(0), bsp(0)],
                 [_sds((t, f), bf16)] * 2 + [_sds((FFN_CONV, f), f32)] * 2 + [_sds((1, f), f32)] * 2,
                 [pltpu.VMEM((t + PAD, 128), f32)] * 3, exchange=exchange)(hup, hup, conv_w, conv_w, bias, bias, dg)


def gate_act(z_tail, b_tile):
    t = z_tail.shape[0]
    tm = _tile(t, (512, 256, 128, 64))

    def body(z_ref, b_ref, o_ref):
        x = z_ref[...] + b_ref[...]
        lane = lax.broadcasted_iota(jnp.int32, x.shape, 1)
        o_ref[...] = jnp.where(lane < T_F, x, jax.nn.log_sigmoid(x))

    row = pl.BlockSpec((tm, TAIL), lambda i: (i, 0))
    return _call(body, "gate_act", (t // tm,), [row, pl.BlockSpec((1, TAIL), lambda i: (0, 0))], row, _sds((t, TAIL), f32))(z_tail, b_tile)


def tail_bwd(z_tail, b_tile, dzt_pe, dgate):
    t = z_tail.shape[0]
    tm = _tile(t, (512, 256, 128, 64))

    def body(z_ref, b_ref, dpe_ref, dg_ref, dz_ref, db_ref):
        x = z_ref[...] + b_ref[...]
        lane = lax.broadcasted_iota(jnp.int32, x.shape, 1)
        _, vjp = jax.vjp(jax.nn.log_sigmoid, x)
        df, = vjp(dg_ref[...])
        dgates = jnp.where(lane < T_F, dg_ref[...], df)
        dgates = jnp.where(jnp.logical_and(lane >= T_I, lane < T_F + ML_HEADS), dgates, 0.0)
        dz_ref[...] = jnp.where(lane < ROPE, dpe_ref[...], dgates).astype(bf16)
        _acc_row(db_ref, jnp.sum(dgates, axis=0, keepdims=True), pl.program_id(0) == 0)

    row = pl.BlockSpec((tm, TAIL), lambda i: (i, 0))
    return _call(body, "tail_bwd", (t // tm,), [row, pl.BlockSpec((1, TAIL), lambda i: (0, 0)), row, row],
                 [row, pl.BlockSpec((8, TAIL), lambda i: (0, 0))], [_sds((t, TAIL), bf16), _sds((8, TAIL), f32)])(z_tail, b_tile, dzt_pe, dgate)


def _hdot(a, b, ca, cb):
    return lax.dot_general(a.astype(bf16), b.astype(bf16), (((ca,), (cb,)), ((0,), (0,))), preferred_element_type=f32)


def _mlstm_step(q, k, v, igr, fgr, c_mat, n_vec, m):
    nh, ln = q.shape[0], CHUNK
    sq = (nh, ln, ln)
    row = lax.broadcasted_iota(jnp.int32, sq, 1)
    col = lax.broadcasted_iota(jnp.int32, sq, 2)
    eye = row == col

    def to_col(r):
        return jnp.sum(jnp.where(eye, jnp.broadcast_to(r, sq), 0.0), axis=2, keepdims=True)

    bc_r = jnp.sum(jnp.where(row <= col, jnp.broadcast_to(to_col(fgr), sq), 0.0), axis=1, keepdims=True)
    bc_c = to_col(bc_r)
    logw = jnp.where(col <= row, bc_c - bc_r + igr, -jnp.inf)
    inter = bc_c + m
    m_t = jnp.maximum(inter, jnp.max(logw, axis=2, keepdims=True))
    w_intra = jnp.exp(logw - m_t)
    w_inter = jnp.exp(inter - m_t)
    sc = _hdot(q, k, 2, 2) * w_intra
    num = w_inter * _hdot(q, c_mat, 2, 1) + _hdot(sc, v, 2, 1)
    qn = jnp.sum(q.astype(bf16).astype(f32) * n_vec.astype(bf16).astype(f32), axis=2, keepdims=True)
    den = w_inter * qn + jnp.sum(sc, axis=2, keepdims=True)
    h = num / jnp.maximum(jnp.abs(den), jnp.exp(-m_t))
    lane = lax.broadcasted_iota(jnp.int32, (nh, 1, ln), 2)
    b_last = jnp.sum(jnp.where(lane == ln - 1, bc_r, 0.0), axis=2, keepdims=True)
    logu = b_last - bc_r + igr
    m_new = jnp.maximum(b_last + m, jnp.max(logu, axis=2, keepdims=True))
    decay = jnp.exp(b_last + m - m_new)
    u_c = to_col(jnp.exp(logu - m_new))
    c_new = decay * c_mat + _hdot(u_c * k, v, 1, 1)
    n_new = decay * n_vec + jnp.sum(u_c.astype(bf16).astype(f32) * k.astype(bf16).astype(f32), axis=1, keepdims=True)
    return h, c_new, n_new, m_new


ML_VHALF = ML_V // 2
assert O_V % ML_VHALF == 0 and ML_HEADS % 2 == 0


def _ml_specs(nc, rev):
    cc = (lambda c: nc - 1 - c) if rev else (lambda c: c)
    q = pl.BlockSpec((ML_HEADS, CHUNK, ML_DK), lambda c: (0, cc(c), 0))
    k = pl.BlockSpec((ML_HEADS, CHUNK, ML_DK), lambda c: (1, cc(c), 0))
    v_lo = pl.BlockSpec((CHUNK, ML_VHALF), lambda c: (cc(c), O_V // ML_VHALF))
    v_hi = pl.BlockSpec((CHUNK, ML_VHALF), lambda c: (cc(c), O_V // ML_VHALF + 1))
    hv = pl.BlockSpec((ML_HEADS, CHUNK, ML_DV), lambda c: (0, cc(c), 0))
    gate = pl.BlockSpec((ML_HEADS, 1, 1, CHUNK), lambda c: (0, cc(c), 0, 0))
    cm = pl.BlockSpec((ML_HEADS, 1, ML_DK, ML_DV), lambda c: (0, cc(c), 0, 0))
    nv = pl.BlockSpec((ML_HEADS, 1, 1, ML_DK), lambda c: (0, cc(c), 0, 0))
    ms = pl.BlockSpec((ML_HEADS, 1, 1, 1), lambda c: (0, cc(c), 0, 0))
    return q, k, v_lo, v_hi, hv, gate, cm, nv, ms


_ML_STATE = [pltpu.VMEM((ML_HEADS, ML_DK, ML_DV), f32), pltpu.VMEM((ML_HEADS, 1, ML_DK), f32), pltpu.VMEM((ML_HEADS, 1, 1), f32)]


def _ml_zero_state(c_s, n_s, m_s):
    @pl.when(pl.program_id(0) == 0)
    def _():
        c_s[...] = jnp.zeros_like(c_s)
        n_s[...] = jnp.zeros_like(n_s)
        m_s[...] = jnp.zeros_like(m_s)


def _ml_heads_of(v_lo_ref, v_hi_ref):
    half = ML_HEADS // 2
    return jnp.stack([r[:, j * ML_DV:(j + 1) * ML_DV] for r in (v_lo_ref, v_hi_ref) for j in range(half)])


def mlstm_fwd(qk_act, z_main, ig, fg):
    t = qk_act.shape[1]
    nc = t // CHUNK

    def body(q_ref, k_ref, vl_ref, vh_ref, ig_ref, fg_ref, h_ref, c_out, n_out, m_out, c_s, n_s, m_s):
        _ml_zero_state(c_s, n_s, m_s)
        c0, n0, m0 = c_s[...], n_s[...], m_s[...]
        c_out[:, 0] = c0
        n_out[:, 0] = n0
        m_out[:, 0] = m0
        h, c2, n2, m2 = _mlstm_step(q_ref[...], k_ref[...], _ml_heads_of(vl_ref, vh_ref), ig_ref[:, 0], fg_ref[:, 0], c0, n0, m0)
        h_ref[...] = h
        c_s[...] = c2
        n_s[...] = n2
        m_s[...] = m2

    q, k, v_lo, v_hi, hv, gate, cm, nv, ms = _ml_specs(nc, False)
    return _call(body, "mlstm_fwd", (nc,), [q, k, v_lo, v_hi, gate, gate], [hv, cm, nv, ms],
                 [_sds((ML_HEADS, t, ML_DV), f32), _sds((ML_HEADS, nc, ML_DK, ML_DV), f32), _sds((ML_HEADS, nc, 1, ML_DK), f32),
                  _sds((ML_HEADS, nc, 1, 1), f32)], _ML_STATE)(qk_act, qk_act, z_main, z_main, ig, fg)


def mlstm_bwd(qk_act, z_main, ig, fg, c_all, n_all, m_all, dh, exchange=None):
    t = qk_act.shape[1]
    nc = t // CHUNK

    def body(q_ref, k_ref, vl_ref, vh_ref, ig_ref, fg_ref, c_ref, n_ref, m_ref, dh_ref, dq_ref, dk_ref, dv_ref, dig_ref, dfg_ref,
             dc_s, dn_s, dm_s):
        _ml_zero_state(dc_s, dn_s, dm_s)
        _, vjp = jax.vjp(_mlstm_step, q_ref[...], k_ref[...], _ml_heads_of(vl_ref, vh_ref), ig_ref[:, 0], fg_ref[:, 0],
                         c_ref[:, 0], n_ref[:, 0], m_ref[:, 0])
        dq, dk, dv, dig, dfg, dc, dn, dm = vjp((dh_ref[...], dc_s[...], dn_s[...], dm_s[...]))
        dq_ref[...] = dq
        dk_ref[...] = dk
        for j in range(ML_HEADS):
            dv_ref[:, j * ML_DV:(j + 1) * ML_DV] = dv[j].astype(bf16)
        dig_ref[:, 0] = dig
        dfg_ref[:, 0] = dfg
        dc_s[...] = dc
        dn_s[...] = dn
        dm_s[...] = dm

    q, k, v_lo, v_hi, hv, gate, cm, nv, ms = _ml_specs(nc, True)
    gshape = _sds((ML_HEADS, nc, 1, CHUNK), f32)
    return _call(body, "mlstm_bwd", (nc,), [q, k, v_lo, v_hi, gate, gate, cm, nv, ms, hv],
                 [q, q, pl.BlockSpec((CHUNK, ML_V), lambda c: (nc - 1 - c, 0)), gate, gate],
                 [_sds((ML_HEADS, t, ML_DK), f32), _sds((ML_HEADS, t, ML_DK), f32), _sds((t, ML_V), bf16), gshape, gshape],
                 _ML_STATE, exchange=exchange)(qk_act, qk_act, z_main, z_main, ig, fg, c_all, n_all, m_all, dh)


def _ml_out(h, zo, g):
    return _rms(h, g) * jax.nn.sigmoid(zo)


def mlstm_out(h, z_main, g_hnorm):
    t = h.shape[1]
    tm = _tile(t, (512, 256, 128, 64))
    zo = O_O // ML_DV

    def body(h_ref, z_ref, g_ref, y_ref):
        y_ref[...] = _ml_out(h_ref[0], z_ref[...], g_ref[0]).astype(bf16)

    return _call(body, "mlstm_out", (t // tm, ML_HEADS),
                 [pl.BlockSpec((1, tm, ML_DV), lambda i, hd: (hd, i, 0)), pl.BlockSpec((tm, ML_DV), lambda i, hd: (i, zo + hd)),
                  pl.BlockSpec((1, 1, ML_DV), lambda i, hd: (hd, 0, 0))],
                 pl.BlockSpec((tm, ML_DV), lambda i, hd: (i, hd)), _sds((t, ML_V), bf16))(h, z_main, g_hnorm)


def mlstm_out_bwd(h, z_main, g_hnorm, dy):
    t = h.shape[1]
    tm = _tile(t, (512, 256, 128, 64))
    zo = O_O // ML_DV

    def body(h_ref, z_ref, g_ref, dy_ref, dh_ref, dzo_ref, dg_ref):
        _, vjp = jax.vjp(_ml_out, h_ref[0], z_ref[...], g_ref[0])
        dh, dz, dg = vjp(dy_ref[...])
        dh_ref[0] = dh
        dzo_ref[...] = dz.astype(bf16)

        @pl.when(pl.program_id(1) == 0)
        def _():
            dg_ref[...] = jnp.zeros_like(dg_ref)

        dg_ref[0, 0:1, :] += dg

    head = pl.BlockSpec((1, tm, ML_DV), lambda hd, i: (hd, i, 0))
    blk = pl.BlockSpec((tm, ML_DV), lambda hd, i: (i, hd))
    return _call(body, "mlstm_out_bwd", (ML_HEADS, t // tm),
                 [head, pl.BlockSpec((tm, ML_DV), lambda hd, i: (i, zo + hd)), pl.BlockSpec((1, 1, ML_DV), lambda hd, i: (hd, 0, 0)), blk],
                 [head, blk, pl.BlockSpec((1, 8, ML_DV), lambda hd, i: (hd, 0, 0))],
                 [_sds((ML_HEADS, t, ML_DV), f32), _sds((t, ML_V), bf16), _sds((ML_HEADS, 8, ML_DV), f32)])(h, z_main, g_hnorm, dy)


def _merge(ga, gb, ya, yb):
    return jax.nn.sigmoid(ga) * ya + jax.nn.sigmoid(gb) * yb


def _merge_specs(t, d):
    tm = _tile(t, (512, 256, 128, 64))
    bw = _tile(d, (512, 256, 128))
    assert O_GA % bw == 0 and (O_GA + d) % bw == 0
    blk = pl.BlockSpec((tm, bw), lambda i, j: (i, j))
    ga = pl.BlockSpec((tm, bw), lambda i, j: (i, O_GA // bw + j))
    gb = pl.BlockSpec((tm, bw), lambda i, j: (i, (O_GA + d) // bw + j))
    return tm, bw, blk, ga, gb


def merge_fwd(z_main, ya, yb):
    t, d = ya.shape
    tm, bw, blk, ga, gb = _merge_specs(t, d)

    def body(ga_ref, gb_ref, ya_ref, yb_ref, o_ref):
        o_ref[...] = _merge(ga_ref[...], gb_ref[...], ya_ref[...], yb_ref[...]).astype(bf16)

    return _call(body, "merge_fwd", (t // tm, d // bw), [ga, gb, blk, blk], blk, _sds((t, d), bf16))(z_main, z_main, ya, yb)


def merge_bwd(z_main, ya, yb, dmerged):
    t, d = ya.shape
    tm, bw, blk, ga, gb = _merge_specs(t, d)

    def body(ga_ref, gb_ref, ya_ref, yb_ref, dm_ref, dga_ref, dgb_ref, dya_ref, dyb_ref):
        _, vjp = jax.vjp(_merge, ga_ref[...], gb_ref[...], ya_ref[...], yb_ref[...])
        dga, dgb, dya, dyb = vjp(dm_ref[...])
        dga_ref[...] = dga.astype(bf16)
        dgb_ref[...] = dgb.astype(bf16)
        dya_ref[...] = dya.astype(bf16)
        dyb_ref[...] = dyb.astype(bf16)

    return _call(body, "merge_bwd", (t // tm, d // bw), [ga, gb, blk, blk, blk], [blk] * 4, [_sds((t, d), bf16)] * 4)(
        z_main, z_main, ya, yb, dmerged)


def _cross(cq, ck, cv, gq, gk):
    outs = []
    for hd in range(CR_HEADS):
        sl = slice(hd * CR_HD, (hd + 1) * CR_HD)
        q = _rms(cq[:, sl], gq)
        k = _rms(ck[:, sl], gk)
        s = _bdot(q, k, 1, 1) * (CR_HD ** -0.5)
        p = jax.nn.softmax(s, axis=-1)
        outs.append(_bdot(p, cv[:, sl], 1, 0))
    return jnp.concatenate(outs, axis=1)


def cross_fwd(cq, ck, cv, gq, gk):
    t, w = cq.shape
    nm = ck.shape[0]
    tm = _tile(t, (512, 256, 128, 64))

    def body(q_ref, k_ref, v_ref, gq_ref, gk_ref, o_ref):
        o_ref[...] = _cross(q_ref[...], k_ref[...], v_ref[...], gq_ref[...], gk_ref[...]).astype(bf16)

    row = pl.BlockSpec((tm, w), lambda i: (i, 0))
    full = pl.BlockSpec((nm, w), lambda i: (0, 0))
    gain = pl.BlockSpec((1, CR_HD), lambda i: (0, 0))
    return _call(body, "cross_fwd", (t // tm,), [row, full, full, gain, gain], row, _sds((t, w), bf16))(cq, ck, cv, gq, gk)


def cross_bwd(cq, ck, cv, gq, gk, do):
    t, w = cq.shape
    nm = ck.shape[0]
    tm = _tile(t, (512, 256, 128, 64))

    def body(q_ref, k_ref, v_ref, gq_ref, gk_ref, do_ref, dq_ref, dk_ref, dv_ref, dgq_ref, dgk_ref):
        first = pl.program_id(0) == 0
        _, vjp = jax.vjp(_cross, q_ref[...], k_ref[...], v_ref[...], gq_ref[...], gk_ref[...])
        dq, dk, dv, dgq, dgk = vjp(do_ref[...])
        dq_ref[...] = dq.astype(bf16)

        @pl.when(first)
        def _():
            dk_ref[...] = jnp.zeros_like(dk_ref)
            dv_ref[...] = jnp.zeros_like(dv_ref)

        dk_ref[...] += dk
        dv_ref[...] += dv
        _acc_row(dgq_ref, dgq, first)
        _acc_row(dgk_ref, dgk, first)

    row = pl.BlockSpec((tm, w), lambda i: (i, 0))
    full = pl.BlockSpec((nm, w), lambda i: (0, 0))
    gain = pl.BlockSpec((1, CR_HD), lambda i: (0, 0))
    acc = pl.BlockSpec((8, CR_HD), lambda i: (0, 0))
    return _call(body, "cross_bwd", (t // tm,), [row, full, full, gain, gain, row], [row, full, full, acc, acc],
                 [_sds((t, w), bf16), _sds((nm, w), f32), _sds((nm, w), f32), _sds((8, CR_HD), f32), _sds((8, CR_HD), f32)])(
        cq, ck, cv, gq, gk, do)


def loss_head(x2, fo, target):
    t, d = x2.shape
    tm = _tile(t, (256, 128, 64, 32, 16, 8))

    def body(a_ref, b_ref, t_ref, dx_ref, dxb_ref, l_ref):
        err = a_ref[...] + b_ref[...] - t_ref[...]
        dx = err / d
        dx_ref[...] = dx
        dxb_ref[...] = dx.astype(bf16)
        part = 0.5 * jnp.sum(jnp.mean(err * err, axis=1, keepdims=True), axis=0, keepdims=True)
        _acc_row(l_ref, jnp.broadcast_to(part, (1, 128)), pl.program_id(0) == 0)

    row = pl.BlockSpec((tm, d), lambda i: (i, 0))
    return _call(body, "loss_head", (t // tm,), [row, row, row], [row, row, pl.BlockSpec((8, 128), lambda i: (0, 0))],
                 [_sds((t, d), f32), _sds((t, d), bf16), _sds((8, 128), f32)])(x2, fo, target)


def _place():
    x, y, c = lax.axis_index("x"), lax.axis_index("y"), lax.axis_index("c")
    peers = {}
    for r in range(1, N_DEV):
        px = 1 - x if r & 4 else x
        py = 1 - y if r & 2 else y
        pc = 1 - c if r & 1 else c
        peers[r] = ((px, py, pc), 4 * px + 2 * py + pc)
    return 4 * x + 2 * y + c, peers


N_REL = N_DEV - 1
RELATIONS = tuple(range(1, N_DEV))
SIBLING = 1
OTHER_CHIPS = (2, 4, 6)
PASSED_ON = (3, 5, 7)


def _exchange_ops(ins, outs, sems, scatter):
    n = len(ins)
    send_sems, recv_sems, local_sems = sems

    def tools():
        me, peers = _place()

        def copy(a, r, src, dst_idx, to):
            return pltpu.make_async_remote_copy(
                src_ref=src, dst_ref=outs[a].at[dst_idx], send_sem=send_sems.at[a * N_REL + r - 1],
                recv_sem=recv_sems.at[a * N_REL + r - 1], device_id=peers[to][0], device_id_type=MESH)

        def local(a):
            return pltpu.make_async_copy(ins[a].at[me] if scatter else ins[a], outs[a].at[me], local_sems.at[a])

        def arrival(a, r):
            return copy(a, r, ins[a].at[me] if scatter else ins[a], peers[r][1], r)

        return me, peers, copy, local, arrival

    if scatter:
        def sends():
            me, peers, copy, local, _ = tools()
            return [local(a) for a in range(n)], [copy(a, r, ins[a].at[peers[r][1]], me, r) for a in range(n) for r in RELATIONS]

        def start():
            loc, out = sends()
            for cp in loc + out:
                cp.start()

        middle = None
        waited_last = RELATIONS
    else:
        def sends():
            me, peers, copy, local, _ = tools()
            own = [copy(a, r, ins[a], me, r) for a in range(n) for r in (SIBLING,) + OTHER_CHIPS]
            return [local(a) for a in range(n)], own

        def passes():
            me, peers, copy, _, _ = tools()
            return [copy(a, r, outs[a].at[peers[r - 1][1]], peers[r - 1][1], SIBLING) for a in range(n) for r in PASSED_ON]

        def start():
            loc, out = sends()
            for cp in loc + out:
                cp.start()

        def middle():
            _, _, _, _, arrival = tools()
            fwd = passes()
            for a in range(n):
                for i, r in enumerate(PASSED_ON):
                    arrival(a, r - 1).wait_recv()
                    fwd[a * len(PASSED_ON) + i].start()

        waited_last = (SIBLING,) + PASSED_ON

    def wait():
        _, _, _, _, arrival = tools()
        for a in range(n):
            for r in waited_last:
                arrival(a, r).wait_recv()
        loc, out = sends()
        for cp in out + ([] if scatter else passes()):
            cp.wait_send()
        for cp in loc:
            cp.wait()

    return start, middle, wait


def _exchange_shapes(arrs, scatter):
    return [_sds(a.shape if scatter else (N_DEV,) + a.shape, a.dtype) for a in arrs]


def _exchange_sems(n):
    return [pltpu.SemaphoreType.DMA((n * N_REL,)), pltpu.SemaphoreType.DMA((n * N_REL,)), pltpu.SemaphoreType.DMA((n,))]


def _exchange(arrs, name, scatter):
    n = len(arrs)

    def body(*refs):
        start, middle, wait = _exchange_ops(refs[:n], refs[n:2 * n], refs[2 * n:], scatter)
        start()
        if middle is not None:
            middle()
        wait()

    any_spec = pl.BlockSpec(memory_space=pl.ANY)
    return pl.pallas_call(body, name=name, in_specs=[any_spec] * n, out_specs=[any_spec] * n,
                          out_shape=_exchange_shapes(arrs, scatter), scratch_shapes=_exchange_sems(n))(*arrs)


def cast_bf16(w, name):
    _, r, c = w.shape
    tr = _tile(r, (256, 128, 64, 32, 16))

    def body(w_ref, o_ref):
        o_ref[...] = w_ref[0].astype(bf16)

    return _call(body, name, (r // tr,), [pl.BlockSpec((1, tr, c), lambda i: (0, i, 0))], pl.BlockSpec((tr, c), lambda i: (i, 0)),
                 _sds((r, c), bf16))(w)


def _adamw(w, g, m, v):
    m = ADAM_B1 * m + (1.0 - ADAM_B1) * g
    v = ADAM_B2 * v + (1.0 - ADAM_B2) * jnp.square(g)
    m_hat = m / (1.0 - ADAM_B1 ** ADAM_STEP)
    v_hat = v / (1.0 - ADAM_B2 ** ADAM_STEP)
    delta = -ADAM_LR * (m_hat / (jnp.sqrt(v_hat) + ADAM_EPS) + ADAM_WD * w)
    return delta, m, v


def adam_sum(parts, w, m, v, name):
    _, r, c = parts.shape
    budget = 4 * 1024 * 1024
    tr = r
    for cand in (1024, 512, 256, 128, 64, 32, 16):
        if r % cand == 0 and N_DEV * cand * c * 4 <= budget:
            tr = cand
            break

    def body(p_ref, w_ref, m_ref, v_ref, g_ref, d_ref, m2_ref, v2_ref):
        g = p_ref[0].astype(f32)
        for k in range(1, N_DEV):
            g = g + p_ref[k].astype(f32)
        d, m2, v2 = _adamw(w_ref[0], g, m_ref[0], v_ref[0])
        g_ref[...] = g
        d_ref[...] = d
        m2_ref[...] = m2
        v2_ref[...] = v2

    blk = pl.BlockSpec((1, tr, c), lambda i: (0, i, 0))
    out = pl.BlockSpec((tr, c), lambda i: (i, 0))
    return _call(body, name, (r // tr,), [pl.BlockSpec((N_DEV, tr, c), lambda i: (0, i, 0)), blk, blk, blk], [out] * 4,
                 [_sds((r, c), f32)] * 4)(parts, w, m, v)


def sum_parts(parts, name):
    _, r, c = parts.shape

    def body(p_ref, o_ref):
        g = p_ref[0]
        for k in range(1, N_DEV):
            g = g + p_ref[k]
        o_ref[...] = g

    return pl.pallas_call(body, name=name, out_shape=_sds((r, c), f32))(parts)


def adam_flat(w, g, m, v, name):
    def body(w_ref, g_ref, m_ref, v_ref, d_ref, m2_ref, v2_ref):
        d, m2, v2 = _adamw(w_ref[...], g_ref[...], m_ref[...], v_ref[...])
        d_ref[...] = d
        m2_ref[...] = m2
        v2_ref[...] = v2

    return pl.pallas_call(body, name=name, out_shape=[_sds(w.shape, f32)] * 3)(w, g, m, v)


def _pack(vecs, multiple):
    flat = jnp.concatenate([v.reshape(-1) for v in vecs])
    n = flat.shape[0]
    total = -(-n // multiple) * multiple
    return jnp.pad(flat, (0, total - n))


def _unpack(flat, shapes):
    out, pos = [], 0
    for s in shapes:
        n = 1
        for d in s:
            n *= d
        out.append(flat[pos:pos + n].reshape(s))
        pos += n
    return out


def _pad_lanes(v, width=TAIL):
    return jnp.pad(v, ((0, 0), (0, width - v.shape[1])))


def kernel(x, mem, positions, g_mix, w_in, g_qa, w_qb, g_kva, w_kvb, g_qn_nope, g_qn_pe, g_kn_nope, g_kn_pe, conv_qk, b_if, g_hnorm, p_a, p_b, w_out, g_cross, g_mem, wq_c, wk_c, wv_c, g_cq, g_ck, wo_c, g_ffn, w_up, conv_ffn, b_conv_ffn, w_down, loss_target, m_g_mix, m_w_in, m_g_qa, m_w_qb, m_g_kva, m_w_kvb, m_g_qn_nope, m_g_qn_pe, m_g_kn_nope, m_g_kn_pe, m_conv_qk, m_b_if, m_g_hnorm, m_p_a, m_p_b, m_w_out, m_g_cross, m_g_mem, m_wq_c, m_wk_c, m_wv_c, m_g_cq, m_g_ck, m_wo_c, m_g_ffn, m_w_up, m_conv_ffn, m_b_conv_ffn, m_w_down, v_g_mix, v_w_in, v_g_qa, v_w_qb, v_g_kva, v_w_kvb, v_g_qn_nope, v_g_qn_pe, v_g_kn_nope, v_g_kn_pe, v_conv_qk, v_b_if, v_g_hnorm, v_p_a, v_p_b, v_w_out, v_g_cross, v_g_mem, v_wq_c, v_wk_c, v_wv_c, v_g_cq, v_g_ck, v_wo_c, v_g_ffn, v_w_up, v_conv_ffn, v_b_conv_ffn, v_w_down):
    args = dict(locals())
    names = ['g_mix', 'w_in', 'g_qa', 'w_qb', 'g_kva', 'w_kvb', 'g_qn_nope', 'g_qn_pe', 'g_kn_nope', 'g_kn_pe', 'conv_qk', 'b_if',
             'g_hnorm', 'p_a', 'p_b', 'w_out', 'g_cross', 'g_mem', 'wq_c', 'wk_c', 'wv_c', 'g_cq', 'g_ck', 'wo_c', 'g_ffn', 'w_up',
             'conv_ffn', 'b_conv_ffn', 'w_down']
    big = ['w_in', 'w_qb', 'w_kvb', 'p_a', 'p_b', 'w_out', 'wq_c', 'wk_c', 'wv_c', 'wo_c', 'w_up', 'w_down']
    sharded_small = ['conv_qk', 'g_hnorm', 'conv_ffn']
    replicated = [n for n in names if n not in big and n not in sharded_small]

    t, d = x.shape[1], x.shape[2]
    x2d, tgt = x[0], loss_target[0]
    mem2d = mem[0]
    me = 4 * lax.axis_index("x") + 2 * lax.axis_index("y") + lax.axis_index("c")
    nc = t // CHUNK
    f2 = b_conv_ffn.shape[1]
    wmain = O_GA + 2 * d

    first = ['w_in', 'w_qb', 'w_kvb']
    behind_in = ['p_a', 'p_b', 'w_out', 'wq_c', 'wk_c', 'wv_c', 'wo_c']
    shards = {n: cast_bf16(args[n], "cast_" + n) for n in big}
    small_local = _pack([args[n] for n in sharded_small], 128).reshape(1, -1)
    gathered = _exchange([shards[n] for n in first] + [small_local], "comm_gather_first", scatter=False)
    gw = dict(zip(first, gathered[:-1]))
    small_all = gathered[-1]
    full_small, pos = [], 0
    for n in sharded_small:
        _, rows, cols = args[n].shape
        piece = small_all[:, 0, pos:pos + rows * cols].reshape(N_DEV, rows, cols)
        full_small.append(piece.transpose(1, 0, 2).reshape(rows, N_DEV * cols))
        pos += rows * cols
    conv_qk_f, g_hnorm_f, conv_ffn_f = full_small

    w_in_f = gw['w_in'].transpose(1, 0, 2).reshape(d, -1)
    c_kpe, c_q, c_i, c_o = O_Q, O_Q + ROPE, O_Q + ROPE + 2 * ML_QK + ML_V, O_Q + ROPE + 2 * ML_QK + ML_V + 2 * ML_HEADS
    w_main = jnp.concatenate([w_in_f[:, :c_kpe], w_in_f[:, c_q:c_i], w_in_f[:, c_o:]], axis=1)[None]
    w_tail = jnp.concatenate([w_in_f[:, c_kpe:c_q], w_in_f[:, c_i:c_o],
                              jnp.zeros((d, TAIL - ROPE - 2 * ML_HEADS), bf16)], axis=1)[None]
    assert w_main.shape[2] == wmain

    inv_freq = ROPE_BASE ** (-jnp.arange(0, ROPE, 2, dtype=f32) / ROPE)
    inv_tile = _pad_lanes(jnp.concatenate([inv_freq, inv_freq])[None])
    cos, sin = rope_tables(positions.reshape(t, 1), inv_tile)
    gqp, gkp = _pad_lanes(g_qn_pe), _pad_lanes(g_kn_pe)
    b_tile = jnp.pad(b_if, ((0, 0), (T_I, TAIL - T_I - 2 * ML_HEADS)))

    u0 = rms_fwd(x2d, g_mix, "rms_mix")
    z_main, got = mm_nn(u0, w_main, f32, "mm_in_main", exchange=([shards[n] for n in behind_in], False))
    gw.update(zip(behind_in, got))
    qb = gw['w_qb'].transpose(1, 0, 2).reshape(Q_LORA, MLA_HEADS, NOPE + ROPE)
    w_qb_p = jnp.concatenate([qb, jnp.zeros((Q_LORA, MLA_HEADS, HEAD_PAD - NOPE - ROPE), bf16)], axis=2).reshape(1, Q_LORA, -1)
    w_kvb3 = gw['w_kvb']
    p_a3, p_b3, w_out3 = (gw[n].reshape(1, -1, d) for n in ('p_a', 'p_b', 'w_out'))
    wq_c3, wk_c3, wv_c3 = (gw[n].reshape(1, d, -1) for n in ('wq_c', 'wk_c', 'wv_c'))
    wo_c3 = gw['wo_c']
    z_tail = mm_nn(u0, w_tail, f32, "mm_in_tail")
    qa_n, kv_n = lat_norm(z_main, g_qa, g_kva)
    q_raw = mm_nn(qa_n, w_qb_p, f32, "mm_qb")
    kv_raw = mm_nn(kv_n, w_kvb3, f32, "mm_kvb")
    qh, kh, vh = mla_prep(q_raw, kv_raw, z_tail, cos, sin, g_qn_nope, gqp, g_kn_nope, gkp)
    (o_a, o_ab, lse), (w_up3,) = mla_fwd(qh, kh, vh, exchange=([shards['w_up']], False))

    qk_act = qk_conv(z_main, conv_qk_f)
    gates = gate_act(z_tail, b_tile)

    def to_rows(cols):
        return cols.T.reshape(ML_HEADS, nc, 1, CHUNK)

    ig, fg = to_rows(gates[:, T_I:T_F]), to_rows(gates[:, T_F:T_F + ML_HEADS])
    h_ml, c_all, n_all, m_all = mlstm_fwd(qk_act, z_main, ig, fg)
    g_hn3 = g_hnorm_f.reshape(ML_HEADS, 1, ML_DV)
    y_b = mlstm_out(h_ml, z_main, g_hn3)

    ya = mm_nn(o_ab, p_a3, f32, "mm_pa")
    yb = mm_nn(y_b, p_b3, f32, "mm_pb")
    merged = merge_fwd(z_main, ya, yb)
    mo = mm_nn(merged, w_out3, f32, "mm_out")
    x1, uc = resid_rms(x2d, mo, g_cross, "resid_cross")
    mem_n = rms_fwd(mem2d, g_mem, "rms_mem")
    cq = mm_nn(uc, wq_c3, f32, "mm_cq")
    ck = mm_nn(mem_n, wk_c3, f32, "mm_ck")
    cv = mm_nn(mem_n, wv_c3, f32, "mm_cv")
    o_c = cross_fwd(cq, ck, cv, g_cq, g_ck)
    co = mm_nn(o_c, wo_c3, f32, "mm_oc")
    x2, u3 = resid_rms(x1, co, g_ffn, "resid_ffn")
    hup, (w_down_g,) = mm_nn(u3, w_up3, f32, "mm_up", exchange=([shards['w_down']], False))
    w_down3 = w_down_g.reshape(1, -1, d)
    gl = glu_fwd(hup, conv_ffn_f, b_conv_ffn)
    fo = mm_nn(gl, w_down3, f32, "mm_down")
    dx3, dx3_b, loss_acc = loss_head(x2, fo, tgt)

    grads, parts = {}, {}
    grads['w_down'] = mm_tn(gl, dx3_b, 1, "mm_d_wdown").reshape(N_DEV, -1, d)
    dgl = mm_nt(dx3_b, w_down3, f32, "mm_d_gl")
    (dh1, dh2, dcw1, dcw2, db1, db2), (parts['w_down'],) = glu_bwd(hup, conv_ffn_f, b_conv_ffn, dgl,
                                                                    exchange=([grads['w_down']], True))
    dconv_ffn, db_ffn = (jnp.concatenate(pair, axis=1) for pair in ((dcw1, dcw2), (db1, db2)))
    grads['w_up'] = mm_tn_cols(u3, [dh1, dh2], N_DEV, "mm_d_wup")
    du3 = mm_nt_cols([dh1, dh2], w_up3, f32, "mm_d_u3")
    dx2, dx2_b, dg_ffn = rms_bwd(x2, g_ffn, [du3], dx3, "rms_bwd_ffn", want_b16=True)
    grads['wo_c'] = mm_tn(o_c, dx2_b, N_DEV, "mm_d_woc")
    do_c = mm_nt(dx2_b, wo_c3, f32, "mm_d_oc")
    dcq, dck, dcv, dg_cq, dg_ck = cross_bwd(cq, ck, cv, g_cq, g_ck, do_c)
    grads['wq_c'] = mm_tn(uc, dcq, 1, "mm_d_wqc").reshape(N_DEV, -1, dcq.shape[1])
    grads['wk_c'] = mm_tn(mem_n, dck, 1, "mm_d_wkc").reshape(N_DEV, -1, dck.shape[1])
    grads['wv_c'] = mm_tn(mem_n, dcv, 1, "mm_d_wvc").reshape(N_DEV, -1, dcv.shape[1])
    duc = mm_nt(dcq, wq_c3, f32, "mm_d_uc")
    dmem_k = mm_nt(dck, wk_c3, f32, "mm_d_memk")
    dmem_v = mm_nt(dcv, wv_c3, f32, "mm_d_memv")
    dg_mem, = rms_bwd(mem2d, g_mem, [dmem_k, dmem_v], None, "rms_bwd_mem", want_dx=False)
    dx1, dx1_b, dg_cross = rms_bwd(x1, g_cross, [duc], dx2, "rms_bwd_cross", want_b16=True)
    grads['w_out'] = mm_tn(merged, dx1_b, 1, "mm_d_wout").reshape(N_DEV, -1, d)
    dmerged = mm_nt(dx1_b, w_out3, f32, "mm_d_merged")
    dga, dgb, dya, dyb = merge_bwd(z_main, ya, yb, dmerged)
    grads['p_a'] = mm_tn(o_ab, dya, 1, "mm_d_pa").reshape(N_DEV, -1, d)
    grads['p_b'] = mm_tn(y_b, dyb, 1, "mm_d_pb").reshape(N_DEV, -1, d)
    do_a = mm_nt(dya, p_a3, f32, "mm_d_oa")
    dy_b = mm_nt(dyb, p_b3, f32, "mm_d_yb")

    dh_ml, dzo, dg_hn = mlstm_out_bwd(h_ml, z_main, g_hn3, dy_b)
    mixers = ['p_a', 'p_b', 'w_out']
    (dq_act, dk_act, dzv, dig, dfg), got = mlstm_bwd(qk_act, z_main, ig, fg, c_all, n_all, m_all, dh_ml,
                                                     exchange=([grads[n] for n in mixers], True))
    parts.update(zip(mixers, got))
    dzqk, dconv_qk = qk_conv_bwd(z_main, conv_qk_f, dq_act, dk_act)

    delta = mla_delta(o_a, do_a)
    (dqh, dkh, dvh), (parts['w_up'],) = mla_bwd(qh, kh, vh, do_a, lse, delta.reshape(MLA_HEADS, 1, t),
                                                exchange=([grads['w_up']], True))
    dq_raw, dkv_raw, dzt_pe, dg_qn, dg_qp, dg_kn, dg_kp = mla_prep_bwd(
        q_raw, kv_raw, z_tail, cos, sin, g_qn_nope, gqp, g_kn_nope, gkp, dqh, dkh, dvh)
    d_wqb_p = mm_tn(qa_n, dq_raw, 1, "mm_d_wqb")[0].reshape(Q_LORA, MLA_HEADS, HEAD_PAD)[:, :, :NOPE + ROPE]
    grads['w_qb'] = d_wqb_p.reshape(Q_LORA, N_DEV, -1).transpose(1, 0, 2)
    grads['w_kvb'] = mm_tn(kv_n, dkv_raw, N_DEV, "mm_d_wkvb")
    dqa = mm_nt(dq_raw, w_qb_p, f32, "mm_d_qa")
    dkvn = mm_nt(dkv_raw, w_kvb3, f32, "mm_d_kvn")
    dz_lat, dg_qa, dg_kva = lat_norm_bwd(z_main, g_qa, g_kva, dqa, dkvn)

    def to_cols(rows):
        return rows.reshape(ML_HEADS, t).T

    dgate = jnp.pad(jnp.concatenate([to_cols(dig), to_cols(dfg)], axis=1), ((0, 0), (T_I, TAIL - T_I - 2 * ML_HEADS)))
    dz_tail, db_if = tail_bwd(z_tail, b_tile, dzt_pe, dgate)
    dz_main = [dz_lat, dzqk, dzv, dzo, dga, dgb]
    small_mats = ['wq_c', 'wk_c', 'wv_c', 'wo_c', 'w_qb', 'w_kvb']
    d_wmain3, got = mm_tn_cols(u0, dz_main, 1, "mm_d_wmain", exchange=([grads[n] for n in small_mats], True))
    parts.update(zip(small_mats, got))
    d_wmain = d_wmain3[0]
    d_wtail = mm_tn(u0, dz_tail, 1, "mm_d_wtail")[0]
    d_win = jnp.concatenate([d_wmain[:, :O_Q], d_wtail[:, :ROPE], d_wmain[:, O_Q:O_O], d_wtail[:, T_I:T_I + 2 * ML_HEADS],
                             d_wmain[:, O_O:]], axis=1)
    grads['w_in'] = d_win.reshape(d, N_DEV, -1).transpose(1, 0, 2)
    du0_a, (parts['w_in'],) = mm_nt_cols(dz_main, w_main, f32, "mm_d_u0_main", exchange=([grads['w_in']], True))
    du0_b = mm_nt(dz_tail, w_tail, f32, "mm_d_u0_tail")
    grad_x, dg_mix = rms_bwd(x2d, g_mix, [du0_a, du0_b], dx1, "rms_bwd_mix")

    out_g, out_d, out_m, out_v = {}, {}, {}, {}
    for n in big:
        res = adam_sum(parts[n], args[n], args['m_' + n], args['v_' + n], "adam_" + n)
        out_g[n], out_d[n], out_m[n], out_v[n] = (a.reshape(args[n].shape) for a in res)

    small_full = {
        'g_mix': dg_mix[0], 'g_qa': dg_qa[0], 'g_kva': dg_kva[0], 'g_qn_nope': dg_qn[0], 'g_qn_pe': dg_qp[0, :ROPE],
        'g_kn_nope': dg_kn[0], 'g_kn_pe': dg_kp[0, :ROPE], 'conv_qk': dconv_qk, 'b_if': db_if[0, T_I:T_I + 2 * ML_HEADS],
        'g_hnorm': dg_hn[:, 0, :], 'g_cross': dg_cross[0], 'g_mem': dg_mem[0], 'g_cq': dg_cq[0], 'g_ck': dg_ck[0],
        'g_ffn': dg_ffn[0], 'conv_ffn': dconv_ffn, 'b_conv_ffn': db_ffn[0], 'loss': loss_acc[0, :1]}
    order = list(small_full)
    packed = _pack([small_full[n] for n in order], 8 * 128).reshape(1, -1)
    gathered_small, = _exchange([packed], "comm_gather_small", scatter=False)
    summed = sum_parts(gathered_small.reshape(N_DEV, -1, 128), "sum_small").reshape(-1)
    full_g = dict(zip(order, _unpack(summed, [small_full[n].shape for n in order])))
    loss = full_g['loss'][0]

    local_g = {}
    for n in replicated:
        local_g[n] = full_g[n].reshape(args[n].shape)
    for n in sharded_small:
        shp = args[n].shape
        full = full_g[n].reshape((1,) + full_g[n].shape)
        local_g[n] = lax.dynamic_slice_in_dim(full, me * shp[-1], shp[-1], axis=2)
    small = replicated + sharded_small
    dl_f, m_f, v_f = adam_flat(*[_pack([src[n] if pre == '' else args[pre + n] for n in small], 8 * 128).reshape(-1, 128)
                                 for pre, src in (('', args), ('', local_g), ('m_', None), ('v_', None))], "adam_small")
    shapes = [args[n].shape for n in small]
    for dst, flat in ((out_d, dl_f), (out_m, m_f), (out_v, v_f)):
        dst.update(zip(small, _unpack(flat.reshape(-1), shapes)))
    out_g.update(local_g)

    return (loss, grad_x[None], *[out_g[n] for n in names], *[out_d[n] for n in names],
            *[out_m[n] for n in names], *[out_v[n] for n in names])
```

```python
import functools

import jax
import jax.numpy as jnp
from jax import lax
from jax.experimental import pallas as pl
from jax.experimental.pallas import tpu as pltpu

f32 = jnp.float32
bf16 = jnp.bfloat16

N_DEV = 8
EPS = 1e-6
CHUNK = 64
CHUNK_SHIFT = 6
assert 1 << CHUNK_SHIFT == CHUNK
MLA_HEADS = 16
Q_LORA = 512
KV_LORA = 512
NOPE = 128
ROPE = 64
V_HEAD = 128
ROPE_BASE = 10000.0
HEAD_PAD = 256
ML_HEADS = 8
ML_DK = 128
ML_DV = 256
ML_CONV = 4
ML_QK = ML_HEADS * ML_DK
ML_V = ML_HEADS * ML_DV
CR_HEADS = 4
CR_HD = 128
FFN_CONV = 3
ADAM_LR = 0.001
ADAM_B1 = 0.9
ADAM_B2 = 0.999
ADAM_EPS = 1e-08
ADAM_WD = 0.01
ADAM_STEP = 10
O_QA, O_KV, O_Q, O_K = 0, Q_LORA, Q_LORA + KV_LORA, Q_LORA + KV_LORA + ML_QK
O_V = O_K + ML_QK
O_O = O_V + ML_V
O_GA = O_O + ML_V
TAIL = 128
T_I, T_F = ROPE, ROPE + ML_HEADS
VMEM_LIMIT_V7X = 48 * 1024 * 1024
MESH = pl.DeviceIdType.MESH


def _call(body, name, grid, in_specs, out_specs, out_shape, scratch=(), exchange=None):
    params = pltpu.CompilerParams(vmem_limit_bytes=VMEM_LIMIT_V7X)
    if exchange is None:
        return pl.pallas_call(body, name=name, grid=grid, in_specs=in_specs, out_specs=out_specs, out_shape=out_shape,
                              scratch_shapes=list(scratch), compiler_params=params)
    arrs, scatter = exchange
    single = not isinstance(out_specs, (list, tuple))
    o_specs = [out_specs] if single else list(out_specs)
    o_shape = [out_shape] if single else list(out_shape)
    n_in, n_out, n_sc, n = len(in_specs), len(o_specs), len(scratch), len(arrs)
    any_spec = pl.BlockSpec(memory_space=pl.ANY)

    def body_with_exchange(*refs):
        pos = [0]

        def take(k):
            pos[0] += k
            return refs[pos[0] - k:pos[0]]

        ins, ex_in, outs, ex_out, sc = take(n_in), take(n), take(n_out), take(n), take(n_sc)
        start, middle, wait = _exchange_ops(ex_in, ex_out, refs[pos[0]:], scatter)
        step, total = 0, 1
        for a in range(len(grid)):
            step = step * grid[a] + pl.program_id(a)
            total *= grid[a]
        pl.when(step == 0)(start)
        body(*ins, *outs, *sc)
        if middle is not None:
            pl.when(step == total // 2)(middle)
        pl.when(step == total - 1)(wait)

    call = pl.pallas_call(body_with_exchange, name="comm_" + name, grid=grid, in_specs=list(in_specs) + [any_spec] * n,
                          out_specs=o_specs + [any_spec] * n, out_shape=o_shape + _exchange_shapes(arrs, scatter),
                          scratch_shapes=list(scratch) + _exchange_sems(n), compiler_params=params)

    def run(*operands):
        res = call(*operands, *arrs)
        return (res[0] if single else list(res[:n_out])), list(res[n_out:])

    return run


def _tile(n, cands):
    for c in cands:
        if n % c == 0:
            return c
    return n


def _sds(shape, dtype):
    return jax.ShapeDtypeStruct(tuple(shape), dtype)


def _bdot(a, b, ca, cb):
    return lax.dot_general(a.astype(bf16), b.astype(bf16), (((ca,), (cb,)), ((), ())), preferred_element_type=f32)


_BIG = (1024, 512, 256, 128)


def _col_tile(nb):
    return nb if nb <= 1536 else _tile(nb, _BIG)


_DEEP = (2048, 1024, 512, 256, 128)


def _mm_call(name, grid, in_specs, out_spec, out_shape, tile, nk, ca, cb, exchange, operands):
    def dot(a_ref, w_ref):
        return _bdot(a_ref[...], w_ref[0] if len(w_ref.shape) == 3 else w_ref[...], ca, cb)

    def store(o_ref, val):
        if len(o_ref.shape) == 3:
            o_ref[0] = val.astype(o_ref.dtype)
        else:
            o_ref[...] = val.astype(o_ref.dtype)

    if nk == 1:
        def body(a_ref, w_ref, o_ref):
            store(o_ref, dot(a_ref, w_ref))

        scratch = []
    else:
        def body(a_ref, w_ref, o_ref, acc):
            kk = pl.program_id(2)

            @pl.when(kk == 0)
            def _():
                acc[...] = jnp.zeros_like(acc)

            acc[...] += dot(a_ref, w_ref)

            @pl.when(kk == nk - 1)
            def _():
                store(o_ref, acc[...])

        scratch = [pltpu.VMEM(tile, f32)]
    return _call(body, name, grid, in_specs, out_spec, out_shape, scratch, exchange=exchange)(*operands)


def mm_nn(a, w3, out_dtype, name, exchange=None):
    m, k = a.shape
    nblk, k2, nb = w3.shape
    assert k == k2
    tm, tk, tn = _tile(m, _BIG), _tile(k, _DEEP), _col_tile(nb)
    per, nk = nb // tn, k // tk
    return _mm_call(name, (m // tm, nblk * per, nk),
                    [pl.BlockSpec((tm, tk), lambda i, j, kk: (i, kk)),
                     pl.BlockSpec((1, tk, tn), lambda i, j, kk: (j // per, kk, j % per))],
                    pl.BlockSpec((tm, tn), lambda i, j, kk: (i, j)), _sds((m, nblk * nb), out_dtype),
                    (tm, tn), nk, 1, 0, exchange, (a, w3))


def mm_nt(a, w3, out_dtype, name, exchange=None):
    m, n = a.shape
    nblk, k, nb = w3.shape
    assert n == nblk * nb
    tm, tn = _tile(m, _BIG), _tile(k, _BIG)
    tc = nb if nb <= 1536 else _tile(nb, _DEEP)
    per = nb // tc
    nk = nblk * per
    return _mm_call(name, (m // tm, k // tn, nk),
                    [pl.BlockSpec((tm, tc), lambda i, j, kk: (i, kk)),
                     pl.BlockSpec((1, tn, tc), lambda i, j, kk: (kk // per, j, kk % per))],
                    pl.BlockSpec((tm, tn), lambda i, j, kk: (i, j)), _sds((m, k), out_dtype),
                    (tm, tn), nk, 1, 1, exchange, (a, w3))


def mm_tn(a, b, nblk, name, exchange=None):
    r, m = a.shape
    r2, n = b.shape
    assert r == r2 and n % nblk == 0
    nb = n // nblk
    tm, tk, tn = _tile(m, _BIG), _tile(r, _DEEP), _col_tile(nb)
    per, nk = nb // tn, r // tk
    return _mm_call(name, (m // tm, nblk * per, nk),
                    [pl.BlockSpec((tk, tm), lambda i, j, kk: (kk, i)),
                     pl.BlockSpec((tk, tn), lambda i, j, kk: (kk, j))],
                    pl.BlockSpec((1, tm, tn), lambda i, j, kk: (j // per, i, j % per)), _sds((nblk, m, nb), bf16),
                    (tm, tn), nk, 0, 0, exchange, (a, b))


def _section_tiles(sections, cands):
    widths = [s.shape[1] for s in sections]
    tile = next(c for c in cands if all(w % c == 0 for w in widths))
    counts = [w // tile for w in widths]
    firsts = [sum(counts[:i]) for i in range(len(counts))]
    return tile, firsts, counts


SECTION_VMEM_BYTES = 24 * 1024 * 1024


def mm_tn_cols(a, sections, nblk, name, exchange=None):
    r, m = a.shape
    n = sum(s.shape[1] for s in sections)
    nb = n // nblk
    tn, firsts, counts = _section_tiles(sections, (nb,) if nb <= 1536 else _BIG)
    per = nb // tn
    tm = _tile(m, _BIG)
    tk = next(c for c in _DEEP if r % c == 0 and len(sections) * c * tn * 4 <= SECTION_VMEM_BYTES)
    nk = r // tk

    def body(a_ref, *refs):
        b_refs, o_ref, acc = refs[:len(sections)], refs[len(sections)], refs[len(sections) + 1]
        j, kk = pl.program_id(1), pl.program_id(2)

        @pl.when(kk == 0)
        def _():
            acc[...] = jnp.zeros_like(acc)

        for b_ref, lo, cnt in zip(b_refs, firsts, counts):
            @pl.when(jnp.logical_and(j >= lo, j < lo + cnt))
            def _(b_ref=b_ref):
                acc[...] += _bdot(a_ref[...], b_ref[...], 0, 0)

        @pl.when(kk == nk - 1)
        def _():
            o_ref[0] = acc[...].astype(bf16)

    def spec(lo, cnt):
        def index(i, j, kk):
            return jnp.where(j < lo, 0, jnp.where(j >= lo + cnt, nk - 1, kk)), jnp.clip(j - lo, 0, cnt - 1)
        return pl.BlockSpec((tk, tn), index)

    return _call(body, name, (m // tm, n // tn, nk),
                 [pl.BlockSpec((tk, tm), lambda i, j, kk: (kk, i))] + [spec(lo, cnt) for lo, cnt in zip(firsts, counts)],
                 pl.BlockSpec((1, tm, tn), lambda i, j, kk: (j // per, i, j % per)), _sds((nblk, m, nb), bf16),
                 [pltpu.VMEM((tm, tn), f32)], exchange=exchange)(a, *sections)


def mm_nt_cols(sections, w3, out_dtype, name, exchange=None):
    m = sections[0].shape[0]
    nblk, k, nb = w3.shape
    tc, firsts, counts = _section_tiles(sections, (nb,) if nb <= 1536 else _DEEP)
    assert nblk * nb == tc * sum(counts) and nb % tc == 0
    per = nb // tc
    tm, tn = _tile(m, _BIG), _tile(k, _BIG)
    nk = nblk * per

    def body(*refs):
        a_refs, w_ref, o_ref, acc = refs[:len(sections)], refs[len(sections)], refs[len(sections) + 1], refs[len(sections) + 2]
        kk = pl.program_id(2)

        @pl.when(kk == 0)
        def _():
            acc[...] = jnp.zeros_like(acc)

        for a_ref, lo, cnt in zip(a_refs, firsts, counts):
            @pl.when(jnp.logical_and(kk >= lo, kk < lo + cnt))
            def _(a_ref=a_ref):
                acc[...] += _bdot(a_ref[...], w_ref[0], 1, 1)

        @pl.when(kk == nk - 1)
        def _():
            o_ref[...] = acc[...].astype(o_ref.dtype)

    def spec(lo, cnt):
        return pl.BlockSpec((tm, tc), lambda i, j, kk: (i, jnp.clip(kk - lo, 0, cnt - 1)))

    return _call(body, name, (m // tm, k // tn, nk),
                 [spec(lo, cnt) for lo, cnt in zip(firsts, counts)] + [pl.BlockSpec((1, tn, tc), lambda i, j, kk: (kk // per, j, kk % per))],
                 pl.BlockSpec((tm, tn), lambda i, j, kk: (i, j)), _sds((m, k), out_dtype),
                 [pltpu.VMEM((tm, tn), f32)], exchange=exchange)(*sections, w3)


def _rms(x, g):
    return x * lax.rsqrt(jnp.mean(x * x, axis=-1, keepdims=True) + EPS) * g


def _rms_pad(x, g, width):
    return x * lax.rsqrt(jnp.sum(x * x, axis=-1, keepdims=True) / width + EPS) * g


def _first(*ids):
    ok = ids[0] == 0
    for i in ids[1:]:
        ok = jnp.logical_and(ok, i == 0)
    return ok


def _acc_row(ref, val, first):
    @pl.when(first)
    def _():
        ref[...] = jnp.zeros_like(ref)

    ref[0:1, :] += val


def rms_fwd(x, g, name):
    r, w = x.shape
    tm = _tile(r, (256, 128, 64, 32, 16, 8))

    def body(x_ref, g_ref, o_ref):
        o_ref[...] = _rms(x_ref[...], g_ref[...]).astype(bf16)

    return _call(body, name, (r // tm,), [pl.BlockSpec((tm, w), lambda i: (i, 0)), pl.BlockSpec((1, w), lambda i: (0, 0))],
                 pl.BlockSpec((tm, w), lambda i: (i, 0)), _sds((r, w), bf16))(x, g)


def resid_rms(xa, xb, g, name):
    r, w = xa.shape
    tm = _tile(r, (256, 128, 64, 32, 16, 8))

    def body(a_ref, b_ref, g_ref, s_ref, u_ref):
        xs = a_ref[...] + b_ref[...]
        s_ref[...] = xs
        u_ref[...] = _rms(xs, g_ref[...]).astype(bf16)

    row = pl.BlockSpec((tm, w), lambda i: (i, 0))
    return _call(body, name, (r // tm,), [row, row, pl.BlockSpec((1, w), lambda i: (0, 0))], [row, row],
                 [_sds((r, w), f32), _sds((r, w), bf16)])(xa, xb, g)


def rms_bwd(x, g, dys, dres, name, want_dx=True, want_b16=False):
    r, w = x.shape
    tm = _tile(r, (256, 128, 64, 32, 16, 8))
    nd = len(dys)

    def body(*refs):
        x_ref, g_ref = refs[0], refs[1]
        dy = refs[2][...]
        for j in range(1, nd):
            dy = dy + refs[2 + j][...]
        pos = 2 + nd
        _, vjp = jax.vjp(_rms, x_ref[...], g_ref[...])
        dx, dg = vjp(dy)
        if dres is not None:
            dx = dx + refs[pos][...]
            pos += 1
        if want_dx:
            refs[pos][...] = dx
            pos += 1
        if want_b16:
            refs[pos][...] = dx.astype(bf16)
            pos += 1
        _acc_row(refs[pos], dg, pl.program_id(0) == 0)

    row = pl.BlockSpec((tm, w), lambda i: (i, 0))
    ins = [x, g] + list(dys) + ([dres] if dres is not None else [])
    in_specs = [row, pl.BlockSpec((1, w), lambda i: (0, 0))] + [row] * (nd + (dres is not None))
    out_specs = [row] * (want_dx + want_b16) + [pl.BlockSpec((8, w), lambda i: (0, 0))]
    out_shape = ([_sds((r, w), f32)] if want_dx else []) + ([_sds((r, w), bf16)] if want_b16 else []) + [_sds((8, w), f32)]
    return _call(body, name, (r // tm,), in_specs, out_specs, out_shape)(*ins)


def lat_norm(z_main, g_qa, g_kva):
    t = z_main.shape[0]
    tm = _tile(t, (512, 256, 128, 64))

    def body(z_ref, gq_ref, gk_ref, q_ref, k_ref):
        q_ref[...] = _rms(z_ref[:, :Q_LORA], gq_ref[...]).astype(bf16)
        k_ref[...] = _rms(z_ref[:, Q_LORA:], gk_ref[...]).astype(bf16)

    return _call(body, "lat_norm", (t // tm,),
                 [pl.BlockSpec((tm, Q_LORA + KV_LORA), lambda i: (i, 0)), pl.BlockSpec((1, Q_LORA), lambda i: (0, 0)),
                  pl.BlockSpec((1, KV_LORA), lambda i: (0, 0))],
                 [pl.BlockSpec((tm, Q_LORA), lambda i: (i, 0)), pl.BlockSpec((tm, KV_LORA), lambda i: (i, 0))],
                 [_sds((t, Q_LORA), bf16), _sds((t, KV_LORA), bf16)])(z_main, g_qa, g_kva)


def lat_norm_bwd(z_main, g_qa, g_kva, dqa, dkv):
    t = z_main.shape[0]
    tm = _tile(t, (512, 256, 128, 64))

    def body(z_ref, gq_ref, gk_ref, dq_ref, dk_ref, dz_ref, dgq_ref, dgk_ref):
        first = pl.program_id(0) == 0
        _, vq = jax.vjp(_rms, z_ref[:, :Q_LORA], gq_ref[...])
        dx, dg = vq(dq_ref[...])
        dz_ref[:, :Q_LORA] = dx.astype(bf16)
        _acc_row(dgq_ref, dg, first)
        _, vk = jax.vjp(_rms, z_ref[:, Q_LORA:], gk_ref[...])
        dx, dg = vk(dk_ref[...])
        dz_ref[:, Q_LORA:] = dx.astype(bf16)
        _acc_row(dgk_ref, dg, first)

    return _call(body, "lat_norm_bwd", (t // tm,),
                 [pl.BlockSpec((tm, Q_LORA + KV_LORA), lambda i: (i, 0)), pl.BlockSpec((1, Q_LORA), lambda i: (0, 0)),
                  pl.BlockSpec((1, KV_LORA), lambda i: (0, 0)), pl.BlockSpec((tm, Q_LORA), lambda i: (i, 0)),
                  pl.BlockSpec((tm, KV_LORA), lambda i: (i, 0))],
                 [pl.BlockSpec((tm, Q_LORA + KV_LORA), lambda i: (i, 0)), pl.BlockSpec((8, Q_LORA), lambda i: (0, 0)),
                  pl.BlockSpec((8, KV_LORA), lambda i: (0, 0))],
                 [_sds((t, Q_LORA + KV_LORA), bf16), _sds((8, Q_LORA), f32), _sds((8, KV_LORA), f32)])(z_main, g_qa, g_kva, dqa, dkv)


def rope_tables(pos_col, inv_freq):
    t = pos_col.shape[0]
    tm = _tile(t, (512, 256, 128, 64))

    def body(p_ref, f_ref, c_ref, s_ref):
        ang = p_ref[...].astype(f32) * f_ref[...]
        lane = lax.broadcasted_iota(jnp.int32, ang.shape, 1)
        c_ref[...] = jnp.where(lane < ROPE, jnp.cos(ang), 0.0)
        sn = jnp.sin(ang)
        s_ref[...] = jnp.where(lane < ROPE // 2, -sn, jnp.where(lane < ROPE, sn, 0.0))

    return _call(body, "rope_tables", (t // tm,),
                 [pl.BlockSpec((tm, 1), lambda i: (i, 0)), pl.BlockSpec((1, TAIL), lambda i: (0, 0))],
                 [pl.BlockSpec((tm, TAIL), lambda i: (i, 0))] * 2, [_sds((t, TAIL), f32)] * 2)(pos_col, inv_freq)


def _swap_halves(n):
    lane = lax.broadcasted_iota(jnp.int32, n.shape, 1)
    return jnp.where(lane < ROPE // 2, pltpu.roll(n, TAIL - ROPE // 2, 1), pltpu.roll(n, ROPE // 2, 1))


def _rope(n, c, s):
    return n * c + _swap_halves(n) * s


def _rope_t(d, c, s):
    return d * c + _swap_halves(d * s)


def _prep_specs(tm):
    head = pl.BlockSpec((tm, HEAD_PAD), lambda i, h: (i, h))
    row = pl.BlockSpec((tm, TAIL), lambda i, h: (i, 0))
    gain = pl.BlockSpec((1, TAIL), lambda i, h: (0, 0))
    return head, row, gain


def _pe_in(zt):
    lane = lax.broadcasted_iota(jnp.int32, zt.shape, 1)
    return jnp.where(lane < ROPE, zt, 0.0)


def mla_prep(q_raw, kv_raw, z_tail, cos, sin, gqn, gqp, gkn, gkp):
    t = q_raw.shape[0]
    tm = _tile(t, (1024, 512, 256, 128, 64))

    def body(q_ref, kv_ref, zt_ref, c_ref, s_ref, gqn_ref, gqp_ref, gkn_ref, gkp_ref, qh_ref, kh_ref, vh_ref):
        c, s = c_ref[...], s_ref[...]
        qh_ref[:, :NOPE] = _rms(q_ref[:, :NOPE], gqn_ref[...]).astype(bf16)
        qh_ref[:, NOPE:] = _rope(_rms_pad(q_ref[:, NOPE:], gqp_ref[...], ROPE), c, s).astype(bf16)
        kh_ref[:, :NOPE] = _rms(kv_ref[:, :NOPE], gkn_ref[...]).astype(bf16)
        kh_ref[:, NOPE:] = _rope(_rms_pad(_pe_in(zt_ref[...]), gkp_ref[...], ROPE), c, s).astype(bf16)
        vh_ref[...] = kv_ref[:, NOPE:].astype(bf16)

    head, row, gain = _prep_specs(tm)
    return _call(body, "mla_prep", (t // tm, MLA_HEADS), [head, head, row, row, row, gain, gain, gain, gain],
                 [head, head, pl.BlockSpec((tm, V_HEAD), lambda i, h: (i, h))],
                 [_sds((t, MLA_HEADS * HEAD_PAD), bf16), _sds((t, MLA_HEADS * HEAD_PAD), bf16), _sds((t, MLA_HEADS * V_HEAD), bf16)],
                 )(q_raw, kv_raw, z_tail, cos, sin, gqn, gqp, gkn, gkp)


def mla_prep_bwd(q_raw, kv_raw, z_tail, cos, sin, gqn, gqp, gkn, gkp, dqh, dkh, dvh):
    t = q_raw.shape[0]
    tm = _tile(t, (1024, 512, 256, 128, 64))
    pad_norm = functools.partial(_rms_pad, width=ROPE)

    def body(q_ref, kv_ref, zt_ref, c_ref, s_ref, gqn_ref, gqp_ref, gkn_ref, gkp_ref, dqh_ref, dkh_ref, dvh_ref,
             dq_ref, dkv_ref, dzt_ref, dgqn_ref, dgqp_ref, dgkn_ref, dgkp_ref):
        i, h = pl.program_id(0), pl.program_id(1)
        first = _first(i, h)
        c, s = c_ref[...], s_ref[...]
        _, v1 = jax.vjp(_rms, q_ref[:, :NOPE], gqn_ref[...])
        dx, dg = v1(dqh_ref[:, :NOPE])
        dq_ref[:, :NOPE] = dx.astype(bf16)
        _acc_row(dgqn_ref, dg, first)
        _, v2 = jax.vjp(pad_norm, q_ref[:, NOPE:], gqp_ref[...])
        dx, dg = v2(_rope_t(dqh_ref[:, NOPE:], c, s))
        dq_ref[:, NOPE:] = dx.astype(bf16)
        _acc_row(dgqp_ref, dg, first)
        _, v3 = jax.vjp(_rms, kv_ref[:, :NOPE], gkn_ref[...])
        dx, dg = v3(dkh_ref[:, :NOPE])
        dkv_ref[:, :NOPE] = dx.astype(bf16)
        _acc_row(dgkn_ref, dg, first)
        dkv_ref[:, NOPE:] = dvh_ref[...].astype(bf16)
        _, v4 = jax.vjp(pad_norm, _pe_in(zt_ref[...]), gkp_ref[...])
        dx, dg = v4(_rope_t(dkh_ref[:, NOPE:], c, s))
        _acc_row(dgkp_ref, dg, first)

        @pl.when(h == 0)
        def _():
            dzt_ref[...] = jnp.zeros_like(dzt_ref)

        dzt_ref[...] += dx

    head, row, gain = _prep_specs(tm)
    acc = pl.BlockSpec((8, TAIL), lambda i, h: (0, 0))
    vspec = pl.BlockSpec((tm, V_HEAD), lambda i, h: (i, h))
    return _call(body, "mla_prep_bwd", (t // tm, MLA_HEADS),
                 [head, head, row, row, row, gain, gain, gain, gain, head, head, vspec],
                 [head, head, row, acc, acc, acc, acc],
                 [_sds((t, MLA_HEADS * HEAD_PAD), bf16), _sds((t, MLA_HEADS * HEAD_PAD), bf16), _sds((t, TAIL), f32)]
                 + [_sds((8, TAIL), f32)] * 4)(q_raw, kv_raw, z_tail, cos, sin, gqn, gqp, gkn, gkp, dqh, dkh, dvh)


ATT_BLOCK = 512
NEG = -1e30
ATT_SCALE = (NOPE + ROPE) ** -0.5
ATT_HEADS = 2
ATT_HEADS_FWD = 4


def _chunk_visible(shape, key_axis):
    kc = lax.broadcasted_iota(jnp.int32, shape, key_axis) >> CHUNK_SHIFT
    qc = lax.broadcasted_iota(jnp.int32, shape, 1 - key_axis) >> CHUNK_SHIFT
    return kc <= qc


def mla_fwd(qh, kh, vh, exchange=None):
    t = qh.shape[0]
    tb = min(ATT_BLOCK, t)
    nb = t // tb

    hp = ATT_HEADS_FWD

    def body(q_ref, k_ref, v_ref, o_ref, ob_ref, lse_ref, m_s, l_s, acc):
        qi, ki = pl.program_id(1), pl.program_id(2)

        @pl.when(ki == 0)
        def _():
            m_s[...] = jnp.full_like(m_s, NEG)
            l_s[...] = jnp.zeros_like(l_s)
            acc[...] = jnp.zeros_like(acc)

        def step(diagonal):
            new = []
            for j in range(hp):
                q, k = q_ref[:, j * HEAD_PAD:(j + 1) * HEAD_PAD], k_ref[:, j * HEAD_PAD:(j + 1) * HEAD_PAD]
                s = _bdot(k, q, 1, 1) * ATT_SCALE
                if diagonal:
                    s = jnp.where(_chunk_visible(s.shape, 0), s, -jnp.inf)
                m_old = m_s[j]
                m_new = jnp.maximum(m_old, jnp.max(s, axis=0, keepdims=True))
                p = jnp.exp(s - m_new)
                alpha = jnp.exp(m_old - m_new)
                l_new = alpha * l_s[j] + jnp.sum(p, axis=0, keepdims=True)
                acc_new = alpha * acc[j] + _bdot(v_ref[:, j * V_HEAD:(j + 1) * V_HEAD], p, 0, 0)
                new.append((m_new, l_new, acc_new))
            for j, (m_new, l_new, acc_new) in enumerate(new):
                m_s[j] = m_new
                l_s[j] = l_new
                acc[j] = acc_new
            return new

        @pl.when(ki < qi)
        def _():
            step(False)

        @pl.when(ki == qi)
        def _():
            for j, (m_new, l_new, acc_new) in enumerate(step(True)):
                o = (acc_new / l_new).T
                o_ref[:, j * V_HEAD:(j + 1) * V_HEAD] = o
                ob_ref[:, j * V_HEAD:(j + 1) * V_HEAD] = o.astype(bf16)
                lse_ref[j] = m_new + jnp.log(l_new)

    kv = lambda g, qi, ki: (jnp.minimum(ki, qi), g)
    o_spec = pl.BlockSpec((tb, hp * V_HEAD), lambda g, qi, ki: (qi, g))
    return _call(body, "mla_fwd", (MLA_HEADS // hp, nb, nb),
                 [pl.BlockSpec((tb, hp * HEAD_PAD), lambda g, qi, ki: (qi, g)), pl.BlockSpec((tb, hp * HEAD_PAD), kv),
                  pl.BlockSpec((tb, hp * V_HEAD), kv)],
                 [o_spec, o_spec, pl.BlockSpec((hp, 1, tb), lambda g, qi, ki: (g, 0, qi))],
                 [_sds((t, MLA_HEADS * V_HEAD), f32), _sds((t, MLA_HEADS * V_HEAD), bf16), _sds((MLA_HEADS, 1, t), f32)],
                 [pltpu.VMEM((hp, 1, tb), f32), pltpu.VMEM((hp, 1, tb), f32), pltpu.VMEM((hp, V_HEAD, tb), f32)],
                 exchange=exchange)(qh, kh, vh)


def mla_delta(o, do):
    t = o.shape[0]
    tm = _tile(t, (512, 256, 128, 64))

    def body(o_ref, do_ref, d_ref):
        for h in range(MLA_HEADS):
            cols = slice(h * V_HEAD, (h + 1) * V_HEAD)
            d_ref[h] = jnp.sum(o_ref[:, cols] * do_ref[:, cols], axis=1, keepdims=True)

    blk = pl.BlockSpec((tm, MLA_HEADS * V_HEAD), lambda i: (i, 0))
    return _call(body, "mla_delta", (t // tm,), [blk, blk], pl.BlockSpec((MLA_HEADS, tm, 1), lambda i: (0, i, 0)),
                 _sds((MLA_HEADS, t, 1), f32))(o, do)


def mla_bwd(qh, kh, vh, do, lse_row, delta_row, exchange=None):
    t = qh.shape[0]
    tb = min(ATT_BLOCK, t)
    nb = t // tb

    hp = ATT_HEADS

    def body(q_ref, k_ref, v_ref, do_ref, lse_ref, dl_ref, dq_ref, dk_ref, dv_ref, dk_acc, dv_acc):
        ki, qi = pl.program_id(1), pl.program_id(2)

        @pl.when(jnp.logical_and(ki == 0, qi == 0))
        def _():
            dq_ref[...] = jnp.zeros_like(dq_ref)

        @pl.when(qi == 0)
        def _():
            dk_acc[...] = jnp.zeros_like(dk_acc)
            dv_acc[...] = jnp.zeros_like(dv_acc)

        def step(diagonal):
            rows = pl.ds(pl.multiple_of(qi * tb, tb), tb)
            new = []
            for j in range(hp):
                qc, vc = slice(j * HEAD_PAD, (j + 1) * HEAD_PAD), slice(j * V_HEAD, (j + 1) * V_HEAD)
                q, k, do_b = q_ref[:, qc], k_ref[:, qc], do_ref[:, vc]
                s = _bdot(k, q, 1, 1) * ATT_SCALE
                if diagonal:
                    s = jnp.where(_chunk_visible(s.shape, 0), s, -jnp.inf)
                p = jnp.exp(s - lse_ref[j])
                dp = _bdot(v_ref[:, vc], do_b, 1, 1)
                ds = p * (dp - dl_ref[j]) * ATT_SCALE
                new.append((dv_acc[:, vc] + _bdot(p, do_b, 1, 0), dk_acc[:, qc] + _bdot(ds, q, 1, 0),
                            dq_ref[rows, qc] + _bdot(ds, k, 0, 0)))
            for j, (dv, dk, dq) in enumerate(new):
                dv_acc[:, j * V_HEAD:(j + 1) * V_HEAD] = dv
                dk_acc[:, j * HEAD_PAD:(j + 1) * HEAD_PAD] = dk
                dq_ref[rows, j * HEAD_PAD:(j + 1) * HEAD_PAD] = dq

        @pl.when(qi > ki)
        def _():
            step(False)

        @pl.when(qi == ki)
        def _():
            step(True)

        @pl.when(qi == nb - 1)
        def _():
            dk_ref[...] = dk_acc[...]
            dv_ref[...] = dv_acc[...]

    qs = lambda g, ki, qi: (jnp.maximum(qi, ki), g)
    ks = lambda g, ki, qi: (ki, g)
    vec = pl.BlockSpec((hp, 1, tb), lambda g, ki, qi: (g, 0, jnp.maximum(qi, ki)))
    return _call(body, "mla_bwd", (MLA_HEADS // hp, nb, nb),
                 [pl.BlockSpec((tb, hp * HEAD_PAD), qs), pl.BlockSpec((tb, hp * HEAD_PAD), ks), pl.BlockSpec((tb, hp * V_HEAD), ks),
                  pl.BlockSpec((tb, hp * V_HEAD), qs), vec, vec],
                 [pl.BlockSpec((t, hp * HEAD_PAD), lambda g, ki, qi: (0, g)), pl.BlockSpec((tb, hp * HEAD_PAD), ks),
                  pl.BlockSpec((tb, hp * V_HEAD), ks)],
                 [_sds((t, MLA_HEADS * HEAD_PAD), f32), _sds((t, MLA_HEADS * HEAD_PAD), f32), _sds((t, MLA_HEADS * V_HEAD), f32)],
                 [pltpu.VMEM((tb, hp * HEAD_PAD), f32), pltpu.VMEM((tb, hp * V_HEAD), f32)], exchange=exchange)(
        qh, kh, vh, do, lse_row, delta_row)


PAD = 8


def _conv_taps(pad_ref, w, width, t):
    y = pad_ref[PAD - width + 1:PAD - width + 1 + t, :] * w[0:1, :]
    for j in range(1, width):
        y = y + pad_ref[PAD - width + 1 + j:PAD - width + 1 + j + t, :] * w[j:j + 1, :]
    return y


def _conv_bwd(xpad_ref, dpad_ref, w, da, width, t):
    dpad_ref[0:t, :] = da
    dpad_ref[t:t + PAD, :] = jnp.zeros((PAD, da.shape[1]), f32)
    dx = dpad_ref[width - 1:width - 1 + t, :] * w[0:1, :]
    for j in range(1, width):
        dx = dx + dpad_ref[width - 1 - j:width - 1 - j + t, :] * w[j:j + 1, :]
    dws = [jnp.sum(da * xpad_ref[PAD - width + 1 + j:PAD - width + 1 + j + t, :], axis=0, keepdims=True) for j in range(width)]
    return dx, dws


def _load_pad(pad_ref, x, t):
    pad_ref[0:PAD, :] = jnp.zeros((PAD, x.shape[1]), f32)
    pad_ref[PAD:PAD + t, :] = x


assert ML_DK == 128


def qk_conv(z_main, conv_qk):
    t = z_main.shape[0]
    base = O_Q // ML_DK

    def body(z_ref, w_ref, o_ref, pad):
        _load_pad(pad, z_ref[...], t)
        a = _conv_taps(pad, w_ref[...], ML_CONV, t)
        sc = jnp.where(pl.program_id(0) < ML_HEADS, ML_DK ** -0.5, 1.0)
        o_ref[0] = jax.nn.silu(a) * sc

    return _call(body, "qk_conv", (2 * ML_HEADS,),
                 [pl.BlockSpec((t, ML_DK), lambda j: (0, base + j)), pl.BlockSpec((ML_CONV, ML_DK), lambda j: (0, j))],
                 pl.BlockSpec((1, t, ML_DK), lambda j: (j, 0, 0)), _sds((2 * ML_HEADS, t, ML_DK), f32),
                 [pltpu.VMEM((t + PAD, ML_DK), f32)])(z_main, conv_qk)


def qk_conv_bwd(z_main, conv_qk, dq, dk):
    t = z_main.shape[0]
    base = O_Q // ML_DK

    def body(z_ref, w_ref, dq_ref, dk_ref, dz_ref, dw_ref, pad, dpad):
        _load_pad(pad, z_ref[...], t)
        w = w_ref[...]
        a = _conv_taps(pad, w, ML_CONV, t)
        is_q = pl.program_id(0) < ML_HEADS
        d = jnp.where(is_q, dq_ref[0] * (ML_DK ** -0.5), dk_ref[0])
        _, vjp = jax.vjp(jax.nn.silu, a)
        da, = vjp(d)
        dx, dws = _conv_bwd(pad, dpad, w, da, ML_CONV, t)
        dz_ref[...] = dx.astype(bf16)
        for j in range(ML_CONV):
            dw_ref[j:j + 1, :] = dws[j]

    head = lambda pick: pl.BlockSpec((1, t, ML_DK), lambda j: (pick(j), 0, 0))
    return _call(body, "qk_conv_bwd", (2 * ML_HEADS,),
                 [pl.BlockSpec((t, ML_DK), lambda j: (0, base + j)), pl.BlockSpec((ML_CONV, ML_DK), lambda j: (0, j)),
                  head(lambda j: jnp.minimum(j, ML_HEADS - 1)), head(lambda j: jnp.maximum(j - ML_HEADS, 0))],
                 [pl.BlockSpec((t, ML_DK), lambda j: (0, j)), pl.BlockSpec((ML_CONV, ML_DK), lambda j: (0, j))],
                 [_sds((t, 2 * ML_QK), bf16), _sds((ML_CONV, 2 * ML_QK), f32)],
                 [pltpu.VMEM((t + PAD, ML_DK), f32), pltpu.VMEM((t + PAD, ML_DK), f32)])(z_main, conv_qk, dq, dk)


def glu_fwd(hup, conv_w, bias):
    t, f2 = hup.shape
    nf = f2 // 2 // 128

    def body(h1_ref, h2_ref, w1_ref, w2_ref, b1_ref, b2_ref, o_ref, pad):
        _load_pad(pad, h1_ref[...], t)
        a1 = _conv_taps(pad, w1_ref[...], FFN_CONV, t) + b1_ref[...]
        _load_pad(pad, h2_ref[...], t)
        a2 = _conv_taps(pad, w2_ref[...], FFN_CONV, t) + b2_ref[...]
        o_ref[...] = (jax.nn.silu(a1) * a2).astype(bf16)

    col = lambda off: pl.BlockSpec((t, 128), lambda j: (0, j + off))
    wsp = lambda off: pl.BlockSpec((FFN_CONV, 128), lambda j: (0, j + off))
    bsp = lambda off: pl.BlockSpec((1, 128), lambda j: (0, j + off))
    return _call(body, "glu_fwd", (nf,), [col(0), col(nf), wsp(0), wsp(nf), bsp(0), bsp(nf)], col(0), _sds((t, f2 // 2), bf16),
                 [pltpu.VMEM((t + PAD, 128), f32)])(hup, hup, conv_w, conv_w, bias, bias)


def glu_bwd(hup, conv_w, bias, dg, exchange=None):
    t, f2 = hup.shape
    f = f2 // 2
    nf = f // 128

    def body(h1_ref, h2_ref, w1_ref, w2_ref, b1_ref, b2_ref, dg_ref, dh1_ref, dh2_ref, dw1_ref, dw2_ref, db1_ref, db2_ref,
             pad1, pad2, dpad):
        _load_pad(pad1, h1_ref[...], t)
        _load_pad(pad2, h2_ref[...], t)
        w1, w2 = w1_ref[...], w2_ref[...]
        a1 = _conv_taps(pad1, w1, FFN_CONV, t) + b1_ref[...]
        a2 = _conv_taps(pad2, w2, FFN_CONV, t) + b2_ref[...]
        d = dg_ref[...]
        _, vjp = jax.vjp(jax.nn.silu, a1)
        da1, = vjp(d * a2)
        da2 = d * jax.nn.silu(a1)
        for da, pad, w, dh_ref, dw_ref, db_ref in ((da1, pad1, w1, dh1_ref, dw1_ref, db1_ref), (da2, pad2, w2, dh2_ref, dw2_ref, db2_ref)):
            dx, dws = _conv_bwd(pad, dpad, w, da, FFN_CONV, t)
            dh_ref[...] = dx.astype(bf16)
            for j in range(FFN_CONV):
                dw_ref[j:j + 1, :] = dws[j]
            db_ref[...] = jnp.sum(da, axis=0, keepdims=True)

    col = lambda off: pl.BlockSpec((t, 128), lambda j: (0, j + off))
    wsp = lambda off: pl.BlockSpec((FFN_CONV, 128), lambda j: (0, j + off))
    bsp = lambda off: pl.BlockSpec((1, 128), lambda j: (0, j + off))
    return _call(body, "glu_bwd", (nf,), [col(0), col(nf), wsp(0), wsp(nf), bsp(0), bsp(nf), col(0)],
                 [col(0), col(0), wsp(0), wsp(0), bsp(0), bsp(0)],
                 [_sds((t, f), bf16)] * 2 + [_sds((FFN_CONV, f), f32)] * 2 + [_sds((1, f), f32)] * 2,
                 [pltpu.VMEM((t + PAD, 128), f32)] * 3, exchange=exchange)(hup, hup, conv_w, conv_w, bias, bias, dg)


def gate_act(z_tail, b_tile):
    t = z_tail.shape[0]
    tm = _tile(t, (512, 256, 128, 64))

    def body(z_ref, b_ref, o_ref):
        x = z_ref[...] + b_ref[...]
        lane = lax.broadcasted_iota(jnp.int32, x.shape, 1)
        o_ref[...] = jnp.where(lane < T_F, x, jax.nn.log_sigmoid(x))

    row = pl.BlockSpec((tm, TAIL), lambda i: (i, 0))
    return _call(body, "gate_act", (t // tm,), [row, pl.BlockSpec((1, TAIL), lambda i: (0, 0))], row, _sds((t, TAIL), f32))(z_tail, b_tile)


def tail_bwd(z_tail, b_tile, dzt_pe, dgate):
    t = z_tail.shape[0]
    tm = _tile(t, (512, 256, 128, 64))

    def body(z_ref, b_ref, dpe_ref, dg_ref, dz_ref, db_ref):
        x = z_ref[...] + b_ref[...]
        lane = lax.broadcasted_iota(jnp.int32, x.shape, 1)
        _, vjp = jax.vjp(jax.nn.log_sigmoid, x)
        df, = vjp(dg_ref[...])
        dgates = jnp.where(lane < T_F, dg_ref[...], df)
        dgates = jnp.where(jnp.logical_and(lane >= T_I, lane < T_F + ML_HEADS), dgates, 0.0)
        dz_ref[...] = jnp.where(lane < ROPE, dpe_ref[...], dgates).astype(bf16)
        _acc_row(db_ref, jnp.sum(dgates, axis=0, keepdims=True), pl.program_id(0) == 0)

    row = pl.BlockSpec((tm, TAIL), lambda i: (i, 0))
    return _call(body, "tail_bwd", (t // tm,), [row, pl.BlockSpec((1, TAIL), lambda i: (0, 0)), row, row],
                 [row, pl.BlockSpec((8, TAIL), lambda i: (0, 0))], [_sds((t, TAIL), bf16), _sds((8, TAIL), f32)])(z_tail, b_tile, dzt_pe, dgate)


def _hdot(a, b, ca, cb):
    return lax.dot_general(a.astype(bf16), b.astype(bf16), (((ca,), (cb,)), ((0,), (0,))), preferred_element_type=f32)


def _mlstm_step(q, k, v, igr, fgr, c_mat, n_vec, m):
    nh, ln = q.shape[0], CHUNK
    sq = (nh, ln, ln)
    row = lax.broadcasted_iota(jnp.int32, sq, 1)
    col = lax.broadcasted_iota(jnp.int32, sq, 2)
    eye = row == col

    def to_col(r):
        return jnp.sum(jnp.where(eye, jnp.broadcast_to(r, sq), 0.0), axis=2, keepdims=True)

    bc_r = jnp.sum(jnp.where(row <= col, jnp.broadcast_to(to_col(fgr), sq), 0.0), axis=1, keepdims=True)
    bc_c = to_col(bc_r)
    logw = jnp.where(col <= row, bc_c - bc_r + igr, -jnp.inf)
    inter = bc_c + m
    m_t = jnp.maximum(inter, jnp.max(logw, axis=2, keepdims=True))
    w_intra = jnp.exp(logw - m_t)
    w_inter = jnp.exp(inter - m_t)
    sc = _hdot(q, k, 2, 2) * w_intra
    num = w_inter * _hdot(q, c_mat, 2, 1) + _hdot(sc, v, 2, 1)
    qn = jnp.sum(q.astype(bf16).astype(f32) * n_vec.astype(bf16).astype(f32), axis=2, keepdims=True)
    den = w_inter * qn + jnp.sum(sc, axis=2, keepdims=True)
    h = num / jnp.maximum(jnp.abs(den), jnp.exp(-m_t))
    lane = lax.broadcasted_iota(jnp.int32, (nh, 1, ln), 2)
    b_last = jnp.sum(jnp.where(lane == ln - 1, bc_r, 0.0), axis=2, keepdims=True)
    logu = b_last - bc_r + igr
    m_new = jnp.maximum(b_last + m, jnp.max(logu, axis=2, keepdims=True))
    decay = jnp.exp(b_last + m - m_new)
    u_c = to_col(jnp.exp(logu - m_new))
    c_new = decay * c_mat + _hdot(u_c * k, v, 1, 1)
    n_new = decay * n_vec + jnp.sum(u_c.astype(bf16).astype(f32) * k.astype(bf16).astype(f32), axis=1, keepdims=True)
    return h, c_new, n_new, m_new


ML_VHALF = ML_V // 2
assert O_V % ML_VHALF == 0 and ML_HEADS % 2 == 0


def _ml_specs(nc, rev):
    cc = (lambda c: nc - 1 - c) if rev else (lambda c: c)
    q = pl.BlockSpec((ML_HEADS, CHUNK, ML_DK), lambda c: (0, cc(c), 0))
    k = pl.BlockSpec((ML_HEADS, CHUNK, ML_DK), lambda c: (1, cc(c), 0))
    v_lo = pl.BlockSpec((CHUNK, ML_VHALF), lambda c: (cc(c), O_V // ML_VHALF))
    v_hi = pl.BlockSpec((CHUNK, ML_VHALF), lambda c: (cc(c), O_V // ML_VHALF + 1))
    hv = pl.BlockSpec((ML_HEADS, CHUNK, ML_DV), lambda c: (0, cc(c), 0))
    gate = pl.BlockSpec((ML_HEADS, 1, 1, CHUNK), lambda c: (0, cc(c), 0, 0))
    cm = pl.BlockSpec((ML_HEADS, 1, ML_DK, ML_DV), lambda c: (0, cc(c), 0, 0))
    nv = pl.BlockSpec((ML_HEADS, 1, 1, ML_DK), lambda c: (0, cc(c), 0, 0))
    ms = pl.BlockSpec((ML_HEADS, 1, 1, 1), lambda c: (0, cc(c), 0, 0))
    return q, k, v_lo, v_hi, hv, gate, cm, nv, ms


_ML_STATE = [pltpu.VMEM((ML_HEADS, ML_DK, ML_DV), f32), pltpu.VMEM((ML_HEADS, 1, ML_DK), f32), pltpu.VMEM((ML_HEADS, 1, 1), f32)]


def _ml_zero_state(c_s, n_s, m_s):
    @pl.when(pl.program_id(0) == 0)
    def _():
        c_s[...] = jnp.zeros_like(c_s)
        n_s[...] = jnp.zeros_like(n_s)
        m_s[...] = jnp.zeros_like(m_s)


def _ml_heads_of(v_lo_ref, v_hi_ref):
    half = ML_HEADS // 2
    return jnp.stack([r[:, j * ML_DV:(j + 1) * ML_DV] for r in (v_lo_ref, v_hi_ref) for j in range(half)])


def mlstm_fwd(qk_act, z_main, ig, fg):
    t = qk_act.shape[1]
    nc = t // CHUNK

    def body(q_ref, k_ref, vl_ref, vh_ref, ig_ref, fg_ref, h_ref, c_out, n_out, m_out, c_s, n_s, m_s):
        _ml_zero_state(c_s, n_s, m_s)
        c0, n0, m0 = c_s[...], n_s[...], m_s[...]
        c_out[:, 0] = c0
        n_out[:, 0] = n0
        m_out[:, 0] = m0
        h, c2, n2, m2 = _mlstm_step(q_ref[...], k_ref[...], _ml_heads_of(vl_ref, vh_ref), ig_ref[:, 0], fg_ref[:, 0], c0, n0, m0)
        h_ref[...] = h
        c_s[...] = c2
        n_s[...] = n2
        m_s[...] = m2

    q, k, v_lo, v_hi, hv, gate, cm, nv, ms = _ml_specs(nc, False)
    return _call(body, "mlstm_fwd", (nc,), [q, k, v_lo, v_hi, gate, gate], [hv, cm, nv, ms],
                 [_sds((ML_HEADS, t, ML_DV), f32), _sds((ML_HEADS, nc, ML_DK, ML_DV), f32), _sds((ML_HEADS, nc, 1, ML_DK), f32),
                  _sds((ML_HEADS, nc, 1, 1), f32)], _ML_STATE)(qk_act, qk_act, z_main, z_main, ig, fg)


def mlstm_bwd(qk_act, z_main, ig, fg, c_all, n_all, m_all, dh, exchange=None):
    t = qk_act.shape[1]
    nc = t // CHUNK

    def body(q_ref, k_ref, vl_ref, vh_ref, ig_ref, fg_ref, c_ref, n_ref, m_ref, dh_ref, dq_ref, dk_ref, dv_ref, dig_ref, dfg_ref,
             dc_s, dn_s, dm_s):
        _ml_zero_state(dc_s, dn_s, dm_s)
        _, vjp = jax.vjp(_mlstm_step, q_ref[...], k_ref[...], _ml_heads_of(vl_ref, vh_ref), ig_ref[:, 0], fg_ref[:, 0],
                         c_ref[:, 0], n_ref[:, 0], m_ref[:, 0])
        dq, dk, dv, dig, dfg, dc, dn, dm = vjp((dh_ref[...], dc_s[...], dn_s[...], dm_s[...]))
        dq_ref[...] = dq
        dk_ref[...] = dk
        for j in range(ML_HEADS):
            dv_ref[:, j * ML_DV:(j + 1) * ML_DV] = dv[j].astype(bf16)
        dig_ref[:, 0] = dig
        dfg_ref[:, 0] = dfg
        dc_s[...] = dc
        dn_s[...] = dn
        dm_s[...] = dm

    q, k, v_lo, v_hi, hv, gate, cm, nv, ms = _ml_specs(nc, True)
    gshape = _sds((ML_HEADS, nc, 1, CHUNK), f32)
    return _call(body, "mlstm_bwd", (nc,), [q, k, v_lo, v_hi, gate, gate, cm, nv, ms, hv],
                 [q, q, pl.BlockSpec((CHUNK, ML_V), lambda c: (nc - 1 - c, 0)), gate, gate],
                 [_sds((ML_HEADS, t, ML_DK), f32), _sds((ML_HEADS, t, ML_DK), f32), _sds((t, ML_V), bf16), gshape, gshape],
                 _ML_STATE, exchange=exchange)(qk_act, qk_act, z_main, z_main, ig, fg, c_all, n_all, m_all, dh)


def _ml_out(h, zo, g):
    return _rms(h, g) * jax.nn.sigmoid(zo)


def mlstm_out(h, z_main, g_hnorm):
    t = h.shape[1]
    tm = _tile(t, (512, 256, 128, 64))
    zo = O_O // ML_DV

    def body(h_ref, z_ref, g_ref, y_ref):
        y_ref[...] = _ml_out(h_ref[0], z_ref[...], g_ref[0]).astype(bf16)

    return _call(body, "mlstm_out", (t // tm, ML_HEADS),
                 [pl.BlockSpec((1, tm, ML_DV), lambda i, hd: (hd, i, 0)), pl.BlockSpec((tm, ML_DV), lambda i, hd: (i, zo + hd)),
                  pl.BlockSpec((1, 1, ML_DV), lambda i, hd: (hd, 0, 0))],
                 pl.BlockSpec((tm, ML_DV), lambda i, hd: (i, hd)), _sds((t, ML_V), bf16))(h, z_main, g_hnorm)


def mlstm_out_bwd(h, z_main, g_hnorm, dy):
    t = h.shape[1]
    tm = _tile(t, (512, 256, 128, 64))
    zo = O_O // ML_DV

    def body(h_ref, z_ref, g_ref, dy_ref, dh_ref, dzo_ref, dg_ref):
        _, vjp = jax.vjp(_ml_out, h_ref[0], z_ref[...], g_ref[0])
        dh, dz, dg = vjp(dy_ref[...])
        dh_ref[0] = dh
        dzo_ref[...] = dz.astype(bf16)

        @pl.when(pl.program_id(1) == 0)
        def _():
            dg_ref[...] = jnp.zeros_like(dg_ref)

        dg_ref[0, 0:1, :] += dg

    head = pl.BlockSpec((1, tm, ML_DV), lambda hd, i: (hd, i, 0))
    blk = pl.BlockSpec((tm, ML_DV), lambda hd, i: (i, hd))
    return _call(body, "mlstm_out_bwd", (ML_HEADS, t // tm),
                 [head, pl.BlockSpec((tm, ML_DV), lambda hd, i: (i, zo + hd)), pl.BlockSpec((1, 1, ML_DV), lambda hd, i: (hd, 0, 0)), blk],
                 [head, blk, pl.BlockSpec((1, 8, ML_DV), lambda hd, i: (hd, 0, 0))],
                 [_sds((ML_HEADS, t, ML_DV), f32), _sds((t, ML_V), bf16), _sds((ML_HEADS, 8, ML_DV), f32)])(h, z_main, g_hnorm, dy)


def _merge(ga, gb, ya, yb):
    return jax.nn.sigmoid(ga) * ya + jax.nn.sigmoid(gb) * yb


def _merge_specs(t, d):
    tm = _tile(t, (512, 256, 128, 64))
    bw = _tile(d, (512, 256, 128))
    assert O_GA % bw == 0 and (O_GA + d) % bw == 0
    blk = pl.BlockSpec((tm, bw), lambda i, j: (i, j))
    ga = pl.BlockSpec((tm, bw), lambda i, j: (i, O_GA // bw + j))
    gb = pl.BlockSpec((tm, bw), lambda i, j: (i, (O_GA + d) // bw + j))
    return tm, bw, blk, ga, gb


def merge_fwd(z_main, ya, yb):
    t, d = ya.shape
    tm, bw, blk, ga, gb = _merge_specs(t, d)

    def body(ga_ref, gb_ref, ya_ref, yb_ref, o_ref):
        o_ref[...] = _merge(ga_ref[...], gb_ref[...], ya_ref[...], yb_ref[...]).astype(bf16)

    return _call(body, "merge_fwd", (t // tm, d // bw), [ga, gb, blk, blk], blk, _sds((t, d), bf16))(z_main, z_main, ya, yb)


def merge_bwd(z_main, ya, yb, dmerged):
    t, d = ya.shape
    tm, bw, blk, ga, gb = _merge_specs(t, d)

    def body(ga_ref, gb_ref, ya_ref, yb_ref, dm_ref, dga_ref, dgb_ref, dya_ref, dyb_ref):
        _, vjp = jax.vjp(_merge, ga_ref[...], gb_ref[...], ya_ref[...], yb_ref[...])
        dga, dgb, dya, dyb = vjp(dm_ref[...])
        dga_ref[...] = dga.astype(bf16)
        dgb_ref[...] = dgb.astype(bf16)
        dya_ref[...] = dya.astype(bf16)
        dyb_ref[...] = dyb.astype(bf16)

    return _call(body, "merge_bwd", (t // tm, d // bw), [ga, gb, blk, blk, blk], [blk] * 4, [_sds((t, d), bf16)] * 4)(
        z_main, z_main, ya, yb, dmerged)


def _cross(cq, ck, cv, gq, gk):
    outs = []
    for hd in range(CR_HEADS):
        sl = slice(hd * CR_HD, (hd + 1) * CR_HD)
        q = _rms(cq[:, sl], gq)
        k = _rms(ck[:, sl], gk)
        s = _bdot(q, k, 1, 1) * (CR_HD ** -0.5)
        p = jax.nn.softmax(s, axis=-1)
        outs.append(_bdot(p, cv[:, sl], 1, 0))
    return jnp.concatenate(outs, axis=1)


def cross_fwd(cq, ck, cv, gq, gk):
    t, w = cq.shape
    nm = ck.shape[0]
    tm = _tile(t, (512, 256, 128, 64))

    def body(q_ref, k_ref, v_ref, gq_ref, gk_ref, o_ref):
        o_ref[...] = _cross(q_ref[...], k_ref[...], v_ref[...], gq_ref[...], gk_ref[...]).astype(bf16)

    row = pl.BlockSpec((tm, w), lambda i: (i, 0))
    full = pl.BlockSpec((nm, w), lambda i: (0, 0))
    gain = pl.BlockSpec((1, CR_HD), lambda i: (0, 0))
    return _call(body, "cross_fwd", (t // tm,), [row, full, full, gain, gain], row, _sds((t, w), bf16))(cq, ck, cv, gq, gk)


def cross_bwd(cq, ck, cv, gq, gk, do):
    t, w = cq.shape
    nm = ck.shape[0]
    tm = _tile(t, (512, 256, 128, 64))

    def body(q_ref, k_ref, v_ref, gq_ref, gk_ref, do_ref, dq_ref, dk_ref, dv_ref, dgq_ref, dgk_ref):
        first = pl.program_id(0) == 0
        _, vjp = jax.vjp(_cross, q_ref[...], k_ref[...], v_ref[...], gq_ref[...], gk_ref[...])
        dq, dk, dv, dgq, dgk = vjp(do_ref[...])
        dq_ref[...] = dq.astype(bf16)

        @pl.when(first)
        def _():
            dk_ref[...] = jnp.zeros_like(dk_ref)
            dv_ref[...] = jnp.zeros_like(dv_ref)

        dk_ref[...] += dk
        dv_ref[...] += dv
        _acc_row(dgq_ref, dgq, first)
        _acc_row(dgk_ref, dgk, first)

    row = pl.BlockSpec((tm, w), lambda i: (i, 0))
    full = pl.BlockSpec((nm, w), lambda i: (0, 0))
    gain = pl.BlockSpec((1, CR_HD), lambda i: (0, 0))
    acc = pl.BlockSpec((8, CR_HD), lambda i: (0, 0))
    return _call(body, "cross_bwd", (t // tm,), [row, full, full, gain, gain, row], [row, full, full, acc, acc],
                 [_sds((t, w), bf16), _sds((nm, w), f32), _sds((nm, w), f32), _sds((8, CR_HD), f32), _sds((8, CR_HD), f32)])(
        cq, ck, cv, gq, gk, do)


def loss_head(x2, fo, target):
    t, d = x2.shape
    tm = _tile(t, (256, 128, 64, 32, 16, 8))

    def body(a_ref, b_ref, t_ref, dx_ref, dxb_ref, l_ref):
        err = a_ref[...] + b_ref[...] - t_ref[...]
        dx = err / d
        dx_ref[...] = dx
        dxb_ref[...] = dx.astype(bf16)
        part = 0.5 * jnp.sum(jnp.mean(err * err, axis=1, keepdims=True), axis=0, keepdims=True)
        _acc_row(l_ref, jnp.broadcast_to(part, (1, 128)), pl.program_id(0) == 0)

    row = pl.BlockSpec((tm, d), lambda i: (i, 0))
    return _call(body, "loss_head", (t // tm,), [row, row, row], [row, row, pl.BlockSpec((8, 128), lambda i: (0, 0))],
                 [_sds((t, d), f32), _sds((t, d), bf16), _sds((8, 128), f32)])(x2, fo, target)


def _place():
    x, y, c = lax.axis_index("x"), lax.axis_index("y"), lax.axis_index("c")
    peers = {}
    for r in range(1, N_DEV):
        px = 1 - x if r & 4 else x
        py = 1 - y if r & 2 else y
        pc = 1 - c if r & 1 else c
        peers[r] = ((px, py, pc), 4 * px + 2 * py + pc)
    return 4 * x + 2 * y + c, peers


N_REL = N_DEV - 1
RELATIONS = tuple(range(1, N_DEV))
SIBLING = 1
OTHER_CHIPS = (2, 4, 6)
PASSED_ON = (3, 5, 7)


def _exchange_ops(ins, outs, sems, scatter):
    n = len(ins)
    send_sems, recv_sems, local_sems = sems

    def tools():
        me, peers = _place()

        def copy(a, r, src, dst_idx, to):
            return pltpu.make_async_remote_copy(
                src_ref=src, dst_ref=outs[a].at[dst_idx], send_sem=send_sems.at[a * N_REL + r - 1],
                recv_sem=recv_sems.at[a * N_REL + r - 1], device_id=peers[to][0], device_id_type=MESH)

        def local(a):
            return pltpu.make_async_copy(ins[a].at[me] if scatter else ins[a], outs[a].at[me], local_sems.at[a])

        def arrival(a, r):
            return copy(a, r, ins[a].at[me] if scatter else ins[a], peers[r][1], r)

        return me, peers, copy, local, arrival

    if scatter:
        def sends():
            me, peers, copy, local, _ = tools()
            return [local(a) for a in range(n)], [copy(a, r, ins[a].at[peers[r][1]], me, r) for a in range(n) for r in RELATIONS]

        def start():
            loc, out = sends()
            for cp in loc + out:
                cp.start()

        middle = None
        waited_last = RELATIONS
    else:
        def sends():
            me, peers, copy, local, _ = tools()
            own = [copy(a, r, ins[a], me, r) for a in range(n) for r in (SIBLING,) + OTHER_CHIPS]
            return [local(a) for a in range(n)], own

        def passes():
            me, peers, copy, _, _ = tools()
            return [copy(a, r, outs[a].at[peers[r - 1][1]], peers[r - 1][1], SIBLING) for a in range(n) for r in PASSED_ON]

        def start():
            loc, out = sends()
            for cp in loc + out:
                cp.start()

        def middle():
            _, _, _, _, arrival = tools()
            fwd = passes()
            for a in range(n):
                for i, r in enumerate(PASSED_ON):
                    arrival(a, r - 1).wait_recv()
                    fwd[a * len(PASSED_ON) + i].start()

        waited_last = (SIBLING,) + PASSED_ON

    def wait():
        _, _, _, _, arrival = tools()
        for a in range(n):
            for r in waited_last:
                arrival(a, r).wait_recv()
        loc, out = sends()
        for cp in out + ([] if scatter else passes()):
            cp.wait_send()
        for cp in loc:
            cp.wait()

    return start, middle, wait


def _exchange_shapes(arrs, scatter):
    return [_sds(a.shape if scatter else (N_DEV,) + a.shape, a.dtype) for a in arrs]


def _exchange_sems(n):
    return [pltpu.SemaphoreType.DMA((n * N_REL,)), pltpu.SemaphoreType.DMA((n * N_REL,)), pltpu.SemaphoreType.DMA((n,))]


def _exchange(arrs, name, scatter):
    n = len(arrs)

    def body(*refs):
        start, middle, wait = _exchange_ops(refs[:n], refs[n:2 * n], refs[2 * n:], scatter)
        start()
        if middle is not None:
            middle()
        wait()

    any_spec = pl.BlockSpec(memory_space=pl.ANY)
    return pl.pallas_call(body, name=name, in_specs=[any_spec] * n, out_specs=[any_spec] * n,
                          out_shape=_exchange_shapes(arrs, scatter), scratch_shapes=_exchange_sems(n))(*arrs)


def cast_bf16(w, name):
    _, r, c = w.shape
    tr = _tile(r, (256, 128, 64, 32, 16))

    def body(w_ref, o_ref):
        o_ref[...] = w_ref[0].astype(bf16)

    return _call(body, name, (r // tr,), [pl.BlockSpec((1, tr, c), lambda i: (0, i, 0))], pl.BlockSpec((tr, c), lambda i: (i, 0)),
                 _sds((r, c), bf16))(w)


def _adamw(w, g, m, v):
    m = ADAM_B1 * m + (1.0 - ADAM_B1) * g
    v = ADAM_B2 * v + (1.0 - ADAM_B2) * jnp.square(g)
    m_hat = m / (1.0 - ADAM_B1 ** ADAM_STEP)
    v_hat = v / (1.0 - ADAM_B2 ** ADAM_STEP)
    delta = -ADAM_LR * (m_hat / (jnp.sqrt(v_hat) + ADAM_EPS) + ADAM_WD * w)
    return delta, m, v


def adam_sum(parts, w, m, v, name):
    _, r, c = parts.shape
    budget = 4 * 1024 * 1024
    tr = r
    for cand in (1024, 512, 256, 128, 64, 32, 16):
        if r % cand == 0 and N_DEV * cand * c * 4 <= budget:
            tr = cand
            break

    def body(p_ref, w_ref, m_ref, v_ref, g_ref, d_ref, m2_ref, v2_ref):
        g = p_ref[0].astype(f32)
        for k in range(1, N_DEV):
            g = g + p_ref[k].astype(f32)
        d, m2, v2 = _adamw(w_ref[0], g, m_ref[0], v_ref[0])
        g_ref[...] = g
        d_ref[...] = d
        m2_ref[...] = m2
        v2_ref[...] = v2

    blk = pl.BlockSpec((1, tr, c), lambda i: (0, i, 0))
    out = pl.BlockSpec((tr, c), lambda i: (i, 0))
    return _call(body, name, (r // tr,), [pl.BlockSpec((N_DEV, tr, c), lambda i: (0, i, 0)), blk, blk, blk], [out] * 4,
                 [_sds((r, c), f32)] * 4)(parts, w, m, v)


def sum_parts(parts, name):
    _, r, c = parts.shape

    def body(p_ref, o_ref):
        g = p_ref[0]
        for k in range(1, N_DEV):
            g = g + p_ref[k]
        o_ref[...] = g

    return pl.pallas_call(body, name=name, out_shape=_sds((r, c), f32))(parts)


def adam_flat(w, g, m, v, name):
    def body(w_ref, g_ref, m_ref, v_ref, d_ref, m2_ref, v2_ref):
        d, m2, v2 = _adamw(w_ref[...], g_ref[...], m_ref[...], v_ref[...])
        d_ref[...] = d
        m2_ref[...] = m2
        v2_ref[...] = v2

    return pl.pallas_call(body, name=name, out_shape=[_sds(w.shape, f32)] * 3)(w, g, m, v)


def _pack(vecs, multiple):
    flat = jnp.concatenate([v.reshape(-1) for v in vecs])
    n = flat.shape[0]
    total = -(-n // multiple) * multiple
    return jnp.pad(flat, (0, total - n))


def _unpack(flat, shapes):
    out, pos = [], 0
    for s in shapes:
        n = 1
        for d in s:
            n *= d
        out.append(flat[pos:pos + n].reshape(s))
        pos += n
    return out


def _pad_lanes(v, width=TAIL):
    return jnp.pad(v, ((0, 0), (0, width - v.shape[1])))


def kernel(x, mem, positions, g_mix, w_in, g_qa, w_qb, g_kva, w_kvb, g_qn_nope, g_qn_pe, g_kn_nope, g_kn_pe, conv_qk, b_if, g_hnorm, p_a, p_b, w_out, g_cross, g_mem, wq_c, wk_c, wv_c, g_cq, g_ck, wo_c, g_ffn, w_up, conv_ffn, b_conv_ffn, w_down, loss_target, m_g_mix, m_w_in, m_g_qa, m_w_qb, m_g_kva, m_w_kvb, m_g_qn_nope, m_g_qn_pe, m_g_kn_nope, m_g_kn_pe, m_conv_qk, m_b_if, m_g_hnorm, m_p_a, m_p_b, m_w_out, m_g_cross, m_g_mem, m_wq_c, m_wk_c, m_wv_c, m_g_cq, m_g_ck, m_wo_c, m_g_ffn, m_w_up, m_conv_ffn, m_b_conv_ffn, m_w_down, v_g_mix, v_w_in, v_g_qa, v_w_qb, v_g_kva, v_w_kvb, v_g_qn_nope, v_g_qn_pe, v_g_kn_nope, v_g_kn_pe, v_conv_qk, v_b_if, v_g_hnorm, v_p_a, v_p_b, v_w_out, v_g_cross, v_g_mem, v_wq_c, v_wk_c, v_wv_c, v_g_cq, v_g_ck, v_wo_c, v_g_ffn, v_w_up, v_conv_ffn, v_b_conv_ffn, v_w_down):
    args = dict(locals())
    names = ['g_mix', 'w_in', 'g_qa', 'w_qb', 'g_kva', 'w_kvb', 'g_qn_nope', 'g_qn_pe', 'g_kn_nope', 'g_kn_pe', 'conv_qk', 'b_if',
             'g_hnorm', 'p_a', 'p_b', 'w_out', 'g_cross', 'g_mem', 'wq_c', 'wk_c', 'wv_c', 'g_cq', 'g_ck', 'wo_c', 'g_ffn', 'w_up',
             'conv_ffn', 'b_conv_ffn', 'w_down']
    big = ['w_in', 'w_qb', 'w_kvb', 'p_a', 'p_b', 'w_out', 'wq_c', 'wk_c', 'wv_c', 'wo_c', 'w_up', 'w_down']
    sharded_small = ['conv_qk', 'g_hnorm', 'conv_ffn']
    replicated = [n for n in names if n not in big and n not in sharded_small]

    t, d = x.shape[1], x.shape[2]
    x2d, tgt = x[0], loss_target[0]
    mem2d = mem[0]
    me = 4 * lax.axis_index("x") + 2 * lax.axis_index("y") + lax.axis_index("c")
    nc = t // CHUNK
    f2 = b_conv_ffn.shape[1]
    wmain = O_GA + 2 * d

    first = ['w_in', 'w_qb', 'w_kvb']
    behind_in = ['p_a', 'p_b', 'w_out', 'wq_c', 'wk_c', 'wv_c', 'wo_c']
    shards = {n: cast_bf16(args[n], "cast_" + n) for n in big}
    small_local = _pack([args[n] for n in sharded_small], 128).reshape(1, -1)
    gathered = _exchange([shards[n] for n in first] + [small_local], "comm_gather_first", scatter=False)
    gw = dict(zip(first, gathered[:-1]))
    small_all = gathered[-1]
    full_small, pos = [], 0
    for n in sharded_small:
        _, rows, cols = args[n].shape
        piece = small_all[:, 0, pos:pos + rows * cols].reshape(N_DEV, rows, cols)
        full_small.append(piece.transpose(1, 0, 2).reshape(rows, N_DEV * cols))
        pos += rows * cols
    conv_qk_f, g_hnorm_f, conv_ffn_f = full_small

    shard_w = w_in.shape[2]
    c_kpe, c_q, c_i, c_o = O_Q, O_Q + ROPE, O_Q + ROPE + 2 * ML_QK + ML_V, O_Q + ROPE + 2 * ML_QK + ML_V + 2 * ML_HEADS
    segments = [(0, c_kpe, 'main', 0), (c_kpe, c_q, 'tail', 0), (c_q, c_i, 'main', O_Q), (c_i, c_o, 'tail', T_I),
                (c_o, N_DEV * shard_w, 'main', O_O)]

    def shard_cuts(lo, hi):
        return [(j, max(lo, j * shard_w) - j * shard_w, min(hi, (j + 1) * shard_w) - j * shard_w)
                for j in range(lo // shard_w, (hi - 1) // shard_w + 1)]

    def gathered_cols(target):
        return [gw['w_in'][j][:, a:b] for lo, hi, tg, _ in segments if tg == target for j, a, b in shard_cuts(lo, hi)]

    w_main = jnp.concatenate(gathered_cols('main'), axis=1)[None]
    w_tail = jnp.concatenate(gathered_cols('tail') + [jnp.zeros((d, TAIL - ROPE - 2 * ML_HEADS), bf16)], axis=1)[None]
    assert w_main.shape[2] == wmain

    inv_freq = ROPE_BASE ** (-jnp.arange(0, ROPE, 2, dtype=f32) / ROPE)
    inv_tile = _pad_lanes(jnp.concatenate([inv_freq, inv_freq])[None])
    cos, sin = rope_tables(positions.reshape(t, 1), inv_tile)
    gqp, gkp = _pad_lanes(g_qn_pe), _pad_lanes(g_kn_pe)
    b_tile = jnp.pad(b_if, ((0, 0), (T_I, TAIL - T_I - 2 * ML_HEADS)))

    u0 = rms_fwd(x2d, g_mix, "rms_mix")
    z_main, got = mm_nn(u0, w_main, f32, "mm_in_main", exchange=([shards[n] for n in behind_in], False))
    gw.update(zip(behind_in, got))
    qb = gw['w_qb'].transpose(1, 0, 2).reshape(Q_LORA, MLA_HEADS, NOPE + ROPE)
    w_qb_p = jnp.concatenate([qb, jnp.zeros((Q_LORA, MLA_HEADS, HEAD_PAD - NOPE - ROPE), bf16)], axis=2).reshape(1, Q_LORA, -1)
    w_kvb3 = gw['w_kvb']
    p_a3, p_b3, w_out3 = (gw[n].reshape(1, -1, d) for n in ('p_a', 'p_b', 'w_out'))
    wq_c3, wk_c3, wv_c3 = (gw[n].reshape(1, d, -1) for n in ('wq_c', 'wk_c', 'wv_c'))
    wo_c3 = gw['wo_c']
    z_tail = mm_nn(u0, w_tail, f32, "mm_in_tail")
    qa_n, kv_n = lat_norm(z_main, g_qa, g_kva)
    q_raw = mm_nn(qa_n, w_qb_p, f32, "mm_qb")
    kv_raw = mm_nn(kv_n, w_kvb3, f32, "mm_kvb")
    qh, kh, vh = mla_prep(q_raw, kv_raw, z_tail, cos, sin, g_qn_nope, gqp, g_kn_nope, gkp)
    (o_a, o_ab, lse), (w_up3,) = mla_fwd(qh, kh, vh, exchange=([shards['w_up']], False))

    qk_act = qk_conv(z_main, conv_qk_f)
    gates = gate_act(z_tail, b_tile)

    def to_rows(cols):
        return cols.T.reshape(ML_HEADS, nc, 1, CHUNK)

    ig, fg = to_rows(gates[:, T_I:T_F]), to_rows(gates[:, T_F:T_F + ML_HEADS])
    h_ml, c_all, n_all, m_all = mlstm_fwd(qk_act, z_main, ig, fg)
    g_hn3 = g_hnorm_f.reshape(ML_HEADS, 1, ML_DV)
    y_b = mlstm_out(h_ml, z_main, g_hn3)

    ya = mm_nn(o_ab, p_a3, f32, "mm_pa")
    yb = mm_nn(y_b, p_b3, f32, "mm_pb")
    merged = merge_fwd(z_main, ya, yb)
    mo = mm_nn(merged, w_out3, f32, "mm_out")
    x1, uc = resid_rms(x2d, mo, g_cross, "resid_cross")
    mem_n = rms_fwd(mem2d, g_mem, "rms_mem")
    cq = mm_nn(uc, wq_c3, f32, "mm_cq")
    ck = mm_nn(mem_n, wk_c3, f32, "mm_ck")
    cv = mm_nn(mem_n, wv_c3, f32, "mm_cv")
    o_c = cross_fwd(cq, ck, cv, g_cq, g_ck)
    co = mm_nn(o_c, wo_c3, f32, "mm_oc")
    x2, u3 = resid_rms(x1, co, g_ffn, "resid_ffn")
    hup, (w_down_g,) = mm_nn(u3, w_up3, f32, "mm_up", exchange=([shards['w_down']], False))
    w_down3 = w_down_g.reshape(1, -1, d)
    gl = glu_fwd(hup, conv_ffn_f, b_conv_ffn)
    fo = mm_nn(gl, w_down3, f32, "mm_down")
    dx3, dx3_b, loss_acc = loss_head(x2, fo, tgt)

    grads, parts = {}, {}
    grads['w_down'] = mm_tn(gl, dx3_b, 1, "mm_d_wdown").reshape(N_DEV, -1, d)
    dgl = mm_nt(dx3_b, w_down3, f32, "mm_d_gl")
    (dh1, dh2, dcw1, dcw2, db1, db2), (parts['w_down'],) = glu_bwd(hup, conv_ffn_f, b_conv_ffn, dgl,
                                                                    exchange=([grads['w_down']], True))
    dconv_ffn, db_ffn = (jnp.concatenate(pair, axis=1) for pair in ((dcw1, dcw2), (db1, db2)))
    grads['w_up'] = mm_tn_cols(u3, [dh1, dh2], N_DEV, "mm_d_wup")
    du3 = mm_nt_cols([dh1, dh2], w_up3, f32, "mm_d_u3")
    dx2, dx2_b, dg_ffn = rms_bwd(x2, g_ffn, [du3], dx3, "rms_bwd_ffn", want_b16=True)
    grads['wo_c'] = mm_tn(o_c, dx2_b, N_DEV, "mm_d_woc")
    do_c = mm_nt(dx2_b, wo_c3, f32, "mm_d_oc")
    dcq, dck, dcv, dg_cq, dg_ck = cross_bwd(cq, ck, cv, g_cq, g_ck, do_c)
    grads['wq_c'] = mm_tn(uc, dcq, 1, "mm_d_wqc").reshape(N_DEV, -1, dcq.shape[1])
    grads['wk_c'] = mm_tn(mem_n, dck, 1, "mm_d_wkc").reshape(N_DEV, -1, dck.shape[1])
    grads['wv_c'] = mm_tn(mem_n, dcv, 1, "mm_d_wvc").reshape(N_DEV, -1, dcv.shape[1])
    duc = mm_nt(dcq, wq_c3, f32, "mm_d_uc")
    dmem_k = mm_nt(dck, wk_c3, f32, "mm_d_memk")
    dmem_v = mm_nt(dcv, wv_c3, f32, "mm_d_memv")
    dg_mem, = rms_bwd(mem2d, g_mem, [dmem_k, dmem_v], None, "rms_bwd_mem", want_dx=False)
    dx1, dx1_b, dg_cross = rms_bwd(x1, g_cross, [duc], dx2, "rms_bwd_cross", want_b16=True)
    grads['w_out'] = mm_tn(merged, dx1_b, 1, "mm_d_wout").reshape(N_DEV, -1, d)
    dmerged = mm_nt(dx1_b, w_out3, f32, "mm_d_merged")
    dga, dgb, dya, dyb = merge_bwd(z_main, ya, yb, dmerged)
    grads['p_a'] = mm_tn(o_ab, dya, 1, "mm_d_pa").reshape(N_DEV, -1, d)
    grads['p_b'] = mm_tn(y_b, dyb, 1, "mm_d_pb").reshape(N_DEV, -1, d)
    do_a = mm_nt(dya, p_a3, f32, "mm_d_oa")
    dy_b = mm_nt(dyb, p_b3, f32, "mm_d_yb")

    dh_ml, dzo, dg_hn = mlstm_out_bwd(h_ml, z_main, g_hn3, dy_b)
    mixers = ['p_a', 'p_b', 'w_out']
    (dq_act, dk_act, dzv, dig, dfg), got = mlstm_bwd(qk_act, z_main, ig, fg, c_all, n_all, m_all, dh_ml,
                                                     exchange=([grads[n] for n in mixers], True))
    parts.update(zip(mixers, got))
    dzqk, dconv_qk = qk_conv_bwd(z_main, conv_qk_f, dq_act, dk_act)

    delta = mla_delta(o_a, do_a)
    (dqh, dkh, dvh), (parts['w_up'],) = mla_bwd(qh, kh, vh, do_a, lse, delta.reshape(MLA_HEADS, 1, t),
                                                exchange=([grads['w_up']], True))
    dq_raw, dkv_raw, dzt_pe, dg_qn, dg_qp, dg_kn, dg_kp = mla_prep_bwd(
        q_raw, kv_raw, z_tail, cos, sin, g_qn_nope, gqp, g_kn_nope, gkp, dqh, dkh, dvh)
    d_wqb_p = mm_tn(qa_n, dq_raw, 1, "mm_d_wqb")[0].reshape(Q_LORA, MLA_HEADS, HEAD_PAD)[:, :, :NOPE + ROPE]
    grads['w_qb'] = d_wqb_p.reshape(Q_LORA, N_DEV, -1).transpose(1, 0, 2)
    grads['w_kvb'] = mm_tn(kv_n, dkv_raw, N_DEV, "mm_d_wkvb")
    dqa = mm_nt(dq_raw, w_qb_p, f32, "mm_d_qa")
    dkvn = mm_nt(dkv_raw, w_kvb3, f32, "mm_d_kvn")
    dz_lat, dg_qa, dg_kva = lat_norm_bwd(z_main, g_qa, g_kva, dqa, dkvn)

    def to_cols(rows):
        return rows.reshape(ML_HEADS, t).T

    dgate = jnp.pad(jnp.concatenate([to_cols(dig), to_cols(dfg)], axis=1), ((0, 0), (T_I, TAIL - T_I - 2 * ML_HEADS)))
    dz_tail, db_if = tail_bwd(z_tail, b_tile, dzt_pe, dgate)
    dz_main = [dz_lat, dzqk, dzv, dzo, dga, dgb]
    small_mats = ['wq_c', 'wk_c', 'wv_c', 'wo_c', 'w_qb', 'w_kvb']
    d_wmain3, got = mm_tn_cols(u0, dz_main, 1, "mm_d_wmain", exchange=([grads[n] for n in small_mats], True))
    parts.update(zip(small_mats, got))
    d_wmain = d_wmain3[0]
    d_wtail = mm_tn(u0, dz_tail, 1, "mm_d_wtail")[0]
    d_target = {'main': d_wmain, 'tail': d_wtail}
    blocks = []
    for j in range(N_DEV):
        lo_j, hi_j = j * shard_w, (j + 1) * shard_w
        cols = [d_target[tg][:, off + max(lo, lo_j) - lo:off + min(hi, hi_j) - lo]
                for lo, hi, tg, off in segments if lo < hi_j and hi > lo_j]
        blocks.append(jnp.concatenate(cols, axis=1))
    grads['w_in'] = jnp.stack(blocks)
    du0_a, (parts['w_in'],) = mm_nt_cols(dz_main, w_main, f32, "mm_d_u0_main", exchange=([grads['w_in']], True))
    du0_b = mm_nt(dz_tail, w_tail, f32, "mm_d_u0_tail")
    grad_x, dg_mix = rms_bwd(x2d, g_mix, [du0_a, du0_b], dx1, "rms_bwd_mix")

    out_g, out_d, out_m, out_v = {}, {}, {}, {}
    for n in big:
        res = adam_sum(parts[n], args[n], args['m_' + n], args['v_' + n], "adam_" + n)
        out_g[n], out_d[n], out_m[n], out_v[n] = (a.reshape(args[n].shape) for a in res)

    small_full = {
        'g_mix': dg_mix[0], 'g_qa': dg_qa[0], 'g_kva': dg_kva[0], 'g_qn_nope': dg_qn[0], 'g_qn_pe': dg_qp[0, :ROPE],
        'g_kn_nope': dg_kn[0], 'g_kn_pe': dg_kp[0, :ROPE], 'conv_qk': dconv_qk, 'b_if': db_if[0, T_I:T_I + 2 * ML_HEADS],
        'g_hnorm': dg_hn[:, 0, :], 'g_cross': dg_cross[0], 'g_mem': dg_mem[0], 'g_cq': dg_cq[0], 'g_ck': dg_ck[0],
        'g_ffn': dg_ffn[0], 'conv_ffn': dconv_ffn, 'b_conv_ffn': db_ffn[0], 'loss': loss_acc[0, :1]}
    order = list(small_full)
    packed = _pack([small_full[n] for n in order], 8 * 128).reshape(1, -1)
    gathered_small, = _exchange([packed], "comm_gather_small", scatter=False)
    summed = sum_parts(gathered_small.reshape(N_DEV, -1, 128), "sum_small").reshape(-1)
    full_g = dict(zip(order, _unpack(summed, [small_full[n].shape for n in order])))
    loss = full_g['loss'][0]

    local_g = {}
    for n in replicated:
        local_g[n] = full_g[n].reshape(args[n].shape)
    for n in sharded_small:
        shp = args[n].shape
        full = full_g[n].reshape((1,) + full_g[n].shape)
        local_g[n] = lax.dynamic_slice_in_dim(full, me * shp[-1], shp[-1], axis=2)
    small = replicated + sharded_small
    dl_f, m_f, v_f = adam_flat(*[_pack([src[n] if pre == '' else args[pre + n] for n in small], 8 * 128).reshape(-1, 128)
                                 for pre, src in (('', args), ('', local_g), ('m_', None), ('v_', None))], "adam_small")
    shapes = [args[n].shape for n in small]
    for dst, flat in ((out_d, dl_f), (out_m, m_f), (out_v, v_f)):
        dst.update(zip(small, _unpack(flat.reshape(-1), shapes)))
    out_g.update(local_g)

    return (loss, grad_x[None], *[out_g[n] for n in names], *[out_d[n] for n in names],
            *[out_m[n] for n in names], *[out_v[n] for n in names])
```

```python
import functools

import jax
import jax.numpy as jnp
from jax import lax
from jax.experimental import pallas as pl
from jax.experimental.pallas import tpu as pltpu

f32 = jnp.float32
bf16 = jnp.bfloat16

N_DEV = 8
EPS = 1e-6
CHUNK = 64
CHUNK_SHIFT = 6
assert 1 << CHUNK_SHIFT == CHUNK
MLA_HEADS = 16
Q_LORA = 512
KV_LORA = 512
NOPE = 128
ROPE = 64
V_HEAD = 128
ROPE_BASE = 10000.0
HEAD_PAD = 256
ML_HEADS = 8
ML_DK = 128
ML_DV = 256
ML_CONV = 4
ML_QK = ML_HEADS * ML_DK
ML_V = ML_HEADS * ML_DV
CR_HEADS = 4
CR_HD = 128
FFN_CONV = 3
ADAM_LR = 0.001
ADAM_B1 = 0.9
ADAM_B2 = 0.999
ADAM_EPS = 1e-08
ADAM_WD = 0.01
ADAM_STEP = 10
O_QA, O_KV, O_Q, O_K = 0, Q_LORA, Q_LORA + KV_LORA, Q_LORA + KV_LORA + ML_QK
O_V = O_K + ML_QK
O_O = O_V + ML_V
O_GA = O_O + ML_V
TAIL = 128
T_I, T_F = ROPE, ROPE + ML_HEADS
VMEM_LIMIT_V7X = 48 * 1024 * 1024
MESH = pl.DeviceIdType.MESH


def _call(body, name, grid, in_specs, out_specs, out_shape, scratch=(), exchange=None):
    params = pltpu.CompilerParams(vmem_limit_bytes=VMEM_LIMIT_V7X)
    if exchange is None:
        return pl.pallas_call(body, name=name, grid=grid, in_specs=in_specs, out_specs=out_specs, out_shape=out_shape,
                              scratch_shapes=list(scratch), compiler_params=params)
    arrs, scatter = exchange
    single = not isinstance(out_specs, (list, tuple))
    o_specs = [out_specs] if single else list(out_specs)
    o_shape = [out_shape] if single else list(out_shape)
    n_in, n_out, n_sc, n = len(in_specs), len(o_specs), len(scratch), len(arrs)
    any_spec = pl.BlockSpec(memory_space=pl.ANY)

    def body_with_exchange(*refs):
        pos = [0]

        def take(k):
            pos[0] += k
            return refs[pos[0] - k:pos[0]]

        ins, ex_in, outs, ex_out, sc = take(n_in), take(n), take(n_out), take(n), take(n_sc)
        start, middle, wait = _exchange_ops(ex_in, ex_out, refs[pos[0]:], scatter)
        step, total = 0, 1
        for a in range(len(grid)):
            step = step * grid[a] + pl.program_id(a)
            total *= grid[a]
        pl.when(step == 0)(start)
        body(*ins, *outs, *sc)
        if middle is not None:
            pl.when(step == total // 2)(middle)
        pl.when(step == total - 1)(wait)

    call = pl.pallas_call(body_with_exchange, name="comm_" + name, grid=grid, in_specs=list(in_specs) + [any_spec] * n,
                          out_specs=o_specs + [any_spec] * n, out_shape=o_shape + _exchange_shapes(arrs, scatter),
                          scratch_shapes=list(scratch) + _exchange_sems(n), compiler_params=params)

    def run(*operands):
        res = call(*operands, *arrs)
        return (res[0] if single else list(res[:n_out])), list(res[n_out:])

    return run


def _tile(n, cands):
    for c in cands:
        if n % c == 0:
            return c
    return n


def _sds(shape, dtype):
    return jax.ShapeDtypeStruct(tuple(shape), dtype)


def _bdot(a, b, ca, cb):
    return lax.dot_general(a.astype(bf16), b.astype(bf16), (((ca,), (cb,)), ((), ())), preferred_element_type=f32)


_BIG = (1024, 512, 256, 128)


def _col_tile(nb):
    return nb if nb <= 1536 else _tile(nb, _BIG)


_DEEP = (2048, 1024, 512, 256, 128)


def _mm_call(name, grid, in_specs, out_spec, out_shape, tile, nk, ca, cb, exchange, operands):
    def dot(a_ref, w_ref):
        return _bdot(a_ref[...], w_ref[0] if len(w_ref.shape) == 3 else w_ref[...], ca, cb)

    def store(o_ref, val):
        if len(o_ref.shape) == 3:
            o_ref[0] = val.astype(o_ref.dtype)
        else:
            o_ref[...] = val.astype(o_ref.dtype)

    if nk == 1:
        def body(a_ref, w_ref, o_ref):
            store(o_ref, dot(a_ref, w_ref))

        scratch = []
    else:
        def body(a_ref, w_ref, o_ref, acc):
            kk = pl.program_id(2)

            @pl.when(kk == 0)
            def _():
                acc[...] = jnp.zeros_like(acc)

            acc[...] += dot(a_ref, w_ref)

            @pl.when(kk == nk - 1)
            def _():
                store(o_ref, acc[...])

        scratch = [pltpu.VMEM(tile, f32)]
    return _call(body, name, grid, in_specs, out_spec, out_shape, scratch, exchange=exchange)(*operands)


def mm_nn(a, w3, out_dtype, name, exchange=None):
    m, k = a.shape
    nblk, k2, nb = w3.shape
    assert k == k2
    tm, tk, tn = _tile(m, _BIG), _tile(k, _DEEP), _col_tile(nb)
    per, nk = nb // tn, k // tk
    return _mm_call(name, (m // tm, nblk * per, nk),
                    [pl.BlockSpec((tm, tk), lambda i, j, kk: (i, kk)),
                     pl.BlockSpec((1, tk, tn), lambda i, j, kk: (j // per, kk, j % per))],
                    pl.BlockSpec((tm, tn), lambda i, j, kk: (i, j)), _sds((m, nblk * nb), out_dtype),
                    (tm, tn), nk, 1, 0, exchange, (a, w3))


def mm_nt(a, w3, out_dtype, name, exchange=None):
    m, n = a.shape
    nblk, k, nb = w3.shape
    assert n == nblk * nb
    tm, tn = _tile(m, _BIG), _tile(k, _BIG)
    tc = nb if nb <= 1536 else _tile(nb, _DEEP)
    per = nb // tc
    nk = nblk * per
    return _mm_call(name, (m // tm, k // tn, nk),
                    [pl.BlockSpec((tm, tc), lambda i, j, kk: (i, kk)),
                     pl.BlockSpec((1, tn, tc), lambda i, j, kk: (kk // per, j, kk % per))],
                    pl.BlockSpec((tm, tn), lambda i, j, kk: (i, j)), _sds((m, k), out_dtype),
                    (tm, tn), nk, 1, 1, exchange, (a, w3))


def mm_tn(a, b, nblk, name, exchange=None):
    r, m = a.shape
    r2, n = b.shape
    assert r == r2 and n % nblk == 0
    nb = n // nblk
    tm, tk, tn = _tile(m, _BIG), _tile(r, _DEEP), _col_tile(nb)
    per, nk = nb // tn, r // tk
    return _mm_call(name, (m // tm, nblk * per, nk),
                    [pl.BlockSpec((tk, tm), lambda i, j, kk: (kk, i)),
                     pl.BlockSpec((tk, tn), lambda i, j, kk: (kk, j))],
                    pl.BlockSpec((1, tm, tn), lambda i, j, kk: (j // per, i, j % per)), _sds((nblk, m, nb), bf16),
                    (tm, tn), nk, 0, 0, exchange, (a, b))


def _section_tiles(sections, cands):
    widths = [s.shape[1] for s in sections]
    tile = next(c for c in cands if all(w % c == 0 for w in widths))
    counts = [w // tile for w in widths]
    firsts = [sum(counts[:i]) for i in range(len(counts))]
    return tile, firsts, counts


SECTION_VMEM_BYTES = 24 * 1024 * 1024


def mm_tn_cols(a, sections, nblk, name, exchange=None):
    r, m = a.shape
    n = sum(s.shape[1] for s in sections)
    nb = n // nblk
    tn, firsts, counts = _section_tiles(sections, (nb,) if nb <= 1536 else _BIG)
    per = nb // tn
    tm = _tile(m, _BIG)
    tk = next(c for c in _DEEP if r % c == 0 and len(sections) * c * tn * 4 <= SECTION_VMEM_BYTES)
    nk = r // tk

    def body(a_ref, *refs):
        b_refs, o_ref, acc = refs[:len(sections)], refs[len(sections)], refs[len(sections) + 1]
        j, kk = pl.program_id(1), pl.program_id(2)

        @pl.when(kk == 0)
        def _():
            acc[...] = jnp.zeros_like(acc)

        for b_ref, lo, cnt in zip(b_refs, firsts, counts):
            @pl.when(jnp.logical_and(j >= lo, j < lo + cnt))
            def _(b_ref=b_ref):
                acc[...] += _bdot(a_ref[...], b_ref[...], 0, 0)

        @pl.when(kk == nk - 1)
        def _():
            o_ref[0] = acc[...].astype(bf16)

    def spec(lo, cnt):
        def index(i, j, kk):
            return jnp.where(j < lo, 0, jnp.where(j >= lo + cnt, nk - 1, kk)), jnp.clip(j - lo, 0, cnt - 1)
        return pl.BlockSpec((tk, tn), index)

    return _call(body, name, (m // tm, n // tn, nk),
                 [pl.BlockSpec((tk, tm), lambda i, j, kk: (kk, i))] + [spec(lo, cnt) for lo, cnt in zip(firsts, counts)],
                 pl.BlockSpec((1, tm, tn), lambda i, j, kk: (j // per, i, j % per)), _sds((nblk, m, nb), bf16),
                 [pltpu.VMEM((tm, tn), f32)], exchange=exchange)(a, *sections)


def mm_nt_cols(sections, w3, out_dtype, name, exchange=None):
    m = sections[0].shape[0]
    nblk, k, nb = w3.shape
    tc, firsts, counts = _section_tiles(sections, (nb,) if nb <= 1536 else _DEEP)
    assert nblk * nb == tc * sum(counts) and nb % tc == 0
    per = nb // tc
    tm, tn = _tile(m, _BIG), _tile(k, _BIG)
    nk = nblk * per

    def body(*refs):
        a_refs, w_ref, o_ref, acc = refs[:len(sections)], refs[len(sections)], refs[len(sections) + 1], refs[len(sections) + 2]
        kk = pl.program_id(2)

        @pl.when(kk == 0)
        def _():
            acc[...] = jnp.zeros_like(acc)

        for a_ref, lo, cnt in zip(a_refs, firsts, counts):
            @pl.when(jnp.logical_and(kk >= lo, kk < lo + cnt))
            def _(a_ref=a_ref):
                acc[...] += _bdot(a_ref[...], w_ref[0], 1, 1)

        @pl.when(kk == nk - 1)
        def _():
            o_ref[...] = acc[...].astype(o_ref.dtype)

    def spec(lo, cnt):
        return pl.BlockSpec((tm, tc), lambda i, j, kk: (i, jnp.clip(kk - lo, 0, cnt - 1)))

    return _call(body, name, (m // tm, k // tn, nk),
                 [spec(lo, cnt) for lo, cnt in zip(firsts, counts)] + [pl.BlockSpec((1, tn, tc), lambda i, j, kk: (kk // per, j, kk % per))],
                 pl.BlockSpec((tm, tn), lambda i, j, kk: (i, j)), _sds((m, k), out_dtype),
                 [pltpu.VMEM((tm, tn), f32)], exchange=exchange)(*sections, w3)


def _rms(x, g):
    return x * lax.rsqrt(jnp.mean(x * x, axis=-1, keepdims=True) + EPS) * g


def _rms_pad(x, g, width):
    return x * lax.rsqrt(jnp.sum(x * x, axis=-1, keepdims=True) / width + EPS) * g


def _first(*ids):
    ok = ids[0] == 0
    for i in ids[1:]:
        ok = jnp.logical_and(ok, i == 0)
    return ok


def _acc_row(ref, val, first):
    @pl.when(first)
    def _():
        ref[...] = jnp.zeros_like(ref)

    ref[0:1, :] += val


def rms_fwd(x, g, name):
    r, w = x.shape
    tm = _tile(r, (256, 128, 64, 32, 16, 8))

    def body(x_ref, g_ref, o_ref):
        o_ref[...] = _rms(x_ref[...], g_ref[...]).astype(bf16)

    return _call(body, name, (r // tm,), [pl.BlockSpec((tm, w), lambda i: (i, 0)), pl.BlockSpec((1, w), lambda i: (0, 0))],
                 pl.BlockSpec((tm, w), lambda i: (i, 0)), _sds((r, w), bf16))(x, g)


def resid_rms(xa, xb, g, name):
    r, w = xa.shape
    tm = _tile(r, (256, 128, 64, 32, 16, 8))

    def body(a_ref, b_ref, g_ref, s_ref, u_ref):
        xs = a_ref[...] + b_ref[...]
        s_ref[...] = xs
        u_ref[...] = _rms(xs, g_ref[...]).astype(bf16)

    row = pl.BlockSpec((tm, w), lambda i: (i, 0))
    return _call(body, name, (r // tm,), [row, row, pl.BlockSpec((1, w), lambda i: (0, 0))], [row, row],
                 [_sds((r, w), f32), _sds((r, w), bf16)])(xa, xb, g)


def rms_bwd(x, g, dys, dres, name, want_dx=True, want_b16=False):
    r, w = x.shape
    tm = _tile(r, (256, 128, 64, 32, 16, 8))
    nd = len(dys)

    def body(*refs):
        x_ref, g_ref = refs[0], refs[1]
        dy = refs[2][...]
        for j in range(1, nd):
            dy = dy + refs[2 + j][...]
        pos = 2 + nd
        _, vjp = jax.vjp(_rms, x_ref[...], g_ref[...])
        dx, dg = vjp(dy)
        if dres is not None:
            dx = dx + refs[pos][...]
            pos += 1
        if want_dx:
            refs[pos][...] = dx
            pos += 1
        if want_b16:
            refs[pos][...] = dx.astype(bf16)
            pos += 1
        _acc_row(refs[pos], dg, pl.program_id(0) == 0)

    row = pl.BlockSpec((tm, w), lambda i: (i, 0))
    ins = [x, g] + list(dys) + ([dres] if dres is not None else [])
    in_specs = [row, pl.BlockSpec((1, w), lambda i: (0, 0))] + [row] * (nd + (dres is not None))
    out_specs = [row] * (want_dx + want_b16) + [pl.BlockSpec((8, w), lambda i: (0, 0))]
    out_shape = ([_sds((r, w), f32)] if want_dx else []) + ([_sds((r, w), bf16)] if want_b16 else []) + [_sds((8, w), f32)]
    return _call(body, name, (r // tm,), in_specs, out_specs, out_shape)(*ins)


def lat_norm(z_main, g_qa, g_kva):
    t = z_main.shape[0]
    tm = _tile(t, (512, 256, 128, 64))

    def body(z_ref, gq_ref, gk_ref, q_ref, k_ref):
        q_ref[...] = _rms(z_ref[:, :Q_LORA], gq_ref[...]).astype(bf16)
        k_ref[...] = _rms(z_ref[:, Q_LORA:], gk_ref[...]).astype(bf16)

    return _call(body, "lat_norm", (t // tm,),
                 [pl.BlockSpec((tm, Q_LORA + KV_LORA), lambda i: (i, 0)), pl.BlockSpec((1, Q_LORA), lambda i: (0, 0)),
                  pl.BlockSpec((1, KV_LORA), lambda i: (0, 0))],
                 [pl.BlockSpec((tm, Q_LORA), lambda i: (i, 0)), pl.BlockSpec((tm, KV_LORA), lambda i: (i, 0))],
                 [_sds((t, Q_LORA), bf16), _sds((t, KV_LORA), bf16)])(z_main, g_qa, g_kva)


def lat_norm_bwd(z_main, g_qa, g_kva, dqa, dkv):
    t = z_main.shape[0]
    tm = _tile(t, (512, 256, 128, 64))

    def body(z_ref, gq_ref, gk_ref, dq_ref, dk_ref, dz_ref, dgq_ref, dgk_ref):
        first = pl.program_id(0) == 0
        _, vq = jax.vjp(_rms, z_ref[:, :Q_LORA], gq_ref[...])
        dx, dg = vq(dq_ref[...])
        dz_ref[:, :Q_LORA] = dx.astype(bf16)
        _acc_row(dgq_ref, dg, first)
        _, vk = jax.vjp(_rms, z_ref[:, Q_LORA:], gk_ref[...])
        dx, dg = vk(dk_ref[...])
        dz_ref[:, Q_LORA:] = dx.astype(bf16)
        _acc_row(dgk_ref, dg, first)

    return _call(body, "lat_norm_bwd", (t // tm,),
                 [pl.BlockSpec((tm, Q_LORA + KV_LORA), lambda i: (i, 0)), pl.BlockSpec((1, Q_LORA), lambda i: (0, 0)),
                  pl.BlockSpec((1, KV_LORA), lambda i: (0, 0)), pl.BlockSpec((tm, Q_LORA), lambda i: (i, 0)),
                  pl.BlockSpec((tm, KV_LORA), lambda i: (i, 0))],
                 [pl.BlockSpec((tm, Q_LORA + KV_LORA), lambda i: (i, 0)), pl.BlockSpec((8, Q_LORA), lambda i: (0, 0)),
                  pl.BlockSpec((8, KV_LORA), lambda i: (0, 0))],
                 [_sds((t, Q_LORA + KV_LORA), bf16), _sds((8, Q_LORA), f32), _sds((8, KV_LORA), f32)])(z_main, g_qa, g_kva, dqa, dkv)


def rope_tables(pos_col, inv_freq):
    t = pos_col.shape[0]
    tm = _tile(t, (512, 256, 128, 64))

    def body(p_ref, f_ref, c_ref, s_ref):
        ang = p_ref[...].astype(f32) * f_ref[...]
        lane = lax.broadcasted_iota(jnp.int32, ang.shape, 1)
        c_ref[...] = jnp.where(lane < ROPE, jnp.cos(ang), 0.0)
        sn = jnp.sin(ang)
        s_ref[...] = jnp.where(lane < ROPE // 2, -sn, jnp.where(lane < ROPE, sn, 0.0))

    return _call(body, "rope_tables", (t // tm,),
                 [pl.BlockSpec((tm, 1), lambda i: (i, 0)), pl.BlockSpec((1, TAIL), lambda i: (0, 0))],
                 [pl.BlockSpec((tm, TAIL), lambda i: (i, 0))] * 2, [_sds((t, TAIL), f32)] * 2)(pos_col, inv_freq)


def _swap_halves(n):
    lane = lax.broadcasted_iota(jnp.int32, n.shape, 1)
    return jnp.where(lane < ROPE // 2, pltpu.roll(n, TAIL - ROPE // 2, 1), pltpu.roll(n, ROPE // 2, 1))


def _rope(n, c, s):
    return n * c + _swap_halves(n) * s


def _rope_t(d, c, s):
    return d * c + _swap_halves(d * s)


def _prep_specs(tm):
    head = pl.BlockSpec((tm, HEAD_PAD), lambda i, h: (i, h))
    row = pl.BlockSpec((tm, TAIL), lambda i, h: (i, 0))
    gain = pl.BlockSpec((1, TAIL), lambda i, h: (0, 0))
    return head, row, gain


def _pe_in(zt):
    lane = lax.broadcasted_iota(jnp.int32, zt.shape, 1)
    return jnp.where(lane < ROPE, zt, 0.0)


def mla_prep(q_raw, kv_raw, z_tail, cos, sin, gqn, gqp, gkn, gkp):
    t = q_raw.shape[0]
    tm = _tile(t, (1024, 512, 256, 128, 64))

    def body(q_ref, kv_ref, zt_ref, c_ref, s_ref, gqn_ref, gqp_ref, gkn_ref, gkp_ref, qh_ref, kh_ref, vh_ref):
        c, s = c_ref[...], s_ref[...]
        qh_ref[:, :NOPE] = _rms(q_ref[:, :NOPE], gqn_ref[...]).astype(bf16)
        qh_ref[:, NOPE:] = _rope(_rms_pad(q_ref[:, NOPE:], gqp_ref[...], ROPE), c, s).astype(bf16)
        kh_ref[:, :NOPE] = _rms(kv_ref[:, :NOPE], gkn_ref[...]).astype(bf16)
        kh_ref[:, NOPE:] = _rope(_rms_pad(_pe_in(zt_ref[...]), gkp_ref[...], ROPE), c, s).astype(bf16)
        vh_ref[...] = kv_ref[:, NOPE:].astype(bf16)

    head, row, gain = _prep_specs(tm)
    return _call(body, "mla_prep", (t // tm, MLA_HEADS), [head, head, row, row, row, gain, gain, gain, gain],
                 [head, head, pl.BlockSpec((tm, V_HEAD), lambda i, h: (i, h))],
                 [_sds((t, MLA_HEADS * HEAD_PAD), bf16), _sds((t, MLA_HEADS * HEAD_PAD), bf16), _sds((t, MLA_HEADS * V_HEAD), bf16)],
                 )(q_raw, kv_raw, z_tail, cos, sin, gqn, gqp, gkn, gkp)


def mla_prep_bwd(q_raw, kv_raw, z_tail, cos, sin, gqn, gqp, gkn, gkp, dqh, dkh, dvh):
    t = q_raw.shape[0]
    tm = _tile(t, (1024, 512, 256, 128, 64))
    pad_norm = functools.partial(_rms_pad, width=ROPE)

    def body(q_ref, kv_ref, zt_ref, c_ref, s_ref, gqn_ref, gqp_ref, gkn_ref, gkp_ref, dqh_ref, dkh_ref, dvh_ref,
             dq_ref, dkv_ref, dzt_ref, dgqn_ref, dgqp_ref, dgkn_ref, dgkp_ref):
        i, h = pl.program_id(0), pl.program_id(1)
        first = _first(i, h)
        c, s = c_ref[...], s_ref[...]
        _, v1 = jax.vjp(_rms, q_ref[:, :NOPE], gqn_ref[...])
        dx, dg = v1(dqh_ref[:, :NOPE])
        dq_ref[:, :NOPE] = dx.astype(bf16)
        _acc_row(dgqn_ref, dg, first)
        _, v2 = jax.vjp(pad_norm, q_ref[:, NOPE:], gqp_ref[...])
        dx, dg = v2(_rope_t(dqh_ref[:, NOPE:], c, s))
        dq_ref[:, NOPE:] = dx.astype(bf16)
        _acc_row(dgqp_ref, dg, first)
        _, v3 = jax.vjp(_rms, kv_ref[:, :NOPE], gkn_ref[...])
        dx, dg = v3(dkh_ref[:, :NOPE])
        dkv_ref[:, :NOPE] = dx.astype(bf16)
        _acc_row(dgkn_ref, dg, first)
        dkv_ref[:, NOPE:] = dvh_ref[...].astype(bf16)
        _, v4 = jax.vjp(pad_norm, _pe_in(zt_ref[...]), gkp_ref[...])
        dx, dg = v4(_rope_t(dkh_ref[:, NOPE:], c, s))
        _acc_row(dgkp_ref, dg, first)

        @pl.when(h == 0)
        def _():
            dzt_ref[...] = jnp.zeros_like(dzt_ref)

        dzt_ref[...] += dx

    head, row, gain = _prep_specs(tm)
    acc = pl.BlockSpec((8, TAIL), lambda i, h: (0, 0))
    vspec = pl.BlockSpec((tm, V_HEAD), lambda i, h: (i, h))
    return _call(body, "mla_prep_bwd", (t // tm, MLA_HEADS),
                 [head, head, row, row, row, gain, gain, gain, gain, head, head, vspec],
                 [head, head, row, acc, acc, acc, acc],
                 [_sds((t, MLA_HEADS * HEAD_PAD), bf16), _sds((t, MLA_HEADS * HEAD_PAD), bf16), _sds((t, TAIL), f32)]
                 + [_sds((8, TAIL), f32)] * 4)(q_raw, kv_raw, z_tail, cos, sin, gqn, gqp, gkn, gkp, dqh, dkh, dvh)


ATT_BLOCK = 512
NEG = -1e30
ATT_SCALE = (NOPE + ROPE) ** -0.5
LOG2_E = 1.4426950408889634
ATT_SCALE2 = ATT_SCALE * LOG2_E
ATT_HEADS = 2
ATT_HEADS_FWD = 4


def _chunk_visible(shape, key_axis):
    kc = lax.broadcasted_iota(jnp.int32, shape, key_axis) >> CHUNK_SHIFT
    qc = lax.broadcasted_iota(jnp.int32, shape, 1 - key_axis) >> CHUNK_SHIFT
    return kc <= qc


def mla_fwd(qh, kh, vh, exchange=None):
    t = qh.shape[0]
    tb = min(ATT_BLOCK, t)
    nb = t // tb

    hp = ATT_HEADS_FWD

    def body(q_ref, k_ref, v_ref, o_ref, ob_ref, lse_ref, m_s, l_s, acc):
        qi, ki = pl.program_id(1), pl.program_id(2)

        @pl.when(ki == 0)
        def _():
            m_s[...] = jnp.full_like(m_s, NEG)
            l_s[...] = jnp.zeros_like(l_s)
            acc[...] = jnp.zeros_like(acc)

        def step(diagonal):
            new = []
            for j in range(hp):
                q, k = q_ref[:, j * HEAD_PAD:(j + 1) * HEAD_PAD], k_ref[:, j * HEAD_PAD:(j + 1) * HEAD_PAD]
                s = _bdot(k, q, 1, 1) * ATT_SCALE2
                if diagonal:
                    s = jnp.where(_chunk_visible(s.shape, 0), s, -jnp.inf)
                m_old = m_s[j]
                m_new = jnp.maximum(m_old, jnp.max(s, axis=0, keepdims=True))
                p = jnp.exp2(s - m_new)
                alpha = jnp.exp2(m_old - m_new)
                l_new = alpha * l_s[j] + jnp.sum(p, axis=0, keepdims=True)
                acc_new = alpha * acc[j] + _bdot(v_ref[:, j * V_HEAD:(j + 1) * V_HEAD], p, 0, 0)
                new.append((m_new, l_new, acc_new))
            for j, (m_new, l_new, acc_new) in enumerate(new):
                m_s[j] = m_new
                l_s[j] = l_new
                acc[j] = acc_new
            return new

        @pl.when(ki < qi)
        def _():
            step(False)

        @pl.when(ki == qi)
        def _():
            for j, (m_new, l_new, acc_new) in enumerate(step(True)):
                o = (acc_new / l_new).T
                o_ref[:, j * V_HEAD:(j + 1) * V_HEAD] = o
                ob_ref[:, j * V_HEAD:(j + 1) * V_HEAD] = o.astype(bf16)
                lse_ref[j] = m_new + jnp.log2(l_new)

    kv = lambda g, qi, ki: (jnp.minimum(ki, qi), g)
    o_spec = pl.BlockSpec((tb, hp * V_HEAD), lambda g, qi, ki: (qi, g))
    return _call(body, "mla_fwd", (MLA_HEADS // hp, nb, nb),
                 [pl.BlockSpec((tb, hp * HEAD_PAD), lambda g, qi, ki: (qi, g)), pl.BlockSpec((tb, hp * HEAD_PAD), kv),
                  pl.BlockSpec((tb, hp * V_HEAD), kv)],
                 [o_spec, o_spec, pl.BlockSpec((hp, 1, tb), lambda g, qi, ki: (g, 0, qi))],
                 [_sds((t, MLA_HEADS * V_HEAD), f32), _sds((t, MLA_HEADS * V_HEAD), bf16), _sds((MLA_HEADS, 1, t), f32)],
                 [pltpu.VMEM((hp, 1, tb), f32), pltpu.VMEM((hp, 1, tb), f32), pltpu.VMEM((hp, V_HEAD, tb), f32)],
                 exchange=exchange)(qh, kh, vh)


def mla_delta(o, do):
    t = o.shape[0]
    tm = _tile(t, (512, 256, 128, 64))

    def body(o_ref, do_ref, d_ref):
        for h in range(MLA_HEADS):
            cols = slice(h * V_HEAD, (h + 1) * V_HEAD)
            d_ref[h] = jnp.sum(o_ref[:, cols] * do_ref[:, cols], axis=1, keepdims=True)

    blk = pl.BlockSpec((tm, MLA_HEADS * V_HEAD), lambda i: (i, 0))
    return _call(body, "mla_delta", (t // tm,), [blk, blk], pl.BlockSpec((MLA_HEADS, tm, 1), lambda i: (0, i, 0)),
                 _sds((MLA_HEADS, t, 1), f32))(o, do)


def mla_bwd(qh, kh, vh, do, lse_row, delta_row, exchange=None):
    t = qh.shape[0]
    tb = min(ATT_BLOCK, t)
    nb = t // tb

    hp = ATT_HEADS

    def body(q_ref, k_ref, v_ref, do_ref, lse_ref, dl_ref, dq_ref, dk_ref, dv_ref, dk_acc, dv_acc):
        ki, qi = pl.program_id(1), pl.program_id(2)

        @pl.when(jnp.logical_and(ki == 0, qi == 0))
        def _():
            dq_ref[...] = jnp.zeros_like(dq_ref)

        @pl.when(qi == 0)
        def _():
            dk_acc[...] = jnp.zeros_like(dk_acc)
            dv_acc[...] = jnp.zeros_like(dv_acc)

        def step(diagonal):
            rows = pl.ds(pl.multiple_of(qi * tb, tb), tb)
            new = []
            for j in range(hp):
                qc, vc = slice(j * HEAD_PAD, (j + 1) * HEAD_PAD), slice(j * V_HEAD, (j + 1) * V_HEAD)
                q, k, do_b = q_ref[:, qc], k_ref[:, qc], do_ref[:, vc]
                s = _bdot(k, q, 1, 1) * ATT_SCALE2
                if diagonal:
                    s = jnp.where(_chunk_visible(s.shape, 0), s, -jnp.inf)
                p = jnp.exp2(s - lse_ref[j])
                dp = _bdot(v_ref[:, vc], do_b, 1, 1)
                ds = p * (dp - dl_ref[j]) * ATT_SCALE
                new.append((dv_acc[:, vc] + _bdot(p, do_b, 1, 0), dk_acc[:, qc] + _bdot(ds, q, 1, 0),
                            dq_ref[rows, qc] + _bdot(ds, k, 0, 0)))
            for j, (dv, dk, dq) in enumerate(new):
                dv_acc[:, j * V_HEAD:(j + 1) * V_HEAD] = dv
                dk_acc[:, j * HEAD_PAD:(j + 1) * HEAD_PAD] = dk
                dq_ref[rows, j * HEAD_PAD:(j + 1) * HEAD_PAD] = dq

        @pl.when(qi > ki)
        def _():
            step(False)

        @pl.when(qi == ki)
        def _():
            step(True)

        @pl.when(qi == nb - 1)
        def _():
            dk_ref[...] = dk_acc[...]
            dv_ref[...] = dv_acc[...]

    qs = lambda g, ki, qi: (jnp.maximum(qi, ki), g)
    ks = lambda g, ki, qi: (ki, g)
    vec = pl.BlockSpec((hp, 1, tb), lambda g, ki, qi: (g, 0, jnp.maximum(qi, ki)))
    return _call(body, "mla_bwd", (MLA_HEADS // hp, nb, nb),
                 [pl.BlockSpec((tb, hp * HEAD_PAD), qs), pl.BlockSpec((tb, hp * HEAD_PAD), ks), pl.BlockSpec((tb, hp * V_HEAD), ks),
                  pl.BlockSpec((tb, hp * V_HEAD), qs), vec, vec],
                 [pl.BlockSpec((t, hp * HEAD_PAD), lambda g, ki, qi: (0, g)), pl.BlockSpec((tb, hp * HEAD_PAD), ks),
                  pl.BlockSpec((tb, hp * V_HEAD), ks)],
                 [_sds((t, MLA_HEADS * HEAD_PAD), f32), _sds((t, MLA_HEADS * HEAD_PAD), f32), _sds((t, MLA_HEADS * V_HEAD), f32)],
                 [pltpu.VMEM((tb, hp * HEAD_PAD), f32), pltpu.VMEM((tb, hp * V_HEAD), f32)], exchange=exchange)(
        qh, kh, vh, do, lse_row, delta_row)


PAD = 8


def _conv_taps(pad_ref, w, width, t):
    y = pad_ref[PAD - width + 1:PAD - width + 1 + t, :] * w[0:1, :]
    for j in range(1, width):
        y = y + pad_ref[PAD - width + 1 + j:PAD - width + 1 + j + t, :] * w[j:j + 1, :]
    return y


def _conv_bwd(xpad_ref, dpad_ref, w, da, width, t):
    dpad_ref[0:t, :] = da
    dpad_ref[t:t + PAD, :] = jnp.zeros((PAD, da.shape[1]), f32)
    dx = dpad_ref[width - 1:width - 1 + t, :] * w[0:1, :]
    for j in range(1, width):
        dx = dx + dpad_ref[width - 1 - j:width - 1 - j + t, :] * w[j:j + 1, :]
    dws = [jnp.sum(da * xpad_ref[PAD - width + 1 + j:PAD - width + 1 + j + t, :], axis=0, keepdims=True) for j in range(width)]
    return dx, dws


def _load_pad(pad_ref, x, t):
    pad_ref[0:PAD, :] = jnp.zeros((PAD, x.shape[1]), f32)
    pad_ref[PAD:PAD + t, :] = x


assert ML_DK == 128


def qk_conv(z_main, conv_qk):
    t = z_main.shape[0]
    base = O_Q // ML_DK

    def body(z_ref, w_ref, o_ref, pad):
        _load_pad(pad, z_ref[...], t)
        a = _conv_taps(pad, w_ref[...], ML_CONV, t)
        sc = jnp.where(pl.program_id(0) < ML_HEADS, ML_DK ** -0.5, 1.0)
        o_ref[0] = jax.nn.silu(a) * sc

    return _call(body, "qk_conv", (2 * ML_HEADS,),
                 [pl.BlockSpec((t, ML_DK), lambda j: (0, base + j)), pl.BlockSpec((ML_CONV, ML_DK), lambda j: (0, j))],
                 pl.BlockSpec((1, t, ML_DK), lambda j: (j, 0, 0)), _sds((2 * ML_HEADS, t, ML_DK), f32),
                 [pltpu.VMEM((t + PAD, ML_DK), f32)])(z_main, conv_qk)


def qk_conv_bwd(z_main, conv_qk, dq, dk):
    t = z_main.shape[0]
    base = O_Q // ML_DK

    def body(z_ref, w_ref, dq_ref, dk_ref, dz_ref, dw_ref, pad, dpad):
        _load_pad(pad, z_ref[...], t)
        w = w_ref[...]
        a = _conv_taps(pad, w, ML_CONV, t)
        is_q = pl.program_id(0) < ML_HEADS
        d = jnp.where(is_q, dq_ref[0] * (ML_DK ** -0.5), dk_ref[0])
        _, vjp = jax.vjp(jax.nn.silu, a)
        da, = vjp(d)
        dx, dws = _conv_bwd(pad, dpad, w, da, ML_CONV, t)
        dz_ref[...] = dx.astype(bf16)
        for j in range(ML_CONV):
            dw_ref[j:j + 1, :] = dws[j]

    head = lambda pick: pl.BlockSpec((1, t, ML_DK), lambda j: (pick(j), 0, 0))
    return _call(body, "qk_conv_bwd", (2 * ML_HEADS,),
                 [pl.BlockSpec((t, ML_DK), lambda j: (0, base + j)), pl.BlockSpec((ML_CONV, ML_DK), lambda j: (0, j)),
                  head(lambda j: jnp.minimum(j, ML_HEADS - 1)), head(lambda j: jnp.maximum(j - ML_HEADS, 0))],
                 [pl.BlockSpec((t, ML_DK), lambda j: (0, j)), pl.BlockSpec((ML_CONV, ML_DK), lambda j: (0, j))],
                 [_sds((t, 2 * ML_QK), bf16), _sds((ML_CONV, 2 * ML_QK), f32)],
                 [pltpu.VMEM((t + PAD, ML_DK), f32), pltpu.VMEM((t + PAD, ML_DK), f32)])(z_main, conv_qk, dq, dk)


def glu_fwd(hup, conv_w, bias):
    t, f2 = hup.shape
    nf = f2 // 2 // 128

    def body(h1_ref, h2_ref, w1_ref, w2_ref, b1_ref, b2_ref, o_ref, pad):
        _load_pad(pad, h1_ref[...], t)
        a1 = _conv_taps(pad, w1_ref[...], FFN_CONV, t) + b1_ref[...]
        _load_pad(pad, h2_ref[...], t)
        a2 = _conv_taps(pad, w2_ref[...], FFN_CONV, t) + b2_ref[...]
        o_ref[...] = (jax.nn.silu(a1) * a2).astype(bf16)

    col = lambda off: pl.BlockSpec((t, 128), lambda j: (0, j + off))
    wsp = lambda off: pl.BlockSpec((FFN_CONV, 128), lambda j: (0, j + off))
    bsp = lambda off: pl.BlockSpec((1, 128), lambda j: (0, j + off))
    return _call(body, "glu_fwd", (nf,), [col(0), col(nf), wsp(0), wsp(nf), bsp(0), bsp(nf)], col(0), _sds((t, f2 // 2), bf16),
                 [pltpu.VMEM((t + PAD, 128), f32)])(hup, hup, conv_w, conv_w, bias, bias)


def glu_bwd(hup, conv_w, bias, dg, exchange=None):
    t, f2 = hup.shape
    f = f2 // 2
    nf = f // 128

    def body(h1_ref, h2_ref, w1_ref, w2_ref, b1_ref, b2_ref, dg_ref, dh1_ref, dh2_ref, dw1_ref, dw2_ref, db1_ref, db2_ref,
             pad1, pad2, dpad):
        _load_pad(pad1, h1_ref[...], t)
        _load_pad(pad2, h2_ref[...], t)
        w1, w2 = w1_ref[...], w2_ref[...]
        a1 = _conv_taps(pad1, w1, FFN_CONV, t) + b1_ref[...]
        a2 = _conv_taps(pad2, w2, FFN_CONV, t) + b2_ref[...]
        d = dg_ref[...]
        _, vjp = jax.vjp(jax.nn.silu, a1)
        da1, = vjp(d * a2)
        da2 = d * jax.nn.silu(a1)
        for da, pad, w, dh_ref, dw_ref, db_ref in ((da1, pad1, w1, dh1_ref, dw1_ref, db1_ref), (da2, pad2, w2, dh2_ref, dw2_ref, db2_ref)):
            dx, dws = _conv_bwd(pad, dpad, w, da, FFN_CONV, t)
            dh_ref[...] = dx.astype(bf16)
            for j in range(FFN_CONV):
                dw_ref[j:j + 1, :] = dws[j]
            db_ref[...] = jnp.sum(da, axis=0, keepdims=True)

    col = lambda off: pl.BlockSpec((t, 128), lambda j: (0, j + off))
    wsp = lambda off: pl.BlockSpec((FFN_CONV, 128), lambda j: (0, j + off))
    bsp = lambda off: pl.BlockSpec((1, 128), lambda j: (0, j + off))
    return _call(body, "glu_bwd", (nf,), [col(0), col(nf), wsp(0), wsp(nf), bsp(0), bsp(nf), col(0)],
                 [col(0), col(0), wsp(0), wsp(0), bsp(0), bsp(0)],
                 [_sds((t, f), bf16)] * 2 + [_sds((FFN_CONV, f), f32)] * 2 + [_sds((1, f), f32)] * 2,
                 [pltpu.VMEM((t + PAD, 128), f32)] * 3, exchange=exchange)(hup, hup, conv_w, conv_w, bias, bias, dg)


def gate_act(z_tail, b_tile):
    t = z_tail.shape[0]
    tm = _tile(t, (512, 256, 128, 64))

    def body(z_ref, b_ref, o_ref):
        x = z_ref[...] + b_ref[...]
        lane = lax.broadcasted_iota(jnp.int32, x.shape, 1)
        o_ref[...] = jnp.where(lane < T_F, x, jax.nn.log_sigmoid(x))

    row = pl.BlockSpec((tm, TAIL), lambda i: (i, 0))
    return _call(body, "gate_act", (t // tm,), [row, pl.BlockSpec((1, TAIL), lambda i: (0, 0))], row, _sds((t, TAIL), f32))(z_tail, b_tile)


def tail_bwd(z_tail, b_tile, dzt_pe, dgate):
    t = z_tail.shape[0]
    tm = _tile(t, (512, 256, 128, 64))

    def body(z_ref, b_ref, dpe_ref, dg_ref, dz_ref, db_ref):
        x = z_ref[...] + b_ref[...]
        lane = lax.broadcasted_iota(jnp.int32, x.shape, 1)
        _, vjp = jax.vjp(jax.nn.log_sigmoid, x)
        df, = vjp(dg_ref[...])
        dgates = jnp.where(lane < T_F, dg_ref[...], df)
        dgates = jnp.where(jnp.logical_and(lane >= T_I, lane < T_F + ML_HEADS), dgates, 0.0)
        dz_ref[...] = jnp.where(lane < ROPE, dpe_ref[...], dgates).astype(bf16)
        _acc_row(db_ref, jnp.sum(dgates, axis=0, keepdims=True), pl.program_id(0) == 0)

    row = pl.BlockSpec((tm, TAIL), lambda i: (i, 0))
    return _call(body, "tail_bwd", (t // tm,), [row, pl.BlockSpec((1, TAIL), lambda i: (0, 0)), row, row],
                 [row, pl.BlockSpec((8, TAIL), lambda i: (0, 0))], [_sds((t, TAIL), bf16), _sds((8, TAIL), f32)])(z_tail, b_tile, dzt_pe, dgate)


def _hdot(a, b, ca, cb):
    return lax.dot_general(a.astype(bf16), b.astype(bf16), (((ca,), (cb,)), ((0,), (0,))), preferred_element_type=f32)


def _mlstm_step(q, k, v, igr, fgr, c_mat, n_vec, m):
    nh, ln = q.shape[0], CHUNK
    sq = (nh, ln, ln)
    row = lax.broadcasted_iota(jnp.int32, sq, 1)
    col = lax.broadcasted_iota(jnp.int32, sq, 2)
    eye = row == col

    def to_col(r):
        return jnp.sum(jnp.where(eye, jnp.broadcast_to(r, sq), 0.0), axis=2, keepdims=True)

    bc_r = jnp.sum(jnp.where(row <= col, jnp.broadcast_to(to_col(fgr), sq), 0.0), axis=1, keepdims=True)
    bc_c = to_col(bc_r)
    logw = jnp.where(col <= row, bc_c - bc_r + igr, -jnp.inf)
    inter = bc_c + m
    m_t = jnp.maximum(inter, jnp.max(logw, axis=2, keepdims=True))
    w_intra = jnp.exp(logw - m_t)
    w_inter = jnp.exp(inter - m_t)
    sc = _hdot(q, k, 2, 2) * w_intra
    num = w_inter * _hdot(q, c_mat, 2, 1) + _hdot(sc, v, 2, 1)
    qn = jnp.sum(q.astype(bf16).astype(f32) * n_vec.astype(bf16).astype(f32), axis=2, keepdims=True)
    den = w_inter * qn + jnp.sum(sc, axis=2, keepdims=True)
    h = num / jnp.maximum(jnp.abs(den), jnp.exp(-m_t))
    lane = lax.broadcasted_iota(jnp.int32, (nh, 1, ln), 2)
    b_last = jnp.sum(jnp.where(lane == ln - 1, bc_r, 0.0), axis=2, keepdims=True)
    logu = b_last - bc_r + igr
    m_new = jnp.maximum(b_last + m, jnp.max(logu, axis=2, keepdims=True))
    decay = jnp.exp(b_last + m - m_new)
    u_c = to_col(jnp.exp(logu - m_new))
    c_new = decay * c_mat + _hdot(u_c * k, v, 1, 1)
    n_new = decay * n_vec + jnp.sum(u_c.astype(bf16).astype(f32) * k.astype(bf16).astype(f32), axis=1, keepdims=True)
    return h, c_new, n_new, m_new


ML_VHALF = ML_V // 2
assert O_V % ML_VHALF == 0 and ML_HEADS % 2 == 0


def _ml_specs(nc, rev):
    cc = (lambda c: nc - 1 - c) if rev else (lambda c: c)
    q = pl.BlockSpec((ML_HEADS, CHUNK, ML_DK), lambda c: (0, cc(c), 0))
    k = pl.BlockSpec((ML_HEADS, CHUNK, ML_DK), lambda c: (1, cc(c), 0))
    v_lo = pl.BlockSpec((CHUNK, ML_VHALF), lambda c: (cc(c), O_V // ML_VHALF))
    v_hi = pl.BlockSpec((CHUNK, ML_VHALF), lambda c: (cc(c), O_V // ML_VHALF + 1))
    hv = pl.BlockSpec((ML_HEADS, CHUNK, ML_DV), lambda c: (0, cc(c), 0))
    gate = pl.BlockSpec((ML_HEADS, 1, 1, CHUNK), lambda c: (0, cc(c), 0, 0))
    cm = pl.BlockSpec((ML_HEADS, 1, ML_DK, ML_DV), lambda c: (0, cc(c), 0, 0))
    nv = pl.BlockSpec((ML_HEADS, 1, 1, ML_DK), lambda c: (0, cc(c), 0, 0))
    ms = pl.BlockSpec((ML_HEADS, 1, 1, 1), lambda c: (0, cc(c), 0, 0))
    return q, k, v_lo, v_hi, hv, gate, cm, nv, ms


_ML_STATE = [pltpu.VMEM((ML_HEADS, ML_DK, ML_DV), f32), pltpu.VMEM((ML_HEADS, 1, ML_DK), f32), pltpu.VMEM((ML_HEADS, 1, 1), f32)]


def _ml_zero_state(c_s, n_s, m_s):
    @pl.when(pl.program_id(0) == 0)
    def _():
        c_s[...] = jnp.zeros_like(c_s)
        n_s[...] = jnp.zeros_like(n_s)
        m_s[...] = jnp.zeros_like(m_s)


def _ml_heads_of(v_lo_ref, v_hi_ref):
    half = ML_HEADS // 2
    return jnp.stack([r[:, j * ML_DV:(j + 1) * ML_DV] for r in (v_lo_ref, v_hi_ref) for j in range(half)])


def mlstm_fwd(qk_act, z_main, ig, fg):
    t = qk_act.shape[1]
    nc = t // CHUNK

    def body(q_ref, k_ref, vl_ref, vh_ref, ig_ref, fg_ref, h_ref, c_out, n_out, m_out, c_s, n_s, m_s):
        _ml_zero_state(c_s, n_s, m_s)
        c0, n0, m0 = c_s[...], n_s[...], m_s[...]
        c_out[:, 0] = c0
        n_out[:, 0] = n0
        m_out[:, 0] = m0
        h, c2, n2, m2 = _mlstm_step(q_ref[...], k_ref[...], _ml_heads_of(vl_ref, vh_ref), ig_ref[:, 0], fg_ref[:, 0], c0, n0, m0)
        h_ref[...] = h
        c_s[...] = c2
        n_s[...] = n2
        m_s[...] = m2

    q, k, v_lo, v_hi, hv, gate, cm, nv, ms = _ml_specs(nc, False)
    return _call(body, "mlstm_fwd", (nc,), [q, k, v_lo, v_hi, gate, gate], [hv, cm, nv, ms],
                 [_sds((ML_HEADS, t, ML_DV), f32), _sds((ML_HEADS, nc, ML_DK, ML_DV), f32), _sds((ML_HEADS, nc, 1, ML_DK), f32),
                  _sds((ML_HEADS, nc, 1, 1), f32)], _ML_STATE)(qk_act, qk_act, z_main, z_main, ig, fg)


def mlstm_bwd(qk_act, z_main, ig, fg, c_all, n_all, m_all, dh, exchange=None):
    t = qk_act.shape[1]
    nc = t // CHUNK

    def body(q_ref, k_ref, vl_ref, vh_ref, ig_ref, fg_ref, c_ref, n_ref, m_ref, dh_ref, dq_ref, dk_ref, dv_ref, dig_ref, dfg_ref,
             dc_s, dn_s, dm_s):
        _ml_zero_state(dc_s, dn_s, dm_s)
        _, vjp = jax.vjp(_mlstm_step, q_ref[...], k_ref[...], _ml_heads_of(vl_ref, vh_ref), ig_ref[:, 0], fg_ref[:, 0],
                         c_ref[:, 0], n_ref[:, 0], m_ref[:, 0])
        dq, dk, dv, dig, dfg, dc, dn, dm = vjp((dh_ref[...], dc_s[...], dn_s[...], dm_s[...]))
        dq_ref[...] = dq
        dk_ref[...] = dk
        for j in range(ML_HEADS):
            dv_ref[:, j * ML_DV:(j + 1) * ML_DV] = dv[j].astype(bf16)
        dig_ref[:, 0] = dig
        dfg_ref[:, 0] = dfg
        dc_s[...] = dc
        dn_s[...] = dn
        dm_s[...] = dm

    q, k, v_lo, v_hi, hv, gate, cm, nv, ms = _ml_specs(nc, True)
    gshape = _sds((ML_HEADS, nc, 1, CHUNK), f32)
    return _call(body, "mlstm_bwd", (nc,), [q, k, v_lo, v_hi, gate, gate, cm, nv, ms, hv],
                 [q, q, pl.BlockSpec((CHUNK, ML_V), lambda c: (nc - 1 - c, 0)), gate, gate],
                 [_sds((ML_HEADS, t, ML_DK), f32), _sds((ML_HEADS, t, ML_DK), f32), _sds((t, ML_V), bf16), gshape, gshape],
                 _ML_STATE, exchange=exchange)(qk_act, qk_act, z_main, z_main, ig, fg, c_all, n_all, m_all, dh)


def _ml_out(h, zo, g):
    return _rms(h, g) * jax.nn.sigmoid(zo)


def mlstm_out(h, z_main, g_hnorm):
    t = h.shape[1]
    tm = _tile(t, (512, 256, 128, 64))
    zo = O_O // ML_DV

    def body(h_ref, z_ref, g_ref, y_ref):
        y_ref[...] = _ml_out(h_ref[0], z_ref[...], g_ref[0]).astype(bf16)

    return _call(body, "mlstm_out", (t // tm, ML_HEADS),
                 [pl.BlockSpec((1, tm, ML_DV), lambda i, hd: (hd, i, 0)), pl.BlockSpec((tm, ML_DV), lambda i, hd: (i, zo + hd)),
                  pl.BlockSpec((1, 1, ML_DV), lambda i, hd: (hd, 0, 0))],
                 pl.BlockSpec((tm, ML_DV), lambda i, hd: (i, hd)), _sds((t, ML_V), bf16))(h, z_main, g_hnorm)


def mlstm_out_bwd(h, z_main, g_hnorm, dy):
    t = h.shape[1]
    tm = _tile(t, (512, 256, 128, 64))
    zo = O_O // ML_DV

    def body(h_ref, z_ref, g_ref, dy_ref, dh_ref, dzo_ref, dg_ref):
        _, vjp = jax.vjp(_ml_out, h_ref[0], z_ref[...], g_ref[0])
        dh, dz, dg = vjp(dy_ref[...])
        dh_ref[0] = dh
        dzo_ref[...] = dz.astype(bf16)

        @pl.when(pl.program_id(1) == 0)
        def _():
            dg_ref[...] = jnp.zeros_like(dg_ref)

        dg_ref[0, 0:1, :] += dg

    head = pl.BlockSpec((1, tm, ML_DV), lambda hd, i: (hd, i, 0))
    blk = pl.BlockSpec((tm, ML_DV), lambda hd, i: (i, hd))
    return _call(body, "mlstm_out_bwd", (ML_HEADS, t // tm),
                 [head, pl.BlockSpec((tm, ML_DV), lambda hd, i: (i, zo + hd)), pl.BlockSpec((1, 1, ML_DV), lambda hd, i: (hd, 0, 0)), blk],
                 [head, blk, pl.BlockSpec((1, 8, ML_DV), lambda hd, i: (hd, 0, 0))],
                 [_sds((ML_HEADS, t, ML_DV), f32), _sds((t, ML_V), bf16), _sds((ML_HEADS, 8, ML_DV), f32)])(h, z_main, g_hnorm, dy)


def _merge(ga, gb, ya, yb):
    return jax.nn.sigmoid(ga) * ya + jax.nn.sigmoid(gb) * yb


def _merge_specs(t, d):
    tm = _tile(t, (512, 256, 128, 64))
    bw = _tile(d, (512, 256, 128))
    assert O_GA % bw == 0 and (O_GA + d) % bw == 0
    blk = pl.BlockSpec((tm, bw), lambda i, j: (i, j))
    ga = pl.BlockSpec((tm, bw), lambda i, j: (i, O_GA // bw + j))
    gb = pl.BlockSpec((tm, bw), lambda i, j: (i, (O_GA + d) // bw + j))
    return tm, bw, blk, ga, gb


def merge_fwd(z_main, ya, yb):
    t, d = ya.shape
    tm, bw, blk, ga, gb = _merge_specs(t, d)

    def body(ga_ref, gb_ref, ya_ref, yb_ref, o_ref):
        o_ref[...] = _merge(ga_ref[...], gb_ref[...], ya_ref[...], yb_ref[...]).astype(bf16)

    return _call(body, "merge_fwd", (t // tm, d // bw), [ga, gb, blk, blk], blk, _sds((t, d), bf16))(z_main, z_main, ya, yb)


def merge_bwd(z_main, ya, yb, dmerged):
    t, d = ya.shape
    tm, bw, blk, ga, gb = _merge_specs(t, d)

    def body(ga_ref, gb_ref, ya_ref, yb_ref, dm_ref, dga_ref, dgb_ref, dya_ref, dyb_ref):
        _, vjp = jax.vjp(_merge, ga_ref[...], gb_ref[...], ya_ref[...], yb_ref[...])
        dga, dgb, dya, dyb = vjp(dm_ref[...])
        dga_ref[...] = dga.astype(bf16)
        dgb_ref[...] = dgb.astype(bf16)
        dya_ref[...] = dya.astype(bf16)
        dyb_ref[...] = dyb.astype(bf16)

    return _call(body, "merge_bwd", (t // tm, d // bw), [ga, gb, blk, blk, blk], [blk] * 4, [_sds((t, d), bf16)] * 4)(
        z_main, z_main, ya, yb, dmerged)


def _cross(cq, ck, cv, gq, gk):
    outs = []
    for hd in range(CR_HEADS):
        sl = slice(hd * CR_HD, (hd + 1) * CR_HD)
        q = _rms(cq[:, sl], gq)
        k = _rms(ck[:, sl], gk)
        s = _bdot(q, k, 1, 1) * (CR_HD ** -0.5)
        p = jax.nn.softmax(s, axis=-1)
        outs.append(_bdot(p, cv[:, sl], 1, 0))
    return jnp.concatenate(outs, axis=1)


def cross_fwd(cq, ck, cv, gq, gk):
    t, w = cq.shape
    nm = ck.shape[0]
    tm = _tile(t, (512, 256, 128, 64))

    def body(q_ref, k_ref, v_ref, gq_ref, gk_ref, o_ref):
        o_ref[...] = _cross(q_ref[...], k_ref[...], v_ref[...], gq_ref[...], gk_ref[...]).astype(bf16)

    row = pl.BlockSpec((tm, w), lambda i: (i, 0))
    full = pl.BlockSpec((nm, w), lambda i: (0, 0))
    gain = pl.BlockSpec((1, CR_HD), lambda i: (0, 0))
    return _call(body, "cross_fwd", (t // tm,), [row, full, full, gain, gain], row, _sds((t, w), bf16))(cq, ck, cv, gq, gk)


def cross_bwd(cq, ck, cv, gq, gk, do):
    t, w = cq.shape
    nm = ck.shape[0]
    tm = _tile(t, (512, 256, 128, 64))

    def body(q_ref, k_ref, v_ref, gq_ref, gk_ref, do_ref, dq_ref, dk_ref, dv_ref, dgq_ref, dgk_ref):
        first = pl.program_id(0) == 0
        _, vjp = jax.vjp(_cross, q_ref[...], k_ref[...], v_ref[...], gq_ref[...], gk_ref[...])
        dq, dk, dv, dgq, dgk = vjp(do_ref[...])
        dq_ref[...] = dq.astype(bf16)

        @pl.when(first)
        def _():
            dk_ref[...] = jnp.zeros_like(dk_ref)
            dv_ref[...] = jnp.zeros_like(dv_ref)

        dk_ref[...] += dk
        dv_ref[...] += dv
        _acc_row(dgq_ref, dgq, first)
        _acc_row(dgk_ref, dgk, first)

    row = pl.BlockSpec((tm, w), lambda i: (i, 0))
    full = pl.BlockSpec((nm, w), lambda i: (0, 0))
    gain = pl.BlockSpec((1, CR_HD), lambda i: (0, 0))
    acc = pl.BlockSpec((8, CR_HD), lambda i: (0, 0))
    return _call(body, "cross_bwd", (t // tm,), [row, full, full, gain, gain, row], [row, full, full, acc, acc],
                 [_sds((t, w), bf16), _sds((nm, w), f32), _sds((nm, w), f32), _sds((8, CR_HD), f32), _sds((8, CR_HD), f32)])(
        cq, ck, cv, gq, gk, do)


def loss_head(x2, fo, target):
    t, d = x2.shape
    tm = _tile(t, (256, 128, 64, 32, 16, 8))

    def body(a_ref, b_ref, t_ref, dx_ref, dxb_ref, l_ref):
        err = a_ref[...] + b_ref[...] - t_ref[...]
        dx = err / d
        dx_ref[...] = dx
        dxb_ref[...] = dx.astype(bf16)
        part = 0.5 * jnp.sum(jnp.mean(err * err, axis=1, keepdims=True), axis=0, keepdims=True)
        _acc_row(l_ref, jnp.broadcast_to(part, (1, 128)), pl.program_id(0) == 0)

    row = pl.BlockSpec((tm, d), lambda i: (i, 0))
    return _call(body, "loss_head", (t // tm,), [row, row, row], [row, row, pl.BlockSpec((8, 128), lambda i: (0, 0))],
                 [_sds((t, d), f32), _sds((t, d), bf16), _sds((8, 128), f32)])(x2, fo, target)


def _place():
    x, y, c = lax.axis_index("x"), lax.axis_index("y"), lax.axis_index("c")
    peers = {}
    for r in range(1, N_DEV):
        px = 1 - x if r & 4 else x
        py = 1 - y if r & 2 else y
        pc = 1 - c if r & 1 else c
        peers[r] = ((px, py, pc), 4 * px + 2 * py + pc)
    return 4 * x + 2 * y + c, peers


N_REL = N_DEV - 1
RELATIONS = tuple(range(1, N_DEV))
SIBLING = 1
OTHER_CHIPS = (2, 4, 6)
PASSED_ON = (3, 5, 7)


def _exchange_ops(ins, outs, sems, scatter):
    n = len(ins)
    send_sems, recv_sems, local_sems = sems

    def tools():
        me, peers = _place()

        def copy(a, r, src, dst_idx, to):
            return pltpu.make_async_remote_copy(
                src_ref=src, dst_ref=outs[a].at[dst_idx], send_sem=send_sems.at[a * N_REL + r - 1],
                recv_sem=recv_sems.at[a * N_REL + r - 1], device_id=peers[to][0], device_id_type=MESH)

        def local(a):
            return pltpu.make_async_copy(ins[a].at[me] if scatter else ins[a], outs[a].at[me], local_sems.at[a])

        def arrival(a, r):
            return copy(a, r, ins[a].at[me] if scatter else ins[a], peers[r][1], r)

        return me, peers, copy, local, arrival

    if scatter:
        def sends():
            me, peers, copy, local, _ = tools()
            return [local(a) for a in range(n)], [copy(a, r, ins[a].at[peers[r][1]], me, r) for a in range(n) for r in RELATIONS]

        def start():
            loc, out = sends()
            for cp in loc + out:
                cp.start()

        middle = None
        waited_last = RELATIONS
    else:
        def sends():
            me, peers, copy, local, _ = tools()
            own = [copy(a, r, ins[a], me, r) for a in range(n) for r in (SIBLING,) + OTHER_CHIPS]
            return [local(a) for a in range(n)], own

        def passes():
            me, peers, copy, _, _ = tools()
            return [copy(a, r, outs[a].at[peers[r - 1][1]], peers[r - 1][1], SIBLING) for a in range(n) for r in PASSED_ON]

        def start():
            loc, out = sends()
            for cp in loc + out:
                cp.start()

        def middle():
            _, _, _, _, arrival = tools()
            fwd = passes()
            for a in range(n):
                for i, r in enumerate(PASSED_ON):
                    arrival(a, r - 1).wait_recv()
                    fwd[a * len(PASSED_ON) + i].start()

        waited_last = (SIBLING,) + PASSED_ON

    def wait():
        _, _, _, _, arrival = tools()
        for a in range(n):
            for r in waited_last:
                arrival(a, r).wait_recv()
        loc, out = sends()
        for cp in out + ([] if scatter else passes()):
            cp.wait_send()
        for cp in loc:
            cp.wait()

    return start, middle, wait


def _exchange_shapes(arrs, scatter):
    return [_sds(a.shape if scatter else (N_DEV,) + a.shape, a.dtype) for a in arrs]


def _exchange_sems(n):
    return [pltpu.SemaphoreType.DMA((n * N_REL,)), pltpu.SemaphoreType.DMA((n * N_REL,)), pltpu.SemaphoreType.DMA((n,))]


def _exchange(arrs, name, scatter):
    n = len(arrs)

    def body(*refs):
        start, middle, wait = _exchange_ops(refs[:n], refs[n:2 * n], refs[2 * n:], scatter)
        start()
        if middle is not None:
            middle()
        wait()

    any_spec = pl.BlockSpec(memory_space=pl.ANY)
    return pl.pallas_call(body, name=name, in_specs=[any_spec] * n, out_specs=[any_spec] * n,
                          out_shape=_exchange_shapes(arrs, scatter), scratch_shapes=_exchange_sems(n))(*arrs)


def cast_bf16(w, name):
    _, r, c = w.shape
    tr = _tile(r, (256, 128, 64, 32, 16))

    def body(w_ref, o_ref):
        o_ref[...] = w_ref[0].astype(bf16)

    return _call(body, name, (r // tr,), [pl.BlockSpec((1, tr, c), lambda i: (0, i, 0))], pl.BlockSpec((tr, c), lambda i: (i, 0)),
                 _sds((r, c), bf16))(w)


def _adamw(w, g, m, v):
    m = ADAM_B1 * m + (1.0 - ADAM_B1) * g
    v = ADAM_B2 * v + (1.0 - ADAM_B2) * jnp.square(g)
    m_hat = m / (1.0 - ADAM_B1 ** ADAM_STEP)
    v_hat = v / (1.0 - ADAM_B2 ** ADAM_STEP)
    delta = -ADAM_LR * (m_hat / (jnp.sqrt(v_hat) + ADAM_EPS) + ADAM_WD * w)
    return delta, m, v


def adam_sum(parts, w, m, v, name):
    _, r, c = parts.shape
    budget = 4 * 1024 * 1024
    tr = r
    for cand in (1024, 512, 256, 128, 64, 32, 16):
        if r % cand == 0 and N_DEV * cand * c * 4 <= budget:
            tr = cand
            break

    def body(p_ref, w_ref, m_ref, v_ref, g_ref, d_ref, m2_ref, v2_ref):
        g = p_ref[0].astype(f32)
        for k in range(1, N_DEV):
            g = g + p_ref[k].astype(f32)
        d, m2, v2 = _adamw(w_ref[0], g, m_ref[0], v_ref[0])
        g_ref[...] = g
        d_ref[...] = d
        m2_ref[...] = m2
        v2_ref[...] = v2

    blk = pl.BlockSpec((1, tr, c), lambda i: (0, i, 0))
    out = pl.BlockSpec((tr, c), lambda i: (i, 0))
    return _call(body, name, (r // tr,), [pl.BlockSpec((N_DEV, tr, c), lambda i: (0, i, 0)), blk, blk, blk], [out] * 4,
                 [_sds((r, c), f32)] * 4)(parts, w, m, v)


def sum_parts(parts, name):
    _, r, c = parts.shape

    def body(p_ref, o_ref):
        g = p_ref[0]
        for k in range(1, N_DEV):
            g = g + p_ref[k]
        o_ref[...] = g

    return pl.pallas_call(body, name=name, out_shape=_sds((r, c), f32))(parts)


def adam_flat(w, g, m, v, name):
    def body(w_ref, g_ref, m_ref, v_ref, d_ref, m2_ref, v2_ref):
        d, m2, v2 = _adamw(w_ref[...], g_ref[...], m_ref[...], v_ref[...])
        d_ref[...] = d
        m2_ref[...] = m2
        v2_ref[...] = v2

    return pl.pallas_call(body, name=name, out_shape=[_sds(w.shape, f32)] * 3)(w, g, m, v)


def _pack(vecs, multiple):
    flat = jnp.concatenate([v.reshape(-1) for v in vecs])
    n = flat.shape[0]
    total = -(-n // multiple) * multiple
    return jnp.pad(flat, (0, total - n))


def _unpack(flat, shapes):
    out, pos = [], 0
    for s in shapes:
        n = 1
        for d in s:
            n *= d
        out.append(flat[pos:pos + n].reshape(s))
        pos += n
    return out


def _pad_lanes(v, width=TAIL):
    return jnp.pad(v, ((0, 0), (0, width - v.shape[1])))


def kernel(x, mem, positions, g_mix, w_in, g_qa, w_qb, g_kva, w_kvb, g_qn_nope, g_qn_pe, g_kn_nope, g_kn_pe, conv_qk, b_if, g_hnorm, p_a, p_b, w_out, g_cross, g_mem, wq_c, wk_c, wv_c, g_cq, g_ck, wo_c, g_ffn, w_up, conv_ffn, b_conv_ffn, w_down, loss_target, m_g_mix, m_w_in, m_g_qa, m_w_qb, m_g_kva, m_w_kvb, m_g_qn_nope, m_g_qn_pe, m_g_kn_nope, m_g_kn_pe, m_conv_qk, m_b_if, m_g_hnorm, m_p_a, m_p_b, m_w_out, m_g_cross, m_g_mem, m_wq_c, m_wk_c, m_wv_c, m_g_cq, m_g_ck, m_wo_c, m_g_ffn, m_w_up, m_conv_ffn, m_b_conv_ffn, m_w_down, v_g_mix, v_w_in, v_g_qa, v_w_qb, v_g_kva, v_w_kvb, v_g_qn_nope, v_g_qn_pe, v_g_kn_nope, v_g_kn_pe, v_conv_qk, v_b_if, v_g_hnorm, v_p_a, v_p_b, v_w_out, v_g_cross, v_g_mem, v_wq_c, v_wk_c, v_wv_c, v_g_cq, v_g_ck, v_wo_c, v_g_ffn, v_w_up, v_conv_ffn, v_b_conv_ffn, v_w_down):
    args = dict(locals())
    names = ['g_mix', 'w_in', 'g_qa', 'w_qb', 'g_kva', 'w_kvb', 'g_qn_nope', 'g_qn_pe', 'g_kn_nope', 'g_kn_pe', 'conv_qk', 'b_if',
             'g_hnorm', 'p_a', 'p_b', 'w_out', 'g_cross', 'g_mem', 'wq_c', 'wk_c', 'wv_c', 'g_cq', 'g_ck', 'wo_c', 'g_ffn', 'w_up',
             'conv_ffn', 'b_conv_ffn', 'w_down']
    big = ['w_in', 'w_qb', 'w_kvb', 'p_a', 'p_b', 'w_out', 'wq_c', 'wk_c', 'wv_c', 'wo_c', 'w_up', 'w_down']
    sharded_small = ['conv_qk', 'g_hnorm', 'conv_ffn']
    replicated = [n for n in names if n not in big and n not in sharded_small]

    t, d = x.shape[1], x.shape[2]
    x2d, tgt = x[0], loss_target[0]
    mem2d = mem[0]
    me = 4 * lax.axis_index("x") + 2 * lax.axis_index("y") + lax.axis_index("c")
    nc = t // CHUNK
    f2 = b_conv_ffn.shape[1]
    wmain = O_GA + 2 * d

    first = ['w_in', 'w_qb', 'w_kvb']
    behind_in = ['p_a', 'p_b', 'w_out', 'wq_c', 'wk_c', 'wv_c', 'wo_c']
    shards = {n: cast_bf16(args[n], "cast_" + n) for n in big}
    small_local = _pack([args[n] for n in sharded_small], 128).reshape(1, -1)
    gathered = _exchange([shards[n] for n in first] + [small_local], "comm_gather_first", scatter=False)
    gw = dict(zip(first, gathered[:-1]))
    small_all = gathered[-1]
    full_small, pos = [], 0
    for n in sharded_small:
        _, rows, cols = args[n].shape
        piece = small_all[:, 0, pos:pos + rows * cols].reshape(N_DEV, rows, cols)
        full_small.append(piece.transpose(1, 0, 2).reshape(rows, N_DEV * cols))
        pos += rows * cols
    conv_qk_f, g_hnorm_f, conv_ffn_f = full_small

    shard_w = w_in.shape[2]
    c_kpe, c_q, c_i, c_o = O_Q, O_Q + ROPE, O_Q + ROPE + 2 * ML_QK + ML_V, O_Q + ROPE + 2 * ML_QK + ML_V + 2 * ML_HEADS
    segments = [(0, c_kpe, 'main', 0), (c_kpe, c_q, 'tail', 0), (c_q, c_i, 'main', O_Q), (c_i, c_o, 'tail', T_I),
                (c_o, N_DEV * shard_w, 'main', O_O)]

    def shard_cuts(lo, hi):
        return [(j, max(lo, j * shard_w) - j * shard_w, min(hi, (j + 1) * shard_w) - j * shard_w)
                for j in range(lo // shard_w, (hi - 1) // shard_w + 1)]

    def gathered_cols(target):
        return [gw['w_in'][j][:, a:b] for lo, hi, tg, _ in segments if tg == target for j, a, b in shard_cuts(lo, hi)]

    w_main = jnp.concatenate(gathered_cols('main'), axis=1)[None]
    w_tail = jnp.concatenate(gathered_cols('tail') + [jnp.zeros((d, TAIL - ROPE - 2 * ML_HEADS), bf16)], axis=1)[None]
    assert w_main.shape[2] == wmain

    inv_freq = ROPE_BASE ** (-jnp.arange(0, ROPE, 2, dtype=f32) / ROPE)
    inv_tile = _pad_lanes(jnp.concatenate([inv_freq, inv_freq])[None])
    cos, sin = rope_tables(positions.reshape(t, 1), inv_tile)
    gqp, gkp = _pad_lanes(g_qn_pe), _pad_lanes(g_kn_pe)
    b_tile = jnp.pad(b_if, ((0, 0), (T_I, TAIL - T_I - 2 * ML_HEADS)))

    u0 = rms_fwd(x2d, g_mix, "rms_mix")
    z_main, got = mm_nn(u0, w_main, f32, "mm_in_main", exchange=([shards[n] for n in behind_in], False))
    gw.update(zip(behind_in, got))
    qb = gw['w_qb'].transpose(1, 0, 2).reshape(Q_LORA, MLA_HEADS, NOPE + ROPE)
    w_qb_p = jnp.concatenate([qb, jnp.zeros((Q_LORA, MLA_HEADS, HEAD_PAD - NOPE - ROPE), bf16)], axis=2).reshape(1, Q_LORA, -1)
    w_kvb3 = gw['w_kvb']
    p_a3, p_b3, w_out3 = (gw[n].reshape(1, -1, d) for n in ('p_a', 'p_b', 'w_out'))
    wq_c3, wk_c3, wv_c3 = (gw[n].reshape(1, d, -1) for n in ('wq_c', 'wk_c', 'wv_c'))
    wo_c3 = gw['wo_c']
    z_tail = mm_nn(u0, w_tail, f32, "mm_in_tail")
    qa_n, kv_n = lat_norm(z_main, g_qa, g_kva)
    q_raw = mm_nn(qa_n, w_qb_p, f32, "mm_qb")
    kv_raw = mm_nn(kv_n, w_kvb3, f32, "mm_kvb")
    qh, kh, vh = mla_prep(q_raw, kv_raw, z_tail, cos, sin, g_qn_nope, gqp, g_kn_nope, gkp)
    (o_a, o_ab, lse), (w_up3,) = mla_fwd(qh, kh, vh, exchange=([shards['w_up']], False))

    qk_act = qk_conv(z_main, conv_qk_f)
    gates = gate_act(z_tail, b_tile)

    def to_rows(cols):
        return cols.T.reshape(ML_HEADS, nc, 1, CHUNK)

    ig, fg = to_rows(gates[:, T_I:T_F]), to_rows(gates[:, T_F:T_F + ML_HEADS])
    h_ml, c_all, n_all, m_all = mlstm_fwd(qk_act, z_main, ig, fg)
    g_hn3 = g_hnorm_f.reshape(ML_HEADS, 1, ML_DV)
    y_b = mlstm_out(h_ml, z_main, g_hn3)

    ya = mm_nn(o_ab, p_a3, f32, "mm_pa")
    yb = mm_nn(y_b, p_b3, f32, "mm_pb")
    merged = merge_fwd(z_main, ya, yb)
    mo = mm_nn(merged, w_out3, f32, "mm_out")
    x1, uc = resid_rms(x2d, mo, g_cross, "resid_cross")
    mem_n = rms_fwd(mem2d, g_mem, "rms_mem")
    cq = mm_nn(uc, wq_c3, f32, "mm_cq")
    ck = mm_nn(mem_n, wk_c3, f32, "mm_ck")
    cv = mm_nn(mem_n, wv_c3, f32, "mm_cv")
    o_c = cross_fwd(cq, ck, cv, g_cq, g_ck)
    co = mm_nn(o_c, wo_c3, f32, "mm_oc")
    x2, u3 = resid_rms(x1, co, g_ffn, "resid_ffn")
    hup, (w_down_g,) = mm_nn(u3, w_up3, f32, "mm_up", exchange=([shards['w_down']], False))
    w_down3 = w_down_g.reshape(1, -1, d)
    gl = glu_fwd(hup, conv_ffn_f, b_conv_ffn)
    fo = mm_nn(gl, w_down3, f32, "mm_down")
    dx3, dx3_b, loss_acc = loss_head(x2, fo, tgt)

    grads, parts = {}, {}
    grads['w_down'] = mm_tn(gl, dx3_b, 1, "mm_d_wdown").reshape(N_DEV, -1, d)
    dgl = mm_nt(dx3_b, w_down3, f32, "mm_d_gl")
    (dh1, dh2, dcw1, dcw2, db1, db2), (parts['w_down'],) = glu_bwd(hup, conv_ffn_f, b_conv_ffn, dgl,
                                                                    exchange=([grads['w_down']], True))
    dconv_ffn, db_ffn = (jnp.concatenate(pair, axis=1) for pair in ((dcw1, dcw2), (db1, db2)))
    grads['w_up'] = mm_tn_cols(u3, [dh1, dh2], N_DEV, "mm_d_wup")
    du3 = mm_nt_cols([dh1, dh2], w_up3, f32, "mm_d_u3")
    dx2, dx2_b, dg_ffn = rms_bwd(x2, g_ffn, [du3], dx3, "rms_bwd_ffn", want_b16=True)
    grads['wo_c'] = mm_tn(o_c, dx2_b, N_DEV, "mm_d_woc")
    do_c = mm_nt(dx2_b, wo_c3, f32, "mm_d_oc")
    dcq, dck, dcv, dg_cq, dg_ck = cross_bwd(cq, ck, cv, g_cq, g_ck, do_c)
    grads['wq_c'] = mm_tn(uc, dcq, 1, "mm_d_wqc").reshape(N_DEV, -1, dcq.shape[1])
    grads['wk_c'] = mm_tn(mem_n, dck, 1, "mm_d_wkc").reshape(N_DEV, -1, dck.shape[1])
    grads['wv_c'] = mm_tn(mem_n, dcv, 1, "mm_d_wvc").reshape(N_DEV, -1, dcv.shape[1])
    duc = mm_nt(dcq, wq_c3, f32, "mm_d_uc")
    dmem_k = mm_nt(dck, wk_c3, f32, "mm_d_memk")
    dmem_v = mm_nt(dcv, wv_c3, f32, "mm_d_memv")
    dg_mem, = rms_bwd(mem2d, g_mem, [dmem_k, dmem_v], None, "rms_bwd_mem", want_dx=False)
    dx1, dx1_b, dg_cross = rms_bwd(x1, g_cross, [duc], dx2, "rms_bwd_cross", want_b16=True)
    grads['w_out'] = mm_tn(merged, dx1_b, 1, "mm_d_wout").reshape(N_DEV, -1, d)
    dmerged = mm_nt(dx1_b, w_out3, f32, "mm_d_merged")
    dga, dgb, dya, dyb = merge_bwd(z_main, ya, yb, dmerged)
    grads['p_a'] = mm_tn(o_ab, dya, 1, "mm_d_pa").reshape(N_DEV, -1, d)
    grads['p_b'] = mm_tn(y_b, dyb, 1, "mm_d_pb").reshape(N_DEV, -1, d)
    do_a = mm_nt(dya, p_a3, f32, "mm_d_oa")
    dy_b = mm_nt(dyb, p_b3, f32, "mm_d_yb")

    dh_ml, dzo, dg_hn = mlstm_out_bwd(h_ml, z_main, g_hn3, dy_b)
    mixers = ['p_a', 'p_b', 'w_out']
    (dq_act, dk_act, dzv, dig, dfg), got = mlstm_bwd(qk_act, z_main, ig, fg, c_all, n_all, m_all, dh_ml,
                                                     exchange=([grads[n] for n in mixers], True))
    parts.update(zip(mixers, got))
    dzqk, dconv_qk = qk_conv_bwd(z_main, conv_qk_f, dq_act, dk_act)

    delta = mla_delta(o_a, do_a)
    (dqh, dkh, dvh), (parts['w_up'],) = mla_bwd(qh, kh, vh, do_a, lse, delta.reshape(MLA_HEADS, 1, t),
                                                exchange=([grads['w_up']], True))
    dq_raw, dkv_raw, dzt_pe, dg_qn, dg_qp, dg_kn, dg_kp = mla_prep_bwd(
        q_raw, kv_raw, z_tail, cos, sin, g_qn_nope, gqp, g_kn_nope, gkp, dqh, dkh, dvh)
    d_wqb_p = mm_tn(qa_n, dq_raw, 1, "mm_d_wqb")[0].reshape(Q_LORA, MLA_HEADS, HEAD_PAD)[:, :, :NOPE + ROPE]
    grads['w_qb'] = d_wqb_p.reshape(Q_LORA, N_DEV, -1).transpose(1, 0, 2)
    grads['w_kvb'] = mm_tn(kv_n, dkv_raw, N_DEV, "mm_d_wkvb")
    dqa = mm_nt(dq_raw, w_qb_p, f32, "mm_d_qa")
    dkvn = mm_nt(dkv_raw, w_kvb3, f32, "mm_d_kvn")
    dz_lat, dg_qa, dg_kva = lat_norm_bwd(z_main, g_qa, g_kva, dqa, dkvn)

    def to_cols(rows):
        return rows.reshape(ML_HEADS, t).T

    dgate = jnp.pad(jnp.concatenate([to_cols(dig), to_cols(dfg)], axis=1), ((0, 0), (T_I, TAIL - T_I - 2 * ML_HEADS)))
    dz_tail, db_if = tail_bwd(z_tail, b_tile, dzt_pe, dgate)
    dz_main = [dz_lat, dzqk, dzv, dzo, dga, dgb]
    small_mats = ['wq_c', 'wk_c', 'wv_c', 'wo_c', 'w_qb', 'w_kvb']
    d_wmain3, got = mm_tn_cols(u0, dz_main, 1, "mm_d_wmain", exchange=([grads[n] for n in small_mats], True))
    parts.update(zip(small_mats, got))
    d_wmain = d_wmain3[0]
    d_wtail = mm_tn(u0, dz_tail, 1, "mm_d_wtail")[0]
    d_target = {'main': d_wmain, 'tail': d_wtail}
    blocks = []
    for j in range(N_DEV):
        lo_j, hi_j = j * shard_w, (j + 1) * shard_w
        cols = [d_target[tg][:, off + max(lo, lo_j) - lo:off + min(hi, hi_j) - lo]
                for lo, hi, tg, off in segments if lo < hi_j and hi > lo_j]
        blocks.append(jnp.concatenate(cols, axis=1))
    grads['w_in'] = jnp.stack(blocks)
    du0_a, (parts['w_in'],) = mm_nt_cols(dz_main, w_main, f32, "mm_d_u0_main", exchange=([grads['w_in']], True))
    du0_b = mm_nt(dz_tail, w_tail, f32, "mm_d_u0_tail")
    grad_x, dg_mix = rms_bwd(x2d, g_mix, [du0_a, du0_b], dx1, "rms_bwd_mix")

    out_g, out_d, out_m, out_v = {}, {}, {}, {}
    for n in big:
        res = adam_sum(parts[n], args[n], args['m_' + n], args['v_' + n], "adam_" + n)
        out_g[n], out_d[n], out_m[n], out_v[n] = (a.reshape(args[n].shape) for a in res)

    small_full = {
        'g_mix': dg_mix[0], 'g_qa': dg_qa[0], 'g_kva': dg_kva[0], 'g_qn_nope': dg_qn[0], 'g_qn_pe': dg_qp[0, :ROPE],
        'g_kn_nope': dg_kn[0], 'g_kn_pe': dg_kp[0, :ROPE], 'conv_qk': dconv_qk, 'b_if': db_if[0, T_I:T_I + 2 * ML_HEADS],
        'g_hnorm': dg_hn[:, 0, :], 'g_cross': dg_cross[0], 'g_mem': dg_mem[0], 'g_cq': dg_cq[0], 'g_ck': dg_ck[0],
        'g_ffn': dg_ffn[0], 'conv_ffn': dconv_ffn, 'b_conv_ffn': db_ffn[0], 'loss': loss_acc[0, :1]}
    order = list(small_full)
    packed = _pack([small_full[n] for n in order], 8 * 128).reshape(1, -1)
    gathered_small, = _exchange([packed], "comm_gather_small", scatter=False)
    summed = sum_parts(gathered_small.reshape(N_DEV, -1, 128), "sum_small").reshape(-1)
    full_g = dict(zip(order, _unpack(summed, [small_full[n].shape for n in order])))
    loss = full_g['loss'][0]

    local_g = {}
    for n in replicated:
        local_g[n] = full_g[n].reshape(args[n].shape)
    for n in sharded_small:
        shp = args[n].shape
        full = full_g[n].reshape((1,) + full_g[n].shape)
        local_g[n] = lax.dynamic_slice_in_dim(full, me * shp[-1], shp[-1], axis=2)
    small = replicated + sharded_small
    dl_f, m_f, v_f = adam_flat(*[_pack([src[n] if pre == '' else args[pre + n] for n in small], 8 * 128).reshape(-1, 128)
                                 for pre, src in (('', args), ('', local_g), ('m_', None), ('v_', None))], "adam_small")
    shapes = [args[n].shape for n in small]
    for dst, flat in ((out_d, dl_f), (out_m, m_f), (out_v, v_f)):
        dst.update(zip(small, _unpack(flat.reshape(-1), shapes)))
    out_g.update(local_g)

    return (loss, grad_x[None], *[out_g[n] for n in names], *[out_d[n] for n in names],
            *[out_m[n] for n in names], *[out_v[n] for n in names])
```

```python
import functools

import jax
import jax.numpy as jnp
from jax import lax
from jax.experimental import pallas as pl
from jax.experimental.pallas import tpu as pltpu

f32 = jnp.float32
bf16 = jnp.bfloat16

N_DEV = 8
EPS = 1e-6
CHUNK = 64
CHUNK_SHIFT = 6
assert 1 << CHUNK_SHIFT == CHUNK
MLA_HEADS = 16
Q_LORA = 512
KV_LORA = 512
NOPE = 128
ROPE = 64
V_HEAD = 128
ROPE_BASE = 10000.0
HEAD_PAD = 256
ML_HEADS = 8
ML_DK = 128
ML_DV = 256
ML_CONV = 4
ML_QK = ML_HEADS * ML_DK
ML_V = ML_HEADS * ML_DV
CR_HEADS = 4
CR_HD = 128
FFN_CONV = 3
ADAM_LR = 0.001
ADAM_B1 = 0.9
ADAM_B2 = 0.999
ADAM_EPS = 1e-08
ADAM_WD = 0.01
ADAM_STEP = 10
O_QA, O_KV, O_Q, O_K = 0, Q_LORA, Q_LORA + KV_LORA, Q_LORA + KV_LORA + ML_QK
O_V = O_K + ML_QK
O_O = O_V + ML_V
O_GA = O_O + ML_V
TAIL = 128
T_I, T_F = ROPE, ROPE + ML_HEADS
VMEM_LIMIT_V7X = 48 * 1024 * 1024
MESH = pl.DeviceIdType.MESH


def _call(body, name, grid, in_specs, out_specs, out_shape, scratch=(), exchange=None):
    params = pltpu.CompilerParams(vmem_limit_bytes=VMEM_LIMIT_V7X)
    if exchange is None:
        return pl.pallas_call(body, name=name, grid=grid, in_specs=in_specs, out_specs=out_specs, out_shape=out_shape,
                              scratch_shapes=list(scratch), compiler_params=params)
    arrs, scatter = exchange
    single = not isinstance(out_specs, (list, tuple))
    o_specs = [out_specs] if single else list(out_specs)
    o_shape = [out_shape] if single else list(out_shape)
    n_in, n_out, n_sc, n = len(in_specs), len(o_specs), len(scratch), len(arrs)
    any_spec = pl.BlockSpec(memory_space=pl.ANY)

    def body_with_exchange(*refs):
        pos = [0]

        def take(k):
            pos[0] += k
            return refs[pos[0] - k:pos[0]]

        ins, ex_in, outs, ex_out, sc = take(n_in), take(n), take(n_out), take(n), take(n_sc)
        start, middle, wait = _exchange_ops(ex_in, ex_out, refs[pos[0]:], scatter)
        step, total = 0, 1
        for a in range(len(grid)):
            step = step * grid[a] + pl.program_id(a)
            total *= grid[a]
        pl.when(step == 0)(start)
        body(*ins, *outs, *sc)
        if middle is not None:
            pl.when(step == total // 2)(middle)
        pl.when(step == total - 1)(wait)

    call = pl.pallas_call(body_with_exchange, name="comm_" + name, grid=grid, in_specs=list(in_specs) + [any_spec] * n,
                          out_specs=o_specs + [any_spec] * n, out_shape=o_shape + _exchange_shapes(arrs, scatter),
                          scratch_shapes=list(scratch) + _exchange_sems(n), compiler_params=params)

    def run(*operands):
        res = call(*operands, *arrs)
        return (res[0] if single else list(res[:n_out])), list(res[n_out:])

    return run


def _tile(n, cands):
    for c in cands:
        if n % c == 0:
            return c
    return n


def _sds(shape, dtype):
    return jax.ShapeDtypeStruct(tuple(shape), dtype)


def _bdot(a, b, ca, cb):
    return lax.dot_general(a.astype(bf16), b.astype(bf16), (((ca,), (cb,)), ((), ())), preferred_element_type=f32)


_BIG = (1024, 512, 256, 128)


def _col_tile(nb):
    return nb if nb <= 1536 else _tile(nb, _BIG)


_DEEP = (2048, 1024, 512, 256, 128)


def _mm_call(name, grid, in_specs, out_spec, out_shape, tile, nk, ca, cb, exchange, operands):
    def dot(a_ref, w_ref):
        return _bdot(a_ref[...], w_ref[0] if len(w_ref.shape) == 3 else w_ref[...], ca, cb)

    def store(o_ref, val):
        if len(o_ref.shape) == 3:
            o_ref[0] = val.astype(o_ref.dtype)
        else:
            o_ref[...] = val.astype(o_ref.dtype)

    if nk == 1:
        def body(a_ref, w_ref, o_ref):
            store(o_ref, dot(a_ref, w_ref))

        scratch = []
    else:
        def body(a_ref, w_ref, o_ref, acc):
            kk = pl.program_id(2)

            @pl.when(kk == 0)
            def _():
                acc[...] = jnp.zeros_like(acc)

            acc[...] += dot(a_ref, w_ref)

            @pl.when(kk == nk - 1)
            def _():
                store(o_ref, acc[...])

        scratch = [pltpu.VMEM(tile, f32)]
    return _call(body, name, grid, in_specs, out_spec, out_shape, scratch, exchange=exchange)(*operands)


def mm_nn(a, w3, out_dtype, name, exchange=None):
    m, k = a.shape
    nblk, k2, nb = w3.shape
    assert k == k2
    tm, tk, tn = _tile(m, _BIG), _tile(k, _DEEP), _col_tile(nb)
    per, nk = nb // tn, k // tk
    return _mm_call(name, (m // tm, nblk * per, nk),
                    [pl.BlockSpec((tm, tk), lambda i, j, kk: (i, kk)),
                     pl.BlockSpec((1, tk, tn), lambda i, j, kk: (j // per, kk, j % per))],
                    pl.BlockSpec((tm, tn), lambda i, j, kk: (i, j)), _sds((m, nblk * nb), out_dtype),
                    (tm, tn), nk, 1, 0, exchange, (a, w3))


def mm_nt(a, w3, out_dtype, name, exchange=None):
    m, n = a.shape
    nblk, k, nb = w3.shape
    assert n == nblk * nb
    tm, tn = _tile(m, _BIG), _tile(k, _BIG)
    tc = nb if nb <= 1536 else _tile(nb, _DEEP)
    per = nb // tc
    nk = nblk * per
    return _mm_call(name, (m // tm, k // tn, nk),
                    [pl.BlockSpec((tm, tc), lambda i, j, kk: (i, kk)),
                     pl.BlockSpec((1, tn, tc), lambda i, j, kk: (kk // per, j, kk % per))],
                    pl.BlockSpec((tm, tn), lambda i, j, kk: (i, j)), _sds((m, k), out_dtype),
                    (tm, tn), nk, 1, 1, exchange, (a, w3))


def mm_tn(a, b, nblk, name, exchange=None):
    r, m = a.shape
    r2, n = b.shape
    assert r == r2 and n % nblk == 0
    nb = n // nblk
    tm, tk, tn = _tile(m, _BIG), _tile(r, _DEEP), _col_tile(nb)
    per, nk = nb // tn, r // tk
    return _mm_call(name, (m // tm, nblk * per, nk),
                    [pl.BlockSpec((tk, tm), lambda i, j, kk: (kk, i)),
                     pl.BlockSpec((tk, tn), lambda i, j, kk: (kk, j))],
                    pl.BlockSpec((1, tm, tn), lambda i, j, kk: (j // per, i, j % per)), _sds((nblk, m, nb), bf16),
                    (tm, tn), nk, 0, 0, exchange, (a, b))


def _section_tiles(sections, cands):
    widths = [s.shape[1] for s in sections]
    tile = next(c for c in cands if all(w % c == 0 for w in widths))
    counts = [w // tile for w in widths]
    firsts = [sum(counts[:i]) for i in range(len(counts))]
    return tile, firsts, counts


SECTION_VMEM_BYTES = 24 * 1024 * 1024


def mm_tn_cols(a, sections, nblk, name, exchange=None):
    r, m = a.shape
    n = sum(s.shape[1] for s in sections)
    nb = n // nblk
    widths = [s.shape[1] for s in sections]
    col_cands = [c for c in ((nb,) if nb <= 1536 else _BIG[:2]) if all(w % c == 0 for w in widths)] or list(_BIG[2:])

    def deepest(c):
        return next(t for t in _DEEP if r % t == 0 and len(sections) * t * c * 4 <= SECTION_VMEM_BYTES)

    tn = max(col_cands, key=lambda c: (deepest(c), c))
    tn, firsts, counts = _section_tiles(sections, (tn,))
    per = nb // tn
    tm = _tile(m, _BIG)
    tk = deepest(tn)
    nk = r // tk

    def body(a_ref, *refs):
        b_refs, o_ref, acc = refs[:len(sections)], refs[len(sections)], refs[len(sections) + 1]
        j, kk = pl.program_id(1), pl.program_id(2)

        @pl.when(kk == 0)
        def _():
            acc[...] = jnp.zeros_like(acc)

        for b_ref, lo, cnt in zip(b_refs, firsts, counts):
            @pl.when(jnp.logical_and(j >= lo, j < lo + cnt))
            def _(b_ref=b_ref):
                acc[...] += _bdot(a_ref[...], b_ref[...], 0, 0)

        @pl.when(kk == nk - 1)
        def _():
            o_ref[0] = acc[...].astype(bf16)

    def spec(lo, cnt):
        def index(i, j, kk):
            return jnp.where(j < lo, 0, jnp.where(j >= lo + cnt, nk - 1, kk)), jnp.clip(j - lo, 0, cnt - 1)
        return pl.BlockSpec((tk, tn), index)

    return _call(body, name, (m // tm, n // tn, nk),
                 [pl.BlockSpec((tk, tm), lambda i, j, kk: (kk, i))] + [spec(lo, cnt) for lo, cnt in zip(firsts, counts)],
                 pl.BlockSpec((1, tm, tn), lambda i, j, kk: (j // per, i, j % per)), _sds((nblk, m, nb), bf16),
                 [pltpu.VMEM((tm, tn), f32)], exchange=exchange)(a, *sections)


def mm_nt_cols(sections, w3, out_dtype, name, exchange=None):
    m = sections[0].shape[0]
    nblk, k, nb = w3.shape
    tc, firsts, counts = _section_tiles(sections, (nb,) if nb <= 1536 else _DEEP)
    assert nblk * nb == tc * sum(counts) and nb % tc == 0
    per = nb // tc
    tm, tn = _tile(m, _BIG), _tile(k, _BIG)
    nk = nblk * per

    def body(*refs):
        a_refs, w_ref, o_ref, acc = refs[:len(sections)], refs[len(sections)], refs[len(sections) + 1], refs[len(sections) + 2]
        kk = pl.program_id(2)

        @pl.when(kk == 0)
        def _():
            acc[...] = jnp.zeros_like(acc)

        for a_ref, lo, cnt in zip(a_refs, firsts, counts):
            @pl.when(jnp.logical_and(kk >= lo, kk < lo + cnt))
            def _(a_ref=a_ref):
                acc[...] += _bdot(a_ref[...], w_ref[0], 1, 1)

        @pl.when(kk == nk - 1)
        def _():
            o_ref[...] = acc[...].astype(o_ref.dtype)

    def spec(lo, cnt):
        return pl.BlockSpec((tm, tc), lambda i, j, kk: (i, jnp.clip(kk - lo, 0, cnt - 1)))

    return _call(body, name, (m // tm, k // tn, nk),
                 [spec(lo, cnt) for lo, cnt in zip(firsts, counts)] + [pl.BlockSpec((1, tn, tc), lambda i, j, kk: (kk // per, j, kk % per))],
                 pl.BlockSpec((tm, tn), lambda i, j, kk: (i, j)), _sds((m, k), out_dtype),
                 [pltpu.VMEM((tm, tn), f32)], exchange=exchange)(*sections, w3)


def _rms(x, g):
    return x * lax.rsqrt(jnp.mean(x * x, axis=-1, keepdims=True) + EPS) * g


def _rms_pad(x, g, width):
    return x * lax.rsqrt(jnp.sum(x * x, axis=-1, keepdims=True) / width + EPS) * g


def _first(*ids):
    ok = ids[0] == 0
    for i in ids[1:]:
        ok = jnp.logical_and(ok, i == 0)
    return ok


def _acc_row(ref, val, first):
    @pl.when(first)
    def _():
        ref[...] = jnp.zeros_like(ref)

    ref[0:1, :] += val


def rms_fwd(x, g, name):
    r, w = x.shape
    tm = _tile(r, (256, 128, 64, 32, 16, 8))

    def body(x_ref, g_ref, o_ref):
        o_ref[...] = _rms(x_ref[...], g_ref[...]).astype(bf16)

    return _call(body, name, (r // tm,), [pl.BlockSpec((tm, w), lambda i: (i, 0)), pl.BlockSpec((1, w), lambda i: (0, 0))],
                 pl.BlockSpec((tm, w), lambda i: (i, 0)), _sds((r, w), bf16))(x, g)


def resid_rms(xa, xb, g, name):
    r, w = xa.shape
    tm = _tile(r, (256, 128, 64, 32, 16, 8))

    def body(a_ref, b_ref, g_ref, s_ref, u_ref):
        xs = a_ref[...] + b_ref[...]
        s_ref[...] = xs
        u_ref[...] = _rms(xs, g_ref[...]).astype(bf16)

    row = pl.BlockSpec((tm, w), lambda i: (i, 0))
    return _call(body, name, (r // tm,), [row, row, pl.BlockSpec((1, w), lambda i: (0, 0))], [row, row],
                 [_sds((r, w), f32), _sds((r, w), bf16)])(xa, xb, g)


def rms_bwd(x, g, dys, dres, name, want_dx=True, want_b16=False):
    r, w = x.shape
    tm = _tile(r, (256, 128, 64, 32, 16, 8))
    nd = len(dys)

    def body(*refs):
        x_ref, g_ref = refs[0], refs[1]
        dy = refs[2][...]
        for j in range(1, nd):
            dy = dy + refs[2 + j][...]
        pos = 2 + nd
        _, vjp = jax.vjp(_rms, x_ref[...], g_ref[...])
        dx, dg = vjp(dy)
        if dres is not None:
            dx = dx + refs[pos][...]
            pos += 1
        if want_dx:
            refs[pos][...] = dx
            pos += 1
        if want_b16:
            refs[pos][...] = dx.astype(bf16)
            pos += 1
        _acc_row(refs[pos], dg, pl.program_id(0) == 0)

    row = pl.BlockSpec((tm, w), lambda i: (i, 0))
    ins = [x, g] + list(dys) + ([dres] if dres is not None else [])
    in_specs = [row, pl.BlockSpec((1, w), lambda i: (0, 0))] + [row] * (nd + (dres is not None))
    out_specs = [row] * (want_dx + want_b16) + [pl.BlockSpec((8, w), lambda i: (0, 0))]
    out_shape = ([_sds((r, w), f32)] if want_dx else []) + ([_sds((r, w), bf16)] if want_b16 else []) + [_sds((8, w), f32)]
    return _call(body, name, (r // tm,), in_specs, out_specs, out_shape)(*ins)


def lat_norm(z_main, g_qa, g_kva):
    t = z_main.shape[0]
    tm = _tile(t, (512, 256, 128, 64))

    def body(z_ref, gq_ref, gk_ref, q_ref, k_ref):
        q_ref[...] = _rms(z_ref[:, :Q_LORA], gq_ref[...]).astype(bf16)
        k_ref[...] = _rms(z_ref[:, Q_LORA:], gk_ref[...]).astype(bf16)

    return _call(body, "lat_norm", (t // tm,),
                 [pl.BlockSpec((tm, Q_LORA + KV_LORA), lambda i: (i, 0)), pl.BlockSpec((1, Q_LORA), lambda i: (0, 0)),
                  pl.BlockSpec((1, KV_LORA), lambda i: (0, 0))],
                 [pl.BlockSpec((tm, Q_LORA), lambda i: (i, 0)), pl.BlockSpec((tm, KV_LORA), lambda i: (i, 0))],
                 [_sds((t, Q_LORA), bf16), _sds((t, KV_LORA), bf16)])(z_main, g_qa, g_kva)


def lat_norm_bwd(z_main, g_qa, g_kva, dqa, dkv):
    t = z_main.shape[0]
    tm = _tile(t, (512, 256, 128, 64))

    def body(z_ref, gq_ref, gk_ref, dq_ref, dk_ref, dz_ref, dgq_ref, dgk_ref):
        first = pl.program_id(0) == 0
        _, vq = jax.vjp(_rms, z_ref[:, :Q_LORA], gq_ref[...])
        dx, dg = vq(dq_ref[...])
        dz_ref[:, :Q_LORA] = dx.astype(bf16)
        _acc_row(dgq_ref, dg, first)
        _, vk = jax.vjp(_rms, z_ref[:, Q_LORA:], gk_ref[...])
        dx, dg = vk(dk_ref[...])
        dz_ref[:, Q_LORA:] = dx.astype(bf16)
        _acc_row(dgk_ref, dg, first)

    return _call(body, "lat_norm_bwd", (t // tm,),
                 [pl.BlockSpec((tm, Q_LORA + KV_LORA), lambda i: (i, 0)), pl.BlockSpec((1, Q_LORA), lambda i: (0, 0)),
                  pl.BlockSpec((1, KV_LORA), lambda i: (0, 0)), pl.BlockSpec((tm, Q_LORA), lambda i: (i, 0)),
                  pl.BlockSpec((tm, KV_LORA), lambda i: (i, 0))],
                 [pl.BlockSpec((tm, Q_LORA + KV_LORA), lambda i: (i, 0)), pl.BlockSpec((8, Q_LORA), lambda i: (0, 0)),
                  pl.BlockSpec((8, KV_LORA), lambda i: (0, 0))],
                 [_sds((t, Q_LORA + KV_LORA), bf16), _sds((8, Q_LORA), f32), _sds((8, KV_LORA), f32)])(z_main, g_qa, g_kva, dqa, dkv)


def rope_tables(pos_col, inv_freq):
    t = pos_col.shape[0]
    tm = _tile(t, (512, 256, 128, 64))

    def body(p_ref, f_ref, c_ref, s_ref):
        ang = p_ref[...].astype(f32) * f_ref[...]
        lane = lax.broadcasted_iota(jnp.int32, ang.shape, 1)
        c_ref[...] = jnp.where(lane < ROPE, jnp.cos(ang), 0.0)
        sn = jnp.sin(ang)
        s_ref[...] = jnp.where(lane < ROPE // 2, -sn, jnp.where(lane < ROPE, sn, 0.0))

    return _call(body, "rope_tables", (t // tm,),
                 [pl.BlockSpec((tm, 1), lambda i: (i, 0)), pl.BlockSpec((1, TAIL), lambda i: (0, 0))],
                 [pl.BlockSpec((tm, TAIL), lambda i: (i, 0))] * 2, [_sds((t, TAIL), f32)] * 2)(pos_col, inv_freq)


def _swap_halves(n):
    lane = lax.broadcasted_iota(jnp.int32, n.shape, 1)
    return jnp.where(lane < ROPE // 2, pltpu.roll(n, TAIL - ROPE // 2, 1), pltpu.roll(n, ROPE // 2, 1))


def _rope(n, c, s):
    return n * c + _swap_halves(n) * s


def _rope_t(d, c, s):
    return d * c + _swap_halves(d * s)


def _prep_specs(tm):
    head = pl.BlockSpec((tm, HEAD_PAD), lambda i, h: (i, h))
    row = pl.BlockSpec((tm, TAIL), lambda i, h: (i, 0))
    gain = pl.BlockSpec((1, TAIL), lambda i, h: (0, 0))
    return head, row, gain


def _pe_in(zt):
    lane = lax.broadcasted_iota(jnp.int32, zt.shape, 1)
    return jnp.where(lane < ROPE, zt, 0.0)


def mla_prep(q_raw, kv_raw, z_tail, cos, sin, gqn, gqp, gkn, gkp):
    t = q_raw.shape[0]
    tm = _tile(t, (1024, 512, 256, 128, 64))

    def body(q_ref, kv_ref, zt_ref, c_ref, s_ref, gqn_ref, gqp_ref, gkn_ref, gkp_ref, qh_ref, kh_ref, vh_ref):
        c, s = c_ref[...], s_ref[...]
        qh_ref[:, :NOPE] = _rms(q_ref[:, :NOPE], gqn_ref[...]).astype(bf16)
        qh_ref[:, NOPE:] = _rope(_rms_pad(q_ref[:, NOPE:], gqp_ref[...], ROPE), c, s).astype(bf16)
        kh_ref[:, :NOPE] = _rms(kv_ref[:, :NOPE], gkn_ref[...]).astype(bf16)
        kh_ref[:, NOPE:] = _rope(_rms_pad(_pe_in(zt_ref[...]), gkp_ref[...], ROPE), c, s).astype(bf16)
        vh_ref[...] = kv_ref[:, NOPE:].astype(bf16)

    head, row, gain = _prep_specs(tm)
    return _call(body, "mla_prep", (t // tm, MLA_HEADS), [head, head, row, row, row, gain, gain, gain, gain],
                 [head, head, pl.BlockSpec((tm, V_HEAD), lambda i, h: (i, h))],
                 [_sds((t, MLA_HEADS * HEAD_PAD), bf16), _sds((t, MLA_HEADS * HEAD_PAD), bf16), _sds((t, MLA_HEADS * V_HEAD), bf16)],
                 )(q_raw, kv_raw, z_tail, cos, sin, gqn, gqp, gkn, gkp)


def mla_prep_bwd(q_raw, kv_raw, z_tail, cos, sin, gqn, gqp, gkn, gkp, dqh, dkh, dvh):
    t = q_raw.shape[0]
    tm = _tile(t, (1024, 512, 256, 128, 64))
    pad_norm = functools.partial(_rms_pad, width=ROPE)

    def body(q_ref, kv_ref, zt_ref, c_ref, s_ref, gqn_ref, gqp_ref, gkn_ref, gkp_ref, dqh_ref, dkh_ref, dvh_ref,
             dq_ref, dkv_ref, dzt_ref, dgqn_ref, dgqp_ref, dgkn_ref, dgkp_ref):
        i, h = pl.program_id(0), pl.program_id(1)
        first = _first(i, h)
        c, s = c_ref[...], s_ref[...]
        _, v1 = jax.vjp(_rms, q_ref[:, :NOPE], gqn_ref[...])
        dx, dg = v1(dqh_ref[:, :NOPE])
        dq_ref[:, :NOPE] = dx.astype(bf16)
        _acc_row(dgqn_ref, dg, first)
        _, v2 = jax.vjp(pad_norm, q_ref[:, NOPE:], gqp_ref[...])
        dx, dg = v2(_rope_t(dqh_ref[:, NOPE:], c, s))
        dq_ref[:, NOPE:] = dx.astype(bf16)
        _acc_row(dgqp_ref, dg, first)
        _, v3 = jax.vjp(_rms, kv_ref[:, :NOPE], gkn_ref[...])
        dx, dg = v3(dkh_ref[:, :NOPE])
        dkv_ref[:, :NOPE] = dx.astype(bf16)
        _acc_row(dgkn_ref, dg, first)
        dkv_ref[:, NOPE:] = dvh_ref[...].astype(bf16)
        _, v4 = jax.vjp(pad_norm, _pe_in(zt_ref[...]), gkp_ref[...])
        dx, dg = v4(_rope_t(dkh_ref[:, NOPE:], c, s))
        _acc_row(dgkp_ref, dg, first)

        @pl.when(h == 0)
        def _():
            dzt_ref[...] = jnp.zeros_like(dzt_ref)

        dzt_ref[...] += dx

    head, row, gain = _prep_specs(tm)
    acc = pl.BlockSpec((8, TAIL), lambda i, h: (0, 0))
    vspec = pl.BlockSpec((tm, V_HEAD), lambda i, h: (i, h))
    return _call(body, "mla_prep_bwd", (t // tm, MLA_HEADS),
                 [head, head, row, row, row, gain, gain, gain, gain, head, head, vspec],
                 [head, head, row, acc, acc, acc, acc],
                 [_sds((t, MLA_HEADS * HEAD_PAD), bf16), _sds((t, MLA_HEADS * HEAD_PAD), bf16), _sds((t, TAIL), f32)]
                 + [_sds((8, TAIL), f32)] * 4)(q_raw, kv_raw, z_tail, cos, sin, gqn, gqp, gkn, gkp, dqh, dkh, dvh)


ATT_BLOCK = 512
NEG = -1e30
ATT_SCALE = (NOPE + ROPE) ** -0.5
LOG2_E = 1.4426950408889634
ATT_SCALE2 = ATT_SCALE * LOG2_E
ATT_HEADS = 2
ATT_HEADS_FWD = 4


def _chunk_visible(shape, key_axis):
    kc = lax.broadcasted_iota(jnp.int32, shape, key_axis) >> CHUNK_SHIFT
    qc = lax.broadcasted_iota(jnp.int32, shape, 1 - key_axis) >> CHUNK_SHIFT
    return kc <= qc


def mla_fwd(qh, kh, vh, exchange=None):
    t = qh.shape[0]
    tb = min(ATT_BLOCK, t)
    nb = t // tb

    hp = ATT_HEADS_FWD

    def body(q_ref, k_ref, v_ref, o_ref, ob_ref, lse_ref, m_s, l_s, acc):
        qi, ki = pl.program_id(1), pl.program_id(2)

        @pl.when(ki == 0)
        def _():
            m_s[...] = jnp.full_like(m_s, NEG)
            l_s[...] = jnp.zeros_like(l_s)
            acc[...] = jnp.zeros_like(acc)

        def step(diagonal):
            new = []
            for j in range(hp):
                q, k = q_ref[:, j * HEAD_PAD:(j + 1) * HEAD_PAD], k_ref[:, j * HEAD_PAD:(j + 1) * HEAD_PAD]
                s = _bdot(k, q, 1, 1) * ATT_SCALE2
                if diagonal:
                    s = jnp.where(_chunk_visible(s.shape, 0), s, -jnp.inf)
                m_old = m_s[j]
                m_new = jnp.maximum(m_old, jnp.max(s, axis=0, keepdims=True))
                p = jnp.exp2(s - m_new)
                alpha = jnp.exp2(m_old - m_new)
                l_new = alpha * l_s[j] + jnp.sum(p, axis=0, keepdims=True)
                acc_new = alpha * acc[j] + _bdot(v_ref[:, j * V_HEAD:(j + 1) * V_HEAD], p, 0, 0)
                new.append((m_new, l_new, acc_new))
            for j, (m_new, l_new, acc_new) in enumerate(new):
                m_s[j] = m_new
                l_s[j] = l_new
                acc[j] = acc_new
            return new

        @pl.when(ki < qi)
        def _():
            step(False)

        @pl.when(ki == qi)
        def _():
            for j, (m_new, l_new, acc_new) in enumerate(step(True)):
                o = (acc_new / l_new).T
                o_ref[:, j * V_HEAD:(j + 1) * V_HEAD] = o
                ob_ref[:, j * V_HEAD:(j + 1) * V_HEAD] = o.astype(bf16)
                lse_ref[j] = m_new + jnp.log2(l_new)

    kv = lambda g, qi, ki: (jnp.minimum(ki, qi), g)
    o_spec = pl.BlockSpec((tb, hp * V_HEAD), lambda g, qi, ki: (qi, g))
    return _call(body, "mla_fwd", (MLA_HEADS // hp, nb, nb),
                 [pl.BlockSpec((tb, hp * HEAD_PAD), lambda g, qi, ki: (qi, g)), pl.BlockSpec((tb, hp * HEAD_PAD), kv),
                  pl.BlockSpec((tb, hp * V_HEAD), kv)],
                 [o_spec, o_spec, pl.BlockSpec((hp, 1, tb), lambda g, qi, ki: (g, 0, qi))],
                 [_sds((t, MLA_HEADS * V_HEAD), f32), _sds((t, MLA_HEADS * V_HEAD), bf16), _sds((MLA_HEADS, 1, t), f32)],
                 [pltpu.VMEM((hp, 1, tb), f32), pltpu.VMEM((hp, 1, tb), f32), pltpu.VMEM((hp, V_HEAD, tb), f32)],
                 exchange=exchange)(qh, kh, vh)


def mla_delta(o, do):
    t = o.shape[0]
    tm = _tile(t, (512, 256, 128, 64))

    def body(o_ref, do_ref, d_ref):
        for h in range(MLA_HEADS):
            cols = slice(h * V_HEAD, (h + 1) * V_HEAD)
            d_ref[h] = jnp.sum(o_ref[:, cols] * do_ref[:, cols], axis=1, keepdims=True)

    blk = pl.BlockSpec((tm, MLA_HEADS * V_HEAD), lambda i: (i, 0))
    return _call(body, "mla_delta", (t // tm,), [blk, blk], pl.BlockSpec((MLA_HEADS, tm, 1), lambda i: (0, i, 0)),
                 _sds((MLA_HEADS, t, 1), f32))(o, do)


def mla_bwd(qh, kh, vh, do, lse_row, delta_row, exchange=None):
    t = qh.shape[0]
    tb = min(ATT_BLOCK, t)
    nb = t // tb

    hp = ATT_HEADS

    def body(q_ref, k_ref, v_ref, do_ref, lse_ref, dl_ref, dq_ref, dk_ref, dv_ref, dk_acc, dv_acc):
        ki, qi = pl.program_id(1), pl.program_id(2)

        @pl.when(jnp.logical_and(ki == 0, qi == 0))
        def _():
            dq_ref[...] = jnp.zeros_like(dq_ref)

        @pl.when(qi == 0)
        def _():
            dk_acc[...] = jnp.zeros_like(dk_acc)
            dv_acc[...] = jnp.zeros_like(dv_acc)

        def step(diagonal):
            rows = pl.ds(pl.multiple_of(qi * tb, tb), tb)
            new = []
            for j in range(hp):
                qc, vc = slice(j * HEAD_PAD, (j + 1) * HEAD_PAD), slice(j * V_HEAD, (j + 1) * V_HEAD)
                q, k, do_b = q_ref[:, qc], k_ref[:, qc], do_ref[:, vc]
                s = _bdot(k, q, 1, 1) * ATT_SCALE2
                if diagonal:
                    s = jnp.where(_chunk_visible(s.shape, 0), s, -jnp.inf)
                p = jnp.exp2(s - lse_ref[j])
                dp = _bdot(v_ref[:, vc], do_b, 1, 1)
                ds = p * (dp - dl_ref[j]) * ATT_SCALE
                new.append((dv_acc[:, vc] + _bdot(p, do_b, 1, 0), dk_acc[:, qc] + _bdot(ds, q, 1, 0),
                            dq_ref[rows, qc] + _bdot(ds, k, 0, 0)))
            for j, (dv, dk, dq) in enumerate(new):
                dv_acc[:, j * V_HEAD:(j + 1) * V_HEAD] = dv
                dk_acc[:, j * HEAD_PAD:(j + 1) * HEAD_PAD] = dk
                dq_ref[rows, j * HEAD_PAD:(j + 1) * HEAD_PAD] = dq

        @pl.when(qi > ki)
        def _():
            step(False)

        @pl.when(qi == ki)
        def _():
            step(True)

        @pl.when(qi == nb - 1)
        def _():
            dk_ref[...] = dk_acc[...]
            dv_ref[...] = dv_acc[...]

    qs = lambda g, ki, qi: (jnp.maximum(qi, ki), g)
    ks = lambda g, ki, qi: (ki, g)
    vec = pl.BlockSpec((hp, 1, tb), lambda g, ki, qi: (g, 0, jnp.maximum(qi, ki)))
    return _call(body, "mla_bwd", (MLA_HEADS // hp, nb, nb),
                 [pl.BlockSpec((tb, hp * HEAD_PAD), qs), pl.BlockSpec((tb, hp * HEAD_PAD), ks), pl.BlockSpec((tb, hp * V_HEAD), ks),
                  pl.BlockSpec((tb, hp * V_HEAD), qs), vec, vec],
                 [pl.BlockSpec((t, hp * HEAD_PAD), lambda g, ki, qi: (0, g)), pl.BlockSpec((tb, hp * HEAD_PAD), ks),
                  pl.BlockSpec((tb, hp * V_HEAD), ks)],
                 [_sds((t, MLA_HEADS * HEAD_PAD), f32), _sds((t, MLA_HEADS * HEAD_PAD), f32), _sds((t, MLA_HEADS * V_HEAD), f32)],
                 [pltpu.VMEM((tb, hp * HEAD_PAD), f32), pltpu.VMEM((tb, hp * V_HEAD), f32)], exchange=exchange)(
        qh, kh, vh, do, lse_row, delta_row)


PAD = 8


def _conv_taps(pad_ref, w, width, t):
    y = pad_ref[PAD - width + 1:PAD - width + 1 + t, :] * w[0:1, :]
    for j in range(1, width):
        y = y + pad_ref[PAD - width + 1 + j:PAD - width + 1 + j + t, :] * w[j:j + 1, :]
    return y


def _conv_bwd(xpad_ref, dpad_ref, w, da, width, t):
    dpad_ref[0:t, :] = da
    dpad_ref[t:t + PAD, :] = jnp.zeros((PAD, da.shape[1]), f32)
    dx = dpad_ref[width - 1:width - 1 + t, :] * w[0:1, :]
    for j in range(1, width):
        dx = dx + dpad_ref[width - 1 - j:width - 1 - j + t, :] * w[j:j + 1, :]
    dws = [jnp.sum(da * xpad_ref[PAD - width + 1 + j:PAD - width + 1 + j + t, :], axis=0, keepdims=True) for j in range(width)]
    return dx, dws


def _load_pad(pad_ref, x, t):
    pad_ref[0:PAD, :] = jnp.zeros((PAD, x.shape[1]), f32)
    pad_ref[PAD:PAD + t, :] = x


assert ML_DK == 128


def qk_conv(z_main, conv_qk):
    t = z_main.shape[0]
    base = O_Q // ML_DK

    def body(z_ref, w_ref, o_ref, pad):
        _load_pad(pad, z_ref[...], t)
        a = _conv_taps(pad, w_ref[...], ML_CONV, t)
        sc = jnp.where(pl.program_id(0) < ML_HEADS, ML_DK ** -0.5, 1.0)
        o_ref[0] = jax.nn.silu(a) * sc

    return _call(body, "qk_conv", (2 * ML_HEADS,),
                 [pl.BlockSpec((t, ML_DK), lambda j: (0, base + j)), pl.BlockSpec((ML_CONV, ML_DK), lambda j: (0, j))],
                 pl.BlockSpec((1, t, ML_DK), lambda j: (j, 0, 0)), _sds((2 * ML_HEADS, t, ML_DK), f32),
                 [pltpu.VMEM((t + PAD, ML_DK), f32)])(z_main, conv_qk)


def qk_conv_bwd(z_main, conv_qk, dq, dk):
    t = z_main.shape[0]
    base = O_Q // ML_DK

    def body(z_ref, w_ref, dq_ref, dk_ref, dz_ref, dw_ref, pad, dpad):
        _load_pad(pad, z_ref[...], t)
        w = w_ref[...]
        a = _conv_taps(pad, w, ML_CONV, t)
        is_q = pl.program_id(0) < ML_HEADS
        d = jnp.where(is_q, dq_ref[0] * (ML_DK ** -0.5), dk_ref[0])
        _, vjp = jax.vjp(jax.nn.silu, a)
        da, = vjp(d)
        dx, dws = _conv_bwd(pad, dpad, w, da, ML_CONV, t)
        dz_ref[...] = dx.astype(bf16)
        for j in range(ML_CONV):
            dw_ref[j:j + 1, :] = dws[j]

    head = lambda pick: pl.BlockSpec((1, t, ML_DK), lambda j: (pick(j), 0, 0))
    return _call(body, "qk_conv_bwd", (2 * ML_HEADS,),
                 [pl.BlockSpec((t, ML_DK), lambda j: (0, base + j)), pl.BlockSpec((ML_CONV, ML_DK), lambda j: (0, j)),
                  head(lambda j: jnp.minimum(j, ML_HEADS - 1)), head(lambda j: jnp.maximum(j - ML_HEADS, 0))],
                 [pl.BlockSpec((t, ML_DK), lambda j: (0, j)), pl.BlockSpec((ML_CONV, ML_DK), lambda j: (0, j))],
                 [_sds((t, 2 * ML_QK), bf16), _sds((ML_CONV, 2 * ML_QK), f32)],
                 [pltpu.VMEM((t + PAD, ML_DK), f32), pltpu.VMEM((t + PAD, ML_DK), f32)])(z_main, conv_qk, dq, dk)


def glu_fwd(hup, conv_w, bias):
    t, f2 = hup.shape
    nf = f2 // 2 // 128

    def body(h1_ref, h2_ref, w1_ref, w2_ref, b1_ref, b2_ref, o_ref, pad):
        _load_pad(pad, h1_ref[...], t)
        a1 = _conv_taps(pad, w1_ref[...], FFN_CONV, t) + b1_ref[...]
        _load_pad(pad, h2_ref[...], t)
        a2 = _conv_taps(pad, w2_ref[...], FFN_CONV, t) + b2_ref[...]
        o_ref[...] = (jax.nn.silu(a1) * a2).astype(bf16)

    col = lambda off: pl.BlockSpec((t, 128), lambda j: (0, j + off))
    wsp = lambda off: pl.BlockSpec((FFN_CONV, 128), lambda j: (0, j + off))
    bsp = lambda off: pl.BlockSpec((1, 128), lambda j: (0, j + off))
    return _call(body, "glu_fwd", (nf,), [col(0), col(nf), wsp(0), wsp(nf), bsp(0), bsp(nf)], col(0), _sds((t, f2 // 2), bf16),
                 [pltpu.VMEM((t + PAD, 128), f32)])(hup, hup, conv_w, conv_w, bias, bias)


def glu_bwd(hup, conv_w, bias, dg, exchange=None):
    t, f2 = hup.shape
    f = f2 // 2
    nf = f // 128

    def body(h1_ref, h2_ref, w1_ref, w2_ref, b1_ref, b2_ref, dg_ref, dh1_ref, dh2_ref, dw1_ref, dw2_ref, db1_ref, db2_ref,
             pad1, pad2, dpad):
        _load_pad(pad1, h1_ref[...], t)
        _load_pad(pad2, h2_ref[...], t)
        w1, w2 = w1_ref[...], w2_ref[...]
        a1 = _conv_taps(pad1, w1, FFN_CONV, t) + b1_ref[...]
        a2 = _conv_taps(pad2, w2, FFN_CONV, t) + b2_ref[...]
        d = dg_ref[...]
        _, vjp = jax.vjp(jax.nn.silu, a1)
        da1, = vjp(d * a2)
        da2 = d * jax.nn.silu(a1)
        for da, pad, w, dh_ref, dw_ref, db_ref in ((da1, pad1, w1, dh1_ref, dw1_ref, db1_ref), (da2, pad2, w2, dh2_ref, dw2_ref, db2_ref)):
            dx, dws = _conv_bwd(pad, dpad, w, da, FFN_CONV, t)
            dh_ref[...] = dx.astype(bf16)
            for j in range(FFN_CONV):
                dw_ref[j:j + 1, :] = dws[j]
            db_ref[...] = jnp.sum(da, axis=0, keepdims=True)

    col = lambda off: pl.BlockSpec((t, 128), lambda j: (0, j + off))
    wsp = lambda off: pl.BlockSpec((FFN_CONV, 128), lambda j: (0, j + off))
    bsp = lambda off: pl.BlockSpec((1, 128), lambda j: (0, j + off))
    return _call(body, "glu_bwd", (nf,), [col(0), col(nf), wsp(0), wsp(nf), bsp(0), bsp(nf), col(0)],
                 [col(0), col(0), wsp(0), wsp(0), bsp(0), bsp(0)],
                 [_sds((t, f), bf16)] * 2 + [_sds((FFN_CONV, f), f32)] * 2 + [_sds((1, f), f32)] * 2,
                 [pltpu.VMEM((t + PAD, 128), f32)] * 3, exchange=exchange)(hup, hup, conv_w, conv_w, bias, bias, dg)


def gate_act(z_tail, b_tile):
    t = z_tail.shape[0]
    tm = _tile(t, (512, 256, 128, 64))

    def body(z_ref, b_ref, o_ref):
        x = z_ref[...] + b_ref[...]
        lane = lax.broadcasted_iota(jnp.int32, x.shape, 1)
        o_ref[...] = jnp.where(lane < T_F, x, jax.nn.log_sigmoid(x))

    row = pl.BlockSpec((tm, TAIL), lambda i: (i, 0))
    return _call(body, "gate_act", (t // tm,), [row, pl.BlockSpec((1, TAIL), lambda i: (0, 0))], row, _sds((t, TAIL), f32))(z_tail, b_tile)


def tail_bwd(z_tail, b_tile, dzt_pe, dgate):
    t = z_tail.shape[0]
    tm = _tile(t, (512, 256, 128, 64))

    def body(z_ref, b_ref, dpe_ref, dg_ref, dz_ref, db_ref):
        x = z_ref[...] + b_ref[...]
        lane = lax.broadcasted_iota(jnp.int32, x.shape, 1)
        _, vjp = jax.vjp(jax.nn.log_sigmoid, x)
        df, = vjp(dg_ref[...])
        dgates = jnp.where(lane < T_F, dg_ref[...], df)
        dgates = jnp.where(jnp.logical_and(lane >= T_I, lane < T_F + ML_HEADS), dgates, 0.0)
        dz_ref[...] = jnp.where(lane < ROPE, dpe_ref[...], dgates).astype(bf16)
        _acc_row(db_ref, jnp.sum(dgates, axis=0, keepdims=True), pl.program_id(0) == 0)

    row = pl.BlockSpec((tm, TAIL), lambda i: (i, 0))
    return _call(body, "tail_bwd", (t // tm,), [row, pl.BlockSpec((1, TAIL), lambda i: (0, 0)), row, row],
                 [row, pl.BlockSpec((8, TAIL), lambda i: (0, 0))], [_sds((t, TAIL), bf16), _sds((8, TAIL), f32)])(z_tail, b_tile, dzt_pe, dgate)


def _hdot(a, b, ca, cb):
    return lax.dot_general(a.astype(bf16), b.astype(bf16), (((ca,), (cb,)), ((0,), (0,))), preferred_element_type=f32)


def _mlstm_step(q, k, v, igr, fgr, c_mat, n_vec, m):
    nh, ln = q.shape[0], CHUNK
    sq = (nh, ln, ln)
    row = lax.broadcasted_iota(jnp.int32, sq, 1)
    col = lax.broadcasted_iota(jnp.int32, sq, 2)
    eye = row == col

    def to_col(r):
        return jnp.sum(jnp.where(eye, jnp.broadcast_to(r, sq), 0.0), axis=2, keepdims=True)

    bc_r = jnp.sum(jnp.where(row <= col, jnp.broadcast_to(to_col(fgr), sq), 0.0), axis=1, keepdims=True)
    bc_c = to_col(bc_r)
    logw = jnp.where(col <= row, bc_c - bc_r + igr, -jnp.inf)
    inter = bc_c + m
    m_t = jnp.maximum(inter, jnp.max(logw, axis=2, keepdims=True))
    w_intra = jnp.exp(logw - m_t)
    w_inter = jnp.exp(inter - m_t)
    sc = _hdot(q, k, 2, 2) * w_intra
    num = w_inter * _hdot(q, c_mat, 2, 1) + _hdot(sc, v, 2, 1)
    qn = jnp.sum(q.astype(bf16).astype(f32) * n_vec.astype(bf16).astype(f32), axis=2, keepdims=True)
    den = w_inter * qn + jnp.sum(sc, axis=2, keepdims=True)
    h = num / jnp.maximum(jnp.abs(den), jnp.exp(-m_t))
    lane = lax.broadcasted_iota(jnp.int32, (nh, 1, ln), 2)
    b_last = jnp.sum(jnp.where(lane == ln - 1, bc_r, 0.0), axis=2, keepdims=True)
    logu = b_last - bc_r + igr
    m_new = jnp.maximum(b_last + m, jnp.max(logu, axis=2, keepdims=True))
    decay = jnp.exp(b_last + m - m_new)
    u_c = to_col(jnp.exp(logu - m_new))
    c_new = decay * c_mat + _hdot(u_c * k, v, 1, 1)
    n_new = decay * n_vec + jnp.sum(u_c.astype(bf16).astype(f32) * k.astype(bf16).astype(f32), axis=1, keepdims=True)
    return h, c_new, n_new, m_new


ML_VHALF = ML_V // 2
assert O_V % ML_VHALF == 0 and ML_HEADS % 2 == 0


def _ml_specs(nc, rev):
    cc = (lambda c: nc - 1 - c) if rev else (lambda c: c)
    q = pl.BlockSpec((ML_HEADS, CHUNK, ML_DK), lambda c: (0, cc(c), 0))
    k = pl.BlockSpec((ML_HEADS, CHUNK, ML_DK), lambda c: (1, cc(c), 0))
    v_lo = pl.BlockSpec((CHUNK, ML_VHALF), lambda c: (cc(c), O_V // ML_VHALF))
    v_hi = pl.BlockSpec((CHUNK, ML_VHALF), lambda c: (cc(c), O_V // ML_VHALF + 1))
    hv = pl.BlockSpec((ML_HEADS, CHUNK, ML_DV), lambda c: (0, cc(c), 0))
    gate = pl.BlockSpec((ML_HEADS, 1, 1, CHUNK), lambda c: (0, cc(c), 0, 0))
    cm = pl.BlockSpec((ML_HEADS, 1, ML_DK, ML_DV), lambda c: (0, cc(c), 0, 0))
    nv = pl.BlockSpec((ML_HEADS, 1, 1, ML_DK), lambda c: (0, cc(c), 0, 0))
    ms = pl.BlockSpec((ML_HEADS, 1, 1, 1), lambda c: (0, cc(c), 0, 0))
    return q, k, v_lo, v_hi, hv, gate, cm, nv, ms


_ML_STATE = [pltpu.VMEM((ML_HEADS, ML_DK, ML_DV), f32), pltpu.VMEM((ML_HEADS, 1, ML_DK), f32), pltpu.VMEM((ML_HEADS, 1, 1), f32)]


def _ml_zero_state(c_s, n_s, m_s):
    @pl.when(pl.program_id(0) == 0)
    def _():
        c_s[...] = jnp.zeros_like(c_s)
        n_s[...] = jnp.zeros_like(n_s)
        m_s[...] = jnp.zeros_like(m_s)


def _ml_heads_of(v_lo_ref, v_hi_ref):
    half = ML_HEADS // 2
    return jnp.stack([r[:, j * ML_DV:(j + 1) * ML_DV] for r in (v_lo_ref, v_hi_ref) for j in range(half)])


def mlstm_fwd(qk_act, z_main, ig, fg):
    t = qk_act.shape[1]
    nc = t // CHUNK

    def body(q_ref, k_ref, vl_ref, vh_ref, ig_ref, fg_ref, h_ref, c_out, n_out, m_out, c_s, n_s, m_s):
        _ml_zero_state(c_s, n_s, m_s)
        c0, n0, m0 = c_s[...], n_s[...], m_s[...]
        c_out[:, 0] = c0
        n_out[:, 0] = n0
        m_out[:, 0] = m0
        h, c2, n2, m2 = _mlstm_step(q_ref[...], k_ref[...], _ml_heads_of(vl_ref, vh_ref), ig_ref[:, 0], fg_ref[:, 0], c0, n0, m0)
        h_ref[...] = h
        c_s[...] = c2
        n_s[...] = n2
        m_s[...] = m2

    q, k, v_lo, v_hi, hv, gate, cm, nv, ms = _ml_specs(nc, False)
    return _call(body, "mlstm_fwd", (nc,), [q, k, v_lo, v_hi, gate, gate], [hv, cm, nv, ms],
                 [_sds((ML_HEADS, t, ML_DV), f32), _sds((ML_HEADS, nc, ML_DK, ML_DV), f32), _sds((ML_HEADS, nc, 1, ML_DK), f32),
                  _sds((ML_HEADS, nc, 1, 1), f32)], _ML_STATE)(qk_act, qk_act, z_main, z_main, ig, fg)


def mlstm_bwd(qk_act, z_main, ig, fg, c_all, n_all, m_all, dh, exchange=None):
    t = qk_act.shape[1]
    nc = t // CHUNK

    def body(q_ref, k_ref, vl_ref, vh_ref, ig_ref, fg_ref, c_ref, n_ref, m_ref, dh_ref, dq_ref, dk_ref, dv_ref, dig_ref, dfg_ref,
             dc_s, dn_s, dm_s):
        _ml_zero_state(dc_s, dn_s, dm_s)
        _, vjp = jax.vjp(_mlstm_step, q_ref[...], k_ref[...], _ml_heads_of(vl_ref, vh_ref), ig_ref[:, 0], fg_ref[:, 0],
                         c_ref[:, 0], n_ref[:, 0], m_ref[:, 0])
        dq, dk, dv, dig, dfg, dc, dn, dm = vjp((dh_ref[...], dc_s[...], dn_s[...], dm_s[...]))
        dq_ref[...] = dq
        dk_ref[...] = dk
        for j in range(ML_HEADS):
            dv_ref[:, j * ML_DV:(j + 1) * ML_DV] = dv[j].astype(bf16)
        dig_ref[:, 0] = dig
        dfg_ref[:, 0] = dfg
        dc_s[...] = dc
        dn_s[...] = dn
        dm_s[...] = dm

    q, k, v_lo, v_hi, hv, gate, cm, nv, ms = _ml_specs(nc, True)
    gshape = _sds((ML_HEADS, nc, 1, CHUNK), f32)
    return _call(body, "mlstm_bwd", (nc,), [q, k, v_lo, v_hi, gate, gate, cm, nv, ms, hv],
                 [q, q, pl.BlockSpec((CHUNK, ML_V), lambda c: (nc - 1 - c, 0)), gate, gate],
                 [_sds((ML_HEADS, t, ML_DK), f32), _sds((ML_HEADS, t, ML_DK), f32), _sds((t, ML_V), bf16), gshape, gshape],
                 _ML_STATE, exchange=exchange)(qk_act, qk_act, z_main, z_main, ig, fg, c_all, n_all, m_all, dh)


def _ml_out(h, zo, g):
    return _rms(h, g) * jax.nn.sigmoid(zo)


def mlstm_out(h, z_main, g_hnorm):
    t = h.shape[1]
    tm = _tile(t, (512, 256, 128, 64))
    zo = O_O // ML_DV

    def body(h_ref, z_ref, g_ref, y_ref):
        y_ref[...] = _ml_out(h_ref[0], z_ref[...], g_ref[0]).astype(bf16)

    return _call(body, "mlstm_out", (t // tm, ML_HEADS),
                 [pl.BlockSpec((1, tm, ML_DV), lambda i, hd: (hd, i, 0)), pl.BlockSpec((tm, ML_DV), lambda i, hd: (i, zo + hd)),
                  pl.BlockSpec((1, 1, ML_DV), lambda i, hd: (hd, 0, 0))],
                 pl.BlockSpec((tm, ML_DV), lambda i, hd: (i, hd)), _sds((t, ML_V), bf16))(h, z_main, g_hnorm)


def mlstm_out_bwd(h, z_main, g_hnorm, dy):
    t = h.shape[1]
    tm = _tile(t, (512, 256, 128, 64))
    zo = O_O // ML_DV

    def body(h_ref, z_ref, g_ref, dy_ref, dh_ref, dzo_ref, dg_ref):
        _, vjp = jax.vjp(_ml_out, h_ref[0], z_ref[...], g_ref[0])
        dh, dz, dg = vjp(dy_ref[...])
        dh_ref[0] = dh
        dzo_ref[...] = dz.astype(bf16)

        @pl.when(pl.program_id(1) == 0)
        def _():
            dg_ref[...] = jnp.zeros_like(dg_ref)

        dg_ref[0, 0:1, :] += dg

    head = pl.BlockSpec((1, tm, ML_DV), lambda hd, i: (hd, i, 0))
    blk = pl.BlockSpec((tm, ML_DV), lambda hd, i: (i, hd))
    return _call(body, "mlstm_out_bwd", (ML_HEADS, t // tm),
                 [head, pl.BlockSpec((tm, ML_DV), lambda hd, i: (i, zo + hd)), pl.BlockSpec((1, 1, ML_DV), lambda hd, i: (hd, 0, 0)), blk],
                 [head, blk, pl.BlockSpec((1, 8, ML_DV), lambda hd, i: (hd, 0, 0))],
                 [_sds((ML_HEADS, t, ML_DV), f32), _sds((t, ML_V), bf16), _sds((ML_HEADS, 8, ML_DV), f32)])(h, z_main, g_hnorm, dy)


def _merge(ga, gb, ya, yb):
    return jax.nn.sigmoid(ga) * ya + jax.nn.sigmoid(gb) * yb


def _merge_specs(t, d):
    tm = _tile(t, (512, 256, 128, 64))
    bw = _tile(d, (512, 256, 128))
    assert O_GA % bw == 0 and (O_GA + d) % bw == 0
    blk = pl.BlockSpec((tm, bw), lambda i, j: (i, j))
    ga = pl.BlockSpec((tm, bw), lambda i, j: (i, O_GA // bw + j))
    gb = pl.BlockSpec((tm, bw), lambda i, j: (i, (O_GA + d) // bw + j))
    return tm, bw, blk, ga, gb


def merge_fwd(z_main, ya, yb):
    t, d = ya.shape
    tm, bw, blk, ga, gb = _merge_specs(t, d)

    def body(ga_ref, gb_ref, ya_ref, yb_ref, o_ref):
        o_ref[...] = _merge(ga_ref[...], gb_ref[...], ya_ref[...], yb_ref[...]).astype(bf16)

    return _call(body, "merge_fwd", (t // tm, d // bw), [ga, gb, blk, blk], blk, _sds((t, d), bf16))(z_main, z_main, ya, yb)


def merge_bwd(z_main, ya, yb, dmerged):
    t, d = ya.shape
    tm, bw, blk, ga, gb = _merge_specs(t, d)

    def body(ga_ref, gb_ref, ya_ref, yb_ref, dm_ref, dga_ref, dgb_ref, dya_ref, dyb_ref):
        _, vjp = jax.vjp(_merge, ga_ref[...], gb_ref[...], ya_ref[...], yb_ref[...])
        dga, dgb, dya, dyb = vjp(dm_ref[...])
        dga_ref[...] = dga.astype(bf16)
        dgb_ref[...] = dgb.astype(bf16)
        dya_ref[...] = dya.astype(bf16)
        dyb_ref[...] = dyb.astype(bf16)

    return _call(body, "merge_bwd", (t // tm, d // bw), [ga, gb, blk, blk, blk], [blk] * 4, [_sds((t, d), bf16)] * 4)(
        z_main, z_main, ya, yb, dmerged)


def _cross(cq, ck, cv, gq, gk):
    outs = []
    for hd in range(CR_HEADS):
        sl = slice(hd * CR_HD, (hd + 1) * CR_HD)
        q = _rms(cq[:, sl], gq)
        k = _rms(ck[:, sl], gk)
        s = _bdot(q, k, 1, 1) * (CR_HD ** -0.5)
        p = jax.nn.softmax(s, axis=-1)
        outs.append(_bdot(p, cv[:, sl], 1, 0))
    return jnp.concatenate(outs, axis=1)


def cross_fwd(cq, ck, cv, gq, gk):
    t, w = cq.shape
    nm = ck.shape[0]
    tm = _tile(t, (512, 256, 128, 64))

    def body(q_ref, k_ref, v_ref, gq_ref, gk_ref, o_ref):
        o_ref[...] = _cross(q_ref[...], k_ref[...], v_ref[...], gq_ref[...], gk_ref[...]).astype(bf16)

    row = pl.BlockSpec((tm, w), lambda i: (i, 0))
    full = pl.BlockSpec((nm, w), lambda i: (0, 0))
    gain = pl.BlockSpec((1, CR_HD), lambda i: (0, 0))
    return _call(body, "cross_fwd", (t // tm,), [row, full, full, gain, gain], row, _sds((t, w), bf16))(cq, ck, cv, gq, gk)


def cross_bwd(cq, ck, cv, gq, gk, do):
    t, w = cq.shape
    nm = ck.shape[0]
    tm = _tile(t, (512, 256, 128, 64))

    def body(q_ref, k_ref, v_ref, gq_ref, gk_ref, do_ref, dq_ref, dk_ref, dv_ref, dgq_ref, dgk_ref):
        first = pl.program_id(0) == 0
        _, vjp = jax.vjp(_cross, q_ref[...], k_ref[...], v_ref[...], gq_ref[...], gk_ref[...])
        dq, dk, dv, dgq, dgk = vjp(do_ref[...])
        dq_ref[...] = dq.astype(bf16)

        @pl.when(first)
        def _():
            dk_ref[...] = jnp.zeros_like(dk_ref)
            dv_ref[...] = jnp.zeros_like(dv_ref)

        dk_ref[...] += dk
        dv_ref[...] += dv
        _acc_row(dgq_ref, dgq, first)
        _acc_row(dgk_ref, dgk, first)

    row = pl.BlockSpec((tm, w), lambda i: (i, 0))
    full = pl.BlockSpec((nm, w), lambda i: (0, 0))
    gain = pl.BlockSpec((1, CR_HD), lambda i: (0, 0))
    acc = pl.BlockSpec((8, CR_HD), lambda i: (0, 0))
    return _call(body, "cross_bwd", (t // tm,), [row, full, full, gain, gain, row], [row, full, full, acc, acc],
                 [_sds((t, w), bf16), _sds((nm, w), f32), _sds((nm, w), f32), _sds((8, CR_HD), f32), _sds((8, CR_HD), f32)])(
        cq, ck, cv, gq, gk, do)


def loss_head(x2, fo, target):
    t, d = x2.shape
    tm = _tile(t, (256, 128, 64, 32, 16, 8))

    def body(a_ref, b_ref, t_ref, dx_ref, dxb_ref, l_ref):
        err = a_ref[...] + b_ref[...] - t_ref[...]
        dx = err / d
        dx_ref[...] = dx
        dxb_ref[...] = dx.astype(bf16)
        part = 0.5 * jnp.sum(jnp.mean(err * err, axis=1, keepdims=True), axis=0, keepdims=True)
        _acc_row(l_ref, jnp.broadcast_to(part, (1, 128)), pl.program_id(0) == 0)

    row = pl.BlockSpec((tm, d), lambda i: (i, 0))
    return _call(body, "loss_head", (t // tm,), [row, row, row], [row, row, pl.BlockSpec((8, 128), lambda i: (0, 0))],
                 [_sds((t, d), f32), _sds((t, d), bf16), _sds((8, 128), f32)])(x2, fo, target)


def _place():
    x, y, c = lax.axis_index("x"), lax.axis_index("y"), lax.axis_index("c")
    peers = {}
    for r in range(1, N_DEV):
        px = 1 - x if r & 4 else x
        py = 1 - y if r & 2 else y
        pc = 1 - c if r & 1 else c
        peers[r] = ((px, py, pc), 4 * px + 2 * py + pc)
    return 4 * x + 2 * y + c, peers


N_REL = N_DEV - 1
RELATIONS = tuple(range(1, N_DEV))
SIBLING = 1
OTHER_CHIPS = (2, 4, 6)
PASSED_ON = (3, 5, 7)


def _exchange_ops(ins, outs, sems, scatter):
    n = len(ins)
    send_sems, recv_sems, local_sems = sems

    def tools():
        me, peers = _place()

        def copy(a, r, src, dst_idx, to):
            return pltpu.make_async_remote_copy(
                src_ref=src, dst_ref=outs[a].at[dst_idx], send_sem=send_sems.at[a * N_REL + r - 1],
                recv_sem=recv_sems.at[a * N_REL + r - 1], device_id=peers[to][0], device_id_type=MESH)

        def local(a):
            return pltpu.make_async_copy(ins[a].at[me] if scatter else ins[a], outs[a].at[me], local_sems.at[a])

        def arrival(a, r):
            return copy(a, r, ins[a].at[me] if scatter else ins[a], peers[r][1], r)

        return me, peers, copy, local, arrival

    if scatter:
        def sends():
            me, peers, copy, local, _ = tools()
            return [local(a) for a in range(n)], [copy(a, r, ins[a].at[peers[r][1]], me, r) for a in range(n) for r in RELATIONS]

        def start():
            loc, out = sends()
            for cp in loc + out:
                cp.start()

        middle = None
        waited_last = RELATIONS
    else:
        def sends():
            me, peers, copy, local, _ = tools()
            own = [copy(a, r, ins[a], me, r) for a in range(n) for r in (SIBLING,) + OTHER_CHIPS]
            return [local(a) for a in range(n)], own

        def passes():
            me, peers, copy, _, _ = tools()
            return [copy(a, r, outs[a].at[peers[r - 1][1]], peers[r - 1][1], SIBLING) for a in range(n) for r in PASSED_ON]

        def start():
            loc, out = sends()
            for cp in loc + out:
                cp.start()

        def middle():
            _, _, _, _, arrival = tools()
            fwd = passes()
            for a in range(n):
                for i, r in enumerate(PASSED_ON):
                    arrival(a, r - 1).wait_recv()
                    fwd[a * len(PASSED_ON) + i].start()

        waited_last = (SIBLING,) + PASSED_ON

    def wait():
        _, _, _, _, arrival = tools()
        for a in range(n):
            for r in waited_last:
                arrival(a, r).wait_recv()
        loc, out = sends()
        for cp in out + ([] if scatter else passes()):
            cp.wait_send()
        for cp in loc:
            cp.wait()

    return start, middle, wait


def _exchange_shapes(arrs, scatter):
    return [_sds(a.shape if scatter else (N_DEV,) + a.shape, a.dtype) for a in arrs]


def _exchange_sems(n):
    return [pltpu.SemaphoreType.DMA((n * N_REL,)), pltpu.SemaphoreType.DMA((n * N_REL,)), pltpu.SemaphoreType.DMA((n,))]


def _exchange(arrs, name, scatter):
    n = len(arrs)

    def body(*refs):
        start, middle, wait = _exchange_ops(refs[:n], refs[n:2 * n], refs[2 * n:], scatter)
        start()
        if middle is not None:
            middle()
        wait()

    any_spec = pl.BlockSpec(memory_space=pl.ANY)
    return pl.pallas_call(body, name=name, in_specs=[any_spec] * n, out_specs=[any_spec] * n,
                          out_shape=_exchange_shapes(arrs, scatter), scratch_shapes=_exchange_sems(n))(*arrs)


def cast_bf16(w, name):
    _, r, c = w.shape
    tr = _tile(r, (256, 128, 64, 32, 16))

    def body(w_ref, o_ref):
        o_ref[...] = w_ref[0].astype(bf16)

    return _call(body, name, (r // tr,), [pl.BlockSpec((1, tr, c), lambda i: (0, i, 0))], pl.BlockSpec((tr, c), lambda i: (i, 0)),
                 _sds((r, c), bf16))(w)


def _adamw(w, g, m, v):
    m = ADAM_B1 * m + (1.0 - ADAM_B1) * g
    v = ADAM_B2 * v + (1.0 - ADAM_B2) * jnp.square(g)
    m_hat = m / (1.0 - ADAM_B1 ** ADAM_STEP)
    v_hat = v / (1.0 - ADAM_B2 ** ADAM_STEP)
    delta = -ADAM_LR * (m_hat / (jnp.sqrt(v_hat) + ADAM_EPS) + ADAM_WD * w)
    return delta, m, v


def adam_sum(parts, w, m, v, name):
    _, r, c = parts.shape
    budget = 4 * 1024 * 1024
    tr = r
    for cand in (1024, 512, 256, 128, 64, 32, 16):
        if r % cand == 0 and N_DEV * cand * c * 4 <= budget:
            tr = cand
            break

    def body(p_ref, w_ref, m_ref, v_ref, g_ref, d_ref, m2_ref, v2_ref):
        g = p_ref[0].astype(f32)
        for k in range(1, N_DEV):
            g = g + p_ref[k].astype(f32)
        d, m2, v2 = _adamw(w_ref[0], g, m_ref[0], v_ref[0])
        g_ref[...] = g
        d_ref[...] = d
        m2_ref[...] = m2
        v2_ref[...] = v2

    blk = pl.BlockSpec((1, tr, c), lambda i: (0, i, 0))
    out = pl.BlockSpec((tr, c), lambda i: (i, 0))
    return _call(body, name, (r // tr,), [pl.BlockSpec((N_DEV, tr, c), lambda i: (0, i, 0)), blk, blk, blk], [out] * 4,
                 [_sds((r, c), f32)] * 4)(parts, w, m, v)


def sum_parts(parts, name):
    _, r, c = parts.shape

    def body(p_ref, o_ref):
        g = p_ref[0]
        for k in range(1, N_DEV):
            g = g + p_ref[k]
        o_ref[...] = g

    return pl.pallas_call(body, name=name, out_shape=_sds((r, c), f32))(parts)


def adam_flat(w, g, m, v, name):
    def body(w_ref, g_ref, m_ref, v_ref, d_ref, m2_ref, v2_ref):
        d, m2, v2 = _adamw(w_ref[...], g_ref[...], m_ref[...], v_ref[...])
        d_ref[...] = d
        m2_ref[...] = m2
        v2_ref[...] = v2

    return pl.pallas_call(body, name=name, out_shape=[_sds(w.shape, f32)] * 3)(w, g, m, v)


def _pack(vecs, multiple):
    flat = jnp.concatenate([v.reshape(-1) for v in vecs])
    n = flat.shape[0]
    total = -(-n // multiple) * multiple
    return jnp.pad(flat, (0, total - n))


def _unpack(flat, shapes):
    out, pos = [], 0
    for s in shapes:
        n = 1
        for d in s:
            n *= d
        out.append(flat[pos:pos + n].reshape(s))
        pos += n
    return out


def _pad_lanes(v, width=TAIL):
    return jnp.pad(v, ((0, 0), (0, width - v.shape[1])))


def kernel(x, mem, positions, g_mix, w_in, g_qa, w_qb, g_kva, w_kvb, g_qn_nope, g_qn_pe, g_kn_nope, g_kn_pe, conv_qk, b_if, g_hnorm, p_a, p_b, w_out, g_cross, g_mem, wq_c, wk_c, wv_c, g_cq, g_ck, wo_c, g_ffn, w_up, conv_ffn, b_conv_ffn, w_down, loss_target, m_g_mix, m_w_in, m_g_qa, m_w_qb, m_g_kva, m_w_kvb, m_g_qn_nope, m_g_qn_pe, m_g_kn_nope, m_g_kn_pe, m_conv_qk, m_b_if, m_g_hnorm, m_p_a, m_p_b, m_w_out, m_g_cross, m_g_mem, m_wq_c, m_wk_c, m_wv_c, m_g_cq, m_g_ck, m_wo_c, m_g_ffn, m_w_up, m_conv_ffn, m_b_conv_ffn, m_w_down, v_g_mix, v_w_in, v_g_qa, v_w_qb, v_g_kva, v_w_kvb, v_g_qn_nope, v_g_qn_pe, v_g_kn_nope, v_g_kn_pe, v_conv_qk, v_b_if, v_g_hnorm, v_p_a, v_p_b, v_w_out, v_g_cross, v_g_mem, v_wq_c, v_wk_c, v_wv_c, v_g_cq, v_g_ck, v_wo_c, v_g_ffn, v_w_up, v_conv_ffn, v_b_conv_ffn, v_w_down):
    args = dict(locals())
    names = ['g_mix', 'w_in', 'g_qa', 'w_qb', 'g_kva', 'w_kvb', 'g_qn_nope', 'g_qn_pe', 'g_kn_nope', 'g_kn_pe', 'conv_qk', 'b_if',
             'g_hnorm', 'p_a', 'p_b', 'w_out', 'g_cross', 'g_mem', 'wq_c', 'wk_c', 'wv_c', 'g_cq', 'g_ck', 'wo_c', 'g_ffn', 'w_up',
             'conv_ffn', 'b_conv_ffn', 'w_down']
    big = ['w_in', 'w_qb', 'w_kvb', 'p_a', 'p_b', 'w_out', 'wq_c', 'wk_c', 'wv_c', 'wo_c', 'w_up', 'w_down']
    sharded_small = ['conv_qk', 'g_hnorm', 'conv_ffn']
    replicated = [n for n in names if n not in big and n not in sharded_small]

    t, d = x.shape[1], x.shape[2]
    x2d, tgt = x[0], loss_target[0]
    mem2d = mem[0]
    me = 4 * lax.axis_index("x") + 2 * lax.axis_index("y") + lax.axis_index("c")
    nc = t // CHUNK
    f2 = b_conv_ffn.shape[1]
    wmain = O_GA + 2 * d

    first = ['w_in', 'w_qb', 'w_kvb']
    behind_in = ['p_a', 'p_b', 'w_out', 'wq_c', 'wk_c', 'wv_c', 'wo_c']
    shards = {n: cast_bf16(args[n], "cast_" + n) for n in big}
    small_local = _pack([args[n] for n in sharded_small], 128).reshape(1, -1)
    gathered = _exchange([shards[n] for n in first] + [small_local], "comm_gather_first", scatter=False)
    gw = dict(zip(first, gathered[:-1]))
    small_all = gathered[-1]
    full_small, pos = [], 0
    for n in sharded_small:
        _, rows, cols = args[n].shape
        piece = small_all[:, 0, pos:pos + rows * cols].reshape(N_DEV, rows, cols)
        full_small.append(piece.transpose(1, 0, 2).reshape(rows, N_DEV * cols))
        pos += rows * cols
    conv_qk_f, g_hnorm_f, conv_ffn_f = full_small

    shard_w = w_in.shape[2]
    c_kpe, c_q, c_i, c_o = O_Q, O_Q + ROPE, O_Q + ROPE + 2 * ML_QK + ML_V, O_Q + ROPE + 2 * ML_QK + ML_V + 2 * ML_HEADS
    segments = [(0, c_kpe, 'main', 0), (c_kpe, c_q, 'tail', 0), (c_q, c_i, 'main', O_Q), (c_i, c_o, 'tail', T_I),
                (c_o, N_DEV * shard_w, 'main', O_O)]

    def shard_cuts(lo, hi):
        return [(j, max(lo, j * shard_w) - j * shard_w, min(hi, (j + 1) * shard_w) - j * shard_w)
                for j in range(lo // shard_w, (hi - 1) // shard_w + 1)]

    def gathered_cols(target):
        return [gw['w_in'][j][:, a:b] for lo, hi, tg, _ in segments if tg == target for j, a, b in shard_cuts(lo, hi)]

    w_main = jnp.concatenate(gathered_cols('main'), axis=1)[None]
    w_tail = jnp.concatenate(gathered_cols('tail') + [jnp.zeros((d, TAIL - ROPE - 2 * ML_HEADS), bf16)], axis=1)[None]
    assert w_main.shape[2] == wmain

    inv_freq = ROPE_BASE ** (-jnp.arange(0, ROPE, 2, dtype=f32) / ROPE)
    inv_tile = _pad_lanes(jnp.concatenate([inv_freq, inv_freq])[None])
    cos, sin = rope_tables(positions.reshape(t, 1), inv_tile)
    gqp, gkp = _pad_lanes(g_qn_pe), _pad_lanes(g_kn_pe)
    b_tile = jnp.pad(b_if, ((0, 0), (T_I, TAIL - T_I - 2 * ML_HEADS)))

    u0 = rms_fwd(x2d, g_mix, "rms_mix")
    z_main, got = mm_nn(u0, w_main, f32, "mm_in_main", exchange=([shards[n] for n in behind_in], False))
    gw.update(zip(behind_in, got))
    qb = gw['w_qb'].transpose(1, 0, 2).reshape(Q_LORA, MLA_HEADS, NOPE + ROPE)
    w_qb_p = jnp.concatenate([qb, jnp.zeros((Q_LORA, MLA_HEADS, HEAD_PAD - NOPE - ROPE), bf16)], axis=2).reshape(1, Q_LORA, -1)
    w_kvb3 = gw['w_kvb']
    p_a3, p_b3, w_out3 = (gw[n].reshape(1, -1, d) for n in ('p_a', 'p_b', 'w_out'))
    wq_c3, wk_c3, wv_c3 = (gw[n].reshape(1, d, -1) for n in ('wq_c', 'wk_c', 'wv_c'))
    wo_c3 = gw['wo_c']
    z_tail = mm_nn(u0, w_tail, f32, "mm_in_tail")
    qa_n, kv_n = lat_norm(z_main, g_qa, g_kva)
    q_raw = mm_nn(qa_n, w_qb_p, f32, "mm_qb")
    kv_raw = mm_nn(kv_n, w_kvb3, f32, "mm_kvb")
    qh, kh, vh = mla_prep(q_raw, kv_raw, z_tail, cos, sin, g_qn_nope, gqp, g_kn_nope, gkp)
    (o_a, o_ab, lse), (w_up3,) = mla_fwd(qh, kh, vh, exchange=([shards['w_up']], False))

    qk_act = qk_conv(z_main, conv_qk_f)
    gates = gate_act(z_tail, b_tile)

    def to_rows(cols):
        return cols.T.reshape(ML_HEADS, nc, 1, CHUNK)

    ig, fg = to_rows(gates[:, T_I:T_F]), to_rows(gates[:, T_F:T_F + ML_HEADS])
    h_ml, c_all, n_all, m_all = mlstm_fwd(qk_act, z_main, ig, fg)
    g_hn3 = g_hnorm_f.reshape(ML_HEADS, 1, ML_DV)
    y_b = mlstm_out(h_ml, z_main, g_hn3)

    ya = mm_nn(o_ab, p_a3, f32, "mm_pa")
    yb = mm_nn(y_b, p_b3, f32, "mm_pb")
    merged = merge_fwd(z_main, ya, yb)
    mo = mm_nn(merged, w_out3, f32, "mm_out")
    x1, uc = resid_rms(x2d, mo, g_cross, "resid_cross")
    mem_n = rms_fwd(mem2d, g_mem, "rms_mem")
    cq = mm_nn(uc, wq_c3, f32, "mm_cq")
    ck = mm_nn(mem_n, wk_c3, f32, "mm_ck")
    cv = mm_nn(mem_n, wv_c3, f32, "mm_cv")
    o_c = cross_fwd(cq, ck, cv, g_cq, g_ck)
    co = mm_nn(o_c, wo_c3, f32, "mm_oc")
    x2, u3 = resid_rms(x1, co, g_ffn, "resid_ffn")
    hup, (w_down_g,) = mm_nn(u3, w_up3, f32, "mm_up", exchange=([shards['w_down']], False))
    w_down3 = w_down_g.reshape(1, -1, d)
    gl = glu_fwd(hup, conv_ffn_f, b_conv_ffn)
    fo = mm_nn(gl, w_down3, f32, "mm_down")
    dx3, dx3_b, loss_acc = loss_head(x2, fo, tgt)

    grads, parts = {}, {}
    grads['w_down'] = mm_tn(gl, dx3_b, 1, "mm_d_wdown").reshape(N_DEV, -1, d)
    dgl = mm_nt(dx3_b, w_down3, f32, "mm_d_gl")
    (dh1, dh2, dcw1, dcw2, db1, db2), (parts['w_down'],) = glu_bwd(hup, conv_ffn_f, b_conv_ffn, dgl,
                                                                    exchange=([grads['w_down']], True))
    dconv_ffn, db_ffn = (jnp.concatenate(pair, axis=1) for pair in ((dcw1, dcw2), (db1, db2)))
    grads['w_up'] = mm_tn_cols(u3, [dh1, dh2], N_DEV, "mm_d_wup")
    du3 = mm_nt_cols([dh1, dh2], w_up3, f32, "mm_d_u3")
    dx2, dx2_b, dg_ffn = rms_bwd(x2, g_ffn, [du3], dx3, "rms_bwd_ffn", want_b16=True)
    grads['wo_c'] = mm_tn(o_c, dx2_b, N_DEV, "mm_d_woc")
    do_c = mm_nt(dx2_b, wo_c3, f32, "mm_d_oc")
    dcq, dck, dcv, dg_cq, dg_ck = cross_bwd(cq, ck, cv, g_cq, g_ck, do_c)
    grads['wq_c'] = mm_tn(uc, dcq, 1, "mm_d_wqc").reshape(N_DEV, -1, dcq.shape[1])
    grads['wk_c'] = mm_tn(mem_n, dck, 1, "mm_d_wkc").reshape(N_DEV, -1, dck.shape[1])
    grads['wv_c'] = mm_tn(mem_n, dcv, 1, "mm_d_wvc").reshape(N_DEV, -1, dcv.shape[1])
    duc = mm_nt(dcq, wq_c3, f32, "mm_d_uc")
    dmem_k = mm_nt(dck, wk_c3, f32, "mm_d_memk")
    dmem_v = mm_nt(dcv, wv_c3, f32, "mm_d_memv")
    dg_mem, = rms_bwd(mem2d, g_mem, [dmem_k, dmem_v], None, "rms_bwd_mem", want_dx=False)
    dx1, dx1_b, dg_cross = rms_bwd(x1, g_cross, [duc], dx2, "rms_bwd_cross", want_b16=True)
    grads['w_out'] = mm_tn(merged, dx1_b, 1, "mm_d_wout").reshape(N_DEV, -1, d)
    dmerged = mm_nt(dx1_b, w_out3, f32, "mm_d_merged")
    dga, dgb, dya, dyb = merge_bwd(z_main, ya, yb, dmerged)
    grads['p_a'] = mm_tn(o_ab, dya, 1, "mm_d_pa").reshape(N_DEV, -1, d)
    grads['p_b'] = mm_tn(y_b, dyb, 1, "mm_d_pb").reshape(N_DEV, -1, d)
    do_a = mm_nt(dya, p_a3, f32, "mm_d_oa")
    dy_b = mm_nt(dyb, p_b3, f32, "mm_d_yb")

    dh_ml, dzo, dg_hn = mlstm_out_bwd(h_ml, z_main, g_hn3, dy_b)
    mixers = ['p_a', 'p_b', 'w_out']
    (dq_act, dk_act, dzv, dig, dfg), got = mlstm_bwd(qk_act, z_main, ig, fg, c_all, n_all, m_all, dh_ml,
                                                     exchange=([grads[n] for n in mixers], True))
    parts.update(zip(mixers, got))
    dzqk, dconv_qk = qk_conv_bwd(z_main, conv_qk_f, dq_act, dk_act)

    delta = mla_delta(o_a, do_a)
    (dqh, dkh, dvh), (parts['w_up'],) = mla_bwd(qh, kh, vh, do_a, lse, delta.reshape(MLA_HEADS, 1, t),
                                                exchange=([grads['w_up']], True))
    dq_raw, dkv_raw, dzt_pe, dg_qn, dg_qp, dg_kn, dg_kp = mla_prep_bwd(
        q_raw, kv_raw, z_tail, cos, sin, g_qn_nope, gqp, g_kn_nope, gkp, dqh, dkh, dvh)
    d_wqb_p = mm_tn(qa_n, dq_raw, 1, "mm_d_wqb")[0].reshape(Q_LORA, MLA_HEADS, HEAD_PAD)[:, :, :NOPE + ROPE]
    grads['w_qb'] = d_wqb_p.reshape(Q_LORA, N_DEV, -1).transpose(1, 0, 2)
    grads['w_kvb'] = mm_tn(kv_n, dkv_raw, N_DEV, "mm_d_wkvb")
    dqa = mm_nt(dq_raw, w_qb_p, f32, "mm_d_qa")
    dkvn = mm_nt(dkv_raw, w_kvb3, f32, "mm_d_kvn")
    dz_lat, dg_qa, dg_kva = lat_norm_bwd(z_main, g_qa, g_kva, dqa, dkvn)

    def to_cols(rows):
        return rows.reshape(ML_HEADS, t).T

    dgate = jnp.pad(jnp.concatenate([to_cols(dig), to_cols(dfg)], axis=1), ((0, 0), (T_I, TAIL - T_I - 2 * ML_HEADS)))
    dz_tail, db_if = tail_bwd(z_tail, b_tile, dzt_pe, dgate)
    dz_main = [dz_lat, dzqk, dzv, dzo, dga, dgb]
    small_mats = ['wq_c', 'wk_c', 'wv_c', 'wo_c', 'w_qb', 'w_kvb']
    d_wmain3, got = mm_tn_cols(u0, dz_main, 1, "mm_d_wmain", exchange=([grads[n] for n in small_mats], True))
    parts.update(zip(small_mats, got))
    d_wmain = d_wmain3[0]
    d_wtail = mm_tn(u0, dz_tail, 1, "mm_d_wtail")[0]
    d_target = {'main': d_wmain, 'tail': d_wtail}
    blocks = []
    for j in range(N_DEV):
        lo_j, hi_j = j * shard_w, (j + 1) * shard_w
        cols = [d_target[tg][:, off + max(lo, lo_j) - lo:off + min(hi, hi_j) - lo]
                for lo, hi, tg, off in segments if lo < hi_j and hi > lo_j]
        blocks.append(jnp.concatenate(cols, axis=1))
    grads['w_in'] = jnp.stack(blocks)
    du0_a, (parts['w_in'],) = mm_nt_cols(dz_main, w_main, f32, "mm_d_u0_main", exchange=([grads['w_in']], True))
    du0_b = mm_nt(dz_tail, w_tail, f32, "mm_d_u0_tail")
    grad_x, dg_mix = rms_bwd(x2d, g_mix, [du0_a, du0_b], dx1, "rms_bwd_mix")

    out_g, out_d, out_m, out_v = {}, {}, {}, {}
    for n in big:
        res = adam_sum(parts[n], args[n], args['m_' + n], args['v_' + n], "adam_" + n)
        out_g[n], out_d[n], out_m[n], out_v[n] = (a.reshape(args[n].shape) for a in res)

    small_full = {
        'g_mix': dg_mix[0], 'g_qa': dg_qa[0], 'g_kva': dg_kva[0], 'g_qn_nope': dg_qn[0], 'g_qn_pe': dg_qp[0, :ROPE],
        'g_kn_nope': dg_kn[0], 'g_kn_pe': dg_kp[0, :ROPE], 'conv_qk': dconv_qk, 'b_if': db_if[0, T_I:T_I + 2 * ML_HEADS],
        'g_hnorm': dg_hn[:, 0, :], 'g_cross': dg_cross[0], 'g_mem': dg_mem[0], 'g_cq': dg_cq[0], 'g_ck': dg_ck[0],
        'g_ffn': dg_ffn[0], 'conv_ffn': dconv_ffn, 'b_conv_ffn': db_ffn[0], 'loss': loss_acc[0, :1]}
    order = list(small_full)
    packed = _pack([small_full[n] for n in order], 8 * 128).reshape(1, -1)
    gathered_small, = _exchange([packed], "comm_gather_small", scatter=False)
    summed = sum_parts(gathered_small.reshape(N_DEV, -1, 128), "sum_small").reshape(-1)
    full_g = dict(zip(order, _unpack(summed, [small_full[n].shape for n in order])))
    loss = full_g['loss'][0]

    local_g = {}
    for n in replicated:
        local_g[n] = full_g[n].reshape(args[n].shape)
    for n in sharded_small:
        shp = args[n].shape
        full = full_g[n].reshape((1,) + full_g[n].shape)
        local_g[n] = lax.dynamic_slice_in_dim(full, me * shp[-1], shp[-1], axis=2)
    small = replicated + sharded_small
    dl_f, m_f, v_f = adam_flat(*[_pack([src[n] if pre == '' else args[pre + n] for n in small], 8 * 128).reshape(-1, 128)
                                 for pre, src in (('', args), ('', local_g), ('m_', None), ('v_', None))], "adam_small")
    shapes = [args[n].shape for n in small]
    for dst, flat in ((out_d, dl_f), (out_m, m_f), (out_v, v_f)):
        dst.update(zip(small, _unpack(flat.reshape(-1), shapes)))
    out_g.update(local_g)

    return (loss, grad_x[None], *[out_g[n] for n in names], *[out_d[n] for n in names],
            *[out_m[n] for n in names], *[out_v[n] for n in names])
```

```python
import functools

import jax
import jax.numpy as jnp
from jax import lax
from jax.experimental import pallas as pl
from jax.experimental.pallas import tpu as pltpu

f32 = jnp.float32
bf16 = jnp.bfloat16

N_DEV = 8
EPS = 1e-6
CHUNK = 64
CHUNK_SHIFT = 6
assert 1 << CHUNK_SHIFT == CHUNK
MLA_HEADS = 16
Q_LORA = 512
KV_LORA = 512
NOPE = 128
ROPE = 64
V_HEAD = 128
ROPE_BASE = 10000.0
HEAD_PAD = 256
ML_HEADS = 8
ML_DK = 128
ML_DV = 256
ML_CONV = 4
ML_QK = ML_HEADS * ML_DK
ML_V = ML_HEADS * ML_DV
CR_HEADS = 4
CR_HD = 128
FFN_CONV = 3
ADAM_LR = 0.001
ADAM_B1 = 0.9
ADAM_B2 = 0.999
ADAM_EPS = 1e-08
ADAM_WD = 0.01
ADAM_STEP = 10
O_QA, O_KV, O_Q, O_K = 0, Q_LORA, Q_LORA + KV_LORA, Q_LORA + KV_LORA + ML_QK
O_V = O_K + ML_QK
O_O = O_V + ML_V
O_GA = O_O + ML_V
TAIL = 128
T_I, T_F = ROPE, ROPE + ML_HEADS
VMEM_LIMIT_V7X = 48 * 1024 * 1024
MESH = pl.DeviceIdType.MESH


def _call(body, name, grid, in_specs, out_specs, out_shape, scratch=(), exchange=None):
    params = pltpu.CompilerParams(vmem_limit_bytes=VMEM_LIMIT_V7X)
    if exchange is None:
        return pl.pallas_call(body, name=name, grid=grid, in_specs=in_specs, out_specs=out_specs, out_shape=out_shape,
                              scratch_shapes=list(scratch), compiler_params=params)
    arrs, scatter = exchange
    single = not isinstance(out_specs, (list, tuple))
    o_specs = [out_specs] if single else list(out_specs)
    o_shape = [out_shape] if single else list(out_shape)
    n_in, n_out, n_sc, n = len(in_specs), len(o_specs), len(scratch), len(arrs)
    any_spec = pl.BlockSpec(memory_space=pl.ANY)

    def body_with_exchange(*refs):
        pos = [0]

        def take(k):
            pos[0] += k
            return refs[pos[0] - k:pos[0]]

        ins, ex_in, outs, ex_out, sc = take(n_in), take(n), take(n_out), take(n), take(n_sc)
        start, middle, wait = _exchange_ops(ex_in, ex_out, refs[pos[0]:], scatter)
        step, total = 0, 1
        for a in range(len(grid)):
            step = step * grid[a] + pl.program_id(a)
            total *= grid[a]
        pl.when(step == 0)(start)
        body(*ins, *outs, *sc)
        if middle is not None:
            pl.when(step == total // 2)(middle)
        pl.when(step == total - 1)(wait)

    call = pl.pallas_call(body_with_exchange, name="comm_" + name, grid=grid, in_specs=list(in_specs) + [any_spec] * n,
                          out_specs=o_specs + [any_spec] * n, out_shape=o_shape + _exchange_shapes(arrs, scatter),
                          scratch_shapes=list(scratch) + _exchange_sems(n), compiler_params=params)

    def run(*operands):
        res = call(*operands, *arrs)
        return (res[0] if single else list(res[:n_out])), list(res[n_out:])

    return run


def _tile(n, cands):
    for c in cands:
        if n % c == 0:
            return c
    return n


def _sds(shape, dtype):
    return jax.ShapeDtypeStruct(tuple(shape), dtype)


def _bdot(a, b, ca, cb):
    return lax.dot_general(a.astype(bf16), b.astype(bf16), (((ca,), (cb,)), ((), ())), preferred_element_type=f32)


_BIG = (1024, 512, 256, 128)


def _col_tile(nb):
    return nb if nb <= 1536 else _tile(nb, _BIG)


_DEEP = (2048, 1024, 512, 256, 128)


def _mm_call(name, grid, in_specs, out_spec, out_shape, tile, nk, ca, cb, exchange, operands):
    def dot(a_ref, w_ref):
        return _bdot(a_ref[...], w_ref[0] if len(w_ref.shape) == 3 else w_ref[...], ca, cb)

    def store(o_ref, val):
        if len(o_ref.shape) == 3:
            o_ref[0] = val.astype(o_ref.dtype)
        else:
            o_ref[...] = val.astype(o_ref.dtype)

    if nk == 1:
        def body(a_ref, w_ref, o_ref):
            store(o_ref, dot(a_ref, w_ref))

        scratch = []
    else:
        def body(a_ref, w_ref, o_ref, acc):
            kk = pl.program_id(2)

            @pl.when(kk == 0)
            def _():
                acc[...] = jnp.zeros_like(acc)

            acc[...] += dot(a_ref, w_ref)

            @pl.when(kk == nk - 1)
            def _():
                store(o_ref, acc[...])

        scratch = [pltpu.VMEM(tile, f32)]
    return _call(body, name, grid, in_specs, out_spec, out_shape, scratch, exchange=exchange)(*operands)


def mm_nn(a, w3, out_dtype, name, exchange=None):
    m, k = a.shape
    nblk, k2, nb = w3.shape
    assert k == k2
    tm, tk, tn = _tile(m, _BIG), _tile(k, _DEEP), _col_tile(nb)
    per, nk = nb // tn, k // tk
    return _mm_call(name, (m // tm, nblk * per, nk),
                    [pl.BlockSpec((tm, tk), lambda i, j, kk: (i, kk)),
                     pl.BlockSpec((1, tk, tn), lambda i, j, kk: (j // per, kk, j % per))],
                    pl.BlockSpec((tm, tn), lambda i, j, kk: (i, j)), _sds((m, nblk * nb), out_dtype),
                    (tm, tn), nk, 1, 0, exchange, (a, w3))


def mm_nt(a, w3, out_dtype, name, exchange=None):
    m, n = a.shape
    nblk, k, nb = w3.shape
    assert n == nblk * nb
    tm, tn = _tile(m, _BIG), _tile(k, _BIG)
    tc = nb if nb <= 1536 else _tile(nb, _DEEP)
    per = nb // tc
    nk = nblk * per
    return _mm_call(name, (m // tm, k // tn, nk),
                    [pl.BlockSpec((tm, tc), lambda i, j, kk: (i, kk)),
                     pl.BlockSpec((1, tn, tc), lambda i, j, kk: (kk // per, j, kk % per))],
                    pl.BlockSpec((tm, tn), lambda i, j, kk: (i, j)), _sds((m, k), out_dtype),
                    (tm, tn), nk, 1, 1, exchange, (a, w3))


def mm_tn(a, b, nblk, name, exchange=None):
    r, m = a.shape
    r2, n = b.shape
    assert r == r2 and n % nblk == 0
    nb = n // nblk
    tm, tk, tn = _tile(m, _BIG), _tile(r, _DEEP), _col_tile(nb)
    per, nk = nb // tn, r // tk
    return _mm_call(name, (m // tm, nblk * per, nk),
                    [pl.BlockSpec((tk, tm), lambda i, j, kk: (kk, i)),
                     pl.BlockSpec((tk, tn), lambda i, j, kk: (kk, j))],
                    pl.BlockSpec((1, tm, tn), lambda i, j, kk: (j // per, i, j % per)), _sds((nblk, m, nb), bf16),
                    (tm, tn), nk, 0, 0, exchange, (a, b))


def _section_tiles(sections, cands):
    widths = [s.shape[1] for s in sections]
    tile = next(c for c in cands if all(w % c == 0 for w in widths))
    counts = [w // tile for w in widths]
    firsts = [sum(counts[:i]) for i in range(len(counts))]
    return tile, firsts, counts


SECTION_VMEM_BYTES = 24 * 1024 * 1024


def mm_tn_cols(a, sections, nblk, name, exchange=None):
    r, m = a.shape
    n = sum(s.shape[1] for s in sections)
    nb = n // nblk
    tn, firsts, counts = _section_tiles(sections, (nb,) if nb <= 1536 else _BIG)
    per = nb // tn
    tm = _tile(m, _BIG)
    tk = next(c for c in _DEEP if r % c == 0 and len(sections) * c * tn * 4 <= SECTION_VMEM_BYTES)
    nk = r // tk

    def body(a_ref, *refs):
        b_refs, o_ref, acc = refs[:len(sections)], refs[len(sections)], refs[len(sections) + 1]
        j, kk = pl.program_id(1), pl.program_id(2)

        @pl.when(kk == 0)
        def _():
            acc[...] = jnp.zeros_like(acc)

        for b_ref, lo, cnt in zip(b_refs, firsts, counts):
            @pl.when(jnp.logical_and(j >= lo, j < lo + cnt))
            def _(b_ref=b_ref):
                acc[...] += _bdot(a_ref[...], b_ref[...], 0, 0)

        @pl.when(kk == nk - 1)
        def _():
            o_ref[0] = acc[...].astype(bf16)

    def spec(lo, cnt):
        def index(i, j, kk):
            return jnp.where(j < lo, 0, jnp.where(j >= lo + cnt, nk - 1, kk)), jnp.clip(j - lo, 0, cnt - 1)
        return pl.BlockSpec((tk, tn), index)

    return _call(body, name, (m // tm, n // tn, nk),
                 [pl.BlockSpec((tk, tm), lambda i, j, kk: (kk, i))] + [spec(lo, cnt) for lo, cnt in zip(firsts, counts)],
                 pl.BlockSpec((1, tm, tn), lambda i, j, kk: (j // per, i, j % per)), _sds((nblk, m, nb), bf16),
                 [pltpu.VMEM((tm, tn), f32)], exchange=exchange)(a, *sections)


def mm_nt_cols(sections, w3, out_dtype, name, exchange=None):
    m = sections[0].shape[0]
    nblk, k, nb = w3.shape
    tc, firsts, counts = _section_tiles(sections, (nb,) if nb <= 1536 else _DEEP)
    assert nblk * nb == tc * sum(counts) and nb % tc == 0
    per = nb // tc
    tm, tn = _tile(m, _BIG), _tile(k, _BIG)
    nk = nblk * per

    def body(*refs):
        a_refs, w_ref, o_ref, acc = refs[:len(sections)], refs[len(sections)], refs[len(sections) + 1], refs[len(sections) + 2]
        kk = pl.program_id(2)

        @pl.when(kk == 0)
        def _():
            acc[...] = jnp.zeros_like(acc)

        for a_ref, lo, cnt in zip(a_refs, firsts, counts):
            @pl.when(jnp.logical_and(kk >= lo, kk < lo + cnt))
            def _(a_ref=a_ref):
                acc[...] += _bdot(a_ref[...], w_ref[0], 1, 1)

        @pl.when(kk == nk - 1)
        def _():
            o_ref[...] = acc[...].astype(o_ref.dtype)

    def spec(lo, cnt):
        return pl.BlockSpec((tm, tc), lambda i, j, kk: (i, jnp.clip(kk - lo, 0, cnt - 1)))

    return _call(body, name, (m // tm, k // tn, nk),
                 [spec(lo, cnt) for lo, cnt in zip(firsts, counts)] + [pl.BlockSpec((1, tn, tc), lambda i, j, kk: (kk // per, j, kk % per))],
                 pl.BlockSpec((tm, tn), lambda i, j, kk: (i, j)), _sds((m, k), out_dtype),
                 [pltpu.VMEM((tm, tn), f32)], exchange=exchange)(*sections, w3)


def _rms(x, g):
    return x * lax.rsqrt(jnp.mean(x * x, axis=-1, keepdims=True) + EPS) * g


def _rms_pad(x, g, width):
    return x * lax.rsqrt(jnp.sum(x * x, axis=-1, keepdims=True) / width + EPS) * g


def _first(*ids):
    ok = ids[0] == 0
    for i in ids[1:]:
        ok = jnp.logical_and(ok, i == 0)
    return ok


def _acc_row(ref, val, first):
    @pl.when(first)
    def _():
        ref[...] = jnp.zeros_like(ref)

    ref[0:1, :] += val


def rms_fwd(x, g, name):
    r, w = x.shape
    tm = _tile(r, (256, 128, 64, 32, 16, 8))

    def body(x_ref, g_ref, o_ref):
        o_ref[...] = _rms(x_ref[...], g_ref[...]).astype(bf16)

    return _call(body, name, (r // tm,), [pl.BlockSpec((tm, w), lambda i: (i, 0)), pl.BlockSpec((1, w), lambda i: (0, 0))],
                 pl.BlockSpec((tm, w), lambda i: (i, 0)), _sds((r, w), bf16))(x, g)


def resid_rms(xa, xb, g, name):
    r, w = xa.shape
    tm = _tile(r, (256, 128, 64, 32, 16, 8))

    def body(a_ref, b_ref, g_ref, s_ref, u_ref):
        xs = a_ref[...] + b_ref[...]
        s_ref[...] = xs
        u_ref[...] = _rms(xs, g_ref[...]).astype(bf16)

    row = pl.BlockSpec((tm, w), lambda i: (i, 0))
    return _call(body, name, (r // tm,), [row, row, pl.BlockSpec((1, w), lambda i: (0, 0))], [row, row],
                 [_sds((r, w), f32), _sds((r, w), bf16)])(xa, xb, g)


def rms_bwd(x, g, dys, dres, name, want_dx=True, want_b16=False):
    r, w = x.shape
    tm = _tile(r, (256, 128, 64, 32, 16, 8))
    nd = len(dys)

    def body(*refs):
        x_ref, g_ref = refs[0], refs[1]
        dy = refs[2][...]
        for j in range(1, nd):
            dy = dy + refs[2 + j][...]
        pos = 2 + nd
        _, vjp = jax.vjp(_rms, x_ref[...], g_ref[...])
        dx, dg = vjp(dy)
        if dres is not None:
            dx = dx + refs[pos][...]
            pos += 1
        if want_dx:
            refs[pos][...] = dx
            pos += 1
        if want_b16:
            refs[pos][...] = dx.astype(bf16)
            pos += 1
        _acc_row(refs[pos], dg, pl.program_id(0) == 0)

    row = pl.BlockSpec((tm, w), lambda i: (i, 0))
    ins = [x, g] + list(dys) + ([dres] if dres is not None else [])
    in_specs = [row, pl.BlockSpec((1, w), lambda i: (0, 0))] + [row] * (nd + (dres is not None))
    out_specs = [row] * (want_dx + want_b16) + [pl.BlockSpec((8, w), lambda i: (0, 0))]
    out_shape = ([_sds((r, w), f32)] if want_dx else []) + ([_sds((r, w), bf16)] if want_b16 else []) + [_sds((8, w), f32)]
    return _call(body, name, (r // tm,), in_specs, out_specs, out_shape)(*ins)


def lat_norm(z_main, g_qa, g_kva):
    t = z_main.shape[0]
    tm = _tile(t, (512, 256, 128, 64))

    def body(z_ref, gq_ref, gk_ref, q_ref, k_ref):
        q_ref[...] = _rms(z_ref[:, :Q_LORA], gq_ref[...]).astype(bf16)
        k_ref[...] = _rms(z_ref[:, Q_LORA:], gk_ref[...]).astype(bf16)

    return _call(body, "lat_norm", (t // tm,),
                 [pl.BlockSpec((tm, Q_LORA + KV_LORA), lambda i: (i, 0)), pl.BlockSpec((1, Q_LORA), lambda i: (0, 0)),
                  pl.BlockSpec((1, KV_LORA), lambda i: (0, 0))],
                 [pl.BlockSpec((tm, Q_LORA), lambda i: (i, 0)), pl.BlockSpec((tm, KV_LORA), lambda i: (i, 0))],
                 [_sds((t, Q_LORA), bf16), _sds((t, KV_LORA), bf16)])(z_main, g_qa, g_kva)


def lat_norm_bwd(z_main, g_qa, g_kva, dqa, dkv):
    t = z_main.shape[0]
    tm = _tile(t, (512, 256, 128, 64))

    def body(z_ref, gq_ref, gk_ref, dq_ref, dk_ref, dz_ref, dgq_ref, dgk_ref):
        first = pl.program_id(0) == 0
        _, vq = jax.vjp(_rms, z_ref[:, :Q_LORA], gq_ref[...])
        dx, dg = vq(dq_ref[...])
        dz_ref[:, :Q_LORA] = dx.astype(bf16)
        _acc_row(dgq_ref, dg, first)
        _, vk = jax.vjp(_rms, z_ref[:, Q_LORA:], gk_ref[...])
        dx, dg = vk(dk_ref[...])
        dz_ref[:, Q_LORA:] = dx.astype(bf16)
        _acc_row(dgk_ref, dg, first)

    return _call(body, "lat_norm_bwd", (t // tm,),
                 [pl.BlockSpec((tm, Q_LORA + KV_LORA), lambda i: (i, 0)), pl.BlockSpec((1, Q_LORA), lambda i: (0, 0)),
                  pl.BlockSpec((1, KV_LORA), lambda i: (0, 0)), pl.BlockSpec((tm, Q_LORA), lambda i: (i, 0)),
                  pl.BlockSpec((tm, KV_LORA), lambda i: (i, 0))],
                 [pl.BlockSpec((tm, Q_LORA + KV_LORA), lambda i: (i, 0)), pl.BlockSpec((8, Q_LORA), lambda i: (0, 0)),
                  pl.BlockSpec((8, KV_LORA), lambda i: (0, 0))],
                 [_sds((t, Q_LORA + KV_LORA), bf16), _sds((8, Q_LORA), f32), _sds((8, KV_LORA), f32)])(z_main, g_qa, g_kva, dqa, dkv)


def rope_tables(pos_col, inv_freq):
    t = pos_col.shape[0]
    tm = _tile(t, (512, 256, 128, 64))

    def body(p_ref, f_ref, c_ref, s_ref):
        ang = p_ref[...].astype(f32) * f_ref[...]
        lane = lax.broadcasted_iota(jnp.int32, ang.shape, 1)
        c_ref[...] = jnp.where(lane < ROPE, jnp.cos(ang), 0.0)
        sn = jnp.sin(ang)
        s_ref[...] = jnp.where(lane < ROPE // 2, -sn, jnp.where(lane < ROPE, sn, 0.0))

    return _call(body, "rope_tables", (t // tm,),
                 [pl.BlockSpec((tm, 1), lambda i: (i, 0)), pl.BlockSpec((1, TAIL), lambda i: (0, 0))],
                 [pl.BlockSpec((tm, TAIL), lambda i: (i, 0))] * 2, [_sds((t, TAIL), f32)] * 2)(pos_col, inv_freq)


def _swap_halves(n):
    lane = lax.broadcasted_iota(jnp.int32, n.shape, 1)
    return jnp.where(lane < ROPE // 2, pltpu.roll(n, TAIL - ROPE // 2, 1), pltpu.roll(n, ROPE // 2, 1))


def _rope(n, c, s):
    return n * c + _swap_halves(n) * s


def _rope_t(d, c, s):
    return d * c + _swap_halves(d * s)


def _prep_specs(tm):
    head = pl.BlockSpec((tm, HEAD_PAD), lambda i, h: (i, h))
    row = pl.BlockSpec((tm, TAIL), lambda i, h: (i, 0))
    gain = pl.BlockSpec((1, TAIL), lambda i, h: (0, 0))
    return head, row, gain


def _pe_in(zt):
    lane = lax.broadcasted_iota(jnp.int32, zt.shape, 1)
    return jnp.where(lane < ROPE, zt, 0.0)


def mla_prep(q_raw, kv_raw, z_tail, cos, sin, gqn, gqp, gkn, gkp):
    t = q_raw.shape[0]
    tm = _tile(t, (1024, 512, 256, 128, 64))

    def body(q_ref, kv_ref, zt_ref, c_ref, s_ref, gqn_ref, gqp_ref, gkn_ref, gkp_ref, qh_ref, kh_ref, vh_ref):
        c, s = c_ref[...], s_ref[...]
        qh_ref[:, :NOPE] = _rms(q_ref[:, :NOPE], gqn_ref[...]).astype(bf16)
        qh_ref[:, NOPE:] = _rope(_rms_pad(q_ref[:, NOPE:], gqp_ref[...], ROPE), c, s).astype(bf16)
        kh_ref[:, :NOPE] = _rms(kv_ref[:, :NOPE], gkn_ref[...]).astype(bf16)
        kh_ref[:, NOPE:] = _rope(_rms_pad(_pe_in(zt_ref[...]), gkp_ref[...], ROPE), c, s).astype(bf16)
        vh_ref[...] = kv_ref[:, NOPE:].astype(bf16)

    head, row, gain = _prep_specs(tm)
    return _call(body, "mla_prep", (t // tm, MLA_HEADS), [head, head, row, row, row, gain, gain, gain, gain],
                 [head, head, pl.BlockSpec((tm, V_HEAD), lambda i, h: (i, h))],
                 [_sds((t, MLA_HEADS * HEAD_PAD), bf16), _sds((t, MLA_HEADS * HEAD_PAD), bf16), _sds((t, MLA_HEADS * V_HEAD), bf16)],
                 )(q_raw, kv_raw, z_tail, cos, sin, gqn, gqp, gkn, gkp)


def mla_prep_bwd(q_raw, kv_raw, z_tail, cos, sin, gqn, gqp, gkn, gkp, dqh, dkh, dvh):
    t = q_raw.shape[0]
    tm = _tile(t, (1024, 512, 256, 128, 64))
    pad_norm = functools.partial(_rms_pad, width=ROPE)

    def body(q_ref, kv_ref, zt_ref, c_ref, s_ref, gqn_ref, gqp_ref, gkn_ref, gkp_ref, dqh_ref, dkh_ref, dvh_ref,
             dq_ref, dkv_ref, dzt_ref, dgqn_ref, dgqp_ref, dgkn_ref, dgkp_ref):
        i, h = pl.program_id(0), pl.program_id(1)
        first = _first(i, h)
        c, s = c_ref[...], s_ref[...]
        _, v1 = jax.vjp(_rms, q_ref[:, :NOPE], gqn_ref[...])
        dx, dg = v1(dqh_ref[:, :NOPE])
        dq_ref[:, :NOPE] = dx.astype(bf16)
        _acc_row(dgqn_ref, dg, first)
        _, v2 = jax.vjp(pad_norm, q_ref[:, NOPE:], gqp_ref[...])
        dx, dg = v2(_rope_t(dqh_ref[:, NOPE:], c, s))
        dq_ref[:, NOPE:] = dx.astype(bf16)
        _acc_row(dgqp_ref, dg, first)
        _, v3 = jax.vjp(_rms, kv_ref[:, :NOPE], gkn_ref[...])
        dx, dg = v3(dkh_ref[:, :NOPE])
        dkv_ref[:, :NOPE] = dx.astype(bf16)
        _acc_row(dgkn_ref, dg, first)
        dkv_ref[:, NOPE:] = dvh_ref[...].astype(bf16)
        _, v4 = jax.vjp(pad_norm, _pe_in(zt_ref[...]), gkp_ref[...])
        dx, dg = v4(_rope_t(dkh_ref[:, NOPE:], c, s))
        _acc_row(dgkp_ref, dg, first)

        @pl.when(h == 0)
        def _():
            dzt_ref[...] = jnp.zeros_like(dzt_ref)

        dzt_ref[...] += dx

    head, row, gain = _prep_specs(tm)
    acc = pl.BlockSpec((8, TAIL), lambda i, h: (0, 0))
    vspec = pl.BlockSpec((tm, V_HEAD), lambda i, h: (i, h))
    return _call(body, "mla_prep_bwd", (t // tm, MLA_HEADS),
                 [head, head, row, row, row, gain, gain, gain, gain, head, head, vspec],
                 [head, head, row, acc, acc, acc, acc],
                 [_sds((t, MLA_HEADS * HEAD_PAD), bf16), _sds((t, MLA_HEADS * HEAD_PAD), bf16), _sds((t, TAIL), f32)]
                 + [_sds((8, TAIL), f32)] * 4)(q_raw, kv_raw, z_tail, cos, sin, gqn, gqp, gkn, gkp, dqh, dkh, dvh)


ATT_BLOCK = 512
NEG = -1e30
ATT_SCALE = (NOPE + ROPE) ** -0.5
LOG2_E = 1.4426950408889634
ATT_SCALE2 = ATT_SCALE * LOG2_E
ATT_HEADS = 2
ATT_HEADS_FWD = 8


def _chunk_visible(shape, key_axis):
    kc = lax.broadcasted_iota(jnp.int32, shape, key_axis) >> CHUNK_SHIFT
    qc = lax.broadcasted_iota(jnp.int32, shape, 1 - key_axis) >> CHUNK_SHIFT
    return kc <= qc


def mla_fwd(qh, kh, vh, exchange=None):
    t = qh.shape[0]
    tb = min(ATT_BLOCK, t)
    nb = t // tb

    hp = ATT_HEADS_FWD

    def body(q_ref, k_ref, v_ref, o_ref, ob_ref, lse_ref, m_s, l_s, acc):
        qi, ki = pl.program_id(1), pl.program_id(2)

        @pl.when(ki == 0)
        def _():
            m_s[...] = jnp.full_like(m_s, NEG)
            l_s[...] = jnp.zeros_like(l_s)
            acc[...] = jnp.zeros_like(acc)

        def step(diagonal):
            new = []
            for j in range(hp):
                q, k = q_ref[:, j * HEAD_PAD:(j + 1) * HEAD_PAD], k_ref[:, j * HEAD_PAD:(j + 1) * HEAD_PAD]
                s = _bdot(k, q, 1, 1) * ATT_SCALE2
                if diagonal:
                    s = jnp.where(_chunk_visible(s.shape, 0), s, -jnp.inf)
                m_old = m_s[j]
                m_new = jnp.maximum(m_old, jnp.max(s, axis=0, keepdims=True))
                p = jnp.exp2(s - m_new)
                alpha = jnp.exp2(m_old - m_new)
                l_new = alpha * l_s[j] + jnp.sum(p, axis=0, keepdims=True)
                acc_new = alpha * acc[j] + _bdot(v_ref[:, j * V_HEAD:(j + 1) * V_HEAD], p, 0, 0)
                new.append((m_new, l_new, acc_new))
            for j, (m_new, l_new, acc_new) in enumerate(new):
                m_s[j] = m_new
                l_s[j] = l_new
                acc[j] = acc_new
            return new

        @pl.when(ki < qi)
        def _():
            step(False)

        @pl.when(ki == qi)
        def _():
            for j, (m_new, l_new, acc_new) in enumerate(step(True)):
                o = (acc_new / l_new).T
                o_ref[:, j * V_HEAD:(j + 1) * V_HEAD] = o
                ob_ref[:, j * V_HEAD:(j + 1) * V_HEAD] = o.astype(bf16)
                lse_ref[j] = m_new + jnp.log2(l_new)

    kv = lambda g, qi, ki: (jnp.minimum(ki, qi), g)
    o_spec = pl.BlockSpec((tb, hp * V_HEAD), lambda g, qi, ki: (qi, g))
    return _call(body, "mla_fwd", (MLA_HEADS // hp, nb, nb),
                 [pl.BlockSpec((tb, hp * HEAD_PAD), lambda g, qi, ki: (qi, g)), pl.BlockSpec((tb, hp * HEAD_PAD), kv),
                  pl.BlockSpec((tb, hp * V_HEAD), kv)],
                 [o_spec, o_spec, pl.BlockSpec((hp, 1, tb), lambda g, qi, ki: (g, 0, qi))],
                 [_sds((t, MLA_HEADS * V_HEAD), f32), _sds((t, MLA_HEADS * V_HEAD), bf16), _sds((MLA_HEADS, 1, t), f32)],
                 [pltpu.VMEM((hp, 1, tb), f32), pltpu.VMEM((hp, 1, tb), f32), pltpu.VMEM((hp, V_HEAD, tb), f32)],
                 exchange=exchange)(qh, kh, vh)


def mla_delta(o, do):
    t = o.shape[0]
    tm = _tile(t, (512, 256, 128, 64))

    def body(o_ref, do_ref, d_ref):
        for h in range(MLA_HEADS):
            cols = slice(h * V_HEAD, (h + 1) * V_HEAD)
            d_ref[h] = jnp.sum(o_ref[:, cols] * do_ref[:, cols], axis=1, keepdims=True)

    blk = pl.BlockSpec((tm, MLA_HEADS * V_HEAD), lambda i: (i, 0))
    return _call(body, "mla_delta", (t // tm,), [blk, blk], pl.BlockSpec((MLA_HEADS, tm, 1), lambda i: (0, i, 0)),
                 _sds((MLA_HEADS, t, 1), f32))(o, do)


def mla_bwd(qh, kh, vh, do, lse_row, delta_row, exchange=None):
    t = qh.shape[0]
    tb = min(ATT_BLOCK, t)
    nb = t // tb

    hp = ATT_HEADS

    def body(q_ref, k_ref, v_ref, do_ref, lse_ref, dl_ref, dq_ref, dk_ref, dv_ref, dk_acc, dv_acc):
        ki, qi = pl.program_id(1), pl.program_id(2)

        @pl.when(jnp.logical_and(ki == 0, qi == 0))
        def _():
            dq_ref[...] = jnp.zeros_like(dq_ref)

        @pl.when(qi == 0)
        def _():
            dk_acc[...] = jnp.zeros_like(dk_acc)
            dv_acc[...] = jnp.zeros_like(dv_acc)

        def step(diagonal):
            rows = pl.ds(pl.multiple_of(qi * tb, tb), tb)
            new = []
            for j in range(hp):
                qc, vc = slice(j * HEAD_PAD, (j + 1) * HEAD_PAD), slice(j * V_HEAD, (j + 1) * V_HEAD)
                q, k, do_b = q_ref[:, qc], k_ref[:, qc], do_ref[:, vc]
                s = _bdot(k, q, 1, 1) * ATT_SCALE2
                if diagonal:
                    s = jnp.where(_chunk_visible(s.shape, 0), s, -jnp.inf)
                p = jnp.exp2(s - lse_ref[j])
                dp = _bdot(v_ref[:, vc], do_b, 1, 1)
                ds = p * (dp - dl_ref[j]) * ATT_SCALE
                new.append((dv_acc[:, vc] + _bdot(p, do_b, 1, 0), dk_acc[:, qc] + _bdot(ds, q, 1, 0),
                            dq_ref[rows, qc] + _bdot(ds, k, 0, 0)))
            for j, (dv, dk, dq) in enumerate(new):
                dv_acc[:, j * V_HEAD:(j + 1) * V_HEAD] = dv
                dk_acc[:, j * HEAD_PAD:(j + 1) * HEAD_PAD] = dk
                dq_ref[rows, j * HEAD_PAD:(j + 1) * HEAD_PAD] = dq

        @pl.when(qi > ki)
        def _():
            step(False)

        @pl.when(qi == ki)
        def _():
            step(True)

        @pl.when(qi == nb - 1)
        def _():
            dk_ref[...] = dk_acc[...]
            dv_ref[...] = dv_acc[...]

    qs = lambda g, ki, qi: (jnp.maximum(qi, ki), g)
    ks = lambda g, ki, qi: (ki, g)
    vec = pl.BlockSpec((hp, 1, tb), lambda g, ki, qi: (g, 0, jnp.maximum(qi, ki)))
    return _call(body, "mla_bwd", (MLA_HEADS // hp, nb, nb),
                 [pl.BlockSpec((tb, hp * HEAD_PAD), qs), pl.BlockSpec((tb, hp * HEAD_PAD), ks), pl.BlockSpec((tb, hp * V_HEAD), ks),
                  pl.BlockSpec((tb, hp * V_HEAD), qs), vec, vec],
                 [pl.BlockSpec((t, hp * HEAD_PAD), lambda g, ki, qi: (0, g)), pl.BlockSpec((tb, hp * HEAD_PAD), ks),
                  pl.BlockSpec((tb, hp * V_HEAD), ks)],
                 [_sds((t, MLA_HEADS * HEAD_PAD), f32), _sds((t, MLA_HEADS * HEAD_PAD), f32), _sds((t, MLA_HEADS * V_HEAD), f32)],
                 [pltpu.VMEM((tb, hp * HEAD_PAD), f32), pltpu.VMEM((tb, hp * V_HEAD), f32)], exchange=exchange)(
        qh, kh, vh, do, lse_row, delta_row)


PAD = 8


def _conv_taps(pad_ref, w, width, t):
    y = pad_ref[PAD - width + 1:PAD - width + 1 + t, :] * w[0:1, :]
    for j in range(1, width):
        y = y + pad_ref[PAD - width + 1 + j:PAD - width + 1 + j + t, :] * w[j:j + 1, :]
    return y


def _conv_bwd(xpad_ref, dpad_ref, w, da, width, t):
    dpad_ref[0:t, :] = da
    dpad_ref[t:t + PAD, :] = jnp.zeros((PAD, da.shape[1]), f32)
    dx = dpad_ref[width - 1:width - 1 + t, :] * w[0:1, :]
    for j in range(1, width):
        dx = dx + dpad_ref[width - 1 - j:width - 1 - j + t, :] * w[j:j + 1, :]
    dws = [jnp.sum(da * xpad_ref[PAD - width + 1 + j:PAD - width + 1 + j + t, :], axis=0, keepdims=True) for j in range(width)]
    return dx, dws


def _load_pad(pad_ref, x, t):
    pad_ref[0:PAD, :] = jnp.zeros((PAD, x.shape[1]), f32)
    pad_ref[PAD:PAD + t, :] = x


assert ML_DK == 128


def qk_conv(z_main, conv_qk):
    t = z_main.shape[0]
    base = O_Q // ML_DK

    def body(z_ref, w_ref, o_ref, pad):
        _load_pad(pad, z_ref[...], t)
        a = _conv_taps(pad, w_ref[...], ML_CONV, t)
        sc = jnp.where(pl.program_id(0) < ML_HEADS, ML_DK ** -0.5, 1.0)
        o_ref[0] = jax.nn.silu(a) * sc

    return _call(body, "qk_conv", (2 * ML_HEADS,),
                 [pl.BlockSpec((t, ML_DK), lambda j: (0, base + j)), pl.BlockSpec((ML_CONV, ML_DK), lambda j: (0, j))],
                 pl.BlockSpec((1, t, ML_DK), lambda j: (j, 0, 0)), _sds((2 * ML_HEADS, t, ML_DK), f32),
                 [pltpu.VMEM((t + PAD, ML_DK), f32)])(z_main, conv_qk)


def qk_conv_bwd(z_main, conv_qk, dq, dk):
    t = z_main.shape[0]
    base = O_Q // ML_DK

    def body(z_ref, w_ref, dq_ref, dk_ref, dz_ref, dw_ref, pad, dpad):
        _load_pad(pad, z_ref[...], t)
        w = w_ref[...]
        a = _conv_taps(pad, w, ML_CONV, t)
        is_q = pl.program_id(0) < ML_HEADS
        d = jnp.where(is_q, dq_ref[0] * (ML_DK ** -0.5), dk_ref[0])
        _, vjp = jax.vjp(jax.nn.silu, a)
        da, = vjp(d)
        dx, dws = _conv_bwd(pad, dpad, w, da, ML_CONV, t)
        dz_ref[...] = dx.astype(bf16)
        for j in range(ML_CONV):
            dw_ref[j:j + 1, :] = dws[j]

    head = lambda pick: pl.BlockSpec((1, t, ML_DK), lambda j: (pick(j), 0, 0))
    return _call(body, "qk_conv_bwd", (2 * ML_HEADS,),
                 [pl.BlockSpec((t, ML_DK), lambda j: (0, base + j)), pl.BlockSpec((ML_CONV, ML_DK), lambda j: (0, j)),
                  head(lambda j: jnp.minimum(j, ML_HEADS - 1)), head(lambda j: jnp.maximum(j - ML_HEADS, 0))],
                 [pl.BlockSpec((t, ML_DK), lambda j: (0, j)), pl.BlockSpec((ML_CONV, ML_DK), lambda j: (0, j))],
                 [_sds((t, 2 * ML_QK), bf16), _sds((ML_CONV, 2 * ML_QK), f32)],
                 [pltpu.VMEM((t + PAD, ML_DK), f32), pltpu.VMEM((t + PAD, ML_DK), f32)])(z_main, conv_qk, dq, dk)


def glu_fwd(hup, conv_w, bias):
    t, f2 = hup.shape
    nf = f2 // 2 // 128

    def body(h1_ref, h2_ref, w1_ref, w2_ref, b1_ref, b2_ref, o_ref, pad):
        _load_pad(pad, h1_ref[...], t)
        a1 = _conv_taps(pad, w1_ref[...], FFN_CONV, t) + b1_ref[...]
        _load_pad(pad, h2_ref[...], t)
        a2 = _conv_taps(pad, w2_ref[...], FFN_CONV, t) + b2_ref[...]
        o_ref[...] = (jax.nn.silu(a1) * a2).astype(bf16)

    col = lambda off: pl.BlockSpec((t, 128), lambda j: (0, j + off))
    wsp = lambda off: pl.BlockSpec((FFN_CONV, 128), lambda j: (0, j + off))
    bsp = lambda off: pl.BlockSpec((1, 128), lambda j: (0, j + off))
    return _call(body, "glu_fwd", (nf,), [col(0), col(nf), wsp(0), wsp(nf), bsp(0), bsp(nf)], col(0), _sds((t, f2 // 2), bf16),
                 [pltpu.VMEM((t + PAD, 128), f32)])(hup, hup, conv_w, conv_w, bias, bias)


def glu_bwd(hup, conv_w, bias, dg, exchange=None):
    t, f2 = hup.shape
    f = f2 // 2
    nf = f // 128

    def body(h1_ref, h2_ref, w1_ref, w2_ref, b1_ref, b2_ref, dg_ref, dh1_ref, dh2_ref, dw1_ref, dw2_ref, db1_ref, db2_ref,
             pad1, pad2, dpad):
        _load_pad(pad1, h1_ref[...], t)
        _load_pad(pad2, h2_ref[...], t)
        w1, w2 = w1_ref[...], w2_ref[...]
        a1 = _conv_taps(pad1, w1, FFN_CONV, t) + b1_ref[...]
        a2 = _conv_taps(pad2, w2, FFN_CONV, t) + b2_ref[...]
        d = dg_ref[...]
        _, vjp = jax.vjp(jax.nn.silu, a1)
        da1, = vjp(d * a2)
        da2 = d * jax.nn.silu(a1)
        for da, pad, w, dh_ref, dw_ref, db_ref in ((da1, pad1, w1, dh1_ref, dw1_ref, db1_ref), (da2, pad2, w2, dh2_ref, dw2_ref, db2_ref)):
            dx, dws = _conv_bwd(pad, dpad, w, da, FFN_CONV, t)
            dh_ref[...] = dx.astype(bf16)
            for j in range(FFN_CONV):
                dw_ref[j:j + 1, :] = dws[j]
            db_ref[...] = jnp.sum(da, axis=0, keepdims=True)

    col = lambda off: pl.BlockSpec((t, 128), lambda j: (0, j + off))
    wsp = lambda off: pl.BlockSpec((FFN_CONV, 128), lambda j: (0, j + off))
    bsp = lambda off: pl.BlockSpec((1, 128), lambda j: (0, j + off))
    return _call(body, "glu_bwd", (nf,), [col(0), col(nf), wsp(0), wsp(nf), bsp(0), bsp(nf), col(0)],
                 [col(0), col(0), wsp(0), wsp(0), bsp(0), bsp(0)],
                 [_sds((t, f), bf16)] * 2 + [_sds((FFN_CONV, f), f32)] * 2 + [_sds((1, f), f32)] * 2,
                 [pltpu.VMEM((t + PAD, 128), f32)] * 3, exchange=exchange)(hup, hup, conv_w, conv_w, bias, bias, dg)


def gate_act(z_tail, b_tile):
    t = z_tail.shape[0]
    tm = _tile(t, (512, 256, 128, 64))

    def body(z_ref, b_ref, o_ref):
        x = z_ref[...] + b_ref[...]
        lane = lax.broadcasted_iota(jnp.int32, x.shape, 1)
        o_ref[...] = jnp.where(lane < T_F, x, jax.nn.log_sigmoid(x))

    row = pl.BlockSpec((tm, TAIL), lambda i: (i, 0))
    return _call(body, "gate_act", (t // tm,), [row, pl.BlockSpec((1, TAIL), lambda i: (0, 0))], row, _sds((t, TAIL), f32))(z_tail, b_tile)


def tail_bwd(z_tail, b_tile, dzt_pe, dgate):
    t = z_tail.shape[0]
    tm = _tile(t, (512, 256, 128, 64))

    def body(z_ref, b_ref, dpe_ref, dg_ref, dz_ref, db_ref):
        x = z_ref[...] + b_ref[...]
        lane = lax.broadcasted_iota(jnp.int32, x.shape, 1)
        _, vjp = jax.vjp(jax.nn.log_sigmoid, x)
        df, = vjp(dg_ref[...])
        dgates = jnp.where(lane < T_F, dg_ref[...], df)
        dgates = jnp.where(jnp.logical_and(lane >= T_I, lane < T_F + ML_HEADS), dgates, 0.0)
        dz_ref[...] = jnp.where(lane < ROPE, dpe_ref[...], dgates).astype(bf16)
        _acc_row(db_ref, jnp.sum(dgates, axis=0, keepdims=True), pl.program_id(0) == 0)

    row = pl.BlockSpec((tm, TAIL), lambda i: (i, 0))
    return _call(body, "tail_bwd", (t // tm,), [row, pl.BlockSpec((1, TAIL), lambda i: (0, 0)), row, row],
                 [row, pl.BlockSpec((8, TAIL), lambda i: (0, 0))], [_sds((t, TAIL), bf16), _sds((8, TAIL), f32)])(z_tail, b_tile, dzt_pe, dgate)


def _hdot(a, b, ca, cb):
    return lax.dot_general(a.astype(bf16), b.astype(bf16), (((ca,), (cb,)), ((0,), (0,))), preferred_element_type=f32)


def _mlstm_step(q, k, v, igr, fgr, c_mat, n_vec, m):
    nh, ln = q.shape[0], CHUNK
    sq = (nh, ln, ln)
    row = lax.broadcasted_iota(jnp.int32, sq, 1)
    col = lax.broadcasted_iota(jnp.int32, sq, 2)
    eye = row == col

    def to_col(r):
        return jnp.sum(jnp.where(eye, jnp.broadcast_to(r, sq), 0.0), axis=2, keepdims=True)

    bc_r = jnp.sum(jnp.where(row <= col, jnp.broadcast_to(to_col(fgr), sq), 0.0), axis=1, keepdims=True)
    bc_c = to_col(bc_r)
    logw = jnp.where(col <= row, bc_c - bc_r + igr, -jnp.inf)
    inter = bc_c + m
    m_t = jnp.maximum(inter, jnp.max(logw, axis=2, keepdims=True))
    w_intra = jnp.exp(logw - m_t)
    w_inter = jnp.exp(inter - m_t)
    sc = _hdot(q, k, 2, 2) * w_intra
    num = w_inter * _hdot(q, c_mat, 2, 1) + _hdot(sc, v, 2, 1)
    qn = jnp.sum(q.astype(bf16).astype(f32) * n_vec.astype(bf16).astype(f32), axis=2, keepdims=True)
    den = w_inter * qn + jnp.sum(sc, axis=2, keepdims=True)
    h = num / jnp.maximum(jnp.abs(den), jnp.exp(-m_t))
    lane = lax.broadcasted_iota(jnp.int32, (nh, 1, ln), 2)
    b_last = jnp.sum(jnp.where(lane == ln - 1, bc_r, 0.0), axis=2, keepdims=True)
    logu = b_last - bc_r + igr
    m_new = jnp.maximum(b_last + m, jnp.max(logu, axis=2, keepdims=True))
    decay = jnp.exp(b_last + m - m_new)
    u_c = to_col(jnp.exp(logu - m_new))
    c_new = decay * c_mat + _hdot(u_c * k, v, 1, 1)
    n_new = decay * n_vec + jnp.sum(u_c.astype(bf16).astype(f32) * k.astype(bf16).astype(f32), axis=1, keepdims=True)
    return h, c_new, n_new, m_new


ML_VHALF = ML_V // 2
assert O_V % ML_VHALF == 0 and ML_HEADS % 2 == 0


def _ml_specs(nc, rev):
    cc = (lambda c: nc - 1 - c) if rev else (lambda c: c)
    q = pl.BlockSpec((ML_HEADS, CHUNK, ML_DK), lambda c: (0, cc(c), 0))
    k = pl.BlockSpec((ML_HEADS, CHUNK, ML_DK), lambda c: (1, cc(c), 0))
    v_lo = pl.BlockSpec((CHUNK, ML_VHALF), lambda c: (cc(c), O_V // ML_VHALF))
    v_hi = pl.BlockSpec((CHUNK, ML_VHALF), lambda c: (cc(c), O_V // ML_VHALF + 1))
    hv = pl.BlockSpec((ML_HEADS, CHUNK, ML_DV), lambda c: (0, cc(c), 0))
    gate = pl.BlockSpec((ML_HEADS, 1, 1, CHUNK), lambda c: (0, cc(c), 0, 0))
    cm = pl.BlockSpec((ML_HEADS, 1, ML_DK, ML_DV), lambda c: (0, cc(c), 0, 0))
    nv = pl.BlockSpec((ML_HEADS, 1, 1, ML_DK), lambda c: (0, cc(c), 0, 0))
    ms = pl.BlockSpec((ML_HEADS, 1, 1, 1), lambda c: (0, cc(c), 0, 0))
    return q, k, v_lo, v_hi, hv, gate, cm, nv, ms


_ML_STATE = [pltpu.VMEM((ML_HEADS, ML_DK, ML_DV), f32), pltpu.VMEM((ML_HEADS, 1, ML_DK), f32), pltpu.VMEM((ML_HEADS, 1, 1), f32)]


def _ml_zero_state(c_s, n_s, m_s):
    @pl.when(pl.program_id(0) == 0)
    def _():
        c_s[...] = jnp.zeros_like(c_s)
        n_s[...] = jnp.zeros_like(n_s)
        m_s[...] = jnp.zeros_like(m_s)


def _ml_heads_of(v_lo_ref, v_hi_ref):
    half = ML_HEADS // 2
    return jnp.stack([r[:, j * ML_DV:(j + 1) * ML_DV] for r in (v_lo_ref, v_hi_ref) for j in range(half)])


def mlstm_fwd(qk_act, z_main, ig, fg):
    t = qk_act.shape[1]
    nc = t // CHUNK

    def body(q_ref, k_ref, vl_ref, vh_ref, ig_ref, fg_ref, h_ref, c_out, n_out, m_out, c_s, n_s, m_s):
        _ml_zero_state(c_s, n_s, m_s)
        c0, n0, m0 = c_s[...], n_s[...], m_s[...]
        c_out[:, 0] = c0
        n_out[:, 0] = n0
        m_out[:, 0] = m0
        h, c2, n2, m2 = _mlstm_step(q_ref[...], k_ref[...], _ml_heads_of(vl_ref, vh_ref), ig_ref[:, 0], fg_ref[:, 0], c0, n0, m0)
        h_ref[...] = h
        c_s[...] = c2
        n_s[...] = n2
        m_s[...] = m2

    q, k, v_lo, v_hi, hv, gate, cm, nv, ms = _ml_specs(nc, False)
    return _call(body, "mlstm_fwd", (nc,), [q, k, v_lo, v_hi, gate, gate], [hv, cm, nv, ms],
                 [_sds((ML_HEADS, t, ML_DV), f32), _sds((ML_HEADS, nc, ML_DK, ML_DV), f32), _sds((ML_HEADS, nc, 1, ML_DK), f32),
                  _sds((ML_HEADS, nc, 1, 1), f32)], _ML_STATE)(qk_act, qk_act, z_main, z_main, ig, fg)


def mlstm_bwd(qk_act, z_main, ig, fg, c_all, n_all, m_all, dh, exchange=None):
    t = qk_act.shape[1]
    nc = t // CHUNK

    def body(q_ref, k_ref, vl_ref, vh_ref, ig_ref, fg_ref, c_ref, n_ref, m_ref, dh_ref, dq_ref, dk_ref, dv_ref, dig_ref, dfg_ref,
             dc_s, dn_s, dm_s):
        _ml_zero_state(dc_s, dn_s, dm_s)
        _, vjp = jax.vjp(_mlstm_step, q_ref[...], k_ref[...], _ml_heads_of(vl_ref, vh_ref), ig_ref[:, 0], fg_ref[:, 0],
                         c_ref[:, 0], n_ref[:, 0], m_ref[:, 0])
        dq, dk, dv, dig, dfg, dc, dn, dm = vjp((dh_ref[...], dc_s[...], dn_s[...], dm_s[...]))
        dq_ref[...] = dq
        dk_ref[...] = dk
        for j in range(ML_HEADS):
            dv_ref[:, j * ML_DV:(j + 1) * ML_DV] = dv[j].astype(bf16)
        dig_ref[:, 0] = dig
        dfg_ref[:, 0] = dfg
        dc_s[...] = dc
        dn_s[...] = dn
        dm_s[...] = dm

    q, k, v_lo, v_hi, hv, gate, cm, nv, ms = _ml_specs(nc, True)
    gshape = _sds((ML_HEADS, nc, 1, CHUNK), f32)
    return _call(body, "mlstm_bwd", (nc,), [q, k, v_lo, v_hi, gate, gate, cm, nv, ms, hv],
                 [q, q, pl.BlockSpec((CHUNK, ML_V), lambda c: (nc - 1 - c, 0)), gate, gate],
                 [_sds((ML_HEADS, t, ML_DK), f32), _sds((ML_HEADS, t, ML_DK), f32), _sds((t, ML_V), bf16), gshape, gshape],
                 _ML_STATE, exchange=exchange)(qk_act, qk_act, z_main, z_main, ig, fg, c_all, n_all, m_all, dh)


def _ml_out(h, zo, g):
    return _rms(h, g) * jax.nn.sigmoid(zo)


def mlstm_out(h, z_main, g_hnorm):
    t = h.shape[1]
    tm = _tile(t, (512, 256, 128, 64))
    zo = O_O // ML_DV

    def body(h_ref, z_ref, g_ref, y_ref):
        y_ref[...] = _ml_out(h_ref[0], z_ref[...], g_ref[0]).astype(bf16)

    return _call(body, "mlstm_out", (t // tm, ML_HEADS),
                 [pl.BlockSpec((1, tm, ML_DV), lambda i, hd: (hd, i, 0)), pl.BlockSpec((tm, ML_DV), lambda i, hd: (i, zo + hd)),
                  pl.BlockSpec((1, 1, ML_DV), lambda i, hd: (hd, 0, 0))],
                 pl.BlockSpec((tm, ML_DV), lambda i, hd: (i, hd)), _sds((t, ML_V), bf16))(h, z_main, g_hnorm)


def mlstm_out_bwd(h, z_main, g_hnorm, dy):
    t = h.shape[1]
    tm = _tile(t, (512, 256, 128, 64))
    zo = O_O // ML_DV

    def body(h_ref, z_ref, g_ref, dy_ref, dh_ref, dzo_ref, dg_ref):
        _, vjp = jax.vjp(_ml_out, h_ref[0], z_ref[...], g_ref[0])
        dh, dz, dg = vjp(dy_ref[...])
        dh_ref[0] = dh
        dzo_ref[...] = dz.astype(bf16)

        @pl.when(pl.program_id(1) == 0)
        def _():
            dg_ref[...] = jnp.zeros_like(dg_ref)

        dg_ref[0, 0:1, :] += dg

    head = pl.BlockSpec((1, tm, ML_DV), lambda hd, i: (hd, i, 0))
    blk = pl.BlockSpec((tm, ML_DV), lambda hd, i: (i, hd))
    return _call(body, "mlstm_out_bwd", (ML_HEADS, t // tm),
                 [head, pl.BlockSpec((tm, ML_DV), lambda hd, i: (i, zo + hd)), pl.BlockSpec((1, 1, ML_DV), lambda hd, i: (hd, 0, 0)), blk],
                 [head, blk, pl.BlockSpec((1, 8, ML_DV), lambda hd, i: (hd, 0, 0))],
                 [_sds((ML_HEADS, t, ML_DV), f32), _sds((t, ML_V), bf16), _sds((ML_HEADS, 8, ML_DV), f32)])(h, z_main, g_hnorm, dy)


def _merge(ga, gb, ya, yb):
    return jax.nn.sigmoid(ga) * ya + jax.nn.sigmoid(gb) * yb


def _merge_specs(t, d):
    tm = _tile(t, (512, 256, 128, 64))
    bw = _tile(d, (512, 256, 128))
    assert O_GA % bw == 0 and (O_GA + d) % bw == 0
    blk = pl.BlockSpec((tm, bw), lambda i, j: (i, j))
    ga = pl.BlockSpec((tm, bw), lambda i, j: (i, O_GA // bw + j))
    gb = pl.BlockSpec((tm, bw), lambda i, j: (i, (O_GA + d) // bw + j))
    return tm, bw, blk, ga, gb


def merge_fwd(z_main, ya, yb):
    t, d = ya.shape
    tm, bw, blk, ga, gb = _merge_specs(t, d)

    def body(ga_ref, gb_ref, ya_ref, yb_ref, o_ref):
        o_ref[...] = _merge(ga_ref[...], gb_ref[...], ya_ref[...], yb_ref[...]).astype(bf16)

    return _call(body, "merge_fwd", (t // tm, d // bw), [ga, gb, blk, blk], blk, _sds((t, d), bf16))(z_main, z_main, ya, yb)


def merge_bwd(z_main, ya, yb, dmerged):
    t, d = ya.shape
    tm, bw, blk, ga, gb = _merge_specs(t, d)

    def body(ga_ref, gb_ref, ya_ref, yb_ref, dm_ref, dga_ref, dgb_ref, dya_ref, dyb_ref):
        _, vjp = jax.vjp(_merge, ga_ref[...], gb_ref[...], ya_ref[...], yb_ref[...])
        dga, dgb, dya, dyb = vjp(dm_ref[...])
        dga_ref[...] = dga.astype(bf16)
        dgb_ref[...] = dgb.astype(bf16)
        dya_ref[...] = dya.astype(bf16)
        dyb_ref[...] = dyb.astype(bf16)

    return _call(body, "merge_bwd", (t // tm, d // bw), [ga, gb, blk, blk, blk], [blk] * 4, [_sds((t, d), bf16)] * 4)(
        z_main, z_main, ya, yb, dmerged)


def _cross(cq, ck, cv, gq, gk):
    outs = []
    for hd in range(CR_HEADS):
        sl = slice(hd * CR_HD, (hd + 1) * CR_HD)
        q = _rms(cq[:, sl], gq)
        k = _rms(ck[:, sl], gk)
        s = _bdot(q, k, 1, 1) * (CR_HD ** -0.5)
        p = jax.nn.softmax(s, axis=-1)
        outs.append(_bdot(p, cv[:, sl], 1, 0))
    return jnp.concatenate(outs, axis=1)


def cross_fwd(cq, ck, cv, gq, gk):
    t, w = cq.shape
    nm = ck.shape[0]
    tm = _tile(t, (512, 256, 128, 64))

    def body(q_ref, k_ref, v_ref, gq_ref, gk_ref, o_ref):
        o_ref[...] = _cross(q_ref[...], k_ref[...], v_ref[...], gq_ref[...], gk_ref[...]).astype(bf16)

    row = pl.BlockSpec((tm, w), lambda i: (i, 0))
    full = pl.BlockSpec((nm, w), lambda i: (0, 0))
    gain = pl.BlockSpec((1, CR_HD), lambda i: (0, 0))
    return _call(body, "cross_fwd", (t // tm,), [row, full, full, gain, gain], row, _sds((t, w), bf16))(cq, ck, cv, gq, gk)


def cross_bwd(cq, ck, cv, gq, gk, do):
    t, w = cq.shape
    nm = ck.shape[0]
    tm = _tile(t, (512, 256, 128, 64))

    def body(q_ref, k_ref, v_ref, gq_ref, gk_ref, do_ref, dq_ref, dk_ref, dv_ref, dgq_ref, dgk_ref):
        first = pl.program_id(0) == 0
        _, vjp = jax.vjp(_cross, q_ref[...], k_ref[...], v_ref[...], gq_ref[...], gk_ref[...])
        dq, dk, dv, dgq, dgk = vjp(do_ref[...])
        dq_ref[...] = dq.astype(bf16)

        @pl.when(first)
        def _():
            dk_ref[...] = jnp.zeros_like(dk_ref)
            dv_ref[...] = jnp.zeros_like(dv_ref)

        dk_ref[...] += dk
        dv_ref[...] += dv
        _acc_row(dgq_ref, dgq, first)
        _acc_row(dgk_ref, dgk, first)

    row = pl.BlockSpec((tm, w), lambda i: (i, 0))
    full = pl.BlockSpec((nm, w), lambda i: (0, 0))
    gain = pl.BlockSpec((1, CR_HD), lambda i: (0, 0))
    acc = pl.BlockSpec((8, CR_HD), lambda i: (0, 0))
    return _call(body, "cross_bwd", (t // tm,), [row, full, full, gain, gain, row], [row, full, full, acc, acc],
                 [_sds((t, w), bf16), _sds((nm, w), f32), _sds((nm, w), f32), _sds((8, CR_HD), f32), _sds((8, CR_HD), f32)])(
        cq, ck, cv, gq, gk, do)


def loss_head(x2, fo, target):
    t, d = x2.shape
    tm = _tile(t, (256, 128, 64, 32, 16, 8))

    def body(a_ref, b_ref, t_ref, dx_ref, dxb_ref, l_ref):
        err = a_ref[...] + b_ref[...] - t_ref[...]
        dx = err / d
        dx_ref[...] = dx
        dxb_ref[...] = dx.astype(bf16)
        part = 0.5 * jnp.sum(jnp.mean(err * err, axis=1, keepdims=True), axis=0, keepdims=True)
        _acc_row(l_ref, jnp.broadcast_to(part, (1, 128)), pl.program_id(0) == 0)

    row = pl.BlockSpec((tm, d), lambda i: (i, 0))
    return _call(body, "loss_head", (t // tm,), [row, row, row], [row, row, pl.BlockSpec((8, 128), lambda i: (0, 0))],
                 [_sds((t, d), f32), _sds((t, d), bf16), _sds((8, 128), f32)])(x2, fo, target)


def _place():
    x, y, c = lax.axis_index("x"), lax.axis_index("y"), lax.axis_index("c")
    peers = {}
    for r in range(1, N_DEV):
        px = 1 - x if r & 4 else x
        py = 1 - y if r & 2 else y
        pc = 1 - c if r & 1 else c
        peers[r] = ((px, py, pc), 4 * px + 2 * py + pc)
    return 4 * x + 2 * y + c, peers


N_REL = N_DEV - 1
RELATIONS = tuple(range(1, N_DEV))
SIBLING = 1
OTHER_CHIPS = (2, 4, 6)
PASSED_ON = (3, 5, 7)


def _exchange_ops(ins, outs, sems, scatter):
    n = len(ins)
    send_sems, recv_sems, local_sems = sems

    def tools():
        me, peers = _place()

        def copy(a, r, src, dst_idx, to):
            return pltpu.make_async_remote_copy(
                src_ref=src, dst_ref=outs[a].at[dst_idx], send_sem=send_sems.at[a * N_REL + r - 1],
                recv_sem=recv_sems.at[a * N_REL + r - 1], device_id=peers[to][0], device_id_type=MESH)

        def local(a):
            return pltpu.make_async_copy(ins[a].at[me] if scatter else ins[a], outs[a].at[me], local_sems.at[a])

        def arrival(a, r):
            return copy(a, r, ins[a].at[me] if scatter else ins[a], peers[r][1], r)

        return me, peers, copy, local, arrival

    if scatter:
        def sends():
            me, peers, copy, local, _ = tools()
            return [local(a) for a in range(n)], [copy(a, r, ins[a].at[peers[r][1]], me, r) for a in range(n) for r in RELATIONS]

        def start():
            loc, out = sends()
            for cp in loc + out:
                cp.start()

        middle = None
        waited_last = RELATIONS
    else:
        def sends():
            me, peers, copy, local, _ = tools()
            own = [copy(a, r, ins[a], me, r) for a in range(n) for r in (SIBLING,) + OTHER_CHIPS]
            return [local(a) for a in range(n)], own

        def passes():
            me, peers, copy, _, _ = tools()
            return [copy(a, r, outs[a].at[peers[r - 1][1]], peers[r - 1][1], SIBLING) for a in range(n) for r in PASSED_ON]

        def start():
            loc, out = sends()
            for cp in loc + out:
                cp.start()

        def middle():
            _, _, _, _, arrival = tools()
            fwd = passes()
            for a in range(n):
                for i, r in enumerate(PASSED_ON):
                    arrival(a, r - 1).wait_recv()
                    fwd[a * len(PASSED_ON) + i].start()

        waited_last = (SIBLING,) + PASSED_ON

    def wait():
        _, _, _, _, arrival = tools()
        for a in range(n):
            for r in waited_last:
                arrival(a, r).wait_recv()
        loc, out = sends()
        for cp in out + ([] if scatter else passes()):
            cp.wait_send()
        for cp in loc:
            cp.wait()

    return start, middle, wait


def _exchange_shapes(arrs, scatter):
    return [_sds(a.shape if scatter else (N_DEV,) + a.shape, a.dtype) for a in arrs]


def _exchange_sems(n):
    return [pltpu.SemaphoreType.DMA((n * N_REL,)), pltpu.SemaphoreType.DMA((n * N_REL,)), pltpu.SemaphoreType.DMA((n,))]


def _exchange(arrs, name, scatter):
    n = len(arrs)

    def body(*refs):
        start, middle, wait = _exchange_ops(refs[:n], refs[n:2 * n], refs[2 * n:], scatter)
        start()
        if middle is not None:
            middle()
        wait()

    any_spec = pl.BlockSpec(memory_space=pl.ANY)
    return pl.pallas_call(body, name=name, in_specs=[any_spec] * n, out_specs=[any_spec] * n,
                          out_shape=_exchange_shapes(arrs, scatter), scratch_shapes=_exchange_sems(n))(*arrs)


def cast_bf16(w, name):
    _, r, c = w.shape
    tr = _tile(r, (256, 128, 64, 32, 16))

    def body(w_ref, o_ref):
        o_ref[...] = w_ref[0].astype(bf16)

    return _call(body, name, (r // tr,), [pl.BlockSpec((1, tr, c), lambda i: (0, i, 0))], pl.BlockSpec((tr, c), lambda i: (i, 0)),
                 _sds((r, c), bf16))(w)


def _adamw(w, g, m, v):
    m = ADAM_B1 * m + (1.0 - ADAM_B1) * g
    v = ADAM_B2 * v + (1.0 - ADAM_B2) * jnp.square(g)
    m_hat = m / (1.0 - ADAM_B1 ** ADAM_STEP)
    v_hat = v / (1.0 - ADAM_B2 ** ADAM_STEP)
    delta = -ADAM_LR * (m_hat / (jnp.sqrt(v_hat) + ADAM_EPS) + ADAM_WD * w)
    return delta, m, v


def adam_sum(parts, w, m, v, name):
    _, r, c = parts.shape
    budget = 4 * 1024 * 1024
    tr = r
    for cand in (1024, 512, 256, 128, 64, 32, 16):
        if r % cand == 0 and N_DEV * cand * c * 4 <= budget:
            tr = cand
            break

    def body(p_ref, w_ref, m_ref, v_ref, g_ref, d_ref, m2_ref, v2_ref):
        g = p_ref[0].astype(f32)
        for k in range(1, N_DEV):
            g = g + p_ref[k].astype(f32)
        d, m2, v2 = _adamw(w_ref[0], g, m_ref[0], v_ref[0])
        g_ref[...] = g
        d_ref[...] = d
        m2_ref[...] = m2
        v2_ref[...] = v2

    blk = pl.BlockSpec((1, tr, c), lambda i: (0, i, 0))
    out = pl.BlockSpec((tr, c), lambda i: (i, 0))
    return _call(body, name, (r // tr,), [pl.BlockSpec((N_DEV, tr, c), lambda i: (0, i, 0)), blk, blk, blk], [out] * 4,
                 [_sds((r, c), f32)] * 4)(parts, w, m, v)


def sum_parts(parts, name):
    _, r, c = parts.shape

    def body(p_ref, o_ref):
        g = p_ref[0]
        for k in range(1, N_DEV):
            g = g + p_ref[k]
        o_ref[...] = g

    return pl.pallas_call(body, name=name, out_shape=_sds((r, c), f32))(parts)


def adam_flat(w, g, m, v, name):
    def body(w_ref, g_ref, m_ref, v_ref, d_ref, m2_ref, v2_ref):
        d, m2, v2 = _adamw(w_ref[...], g_ref[...], m_ref[...], v_ref[...])
        d_ref[...] = d
        m2_ref[...] = m2
        v2_ref[...] = v2

    return pl.pallas_call(body, name=name, out_shape=[_sds(w.shape, f32)] * 3)(w, g, m, v)


def _pack(vecs, multiple):
    flat = jnp.concatenate([v.reshape(-1) for v in vecs])
    n = flat.shape[0]
    total = -(-n // multiple) * multiple
    return jnp.pad(flat, (0, total - n))


def _unpack(flat, shapes):
    out, pos = [], 0
    for s in shapes:
        n = 1
        for d in s:
            n *= d
        out.append(flat[pos:pos + n].reshape(s))
        pos += n
    return out


def _pad_lanes(v, width=TAIL):
    return jnp.pad(v, ((0, 0), (0, width - v.shape[1])))


def kernel(x, mem, positions, g_mix, w_in, g_qa, w_qb, g_kva, w_kvb, g_qn_nope, g_qn_pe, g_kn_nope, g_kn_pe, conv_qk, b_if, g_hnorm, p_a, p_b, w_out, g_cross, g_mem, wq_c, wk_c, wv_c, g_cq, g_ck, wo_c, g_ffn, w_up, conv_ffn, b_conv_ffn, w_down, loss_target, m_g_mix, m_w_in, m_g_qa, m_w_qb, m_g_kva, m_w_kvb, m_g_qn_nope, m_g_qn_pe, m_g_kn_nope, m_g_kn_pe, m_conv_qk, m_b_if, m_g_hnorm, m_p_a, m_p_b, m_w_out, m_g_cross, m_g_mem, m_wq_c, m_wk_c, m_wv_c, m_g_cq, m_g_ck, m_wo_c, m_g_ffn, m_w_up, m_conv_ffn, m_b_conv_ffn, m_w_down, v_g_mix, v_w_in, v_g_qa, v_w_qb, v_g_kva, v_w_kvb, v_g_qn_nope, v_g_qn_pe, v_g_kn_nope, v_g_kn_pe, v_conv_qk, v_b_if, v_g_hnorm, v_p_a, v_p_b, v_w_out, v_g_cross, v_g_mem, v_wq_c, v_wk_c, v_wv_c, v_g_cq, v_g_ck, v_wo_c, v_g_ffn, v_w_up, v_conv_ffn, v_b_conv_ffn, v_w_down):
    args = dict(locals())
    names = ['g_mix', 'w_in', 'g_qa', 'w_qb', 'g_kva', 'w_kvb', 'g_qn_nope', 'g_qn_pe', 'g_kn_nope', 'g_kn_pe', 'conv_qk', 'b_if',
             'g_hnorm', 'p_a', 'p_b', 'w_out', 'g_cross', 'g_mem', 'wq_c', 'wk_c', 'wv_c', 'g_cq', 'g_ck', 'wo_c', 'g_ffn', 'w_up',
             'conv_ffn', 'b_conv_ffn', 'w_down']
    big = ['w_in', 'w_qb', 'w_kvb', 'p_a', 'p_b', 'w_out', 'wq_c', 'wk_c', 'wv_c', 'wo_c', 'w_up', 'w_down']
    sharded_small = ['conv_qk', 'g_hnorm', 'conv_ffn']
    replicated = [n for n in names if n not in big and n not in sharded_small]

    t, d = x.shape[1], x.shape[2]
    x2d, tgt = x[0], loss_target[0]
    mem2d = mem[0]
    me = 4 * lax.axis_index("x") + 2 * lax.axis_index("y") + lax.axis_index("c")
    nc = t // CHUNK
    f2 = b_conv_ffn.shape[1]
    wmain = O_GA + 2 * d

    first = ['w_in', 'w_qb', 'w_kvb']
    behind_in = ['p_a', 'p_b', 'w_out', 'wq_c', 'wk_c', 'wv_c', 'wo_c']
    shards = {n: cast_bf16(args[n], "cast_" + n) for n in big}
    small_local = _pack([args[n] for n in sharded_small], 128).reshape(1, -1)
    gathered = _exchange([shards[n] for n in first] + [small_local], "comm_gather_first", scatter=False)
    gw = dict(zip(first, gathered[:-1]))
    small_all = gathered[-1]
    full_small, pos = [], 0
    for n in sharded_small:
        _, rows, cols = args[n].shape
        piece = small_all[:, 0, pos:pos + rows * cols].reshape(N_DEV, rows, cols)
        full_small.append(piece.transpose(1, 0, 2).reshape(rows, N_DEV * cols))
        pos += rows * cols
    conv_qk_f, g_hnorm_f, conv_ffn_f = full_small

    shard_w = w_in.shape[2]
    c_kpe, c_q, c_i, c_o = O_Q, O_Q + ROPE, O_Q + ROPE + 2 * ML_QK + ML_V, O_Q + ROPE + 2 * ML_QK + ML_V + 2 * ML_HEADS
    segments = [(0, c_kpe, 'main', 0), (c_kpe, c_q, 'tail', 0), (c_q, c_i, 'main', O_Q), (c_i, c_o, 'tail', T_I),
                (c_o, N_DEV * shard_w, 'main', O_O)]

    def shard_cuts(lo, hi):
        return [(j, max(lo, j * shard_w) - j * shard_w, min(hi, (j + 1) * shard_w) - j * shard_w)
                for j in range(lo // shard_w, (hi - 1) // shard_w + 1)]

    def gathered_cols(target):
        return [gw['w_in'][j][:, a:b] for lo, hi, tg, _ in segments if tg == target for j, a, b in shard_cuts(lo, hi)]

    w_main = jnp.concatenate(gathered_cols('main'), axis=1)[None]
    w_tail = jnp.concatenate(gathered_cols('tail') + [jnp.zeros((d, TAIL - ROPE - 2 * ML_HEADS), bf16)], axis=1)[None]
    assert w_main.shape[2] == wmain

    inv_freq = ROPE_BASE ** (-jnp.arange(0, ROPE, 2, dtype=f32) / ROPE)
    inv_tile = _pad_lanes(jnp.concatenate([inv_freq, inv_freq])[None])
    cos, sin = rope_tables(positions.reshape(t, 1), inv_tile)
    gqp, gkp = _pad_lanes(g_qn_pe), _pad_lanes(g_kn_pe)
    b_tile = jnp.pad(b_if, ((0, 0), (T_I, TAIL - T_I - 2 * ML_HEADS)))

    u0 = rms_fwd(x2d, g_mix, "rms_mix")
    z_main, got = mm_nn(u0, w_main, f32, "mm_in_main", exchange=([shards[n] for n in behind_in], False))
    gw.update(zip(behind_in, got))
    qb = gw['w_qb'].transpose(1, 0, 2).reshape(Q_LORA, MLA_HEADS, NOPE + ROPE)
    w_qb_p = jnp.concatenate([qb, jnp.zeros((Q_LORA, MLA_HEADS, HEAD_PAD - NOPE - ROPE), bf16)], axis=2).reshape(1, Q_LORA, -1)
    w_kvb3 = gw['w_kvb']
    p_a3, p_b3, w_out3 = (gw[n].reshape(1, -1, d) for n in ('p_a', 'p_b', 'w_out'))
    wq_c3, wk_c3, wv_c3 = (gw[n].reshape(1, d, -1) for n in ('wq_c', 'wk_c', 'wv_c'))
    wo_c3 = gw['wo_c']
    z_tail = mm_nn(u0, w_tail, f32, "mm_in_tail")
    qa_n, kv_n = lat_norm(z_main, g_qa, g_kva)
    q_raw = mm_nn(qa_n, w_qb_p, f32, "mm_qb")
    kv_raw = mm_nn(kv_n, w_kvb3, f32, "mm_kvb")
    qh, kh, vh = mla_prep(q_raw, kv_raw, z_tail, cos, sin, g_qn_nope, gqp, g_kn_nope, gkp)
    (o_a, o_ab, lse), (w_up3,) = mla_fwd(qh, kh, vh, exchange=([shards['w_up']], False))

    qk_act = qk_conv(z_main, conv_qk_f)
    gates = gate_act(z_tail, b_tile)

    def to_rows(cols):
        return cols.T.reshape(ML_HEADS, nc, 1, CHUNK)

    ig, fg = to_rows(gates[:, T_I:T_F]), to_rows(gates[:, T_F:T_F + ML_HEADS])
    h_ml, c_all, n_all, m_all = mlstm_fwd(qk_act, z_main, ig, fg)
    g_hn3 = g_hnorm_f.reshape(ML_HEADS, 1, ML_DV)
    y_b = mlstm_out(h_ml, z_main, g_hn3)

    ya = mm_nn(o_ab, p_a3, f32, "mm_pa")
    yb = mm_nn(y_b, p_b3, f32, "mm_pb")
    merged = merge_fwd(z_main, ya, yb)
    mo = mm_nn(merged, w_out3, f32, "mm_out")
    x1, uc = resid_rms(x2d, mo, g_cross, "resid_cross")
    mem_n = rms_fwd(mem2d, g_mem, "rms_mem")
    cq = mm_nn(uc, wq_c3, f32, "mm_cq")
    ck = mm_nn(mem_n, wk_c3, f32, "mm_ck")
    cv = mm_nn(mem_n, wv_c3, f32, "mm_cv")
    o_c = cross_fwd(cq, ck, cv, g_cq, g_ck)
    co = mm_nn(o_c, wo_c3, f32, "mm_oc")
    x2, u3 = resid_rms(x1, co, g_ffn, "resid_ffn")
    hup, (w_down_g,) = mm_nn(u3, w_up3, f32, "mm_up", exchange=([shards['w_down']], False))
    w_down3 = w_down_g.reshape(1, -1, d)
    gl = glu_fwd(hup, conv_ffn_f, b_conv_ffn)
    fo = mm_nn(gl, w_down3, f32, "mm_down")
    dx3, dx3_b, loss_acc = loss_head(x2, fo, tgt)

    grads, parts = {}, {}
    grads['w_down'] = mm_tn(gl, dx3_b, 1, "mm_d_wdown").reshape(N_DEV, -1, d)
    dgl = mm_nt(dx3_b, w_down3, f32, "mm_d_gl")
    (dh1, dh2, dcw1, dcw2, db1, db2), (parts['w_down'],) = glu_bwd(hup, conv_ffn_f, b_conv_ffn, dgl,
                                                                    exchange=([grads['w_down']], True))
    dconv_ffn, db_ffn = (jnp.concatenate(pair, axis=1) for pair in ((dcw1, dcw2), (db1, db2)))
    grads['w_up'] = mm_tn_cols(u3, [dh1, dh2], N_DEV, "mm_d_wup")
    du3 = mm_nt_cols([dh1, dh2], w_up3, f32, "mm_d_u3")
    dx2, dx2_b, dg_ffn = rms_bwd(x2, g_ffn, [du3], dx3, "rms_bwd_ffn", want_b16=True)
    grads['wo_c'] = mm_tn(o_c, dx2_b, N_DEV, "mm_d_woc")
    do_c = mm_nt(dx2_b, wo_c3, f32, "mm_d_oc")
    dcq, dck, dcv, dg_cq, dg_ck = cross_bwd(cq, ck, cv, g_cq, g_ck, do_c)
    grads['wq_c'] = mm_tn(uc, dcq, 1, "mm_d_wqc").reshape(N_DEV, -1, dcq.shape[1])
    grads['wk_c'] = mm_tn(mem_n, dck, 1, "mm_d_wkc").reshape(N_DEV, -1, dck.shape[1])
    grads['wv_c'] = mm_tn(mem_n, dcv, 1, "mm_d_wvc").reshape(N_DEV, -1, dcv.shape[1])
    duc = mm_nt(dcq, wq_c3, f32, "mm_d_uc")
    dmem_k = mm_nt(dck, wk_c3, f32, "mm_d_memk")
    dmem_v = mm_nt(dcv, wv_c3, f32, "mm_d_memv")
    dg_mem, = rms_bwd(mem2d, g_mem, [dmem_k, dmem_v], None, "rms_bwd_mem", want_dx=False)
    dx1, dx1_b, dg_cross = rms_bwd(x1, g_cross, [duc], dx2, "rms_bwd_cross", want_b16=True)
    grads['w_out'] = mm_tn(merged, dx1_b, 1, "mm_d_wout").reshape(N_DEV, -1, d)
    dmerged = mm_nt(dx1_b, w_out3, f32, "mm_d_merged")
    dga, dgb, dya, dyb = merge_bwd(z_main, ya, yb, dmerged)
    grads['p_a'] = mm_tn(o_ab, dya, 1, "mm_d_pa").reshape(N_DEV, -1, d)
    grads['p_b'] = mm_tn(y_b, dyb, 1, "mm_d_pb").reshape(N_DEV, -1, d)
    do_a = mm_nt(dya, p_a3, f32, "mm_d_oa")
    dy_b = mm_nt(dyb, p_b3, f32, "mm_d_yb")

    dh_ml, dzo, dg_hn = mlstm_out_bwd(h_ml, z_main, g_hn3, dy_b)
    mixers = ['p_a', 'p_b', 'w_out']
    (dq_act, dk_act, dzv, dig, dfg), got = mlstm_bwd(qk_act, z_main, ig, fg, c_all, n_all, m_all, dh_ml,
                                                     exchange=([grads[n] for n in mixers], True))
    parts.update(zip(mixers, got))
    dzqk, dconv_qk = qk_conv_bwd(z_main, conv_qk_f, dq_act, dk_act)

    delta = mla_delta(o_a, do_a)
    (dqh, dkh, dvh), (parts['w_up'],) = mla_bwd(qh, kh, vh, do_a, lse, delta.reshape(MLA_HEADS, 1, t),
                                                exchange=([grads['w_up']], True))
    dq_raw, dkv_raw, dzt_pe, dg_qn, dg_qp, dg_kn, dg_kp = mla_prep_bwd(
        q_raw, kv_raw, z_tail, cos, sin, g_qn_nope, gqp, g_kn_nope, gkp, dqh, dkh, dvh)
    d_wqb_p = mm_tn(qa_n, dq_raw, 1, "mm_d_wqb")[0].reshape(Q_LORA, MLA_HEADS, HEAD_PAD)[:, :, :NOPE + ROPE]
    grads['w_qb'] = d_wqb_p.reshape(Q_LORA, N_DEV, -1).transpose(1, 0, 2)
    grads['w_kvb'] = mm_tn(kv_n, dkv_raw, N_DEV, "mm_d_wkvb")
    dqa = mm_nt(dq_raw, w_qb_p, f32, "mm_d_qa")
    dkvn = mm_nt(dkv_raw, w_kvb3, f32, "mm_d_kvn")
    dz_lat, dg_qa, dg_kva = lat_norm_bwd(z_main, g_qa, g_kva, dqa, dkvn)

    def to_cols(rows):
        return rows.reshape(ML_HEADS, t).T

    dgate = jnp.pad(jnp.concatenate([to_cols(dig), to_cols(dfg)], axis=1), ((0, 0), (T_I, TAIL - T_I - 2 * ML_HEADS)))
    dz_tail, db_if = tail_bwd(z_tail, b_tile, dzt_pe, dgate)
    dz_main = [dz_lat, dzqk, dzv, dzo, dga, dgb]
    small_mats = ['wq_c', 'wk_c', 'wv_c', 'wo_c', 'w_qb', 'w_kvb']
    d_wmain3, got = mm_tn_cols(u0, dz_main, 1, "mm_d_wmain", exchange=([grads[n] for n in small_mats], True))
    parts.update(zip(small_mats, got))
    d_wmain = d_wmain3[0]
    d_wtail = mm_tn(u0, dz_tail, 1, "mm_d_wtail")[0]
    d_target = {'main': d_wmain, 'tail': d_wtail}
    blocks = []
    for j in range(N_DEV):
        lo_j, hi_j = j * shard_w, (j + 1) * shard_w
        cols = [d_target[tg][:, off + max(lo, lo_j) - lo:off + min(hi, hi_j) - lo]
                for lo, hi, tg, off in segments if lo < hi_j and hi > lo_j]
        blocks.append(jnp.concatenate(cols, axis=1))
    grads['w_in'] = jnp.stack(blocks)
    du0_a, (parts['w_in'],) = mm_nt_cols(dz_main, w_main, f32, "mm_d_u0_main", exchange=([grads['w_in']], True))
    du0_b = mm_nt(dz_tail, w_tail, f32, "mm_d_u0_tail")
    grad_x, dg_mix = rms_bwd(x2d, g_mix, [du0_a, du0_b], dx1, "rms_bwd_mix")

    out_g, out_d, out_m, out_v = {}, {}, {}, {}
    for n in big:
        res = adam_sum(parts[n], args[n], args['m_' + n], args['v_' + n], "adam_" + n)
        out_g[n], out_d[n], out_m[n], out_v[n] = (a.reshape(args[n].shape) for a in res)

    small_full = {
        'g_mix': dg_mix[0], 'g_qa': dg_qa[0], 'g_kva': dg_kva[0], 'g_qn_nope': dg_qn[0], 'g_qn_pe': dg_qp[0, :ROPE],
        'g_kn_nope': dg_kn[0], 'g_kn_pe': dg_kp[0, :ROPE], 'conv_qk': dconv_qk, 'b_if': db_if[0, T_I:T_I + 2 * ML_HEADS],
        'g_hnorm': dg_hn[:, 0, :], 'g_cross': dg_cross[0], 'g_mem': dg_mem[0], 'g_cq': dg_cq[0], 'g_ck': dg_ck[0],
        'g_ffn': dg_ffn[0], 'conv_ffn': dconv_ffn, 'b_conv_ffn': db_ffn[0], 'loss': loss_acc[0, :1]}
    order = list(small_full)
    packed = _pack([small_full[n] for n in order], 8 * 128).reshape(1, -1)
    gathered_small, = _exchange([packed], "comm_gather_small", scatter=False)
    summed = sum_parts(gathered_small.reshape(N_DEV, -1, 128), "sum_small").reshape(-1)
    full_g = dict(zip(order, _unpack(summed, [small_full[n].shape for n in order])))
    loss = full_g['loss'][0]

    local_g = {}
    for n in replicated:
        local_g[n] = full_g[n].reshape(args[n].shape)
    for n in sharded_small:
        shp = args[n].shape
        full = full_g[n].reshape((1,) + full_g[n].shape)
        local_g[n] = lax.dynamic_slice_in_dim(full, me * shp[-1], shp[-1], axis=2)
    small = replicated + sharded_small
    dl_f, m_f, v_f = adam_flat(*[_pack([src[n] if pre == '' else args[pre + n] for n in small], 8 * 128).reshape(-1, 128)
                                 for pre, src in (('', args), ('', local_g), ('m_', None), ('v_', None))], "adam_small")
    shapes = [args[n].shape for n in small]
    for dst, flat in ((out_d, dl_f), (out_m, m_f), (out_v, v_f)):
        dst.update(zip(small, _unpack(flat.reshape(-1), shapes)))
    out_g.update(local_g)

    return (loss, grad_x[None], *[out_g[n] for n in names], *[out_d[n] for n in names],
            *[out_m[n] for n in names], *[out_v[n] for n in names])
```

```python
import functools

import jax
import jax.numpy as jnp
from jax import lax
from jax.experimental import pallas as pl
from jax.experimental.pallas import tpu as pltpu

f32 = jnp.float32
bf16 = jnp.bfloat16

N_DEV = 8
EPS = 1e-6
CHUNK = 64
CHUNK_SHIFT = 6
assert 1 << CHUNK_SHIFT == CHUNK
MLA_HEADS = 16
Q_LORA = 512
KV_LORA = 512
NOPE = 128
ROPE = 64
V_HEAD = 128
ROPE_BASE = 10000.0
HEAD_PAD = 256
ML_HEADS = 8
ML_DK = 128
ML_DV = 256
ML_CONV = 4
ML_QK = ML_HEADS * ML_DK
ML_V = ML_HEADS * ML_DV
CR_HEADS = 4
CR_HD = 128
FFN_CONV = 3
ADAM_LR = 0.001
ADAM_B1 = 0.9
ADAM_B2 = 0.999
ADAM_EPS = 1e-08
ADAM_WD = 0.01
ADAM_STEP = 10
O_QA, O_KV, O_Q, O_K = 0, Q_LORA, Q_LORA + KV_LORA, Q_LORA + KV_LORA + ML_QK
O_V = O_K + ML_QK
O_O = O_V + ML_V
O_GA = O_O + ML_V
TAIL = 128
T_I, T_F = ROPE, ROPE + ML_HEADS
VMEM_LIMIT_V7X = 48 * 1024 * 1024
MESH = pl.DeviceIdType.MESH


def _call(body, name, grid, in_specs, out_specs, out_shape, scratch=(), exchange=None):
    params = pltpu.CompilerParams(vmem_limit_bytes=VMEM_LIMIT_V7X)
    if exchange is None:
        return pl.pallas_call(body, name=name, grid=grid, in_specs=in_specs, out_specs=out_specs, out_shape=out_shape,
                              scratch_shapes=list(scratch), compiler_params=params)
    arrs, scatter = exchange
    single = not isinstance(out_specs, (list, tuple))
    o_specs = [out_specs] if single else list(out_specs)
    o_shape = [out_shape] if single else list(out_shape)
    n_in, n_out, n_sc, n = len(in_specs), len(o_specs), len(scratch), len(arrs)
    any_spec = pl.BlockSpec(memory_space=pl.ANY)

    def body_with_exchange(*refs):
        pos = [0]

        def take(k):
            pos[0] += k
            return refs[pos[0] - k:pos[0]]

        ins, ex_in, outs, ex_out, sc = take(n_in), take(n), take(n_out), take(n), take(n_sc)
        start, middle, wait = _exchange_ops(ex_in, ex_out, refs[pos[0]:], scatter)
        step, total = 0, 1
        for a in range(len(grid)):
            step = step * grid[a] + pl.program_id(a)
            total *= grid[a]
        pl.when(step == 0)(start)
        body(*ins, *outs, *sc)
        if middle is not None:
            pl.when(step == total // 2)(middle)
        pl.when(step == total - 1)(wait)

    call = pl.pallas_call(body_with_exchange, name="comm_" + name, grid=grid, in_specs=list(in_specs) + [any_spec] * n,
                          out_specs=o_specs + [any_spec] * n, out_shape=o_shape + _exchange_shapes(arrs, scatter),
                          scratch_shapes=list(scratch) + _exchange_sems(n), compiler_params=params)

    def run(*operands):
        res = call(*operands, *arrs)
        return (res[0] if single else list(res[:n_out])), list(res[n_out:])

    return run


def _tile(n, cands):
    for c in cands:
        if n % c == 0:
            return c
    return n


def _sds(shape, dtype):
    return jax.ShapeDtypeStruct(tuple(shape), dtype)


def _bdot(a, b, ca, cb):
    return lax.dot_general(a.astype(bf16), b.astype(bf16), (((ca,), (cb,)), ((), ())), preferred_element_type=f32)


_BIG = (1024, 512, 256, 128)


def _col_tile(nb):
    return nb if nb <= 1536 else _tile(nb, _BIG)


_DEEP = (2048, 1024, 512, 256, 128)


def _mm_call(name, grid, in_specs, out_spec, out_shape, tile, nk, ca, cb, exchange, operands):
    def dot(a_ref, w_ref):
        return _bdot(a_ref[...], w_ref[0] if len(w_ref.shape) == 3 else w_ref[...], ca, cb)

    def store(o_ref, val):
        if len(o_ref.shape) == 3:
            o_ref[0] = val.astype(o_ref.dtype)
        else:
            o_ref[...] = val.astype(o_ref.dtype)

    if nk == 1:
        def body(a_ref, w_ref, o_ref):
            store(o_ref, dot(a_ref, w_ref))

        scratch = []
    else:
        def body(a_ref, w_ref, o_ref, acc):
            kk = pl.program_id(2)

            @pl.when(kk == 0)
            def _():
                acc[...] = jnp.zeros_like(acc)

            acc[...] += dot(a_ref, w_ref)

            @pl.when(kk == nk - 1)
            def _():
                store(o_ref, acc[...])

        scratch = [pltpu.VMEM(tile, f32)]
    return _call(body, name, grid, in_specs, out_spec, out_shape, scratch, exchange=exchange)(*operands)


def mm_nn(a, w3, out_dtype, name, exchange=None):
    m, k = a.shape
    nblk, k2, nb = w3.shape
    assert k == k2
    tm, tk, tn = _tile(m, _BIG), _tile(k, _DEEP), _col_tile(nb)
    per, nk = nb // tn, k // tk
    return _mm_call(name, (m // tm, nblk * per, nk),
                    [pl.BlockSpec((tm, tk), lambda i, j, kk: (i, kk)),
                     pl.BlockSpec((1, tk, tn), lambda i, j, kk: (j // per, kk, j % per))],
                    pl.BlockSpec((tm, tn), lambda i, j, kk: (i, j)), _sds((m, nblk * nb), out_dtype),
                    (tm, tn), nk, 1, 0, exchange, (a, w3))


def mm_nt(a, w3, out_dtype, name, exchange=None):
    m, n = a.shape
    nblk, k, nb = w3.shape
    assert n == nblk * nb
    tm, tn = _tile(m, _BIG), _tile(k, _BIG)
    tc = nb if nb <= 1536 else _tile(nb, _DEEP)
    per = nb // tc
    nk = nblk * per
    return _mm_call(name, (m // tm, k // tn, nk),
                    [pl.BlockSpec((tm, tc), lambda i, j, kk: (i, kk)),
                     pl.BlockSpec((1, tn, tc), lambda i, j, kk: (kk // per, j, kk % per))],
                    pl.BlockSpec((tm, tn), lambda i, j, kk: (i, j)), _sds((m, k), out_dtype),
                    (tm, tn), nk, 1, 1, exchange, (a, w3))


def mm_tn(a, b, nblk, name, exchange=None):
    r, m = a.shape
    r2, n = b.shape
    assert r == r2 and n % nblk == 0
    nb = n // nblk
    tm, tk, tn = _tile(m, _BIG), _tile(r, _DEEP), _col_tile(nb)
    per, nk = nb // tn, r // tk
    return _mm_call(name, (m // tm, nblk * per, nk),
                    [pl.BlockSpec((tk, tm), lambda i, j, kk: (kk, i)),
                     pl.BlockSpec((tk, tn), lambda i, j, kk: (kk, j))],
                    pl.BlockSpec((1, tm, tn), lambda i, j, kk: (j // per, i, j % per)), _sds((nblk, m, nb), bf16),
                    (tm, tn), nk, 0, 0, exchange, (a, b))


def mm_nn_fused(a, w3, tiles_in, rows_in, outs, epilogue, name, full_rows=False, with_sum=False):
    m, k = a.shape
    nblk, _, n = w3.shape
    assert nblk == 1
    tm = _tile(m, (512, 256, 128)) if full_rows else _tile(m, _BIG)
    tn = n if full_rows else _col_tile(n)
    tk = _tile(k, _BIG if full_rows else _DEEP)
    nk = k // tk
    n_t, n_r, n_o = len(tiles_in), len(rows_in), len(outs)

    def body(a_ref, w_ref, *refs):
        t_refs, r_refs = refs[:n_t], refs[n_t:n_t + n_r]
        o_refs = refs[n_t + n_r:n_t + n_r + n_o]
        acc = refs[-1]
        i, j, kk = pl.program_id(0), pl.program_id(1), pl.program_id(2)

        @pl.when(kk == 0)
        def _():
            acc[...] = jnp.zeros_like(acc)

        acc[...] += _bdot(a_ref[...], w_ref[0], 1, 0)

        @pl.when(kk == nk - 1)
        def _():
            res = epilogue(acc[...], [r[...] for r in t_refs], [r[...] for r in r_refs])
            for o_ref, val in zip(o_refs, res[0]):
                o_ref[...] = val.astype(o_ref.dtype)
            if with_sum:
                _acc_row(refs[n_t + n_r + n_o], jnp.broadcast_to(res[1], (1, 128)), _first(i, j))

    tile = pl.BlockSpec((tm, tn), lambda i, j, kk: (i, j))
    row = pl.BlockSpec((1, tn), lambda i, j, kk: (0, j))
    out_specs = [tile] * n_o + ([pl.BlockSpec((8, 128), lambda i, j, kk: (0, 0))] if with_sum else [])
    out_shape = [_sds((m, n), dt) for dt in outs] + ([_sds((8, 128), f32)] if with_sum else [])
    return _call(body, name, (m // tm, n // tn, nk),
                 [pl.BlockSpec((tm, tk), lambda i, j, kk: (i, kk)), pl.BlockSpec((1, tk, tn), lambda i, j, kk: (0, kk, j))]
                 + [tile] * n_t + [row] * n_r, out_specs, out_shape, [pltpu.VMEM((tm, tn), f32)])(a, w3, *tiles_in, *rows_in)


def _section_tiles(sections, cands):
    widths = [s.shape[1] for s in sections]
    tile = next(c for c in cands if all(w % c == 0 for w in widths))
    counts = [w // tile for w in widths]
    firsts = [sum(counts[:i]) for i in range(len(counts))]
    return tile, firsts, counts


SECTION_VMEM_BYTES = 24 * 1024 * 1024


def mm_tn_cols(a, sections, nblk, name, exchange=None):
    r, m = a.shape
    n = sum(s.shape[1] for s in sections)
    nb = n // nblk
    tn, firsts, counts = _section_tiles(sections, (nb,) if nb <= 1536 else _BIG)
    per = nb // tn
    tm = _tile(m, _BIG)
    tk = next(c for c in _DEEP if r % c == 0 and len(sections) * c * tn * 4 <= SECTION_VMEM_BYTES)
    nk = r // tk

    def body(a_ref, *refs):
        b_refs, o_ref, acc = refs[:len(sections)], refs[len(sections)], refs[len(sections) + 1]
        j, kk = pl.program_id(1), pl.program_id(2)

        @pl.when(kk == 0)
        def _():
            acc[...] = jnp.zeros_like(acc)

        for b_ref, lo, cnt in zip(b_refs, firsts, counts):
            @pl.when(jnp.logical_and(j >= lo, j < lo + cnt))
            def _(b_ref=b_ref):
                acc[...] += _bdot(a_ref[...], b_ref[...], 0, 0)

        @pl.when(kk == nk - 1)
        def _():
            o_ref[0] = acc[...].astype(bf16)

    def spec(lo, cnt):
        def index(i, j, kk):
            return jnp.where(j < lo, 0, jnp.where(j >= lo + cnt, nk - 1, kk)), jnp.clip(j - lo, 0, cnt - 1)
        return pl.BlockSpec((tk, tn), index)

    return _call(body, name, (m // tm, n // tn, nk),
                 [pl.BlockSpec((tk, tm), lambda i, j, kk: (kk, i))] + [spec(lo, cnt) for lo, cnt in zip(firsts, counts)],
                 pl.BlockSpec((1, tm, tn), lambda i, j, kk: (j // per, i, j % per)), _sds((nblk, m, nb), bf16),
                 [pltpu.VMEM((tm, tn), f32)], exchange=exchange)(a, *sections)


def mm_nt_cols(sections, w3, out_dtype, name, exchange=None):
    m = sections[0].shape[0]
    nblk, k, nb = w3.shape
    tc, firsts, counts = _section_tiles(sections, (nb,) if nb <= 1536 else _DEEP)
    assert nblk * nb == tc * sum(counts) and nb % tc == 0
    per = nb // tc
    tm, tn = _tile(m, _BIG), _tile(k, _BIG)
    nk = nblk * per

    def body(*refs):
        a_refs, w_ref, o_ref, acc = refs[:len(sections)], refs[len(sections)], refs[len(sections) + 1], refs[len(sections) + 2]
        kk = pl.program_id(2)

        @pl.when(kk == 0)
        def _():
            acc[...] = jnp.zeros_like(acc)

        for a_ref, lo, cnt in zip(a_refs, firsts, counts):
            @pl.when(jnp.logical_and(kk >= lo, kk < lo + cnt))
            def _(a_ref=a_ref):
                acc[...] += _bdot(a_ref[...], w_ref[0], 1, 1)

        @pl.when(kk == nk - 1)
        def _():
            o_ref[...] = acc[...].astype(o_ref.dtype)

    def spec(lo, cnt):
        return pl.BlockSpec((tm, tc), lambda i, j, kk: (i, jnp.clip(kk - lo, 0, cnt - 1)))

    return _call(body, name, (m // tm, k // tn, nk),
                 [spec(lo, cnt) for lo, cnt in zip(firsts, counts)] + [pl.BlockSpec((1, tn, tc), lambda i, j, kk: (kk // per, j, kk % per))],
                 pl.BlockSpec((tm, tn), lambda i, j, kk: (i, j)), _sds((m, k), out_dtype),
                 [pltpu.VMEM((tm, tn), f32)], exchange=exchange)(*sections, w3)


def _rms(x, g):
    return x * lax.rsqrt(jnp.mean(x * x, axis=-1, keepdims=True) + EPS) * g


def _rms_pad(x, g, width):
    return x * lax.rsqrt(jnp.sum(x * x, axis=-1, keepdims=True) / width + EPS) * g


def _first(*ids):
    ok = ids[0] == 0
    for i in ids[1:]:
        ok = jnp.logical_and(ok, i == 0)
    return ok


def _acc_row(ref, val, first):
    @pl.when(first)
    def _():
        ref[...] = jnp.zeros_like(ref)

    ref[0:1, :] += val


def rms_fwd(x, g, name):
    r, w = x.shape
    tm = _tile(r, (256, 128, 64, 32, 16, 8))

    def body(x_ref, g_ref, o_ref):
        o_ref[...] = _rms(x_ref[...], g_ref[...]).astype(bf16)

    return _call(body, name, (r // tm,), [pl.BlockSpec((tm, w), lambda i: (i, 0)), pl.BlockSpec((1, w), lambda i: (0, 0))],
                 pl.BlockSpec((tm, w), lambda i: (i, 0)), _sds((r, w), bf16))(x, g)


def resid_rms(xa, xb, g, name):
    r, w = xa.shape
    tm = _tile(r, (256, 128, 64, 32, 16, 8))

    def body(a_ref, b_ref, g_ref, s_ref, u_ref):
        xs = a_ref[...] + b_ref[...]
        s_ref[...] = xs
        u_ref[...] = _rms(xs, g_ref[...]).astype(bf16)

    row = pl.BlockSpec((tm, w), lambda i: (i, 0))
    return _call(body, name, (r // tm,), [row, row, pl.BlockSpec((1, w), lambda i: (0, 0))], [row, row],
                 [_sds((r, w), f32), _sds((r, w), bf16)])(xa, xb, g)


def rms_bwd(x, g, dys, dres, name, want_dx=True, want_b16=False):
    r, w = x.shape
    tm = _tile(r, (256, 128, 64, 32, 16, 8))
    nd = len(dys)

    def body(*refs):
        x_ref, g_ref = refs[0], refs[1]
        dy = refs[2][...]
        for j in range(1, nd):
            dy = dy + refs[2 + j][...]
        pos = 2 + nd
        _, vjp = jax.vjp(_rms, x_ref[...], g_ref[...])
        dx, dg = vjp(dy)
        if dres is not None:
            dx = dx + refs[pos][...]
            pos += 1
        if want_dx:
            refs[pos][...] = dx
            pos += 1
        if want_b16:
            refs[pos][...] = dx.astype(bf16)
            pos += 1
        _acc_row(refs[pos], dg, pl.program_id(0) == 0)

    row = pl.BlockSpec((tm, w), lambda i: (i, 0))
    ins = [x, g] + list(dys) + ([dres] if dres is not None else [])
    in_specs = [row, pl.BlockSpec((1, w), lambda i: (0, 0))] + [row] * (nd + (dres is not None))
    out_specs = [row] * (want_dx + want_b16) + [pl.BlockSpec((8, w), lambda i: (0, 0))]
    out_shape = ([_sds((r, w), f32)] if want_dx else []) + ([_sds((r, w), bf16)] if want_b16 else []) + [_sds((8, w), f32)]
    return _call(body, name, (r // tm,), in_specs, out_specs, out_shape)(*ins)


def lat_norm(z_main, g_qa, g_kva):
    t = z_main.shape[0]
    tm = _tile(t, (512, 256, 128, 64))

    def body(z_ref, gq_ref, gk_ref, q_ref, k_ref):
        q_ref[...] = _rms(z_ref[:, :Q_LORA], gq_ref[...]).astype(bf16)
        k_ref[...] = _rms(z_ref[:, Q_LORA:], gk_ref[...]).astype(bf16)

    return _call(body, "lat_norm", (t // tm,),
                 [pl.BlockSpec((tm, Q_LORA + KV_LORA), lambda i: (i, 0)), pl.BlockSpec((1, Q_LORA), lambda i: (0, 0)),
                  pl.BlockSpec((1, KV_LORA), lambda i: (0, 0))],
                 [pl.BlockSpec((tm, Q_LORA), lambda i: (i, 0)), pl.BlockSpec((tm, KV_LORA), lambda i: (i, 0))],
                 [_sds((t, Q_LORA), bf16), _sds((t, KV_LORA), bf16)])(z_main, g_qa, g_kva)


def lat_norm_bwd(z_main, g_qa, g_kva, dqa, dkv):
    t = z_main.shape[0]
    tm = _tile(t, (512, 256, 128, 64))

    def body(z_ref, gq_ref, gk_ref, dq_ref, dk_ref, dz_ref, dgq_ref, dgk_ref):
        first = pl.program_id(0) == 0
        _, vq = jax.vjp(_rms, z_ref[:, :Q_LORA], gq_ref[...])
        dx, dg = vq(dq_ref[...])
        dz_ref[:, :Q_LORA] = dx.astype(bf16)
        _acc_row(dgq_ref, dg, first)
        _, vk = jax.vjp(_rms, z_ref[:, Q_LORA:], gk_ref[...])
        dx, dg = vk(dk_ref[...])
        dz_ref[:, Q_LORA:] = dx.astype(bf16)
        _acc_row(dgk_ref, dg, first)

    return _call(body, "lat_norm_bwd", (t // tm,),
                 [pl.BlockSpec((tm, Q_LORA + KV_LORA), lambda i: (i, 0)), pl.BlockSpec((1, Q_LORA), lambda i: (0, 0)),
                  pl.BlockSpec((1, KV_LORA), lambda i: (0, 0)), pl.BlockSpec((tm, Q_LORA), lambda i: (i, 0)),
                  pl.BlockSpec((tm, KV_LORA), lambda i: (i, 0))],
                 [pl.BlockSpec((tm, Q_LORA + KV_LORA), lambda i: (i, 0)), pl.BlockSpec((8, Q_LORA), lambda i: (0, 0)),
                  pl.BlockSpec((8, KV_LORA), lambda i: (0, 0))],
                 [_sds((t, Q_LORA + KV_LORA), bf16), _sds((8, Q_LORA), f32), _sds((8, KV_LORA), f32)])(z_main, g_qa, g_kva, dqa, dkv)


def rope_tables(pos_col, inv_freq):
    t = pos_col.shape[0]
    tm = _tile(t, (512, 256, 128, 64))

    def body(p_ref, f_ref, c_ref, s_ref):
        ang = p_ref[...].astype(f32) * f_ref[...]
        lane = lax.broadcasted_iota(jnp.int32, ang.shape, 1)
        c_ref[...] = jnp.where(lane < ROPE, jnp.cos(ang), 0.0)
        sn = jnp.sin(ang)
        s_ref[...] = jnp.where(lane < ROPE // 2, -sn, jnp.where(lane < ROPE, sn, 0.0))

    return _call(body, "rope_tables", (t // tm,),
                 [pl.BlockSpec((tm, 1), lambda i: (i, 0)), pl.BlockSpec((1, TAIL), lambda i: (0, 0))],
                 [pl.BlockSpec((tm, TAIL), lambda i: (i, 0))] * 2, [_sds((t, TAIL), f32)] * 2)(pos_col, inv_freq)


def _swap_halves(n):
    lane = lax.broadcasted_iota(jnp.int32, n.shape, 1)
    return jnp.where(lane < ROPE // 2, pltpu.roll(n, TAIL - ROPE // 2, 1), pltpu.roll(n, ROPE // 2, 1))


def _rope(n, c, s):
    return n * c + _swap_halves(n) * s


def _rope_t(d, c, s):
    return d * c + _swap_halves(d * s)


def _prep_specs(tm):
    head = pl.BlockSpec((tm, HEAD_PAD), lambda i, h: (i, h))
    row = pl.BlockSpec((tm, TAIL), lambda i, h: (i, 0))
    gain = pl.BlockSpec((1, TAIL), lambda i, h: (0, 0))
    return head, row, gain


def _pe_in(zt):
    lane = lax.broadcasted_iota(jnp.int32, zt.shape, 1)
    return jnp.where(lane < ROPE, zt, 0.0)


def mla_prep(q_raw, kv_raw, z_tail, cos, sin, gqn, gqp, gkn, gkp):
    t = q_raw.shape[0]
    tm = _tile(t, (1024, 512, 256, 128, 64))

    def body(q_ref, kv_ref, zt_ref, c_ref, s_ref, gqn_ref, gqp_ref, gkn_ref, gkp_ref, qh_ref, kh_ref, vh_ref):
        c, s = c_ref[...], s_ref[...]
        qh_ref[:, :NOPE] = _rms(q_ref[:, :NOPE], gqn_ref[...]).astype(bf16)
        qh_ref[:, NOPE:] = _rope(_rms_pad(q_ref[:, NOPE:], gqp_ref[...], ROPE), c, s).astype(bf16)
        kh_ref[:, :NOPE] = _rms(kv_ref[:, :NOPE], gkn_ref[...]).astype(bf16)
        kh_ref[:, NOPE:] = _rope(_rms_pad(_pe_in(zt_ref[...]), gkp_ref[...], ROPE), c, s).astype(bf16)
        vh_ref[...] = kv_ref[:, NOPE:].astype(bf16)

    head, row, gain = _prep_specs(tm)
    return _call(body, "mla_prep", (t // tm, MLA_HEADS), [head, head, row, row, row, gain, gain, gain, gain],
                 [head, head, pl.BlockSpec((tm, V_HEAD), lambda i, h: (i, h))],
                 [_sds((t, MLA_HEADS * HEAD_PAD), bf16), _sds((t, MLA_HEADS * HEAD_PAD), bf16), _sds((t, MLA_HEADS * V_HEAD), bf16)],
                 )(q_raw, kv_raw, z_tail, cos, sin, gqn, gqp, gkn, gkp)


def mla_prep_bwd(q_raw, kv_raw, z_tail, cos, sin, gqn, gqp, gkn, gkp, dqh, dkh, dvh):
    t = q_raw.shape[0]
    tm = _tile(t, (1024, 512, 256, 128, 64))
    pad_norm = functools.partial(_rms_pad, width=ROPE)

    def body(q_ref, kv_ref, zt_ref, c_ref, s_ref, gqn_ref, gqp_ref, gkn_ref, gkp_ref, dqh_ref, dkh_ref, dvh_ref,
             dq_ref, dkv_ref, dzt_ref, dgqn_ref, dgqp_ref, dgkn_ref, dgkp_ref):
        i, h = pl.program_id(0), pl.program_id(1)
        first = _first(i, h)
        c, s = c_ref[...], s_ref[...]
        _, v1 = jax.vjp(_rms, q_ref[:, :NOPE], gqn_ref[...])
        dx, dg = v1(dqh_ref[:, :NOPE])
        dq_ref[:, :NOPE] = dx.astype(bf16)
        _acc_row(dgqn_ref, dg, first)
        _, v2 = jax.vjp(pad_norm, q_ref[:, NOPE:], gqp_ref[...])
        dx, dg = v2(_rope_t(dqh_ref[:, NOPE:], c, s))
        dq_ref[:, NOPE:] = dx.astype(bf16)
        _acc_row(dgqp_ref, dg, first)
        _, v3 = jax.vjp(_rms, kv_ref[:, :NOPE], gkn_ref[...])
        dx, dg = v3(dkh_ref[:, :NOPE])
        dkv_ref[:, :NOPE] = dx.astype(bf16)
        _acc_row(dgkn_ref, dg, first)
        dkv_ref[:, NOPE:] = dvh_ref[...].astype(bf16)
        _, v4 = jax.vjp(pad_norm, _pe_in(zt_ref[...]), gkp_ref[...])
        dx, dg = v4(_rope_t(dkh_ref[:, NOPE:], c, s))
        _acc_row(dgkp_ref, dg, first)

        @pl.when(h == 0)
        def _():
            dzt_ref[...] = jnp.zeros_like(dzt_ref)

        dzt_ref[...] += dx

    head, row, gain = _prep_specs(tm)
    acc = pl.BlockSpec((8, TAIL), lambda i, h: (0, 0))
    vspec = pl.BlockSpec((tm, V_HEAD), lambda i, h: (i, h))
    return _call(body, "mla_prep_bwd", (t // tm, MLA_HEADS),
                 [head, head, row, row, row, gain, gain, gain, gain, head, head, vspec],
                 [head, head, row, acc, acc, acc, acc],
                 [_sds((t, MLA_HEADS * HEAD_PAD), bf16), _sds((t, MLA_HEADS * HEAD_PAD), bf16), _sds((t, TAIL), f32)]
                 + [_sds((8, TAIL), f32)] * 4)(q_raw, kv_raw, z_tail, cos, sin, gqn, gqp, gkn, gkp, dqh, dkh, dvh)


ATT_BLOCK = 512
NEG = -1e30
ATT_SCALE = (NOPE + ROPE) ** -0.5
LOG2_E = 1.4426950408889634
ATT_SCALE2 = ATT_SCALE * LOG2_E
ATT_HEADS = 2
ATT_HEADS_FWD = 8


def _chunk_visible(shape, key_axis):
    kc = lax.broadcasted_iota(jnp.int32, shape, key_axis) >> CHUNK_SHIFT
    qc = lax.broadcasted_iota(jnp.int32, shape, 1 - key_axis) >> CHUNK_SHIFT
    return kc <= qc


def mla_fwd(qh, kh, vh, exchange=None):
    t = qh.shape[0]
    tb = min(ATT_BLOCK, t)
    nb = t // tb

    hp = ATT_HEADS_FWD

    def body(q_ref, k_ref, v_ref, o_ref, ob_ref, lse_ref, m_s, l_s, acc):
        qi, ki = pl.program_id(1), pl.program_id(2)

        @pl.when(ki == 0)
        def _():
            m_s[...] = jnp.full_like(m_s, NEG)
            l_s[...] = jnp.zeros_like(l_s)
            acc[...] = jnp.zeros_like(acc)

        def step(diagonal):
            new = []
            for j in range(hp):
                q, k = q_ref[:, j * HEAD_PAD:(j + 1) * HEAD_PAD], k_ref[:, j * HEAD_PAD:(j + 1) * HEAD_PAD]
                s = _bdot(k, q, 1, 1) * ATT_SCALE2
                if diagonal:
                    s = jnp.where(_chunk_visible(s.shape, 0), s, -jnp.inf)
                m_old = m_s[j]
                m_new = jnp.maximum(m_old, jnp.max(s, axis=0, keepdims=True))
                p = jnp.exp2(s - m_new)
                alpha = jnp.exp2(m_old - m_new)
                l_new = alpha * l_s[j] + jnp.sum(p, axis=0, keepdims=True)
                acc_new = alpha * acc[j] + _bdot(v_ref[:, j * V_HEAD:(j + 1) * V_HEAD], p, 0, 0)
                new.append((m_new, l_new, acc_new))
            for j, (m_new, l_new, acc_new) in enumerate(new):
                m_s[j] = m_new
                l_s[j] = l_new
                acc[j] = acc_new
            return new

        @pl.when(ki < qi)
        def _():
            step(False)

        @pl.when(ki == qi)
        def _():
            for j, (m_new, l_new, acc_new) in enumerate(step(True)):
                o = (acc_new / l_new).T
                o_ref[:, j * V_HEAD:(j + 1) * V_HEAD] = o
                ob_ref[:, j * V_HEAD:(j + 1) * V_HEAD] = o.astype(bf16)
                lse_ref[j] = m_new + jnp.log2(l_new)

    kv = lambda g, qi, ki: (jnp.minimum(ki, qi), g)
    o_spec = pl.BlockSpec((tb, hp * V_HEAD), lambda g, qi, ki: (qi, g))
    return _call(body, "mla_fwd", (MLA_HEADS // hp, nb, nb),
                 [pl.BlockSpec((tb, hp * HEAD_PAD), lambda g, qi, ki: (qi, g)), pl.BlockSpec((tb, hp * HEAD_PAD), kv),
                  pl.BlockSpec((tb, hp * V_HEAD), kv)],
                 [o_spec, o_spec, pl.BlockSpec((hp, 1, tb), lambda g, qi, ki: (g, 0, qi))],
                 [_sds((t, MLA_HEADS * V_HEAD), f32), _sds((t, MLA_HEADS * V_HEAD), bf16), _sds((MLA_HEADS, 1, t), f32)],
                 [pltpu.VMEM((hp, 1, tb), f32), pltpu.VMEM((hp, 1, tb), f32), pltpu.VMEM((hp, V_HEAD, tb), f32)],
                 exchange=exchange)(qh, kh, vh)


def mla_delta(o, do):
    t = o.shape[0]
    tm = _tile(t, (512, 256, 128, 64))

    def body(o_ref, do_ref, d_ref):
        for h in range(MLA_HEADS):
            cols = slice(h * V_HEAD, (h + 1) * V_HEAD)
            d_ref[h] = jnp.sum(o_ref[:, cols] * do_ref[:, cols], axis=1, keepdims=True)

    blk = pl.BlockSpec((tm, MLA_HEADS * V_HEAD), lambda i: (i, 0))
    return _call(body, "mla_delta", (t // tm,), [blk, blk], pl.BlockSpec((MLA_HEADS, tm, 1), lambda i: (0, i, 0)),
                 _sds((MLA_HEADS, t, 1), f32))(o, do)


def mla_bwd(qh, kh, vh, do, lse_row, delta_row, exchange=None):
    t = qh.shape[0]
    tb = min(ATT_BLOCK, t)
    nb = t // tb

    hp = ATT_HEADS

    def body(q_ref, k_ref, v_ref, do_ref, lse_ref, dl_ref, dq_ref, dk_ref, dv_ref, dk_acc, dv_acc):
        ki, qi = pl.program_id(1), pl.program_id(2)

        @pl.when(jnp.logical_and(ki == 0, qi == 0))
        def _():
            dq_ref[...] = jnp.zeros_like(dq_ref)

        @pl.when(qi == 0)
        def _():
            dk_acc[...] = jnp.zeros_like(dk_acc)
            dv_acc[...] = jnp.zeros_like(dv_acc)

        def step(diagonal):
            rows = pl.ds(pl.multiple_of(qi * tb, tb), tb)
            new = []
            for j in range(hp):
                qc, vc = slice(j * HEAD_PAD, (j + 1) * HEAD_PAD), slice(j * V_HEAD, (j + 1) * V_HEAD)
                q, k, do_b = q_ref[:, qc], k_ref[:, qc], do_ref[:, vc]
                s = _bdot(k, q, 1, 1) * ATT_SCALE2
                if diagonal:
                    s = jnp.where(_chunk_visible(s.shape, 0), s, -jnp.inf)
                p = jnp.exp2(s - lse_ref[j])
                dp = _bdot(v_ref[:, vc], do_b, 1, 1)
                ds = p * (dp - dl_ref[j]) * ATT_SCALE
                new.append((dv_acc[:, vc] + _bdot(p, do_b, 1, 0), dk_acc[:, qc] + _bdot(ds, q, 1, 0),
                            dq_ref[rows, qc] + _bdot(ds, k, 0, 0)))
            for j, (dv, dk, dq) in enumerate(new):
                dv_acc[:, j * V_HEAD:(j + 1) * V_HEAD] = dv
                dk_acc[:, j * HEAD_PAD:(j + 1) * HEAD_PAD] = dk
                dq_ref[rows, j * HEAD_PAD:(j + 1) * HEAD_PAD] = dq

        @pl.when(qi > ki)
        def _():
            step(False)

        @pl.when(qi == ki)
        def _():
            step(True)

        @pl.when(qi == nb - 1)
        def _():
            dk_ref[...] = dk_acc[...]
            dv_ref[...] = dv_acc[...]

    qs = lambda g, ki, qi: (jnp.maximum(qi, ki), g)
    ks = lambda g, ki, qi: (ki, g)
    vec = pl.BlockSpec((hp, 1, tb), lambda g, ki, qi: (g, 0, jnp.maximum(qi, ki)))
    return _call(body, "mla_bwd", (MLA_HEADS // hp, nb, nb),
                 [pl.BlockSpec((tb, hp * HEAD_PAD), qs), pl.BlockSpec((tb, hp * HEAD_PAD), ks), pl.BlockSpec((tb, hp * V_HEAD), ks),
                  pl.BlockSpec((tb, hp * V_HEAD), qs), vec, vec],
                 [pl.BlockSpec((t, hp * HEAD_PAD), lambda g, ki, qi: (0, g)), pl.BlockSpec((tb, hp * HEAD_PAD), ks),
                  pl.BlockSpec((tb, hp * V_HEAD), ks)],
                 [_sds((t, MLA_HEADS * HEAD_PAD), f32), _sds((t, MLA_HEADS * HEAD_PAD), f32), _sds((t, MLA_HEADS * V_HEAD), f32)],
                 [pltpu.VMEM((tb, hp * HEAD_PAD), f32), pltpu.VMEM((tb, hp * V_HEAD), f32)], exchange=exchange)(
        qh, kh, vh, do, lse_row, delta_row)


PAD = 8


def _conv_taps(pad_ref, w, width, t):
    y = pad_ref[PAD - width + 1:PAD - width + 1 + t, :] * w[0:1, :]
    for j in range(1, width):
        y = y + pad_ref[PAD - width + 1 + j:PAD - width + 1 + j + t, :] * w[j:j + 1, :]
    return y


def _conv_bwd(xpad_ref, dpad_ref, w, da, width, t):
    dpad_ref[0:t, :] = da
    dpad_ref[t:t + PAD, :] = jnp.zeros((PAD, da.shape[1]), f32)
    dx = dpad_ref[width - 1:width - 1 + t, :] * w[0:1, :]
    for j in range(1, width):
        dx = dx + dpad_ref[width - 1 - j:width - 1 - j + t, :] * w[j:j + 1, :]
    dws = [jnp.sum(da * xpad_ref[PAD - width + 1 + j:PAD - width + 1 + j + t, :], axis=0, keepdims=True) for j in range(width)]
    return dx, dws


def _load_pad(pad_ref, x, t):
    pad_ref[0:PAD, :] = jnp.zeros((PAD, x.shape[1]), f32)
    pad_ref[PAD:PAD + t, :] = x


assert ML_DK == 128


def qk_conv(z_main, conv_qk):
    t = z_main.shape[0]
    base = O_Q // ML_DK

    def body(z_ref, w_ref, o_ref, pad):
        _load_pad(pad, z_ref[...], t)
        a = _conv_taps(pad, w_ref[...], ML_CONV, t)
        sc = jnp.where(pl.program_id(0) < ML_HEADS, ML_DK ** -0.5, 1.0)
        o_ref[0] = jax.nn.silu(a) * sc

    return _call(body, "qk_conv", (2 * ML_HEADS,),
                 [pl.BlockSpec((t, ML_DK), lambda j: (0, base + j)), pl.BlockSpec((ML_CONV, ML_DK), lambda j: (0, j))],
                 pl.BlockSpec((1, t, ML_DK), lambda j: (j, 0, 0)), _sds((2 * ML_HEADS, t, ML_DK), f32),
                 [pltpu.VMEM((t + PAD, ML_DK), f32)])(z_main, conv_qk)


def qk_conv_bwd(z_main, conv_qk, dq, dk):
    t = z_main.shape[0]
    base = O_Q // ML_DK

    def body(z_ref, w_ref, dq_ref, dk_ref, dz_ref, dw_ref, pad, dpad):
        _load_pad(pad, z_ref[...], t)
        w = w_ref[...]
        a = _conv_taps(pad, w, ML_CONV, t)
        is_q = pl.program_id(0) < ML_HEADS
        d = jnp.where(is_q, dq_ref[0] * (ML_DK ** -0.5), dk_ref[0])
        _, vjp = jax.vjp(jax.nn.silu, a)
        da, = vjp(d)
        dx, dws = _conv_bwd(pad, dpad, w, da, ML_CONV, t)
        dz_ref[...] = dx.astype(bf16)
        for j in range(ML_CONV):
            dw_ref[j:j + 1, :] = dws[j]

    head = lambda pick: pl.BlockSpec((1, t, ML_DK), lambda j: (pick(j), 0, 0))
    return _call(body, "qk_conv_bwd", (2 * ML_HEADS,),
                 [pl.BlockSpec((t, ML_DK), lambda j: (0, base + j)), pl.BlockSpec((ML_CONV, ML_DK), lambda j: (0, j)),
                  head(lambda j: jnp.minimum(j, ML_HEADS - 1)), head(lambda j: jnp.maximum(j - ML_HEADS, 0))],
                 [pl.BlockSpec((t, ML_DK), lambda j: (0, j)), pl.BlockSpec((ML_CONV, ML_DK), lambda j: (0, j))],
                 [_sds((t, 2 * ML_QK), bf16), _sds((ML_CONV, 2 * ML_QK), f32)],
                 [pltpu.VMEM((t + PAD, ML_DK), f32), pltpu.VMEM((t + PAD, ML_DK), f32)])(z_main, conv_qk, dq, dk)


def glu_fwd(hup, conv_w, bias):
    t, f2 = hup.shape
    nf = f2 // 2 // 128

    def body(h1_ref, h2_ref, w1_ref, w2_ref, b1_ref, b2_ref, o_ref, pad):
        _load_pad(pad, h1_ref[...], t)
        a1 = _conv_taps(pad, w1_ref[...], FFN_CONV, t) + b1_ref[...]
        _load_pad(pad, h2_ref[...], t)
        a2 = _conv_taps(pad, w2_ref[...], FFN_CONV, t) + b2_ref[...]
        o_ref[...] = (jax.nn.silu(a1) * a2).astype(bf16)

    col = lambda off: pl.BlockSpec((t, 128), lambda j: (0, j + off))
    wsp = lambda off: pl.BlockSpec((FFN_CONV, 128), lambda j: (0, j + off))
    bsp = lambda off: pl.BlockSpec((1, 128), lambda j: (0, j + off))
    return _call(body, "glu_fwd", (nf,), [col(0), col(nf), wsp(0), wsp(nf), bsp(0), bsp(nf)], col(0), _sds((t, f2 // 2), bf16),
                 [pltpu.VMEM((t + PAD, 128), f32)])(hup, hup, conv_w, conv_w, bias, bias)


def glu_bwd(hup, conv_w, bias, dg, exchange=None):
    t, f2 = hup.shape
    f = f2 // 2
    nf = f // 128

    def body(h1_ref, h2_ref, w1_ref, w2_ref, b1_ref, b2_ref, dg_ref, dh1_ref, dh2_ref, dw1_ref, dw2_ref, db1_ref, db2_ref,
             pad1, pad2, dpad):
        _load_pad(pad1, h1_ref[...], t)
        _load_pad(pad2, h2_ref[...], t)
        w1, w2 = w1_ref[...], w2_ref[...]
        a1 = _conv_taps(pad1, w1, FFN_CONV, t) + b1_ref[...]
        a2 = _conv_taps(pad2, w2, FFN_CONV, t) + b2_ref[...]
        d = dg_ref[...]
        _, vjp = jax.vjp(jax.nn.silu, a1)
        da1, = vjp(d * a2)
        da2 = d * jax.nn.silu(a1)
        for da, pad, w, dh_ref, dw_ref, db_ref in ((da1, pad1, w1, dh1_ref, dw1_ref, db1_ref), (da2, pad2, w2, dh2_ref, dw2_ref, db2_ref)):
            dx, dws = _conv_bwd(pad, dpad, w, da, FFN_CONV, t)
            dh_ref[...] = dx.astype(bf16)
            for j in range(FFN_CONV):
                dw_ref[j:j + 1, :] = dws[j]
            db_ref[...] = jnp.sum(da, axis=0, keepdims=True)

    col = lambda off: pl.BlockSpec((t, 128), lambda j: (0, j + off))
    wsp = lambda off: pl.BlockSpec((FFN_CONV, 128), lambda j: (0, j + off))
    bsp = lambda off: pl.BlockSpec((1, 128), lambda j: (0, j + off))
    return _call(body, "glu_bwd", (nf,), [col(0), col(nf), wsp(0), wsp(nf), bsp(0), bsp(nf), col(0)],
                 [col(0), col(0), wsp(0), wsp(0), bsp(0), bsp(0)],
                 [_sds((t, f), bf16)] * 2 + [_sds((FFN_CONV, f), f32)] * 2 + [_sds((1, f), f32)] * 2,
                 [pltpu.VMEM((t + PAD, 128), f32)] * 3, exchange=exchange)(hup, hup, conv_w, conv_w, bias, bias, dg)


def gate_act(z_tail, b_tile):
    t = z_tail.shape[0]
    tm = _tile(t, (512, 256, 128, 64))

    def body(z_ref, b_ref, o_ref):
        x = z_ref[...] + b_ref[...]
        lane = lax.broadcasted_iota(jnp.int32, x.shape, 1)
        o_ref[...] = jnp.where(lane < T_F, x, jax.nn.log_sigmoid(x))

    row = pl.BlockSpec((tm, TAIL), lambda i: (i, 0))
    return _call(body, "gate_act", (t // tm,), [row, pl.BlockSpec((1, TAIL), lambda i: (0, 0))], row, _sds((t, TAIL), f32))(z_tail, b_tile)


def tail_bwd(z_tail, b_tile, dzt_pe, dgate):
    t = z_tail.shape[0]
    tm = _tile(t, (512, 256, 128, 64))

    def body(z_ref, b_ref, dpe_ref, dg_ref, dz_ref, db_ref):
        x = z_ref[...] + b_ref[...]
        lane = lax.broadcasted_iota(jnp.int32, x.shape, 1)
        _, vjp = jax.vjp(jax.nn.log_sigmoid, x)
        df, = vjp(dg_ref[...])
        dgates = jnp.where(lane < T_F, dg_ref[...], df)
        dgates = jnp.where(jnp.logical_and(lane >= T_I, lane < T_F + ML_HEADS), dgates, 0.0)
        dz_ref[...] = jnp.where(lane < ROPE, dpe_ref[...], dgates).astype(bf16)
        _acc_row(db_ref, jnp.sum(dgates, axis=0, keepdims=True), pl.program_id(0) == 0)

    row = pl.BlockSpec((tm, TAIL), lambda i: (i, 0))
    return _call(body, "tail_bwd", (t // tm,), [row, pl.BlockSpec((1, TAIL), lambda i: (0, 0)), row, row],
                 [row, pl.BlockSpec((8, TAIL), lambda i: (0, 0))], [_sds((t, TAIL), bf16), _sds((8, TAIL), f32)])(z_tail, b_tile, dzt_pe, dgate)


def _hdot(a, b, ca, cb):
    return lax.dot_general(a.astype(bf16), b.astype(bf16), (((ca,), (cb,)), ((0,), (0,))), preferred_element_type=f32)


def _mlstm_step(q, k, v, igr, fgr, c_mat, n_vec, m):
    nh, ln = q.shape[0], CHUNK
    sq = (nh, ln, ln)
    row = lax.broadcasted_iota(jnp.int32, sq, 1)
    col = lax.broadcasted_iota(jnp.int32, sq, 2)
    eye = row == col

    def to_col(r):
        return jnp.sum(jnp.where(eye, jnp.broadcast_to(r, sq), 0.0), axis=2, keepdims=True)

    bc_r = jnp.sum(jnp.where(row <= col, jnp.broadcast_to(to_col(fgr), sq), 0.0), axis=1, keepdims=True)
    bc_c = to_col(bc_r)
    logw = jnp.where(col <= row, bc_c - bc_r + igr, -jnp.inf)
    inter = bc_c + m
    m_t = jnp.maximum(inter, jnp.max(logw, axis=2, keepdims=True))
    w_intra = jnp.exp(logw - m_t)
    w_inter = jnp.exp(inter - m_t)
    sc = _hdot(q, k, 2, 2) * w_intra
    num = w_inter * _hdot(q, c_mat, 2, 1) + _hdot(sc, v, 2, 1)
    qn = jnp.sum(q.astype(bf16).astype(f32) * n_vec.astype(bf16).astype(f32), axis=2, keepdims=True)
    den = w_inter * qn + jnp.sum(sc, axis=2, keepdims=True)
    h = num / jnp.maximum(jnp.abs(den), jnp.exp(-m_t))
    lane = lax.broadcasted_iota(jnp.int32, (nh, 1, ln), 2)
    b_last = jnp.sum(jnp.where(lane == ln - 1, bc_r, 0.0), axis=2, keepdims=True)
    logu = b_last - bc_r + igr
    m_new = jnp.maximum(b_last + m, jnp.max(logu, axis=2, keepdims=True))
    decay = jnp.exp(b_last + m - m_new)
    u_c = to_col(jnp.exp(logu - m_new))
    c_new = decay * c_mat + _hdot(u_c * k, v, 1, 1)
    n_new = decay * n_vec + jnp.sum(u_c.astype(bf16).astype(f32) * k.astype(bf16).astype(f32), axis=1, keepdims=True)
    return h, c_new, n_new, m_new


ML_VHALF = ML_V // 2
assert O_V % ML_VHALF == 0 and ML_HEADS % 2 == 0


def _ml_specs(nc, rev):
    cc = (lambda c: nc - 1 - c) if rev else (lambda c: c)
    q = pl.BlockSpec((ML_HEADS, CHUNK, ML_DK), lambda c: (0, cc(c), 0))
    k = pl.BlockSpec((ML_HEADS, CHUNK, ML_DK), lambda c: (1, cc(c), 0))
    v_lo = pl.BlockSpec((CHUNK, ML_VHALF), lambda c: (cc(c), O_V // ML_VHALF))
    v_hi = pl.BlockSpec((CHUNK, ML_VHALF), lambda c: (cc(c), O_V // ML_VHALF + 1))
    hv = pl.BlockSpec((ML_HEADS, CHUNK, ML_DV), lambda c: (0, cc(c), 0))
    gate = pl.BlockSpec((ML_HEADS, 1, 1, CHUNK), lambda c: (0, cc(c), 0, 0))
    cm = pl.BlockSpec((ML_HEADS, 1, ML_DK, ML_DV), lambda c: (0, cc(c), 0, 0))
    nv = pl.BlockSpec((ML_HEADS, 1, 1, ML_DK), lambda c: (0, cc(c), 0, 0))
    ms = pl.BlockSpec((ML_HEADS, 1, 1, 1), lambda c: (0, cc(c), 0, 0))
    return q, k, v_lo, v_hi, hv, gate, cm, nv, ms


_ML_STATE = [pltpu.VMEM((ML_HEADS, ML_DK, ML_DV), f32), pltpu.VMEM((ML_HEADS, 1, ML_DK), f32), pltpu.VMEM((ML_HEADS, 1, 1), f32)]


def _ml_zero_state(c_s, n_s, m_s):
    @pl.when(pl.program_id(0) == 0)
    def _():
        c_s[...] = jnp.zeros_like(c_s)
        n_s[...] = jnp.zeros_like(n_s)
        m_s[...] = jnp.zeros_like(m_s)


def _ml_heads_of(v_lo_ref, v_hi_ref):
    half = ML_HEADS // 2
    return jnp.stack([r[:, j * ML_DV:(j + 1) * ML_DV] for r in (v_lo_ref, v_hi_ref) for j in range(half)])


def mlstm_fwd(qk_act, z_main, ig, fg):
    t = qk_act.shape[1]
    nc = t // CHUNK

    def body(q_ref, k_ref, vl_ref, vh_ref, ig_ref, fg_ref, h_ref, c_out, n_out, m_out, c_s, n_s, m_s):
        _ml_zero_state(c_s, n_s, m_s)
        c0, n0, m0 = c_s[...], n_s[...], m_s[...]
        c_out[:, 0] = c0
        n_out[:, 0] = n0
        m_out[:, 0] = m0
        h, c2, n2, m2 = _mlstm_step(q_ref[...], k_ref[...], _ml_heads_of(vl_ref, vh_ref), ig_ref[:, 0], fg_ref[:, 0], c0, n0, m0)
        h_ref[...] = h
        c_s[...] = c2
        n_s[...] = n2
        m_s[...] = m2

    q, k, v_lo, v_hi, hv, gate, cm, nv, ms = _ml_specs(nc, False)
    return _call(body, "mlstm_fwd", (nc,), [q, k, v_lo, v_hi, gate, gate], [hv, cm, nv, ms],
                 [_sds((ML_HEADS, t, ML_DV), f32), _sds((ML_HEADS, nc, ML_DK, ML_DV), f32), _sds((ML_HEADS, nc, 1, ML_DK), f32),
                  _sds((ML_HEADS, nc, 1, 1), f32)], _ML_STATE)(qk_act, qk_act, z_main, z_main, ig, fg)


def mlstm_bwd(qk_act, z_main, ig, fg, c_all, n_all, m_all, dh, exchange=None):
    t = qk_act.shape[1]
    nc = t // CHUNK

    def body(q_ref, k_ref, vl_ref, vh_ref, ig_ref, fg_ref, c_ref, n_ref, m_ref, dh_ref, dq_ref, dk_ref, dv_ref, dig_ref, dfg_ref,
             dc_s, dn_s, dm_s):
        _ml_zero_state(dc_s, dn_s, dm_s)
        _, vjp = jax.vjp(_mlstm_step, q_ref[...], k_ref[...], _ml_heads_of(vl_ref, vh_ref), ig_ref[:, 0], fg_ref[:, 0],
                         c_ref[:, 0], n_ref[:, 0], m_ref[:, 0])
        dq, dk, dv, dig, dfg, dc, dn, dm = vjp((dh_ref[...], dc_s[...], dn_s[...], dm_s[...]))
        dq_ref[...] = dq
        dk_ref[...] = dk
        for j in range(ML_HEADS):
            dv_ref[:, j * ML_DV:(j + 1) * ML_DV] = dv[j].astype(bf16)
        dig_ref[:, 0] = dig
        dfg_ref[:, 0] = dfg
        dc_s[...] = dc
        dn_s[...] = dn
        dm_s[...] = dm

    q, k, v_lo, v_hi, hv, gate, cm, nv, ms = _ml_specs(nc, True)
    gshape = _sds((ML_HEADS, nc, 1, CHUNK), f32)
    return _call(body, "mlstm_bwd", (nc,), [q, k, v_lo, v_hi, gate, gate, cm, nv, ms, hv],
                 [q, q, pl.BlockSpec((CHUNK, ML_V), lambda c: (nc - 1 - c, 0)), gate, gate],
                 [_sds((ML_HEADS, t, ML_DK), f32), _sds((ML_HEADS, t, ML_DK), f32), _sds((t, ML_V), bf16), gshape, gshape],
                 _ML_STATE, exchange=exchange)(qk_act, qk_act, z_main, z_main, ig, fg, c_all, n_all, m_all, dh)


def _ml_out(h, zo, g):
    return _rms(h, g) * jax.nn.sigmoid(zo)


def mlstm_out(h, z_main, g_hnorm):
    t = h.shape[1]
    tm = _tile(t, (512, 256, 128, 64))
    zo = O_O // ML_DV

    def body(h_ref, z_ref, g_ref, y_ref):
        y_ref[...] = _ml_out(h_ref[0], z_ref[...], g_ref[0]).astype(bf16)

    return _call(body, "mlstm_out", (t // tm, ML_HEADS),
                 [pl.BlockSpec((1, tm, ML_DV), lambda i, hd: (hd, i, 0)), pl.BlockSpec((tm, ML_DV), lambda i, hd: (i, zo + hd)),
                  pl.BlockSpec((1, 1, ML_DV), lambda i, hd: (hd, 0, 0))],
                 pl.BlockSpec((tm, ML_DV), lambda i, hd: (i, hd)), _sds((t, ML_V), bf16))(h, z_main, g_hnorm)


def mlstm_out_bwd(h, z_main, g_hnorm, dy):
    t = h.shape[1]
    tm = _tile(t, (512, 256, 128, 64))
    zo = O_O // ML_DV

    def body(h_ref, z_ref, g_ref, dy_ref, dh_ref, dzo_ref, dg_ref):
        _, vjp = jax.vjp(_ml_out, h_ref[0], z_ref[...], g_ref[0])
        dh, dz, dg = vjp(dy_ref[...])
        dh_ref[0] = dh
        dzo_ref[...] = dz.astype(bf16)

        @pl.when(pl.program_id(1) == 0)
        def _():
            dg_ref[...] = jnp.zeros_like(dg_ref)

        dg_ref[0, 0:1, :] += dg

    head = pl.BlockSpec((1, tm, ML_DV), lambda hd, i: (hd, i, 0))
    blk = pl.BlockSpec((tm, ML_DV), lambda hd, i: (i, hd))
    return _call(body, "mlstm_out_bwd", (ML_HEADS, t // tm),
                 [head, pl.BlockSpec((tm, ML_DV), lambda hd, i: (i, zo + hd)), pl.BlockSpec((1, 1, ML_DV), lambda hd, i: (hd, 0, 0)), blk],
                 [head, blk, pl.BlockSpec((1, 8, ML_DV), lambda hd, i: (hd, 0, 0))],
                 [_sds((ML_HEADS, t, ML_DV), f32), _sds((t, ML_V), bf16), _sds((ML_HEADS, 8, ML_DV), f32)])(h, z_main, g_hnorm, dy)


def _merge(ga, gb, ya, yb):
    return jax.nn.sigmoid(ga) * ya + jax.nn.sigmoid(gb) * yb


def _merge_specs(t, d):
    tm = _tile(t, (512, 256, 128, 64))
    bw = _tile(d, (512, 256, 128))
    assert O_GA % bw == 0 and (O_GA + d) % bw == 0
    blk = pl.BlockSpec((tm, bw), lambda i, j: (i, j))
    ga = pl.BlockSpec((tm, bw), lambda i, j: (i, O_GA // bw + j))
    gb = pl.BlockSpec((tm, bw), lambda i, j: (i, (O_GA + d) // bw + j))
    return tm, bw, blk, ga, gb


def merge_fwd(z_main, ya, yb):
    t, d = ya.shape
    tm, bw, blk, ga, gb = _merge_specs(t, d)

    def body(ga_ref, gb_ref, ya_ref, yb_ref, o_ref):
        o_ref[...] = _merge(ga_ref[...], gb_ref[...], ya_ref[...], yb_ref[...]).astype(bf16)

    return _call(body, "merge_fwd", (t // tm, d // bw), [ga, gb, blk, blk], blk, _sds((t, d), bf16))(z_main, z_main, ya, yb)


def merge_bwd(z_main, ya, yb, dmerged):
    t, d = ya.shape
    tm, bw, blk, ga, gb = _merge_specs(t, d)

    def body(ga_ref, gb_ref, ya_ref, yb_ref, dm_ref, dga_ref, dgb_ref, dya_ref, dyb_ref):
        _, vjp = jax.vjp(_merge, ga_ref[...], gb_ref[...], ya_ref[...], yb_ref[...])
        dga, dgb, dya, dyb = vjp(dm_ref[...])
        dga_ref[...] = dga.astype(bf16)
        dgb_ref[...] = dgb.astype(bf16)
        dya_ref[...] = dya.astype(bf16)
        dyb_ref[...] = dyb.astype(bf16)

    return _call(body, "merge_bwd", (t // tm, d // bw), [ga, gb, blk, blk, blk], [blk] * 4, [_sds((t, d), bf16)] * 4)(
        z_main, z_main, ya, yb, dmerged)


def _cross(cq, ck, cv, gq, gk):
    outs = []
    for hd in range(CR_HEADS):
        sl = slice(hd * CR_HD, (hd + 1) * CR_HD)
        q = _rms(cq[:, sl], gq)
        k = _rms(ck[:, sl], gk)
        s = _bdot(q, k, 1, 1) * (CR_HD ** -0.5)
        p = jax.nn.softmax(s, axis=-1)
        outs.append(_bdot(p, cv[:, sl], 1, 0))
    return jnp.concatenate(outs, axis=1)


def cross_fwd(cq, ck, cv, gq, gk):
    t, w = cq.shape
    nm = ck.shape[0]
    tm = _tile(t, (512, 256, 128, 64))

    def body(q_ref, k_ref, v_ref, gq_ref, gk_ref, o_ref):
        o_ref[...] = _cross(q_ref[...], k_ref[...], v_ref[...], gq_ref[...], gk_ref[...]).astype(bf16)

    row = pl.BlockSpec((tm, w), lambda i: (i, 0))
    full = pl.BlockSpec((nm, w), lambda i: (0, 0))
    gain = pl.BlockSpec((1, CR_HD), lambda i: (0, 0))
    return _call(body, "cross_fwd", (t // tm,), [row, full, full, gain, gain], row, _sds((t, w), bf16))(cq, ck, cv, gq, gk)


def cross_bwd(cq, ck, cv, gq, gk, do):
    t, w = cq.shape
    nm = ck.shape[0]
    tm = _tile(t, (512, 256, 128, 64))

    def body(q_ref, k_ref, v_ref, gq_ref, gk_ref, do_ref, dq_ref, dk_ref, dv_ref, dgq_ref, dgk_ref):
        first = pl.program_id(0) == 0
        _, vjp = jax.vjp(_cross, q_ref[...], k_ref[...], v_ref[...], gq_ref[...], gk_ref[...])
        dq, dk, dv, dgq, dgk = vjp(do_ref[...])
        dq_ref[...] = dq.astype(bf16)

        @pl.when(first)
        def _():
            dk_ref[...] = jnp.zeros_like(dk_ref)
            dv_ref[...] = jnp.zeros_like(dv_ref)

        dk_ref[...] += dk
        dv_ref[...] += dv
        _acc_row(dgq_ref, dgq, first)
        _acc_row(dgk_ref, dgk, first)

    row = pl.BlockSpec((tm, w), lambda i: (i, 0))
    full = pl.BlockSpec((nm, w), lambda i: (0, 0))
    gain = pl.BlockSpec((1, CR_HD), lambda i: (0, 0))
    acc = pl.BlockSpec((8, CR_HD), lambda i: (0, 0))
    return _call(body, "cross_bwd", (t // tm,), [row, full, full, gain, gain, row], [row, full, full, acc, acc],
                 [_sds((t, w), bf16), _sds((nm, w), f32), _sds((nm, w), f32), _sds((8, CR_HD), f32), _sds((8, CR_HD), f32)])(
        cq, ck, cv, gq, gk, do)


def _place():
    x, y, c = lax.axis_index("x"), lax.axis_index("y"), lax.axis_index("c")
    peers = {}
    for r in range(1, N_DEV):
        px = 1 - x if r & 4 else x
        py = 1 - y if r & 2 else y
        pc = 1 - c if r & 1 else c
        peers[r] = ((px, py, pc), 4 * px + 2 * py + pc)
    return 4 * x + 2 * y + c, peers


N_REL = N_DEV - 1
RELATIONS = tuple(range(1, N_DEV))
SIBLING = 1
OTHER_CHIPS = (2, 4, 6)
PASSED_ON = (3, 5, 7)


def _exchange_ops(ins, outs, sems, scatter):
    n = len(ins)
    send_sems, recv_sems, local_sems = sems

    def tools():
        me, peers = _place()

        def copy(a, r, src, dst_idx, to):
            return pltpu.make_async_remote_copy(
                src_ref=src, dst_ref=outs[a].at[dst_idx], send_sem=send_sems.at[a * N_REL + r - 1],
                recv_sem=recv_sems.at[a * N_REL + r - 1], device_id=peers[to][0], device_id_type=MESH)

        def local(a):
            return pltpu.make_async_copy(ins[a].at[me] if scatter else ins[a], outs[a].at[me], local_sems.at[a])

        def arrival(a, r):
            return copy(a, r, ins[a].at[me] if scatter else ins[a], peers[r][1], r)

        return me, peers, copy, local, arrival

    if scatter:
        def sends():
            me, peers, copy, local, _ = tools()
            return [local(a) for a in range(n)], [copy(a, r, ins[a].at[peers[r][1]], me, r) for a in range(n) for r in RELATIONS]

        def start():
            loc, out = sends()
            for cp in loc + out:
                cp.start()

        middle = None
        waited_last = RELATIONS
    else:
        def sends():
            me, peers, copy, local, _ = tools()
            own = [copy(a, r, ins[a], me, r) for a in range(n) for r in (SIBLING,) + OTHER_CHIPS]
            return [local(a) for a in range(n)], own

        def passes():
            me, peers, copy, _, _ = tools()
            return [copy(a, r, outs[a].at[peers[r - 1][1]], peers[r - 1][1], SIBLING) for a in range(n) for r in PASSED_ON]

        def start():
            loc, out = sends()
            for cp in loc + out:
                cp.start()

        def middle():
            _, _, _, _, arrival = tools()
            fwd = passes()
            for a in range(n):
                for i, r in enumerate(PASSED_ON):
                    arrival(a, r - 1).wait_recv()
                    fwd[a * len(PASSED_ON) + i].start()

        waited_last = (SIBLING,) + PASSED_ON

    def wait():
        _, _, _, _, arrival = tools()
        for a in range(n):
            for r in waited_last:
                arrival(a, r).wait_recv()
        loc, out = sends()
        for cp in out + ([] if scatter else passes()):
            cp.wait_send()
        for cp in loc:
            cp.wait()

    return start, middle, wait


def _exchange_shapes(arrs, scatter):
    return [_sds(a.shape if scatter else (N_DEV,) + a.shape, a.dtype) for a in arrs]


def _exchange_sems(n):
    return [pltpu.SemaphoreType.DMA((n * N_REL,)), pltpu.SemaphoreType.DMA((n * N_REL,)), pltpu.SemaphoreType.DMA((n,))]


def _exchange(arrs, name, scatter):
    n = len(arrs)

    def body(*refs):
        start, middle, wait = _exchange_ops(refs[:n], refs[n:2 * n], refs[2 * n:], scatter)
        start()
        if middle is not None:
            middle()
        wait()

    any_spec = pl.BlockSpec(memory_space=pl.ANY)
    return pl.pallas_call(body, name=name, in_specs=[any_spec] * n, out_specs=[any_spec] * n,
                          out_shape=_exchange_shapes(arrs, scatter), scratch_shapes=_exchange_sems(n))(*arrs)


def cast_bf16(w, name):
    _, r, c = w.shape
    tr = _tile(r, (256, 128, 64, 32, 16))

    def body(w_ref, o_ref):
        o_ref[...] = w_ref[0].astype(bf16)

    return _call(body, name, (r // tr,), [pl.BlockSpec((1, tr, c), lambda i: (0, i, 0))], pl.BlockSpec((tr, c), lambda i: (i, 0)),
                 _sds((r, c), bf16))(w)


def _adamw(w, g, m, v):
    m = ADAM_B1 * m + (1.0 - ADAM_B1) * g
    v = ADAM_B2 * v + (1.0 - ADAM_B2) * jnp.square(g)
    m_hat = m / (1.0 - ADAM_B1 ** ADAM_STEP)
    v_hat = v / (1.0 - ADAM_B2 ** ADAM_STEP)
    delta = -ADAM_LR * (m_hat / (jnp.sqrt(v_hat) + ADAM_EPS) + ADAM_WD * w)
    return delta, m, v


def adam_sum(parts, w, m, v, name):
    _, r, c = parts.shape
    budget = 4 * 1024 * 1024
    tr = r
    for cand in (1024, 512, 256, 128, 64, 32, 16):
        if r % cand == 0 and N_DEV * cand * c * 4 <= budget:
            tr = cand
            break

    def body(p_ref, w_ref, m_ref, v_ref, g_ref, d_ref, m2_ref, v2_ref):
        g = p_ref[0].astype(f32)
        for k in range(1, N_DEV):
            g = g + p_ref[k].astype(f32)
        d, m2, v2 = _adamw(w_ref[0], g, m_ref[0], v_ref[0])
        g_ref[...] = g
        d_ref[...] = d
        m2_ref[...] = m2
        v2_ref[...] = v2

    blk = pl.BlockSpec((1, tr, c), lambda i: (0, i, 0))
    out = pl.BlockSpec((tr, c), lambda i: (i, 0))
    return _call(body, name, (r // tr,), [pl.BlockSpec((N_DEV, tr, c), lambda i: (0, i, 0)), blk, blk, blk], [out] * 4,
                 [_sds((r, c), f32)] * 4)(parts, w, m, v)


def sum_parts(parts, name):
    _, r, c = parts.shape

    def body(p_ref, o_ref):
        g = p_ref[0]
        for k in range(1, N_DEV):
            g = g + p_ref[k]
        o_ref[...] = g

    return pl.pallas_call(body, name=name, out_shape=_sds((r, c), f32))(parts)


def adam_flat(w, g, m, v, name):
    def body(w_ref, g_ref, m_ref, v_ref, d_ref, m2_ref, v2_ref):
        d, m2, v2 = _adamw(w_ref[...], g_ref[...], m_ref[...], v_ref[...])
        d_ref[...] = d
        m2_ref[...] = m2
        v2_ref[...] = v2

    return pl.pallas_call(body, name=name, out_shape=[_sds(w.shape, f32)] * 3)(w, g, m, v)


def _pack(vecs, multiple):
    flat = jnp.concatenate([v.reshape(-1) for v in vecs])
    n = flat.shape[0]
    total = -(-n // multiple) * multiple
    return jnp.pad(flat, (0, total - n))


def _unpack(flat, shapes):
    out, pos = [], 0
    for s in shapes:
        n = 1
        for d in s:
            n *= d
        out.append(flat[pos:pos + n].reshape(s))
        pos += n
    return out


def _pad_lanes(v, width=TAIL):
    return jnp.pad(v, ((0, 0), (0, width - v.shape[1])))


def kernel(x, mem, positions, g_mix, w_in, g_qa, w_qb, g_kva, w_kvb, g_qn_nope, g_qn_pe, g_kn_nope, g_kn_pe, conv_qk, b_if, g_hnorm, p_a, p_b, w_out, g_cross, g_mem, wq_c, wk_c, wv_c, g_cq, g_ck, wo_c, g_ffn, w_up, conv_ffn, b_conv_ffn, w_down, loss_target, m_g_mix, m_w_in, m_g_qa, m_w_qb, m_g_kva, m_w_kvb, m_g_qn_nope, m_g_qn_pe, m_g_kn_nope, m_g_kn_pe, m_conv_qk, m_b_if, m_g_hnorm, m_p_a, m_p_b, m_w_out, m_g_cross, m_g_mem, m_wq_c, m_wk_c, m_wv_c, m_g_cq, m_g_ck, m_wo_c, m_g_ffn, m_w_up, m_conv_ffn, m_b_conv_ffn, m_w_down, v_g_mix, v_w_in, v_g_qa, v_w_qb, v_g_kva, v_w_kvb, v_g_qn_nope, v_g_qn_pe, v_g_kn_nope, v_g_kn_pe, v_conv_qk, v_b_if, v_g_hnorm, v_p_a, v_p_b, v_w_out, v_g_cross, v_g_mem, v_wq_c, v_wk_c, v_wv_c, v_g_cq, v_g_ck, v_wo_c, v_g_ffn, v_w_up, v_conv_ffn, v_b_conv_ffn, v_w_down):
    args = dict(locals())
    names = ['g_mix', 'w_in', 'g_qa', 'w_qb', 'g_kva', 'w_kvb', 'g_qn_nope', 'g_qn_pe', 'g_kn_nope', 'g_kn_pe', 'conv_qk', 'b_if',
             'g_hnorm', 'p_a', 'p_b', 'w_out', 'g_cross', 'g_mem', 'wq_c', 'wk_c', 'wv_c', 'g_cq', 'g_ck', 'wo_c', 'g_ffn', 'w_up',
             'conv_ffn', 'b_conv_ffn', 'w_down']
    big = ['w_in', 'w_qb', 'w_kvb', 'p_a', 'p_b', 'w_out', 'wq_c', 'wk_c', 'wv_c', 'wo_c', 'w_up', 'w_down']
    sharded_small = ['conv_qk', 'g_hnorm', 'conv_ffn']
    replicated = [n for n in names if n not in big and n not in sharded_small]

    t, d = x.shape[1], x.shape[2]
    x2d, tgt = x[0], loss_target[0]
    mem2d = mem[0]
    me = 4 * lax.axis_index("x") + 2 * lax.axis_index("y") + lax.axis_index("c")
    nc = t // CHUNK
    f2 = b_conv_ffn.shape[1]
    wmain = O_GA + 2 * d

    first = ['w_in', 'w_qb', 'w_kvb']
    behind_in = ['p_a', 'p_b', 'w_out', 'wq_c', 'wk_c', 'wv_c', 'wo_c']
    shards = {n: cast_bf16(args[n], "cast_" + n) for n in big}
    small_local = _pack([args[n] for n in sharded_small], 128).reshape(1, -1)
    gathered = _exchange([shards[n] for n in first] + [small_local], "comm_gather_first", scatter=False)
    gw = dict(zip(first, gathered[:-1]))
    small_all = gathered[-1]
    full_small, pos = [], 0
    for n in sharded_small:
        _, rows, cols = args[n].shape
        piece = small_all[:, 0, pos:pos + rows * cols].reshape(N_DEV, rows, cols)
        full_small.append(piece.transpose(1, 0, 2).reshape(rows, N_DEV * cols))
        pos += rows * cols
    conv_qk_f, g_hnorm_f, conv_ffn_f = full_small

    shard_w = w_in.shape[2]
    c_kpe, c_q, c_i, c_o = O_Q, O_Q + ROPE, O_Q + ROPE + 2 * ML_QK + ML_V, O_Q + ROPE + 2 * ML_QK + ML_V + 2 * ML_HEADS
    segments = [(0, c_kpe, 'main', 0), (c_kpe, c_q, 'tail', 0), (c_q, c_i, 'main', O_Q), (c_i, c_o, 'tail', T_I),
                (c_o, N_DEV * shard_w, 'main', O_O)]

    def shard_cuts(lo, hi):
        return [(j, max(lo, j * shard_w) - j * shard_w, min(hi, (j + 1) * shard_w) - j * shard_w)
                for j in range(lo // shard_w, (hi - 1) // shard_w + 1)]

    def gathered_cols(target):
        return [gw['w_in'][j][:, a:b] for lo, hi, tg, _ in segments if tg == target for j, a, b in shard_cuts(lo, hi)]

    w_main = jnp.concatenate(gathered_cols('main'), axis=1)[None]
    w_tail = jnp.concatenate(gathered_cols('tail') + [jnp.zeros((d, TAIL - ROPE - 2 * ML_HEADS), bf16)], axis=1)[None]
    assert w_main.shape[2] == wmain

    inv_freq = ROPE_BASE ** (-jnp.arange(0, ROPE, 2, dtype=f32) / ROPE)
    inv_tile = _pad_lanes(jnp.concatenate([inv_freq, inv_freq])[None])
    cos, sin = rope_tables(positions.reshape(t, 1), inv_tile)
    gqp, gkp = _pad_lanes(g_qn_pe), _pad_lanes(g_kn_pe)
    b_tile = jnp.pad(b_if, ((0, 0), (T_I, TAIL - T_I - 2 * ML_HEADS)))

    u0 = rms_fwd(x2d, g_mix, "rms_mix")
    z_main, got = mm_nn(u0, w_main, f32, "mm_in_main", exchange=([shards[n] for n in behind_in], False))
    gw.update(zip(behind_in, got))
    qb = gw['w_qb'].transpose(1, 0, 2).reshape(Q_LORA, MLA_HEADS, NOPE + ROPE)
    w_qb_p = jnp.concatenate([qb, jnp.zeros((Q_LORA, MLA_HEADS, HEAD_PAD - NOPE - ROPE), bf16)], axis=2).reshape(1, Q_LORA, -1)
    w_kvb3 = gw['w_kvb']
    p_a3, p_b3, w_out3 = (gw[n].reshape(1, -1, d) for n in ('p_a', 'p_b', 'w_out'))
    wq_c3, wk_c3, wv_c3 = (gw[n].reshape(1, d, -1) for n in ('wq_c', 'wk_c', 'wv_c'))
    wo_c3 = gw['wo_c']
    z_tail = mm_nn(u0, w_tail, f32, "mm_in_tail")
    qa_n, kv_n = lat_norm(z_main, g_qa, g_kva)
    q_raw = mm_nn(qa_n, w_qb_p, f32, "mm_qb")
    kv_raw = mm_nn(kv_n, w_kvb3, f32, "mm_kvb")
    qh, kh, vh = mla_prep(q_raw, kv_raw, z_tail, cos, sin, g_qn_nope, gqp, g_kn_nope, gkp)
    (o_a, o_ab, lse), (w_up3,) = mla_fwd(qh, kh, vh, exchange=([shards['w_up']], False))

    qk_act = qk_conv(z_main, conv_qk_f)
    gates = gate_act(z_tail, b_tile)

    def to_rows(cols):
        return cols.T.reshape(ML_HEADS, nc, 1, CHUNK)

    ig, fg = to_rows(gates[:, T_I:T_F]), to_rows(gates[:, T_F:T_F + ML_HEADS])
    h_ml, c_all, n_all, m_all = mlstm_fwd(qk_act, z_main, ig, fg)
    g_hn3 = g_hnorm_f.reshape(ML_HEADS, 1, ML_DV)
    y_b = mlstm_out(h_ml, z_main, g_hn3)

    ya = mm_nn(o_ab, p_a3, f32, "mm_pa")
    yb = mm_nn(y_b, p_b3, f32, "mm_pb")
    merged = merge_fwd(z_main, ya, yb)
    def add_and_norm(acc, tiles, rows):
        xs = tiles[0] + acc
        return ([xs, _rms(xs, rows[0])],)

    x1, uc = mm_nn_fused(merged, w_out3, [x2d], [g_cross], [f32, bf16], add_and_norm, "mm_out_resid", full_rows=True)
    mem_n = rms_fwd(mem2d, g_mem, "rms_mem")
    cq = mm_nn(uc, wq_c3, f32, "mm_cq")
    ck = mm_nn(mem_n, wk_c3, f32, "mm_ck")
    cv = mm_nn(mem_n, wv_c3, f32, "mm_cv")
    o_c = cross_fwd(cq, ck, cv, g_cq, g_ck)
    co = mm_nn(o_c, wo_c3, f32, "mm_oc")
    x2, u3 = resid_rms(x1, co, g_ffn, "resid_ffn")
    hup, (w_down_g,) = mm_nn(u3, w_up3, f32, "mm_up", exchange=([shards['w_down']], False))
    w_down3 = w_down_g.reshape(1, -1, d)
    gl = glu_fwd(hup, conv_ffn_f, b_conv_ffn)
    def loss_grad(acc, tiles, rows):
        err = tiles[0] + acc - tiles[1]
        dx = err / d
        part = 0.5 * jnp.sum(jnp.sum(err * err, axis=1, keepdims=True), axis=0, keepdims=True) / d
        return [dx, dx], part

    dx3, dx3_b, loss_acc = mm_nn_fused(gl, w_down3, [x2, tgt], [], [f32, bf16], loss_grad, "mm_down_loss", with_sum=True)

    grads, parts = {}, {}
    grads['w_down'] = mm_tn(gl, dx3_b, 1, "mm_d_wdown").reshape(N_DEV, -1, d)
    dgl = mm_nt(dx3_b, w_down3, f32, "mm_d_gl")
    (dh1, dh2, dcw1, dcw2, db1, db2), (parts['w_down'],) = glu_bwd(hup, conv_ffn_f, b_conv_ffn, dgl,
                                                                    exchange=([grads['w_down']], True))
    dconv_ffn, db_ffn = (jnp.concatenate(pair, axis=1) for pair in ((dcw1, dcw2), (db1, db2)))
    grads['w_up'] = mm_tn_cols(u3, [dh1, dh2], N_DEV, "mm_d_wup")
    du3 = mm_nt_cols([dh1, dh2], w_up3, f32, "mm_d_u3")
    dx2, dx2_b, dg_ffn = rms_bwd(x2, g_ffn, [du3], dx3, "rms_bwd_ffn", want_b16=True)
    grads['wo_c'] = mm_tn(o_c, dx2_b, N_DEV, "mm_d_woc")
    do_c = mm_nt(dx2_b, wo_c3, f32, "mm_d_oc")
    dcq, dck, dcv, dg_cq, dg_ck = cross_bwd(cq, ck, cv, g_cq, g_ck, do_c)
    grads['wq_c'] = mm_tn(uc, dcq, 1, "mm_d_wqc").reshape(N_DEV, -1, dcq.shape[1])
    grads['wk_c'] = mm_tn(mem_n, dck, 1, "mm_d_wkc").reshape(N_DEV, -1, dck.shape[1])
    grads['wv_c'] = mm_tn(mem_n, dcv, 1, "mm_d_wvc").reshape(N_DEV, -1, dcv.shape[1])
    duc = mm_nt(dcq, wq_c3, f32, "mm_d_uc")
    dmem_k = mm_nt(dck, wk_c3, f32, "mm_d_memk")
    dmem_v = mm_nt(dcv, wv_c3, f32, "mm_d_memv")
    dg_mem, = rms_bwd(mem2d, g_mem, [dmem_k, dmem_v], None, "rms_bwd_mem", want_dx=False)
    dx1, dx1_b, dg_cross = rms_bwd(x1, g_cross, [duc], dx2, "rms_bwd_cross", want_b16=True)
    grads['w_out'] = mm_tn(merged, dx1_b, 1, "mm_d_wout").reshape(N_DEV, -1, d)
    dmerged = mm_nt(dx1_b, w_out3, f32, "mm_d_merged")
    dga, dgb, dya, dyb = merge_bwd(z_main, ya, yb, dmerged)
    grads['p_a'] = mm_tn(o_ab, dya, 1, "mm_d_pa").reshape(N_DEV, -1, d)
    grads['p_b'] = mm_tn(y_b, dyb, 1, "mm_d_pb").reshape(N_DEV, -1, d)
    do_a = mm_nt(dya, p_a3, f32, "mm_d_oa")
    dy_b = mm_nt(dyb, p_b3, f32, "mm_d_yb")

    dh_ml, dzo, dg_hn = mlstm_out_bwd(h_ml, z_main, g_hn3, dy_b)
    mixers = ['p_a', 'p_b', 'w_out']
    (dq_act, dk_act, dzv, dig, dfg), got = mlstm_bwd(qk_act, z_main, ig, fg, c_all, n_all, m_all, dh_ml,
                                                     exchange=([grads[n] for n in mixers], True))
    parts.update(zip(mixers, got))
    dzqk, dconv_qk = qk_conv_bwd(z_main, conv_qk_f, dq_act, dk_act)

    delta = mla_delta(o_a, do_a)
    (dqh, dkh, dvh), (parts['w_up'],) = mla_bwd(qh, kh, vh, do_a, lse, delta.reshape(MLA_HEADS, 1, t),
                                                exchange=([grads['w_up']], True))
    dq_raw, dkv_raw, dzt_pe, dg_qn, dg_qp, dg_kn, dg_kp = mla_prep_bwd(
        q_raw, kv_raw, z_tail, cos, sin, g_qn_nope, gqp, g_kn_nope, gkp, dqh, dkh, dvh)
    d_wqb_p = mm_tn(qa_n, dq_raw, 1, "mm_d_wqb")[0].reshape(Q_LORA, MLA_HEADS, HEAD_PAD)[:, :, :NOPE + ROPE]
    grads['w_qb'] = d_wqb_p.reshape(Q_LORA, N_DEV, -1).transpose(1, 0, 2)
    grads['w_kvb'] = mm_tn(kv_n, dkv_raw, N_DEV, "mm_d_wkvb")
    dqa = mm_nt(dq_raw, w_qb_p, f32, "mm_d_qa")
    dkvn = mm_nt(dkv_raw, w_kvb3, f32, "mm_d_kvn")
    dz_lat, dg_qa, dg_kva = lat_norm_bwd(z_main, g_qa, g_kva, dqa, dkvn)

    def to_cols(rows):
        return rows.reshape(ML_HEADS, t).T

    dgate = jnp.pad(jnp.concatenate([to_cols(dig), to_cols(dfg)], axis=1), ((0, 0), (T_I, TAIL - T_I - 2 * ML_HEADS)))
    dz_tail, db_if = tail_bwd(z_tail, b_tile, dzt_pe, dgate)
    dz_main = [dz_lat, dzqk, dzv, dzo, dga, dgb]
    small_mats = ['wq_c', 'wk_c', 'wv_c', 'wo_c', 'w_qb', 'w_kvb']
    d_wmain3, got = mm_tn_cols(u0, dz_main, 1, "mm_d_wmain", exchange=([grads[n] for n in small_mats], True))
    parts.update(zip(small_mats, got))
    d_wmain = d_wmain3[0]
    d_wtail = mm_tn(u0, dz_tail, 1, "mm_d_wtail")[0]
    d_target = {'main': d_wmain, 'tail': d_wtail}
    blocks = []
    for j in range(N_DEV):
        lo_j, hi_j = j * shard_w, (j + 1) * shard_w
        cols = [d_target[tg][:, off + max(lo, lo_j) - lo:off + min(hi, hi_j) - lo]
                for lo, hi, tg, off in segments if lo < hi_j and hi > lo_j]
        blocks.append(jnp.concatenate(cols, axis=1))
    grads['w_in'] = jnp.stack(blocks)
    du0_a, (parts['w_in'],) = mm_nt_cols(dz_main, w_main, f32, "mm_d_u0_main", exchange=([grads['w_in']], True))
    du0_b = mm_nt(dz_tail, w_tail, f32, "mm_d_u0_tail")
    grad_x, dg_mix = rms_bwd(x2d, g_mix, [du0_a, du0_b], dx1, "rms_bwd_mix")

    out_g, out_d, out_m, out_v = {}, {}, {}, {}
    for n in big:
        res = adam_sum(parts[n], args[n], args['m_' + n], args['v_' + n], "adam_" + n)
        out_g[n], out_d[n], out_m[n], out_v[n] = (a.reshape(args[n].shape) for a in res)

    small_full = {
        'g_mix': dg_mix[0], 'g_qa': dg_qa[0], 'g_kva': dg_kva[0], 'g_qn_nope': dg_qn[0], 'g_qn_pe': dg_qp[0, :ROPE],
        'g_kn_nope': dg_kn[0], 'g_kn_pe': dg_kp[0, :ROPE], 'conv_qk': dconv_qk, 'b_if': db_if[0, T_I:T_I + 2 * ML_HEADS],
        'g_hnorm': dg_hn[:, 0, :], 'g_cross': dg_cross[0], 'g_mem': dg_mem[0], 'g_cq': dg_cq[0], 'g_ck': dg_ck[0],
        'g_ffn': dg_ffn[0], 'conv_ffn': dconv_ffn, 'b_conv_ffn': db_ffn[0], 'loss': loss_acc[0, :1]}
    order = list(small_full)
    packed = _pack([small_full[n] for n in order], 8 * 128).reshape(1, -1)
    gathered_small, = _exchange([packed], "comm_gather_small", scatter=False)
    summed = sum_parts(gathered_small.reshape(N_DEV, -1, 128), "sum_small").reshape(-1)
    full_g = dict(zip(order, _unpack(summed, [small_full[n].shape for n in order])))
    loss = full_g['loss'][0]

    local_g = {}
    for n in replicated:
        local_g[n] = full_g[n].reshape(args[n].shape)
    for n in sharded_small:
        shp = args[n].shape
        full = full_g[n].reshape((1,) + full_g[n].shape)
        local_g[n] = lax.dynamic_slice_in_dim(full, me * shp[-1], shp[-1], axis=2)
    small = replicated + sharded_small
    dl_f, m_f, v_f = adam_flat(*[_pack([src[n] if pre == '' else args[pre + n] for n in small], 8 * 128).reshape(-1, 128)
                                 for pre, src in (('', args), ('', local_g), ('m_', None), ('v_', None))], "adam_small")
    shapes = [args[n].shape for n in small]
    for dst, flat in ((out_d, dl_f), (out_m, m_f), (out_v, v_f)):
        dst.update(zip(small, _unpack(flat.reshape(-1), shapes)))
    out_g.update(local_g)

    return (loss, grad_x[None], *[out_g[n] for n in names], *[out_d[n] for n in names],
            *[out_m[n] for n in names], *[out_v[n] for n in names])
```

```python
import functools

import jax
import jax.numpy as jnp
from jax import lax
from jax.experimental import pallas as pl
from jax.experimental.pallas import tpu as pltpu

f32 = jnp.float32
bf16 = jnp.bfloat16

N_DEV = 8
EPS = 1e-6
CHUNK = 64
CHUNK_SHIFT = 6
assert 1 << CHUNK_SHIFT == CHUNK
MLA_HEADS = 16
Q_LORA = 512
KV_LORA = 512
NOPE = 128
ROPE = 64
V_HEAD = 128
ROPE_BASE = 10000.0
HEAD_PAD = 256
ML_HEADS = 8
ML_DK = 128
ML_DV = 256
ML_CONV = 4
ML_QK = ML_HEADS * ML_DK
ML_V = ML_HEADS * ML_DV
CR_HEADS = 4
CR_HD = 128
FFN_CONV = 3
ADAM_LR = 0.001
ADAM_B1 = 0.9
ADAM_B2 = 0.999
ADAM_EPS = 1e-08
ADAM_WD = 0.01
ADAM_STEP = 10
O_QA, O_KV, O_Q, O_K = 0, Q_LORA, Q_LORA + KV_LORA, Q_LORA + KV_LORA + ML_QK
O_V = O_K + ML_QK
O_O = O_V + ML_V
O_GA = O_O + ML_V
TAIL = 128
T_I, T_F = ROPE, ROPE + ML_HEADS
VMEM_LIMIT_V7X = 48 * 1024 * 1024
MESH = pl.DeviceIdType.MESH


def _call(body, name, grid, in_specs, out_specs, out_shape, scratch=(), exchange=None):
    params = pltpu.CompilerParams(vmem_limit_bytes=VMEM_LIMIT_V7X)
    if exchange is None:
        return pl.pallas_call(body, name=name, grid=grid, in_specs=in_specs, out_specs=out_specs, out_shape=out_shape,
                              scratch_shapes=list(scratch), compiler_params=params)
    arrs, scatter = exchange
    single = not isinstance(out_specs, (list, tuple))
    o_specs = [out_specs] if single else list(out_specs)
    o_shape = [out_shape] if single else list(out_shape)
    n_in, n_out, n_sc, n = len(in_specs), len(o_specs), len(scratch), len(arrs)
    any_spec = pl.BlockSpec(memory_space=pl.ANY)

    def body_with_exchange(*refs):
        pos = [0]

        def take(k):
            pos[0] += k
            return refs[pos[0] - k:pos[0]]

        ins, ex_in, outs, ex_out, sc = take(n_in), take(n), take(n_out), take(n), take(n_sc)
        start, middle, wait = _exchange_ops(ex_in, ex_out, refs[pos[0]:], scatter)
        step, total = 0, 1
        for a in range(len(grid)):
            step = step * grid[a] + pl.program_id(a)
            total *= grid[a]
        pl.when(step == 0)(start)
        body(*ins, *outs, *sc)
        if middle is not None:
            pl.when(step == total // 2)(middle)
        pl.when(step == total - 1)(wait)

    call = pl.pallas_call(body_with_exchange, name="comm_" + name, grid=grid, in_specs=list(in_specs) + [any_spec] * n,
                          out_specs=o_specs + [any_spec] * n, out_shape=o_shape + _exchange_shapes(arrs, scatter),
                          scratch_shapes=list(scratch) + _exchange_sems(n), compiler_params=params)

    def run(*operands):
        res = call(*operands, *arrs)
        return (res[0] if single else list(res[:n_out])), list(res[n_out:])

    return run


def _tile(n, cands):
    for c in cands:
        if n % c == 0:
            return c
    return n


def _sds(shape, dtype):
    return jax.ShapeDtypeStruct(tuple(shape), dtype)


def _bdot(a, b, ca, cb):
    return lax.dot_general(a.astype(bf16), b.astype(bf16), (((ca,), (cb,)), ((), ())), preferred_element_type=f32)


_BIG = (1024, 512, 256, 128)


def _col_tile(nb):
    return nb if nb <= 1536 else _tile(nb, _BIG)


_DEEP = (2048, 1024, 512, 256, 128)


def _mm_call(name, grid, in_specs, out_spec, out_shape, tile, nk, ca, cb, exchange, operands):
    def dot(a_ref, w_ref):
        return _bdot(a_ref[...], w_ref[0] if len(w_ref.shape) == 3 else w_ref[...], ca, cb)

    def store(o_ref, val):
        if len(o_ref.shape) == 3:
            o_ref[0] = val.astype(o_ref.dtype)
        else:
            o_ref[...] = val.astype(o_ref.dtype)

    if nk == 1:
        def body(a_ref, w_ref, o_ref):
            store(o_ref, dot(a_ref, w_ref))

        scratch = []
    else:
        def body(a_ref, w_ref, o_ref, acc):
            kk = pl.program_id(2)

            @pl.when(kk == 0)
            def _():
                acc[...] = jnp.zeros_like(acc)

            acc[...] += dot(a_ref, w_ref)

            @pl.when(kk == nk - 1)
            def _():
                store(o_ref, acc[...])

        scratch = [pltpu.VMEM(tile, f32)]
    return _call(body, name, grid, in_specs, out_spec, out_shape, scratch, exchange=exchange)(*operands)


def mm_nn(a, w3, out_dtype, name, exchange=None):
    m, k = a.shape
    nblk, k2, nb = w3.shape
    assert k == k2
    tm, tk, tn = _tile(m, _BIG), _tile(k, _DEEP), _col_tile(nb)
    per, nk = nb // tn, k // tk
    return _mm_call(name, (m // tm, nblk * per, nk),
                    [pl.BlockSpec((tm, tk), lambda i, j, kk: (i, kk)),
                     pl.BlockSpec((1, tk, tn), lambda i, j, kk: (j // per, kk, j % per))],
                    pl.BlockSpec((tm, tn), lambda i, j, kk: (i, j)), _sds((m, nblk * nb), out_dtype),
                    (tm, tn), nk, 1, 0, exchange, (a, w3))


def mm_nt(a, w3, out_dtype, name, exchange=None):
    m, n = a.shape
    nblk, k, nb = w3.shape
    assert n == nblk * nb
    tm, tn = _tile(m, _BIG), _tile(k, _BIG)
    tc = nb if nb <= 1536 else _tile(nb, _DEEP)
    per = nb // tc
    nk = nblk * per
    return _mm_call(name, (m // tm, k // tn, nk),
                    [pl.BlockSpec((tm, tc), lambda i, j, kk: (i, kk)),
                     pl.BlockSpec((1, tn, tc), lambda i, j, kk: (kk // per, j, kk % per))],
                    pl.BlockSpec((tm, tn), lambda i, j, kk: (i, j)), _sds((m, k), out_dtype),
                    (tm, tn), nk, 1, 1, exchange, (a, w3))


def mm_tn(a, b, nblk, name, exchange=None):
    r, m = a.shape
    r2, n = b.shape
    assert r == r2 and n % nblk == 0
    nb = n // nblk
    tm, tk, tn = _tile(m, _BIG), _tile(r, _DEEP), _col_tile(nb)
    per, nk = nb // tn, r // tk
    return _mm_call(name, (m // tm, nblk * per, nk),
                    [pl.BlockSpec((tk, tm), lambda i, j, kk: (kk, i)),
                     pl.BlockSpec((tk, tn), lambda i, j, kk: (kk, j))],
                    pl.BlockSpec((1, tm, tn), lambda i, j, kk: (j // per, i, j % per)), _sds((nblk, m, nb), bf16),
                    (tm, tn), nk, 0, 0, exchange, (a, b))


def mm_nn_fused(a, w3, tiles_in, rows_in, outs, epilogue, name, full_rows=False, with_sum=False):
    m, k = a.shape
    nblk, _, n = w3.shape
    assert nblk == 1
    tm = _tile(m, (512, 256, 128)) if full_rows else _tile(m, _BIG)
    tn = n if full_rows else _col_tile(n)
    tk = _tile(k, _BIG if full_rows else _DEEP)
    nk = k // tk
    n_t, n_r, n_o = len(tiles_in), len(rows_in), len(outs)

    def body(a_ref, w_ref, *refs):
        t_refs, r_refs = refs[:n_t], refs[n_t:n_t + n_r]
        o_refs = refs[n_t + n_r:n_t + n_r + n_o]
        acc = refs[-1]
        i, j, kk = pl.program_id(0), pl.program_id(1), pl.program_id(2)

        @pl.when(kk == 0)
        def _():
            acc[...] = jnp.zeros_like(acc)

        acc[...] += _bdot(a_ref[...], w_ref[0], 1, 0)

        @pl.when(kk == nk - 1)
        def _():
            res = epilogue(acc[...], [r[...] for r in t_refs], [r[...] for r in r_refs])
            for o_ref, val in zip(o_refs, res[0]):
                o_ref[...] = val.astype(o_ref.dtype)
            if with_sum:
                _acc_row(refs[n_t + n_r + n_o], jnp.broadcast_to(res[1], (1, 128)), _first(i, j))

    tile = pl.BlockSpec((tm, tn), lambda i, j, kk: (i, j))
    row = pl.BlockSpec((1, tn), lambda i, j, kk: (0, j))
    out_specs = [tile] * n_o + ([pl.BlockSpec((8, 128), lambda i, j, kk: (0, 0))] if with_sum else [])
    out_shape = [_sds((m, n), dt) for dt in outs] + ([_sds((8, 128), f32)] if with_sum else [])
    return _call(body, name, (m // tm, n // tn, nk),
                 [pl.BlockSpec((tm, tk), lambda i, j, kk: (i, kk)), pl.BlockSpec((1, tk, tn), lambda i, j, kk: (0, kk, j))]
                 + [tile] * n_t + [row] * n_r, out_specs, out_shape, [pltpu.VMEM((tm, tn), f32)])(a, w3, *tiles_in, *rows_in)


def _section_tiles(sections, cands):
    widths = [s.shape[1] for s in sections]
    tile = next(c for c in cands if all(w % c == 0 for w in widths))
    counts = [w // tile for w in widths]
    firsts = [sum(counts[:i]) for i in range(len(counts))]
    return tile, firsts, counts


SECTION_VMEM_BYTES = 24 * 1024 * 1024


def mm_tn_cols(a, sections, nblk, name, exchange=None):
    r, m = a.shape
    n = sum(s.shape[1] for s in sections)
    nb = n // nblk
    tn, firsts, counts = _section_tiles(sections, (nb,) if nb <= 1536 else _BIG)
    per = nb // tn
    tm = _tile(m, _BIG)
    tk = next(c for c in _DEEP if r % c == 0 and len(sections) * c * tn * 4 <= SECTION_VMEM_BYTES)
    nk = r // tk

    def body(a_ref, *refs):
        b_refs, o_ref, acc = refs[:len(sections)], refs[len(sections)], refs[len(sections) + 1]
        j, kk = pl.program_id(1), pl.program_id(2)

        @pl.when(kk == 0)
        def _():
            acc[...] = jnp.zeros_like(acc)

        for b_ref, lo, cnt in zip(b_refs, firsts, counts):
            @pl.when(jnp.logical_and(j >= lo, j < lo + cnt))
            def _(b_ref=b_ref):
                acc[...] += _bdot(a_ref[...], b_ref[...], 0, 0)

        @pl.when(kk == nk - 1)
        def _():
            o_ref[0] = acc[...].astype(bf16)

    def spec(lo, cnt):
        def index(i, j, kk):
            return jnp.where(j < lo, 0, jnp.where(j >= lo + cnt, nk - 1, kk)), jnp.clip(j - lo, 0, cnt - 1)
        return pl.BlockSpec((tk, tn), index)

    return _call(body, name, (m // tm, n // tn, nk),
                 [pl.BlockSpec((tk, tm), lambda i, j, kk: (kk, i))] + [spec(lo, cnt) for lo, cnt in zip(firsts, counts)],
                 pl.BlockSpec((1, tm, tn), lambda i, j, kk: (j // per, i, j % per)), _sds((nblk, m, nb), bf16),
                 [pltpu.VMEM((tm, tn), f32)], exchange=exchange)(a, *sections)


def mm_nt_cols(sections, w3, out_dtype, name, exchange=None):
    m = sections[0].shape[0]
    nblk, k, nb = w3.shape
    tc, firsts, counts = _section_tiles(sections, (nb,) if nb <= 1536 else _DEEP)
    assert nblk * nb == tc * sum(counts) and nb % tc == 0
    per = nb // tc
    tm, tn = _tile(m, _BIG), _tile(k, _BIG)
    nk = nblk * per

    def body(*refs):
        a_refs, w_ref, o_ref, acc = refs[:len(sections)], refs[len(sections)], refs[len(sections) + 1], refs[len(sections) + 2]
        kk = pl.program_id(2)

        @pl.when(kk == 0)
        def _():
            acc[...] = jnp.zeros_like(acc)

        for a_ref, lo, cnt in zip(a_refs, firsts, counts):
            @pl.when(jnp.logical_and(kk >= lo, kk < lo + cnt))
            def _(a_ref=a_ref):
                acc[...] += _bdot(a_ref[...], w_ref[0], 1, 1)

        @pl.when(kk == nk - 1)
        def _():
            o_ref[...] = acc[...].astype(o_ref.dtype)

    def spec(lo, cnt):
        return pl.BlockSpec((tm, tc), lambda i, j, kk: (i, jnp.clip(kk - lo, 0, cnt - 1)))

    return _call(body, name, (m // tm, k // tn, nk),
                 [spec(lo, cnt) for lo, cnt in zip(firsts, counts)] + [pl.BlockSpec((1, tn, tc), lambda i, j, kk: (kk // per, j, kk % per))],
                 pl.BlockSpec((tm, tn), lambda i, j, kk: (i, j)), _sds((m, k), out_dtype),
                 [pltpu.VMEM((tm, tn), f32)], exchange=exchange)(*sections, w3)


def _rms(x, g):
    return x * lax.rsqrt(jnp.mean(x * x, axis=-1, keepdims=True) + EPS) * g


def _rms_pad(x, g, width):
    return x * lax.rsqrt(jnp.sum(x * x, axis=-1, keepdims=True) / width + EPS) * g


def _first(*ids):
    ok = ids[0] == 0
    for i in ids[1:]:
        ok = jnp.logical_and(ok, i == 0)
    return ok


def _acc_row(ref, val, first):
    @pl.when(first)
    def _():
        ref[...] = jnp.zeros_like(ref)

    ref[0:1, :] += val


def rms_fwd(x, g, name):
    r, w = x.shape
    tm = _tile(r, (256, 128, 64, 32, 16, 8))

    def body(x_ref, g_ref, o_ref):
        o_ref[...] = _rms(x_ref[...], g_ref[...]).astype(bf16)

    return _call(body, name, (r // tm,), [pl.BlockSpec((tm, w), lambda i: (i, 0)), pl.BlockSpec((1, w), lambda i: (0, 0))],
                 pl.BlockSpec((tm, w), lambda i: (i, 0)), _sds((r, w), bf16))(x, g)


def resid_rms(xa, xb, g, name):
    r, w = xa.shape
    tm = _tile(r, (256, 128, 64, 32, 16, 8))

    def body(a_ref, b_ref, g_ref, s_ref, u_ref):
        xs = a_ref[...] + b_ref[...]
        s_ref[...] = xs
        u_ref[...] = _rms(xs, g_ref[...]).astype(bf16)

    row = pl.BlockSpec((tm, w), lambda i: (i, 0))
    return _call(body, name, (r // tm,), [row, row, pl.BlockSpec((1, w), lambda i: (0, 0))], [row, row],
                 [_sds((r, w), f32), _sds((r, w), bf16)])(xa, xb, g)


def rms_bwd(x, g, dys, dres, name, want_dx=True, want_b16=False):
    r, w = x.shape
    tm = _tile(r, (256, 128, 64, 32, 16, 8))
    nd = len(dys)

    def body(*refs):
        x_ref, g_ref = refs[0], refs[1]
        dy = refs[2][...]
        for j in range(1, nd):
            dy = dy + refs[2 + j][...]
        pos = 2 + nd
        _, vjp = jax.vjp(_rms, x_ref[...], g_ref[...])
        dx, dg = vjp(dy)
        if dres is not None:
            dx = dx + refs[pos][...]
            pos += 1
        if want_dx:
            refs[pos][...] = dx
            pos += 1
        if want_b16:
            refs[pos][...] = dx.astype(bf16)
            pos += 1
        _acc_row(refs[pos], dg, pl.program_id(0) == 0)

    row = pl.BlockSpec((tm, w), lambda i: (i, 0))
    ins = [x, g] + list(dys) + ([dres] if dres is not None else [])
    in_specs = [row, pl.BlockSpec((1, w), lambda i: (0, 0))] + [row] * (nd + (dres is not None))
    out_specs = [row] * (want_dx + want_b16) + [pl.BlockSpec((8, w), lambda i: (0, 0))]
    out_shape = ([_sds((r, w), f32)] if want_dx else []) + ([_sds((r, w), bf16)] if want_b16 else []) + [_sds((8, w), f32)]
    return _call(body, name, (r // tm,), in_specs, out_specs, out_shape)(*ins)


def lat_norm(z_main, g_qa, g_kva):
    t = z_main.shape[0]
    tm = _tile(t, (512, 256, 128, 64))

    def body(z_ref, gq_ref, gk_ref, q_ref, k_ref):
        q_ref[...] = _rms(z_ref[:, :Q_LORA], gq_ref[...]).astype(bf16)
        k_ref[...] = _rms(z_ref[:, Q_LORA:], gk_ref[...]).astype(bf16)

    return _call(body, "lat_norm", (t // tm,),
                 [pl.BlockSpec((tm, Q_LORA + KV_LORA), lambda i: (i, 0)), pl.BlockSpec((1, Q_LORA), lambda i: (0, 0)),
                  pl.BlockSpec((1, KV_LORA), lambda i: (0, 0))],
                 [pl.BlockSpec((tm, Q_LORA), lambda i: (i, 0)), pl.BlockSpec((tm, KV_LORA), lambda i: (i, 0))],
                 [_sds((t, Q_LORA), bf16), _sds((t, KV_LORA), bf16)])(z_main, g_qa, g_kva)


def lat_norm_bwd(z_main, g_qa, g_kva, dqa, dkv):
    t = z_main.shape[0]
    tm = _tile(t, (512, 256, 128, 64))

    def body(z_ref, gq_ref, gk_ref, dq_ref, dk_ref, dz_ref, dgq_ref, dgk_ref):
        first = pl.program_id(0) == 0
        _, vq = jax.vjp(_rms, z_ref[:, :Q_LORA], gq_ref[...])
        dx, dg = vq(dq_ref[...])
        dz_ref[:, :Q_LORA] = dx.astype(bf16)
        _acc_row(dgq_ref, dg, first)
        _, vk = jax.vjp(_rms, z_ref[:, Q_LORA:], gk_ref[...])
        dx, dg = vk(dk_ref[...])
        dz_ref[:, Q_LORA:] = dx.astype(bf16)
        _acc_row(dgk_ref, dg, first)

    return _call(body, "lat_norm_bwd", (t // tm,),
                 [pl.BlockSpec((tm, Q_LORA + KV_LORA), lambda i: (i, 0)), pl.BlockSpec((1, Q_LORA), lambda i: (0, 0)),
                  pl.BlockSpec((1, KV_LORA), lambda i: (0, 0)), pl.BlockSpec((tm, Q_LORA), lambda i: (i, 0)),
                  pl.BlockSpec((tm, KV_LORA), lambda i: (i, 0))],
                 [pl.BlockSpec((tm, Q_LORA + KV_LORA), lambda i: (i, 0)), pl.BlockSpec((8, Q_LORA), lambda i: (0, 0)),
                  pl.BlockSpec((8, KV_LORA), lambda i: (0, 0))],
                 [_sds((t, Q_LORA + KV_LORA), bf16), _sds((8, Q_LORA), f32), _sds((8, KV_LORA), f32)])(z_main, g_qa, g_kva, dqa, dkv)


def rope_tables(pos_col, inv_freq):
    t = pos_col.shape[0]
    tm = _tile(t, (512, 256, 128, 64))

    def body(p_ref, f_ref, c_ref, s_ref):
        ang = p_ref[...].astype(f32) * f_ref[...]
        lane = lax.broadcasted_iota(jnp.int32, ang.shape, 1)
        c_ref[...] = jnp.where(lane < ROPE, jnp.cos(ang), 0.0)
        sn = jnp.sin(ang)
        s_ref[...] = jnp.where(lane < ROPE // 2, -sn, jnp.where(lane < ROPE, sn, 0.0))

    return _call(body, "rope_tables", (t // tm,),
                 [pl.BlockSpec((tm, 1), lambda i: (i, 0)), pl.BlockSpec((1, TAIL), lambda i: (0, 0))],
                 [pl.BlockSpec((tm, TAIL), lambda i: (i, 0))] * 2, [_sds((t, TAIL), f32)] * 2)(pos_col, inv_freq)


def _swap_halves(n):
    lane = lax.broadcasted_iota(jnp.int32, n.shape, 1)
    return jnp.where(lane < ROPE // 2, pltpu.roll(n, TAIL - ROPE // 2, 1), pltpu.roll(n, ROPE // 2, 1))


def _rope(n, c, s):
    return n * c + _swap_halves(n) * s


def _rope_t(d, c, s):
    return d * c + _swap_halves(d * s)


def _prep_specs(tm):
    head = pl.BlockSpec((tm, HEAD_PAD), lambda i, h: (i, h))
    row = pl.BlockSpec((tm, TAIL), lambda i, h: (i, 0))
    gain = pl.BlockSpec((1, TAIL), lambda i, h: (0, 0))
    return head, row, gain


def _pe_in(zt):
    lane = lax.broadcasted_iota(jnp.int32, zt.shape, 1)
    return jnp.where(lane < ROPE, zt, 0.0)


def mla_prep(q_raw, kv_raw, z_tail, cos, sin, gqn, gqp, gkn, gkp):
    t = q_raw.shape[0]
    tm = _tile(t, (1024, 512, 256, 128, 64))

    def body(q_ref, kv_ref, zt_ref, c_ref, s_ref, gqn_ref, gqp_ref, gkn_ref, gkp_ref, qh_ref, kh_ref, vh_ref):
        c, s = c_ref[...], s_ref[...]
        qh_ref[:, :NOPE] = _rms(q_ref[:, :NOPE], gqn_ref[...]).astype(bf16)
        qh_ref[:, NOPE:] = _rope(_rms_pad(q_ref[:, NOPE:], gqp_ref[...], ROPE), c, s).astype(bf16)
        kh_ref[:, :NOPE] = _rms(kv_ref[:, :NOPE], gkn_ref[...]).astype(bf16)
        kh_ref[:, NOPE:] = _rope(_rms_pad(_pe_in(zt_ref[...]), gkp_ref[...], ROPE), c, s).astype(bf16)
        vh_ref[...] = kv_ref[:, NOPE:].astype(bf16)

    head, row, gain = _prep_specs(tm)
    return _call(body, "mla_prep", (t // tm, MLA_HEADS), [head, head, row, row, row, gain, gain, gain, gain],
                 [head, head, pl.BlockSpec((tm, V_HEAD), lambda i, h: (i, h))],
                 [_sds((t, MLA_HEADS * HEAD_PAD), bf16), _sds((t, MLA_HEADS * HEAD_PAD), bf16), _sds((t, MLA_HEADS * V_HEAD), bf16)],
                 )(q_raw, kv_raw, z_tail, cos, sin, gqn, gqp, gkn, gkp)


def mla_prep_bwd(q_raw, kv_raw, z_tail, cos, sin, gqn, gqp, gkn, gkp, dqh, dkh, dvh):
    t = q_raw.shape[0]
    tm = _tile(t, (1024, 512, 256, 128, 64))
    pad_norm = functools.partial(_rms_pad, width=ROPE)

    def body(q_ref, kv_ref, zt_ref, c_ref, s_ref, gqn_ref, gqp_ref, gkn_ref, gkp_ref, dqh_ref, dkh_ref, dvh_ref,
             dq_ref, dkv_ref, dzt_ref, dgqn_ref, dgqp_ref, dgkn_ref, dgkp_ref):
        i, h = pl.program_id(0), pl.program_id(1)
        first = _first(i, h)
        c, s = c_ref[...], s_ref[...]
        _, v1 = jax.vjp(_rms, q_ref[:, :NOPE], gqn_ref[...])
        dx, dg = v1(dqh_ref[:, :NOPE])
        dq_ref[:, :NOPE] = dx.astype(bf16)
        _acc_row(dgqn_ref, dg, first)
        _, v2 = jax.vjp(pad_norm, q_ref[:, NOPE:], gqp_ref[...])
        dx, dg = v2(_rope_t(dqh_ref[:, NOPE:], c, s))
        dq_ref[:, NOPE:] = dx.astype(bf16)
        _acc_row(dgqp_ref, dg, first)
        _, v3 = jax.vjp(_rms, kv_ref[:, :NOPE], gkn_ref[...])
        dx, dg = v3(dkh_ref[:, :NOPE])
        dkv_ref[:, :NOPE] = dx.astype(bf16)
        _acc_row(dgkn_ref, dg, first)
        dkv_ref[:, NOPE:] = dvh_ref[...].astype(bf16)
        _, v4 = jax.vjp(pad_norm, _pe_in(zt_ref[...]), gkp_ref[...])
        dx, dg = v4(_rope_t(dkh_ref[:, NOPE:], c, s))
        _acc_row(dgkp_ref, dg, first)

        @pl.when(h == 0)
        def _():
            dzt_ref[...] = jnp.zeros_like(dzt_ref)

        dzt_ref[...] += dx

    head, row, gain = _prep_specs(tm)
    acc = pl.BlockSpec((8, TAIL), lambda i, h: (0, 0))
    vspec = pl.BlockSpec((tm, V_HEAD), lambda i, h: (i, h))
    return _call(body, "mla_prep_bwd", (t // tm, MLA_HEADS),
                 [head, head, row, row, row, gain, gain, gain, gain, head, head, vspec],
                 [head, head, row, acc, acc, acc, acc],
                 [_sds((t, MLA_HEADS * HEAD_PAD), bf16), _sds((t, MLA_HEADS * HEAD_PAD), bf16), _sds((t, TAIL), f32)]
                 + [_sds((8, TAIL), f32)] * 4)(q_raw, kv_raw, z_tail, cos, sin, gqn, gqp, gkn, gkp, dqh, dkh, dvh)


ATT_BLOCK = 512
NEG = -1e30
ATT_SCALE = (NOPE + ROPE) ** -0.5
LOG2_E = 1.4426950408889634
ATT_SCALE2 = ATT_SCALE * LOG2_E
ATT_HEADS = 2
ATT_HEADS_FWD = 8


def _chunk_visible(shape, key_axis):
    kc = lax.broadcasted_iota(jnp.int32, shape, key_axis) >> CHUNK_SHIFT
    qc = lax.broadcasted_iota(jnp.int32, shape, 1 - key_axis) >> CHUNK_SHIFT
    return kc <= qc


def mla_fwd(qh, kh, vh, exchange=None):
    t = qh.shape[0]
    tb = min(ATT_BLOCK, t)
    nb = t // tb

    hp = ATT_HEADS_FWD

    def body(q_ref, k_ref, v_ref, o_ref, ob_ref, lse_ref, m_s, l_s, acc):
        qi, ki = pl.program_id(1), pl.program_id(2)

        @pl.when(ki == 0)
        def _():
            m_s[...] = jnp.full_like(m_s, NEG)
            l_s[...] = jnp.zeros_like(l_s)
            acc[...] = jnp.zeros_like(acc)

        def step(diagonal):
            new = []
            for j in range(hp):
                q, k = q_ref[:, j * HEAD_PAD:(j + 1) * HEAD_PAD], k_ref[:, j * HEAD_PAD:(j + 1) * HEAD_PAD]
                s = _bdot(k, q, 1, 1) * ATT_SCALE2
                if diagonal:
                    s = jnp.where(_chunk_visible(s.shape, 0), s, -jnp.inf)
                m_old = m_s[j]
                m_new = jnp.maximum(m_old, jnp.max(s, axis=0, keepdims=True))
                p = jnp.exp2(s - m_new)
                alpha = jnp.exp2(m_old - m_new)
                l_new = alpha * l_s[j] + jnp.sum(p, axis=0, keepdims=True)
                acc_new = alpha * acc[j] + _bdot(v_ref[:, j * V_HEAD:(j + 1) * V_HEAD], p, 0, 0)
                new.append((m_new, l_new, acc_new))
            for j, (m_new, l_new, acc_new) in enumerate(new):
                m_s[j] = m_new
                l_s[j] = l_new
                acc[j] = acc_new
            return new

        @pl.when(ki < qi)
        def _():
            step(False)

        @pl.when(ki == qi)
        def _():
            for j, (m_new, l_new, acc_new) in enumerate(step(True)):
                o = (acc_new / l_new).T
                o_ref[:, j * V_HEAD:(j + 1) * V_HEAD] = o
                ob_ref[:, j * V_HEAD:(j + 1) * V_HEAD] = o.astype(bf16)
                lse_ref[j] = m_new + jnp.log2(l_new)

    kv = lambda g, qi, ki: (jnp.minimum(ki, qi), g)
    o_spec = pl.BlockSpec((tb, hp * V_HEAD), lambda g, qi, ki: (qi, g))
    return _call(body, "mla_fwd", (MLA_HEADS // hp, nb, nb),
                 [pl.BlockSpec((tb, hp * HEAD_PAD), lambda g, qi, ki: (qi, g)), pl.BlockSpec((tb, hp * HEAD_PAD), kv),
                  pl.BlockSpec((tb, hp * V_HEAD), kv)],
                 [o_spec, o_spec, pl.BlockSpec((hp, 1, tb), lambda g, qi, ki: (g, 0, qi))],
                 [_sds((t, MLA_HEADS * V_HEAD), f32), _sds((t, MLA_HEADS * V_HEAD), bf16), _sds((MLA_HEADS, 1, t), f32)],
                 [pltpu.VMEM((hp, 1, tb), f32), pltpu.VMEM((hp, 1, tb), f32), pltpu.VMEM((hp, V_HEAD, tb), f32)],
                 exchange=exchange)(qh, kh, vh)


def mla_delta(o, do):
    t = o.shape[0]
    tm = _tile(t, (512, 256, 128, 64))

    def body(o_ref, do_ref, d_ref):
        for h in range(MLA_HEADS):
            cols = slice(h * V_HEAD, (h + 1) * V_HEAD)
            d_ref[h] = jnp.sum(o_ref[:, cols] * do_ref[:, cols], axis=1, keepdims=True)

    blk = pl.BlockSpec((tm, MLA_HEADS * V_HEAD), lambda i: (i, 0))
    return _call(body, "mla_delta", (t // tm,), [blk, blk], pl.BlockSpec((MLA_HEADS, tm, 1), lambda i: (0, i, 0)),
                 _sds((MLA_HEADS, t, 1), f32))(o, do)


def mla_bwd(qh, kh, vh, do, lse_row, delta_row, exchange=None):
    t = qh.shape[0]
    tb = min(ATT_BLOCK, t)
    nb = t // tb

    hp = ATT_HEADS

    def body(q_ref, k_ref, v_ref, do_ref, lse_ref, dl_ref, dq_ref, dk_ref, dv_ref, dk_acc, dv_acc):
        ki, qi = pl.program_id(1), pl.program_id(2)

        @pl.when(jnp.logical_and(ki == 0, qi == 0))
        def _():
            dq_ref[...] = jnp.zeros_like(dq_ref)

        @pl.when(qi == 0)
        def _():
            dk_acc[...] = jnp.zeros_like(dk_acc)
            dv_acc[...] = jnp.zeros_like(dv_acc)

        def step(diagonal):
            rows = pl.ds(pl.multiple_of(qi * tb, tb), tb)
            new = []
            for j in range(hp):
                qc, vc = slice(j * HEAD_PAD, (j + 1) * HEAD_PAD), slice(j * V_HEAD, (j + 1) * V_HEAD)
                q, k, do_b = q_ref[:, qc], k_ref[:, qc], do_ref[:, vc]
                s = _bdot(k, q, 1, 1) * ATT_SCALE2
                if diagonal:
                    s = jnp.where(_chunk_visible(s.shape, 0), s, -jnp.inf)
                p = jnp.exp2(s - lse_ref[j])
                dp = _bdot(v_ref[:, vc], do_b, 1, 1)
                ds = p * (dp - dl_ref[j]) * ATT_SCALE
                new.append((dv_acc[:, vc] + _bdot(p, do_b, 1, 0), dk_acc[:, qc] + _bdot(ds, q, 1, 0),
                            dq_ref[rows, qc] + _bdot(ds, k, 0, 0)))
            for j, (dv, dk, dq) in enumerate(new):
                dv_acc[:, j * V_HEAD:(j + 1) * V_HEAD] = dv
                dk_acc[:, j * HEAD_PAD:(j + 1) * HEAD_PAD] = dk
                dq_ref[rows, j * HEAD_PAD:(j + 1) * HEAD_PAD] = dq

        @pl.when(qi > ki)
        def _():
            step(False)

        @pl.when(qi == ki)
        def _():
            step(True)

        @pl.when(qi == nb - 1)
        def _():
            dk_ref[...] = dk_acc[...]
            dv_ref[...] = dv_acc[...]

    qs = lambda g, ki, qi: (jnp.maximum(qi, ki), g)
    ks = lambda g, ki, qi: (ki, g)
    vec = pl.BlockSpec((hp, 1, tb), lambda g, ki, qi: (g, 0, jnp.maximum(qi, ki)))
    return _call(body, "mla_bwd", (MLA_HEADS // hp, nb, nb),
                 [pl.BlockSpec((tb, hp * HEAD_PAD), qs), pl.BlockSpec((tb, hp * HEAD_PAD), ks), pl.BlockSpec((tb, hp * V_HEAD), ks),
                  pl.BlockSpec((tb, hp * V_HEAD), qs), vec, vec],
                 [pl.BlockSpec((t, hp * HEAD_PAD), lambda g, ki, qi: (0, g)), pl.BlockSpec((tb, hp * HEAD_PAD), ks),
                  pl.BlockSpec((tb, hp * V_HEAD), ks)],
                 [_sds((t, MLA_HEADS * HEAD_PAD), f32), _sds((t, MLA_HEADS * HEAD_PAD), f32), _sds((t, MLA_HEADS * V_HEAD), f32)],
                 [pltpu.VMEM((tb, hp * HEAD_PAD), f32), pltpu.VMEM((tb, hp * V_HEAD), f32)], exchange=exchange)(
        qh, kh, vh, do, lse_row, delta_row)


PAD = 8


def _conv_taps(pad_ref, w, width, t):
    y = pad_ref[PAD - width + 1:PAD - width + 1 + t, :] * w[0:1, :]
    for j in range(1, width):
        y = y + pad_ref[PAD - width + 1 + j:PAD - width + 1 + j + t, :] * w[j:j + 1, :]
    return y


def _conv_bwd(xpad_ref, dpad_ref, w, da, width, t):
    dpad_ref[0:t, :] = da
    dpad_ref[t:t + PAD, :] = jnp.zeros((PAD, da.shape[1]), f32)
    dx = dpad_ref[width - 1:width - 1 + t, :] * w[0:1, :]
    for j in range(1, width):
        dx = dx + dpad_ref[width - 1 - j:width - 1 - j + t, :] * w[j:j + 1, :]
    dws = [jnp.sum(da * xpad_ref[PAD - width + 1 + j:PAD - width + 1 + j + t, :], axis=0, keepdims=True) for j in range(width)]
    return dx, dws


def _load_pad(pad_ref, x, t):
    pad_ref[0:PAD, :] = jnp.zeros((PAD, x.shape[1]), f32)
    pad_ref[PAD:PAD + t, :] = x


assert ML_DK == 128


def qk_conv(z_main, conv_qk):
    t = z_main.shape[0]
    base = O_Q // ML_DK

    def body(z_ref, w_ref, o_ref, pad):
        _load_pad(pad, z_ref[...], t)
        a = _conv_taps(pad, w_ref[...], ML_CONV, t)
        sc = jnp.where(pl.program_id(0) < ML_HEADS, ML_DK ** -0.5, 1.0)
        o_ref[0] = jax.nn.silu(a) * sc

    return _call(body, "qk_conv", (2 * ML_HEADS,),
                 [pl.BlockSpec((t, ML_DK), lambda j: (0, base + j)), pl.BlockSpec((ML_CONV, ML_DK), lambda j: (0, j))],
                 pl.BlockSpec((1, t, ML_DK), lambda j: (j, 0, 0)), _sds((2 * ML_HEADS, t, ML_DK), f32),
                 [pltpu.VMEM((t + PAD, ML_DK), f32)])(z_main, conv_qk)


def qk_conv_bwd(z_main, conv_qk, dq, dk):
    t = z_main.shape[0]
    base = O_Q // ML_DK

    def body(z_ref, w_ref, dq_ref, dk_ref, dz_ref, dw_ref, pad, dpad):
        _load_pad(pad, z_ref[...], t)
        w = w_ref[...]
        a = _conv_taps(pad, w, ML_CONV, t)
        is_q = pl.program_id(0) < ML_HEADS
        d = jnp.where(is_q, dq_ref[0] * (ML_DK ** -0.5), dk_ref[0])
        _, vjp = jax.vjp(jax.nn.silu, a)
        da, = vjp(d)
        dx, dws = _conv_bwd(pad, dpad, w, da, ML_CONV, t)
        dz_ref[...] = dx.astype(bf16)
        for j in range(ML_CONV):
            dw_ref[j:j + 1, :] = dws[j]

    head = lambda pick: pl.BlockSpec((1, t, ML_DK), lambda j: (pick(j), 0, 0))
    return _call(body, "qk_conv_bwd", (2 * ML_HEADS,),
                 [pl.BlockSpec((t, ML_DK), lambda j: (0, base + j)), pl.BlockSpec((ML_CONV, ML_DK), lambda j: (0, j)),
                  head(lambda j: jnp.minimum(j, ML_HEADS - 1)), head(lambda j: jnp.maximum(j - ML_HEADS, 0))],
                 [pl.BlockSpec((t, ML_DK), lambda j: (0, j)), pl.BlockSpec((ML_CONV, ML_DK), lambda j: (0, j))],
                 [_sds((t, 2 * ML_QK), bf16), _sds((ML_CONV, 2 * ML_QK), f32)],
                 [pltpu.VMEM((t + PAD, ML_DK), f32), pltpu.VMEM((t + PAD, ML_DK), f32)])(z_main, conv_qk, dq, dk)


def glu_fwd(hup, conv_w, bias):
    t, f2 = hup.shape
    nf = f2 // 2 // 128

    def body(h1_ref, h2_ref, w1_ref, w2_ref, b1_ref, b2_ref, o_ref, pad):
        _load_pad(pad, h1_ref[...], t)
        a1 = _conv_taps(pad, w1_ref[...], FFN_CONV, t) + b1_ref[...]
        _load_pad(pad, h2_ref[...], t)
        a2 = _conv_taps(pad, w2_ref[...], FFN_CONV, t) + b2_ref[...]
        o_ref[...] = (jax.nn.silu(a1) * a2).astype(bf16)

    col = lambda off: pl.BlockSpec((t, 128), lambda j: (0, j + off))
    wsp = lambda off: pl.BlockSpec((FFN_CONV, 128), lambda j: (0, j + off))
    bsp = lambda off: pl.BlockSpec((1, 128), lambda j: (0, j + off))
    return _call(body, "glu_fwd", (nf,), [col(0), col(nf), wsp(0), wsp(nf), bsp(0), bsp(nf)], col(0), _sds((t, f2 // 2), bf16),
                 [pltpu.VMEM((t + PAD, 128), f32)])(hup, hup, conv_w, conv_w, bias, bias)


def glu_bwd(hup, conv_w, bias, dg, exchange=None):
    t, f2 = hup.shape
    f = f2 // 2
    nf = f // 128

    def body(h1_ref, h2_ref, w1_ref, w2_ref, b1_ref, b2_ref, dg_ref, dh1_ref, dh2_ref, dw1_ref, dw2_ref, db1_ref, db2_ref,
             pad1, pad2, dpad):
        _load_pad(pad1, h1_ref[...], t)
        _load_pad(pad2, h2_ref[...], t)
        w1, w2 = w1_ref[...], w2_ref[...]
        a1 = _conv_taps(pad1, w1, FFN_CONV, t) + b1_ref[...]
        a2 = _conv_taps(pad2, w2, FFN_CONV, t) + b2_ref[...]
        d = dg_ref[...]
        _, vjp = jax.vjp(jax.nn.silu, a1)
        da1, = vjp(d * a2)
        da2 = d * jax.nn.silu(a1)
        for da, pad, w, dh_ref, dw_ref, db_ref in ((da1, pad1, w1, dh1_ref, dw1_ref, db1_ref), (da2, pad2, w2, dh2_ref, dw2_ref, db2_ref)):
            dx, dws = _conv_bwd(pad, dpad, w, da, FFN_CONV, t)
            dh_ref[...] = dx.astype(bf16)
            for j in range(FFN_CONV):
                dw_ref[j:j + 1, :] = dws[j]
            db_ref[...] = jnp.sum(da, axis=0, keepdims=True)

    col = lambda off: pl.BlockSpec((t, 128), lambda j: (0, j + off))
    wsp = lambda off: pl.BlockSpec((FFN_CONV, 128), lambda j: (0, j + off))
    bsp = lambda off: pl.BlockSpec((1, 128), lambda j: (0, j + off))
    return _call(body, "glu_bwd", (nf,), [col(0), col(nf), wsp(0), wsp(nf), bsp(0), bsp(nf), col(0)],
                 [col(0), col(0), wsp(0), wsp(0), bsp(0), bsp(0)],
                 [_sds((t, f), bf16)] * 2 + [_sds((FFN_CONV, f), f32)] * 2 + [_sds((1, f), f32)] * 2,
                 [pltpu.VMEM((t + PAD, 128), f32)] * 3, exchange=exchange)(hup, hup, conv_w, conv_w, bias, bias, dg)


def gate_act(z_tail, b_tile):
    t = z_tail.shape[0]
    tm = _tile(t, (512, 256, 128, 64))

    def body(z_ref, b_ref, o_ref):
        x = z_ref[...] + b_ref[...]
        lane = lax.broadcasted_iota(jnp.int32, x.shape, 1)
        o_ref[...] = jnp.where(lane < T_F, x, jax.nn.log_sigmoid(x))

    row = pl.BlockSpec((tm, TAIL), lambda i: (i, 0))
    return _call(body, "gate_act", (t // tm,), [row, pl.BlockSpec((1, TAIL), lambda i: (0, 0))], row, _sds((t, TAIL), f32))(z_tail, b_tile)


def tail_bwd(z_tail, b_tile, dzt_pe, dgate):
    t = z_tail.shape[0]
    tm = _tile(t, (512, 256, 128, 64))

    def body(z_ref, b_ref, dpe_ref, dg_ref, dz_ref, db_ref):
        x = z_ref[...] + b_ref[...]
        lane = lax.broadcasted_iota(jnp.int32, x.shape, 1)
        _, vjp = jax.vjp(jax.nn.log_sigmoid, x)
        df, = vjp(dg_ref[...])
        dgates = jnp.where(lane < T_F, dg_ref[...], df)
        dgates = jnp.where(jnp.logical_and(lane >= T_I, lane < T_F + ML_HEADS), dgates, 0.0)
        dz_ref[...] = jnp.where(lane < ROPE, dpe_ref[...], dgates).astype(bf16)
        _acc_row(db_ref, jnp.sum(dgates, axis=0, keepdims=True), pl.program_id(0) == 0)

    row = pl.BlockSpec((tm, TAIL), lambda i: (i, 0))
    return _call(body, "tail_bwd", (t // tm,), [row, pl.BlockSpec((1, TAIL), lambda i: (0, 0)), row, row],
                 [row, pl.BlockSpec((8, TAIL), lambda i: (0, 0))], [_sds((t, TAIL), bf16), _sds((8, TAIL), f32)])(z_tail, b_tile, dzt_pe, dgate)


def _hdot(a, b, ca, cb):
    return lax.dot_general(a.astype(bf16), b.astype(bf16), (((ca,), (cb,)), ((0,), (0,))), preferred_element_type=f32)


def _mlstm_step(q, k, v, igr, fgr, c_mat, n_vec, m):
    nh, ln = q.shape[0], CHUNK
    sq = (nh, ln, ln)
    row = lax.broadcasted_iota(jnp.int32, sq, 1)
    col = lax.broadcasted_iota(jnp.int32, sq, 2)
    eye = row == col

    def to_col(r):
        return jnp.sum(jnp.where(eye, jnp.broadcast_to(r, sq), 0.0), axis=2, keepdims=True)

    bc_r = jnp.sum(jnp.where(row <= col, jnp.broadcast_to(to_col(fgr), sq), 0.0), axis=1, keepdims=True)
    bc_c = to_col(bc_r)
    logw = jnp.where(col <= row, bc_c - bc_r + igr, -jnp.inf)
    inter = bc_c + m
    m_t = jnp.maximum(inter, jnp.max(logw, axis=2, keepdims=True))
    w_intra = jnp.exp(logw - m_t)
    w_inter = jnp.exp(inter - m_t)
    sc = _hdot(q, k, 2, 2) * w_intra
    num = w_inter * _hdot(q, c_mat, 2, 1) + _hdot(sc, v, 2, 1)
    qn = jnp.sum(q.astype(bf16).astype(f32) * n_vec.astype(bf16).astype(f32), axis=2, keepdims=True)
    den = w_inter * qn + jnp.sum(sc, axis=2, keepdims=True)
    h = num / jnp.maximum(jnp.abs(den), jnp.exp(-m_t))
    lane = lax.broadcasted_iota(jnp.int32, (nh, 1, ln), 2)
    b_last = jnp.sum(jnp.where(lane == ln - 1, bc_r, 0.0), axis=2, keepdims=True)
    logu = b_last - bc_r + igr
    m_new = jnp.maximum(b_last + m, jnp.max(logu, axis=2, keepdims=True))
    decay = jnp.exp(b_last + m - m_new)
    u_c = to_col(jnp.exp(logu - m_new))
    c_new = decay * c_mat + _hdot(u_c * k, v, 1, 1)
    n_new = decay * n_vec + jnp.sum(u_c.astype(bf16).astype(f32) * k.astype(bf16).astype(f32), axis=1, keepdims=True)
    return h, c_new, n_new, m_new


ML_VHALF = ML_V // 2
assert O_V % ML_VHALF == 0 and ML_HEADS % 2 == 0


def _ml_specs(nc, rev):
    cc = (lambda c: nc - 1 - c) if rev else (lambda c: c)
    q = pl.BlockSpec((ML_HEADS, CHUNK, ML_DK), lambda c: (0, cc(c), 0))
    k = pl.BlockSpec((ML_HEADS, CHUNK, ML_DK), lambda c: (1, cc(c), 0))
    v_lo = pl.BlockSpec((CHUNK, ML_VHALF), lambda c: (cc(c), O_V // ML_VHALF))
    v_hi = pl.BlockSpec((CHUNK, ML_VHALF), lambda c: (cc(c), O_V // ML_VHALF + 1))
    hv = pl.BlockSpec((ML_HEADS, CHUNK, ML_DV), lambda c: (0, cc(c), 0))
    gate = pl.BlockSpec((ML_HEADS, 1, 1, CHUNK), lambda c: (0, cc(c), 0, 0))
    cm = pl.BlockSpec((ML_HEADS, 1, ML_DK, ML_DV), lambda c: (0, cc(c), 0, 0))
    nv = pl.BlockSpec((ML_HEADS, 1, 1, ML_DK), lambda c: (0, cc(c), 0, 0))
    ms = pl.BlockSpec((ML_HEADS, 1, 1, 1), lambda c: (0, cc(c), 0, 0))
    return q, k, v_lo, v_hi, hv, gate, cm, nv, ms


_ML_STATE = [pltpu.VMEM((ML_HEADS, ML_DK, ML_DV), f32), pltpu.VMEM((ML_HEADS, 1, ML_DK), f32), pltpu.VMEM((ML_HEADS, 1, 1), f32)]


def _ml_zero_state(c_s, n_s, m_s):
    @pl.when(pl.program_id(0) == 0)
    def _():
        c_s[...] = jnp.zeros_like(c_s)
        n_s[...] = jnp.zeros_like(n_s)
        m_s[...] = jnp.zeros_like(m_s)


def _ml_heads_of(v_lo_ref, v_hi_ref):
    half = ML_HEADS // 2
    return jnp.stack([r[:, j * ML_DV:(j + 1) * ML_DV] for r in (v_lo_ref, v_hi_ref) for j in range(half)])


def mlstm_fwd(qk_act, z_main, ig, fg):
    t = qk_act.shape[1]
    nc = t // CHUNK

    def body(q_ref, k_ref, vl_ref, vh_ref, ig_ref, fg_ref, h_ref, c_out, n_out, m_out, c_s, n_s, m_s):
        _ml_zero_state(c_s, n_s, m_s)
        c0, n0, m0 = c_s[...], n_s[...], m_s[...]
        c_out[:, 0] = c0
        n_out[:, 0] = n0
        m_out[:, 0] = m0
        h, c2, n2, m2 = _mlstm_step(q_ref[...], k_ref[...], _ml_heads_of(vl_ref, vh_ref), ig_ref[:, 0], fg_ref[:, 0], c0, n0, m0)
        h_ref[...] = h
        c_s[...] = c2
        n_s[...] = n2
        m_s[...] = m2

    q, k, v_lo, v_hi, hv, gate, cm, nv, ms = _ml_specs(nc, False)
    return _call(body, "mlstm_fwd", (nc,), [q, k, v_lo, v_hi, gate, gate], [hv, cm, nv, ms],
                 [_sds((ML_HEADS, t, ML_DV), f32), _sds((ML_HEADS, nc, ML_DK, ML_DV), f32), _sds((ML_HEADS, nc, 1, ML_DK), f32),
                  _sds((ML_HEADS, nc, 1, 1), f32)], _ML_STATE)(qk_act, qk_act, z_main, z_main, ig, fg)


def mlstm_bwd(qk_act, z_main, ig, fg, c_all, n_all, m_all, dh, exchange=None):
    t = qk_act.shape[1]
    nc = t // CHUNK

    def body(q_ref, k_ref, vl_ref, vh_ref, ig_ref, fg_ref, c_ref, n_ref, m_ref, dh_ref, dq_ref, dk_ref, dv_ref, dig_ref, dfg_ref,
             dc_s, dn_s, dm_s):
        _ml_zero_state(dc_s, dn_s, dm_s)
        _, vjp = jax.vjp(_mlstm_step, q_ref[...], k_ref[...], _ml_heads_of(vl_ref, vh_ref), ig_ref[:, 0], fg_ref[:, 0],
                         c_ref[:, 0], n_ref[:, 0], m_ref[:, 0])
        dq, dk, dv, dig, dfg, dc, dn, dm = vjp((dh_ref[...], dc_s[...], dn_s[...], dm_s[...]))
        dq_ref[...] = dq
        dk_ref[...] = dk
        for j in range(ML_HEADS):
            dv_ref[:, j * ML_DV:(j + 1) * ML_DV] = dv[j].astype(bf16)
        dig_ref[:, 0] = dig
        dfg_ref[:, 0] = dfg
        dc_s[...] = dc
        dn_s[...] = dn
        dm_s[...] = dm

    q, k, v_lo, v_hi, hv, gate, cm, nv, ms = _ml_specs(nc, True)
    gshape = _sds((ML_HEADS, nc, 1, CHUNK), f32)
    return _call(body, "mlstm_bwd", (nc,), [q, k, v_lo, v_hi, gate, gate, cm, nv, ms, hv],
                 [q, q, pl.BlockSpec((CHUNK, ML_V), lambda c: (nc - 1 - c, 0)), gate, gate],
                 [_sds((ML_HEADS, t, ML_DK), f32), _sds((ML_HEADS, t, ML_DK), f32), _sds((t, ML_V), bf16), gshape, gshape],
                 _ML_STATE, exchange=exchange)(qk_act, qk_act, z_main, z_main, ig, fg, c_all, n_all, m_all, dh)


def _ml_out(h, zo, g):
    return _rms(h, g) * jax.nn.sigmoid(zo)


def mlstm_out(h, z_main, g_hnorm):
    t = h.shape[1]
    tm = _tile(t, (512, 256, 128, 64))
    zo = O_O // ML_DV

    def body(h_ref, z_ref, g_ref, y_ref):
        y_ref[...] = _ml_out(h_ref[0], z_ref[...], g_ref[0]).astype(bf16)

    return _call(body, "mlstm_out", (t // tm, ML_HEADS),
                 [pl.BlockSpec((1, tm, ML_DV), lambda i, hd: (hd, i, 0)), pl.BlockSpec((tm, ML_DV), lambda i, hd: (i, zo + hd)),
                  pl.BlockSpec((1, 1, ML_DV), lambda i, hd: (hd, 0, 0))],
                 pl.BlockSpec((tm, ML_DV), lambda i, hd: (i, hd)), _sds((t, ML_V), bf16))(h, z_main, g_hnorm)


def mlstm_out_bwd(h, z_main, g_hnorm, dy):
    t = h.shape[1]
    tm = _tile(t, (512, 256, 128, 64))
    zo = O_O // ML_DV

    def body(h_ref, z_ref, g_ref, dy_ref, dh_ref, dzo_ref, dg_ref):
        _, vjp = jax.vjp(_ml_out, h_ref[0], z_ref[...], g_ref[0])
        dh, dz, dg = vjp(dy_ref[...])
        dh_ref[0] = dh
        dzo_ref[...] = dz.astype(bf16)

        @pl.when(pl.program_id(1) == 0)
        def _():
            dg_ref[...] = jnp.zeros_like(dg_ref)

        dg_ref[0, 0:1, :] += dg

    head = pl.BlockSpec((1, tm, ML_DV), lambda hd, i: (hd, i, 0))
    blk = pl.BlockSpec((tm, ML_DV), lambda hd, i: (i, hd))
    return _call(body, "mlstm_out_bwd", (ML_HEADS, t // tm),
                 [head, pl.BlockSpec((tm, ML_DV), lambda hd, i: (i, zo + hd)), pl.BlockSpec((1, 1, ML_DV), lambda hd, i: (hd, 0, 0)), blk],
                 [head, blk, pl.BlockSpec((1, 8, ML_DV), lambda hd, i: (hd, 0, 0))],
                 [_sds((ML_HEADS, t, ML_DV), f32), _sds((t, ML_V), bf16), _sds((ML_HEADS, 8, ML_DV), f32)])(h, z_main, g_hnorm, dy)


def _merge(ga, gb, ya, yb):
    return jax.nn.sigmoid(ga) * ya + jax.nn.sigmoid(gb) * yb


def _merge_specs(t, d):
    tm = _tile(t, (512, 256, 128, 64))
    bw = _tile(d, (512, 256, 128))
    assert O_GA % bw == 0 and (O_GA + d) % bw == 0
    blk = pl.BlockSpec((tm, bw), lambda i, j: (i, j))
    ga = pl.BlockSpec((tm, bw), lambda i, j: (i, O_GA // bw + j))
    gb = pl.BlockSpec((tm, bw), lambda i, j: (i, (O_GA + d) // bw + j))
    return tm, bw, blk, ga, gb


def merge_fwd(z_main, ya, yb):
    t, d = ya.shape
    tm, bw, blk, ga, gb = _merge_specs(t, d)

    def body(ga_ref, gb_ref, ya_ref, yb_ref, o_ref):
        o_ref[...] = _merge(ga_ref[...], gb_ref[...], ya_ref[...], yb_ref[...]).astype(bf16)

    return _call(body, "merge_fwd", (t // tm, d // bw), [ga, gb, blk, blk], blk, _sds((t, d), bf16))(z_main, z_main, ya, yb)


def merge_bwd(z_main, ya, yb, dmerged):
    t, d = ya.shape
    tm, bw, blk, ga, gb = _merge_specs(t, d)

    def body(ga_ref, gb_ref, ya_ref, yb_ref, dm_ref, dga_ref, dgb_ref, dya_ref, dyb_ref):
        _, vjp = jax.vjp(_merge, ga_ref[...], gb_ref[...], ya_ref[...], yb_ref[...])
        dga, dgb, dya, dyb = vjp(dm_ref[...])
        dga_ref[...] = dga.astype(bf16)
        dgb_ref[...] = dgb.astype(bf16)
        dya_ref[...] = dya.astype(bf16)
        dyb_ref[...] = dyb.astype(bf16)

    return _call(body, "merge_bwd", (t // tm, d // bw), [ga, gb, blk, blk, blk], [blk] * 4, [_sds((t, d), bf16)] * 4)(
        z_main, z_main, ya, yb, dmerged)


def _cross(cq, ck, cv, gq, gk):
    outs = []
    for hd in range(CR_HEADS):
        sl = slice(hd * CR_HD, (hd + 1) * CR_HD)
        q = _rms(cq[:, sl], gq)
        k = _rms(ck[:, sl], gk)
        s = _bdot(q, k, 1, 1) * (CR_HD ** -0.5)
        p = jax.nn.softmax(s, axis=-1)
        outs.append(_bdot(p, cv[:, sl], 1, 0))
    return jnp.concatenate(outs, axis=1)


def cross_fwd(cq, ck, cv, gq, gk):
    t, w = cq.shape
    nm = ck.shape[0]
    tm = _tile(t, (512, 256, 128, 64))

    def body(q_ref, k_ref, v_ref, gq_ref, gk_ref, o_ref):
        o_ref[...] = _cross(q_ref[...], k_ref[...], v_ref[...], gq_ref[...], gk_ref[...]).astype(bf16)

    row = pl.BlockSpec((tm, w), lambda i: (i, 0))
    full = pl.BlockSpec((nm, w), lambda i: (0, 0))
    gain = pl.BlockSpec((1, CR_HD), lambda i: (0, 0))
    return _call(body, "cross_fwd", (t // tm,), [row, full, full, gain, gain], row, _sds((t, w), bf16))(cq, ck, cv, gq, gk)


def cross_bwd(cq, ck, cv, gq, gk, do):
    t, w = cq.shape
    nm = ck.shape[0]
    tm = _tile(t, (512, 256, 128, 64))

    def body(q_ref, k_ref, v_ref, gq_ref, gk_ref, do_ref, dq_ref, dk_ref, dv_ref, dgq_ref, dgk_ref):
        first = pl.program_id(0) == 0
        _, vjp = jax.vjp(_cross, q_ref[...], k_ref[...], v_ref[...], gq_ref[...], gk_ref[...])
        dq, dk, dv, dgq, dgk = vjp(do_ref[...])
        dq_ref[...] = dq.astype(bf16)

        @pl.when(first)
        def _():
            dk_ref[...] = jnp.zeros_like(dk_ref)
            dv_ref[...] = jnp.zeros_like(dv_ref)

        dk_ref[...] += dk
        dv_ref[...] += dv
        _acc_row(dgq_ref, dgq, first)
        _acc_row(dgk_ref, dgk, first)

    row = pl.BlockSpec((tm, w), lambda i: (i, 0))
    full = pl.BlockSpec((nm, w), lambda i: (0, 0))
    gain = pl.BlockSpec((1, CR_HD), lambda i: (0, 0))
    acc = pl.BlockSpec((8, CR_HD), lambda i: (0, 0))
    return _call(body, "cross_bwd", (t // tm,), [row, full, full, gain, gain, row], [row, full, full, acc, acc],
                 [_sds((t, w), bf16), _sds((nm, w), f32), _sds((nm, w), f32), _sds((8, CR_HD), f32), _sds((8, CR_HD), f32)])(
        cq, ck, cv, gq, gk, do)


def _place():
    x, y, c = lax.axis_index("x"), lax.axis_index("y"), lax.axis_index("c")
    peers = {}
    for r in range(1, N_DEV):
        px = 1 - x if r & 4 else x
        py = 1 - y if r & 2 else y
        pc = 1 - c if r & 1 else c
        peers[r] = ((px, py, pc), 4 * px + 2 * py + pc)
    return 4 * x + 2 * y + c, peers


N_REL = N_DEV - 1
RELATIONS = tuple(range(1, N_DEV))
SIBLING = 1
OTHER_CHIPS = (2, 4, 6)
PASSED_ON = (3, 5, 7)


def _exchange_ops(ins, outs, sems, scatter):
    n = len(ins)
    send_sems, recv_sems, local_sems = sems

    def tools():
        me, peers = _place()

        def copy(a, r, src, dst_idx, to):
            return pltpu.make_async_remote_copy(
                src_ref=src, dst_ref=outs[a].at[dst_idx], send_sem=send_sems.at[a * N_REL + r - 1],
                recv_sem=recv_sems.at[a * N_REL + r - 1], device_id=peers[to][0], device_id_type=MESH)

        def local(a):
            return pltpu.make_async_copy(ins[a].at[me] if scatter else ins[a], outs[a].at[me], local_sems.at[a])

        def arrival(a, r):
            return copy(a, r, ins[a].at[me] if scatter else ins[a], peers[r][1], r)

        return me, peers, copy, local, arrival

    if scatter:
        def sends():
            me, peers, copy, local, _ = tools()
            return [local(a) for a in range(n)], [copy(a, r, ins[a].at[peers[r][1]], me, r) for a in range(n) for r in RELATIONS]

        def start():
            loc, out = sends()
            for cp in loc + out:
                cp.start()

        middle = None
        waited_last = RELATIONS
    else:
        def sends():
            me, peers, copy, local, _ = tools()
            own = [copy(a, r, ins[a], me, r) for a in range(n) for r in (SIBLING,) + OTHER_CHIPS]
            return [local(a) for a in range(n)], own

        def passes():
            me, peers, copy, _, _ = tools()
            return [copy(a, r, outs[a].at[peers[r - 1][1]], peers[r - 1][1], SIBLING) for a in range(n) for r in PASSED_ON]

        def start():
            loc, out = sends()
            for cp in loc + out:
                cp.start()

        def middle():
            _, _, _, _, arrival = tools()
            fwd = passes()
            for a in range(n):
                for i, r in enumerate(PASSED_ON):
                    arrival(a, r - 1).wait_recv()
                    fwd[a * len(PASSED_ON) + i].start()

        waited_last = (SIBLING,) + PASSED_ON

    def wait():
        _, _, _, _, arrival = tools()
        for a in range(n):
            for r in waited_last:
                arrival(a, r).wait_recv()
        loc, out = sends()
        for cp in out + ([] if scatter else passes()):
            cp.wait_send()
        for cp in loc:
            cp.wait()

    return start, middle, wait


def _exchange_shapes(arrs, scatter):
    return [_sds(a.shape if scatter else (N_DEV,) + a.shape, a.dtype) for a in arrs]


def _exchange_sems(n):
    return [pltpu.SemaphoreType.DMA((n * N_REL,)), pltpu.SemaphoreType.DMA((n * N_REL,)), pltpu.SemaphoreType.DMA((n,))]


def _exchange(arrs, name, scatter):
    n = len(arrs)

    def body(*refs):
        start, middle, wait = _exchange_ops(refs[:n], refs[n:2 * n], refs[2 * n:], scatter)
        start()
        if middle is not None:
            middle()
        wait()

    any_spec = pl.BlockSpec(memory_space=pl.ANY)
    return pl.pallas_call(body, name=name, in_specs=[any_spec] * n, out_specs=[any_spec] * n,
                          out_shape=_exchange_shapes(arrs, scatter), scratch_shapes=_exchange_sems(n))(*arrs)


def cast_bf16(w, name):
    _, r, c = w.shape
    tr = _tile(r, (256, 128, 64, 32, 16))

    def body(w_ref, o_ref):
        o_ref[...] = w_ref[0].astype(bf16)

    return _call(body, name, (r // tr,), [pl.BlockSpec((1, tr, c), lambda i: (0, i, 0))], pl.BlockSpec((tr, c), lambda i: (i, 0)),
                 _sds((r, c), bf16))(w)


def _adamw(w, g, m, v):
    m = ADAM_B1 * m + (1.0 - ADAM_B1) * g
    v = ADAM_B2 * v + (1.0 - ADAM_B2) * jnp.square(g)
    m_hat = m / (1.0 - ADAM_B1 ** ADAM_STEP)
    v_hat = v / (1.0 - ADAM_B2 ** ADAM_STEP)
    delta = -ADAM_LR * (m_hat / (jnp.sqrt(v_hat) + ADAM_EPS) + ADAM_WD * w)
    return delta, m, v


def adam_sum(parts, w, m, v, name):
    _, r, c = parts.shape
    budget = 4 * 1024 * 1024
    tr = r
    for cand in (1024, 512, 256, 128, 64, 32, 16):
        if r % cand == 0 and N_DEV * cand * c * 4 <= budget:
            tr = cand
            break

    def body(p_ref, w_ref, m_ref, v_ref, g_ref, d_ref, m2_ref, v2_ref):
        g = p_ref[0].astype(f32)
        for k in range(1, N_DEV):
            g = g + p_ref[k].astype(f32)
        d, m2, v2 = _adamw(w_ref[0], g, m_ref[0], v_ref[0])
        g_ref[...] = g
        d_ref[...] = d
        m2_ref[...] = m2
        v2_ref[...] = v2

    blk = pl.BlockSpec((1, tr, c), lambda i: (0, i, 0))
    out = pl.BlockSpec((tr, c), lambda i: (i, 0))
    return _call(body, name, (r // tr,), [pl.BlockSpec((N_DEV, tr, c), lambda i: (0, i, 0)), blk, blk, blk], [out] * 4,
                 [_sds((r, c), f32)] * 4)(parts, w, m, v)


def sum_parts(parts, name):
    _, r, c = parts.shape

    def body(p_ref, o_ref):
        g = p_ref[0]
        for k in range(1, N_DEV):
            g = g + p_ref[k]
        o_ref[...] = g

    return pl.pallas_call(body, name=name, out_shape=_sds((r, c), f32))(parts)


def adam_flat(w, g, m, v, name):
    def body(w_ref, g_ref, m_ref, v_ref, d_ref, m2_ref, v2_ref):
        d, m2, v2 = _adamw(w_ref[...], g_ref[...], m_ref[...], v_ref[...])
        d_ref[...] = d
        m2_ref[...] = m2
        v2_ref[...] = v2

    return pl.pallas_call(body, name=name, out_shape=[_sds(w.shape, f32)] * 3)(w, g, m, v)


def _pack(vecs, multiple):
    flat = jnp.concatenate([v.reshape(-1) for v in vecs])
    n = flat.shape[0]
    total = -(-n // multiple) * multiple
    return jnp.pad(flat, (0, total - n))


def _unpack(flat, shapes):
    out, pos = [], 0
    for s in shapes:
        n = 1
        for d in s:
            n *= d
        out.append(flat[pos:pos + n].reshape(s))
        pos += n
    return out


def _pad_lanes(v, width=TAIL):
    return jnp.pad(v, ((0, 0), (0, width - v.shape[1])))


def kernel(x, mem, positions, g_mix, w_in, g_qa, w_qb, g_kva, w_kvb, g_qn_nope, g_qn_pe, g_kn_nope, g_kn_pe, conv_qk, b_if, g_hnorm, p_a, p_b, w_out, g_cross, g_mem, wq_c, wk_c, wv_c, g_cq, g_ck, wo_c, g_ffn, w_up, conv_ffn, b_conv_ffn, w_down, loss_target, m_g_mix, m_w_in, m_g_qa, m_w_qb, m_g_kva, m_w_kvb, m_g_qn_nope, m_g_qn_pe, m_g_kn_nope, m_g_kn_pe, m_conv_qk, m_b_if, m_g_hnorm, m_p_a, m_p_b, m_w_out, m_g_cross, m_g_mem, m_wq_c, m_wk_c, m_wv_c, m_g_cq, m_g_ck, m_wo_c, m_g_ffn, m_w_up, m_conv_ffn, m_b_conv_ffn, m_w_down, v_g_mix, v_w_in, v_g_qa, v_w_qb, v_g_kva, v_w_kvb, v_g_qn_nope, v_g_qn_pe, v_g_kn_nope, v_g_kn_pe, v_conv_qk, v_b_if, v_g_hnorm, v_p_a, v_p_b, v_w_out, v_g_cross, v_g_mem, v_wq_c, v_wk_c, v_wv_c, v_g_cq, v_g_ck, v_wo_c, v_g_ffn, v_w_up, v_conv_ffn, v_b_conv_ffn, v_w_down):
    args = dict(locals())
    names = ['g_mix', 'w_in', 'g_qa', 'w_qb', 'g_kva', 'w_kvb', 'g_qn_nope', 'g_qn_pe', 'g_kn_nope', 'g_kn_pe', 'conv_qk', 'b_if',
             'g_hnorm', 'p_a', 'p_b', 'w_out', 'g_cross', 'g_mem', 'wq_c', 'wk_c', 'wv_c', 'g_cq', 'g_ck', 'wo_c', 'g_ffn', 'w_up',
             'conv_ffn', 'b_conv_ffn', 'w_down']
    big = ['w_in', 'w_qb', 'w_kvb', 'p_a', 'p_b', 'w_out', 'wq_c', 'wk_c', 'wv_c', 'wo_c', 'w_up', 'w_down']
    sharded_small = ['conv_qk', 'g_hnorm', 'conv_ffn']
    replicated = [n for n in names if n not in big and n not in sharded_small]

    t, d = x.shape[1], x.shape[2]
    x2d, tgt = x[0], loss_target[0]
    mem2d = mem[0]
    me = 4 * lax.axis_index("x") + 2 * lax.axis_index("y") + lax.axis_index("c")
    nc = t // CHUNK
    f2 = b_conv_ffn.shape[1]
    wmain = O_GA + 2 * d

    first = ['w_in', 'w_qb', 'w_kvb']
    behind_in = ['p_a', 'p_b', 'w_out', 'wq_c', 'wk_c', 'wv_c', 'wo_c']
    shards = {n: cast_bf16(args[n], "cast_" + n) for n in big}
    small_local = _pack([args[n] for n in sharded_small], 128).reshape(1, -1)
    gathered = _exchange([shards[n] for n in first] + [small_local], "comm_gather_first", scatter=False)
    gw = dict(zip(first, gathered[:-1]))
    small_all = gathered[-1]
    full_small, pos = [], 0
    for n in sharded_small:
        _, rows, cols = args[n].shape
        piece = small_all[:, 0, pos:pos + rows * cols].reshape(N_DEV, rows, cols)
        full_small.append(piece.transpose(1, 0, 2).reshape(rows, N_DEV * cols))
        pos += rows * cols
    conv_qk_f, g_hnorm_f, conv_ffn_f = full_small

    shard_w = w_in.shape[2]
    c_kpe, c_q, c_i, c_o = O_Q, O_Q + ROPE, O_Q + ROPE + 2 * ML_QK + ML_V, O_Q + ROPE + 2 * ML_QK + ML_V + 2 * ML_HEADS
    segments = [(0, c_kpe, 'main', 0), (c_kpe, c_q, 'tail', 0), (c_q, c_i, 'main', O_Q), (c_i, c_o, 'tail', T_I),
                (c_o, N_DEV * shard_w, 'main', O_O)]

    def shard_cuts(lo, hi):
        return [(j, max(lo, j * shard_w) - j * shard_w, min(hi, (j + 1) * shard_w) - j * shard_w)
                for j in range(lo // shard_w, (hi - 1) // shard_w + 1)]

    def gathered_cols(target):
        return [gw['w_in'][j][:, a:b] for lo, hi, tg, _ in segments if tg == target for j, a, b in shard_cuts(lo, hi)]

    w_main = jnp.concatenate(gathered_cols('main'), axis=1)[None]
    w_tail = jnp.concatenate(gathered_cols('tail') + [jnp.zeros((d, TAIL - ROPE - 2 * ML_HEADS), bf16)], axis=1)[None]
    assert w_main.shape[2] == wmain

    inv_freq = ROPE_BASE ** (-jnp.arange(0, ROPE, 2, dtype=f32) / ROPE)
    inv_tile = _pad_lanes(jnp.concatenate([inv_freq, inv_freq])[None])
    cos, sin = rope_tables(positions.reshape(t, 1), inv_tile)
    gqp, gkp = _pad_lanes(g_qn_pe), _pad_lanes(g_kn_pe)
    b_tile = jnp.pad(b_if, ((0, 0), (T_I, TAIL - T_I - 2 * ML_HEADS)))

    u0 = rms_fwd(x2d, g_mix, "rms_mix")
    z_main, got = mm_nn(u0, w_main, f32, "mm_in_main", exchange=([shards[n] for n in behind_in], False))
    gw.update(zip(behind_in, got))
    qb = gw['w_qb'].transpose(1, 0, 2).reshape(Q_LORA, MLA_HEADS, NOPE + ROPE)
    w_qb_p = jnp.concatenate([qb, jnp.zeros((Q_LORA, MLA_HEADS, HEAD_PAD - NOPE - ROPE), bf16)], axis=2).reshape(1, Q_LORA, -1)
    w_kvb3 = gw['w_kvb']
    p_a3, p_b3, w_out3 = (gw[n].reshape(1, -1, d) for n in ('p_a', 'p_b', 'w_out'))
    wq_c3, wk_c3, wv_c3 = (gw[n].reshape(1, d, -1) for n in ('wq_c', 'wk_c', 'wv_c'))
    wo_c3 = gw['wo_c']
    z_tail = mm_nn(u0, w_tail, f32, "mm_in_tail")
    qa_n, kv_n = lat_norm(z_main, g_qa, g_kva)
    q_raw = mm_nn(qa_n, w_qb_p, f32, "mm_qb")
    kv_raw = mm_nn(kv_n, w_kvb3, f32, "mm_kvb")
    qh, kh, vh = mla_prep(q_raw, kv_raw, z_tail, cos, sin, g_qn_nope, gqp, g_kn_nope, gkp)
    (o_a, o_ab, lse), (w_up3,) = mla_fwd(qh, kh, vh, exchange=([shards['w_up']], False))

    qk_act = qk_conv(z_main, conv_qk_f)
    gates = gate_act(z_tail, b_tile)

    def to_rows(cols):
        return cols.T.reshape(ML_HEADS, nc, 1, CHUNK)

    ig, fg = to_rows(gates[:, T_I:T_F]), to_rows(gates[:, T_F:T_F + ML_HEADS])
    h_ml, c_all, n_all, m_all = mlstm_fwd(qk_act, z_main, ig, fg)
    g_hn3 = g_hnorm_f.reshape(ML_HEADS, 1, ML_DV)
    y_b = mlstm_out(h_ml, z_main, g_hn3)

    ya = mm_nn(o_ab, p_a3, f32, "mm_pa")
    yb = mm_nn(y_b, p_b3, f32, "mm_pb")
    merged = merge_fwd(z_main, ya, yb)
    def add_and_norm(acc, tiles, rows):
        xs = tiles[0] + acc
        return ([xs, _rms(xs, rows[0])],)

    x1, uc = mm_nn_fused(merged, w_out3, [x2d], [g_cross], [f32, bf16], add_and_norm, "mm_out_resid", full_rows=True)
    mem_n = rms_fwd(mem2d, g_mem, "rms_mem")
    cq = mm_nn(uc, wq_c3, f32, "mm_cq")
    ck = mm_nn(mem_n, wk_c3, f32, "mm_ck")
    cv = mm_nn(mem_n, wv_c3, f32, "mm_cv")
    o_c = cross_fwd(cq, ck, cv, g_cq, g_ck)
    wo_c_rows = wo_c3.transpose(1, 0, 2).reshape(1, wo_c3.shape[1], d)
    x2, u3 = mm_nn_fused(o_c, wo_c_rows, [x1], [g_ffn], [f32, bf16], add_and_norm, "mm_oc_resid", full_rows=True)
    hup, (w_down_g,) = mm_nn(u3, w_up3, f32, "mm_up", exchange=([shards['w_down']], False))
    w_down3 = w_down_g.reshape(1, -1, d)
    gl = glu_fwd(hup, conv_ffn_f, b_conv_ffn)
    def loss_grad(acc, tiles, rows):
        err = tiles[0] + acc - tiles[1]
        dx = err / d
        part = 0.5 * jnp.sum(jnp.sum(err * err, axis=1, keepdims=True), axis=0, keepdims=True) / d
        return [dx, dx], part

    dx3, dx3_b, loss_acc = mm_nn_fused(gl, w_down3, [x2, tgt], [], [f32, bf16], loss_grad, "mm_down_loss", with_sum=True)

    grads, parts = {}, {}
    grads['w_down'] = mm_tn(gl, dx3_b, 1, "mm_d_wdown").reshape(N_DEV, -1, d)
    dgl = mm_nt(dx3_b, w_down3, f32, "mm_d_gl")
    (dh1, dh2, dcw1, dcw2, db1, db2), (parts['w_down'],) = glu_bwd(hup, conv_ffn_f, b_conv_ffn, dgl,
                                                                    exchange=([grads['w_down']], True))
    dconv_ffn, db_ffn = (jnp.concatenate(pair, axis=1) for pair in ((dcw1, dcw2), (db1, db2)))
    grads['w_up'] = mm_tn_cols(u3, [dh1, dh2], N_DEV, "mm_d_wup")
    du3 = mm_nt_cols([dh1, dh2], w_up3, f32, "mm_d_u3")
    dx2, dx2_b, dg_ffn = rms_bwd(x2, g_ffn, [du3], dx3, "rms_bwd_ffn", want_b16=True)
    grads['wo_c'] = mm_tn(o_c, dx2_b, N_DEV, "mm_d_woc")
    do_c = mm_nt(dx2_b, wo_c3, f32, "mm_d_oc")
    dcq, dck, dcv, dg_cq, dg_ck = cross_bwd(cq, ck, cv, g_cq, g_ck, do_c)
    grads['wq_c'] = mm_tn(uc, dcq, 1, "mm_d_wqc").reshape(N_DEV, -1, dcq.shape[1])
    grads['wk_c'] = mm_tn(mem_n, dck, 1, "mm_d_wkc").reshape(N_DEV, -1, dck.shape[1])
    grads['wv_c'] = mm_tn(mem_n, dcv, 1, "mm_d_wvc").reshape(N_DEV, -1, dcv.shape[1])
    duc = mm_nt(dcq, wq_c3, f32, "mm_d_uc")
    dmem_k = mm_nt(dck, wk_c3, f32, "mm_d_memk")
    dmem_v = mm_nt(dcv, wv_c3, f32, "mm_d_memv")
    dg_mem, = rms_bwd(mem2d, g_mem, [dmem_k, dmem_v], None, "rms_bwd_mem", want_dx=False)
    dx1, dx1_b, dg_cross = rms_bwd(x1, g_cross, [duc], dx2, "rms_bwd_cross", want_b16=True)
    grads['w_out'] = mm_tn(merged, dx1_b, 1, "mm_d_wout").reshape(N_DEV, -1, d)
    dmerged = mm_nt(dx1_b, w_out3, f32, "mm_d_merged")
    dga, dgb, dya, dyb = merge_bwd(z_main, ya, yb, dmerged)
    grads['p_a'] = mm_tn(o_ab, dya, 1, "mm_d_pa").reshape(N_DEV, -1, d)
    grads['p_b'] = mm_tn(y_b, dyb, 1, "mm_d_pb").reshape(N_DEV, -1, d)
    do_a = mm_nt(dya, p_a3, f32, "mm_d_oa")
    dy_b = mm_nt(dyb, p_b3, f32, "mm_d_yb")

    dh_ml, dzo, dg_hn = mlstm_out_bwd(h_ml, z_main, g_hn3, dy_b)
    mixers = ['p_a', 'p_b', 'w_out']
    (dq_act, dk_act, dzv, dig, dfg), got = mlstm_bwd(qk_act, z_main, ig, fg, c_all, n_all, m_all, dh_ml,
                                                     exchange=([grads[n] for n in mixers], True))
    parts.update(zip(mixers, got))
    dzqk, dconv_qk = qk_conv_bwd(z_main, conv_qk_f, dq_act, dk_act)

    delta = mla_delta(o_a, do_a)
    (dqh, dkh, dvh), (parts['w_up'],) = mla_bwd(qh, kh, vh, do_a, lse, delta.reshape(MLA_HEADS, 1, t),
                                                exchange=([grads['w_up']], True))
    dq_raw, dkv_raw, dzt_pe, dg_qn, dg_qp, dg_kn, dg_kp = mla_prep_bwd(
        q_raw, kv_raw, z_tail, cos, sin, g_qn_nope, gqp, g_kn_nope, gkp, dqh, dkh, dvh)
    d_wqb_p = mm_tn(qa_n, dq_raw, 1, "mm_d_wqb")[0].reshape(Q_LORA, MLA_HEADS, HEAD_PAD)[:, :, :NOPE + ROPE]
    grads['w_qb'] = d_wqb_p.reshape(Q_LORA, N_DEV, -1).transpose(1, 0, 2)
    grads['w_kvb'] = mm_tn(kv_n, dkv_raw, N_DEV, "mm_d_wkvb")
    dqa = mm_nt(dq_raw, w_qb_p, f32, "mm_d_qa")
    dkvn = mm_nt(dkv_raw, w_kvb3, f32, "mm_d_kvn")
    dz_lat, dg_qa, dg_kva = lat_norm_bwd(z_main, g_qa, g_kva, dqa, dkvn)

    def to_cols(rows):
        return rows.reshape(ML_HEADS, t).T

    dgate = jnp.pad(jnp.concatenate([to_cols(dig), to_cols(dfg)], axis=1), ((0, 0), (T_I, TAIL - T_I - 2 * ML_HEADS)))
    dz_tail, db_if = tail_bwd(z_tail, b_tile, dzt_pe, dgate)
    dz_main = [dz_lat, dzqk, dzv, dzo, dga, dgb]
    small_mats = ['wq_c', 'wk_c', 'wv_c', 'wo_c', 'w_qb', 'w_kvb']
    d_wmain3, got = mm_tn_cols(u0, dz_main, 1, "mm_d_wmain", exchange=([grads[n] for n in small_mats], True))
    parts.update(zip(small_mats, got))
    d_wmain = d_wmain3[0]
    d_wtail = mm_tn(u0, dz_tail, 1, "mm_d_wtail")[0]
    d_target = {'main': d_wmain, 'tail': d_wtail}
    blocks = []
    for j in range(N_DEV):
        lo_j, hi_j = j * shard_w, (j + 1) * shard_w
        cols = [d_target[tg][:, off + max(lo, lo_j) - lo:off + min(hi, hi_j) - lo]
                for lo, hi, tg, off in segments if lo < hi_j and hi > lo_j]
        blocks.append(jnp.concatenate(cols, axis=1))
    grads['w_in'] = jnp.stack(blocks)
    du0_a, (parts['w_in'],) = mm_nt_cols(dz_main, w_main, f32, "mm_d_u0_main", exchange=([grads['w_in']], True))
    du0_b = mm_nt(dz_tail, w_tail, f32, "mm_d_u0_tail")
    grad_x, dg_mix = rms_bwd(x2d, g_mix, [du0_a, du0_b], dx1, "rms_bwd_mix")

    out_g, out_d, out_m, out_v = {}, {}, {}, {}
    for n in big:
        res = adam_sum(parts[n], args[n], args['m_' + n], args['v_' + n], "adam_" + n)
        out_g[n], out_d[n], out_m[n], out_v[n] = (a.reshape(args[n].shape) for a in res)

    small_full = {
        'g_mix': dg_mix[0], 'g_qa': dg_qa[0], 'g_kva': dg_kva[0], 'g_qn_nope': dg_qn[0], 'g_qn_pe': dg_qp[0, :ROPE],
        'g_kn_nope': dg_kn[0], 'g_kn_pe': dg_kp[0, :ROPE], 'conv_qk': dconv_qk, 'b_if': db_if[0, T_I:T_I + 2 * ML_HEADS],
        'g_hnorm': dg_hn[:, 0, :], 'g_cross': dg_cross[0], 'g_mem': dg_mem[0], 'g_cq': dg_cq[0], 'g_ck': dg_ck[0],
        'g_ffn': dg_ffn[0], 'conv_ffn': dconv_ffn, 'b_conv_ffn': db_ffn[0], 'loss': loss_acc[0, :1]}
    order = list(small_full)
    packed = _pack([small_full[n] for n in order], 8 * 128).reshape(1, -1)
    gathered_small, = _exchange([packed], "comm_gather_small", scatter=False)
    summed = sum_parts(gathered_small.reshape(N_DEV, -1, 128), "sum_small").reshape(-1)
    full_g = dict(zip(order, _unpack(summed, [small_full[n].shape for n in order])))
    loss = full_g['loss'][0]

    local_g = {}
    for n in replicated:
        local_g[n] = full_g[n].reshape(args[n].shape)
    for n in sharded_small:
        shp = args[n].shape
        full = full_g[n].reshape((1,) + full_g[n].shape)
        local_g[n] = lax.dynamic_slice_in_dim(full, me * shp[-1], shp[-1], axis=2)
    small = replicated + sharded_small
    dl_f, m_f, v_f = adam_flat(*[_pack([src[n] if pre == '' else args[pre + n] for n in small], 8 * 128).reshape(-1, 128)
                                 for pre, src in (('', args), ('', local_g), ('m_', None), ('v_', None))], "adam_small")
    shapes = [args[n].shape for n in small]
    for dst, flat in ((out_d, dl_f), (out_m, m_f), (out_v, v_f)):
        dst.update(zip(small, _unpack(flat.reshape(-1), shapes)))
    out_g.update(local_g)

    return (loss, grad_x[None], *[out_g[n] for n in names], *[out_d[n] for n in names],
            *[out_m[n] for n in names], *[out_v[n] for n in names])
```
